```python
import math
import jax, jax.numpy as jnp
from jax import lax
import numpy as np

D_MODEL = 2048
BATCH = 8
SEQ = 4096
DEPTH = 4

HEAD_DIM = 64
N_HEADS_TOTAL = D_MODEL // HEAD_DIM
N_HEADS_A = N_HEADS_TOTAL // 4
N_KV_A = N_HEADS_A // 4
N_HEADS_B = N_HEADS_TOTAL // 4
N_HEADS_C = N_HEADS_TOTAL // 2
BLOCK = 128
WINDOW_A = 128
DILATED_PAIRS = ((128, 1), (512, 4), (2048, 16))
N_BUCKETS = 32
T5_MAX_DIST = 2048
D_FF = 256 * (-(-(8 * D_MODEL // 3) // 256))
CONV_WIDTH = 3
EPS = 1e-6
NEG_INF = -1e30

A_Q = N_HEADS_A * HEAD_DIM
A_KV = N_KV_A * HEAD_DIM
B_W = N_HEADS_B * HEAD_DIM
C_W = N_HEADS_C * HEAD_DIM
IN_WIDTH = A_Q + 2 * A_KV + 3 * B_W + 3 * C_W
MIX_WIDTH = A_Q + B_W + C_W

kernel_name = "hymba_style_swa_stickbreak_dilated_convffn"


def rmsnorm(x, g):
    xf = x.astype(jnp.float32)
    y = xf * lax.rsqrt(jnp.mean(xf * xf, axis=-1, keepdims=True) + EPS)
    return (y * g.astype(jnp.float32)).astype(x.dtype)


def t5_bucket(dist):
    max_exact = N_BUCKETS // 2
    d = jnp.maximum(dist, 0)
    large = max_exact + (jnp.log(jnp.maximum(d, 1).astype(jnp.float32) / max_exact)
                         / math.log(T5_MAX_DIST / max_exact) * (N_BUCKETS - max_exact)).astype(jnp.int32)
    large = jnp.minimum(large, N_BUCKETS - 1)
    return jnp.where(d < max_exact, d, large)


def block_rel_bias(table, dil):
    rel = jnp.arange(BLOCK)[:, None] + BLOCK - jnp.arange(2 * BLOCK)[None, :]
    buckets = t5_bucket(rel * dil)
    return jnp.transpose(table[buckets], (2, 0, 1)).astype(jnp.float32)


def banded_attention(q, k, v, bias, max_dist, sinks=None):
    n, length, hq, hd = q.shape
    hk = k.shape[2]
    grp = hq // hk
    lp = -(-length // BLOCK) * BLOCK
    if lp != length:
        padw = ((0, 0), (0, lp - length), (0, 0), (0, 0))
        q, k, v = jnp.pad(q, padw), jnp.pad(k, padw), jnp.pad(v, padw)
    nb = lp // BLOCK
    qb = q.reshape(n, nb, BLOCK, hk, grp, hd).astype(jnp.float32)
    kb = k.reshape(n, nb, BLOCK, hk, hd).astype(jnp.float32)
    vb = v.reshape(n, nb, BLOCK, hk, hd).astype(jnp.float32)
    prev = ((0, 0), (1, 0), (0, 0), (0, 0), (0, 0))
    kk = jnp.concatenate([jnp.pad(kb[:, :-1], prev), kb], axis=2)
    vv = jnp.concatenate([jnp.pad(vb[:, :-1], prev), vb], axis=2)
    logits = (jnp.einsum('nbqhgd,nbkhd->nbhgqk', qb, kk) * (hd ** -0.5)
              + bias.reshape(hk, grp, BLOCK, 2 * BLOCK))
    rel = jnp.arange(BLOCK)[:, None] + BLOCK - jnp.arange(2 * BLOCK)[None, :]
    key_abs = (jnp.arange(nb)[:, None] - 1) * BLOCK + jnp.arange(2 * BLOCK)[None, :]
    mask = ((rel >= 0) & (rel <= max_dist))[None] & (key_abs >= 0)[:, None, :]
    logits = jnp.where(mask[None, :, None, None], logits, NEG_INF)
    m = jnp.max(logits, axis=-1, keepdims=True)
    if sinks is not None:
        s = sinks.astype(jnp.float32).reshape(hk, grp, 1, 1)
        m = jnp.maximum(m, s)
    p = jnp.exp(logits - m)
    denom = jnp.sum(p, axis=-1, keepdims=True)
    if sinks is not None:
        denom = denom + jnp.exp(s - m)
    out = jnp.einsum('nbhgqk,nbkhd->nbhgqd', p / denom, vv)
    out = jnp.transpose(out, (0, 1, 4, 2, 3, 5)).reshape(n, lp, hq, hd)[:, :length]
    lse = jnp.transpose((m + jnp.log(denom))[..., 0], (0, 1, 4, 2, 3)).reshape(n, lp, hq)[:, :length]
    return out, lse


def stick_breaking_attention(q, k, v):
    b_, s_, h, hd = q.shape
    nb = s_ // BLOCK
    qb = jnp.transpose(q.reshape(b_, nb, BLOCK, h, hd), (1, 0, 2, 3, 4))
    kf = k.astype(jnp.float32)
    vf = v.astype(jnp.float32)
    s_pos = jnp.arange(s_)

    def one_block(args):
        qblk, blk = args
        z = jnp.einsum('bqhd,bkhd->bhqk', qblk.astype(jnp.float32), kf) * (hd ** -0.5)
        t_pos = blk * BLOCK + jnp.arange(BLOCK)
        causal = s_pos[None, :] < t_pos[:, None]
        log_rem = jnp.where(causal, jax.nn.log_sigmoid(-z), 0.0)
        suffix = lax.cumsum(log_rem, axis=3, reverse=True) - log_rem
        a = jnp.where(causal, jnp.exp(jax.nn.log_sigmoid(z) + suffix), 0.0)
        return jnp.einsum('bhqk,bkhd->bqhd', a, vf)

    out = lax.map(one_block, (qb, jnp.arange(nb)))
    return jnp.transpose(out, (1, 0, 2, 3, 4)).reshape(b_, s_, h, hd)


def dilated_attention(q, k, v, table_c):
    b_, s_, h, hd = q.shape
    outs, lses = [], []
    for window, dil in DILATED_PAIRS:
        def to_sub(t):
            return jnp.transpose(t.reshape(b_, s_ // dil, dil, h, hd), (0, 2, 1, 3, 4)).reshape(b_ * dil, s_ // dil, h, hd)
        o, lse = banded_attention(to_sub(q), to_sub(k), to_sub(v), block_rel_bias(table_c, dil), window // dil)
        outs.append(jnp.transpose(o.reshape(b_, dil, s_ // dil, h, hd), (0, 2, 1, 3, 4)).reshape(b_, s_, h, hd))
        lses.append(jnp.transpose(lse.reshape(b_, dil, s_ // dil, h), (0, 2, 1, 3)).reshape(b_, s_, h))
    w = jax.nn.softmax(jnp.stack(lses, axis=0), axis=0)
    return jnp.sum(w[..., None] * jnp.stack(outs, axis=0), axis=0)


def causal_dwconv(u, w, b):
    up = jnp.pad(u, ((0, 0), (CONV_WIDTH - 1, 0), (0, 0)))
    s_ = u.shape[1]
    acc = b
    for i in range(CONV_WIDTH):
        acc = acc + w[i] * up[:, i:i + s_]
    return acc


def _fwd_setup_inputs(seed: int = 0) -> dict:
    key = jax.random.key(seed)
    ks = jax.random.split(key, 20)
    f32 = jnp.float32

    def nrm(k, shape, scale):
        return jax.random.normal(k, shape, f32) * scale

    return {
        "x": nrm(ks[0], (BATCH, SEQ, D_MODEL), 1.0),
        "attn_norm": 1.0 + nrm(ks[1], (DEPTH, D_MODEL), 0.02),
        "w_in": nrm(ks[2], (DEPTH, D_MODEL, IN_WIDTH), D_MODEL ** -0.5),
        "a_q_gain": 1.0 + nrm(ks[3], (DEPTH, HEAD_DIM), 0.02),
        "a_k_gain": 1.0 + nrm(ks[4], (DEPTH, HEAD_DIM), 0.02),
        "a_sinks": nrm(ks[5], (DEPTH, N_HEADS_A), 0.5),
        "c_q_gain": 1.0 + nrm(ks[6], (DEPTH, HEAD_DIM), 0.02),
        "c_k_gain": 1.0 + nrm(ks[7], (DEPTH, HEAD_DIM), 0.02),
        "rel_bias_table": nrm(ks[8], (N_BUCKETS, N_HEADS_A + N_HEADS_C), 0.5),
        "mix_out_gain": 1.0 + nrm(ks[9], (DEPTH, MIX_WIDTH), 0.02),
        "w_out": nrm(ks[10], (DEPTH, MIX_WIDTH, D_MODEL), MIX_WIDTH ** -0.5),
        "ffn_norm": 1.0 + nrm(ks[11], (DEPTH, D_MODEL), 0.02),
        "w_up": nrm(ks[12], (DEPTH, D_MODEL, 2 * D_FF), D_MODEL ** -0.5),
        "conv_w": nrm(ks[13], (DEPTH, CONV_WIDTH, 2 * D_FF), CONV_WIDTH ** -0.5),
        "conv_b": nrm(ks[14], (DEPTH, 2 * D_FF), 0.02),
        "w_down": nrm(ks[15], (DEPTH, D_FF, D_MODEL), D_FF ** -0.5),
    }


def _fwd_reference(x, attn_norm, w_in, a_q_gain, a_k_gain, a_sinks, c_q_gain, c_k_gain, rel_bias_table,
              mix_out_gain, w_out, ffn_norm, w_up, conv_w, conv_b, w_down):
    b_, s_, _ = x.shape
    sizes = [A_Q, A_KV, A_KV, B_W, B_W, B_W, C_W, C_W, C_W]
    offsets = [int(o) for o in np.cumsum(sizes)[:-1]]
    table_a = rel_bias_table[:, :N_HEADS_A]
    table_c = rel_bias_table[:, N_HEADS_A:]
    bias_a = block_rel_bias(table_a, 1)
    for l in range(DEPTH):
        h = rmsnorm(x, attn_norm[l])
        proj = jnp.einsum('bsd,de->bse', h, w_in[l])
        aq, ak, av, bq, bk, bv, cq, ck, cv = jnp.split(proj, offsets, axis=-1)
        heads = lambda t, n: t.reshape(b_, s_, n, HEAD_DIM)
        out_a, _ = banded_attention(rmsnorm(heads(aq, N_HEADS_A), a_q_gain[l]),
                                    rmsnorm(heads(ak, N_KV_A), a_k_gain[l]),
                                    heads(av, N_KV_A), bias_a, WINDOW_A - 1, sinks=a_sinks[l])
        out_b = stick_breaking_attention(heads(bq, N_HEADS_B), heads(bk, N_HEADS_B), heads(bv, N_HEADS_B))
        out_c = dilated_attention(rmsnorm(heads(cq, N_HEADS_C), c_q_gain[l]),
                                  rmsnorm(heads(ck, N_HEADS_C), c_k_gain[l]),
                                  heads(cv, N_HEADS_C), table_c)
        g = mix_out_gain[l]
        ya = rmsnorm(out_a.reshape(b_, s_, A_Q), g[:A_Q])
        yb = rmsnorm(out_b.reshape(b_, s_, B_W), g[A_Q:A_Q + B_W])
        yc = rmsnorm(out_c.reshape(b_, s_, C_W), g[A_Q + B_W:])
        mix = jnp.concatenate([ya, yb, yc], axis=-1).astype(x.dtype)
        x = x + jnp.einsum('bse,ed->bsd', mix, w_out[l])
        h = rmsnorm(x, ffn_norm[l])
        u = causal_dwconv(jnp.einsum('bsd,df->bsf', h, w_up[l]), conv_w[l], conv_b[l])
        gate, up = jnp.split(u, [D_FF], axis=-1)
        x = x + jnp.einsum('bsf,fd->bsd', jax.nn.silu(gate) * up, w_down[l])
    return x


import jax as _jax
import jax.numpy as _jnp

TWIN_FORMAT = 'train_step'
FWD_PARAMS = ['x', 'attn_norm', 'w_in', 'a_q_gain', 'a_k_gain', 'a_sinks', 'c_q_gain', 'c_k_gain', 'rel_bias_table', 'mix_out_gain', 'w_out', 'ffn_norm', 'w_up', 'conv_w', 'conv_b', 'w_down']
TWIN_WEIGHTS = ['attn_norm', 'w_in', 'a_q_gain', 'a_k_gain', 'a_sinks', 'c_q_gain', 'c_k_gain', 'rel_bias_table', 'mix_out_gain', 'w_out', 'ffn_norm', 'w_up', 'conv_w', 'conv_b', 'w_down']
TWIN_DIFF_INPUT = 'x'
TWIN_INPUTS = ['x', 'attn_norm', 'w_in', 'a_q_gain', 'a_k_gain', 'a_sinks', 'c_q_gain', 'c_k_gain', 'rel_bias_table', 'mix_out_gain', 'w_out', 'ffn_norm', 'w_up', 'conv_w', 'conv_b', 'w_down', 'loss_target', 'm_attn_norm', 'm_w_in', 'm_a_q_gain', 'm_a_k_gain', 'm_a_sinks', 'm_c_q_gain', 'm_c_k_gain', 'm_rel_bias_table', 'm_mix_out_gain', 'm_w_out', 'm_ffn_norm', 'm_w_up', 'm_conv_w', 'm_conv_b', 'm_w_down', 'v_attn_norm', 'v_w_in', 'v_a_q_gain', 'v_a_k_gain', 'v_a_sinks', 'v_c_q_gain', 'v_c_k_gain', 'v_rel_bias_table', 'v_mix_out_gain', 'v_w_out', 'v_ffn_norm', 'v_w_up', 'v_conv_w', 'v_conv_b', 'v_w_down']
TWIN_OUTPUTS = ['loss', 'grad_x', 'grad_attn_norm', 'grad_w_in', 'grad_a_q_gain', 'grad_a_k_gain', 'grad_a_sinks', 'grad_c_q_gain', 'grad_c_k_gain', 'grad_rel_bias_table', 'grad_mix_out_gain', 'grad_w_out', 'grad_ffn_norm', 'grad_w_up', 'grad_conv_w', 'grad_conv_b', 'grad_w_down', 'delta_attn_norm', 'delta_w_in', 'delta_a_q_gain', 'delta_a_k_gain', 'delta_a_sinks', 'delta_c_q_gain', 'delta_c_k_gain', 'delta_rel_bias_table', 'delta_mix_out_gain', 'delta_w_out', 'delta_ffn_norm', 'delta_w_up', 'delta_conv_w', 'delta_conv_b', 'delta_w_down', 'new_m_attn_norm', 'new_m_w_in', 'new_m_a_q_gain', 'new_m_a_k_gain', 'new_m_a_sinks', 'new_m_c_q_gain', 'new_m_c_k_gain', 'new_m_rel_bias_table', 'new_m_mix_out_gain', 'new_m_w_out', 'new_m_ffn_norm', 'new_m_w_up', 'new_m_conv_w', 'new_m_conv_b', 'new_m_w_down', 'new_v_attn_norm', 'new_v_w_in', 'new_v_a_q_gain', 'new_v_a_k_gain', 'new_v_a_sinks', 'new_v_c_q_gain', 'new_v_c_k_gain', 'new_v_rel_bias_table', 'new_v_mix_out_gain', 'new_v_w_out', 'new_v_ffn_norm', 'new_v_w_up', 'new_v_conv_w', 'new_v_conv_b', 'new_v_w_down']
TWIN_LEAF_KINDS = {'loss': 'loss', 'grad_x': 'grad_x', 'grad_attn_norm': 'grad_w', 'grad_w_in': 'grad_w', 'grad_a_q_gain': 'grad_w', 'grad_a_k_gain': 'grad_w', 'grad_a_sinks': 'grad_w', 'grad_c_q_gain': 'grad_w', 'grad_c_k_gain': 'grad_w', 'grad_rel_bias_table': 'grad_w', 'grad_mix_out_gain': 'grad_w', 'grad_w_out': 'grad_w', 'grad_ffn_norm': 'grad_w', 'grad_w_up': 'grad_w', 'grad_conv_w': 'grad_w', 'grad_conv_b': 'grad_w', 'grad_w_down': 'grad_w', 'delta_attn_norm': 'delta_w', 'delta_w_in': 'delta_w', 'delta_a_q_gain': 'delta_w', 'delta_a_k_gain': 'delta_w', 'delta_a_sinks': 'delta_w', 'delta_c_q_gain': 'delta_w', 'delta_c_k_gain': 'delta_w', 'delta_rel_bias_table': 'delta_w', 'delta_mix_out_gain': 'delta_w', 'delta_w_out': 'delta_w', 'delta_ffn_norm': 'delta_w', 'delta_w_up': 'delta_w', 'delta_conv_w': 'delta_w', 'delta_conv_b': 'delta_w', 'delta_w_down': 'delta_w', 'new_m_attn_norm': 'new_m', 'new_m_w_in': 'new_m', 'new_m_a_q_gain': 'new_m', 'new_m_a_k_gain': 'new_m', 'new_m_a_sinks': 'new_m', 'new_m_c_q_gain': 'new_m', 'new_m_c_k_gain': 'new_m', 'new_m_rel_bias_table': 'new_m', 'new_m_mix_out_gain': 'new_m', 'new_m_w_out': 'new_m', 'new_m_ffn_norm': 'new_m', 'new_m_w_up': 'new_m', 'new_m_conv_w': 'new_m', 'new_m_conv_b': 'new_m', 'new_m_w_down': 'new_m', 'new_v_attn_norm': 'new_v', 'new_v_w_in': 'new_v', 'new_v_a_q_gain': 'new_v', 'new_v_a_k_gain': 'new_v', 'new_v_a_sinks': 'new_v', 'new_v_c_q_gain': 'new_v', 'new_v_c_k_gain': 'new_v', 'new_v_rel_bias_table': 'new_v', 'new_v_mix_out_gain': 'new_v', 'new_v_w_out': 'new_v', 'new_v_ffn_norm': 'new_v', 'new_v_w_up': 'new_v', 'new_v_conv_w': 'new_v', 'new_v_conv_b': 'new_v', 'new_v_w_down': 'new_v'}


def _forward(args):
    return _fwd_reference(*[args[k] for k in FWD_PARAMS])


def _output_shape():
    out = _jax.eval_shape(lambda: _forward(_fwd_setup_inputs(0)))
    return out.shape, out.dtype

N_MICROBATCH = 1
ADAM_LR = 0.001
ADAM_B1 = 0.9
ADAM_B2 = 0.999
ADAM_EPS = 1e-08
ADAM_WD = 0.01
ADAM_STEP = 10
PER_EXAMPLE_BATCH_AXIS = {'x': 0, 'loss_target': 0}
SHARED_INPUTS = []
_WEIGHT_DTYPES = {'attn_norm': _jnp.float32, 'w_in': _jnp.float32, 'a_q_gain': _jnp.float32, 'a_k_gain': _jnp.float32, 'a_sinks': _jnp.float32, 'c_q_gain': _jnp.float32, 'c_k_gain': _jnp.float32, 'rel_bias_table': _jnp.float32, 'mix_out_gain': _jnp.float32, 'w_out': _jnp.float32, 'ffn_norm': _jnp.float32, 'w_up': _jnp.float32, 'conv_w': _jnp.float32, 'conv_b': _jnp.float32, 'w_down': _jnp.float32}
MOMENT_SCALE = {'attn_norm': 4.776449e+00, 'w_in': 3.041162e+00, 'a_q_gain': 2.445753e+00, 'a_k_gain': 2.472326e+00, 'a_sinks': 9.571196e-01, 'c_q_gain': 4.391147e+00, 'c_k_gain': 4.398487e+00, 'rel_bias_table': 3.348553e+00, 'mix_out_gain': 1.490729e+01, 'w_out': 4.954233e+00, 'ffn_norm': 1.210441e+01, 'w_up': 5.598185e-01, 'conv_w': 1.755112e+00, 'conv_b': 2.822196e+00, 'w_down': 9.305714e-01}


def _to_microbatches(a, axis):
    t = _jnp.moveaxis(a, axis, 0)
    t = t.reshape((N_MICROBATCH, t.shape[0] // N_MICROBATCH) + t.shape[1:])
    return _jnp.moveaxis(t, 1, axis + 1)


def setup_inputs(seed: int = 0) -> dict:
    inp = _fwd_setup_inputs(seed)
    key = _jax.random.fold_in(_jax.random.key(seed), 7919)
    shape, _ = _output_shape()
    out = dict(inp)
    out["loss_target"] = _jax.random.normal(_jax.random.fold_in(key, 0), shape, _jnp.float32)
    for i, name in enumerate(TWIN_WEIGHTS):
        w = inp[name].astype(_jnp.float32)
        if MOMENT_SCALE is None:
            s = _jnp.sqrt(_jnp.mean(_jnp.square(w)) + 1e-30)
        else:
            s = MOMENT_SCALE[name]
        km, kv = _jax.random.split(_jax.random.fold_in(key, i + 1))
        out[name] = w
        out["m_" + name] = s * _jax.random.normal(km, w.shape, _jnp.float32)
        out["v_" + name] = (s * s) * _jax.random.uniform(kv, w.shape, _jnp.float32, 0.5, 1.5)
    if N_MICROBATCH > 1:
        for name, axis in PER_EXAMPLE_BATCH_AXIS.items():
            out[name] = _to_microbatches(out[name], axis)
    return {'x': out['x'], 'attn_norm': out['attn_norm'], 'w_in': out['w_in'], 'a_q_gain': out['a_q_gain'], 'a_k_gain': out['a_k_gain'], 'a_sinks': out['a_sinks'], 'c_q_gain': out['c_q_gain'], 'c_k_gain': out['c_k_gain'], 'rel_bias_table': out['rel_bias_table'], 'mix_out_gain': out['mix_out_gain'], 'w_out': out['w_out'], 'ffn_norm': out['ffn_norm'], 'w_up': out['w_up'], 'conv_w': out['conv_w'], 'conv_b': out['conv_b'], 'w_down': out['w_down'], 'loss_target': out['loss_target'], 'm_attn_norm': out['m_attn_norm'], 'm_w_in': out['m_w_in'], 'm_a_q_gain': out['m_a_q_gain'], 'm_a_k_gain': out['m_a_k_gain'], 'm_a_sinks': out['m_a_sinks'], 'm_c_q_gain': out['m_c_q_gain'], 'm_c_k_gain': out['m_c_k_gain'], 'm_rel_bias_table': out['m_rel_bias_table'], 'm_mix_out_gain': out['m_mix_out_gain'], 'm_w_out': out['m_w_out'], 'm_ffn_norm': out['m_ffn_norm'], 'm_w_up': out['m_w_up'], 'm_conv_w': out['m_conv_w'], 'm_conv_b': out['m_conv_b'], 'm_w_down': out['m_w_down'], 'v_attn_norm': out['v_attn_norm'], 'v_w_in': out['v_w_in'], 'v_a_q_gain': out['v_a_q_gain'], 'v_a_k_gain': out['v_a_k_gain'], 'v_a_sinks': out['v_a_sinks'], 'v_c_q_gain': out['v_c_q_gain'], 'v_c_k_gain': out['v_c_k_gain'], 'v_rel_bias_table': out['v_rel_bias_table'], 'v_mix_out_gain': out['v_mix_out_gain'], 'v_w_out': out['v_w_out'], 'v_ffn_norm': out['v_ffn_norm'], 'v_w_up': out['v_w_up'], 'v_conv_w': out['v_conv_w'], 'v_conv_b': out['v_conv_b'], 'v_w_down': out['v_w_down']}


def _loss(weights, diff, rest, loss_target):
    with _jax.named_scope("forward"):
        args = {**rest, TWIN_DIFF_INPUT: diff, **{k: w.astype(_WEIGHT_DTYPES[k]) for k, w in weights.items()}}
        y = _forward(args)
    with _jax.named_scope("loss_head"):
        err = _jnp.square(y.astype(_jnp.float32) - loss_target)
        return 0.5 * _jnp.sum(_jnp.mean(err, axis=-1)) if err.ndim else 0.5 * err


def _adamw(w, g, m, v):
    m = ADAM_B1 * m + (1.0 - ADAM_B1) * g
    v = ADAM_B2 * v + (1.0 - ADAM_B2) * _jnp.square(g)
    m_hat = m / (1.0 - ADAM_B1 ** ADAM_STEP)
    v_hat = v / (1.0 - ADAM_B2 ** ADAM_STEP)
    delta = -ADAM_LR * (m_hat / (_jnp.sqrt(v_hat) + ADAM_EPS) + ADAM_WD * w)
    return delta, m, v


def reference(x, attn_norm, w_in, a_q_gain, a_k_gain, a_sinks, c_q_gain, c_k_gain, rel_bias_table, mix_out_gain, w_out, ffn_norm, w_up, conv_w, conv_b, w_down, loss_target, m_attn_norm, m_w_in, m_a_q_gain, m_a_k_gain, m_a_sinks, m_c_q_gain, m_c_k_gain, m_rel_bias_table, m_mix_out_gain, m_w_out, m_ffn_norm, m_w_up, m_conv_w, m_conv_b, m_w_down, v_attn_norm, v_w_in, v_a_q_gain, v_a_k_gain, v_a_sinks, v_c_q_gain, v_c_k_gain, v_rel_bias_table, v_mix_out_gain, v_w_out, v_ffn_norm, v_w_up, v_conv_w, v_conv_b, v_w_down):
    given = dict(x=x, attn_norm=attn_norm, w_in=w_in, a_q_gain=a_q_gain, a_k_gain=a_k_gain, a_sinks=a_sinks, c_q_gain=c_q_gain, c_k_gain=c_k_gain, rel_bias_table=rel_bias_table, mix_out_gain=mix_out_gain, w_out=w_out, ffn_norm=ffn_norm, w_up=w_up, conv_w=conv_w, conv_b=conv_b, w_down=w_down, loss_target=loss_target, m_attn_norm=m_attn_norm, m_w_in=m_w_in, m_a_q_gain=m_a_q_gain, m_a_k_gain=m_a_k_gain, m_a_sinks=m_a_sinks, m_c_q_gain=m_c_q_gain, m_c_k_gain=m_c_k_gain, m_rel_bias_table=m_rel_bias_table, m_mix_out_gain=m_mix_out_gain, m_w_out=m_w_out, m_ffn_norm=m_ffn_norm, m_w_up=m_w_up, m_conv_w=m_conv_w, m_conv_b=m_conv_b, m_w_down=m_w_down, v_attn_norm=v_attn_norm, v_w_in=v_w_in, v_a_q_gain=v_a_q_gain, v_a_k_gain=v_a_k_gain, v_a_sinks=v_a_sinks, v_c_q_gain=v_c_q_gain, v_c_k_gain=v_c_k_gain, v_rel_bias_table=v_rel_bias_table, v_mix_out_gain=v_mix_out_gain, v_w_out=v_w_out, v_ffn_norm=v_ffn_norm, v_w_up=v_w_up, v_conv_w=v_conv_w, v_conv_b=v_conv_b, v_w_down=v_w_down)
    weights = {n: given[n] for n in TWIN_WEIGHTS}
    shared = {n: given[n] for n in SHARED_INPUTS}
    per_example = {n: given[n] for n in ['x']}
    grad_fn = _jax.value_and_grad(_loss, argnums=(0, 1))

    def one_microbatch(ex, loss_target):
        ex = dict(ex)
        diff = ex.pop(TWIN_DIFF_INPUT)
        return grad_fn(weights, diff, {**shared, **ex}, loss_target)

    if N_MICROBATCH == 1:
        loss, (grad_w, grad_x) = one_microbatch(per_example, given["loss_target"])
    else:
        def body(carry, xs):
            loss_sum, grad_sum = carry
            l_k, (gw_k, gx_k) = one_microbatch(xs[0], xs[1])
            with _jax.named_scope("update"):
                return (loss_sum + l_k, _jax.tree.map(_jnp.add, grad_sum, gw_k)), gx_k

        init = (_jnp.zeros((), _jnp.float32), _jax.tree.map(_jnp.zeros_like, weights))
        (loss, grad_w), grad_x = _jax.lax.scan(body, init, (per_example, given["loss_target"]))
    with _jax.named_scope("update"):
        delta_w, new_m, new_v = {}, {}, {}
        for n in TWIN_WEIGHTS:
            delta_w[n], new_m[n], new_v[n] = _adamw(weights[n], grad_w[n], given["m_" + n], given["v_" + n])
    return (loss, grad_x, *[grad_w[n] for n in TWIN_WEIGHTS], *[delta_w[n] for n in TWIN_WEIGHTS],
            *[new_m[n] for n in TWIN_WEIGHTS], *[new_v[n] for n in TWIN_WEIGHTS])
```

```python
import numpy as np
import jax
import jax.numpy as jnp
from jax import lax
from jax.experimental import pallas as pl
from jax.experimental.pallas import tpu as pltpu

F32 = jnp.float32
BF16 = jnp.bfloat16
MESH = pl.DeviceIdType.MESH

HEAD_DIM = 64
BLOCK = 128
LANES = 128
EPS = 1e-6
NEG_INF = -1e30
WINDOW_A = 128
DILATED_PAIRS = ((128, 1), (512, 4), (2048, 16))
N_BUCKETS = 32
T5_MAX_DIST = 2048
CONV_WIDTH = 3
ADAM_LR = 0.001
ADAM_B1 = 0.9
ADAM_B2 = 0.999
ADAM_EPS = 1e-08
ADAM_WD = 0.01
ADAM_STEP = 10
N_CHIPS = 4
N_DEVICES = 8
VMEM_LIMIT_BYTES = 48 * 1024 * 1024
QK_SCALE = HEAD_DIM ** -0.5


def _params(sem=None):
    return pltpu.CompilerParams(dimension_semantics=sem, vmem_limit_bytes=VMEM_LIMIT_BYTES)


def _div_tile(n, cap, mult):
    best = None
    for t in range(mult, min(n, cap) + 1, mult):
        if n % t == 0:
            best = t
    return n if best is None else best


def _dot(a, b):
    return lax.dot_general(a, b, (((1,), (0,)), ((), ())), preferred_element_type=F32)


def _dot_nt(a, b):
    return lax.dot_general(a, b, (((1,), (1,)), ((), ())), preferred_element_type=F32)


def _dot_tn(a, b):
    return lax.dot_general(a, b, (((0,), (0,)), ((), ())), preferred_element_type=F32)


def _split_dot(x, m):
    hi = x.astype(BF16)
    lo = (x - hi.astype(F32)).astype(BF16)
    return _dot(hi, m) + _dot(lo, m)


class _Cfg:
    def __init__(self, d_model, d_ff):
        nh = d_model // HEAD_DIM
        self.d = d_model
        self.f = d_ff
        self.nha = nh // 4
        self.nkva = self.nha // 4
        self.nhb = nh // 4
        self.nhc = nh // 2
        self.a_q = self.nha * HEAD_DIM
        self.a_kv = self.nkva * HEAD_DIM
        self.b_w = self.nhb * HEAD_DIM
        self.c_w = self.nhc * HEAD_DIM
        sizes = [self.a_q, self.a_kv, self.a_kv, self.b_w, self.b_w, self.b_w, self.c_w, self.c_w, self.c_w]
        starts = [0] + [int(s) for s in np.cumsum(sizes)[:-1]]
        self.sections = list(zip(starts, sizes))
        self.in_width = int(sum(sizes))
        assert all(s % LANES == 0 for s in sizes)


def _matmul(a, b, mode, out_dtype, name, tm=512, tn=512, tk=512, residual=None):
    if mode == "tn":
        kdim, m = a.shape
    else:
        m, kdim = a.shape
    n = b.shape[0] if mode == "nt" else b.shape[1]
    tm, tn, tk = _div_tile(m, tm, LANES), _div_tile(n, tn, LANES), _div_tile(kdim, tk, LANES)
    nk = kdim // tk
    if mode == "tn":
        a_spec = pl.BlockSpec((tk, tm), lambda i, j, k: (k, i))
    else:
        a_spec = pl.BlockSpec((tm, tk), lambda i, j, k: (i, k))
    if mode == "nt":
        b_spec = pl.BlockSpec((tn, tk), lambda i, j, k: (j, k))
    else:
        b_spec = pl.BlockSpec((tk, tn), lambda i, j, k: (k, j))
    dot = {"nn": _dot, "nt": _dot_nt, "tn": _dot_tn}[mode]
    o_spec = pl.BlockSpec((tm, tn), lambda i, j, k: (i, j))
    in_specs = [a_spec, b_spec]
    args = [a, b]
    if residual is not None:
        in_specs.append(o_spec)
        args.append(residual)

    def body(*refs):
        if residual is None:
            a_ref, b_ref, o_ref, acc = refs
        else:
            a_ref, b_ref, r_ref, o_ref, acc = refs
        k = pl.program_id(2)

        @pl.when(k == 0)
        def _():
            acc[...] = jnp.zeros_like(acc)

        acc[...] += dot(a_ref[...].astype(BF16), b_ref[...].astype(BF16))

        @pl.when(k == nk - 1)
        def _():
            r = acc[...]
            if residual is not None:
                r = r + r_ref[...]
            o_ref[...] = r.astype(out_dtype)

    return pl.pallas_call(
        body, name=name, grid=(m // tm, n // tn, nk), in_specs=in_specs, out_specs=o_spec,
        out_shape=jax.ShapeDtypeStruct((m, n), out_dtype), scratch_shapes=[pltpu.VMEM((tm, tn), F32)],
        compiler_params=_params(("parallel", "parallel", "arbitrary")),
    )(*args)


def _rmsnorm_fwd(x, g, name):
    s, d = x.shape
    ts = _div_tile(s, 256, 8)

    def body(x_ref, g_ref, o_ref):
        xv = x_ref[...]
        r = lax.rsqrt(jnp.mean(xv * xv, axis=-1, keepdims=True) + EPS)
        o_ref[...] = (xv * r * g_ref[...]).astype(BF16)

    return pl.pallas_call(
        body, name=name, grid=(s // ts,),
        in_specs=[pl.BlockSpec((ts, d), lambda i: (i, 0)), pl.BlockSpec((1, d), lambda i: (0, 0))],
        out_specs=pl.BlockSpec((ts, d), lambda i: (i, 0)), out_shape=jax.ShapeDtypeStruct((s, d), BF16),
        compiler_params=_params(("parallel",)),
    )(x, g)


def _rmsnorm_bwd(x, g, dh, dres, name):
    s, d = x.shape
    ts = _div_tile(s, 256, 8)

    def body(x_ref, g_ref, dh_ref, dres_ref, dx_ref, dg_ref):
        @pl.when(pl.program_id(0) == 0)
        def _():
            dg_ref[...] = jnp.zeros_like(dg_ref)

        xv = x_ref[...]
        r = lax.rsqrt(jnp.mean(xv * xv, axis=-1, keepdims=True) + EPS)
        xhat = xv * r
        dhv = dh_ref[...]
        dxhat = dhv * g_ref[...]
        dx_ref[...] = dres_ref[...] + r * (dxhat - xhat * jnp.mean(dxhat * xhat, axis=-1, keepdims=True))
        dg_ref[...] += jnp.sum(dhv * xhat, axis=0, keepdims=True)

    row = pl.BlockSpec((ts, d), lambda i: (i, 0))
    vec = pl.BlockSpec((1, d), lambda i: (0, 0))
    return pl.pallas_call(
        body, name=name, grid=(s // ts,), in_specs=[row, vec, row, row], out_specs=[row, vec],
        out_shape=[jax.ShapeDtypeStruct((s, d), F32), jax.ShapeDtypeStruct((1, d), F32)],
        compiler_params=_params(("arbitrary",)),
    )(x, g, dh, dres)


def _head_mean_matrix():
    idx = np.arange(LANES) // HEAD_DIM
    return jnp.asarray((idx[:, None] == idx[None, :]).astype(np.float32) / HEAD_DIM, dtype=BF16)


def _head_mean(y, m128):
    w = y.shape[1]
    parts = [_split_dot(y[:, c * LANES:(c + 1) * LANES], m128) for c in range(w // LANES)]
    return parts[0] if len(parts) == 1 else jnp.concatenate(parts, axis=1)


_NORMED_SECTIONS = (0, 1, 6, 7)


def _qk_prep(proj, gains, cfg, name):
    s = proj.shape[0]
    ts = _div_tile(s, 256, 16)
    m128 = _head_mean_matrix()

    def body(p_ref, m_ref, g0, g1, g6, g7, *outs):
        gref = dict(zip(_NORMED_SECTIONS, (g0, g1, g6, g7)))
        for idx, (st, w) in enumerate(cfg.sections):
            xv = p_ref[:, st:st + w]
            if idx in gref:
                r = lax.rsqrt(_head_mean(xv * xv, m_ref[...]) + EPS)
                xv = xv * r * gref[idx][...]
            outs[idx][...] = xv.astype(BF16)

    in_specs = [pl.BlockSpec((ts, cfg.in_width), lambda i: (i, 0)), pl.BlockSpec((LANES, LANES), lambda i: (0, 0))]
    in_specs += [pl.BlockSpec((1, cfg.sections[k][1]), lambda i: (0, 0)) for k in _NORMED_SECTIONS]
    out_specs = [pl.BlockSpec((ts, w), lambda i: (i, 0)) for _, w in cfg.sections]
    out_shape = [jax.ShapeDtypeStruct((s, w), BF16) for _, w in cfg.sections]
    return pl.pallas_call(
        body, name=name, grid=(s // ts,), in_specs=in_specs, out_specs=out_specs, out_shape=out_shape,
        compiler_params=_params(("parallel",)),
    )(proj, m128, *gains)


def _qk_prep_bwd(proj, gains, grads, cfg, name):
    s = proj.shape[0]
    ts = _div_tile(s, 128, 16)
    m128 = _head_mean_matrix()
    counts = [len(gl) for gl in grads]
    flat = [g for gl in grads for g in gl]

    def body(*refs):
        p_ref, m_ref = refs[0], refs[1]
        gref = dict(zip(_NORMED_SECTIONS, refs[2:6]))
        g_in = refs[6:6 + len(flat)]
        dp_ref = refs[6 + len(flat)]
        dgain = dict(zip(_NORMED_SECTIONS, refs[7 + len(flat):]))

        @pl.when(pl.program_id(0) == 0)
        def _():
            for k in _NORMED_SECTIONS:
                dgain[k][...] = jnp.zeros_like(dgain[k])

        pos = 0
        for idx, (st, w) in enumerate(cfg.sections):
            dy = g_in[pos][...]
            for extra in g_in[pos + 1:pos + counts[idx]]:
                dy = dy + extra[...]
            pos += counts[idx]
            if idx in gref:
                xv = p_ref[:, st:st + w]
                r = lax.rsqrt(_head_mean(xv * xv, m_ref[...]) + EPS)
                xhat = xv * r
                dxhat = dy * gref[idx][...]
                dgain[idx][...] += jnp.sum(dy * xhat, axis=0, keepdims=True)
                dy = r * (dxhat - xhat * _head_mean(dxhat * xhat, m_ref[...]))
            dp_ref[:, st:st + w] = dy.astype(BF16)

    in_specs = [pl.BlockSpec((ts, cfg.in_width), lambda i: (i, 0)), pl.BlockSpec((LANES, LANES), lambda i: (0, 0))]
    in_specs += [pl.BlockSpec((1, cfg.sections[k][1]), lambda i: (0, 0)) for k in _NORMED_SECTIONS]
    for idx, (_, w) in enumerate(cfg.sections):
        in_specs += [pl.BlockSpec((ts, w), lambda i: (i, 0))] * counts[idx]
    out_specs = [pl.BlockSpec((ts, cfg.in_width), lambda i: (i, 0))]
    out_specs += [pl.BlockSpec((1, cfg.sections[k][1]), lambda i: (0, 0)) for k in _NORMED_SECTIONS]
    out_shape = [jax.ShapeDtypeStruct((s, cfg.in_width), BF16)]
    out_shape += [jax.ShapeDtypeStruct((1, cfg.sections[k][1]), F32) for k in _NORMED_SECTIONS]
    return pl.pallas_call(
        body, name=name, grid=(s // ts,), in_specs=in_specs, out_specs=out_specs, out_shape=out_shape,
        compiler_params=_params(("arbitrary",)),
    )(proj, m128, *gains, *flat)


def _band_masks(max_dist):
    row = lax.broadcasted_iota(jnp.int32, (BLOCK, BLOCK), 0)
    col = lax.broadcasted_iota(jnp.int32, (BLOCK, BLOCK), 1)
    return row + BLOCK - col <= max_dist, col <= row


def _banded_fwd(q, k, v, bias, sinks, hq, hk, max_dist, dil, name):
    s = q.shape[0]
    wq, wk, sd, grp = hq * HEAD_DIM, hk * HEAD_DIM, s // dil, hq // hk
    nb = sd // BLOCK
    has_sink = sinks is not None

    def body(*refs):
        if has_sink:
            q_ref, kp_ref, kc_ref, vp_ref, vc_ref, b_ref, s_ref, o_ref, l_ref = refs
        else:
            q_ref, kp_ref, kc_ref, vp_ref, vc_ref, b_ref, o_ref, l_ref = refs
        i = pl.program_id(1)
        mprev, mcur = _band_masks(max_dist)
        mprev = jnp.logical_and(mprev, i > 0)
        for h in range(hq):
            sq = slice(h * HEAD_DIM, (h + 1) * HEAD_DIM)
            sk = slice((h // grp) * HEAD_DIM, (h // grp + 1) * HEAD_DIM)
            qh = q_ref[:, sq]
            sp = jnp.where(mprev, _dot_nt(qh, kp_ref[:, sk]) * QK_SCALE + b_ref[h, :, 0:BLOCK], NEG_INF)
            sc = jnp.where(mcur, _dot_nt(qh, kc_ref[:, sk]) * QK_SCALE + b_ref[h, :, BLOCK:2 * BLOCK], NEG_INF)
            m = jnp.maximum(jnp.max(sp, axis=-1, keepdims=True), jnp.max(sc, axis=-1, keepdims=True))
            if has_sink:
                m = jnp.maximum(m, s_ref[h])
            pp = jnp.exp(sp - m)
            pc = jnp.exp(sc - m)
            den = jnp.sum(pp, axis=-1, keepdims=True) + jnp.sum(pc, axis=-1, keepdims=True)
            if has_sink:
                den = den + jnp.exp(s_ref[h] - m)
            acc = _dot(pp.astype(BF16), vp_ref[:, sk]) + _dot(pc.astype(BF16), vc_ref[:, sk])
            o_ref[:, sq] = acc / den
            l_ref[:, sq] = jnp.broadcast_to(m + jnp.log(den), (BLOCK, HEAD_DIM))

    qspec = pl.BlockSpec((BLOCK, wq), lambda r, i: (i, r))
    kprev = pl.BlockSpec((BLOCK, wk), lambda r, i: (jnp.maximum(i - 1, 0), r))
    kcur = pl.BlockSpec((BLOCK, wk), lambda r, i: (i, r))
    in_specs = [qspec, kprev, kcur, kprev, kcur, pl.BlockSpec((hq, BLOCK, 2 * BLOCK), lambda r, i: (0, 0, 0))]
    args = [q.reshape(sd, dil * wq), k.reshape(sd, dil * wk), k.reshape(sd, dil * wk),
            v.reshape(sd, dil * wk), v.reshape(sd, dil * wk), bias]
    if has_sink:
        in_specs.append(pl.BlockSpec(memory_space=pltpu.SMEM))
        args.append(sinks)
    out, lse = pl.pallas_call(
        body, name=name, grid=(dil, nb), in_specs=in_specs, out_specs=[qspec, qspec],
        out_shape=[jax.ShapeDtypeStruct((sd, dil * wq), F32)] * 2,
        compiler_params=_params(("parallel", "parallel")),
    )(*args)
    return out.reshape(s, wq), lse.reshape(s, wq)


def _banded_bwd(q, k, v, o, lse, do, bias, sinks, dbias_init, hq, hk, max_dist, dil, name):
    s = q.shape[0]
    wq, wk, sd, grp = hq * HEAD_DIM, hk * HEAD_DIM, s // dil, hq // hk
    nb = sd // BLOCK
    has_sink = sinks is not None

    def body(*refs):
        (q_ref, qn_ref, kp_ref, kc_ref, vp_ref, vc_ref, o_ref, on_ref, l_ref, ln_ref, do_ref, don_ref,
         b_ref, dbi_ref) = refs[:14]
        rest = refs[14:]
        if has_sink:
            s_ref, dq_ref, dk_ref, dv_ref, db_ref, ds_ref = rest
        else:
            dq_ref, dk_ref, dv_ref, db_ref = rest
        j = pl.program_id(1)

        @pl.when(jnp.logical_and(pl.program_id(0) == 0, j == 0))
        def _():
            db_ref[...] = dbi_ref[...]
            if has_sink:
                ds_ref[...] = jnp.zeros_like(ds_ref)

        mprev_static, mcur = _band_masks(max_dist)
        mprev = jnp.logical_and(mprev_static, j > 0)
        mnext = jnp.logical_and(mprev_static, j + 1 < nb)
        dk_acc = [jnp.zeros((BLOCK, HEAD_DIM), F32) for _ in range(hk)]
        dv_acc = [jnp.zeros((BLOCK, HEAD_DIM), F32) for _ in range(hk)]
        for h in range(hq):
            g = h // grp
            sq = slice(h * HEAD_DIM, (h + 1) * HEAD_DIM)
            sk = slice(g * HEAD_DIM, (g + 1) * HEAD_DIM)
            kp, kc, vp, vc = kp_ref[:, sk], kc_ref[:, sk], vp_ref[:, sk], vc_ref[:, sk]
            qh = q_ref[:, sq]
            doh = do_ref[:, sq]
            dohb = doh.astype(BF16)
            lcol = l_ref[:, h * HEAD_DIM:h * HEAD_DIM + 1]
            dcol = jnp.sum(doh * o_ref[:, sq], axis=-1, keepdims=True)
            sp = _dot_nt(qh, kp) * QK_SCALE + b_ref[h, :, 0:BLOCK]
            sc = _dot_nt(qh, kc) * QK_SCALE + b_ref[h, :, BLOCK:2 * BLOCK]
            pp = jnp.where(mprev, jnp.exp(sp - lcol), 0.0)
            pc = jnp.where(mcur, jnp.exp(sc - lcol), 0.0)
            dsp = pp * (_dot_nt(dohb, vp) - dcol)
            dsc = pc * (_dot_nt(dohb, vc) - dcol)
            dspb, dscb = dsp.astype(BF16), dsc.astype(BF16)
            dq_ref[:, sq] = (_dot(dspb, kp) + _dot(dscb, kc)) * QK_SCALE
            db_ref[h, :, 0:BLOCK] += dsp
            db_ref[h, :, BLOCK:2 * BLOCK] += dsc
            if has_sink:
                psink = jnp.exp(s_ref[h] - lcol)
                tot = jnp.sum(psink * dcol, axis=0, keepdims=True)
                ds_ref[h:h + 1, :] -= jnp.broadcast_to(tot, (1, LANES))
            qn = qn_ref[:, sq]
            don = don_ref[:, sq]
            donb = don.astype(BF16)
            lncol = ln_ref[:, h * HEAD_DIM:h * HEAD_DIM + 1]
            dncol = jnp.sum(don * on_ref[:, sq], axis=-1, keepdims=True)
            sn = _dot_nt(qn, kc) * QK_SCALE + b_ref[h, :, 0:BLOCK]
            pn = jnp.where(mnext, jnp.exp(sn - lncol), 0.0)
            dsn = pn * (_dot_nt(donb, vc) - dncol)
            dk_acc[g] = dk_acc[g] + (_dot_tn(dscb, qh) + _dot_tn(dsn.astype(BF16), qn)) * QK_SCALE
            dv_acc[g] = dv_acc[g] + _dot_tn(pc.astype(BF16), dohb) + _dot_tn(pn.astype(BF16), donb)
        for g in range(hk):
            sk = slice(g * HEAD_DIM, (g + 1) * HEAD_DIM)
            dk_ref[:, sk] = dk_acc[g]
            dv_ref[:, sk] = dv_acc[g]

    qcur = pl.BlockSpec((BLOCK, wq), lambda r, j: (j, r))
    qnext = pl.BlockSpec((BLOCK, wq), lambda r, j: (jnp.minimum(j + 1, nb - 1), r))
    kprev = pl.BlockSpec((BLOCK, wk), lambda r, j: (jnp.maximum(j - 1, 0), r))
    kcur = pl.BlockSpec((BLOCK, wk), lambda r, j: (j, r))
    bspec = pl.BlockSpec((hq, BLOCK, 2 * BLOCK), lambda r, j: (0, 0, 0))
    q2, k2, v2 = q.reshape(sd, dil * wq), k.reshape(sd, dil * wk), v.reshape(sd, dil * wk)
    o2, l2, do2 = o.reshape(sd, dil * wq), lse.reshape(sd, dil * wq), do.reshape(sd, dil * wq)
    in_specs = [qcur, qnext, kprev, kcur, kprev, kcur, qcur, qnext, qcur, qnext, qcur, qnext, bspec, bspec]
    args = [q2, q2, k2, k2, v2, v2, o2, o2, l2, l2, do2, do2, bias, dbias_init]
    out_specs = [qcur, kcur, kcur, bspec]
    out_shape = [jax.ShapeDtypeStruct((sd, dil * wq), F32), jax.ShapeDtypeStruct((sd, dil * wk), F32),
                 jax.ShapeDtypeStruct((sd, dil * wk), F32), jax.ShapeDtypeStruct((hq, BLOCK, 2 * BLOCK), F32)]
    if has_sink:
        in_specs.append(pl.BlockSpec(memory_space=pltpu.SMEM))
        args.append(sinks)
        out_specs.append(pl.BlockSpec((hq, LANES), lambda r, j: (0, 0)))
        out_shape.append(jax.ShapeDtypeStruct((hq, LANES), F32))
    res = pl.pallas_call(
        body, name=name, grid=(dil, nb), in_specs=in_specs, out_specs=out_specs, out_shape=out_shape,
        compiler_params=_params(("arbitrary", "arbitrary")),
    )(*args)
    dq, dk, dv, dbias = res[0].reshape(s, wq), res[1].reshape(s, wk), res[2].reshape(s, wk), res[3]
    return dq, dk, dv, dbias, (res[4][:, 0] if has_sink else None)


def _neg_softplus(z):
    return -(jnp.maximum(z, 0.0) + jnp.log(1.0 + jnp.exp(-jnp.abs(z))))


def _tri(kind):
    row = lax.broadcasted_iota(jnp.int32, (BLOCK, BLOCK), 0)
    col = lax.broadcasted_iota(jnp.int32, (BLOCK, BLOCK), 1)
    return {"ge": row >= col, "lt": row < col, "le": row <= col, "gt": row > col}[kind]


def _sb_fwd(q, k, v, name):
    s, w = q.shape
    npair, nb = w // LANES, s // BLOCK

    def body(q_ref, k_ref, v_ref, o_ref, t_ref):
        i = pl.program_id(1)
        strict = _tri("gt")
        lincl = _tri("ge").astype(BF16)
        for hh in range(LANES // HEAD_DIM):
            sl = slice(hh * HEAD_DIM, (hh + 1) * HEAD_DIM)
            qh = q_ref[:, sl]

            def tile(jj, diag, carry, qh=qh, sl=sl):
                o_acc, rem = carry
                rows = pl.ds(pl.multiple_of(jj * BLOCK, BLOCK), BLOCK)
                ks, vs = k_ref[rows, sl], v_ref[rows, sl]
                z = _dot_nt(qh, ks) * QK_SCALE
                lr = _neg_softplus(z)
                if diag:
                    lr = jnp.where(strict, lr, 0.0)
                c = rem + _split_dot(lr, lincl)
                a = jnp.exp(z + c)
                if diag:
                    a = jnp.where(strict, a, 0.0)
                return o_acc + _dot(a.astype(BF16), vs), rem + jnp.sum(lr, axis=-1, keepdims=True)

            carry = tile(i, True, (jnp.zeros((BLOCK, HEAD_DIM), F32), jnp.zeros((BLOCK, 1), F32)))
            carry = lax.fori_loop(0, i, lambda t, cr: tile(i - 1 - t, False, cr), carry)
            o_ref[:, sl] = carry[0]
            t_ref[:, sl] = jnp.broadcast_to(carry[1], (BLOCK, HEAD_DIM))

    qspec = pl.BlockSpec((BLOCK, LANES), lambda p, i: (i, p))
    kspec = pl.BlockSpec((s, LANES), lambda p, i: (0, p))
    return pl.pallas_call(
        body, name=name, grid=(npair, nb), in_specs=[qspec, kspec, kspec], out_specs=[qspec, qspec],
        out_shape=[jax.ShapeDtypeStruct((s, w), F32)] * 2, compiler_params=_params(("parallel", "parallel")),
    )(q, k, v)


def _sb_bwd(q, k, v, tot, do, name):
    s, w = q.shape
    npair, nb = w // LANES, s // BLOCK

    def body(q_ref, k_ref, v_ref, t_ref, do_ref, dq_ref, dk_ref, dv_ref):
        i = pl.program_id(1)

        @pl.when(i == 0)
        def _():
            dk_ref[...] = jnp.zeros_like(dk_ref)
            dv_ref[...] = jnp.zeros_like(dv_ref)

        strict = _tri("gt")
        lbefore = _tri("lt").astype(BF16)
        lupto = _tri("le").astype(BF16)
        for hh in range(LANES // HEAD_DIM):
            sl = slice(hh * HEAD_DIM, (hh + 1) * HEAD_DIM)
            qh = q_ref[:, sl]
            dohb = do_ref[:, sl].astype(BF16)
            total = t_ref[:, hh * HEAD_DIM:hh * HEAD_DIM + 1]

            def tile(jj, diag, carry, qh=qh, dohb=dohb, total=total, sl=sl):
                dq_acc, plr, pg = carry
                rows = pl.ds(pl.multiple_of(jj * BLOCK, BLOCK), BLOCK)
                ks, vs = k_ref[rows, sl], v_ref[rows, sl]
                z = _dot_nt(qh, ks) * QK_SCALE
                lr = _neg_softplus(z)
                if diag:
                    lr = jnp.where(strict, lr, 0.0)
                c = total - (plr + _split_dot(lr, lbefore))
                a = jnp.exp(z + c)
                if diag:
                    a = jnp.where(strict, a, 0.0)
                g = _dot_nt(dohb, vs) * a
                dlr = pg + _split_dot(g, lupto)
                dz = g - jnp.exp(z + lr) * dlr
                if diag:
                    dz = jnp.where(strict, dz, 0.0)
                dzb = (dz * QK_SCALE).astype(BF16)
                dk_ref[rows, sl] += _dot_tn(dzb, qh)
                dv_ref[rows, sl] += _dot_tn(a.astype(BF16), dohb)
                return (dq_acc + _dot(dzb, ks), plr + jnp.sum(lr, axis=-1, keepdims=True),
                        pg + jnp.sum(g, axis=-1, keepdims=True))

            zero = jnp.zeros((BLOCK, 1), F32)
            carry = lax.fori_loop(0, i, lambda t, cr: tile(t, False, cr),
                                  (jnp.zeros((BLOCK, HEAD_DIM), F32), zero, zero))
            carry = tile(i, True, carry)
            dq_ref[:, sl] = carry[0]

    qspec = pl.BlockSpec((BLOCK, LANES), lambda p, i: (i, p))
    kspec = pl.BlockSpec((s, LANES), lambda p, i: (0, p))
    return pl.pallas_call(
        body, name=name, grid=(npair, nb), in_specs=[qspec, kspec, kspec, qspec, qspec],
        out_specs=[qspec, kspec, kspec], out_shape=[jax.ShapeDtypeStruct((s, w), F32)] * 3,
        compiler_params=_params(("parallel", "arbitrary")),
    )(q, k, v, tot, do)


def _group_norm(xv, g):
    r = lax.rsqrt(jnp.mean(xv * xv, axis=-1, keepdims=True) + EPS)
    return xv * r * g


def _mix_fwd(oa, ob, ocs, lses, gain, cfg, name):
    s = oa.shape[0]
    ts = _div_tile(s, 256, 16)
    aq, bw, cw = cfg.a_q, cfg.b_w, cfg.c_w

    def body(oa_ref, ob_ref, c1, c2, c3, l1, l2, l3, g_ref, mix_ref, oc_ref, lse_ref):
        m = jnp.maximum(jnp.maximum(l1[...], l2[...]), l3[...])
        e1, e2, e3 = jnp.exp(l1[...] - m), jnp.exp(l2[...] - m), jnp.exp(l3[...] - m)
        den = e1 + e2 + e3
        oc = (e1 * c1[...] + e2 * c2[...] + e3 * c3[...]) / den
        oc_ref[...] = oc
        lse_ref[...] = m + jnp.log(den)
        mix_ref[:, 0:aq] = _group_norm(oa_ref[...], g_ref[:, 0:aq]).astype(BF16)
        mix_ref[:, aq:aq + bw] = _group_norm(ob_ref[...], g_ref[:, aq:aq + bw]).astype(BF16)
        mix_ref[:, aq + bw:] = _group_norm(oc, g_ref[:, aq + bw:]).astype(BF16)

    def row(wd):
        return pl.BlockSpec((ts, wd), lambda i: (i, 0))

    return pl.pallas_call(
        body, name=name, grid=(s // ts,),
        in_specs=[row(aq), row(bw)] + [row(cw)] * 6 + [pl.BlockSpec((1, cfg.d), lambda i: (0, 0))],
        out_specs=[row(cfg.d), row(cw), row(cw)],
        out_shape=[jax.ShapeDtypeStruct((s, cfg.d), BF16), jax.ShapeDtypeStruct((s, cw), F32),
                   jax.ShapeDtypeStruct((s, cw), F32)],
        compiler_params=_params(("parallel",)),
    )(oa, ob, *ocs, *lses, gain)


def _mix_bwd(dmix, oa, ob, oc, gain, cfg, name):
    s = oa.shape[0]
    ts = _div_tile(s, 256, 8)
    aq, bw, cw = cfg.a_q, cfg.b_w, cfg.c_w

    def body(dm_ref, oa_ref, ob_ref, oc_ref, g_ref, da_ref, db_ref, dc_ref, dg_ref):
        @pl.when(pl.program_id(0) == 0)
        def _():
            dg_ref[...] = jnp.zeros_like(dg_ref)

        for x_ref, dx_ref, lo, hi in ((oa_ref, da_ref, 0, aq), (ob_ref, db_ref, aq, aq + bw),
                                      (oc_ref, dc_ref, aq + bw, aq + bw + cw)):
            xv = x_ref[...]
            dy = dm_ref[:, lo:hi]
            r = lax.rsqrt(jnp.mean(xv * xv, axis=-1, keepdims=True) + EPS)
            xhat = xv * r
            dxhat = dy * g_ref[:, lo:hi]
            dx_ref[...] = r * (dxhat - xhat * jnp.mean(dxhat * xhat, axis=-1, keepdims=True))
            dg_ref[:, lo:hi] += jnp.sum(dy * xhat, axis=0, keepdims=True)

    def row(wd):
        return pl.BlockSpec((ts, wd), lambda i: (i, 0))

    vec = pl.BlockSpec((1, cfg.d), lambda i: (0, 0))
    return pl.pallas_call(
        body, name=name, grid=(s // ts,), in_specs=[row(cfg.d), row(aq), row(bw), row(cw), vec],
        out_specs=[row(aq), row(bw), row(cw), vec],
        out_shape=[jax.ShapeDtypeStruct((s, aq), F32), jax.ShapeDtypeStruct((s, bw), F32),
                   jax.ShapeDtypeStruct((s, cw), F32), jax.ShapeDtypeStruct((1, cfg.d), F32)],
        compiler_params=_params(("arbitrary",)),
    )(dmix, oa, ob, oc, gain)


def _bias_table_grad(dbiases, buckets, name):
    outs = []
    for idx, (db, bk) in enumerate(zip(dbiases, buckets)):
        h = db.shape[0]

        def body(db_ref, bk_ref, o_ref):
            xv = db_ref[0]
            ids = bk_ref[...]
            lane = lax.broadcasted_iota(jnp.int32, (1, LANES), 1)
            acc = jnp.zeros((1, LANES), F32)
            for b in range(N_BUCKETS):
                tot = jnp.sum(jnp.where(ids == b, xv, 0.0), axis=0, keepdims=True)
                tot = jnp.sum(tot, axis=1, keepdims=True)
                acc = jnp.where(lane == b, tot, acc)
            o_ref[0] = acc

        outs.append(pl.pallas_call(
            body, name=f"{name}_{idx}", grid=(h,),
            in_specs=[pl.BlockSpec((1, BLOCK, 2 * BLOCK), lambda i: (i, 0, 0)),
                      pl.BlockSpec((BLOCK, 2 * BLOCK), lambda i: (0, 0))],
            out_specs=pl.BlockSpec((1, 1, LANES), lambda i: (i, 0, 0)),
            out_shape=jax.ShapeDtypeStruct((h, 1, LANES), F32), compiler_params=_params(("parallel",)),
        )(db, bk)[:, 0, :])
    return outs


def _shift_down(u, n, rows):
    return jnp.where(rows >= n, pltpu.roll(u, n, 0), 0.0)


def _shift_up(u, n, rows, s):
    return jnp.where(rows < s - n, pltpu.roll(u, s - n, 0), 0.0)


def _conv(u, w_ref, b_ref, rows):
    return (b_ref[...] + w_ref[0:1, :] * _shift_down(u, 2, rows) + w_ref[1:2, :] * _shift_down(u, 1, rows)
            + w_ref[2:3, :] * u)


def _conv_act_fwd(u, conv_w, conv_b, f, name):
    s = u.shape[0]
    nf = f // LANES

    def body(ug_ref, uu_ref, wg_ref, wu_ref, bg_ref, bu_ref, act_ref):
        rows = lax.broadcasted_iota(jnp.int32, (s, LANES), 0)
        gate = _conv(ug_ref[...], wg_ref, bg_ref, rows)
        up = _conv(uu_ref[...], wu_ref, bu_ref, rows)
        act_ref[...] = (gate * jax.nn.sigmoid(gate) * up).astype(BF16)

    def col(rws, off):
        return pl.BlockSpec((rws, LANES), lambda j: (0, j + off))

    return pl.pallas_call(
        body, name=name, grid=(nf,),
        in_specs=[col(s, 0), col(s, nf), col(CONV_WIDTH, 0), col(CONV_WIDTH, nf), col(1, 0), col(1, nf)],
        out_specs=col(s, 0), out_shape=jax.ShapeDtypeStruct((s, f), BF16), compiler_params=_params(("parallel",)),
    )(u, u, conv_w, conv_w, conv_b, conv_b)


def _conv_act_bwd(u, dact, conv_w, conv_b, f, name):
    s = u.shape[0]
    nf = f // LANES

    def body(ug_ref, uu_ref, da_ref, wg_ref, wu_ref, bg_ref, bu_ref, dug_ref, duu_ref, dwg_ref, dwu_ref, dbg_ref,
             dbu_ref):
        rows = lax.broadcasted_iota(jnp.int32, (s, LANES), 0)
        ug, uu = ug_ref[...], uu_ref[...]
        gate = _conv(ug, wg_ref, bg_ref, rows)
        up = _conv(uu, wu_ref, bu_ref, rows)
        sg = jax.nn.sigmoid(gate)
        da = da_ref[...]
        dgate = da * up * (sg * (1.0 + gate * (1.0 - sg)))
        dup = da * (gate * sg)
        for du, uv, w_ref, du_ref, dw_ref, db_ref in ((dgate, ug, wg_ref, dug_ref, dwg_ref, dbg_ref),
                                                     (dup, uu, wu_ref, duu_ref, dwu_ref, dbu_ref)):
            du_ref[...] = (w_ref[2:3, :] * du + w_ref[1:2, :] * _shift_up(du, 1, rows, s)
                           + w_ref[0:1, :] * _shift_up(du, 2, rows, s)).astype(BF16)
            dw_ref[0:1, :] = jnp.sum(du * _shift_down(uv, 2, rows), axis=0, keepdims=True)
            dw_ref[1:2, :] = jnp.sum(du * _shift_down(uv, 1, rows), axis=0, keepdims=True)
            dw_ref[2:3, :] = jnp.sum(du * uv, axis=0, keepdims=True)
            db_ref[...] = jnp.sum(du, axis=0, keepdims=True)

    def col(rws, off):
        return pl.BlockSpec((rws, LANES), lambda j: (0, j + off))

    return pl.pallas_call(
        body, name=name, grid=(nf,),
        in_specs=[col(s, 0), col(s, nf), col(s, 0), col(CONV_WIDTH, 0), col(CONV_WIDTH, nf), col(1, 0), col(1, nf)],
        out_specs=[col(s, 0), col(s, 0), col(CONV_WIDTH, 0), col(CONV_WIDTH, 0), col(1, 0), col(1, 0)],
        out_shape=[jax.ShapeDtypeStruct((s, f), BF16)] * 2 + [jax.ShapeDtypeStruct((CONV_WIDTH, f), F32)] * 2
        + [jax.ShapeDtypeStruct((1, f), F32)] * 2,
        compiler_params=_params(("parallel",)),
    )(u, u, dact, conv_w, conv_w, conv_b, conv_b)


def _loss_head(y, target, name):
    s, d = y.shape
    ts = _div_tile(s, 256, 8)

    def body(y_ref, t_ref, dy_ref, l_ref):
        @pl.when(pl.program_id(0) == 0)
        def _():
            l_ref[...] = jnp.zeros_like(l_ref)

        err = y_ref[...] - t_ref[...]
        dy_ref[...] = err * (1.0 / d)
        tot = jnp.sum(jnp.sum(err * err, axis=0, keepdims=True), axis=1, keepdims=True) * (0.5 / d)
        l_ref[...] += jnp.broadcast_to(tot, l_ref.shape)

    row = pl.BlockSpec((ts, d), lambda i: (i, 0))
    return pl.pallas_call(
        body, name=name, grid=(s // ts,), in_specs=[row, row],
        out_specs=[row, pl.BlockSpec((8, LANES), lambda i: (0, 0))],
        out_shape=[jax.ShapeDtypeStruct((s, d), F32), jax.ShapeDtypeStruct((8, LANES), F32)],
        compiler_params=_params(("arbitrary",)),
    )(y, target)


def _adamw(w, g, m, v, name):
    r, c = w.shape
    tr = _div_tile(r, max(8, (1 << 18) // c // 8 * 8), 8)
    c1 = 1.0 - ADAM_B1 ** ADAM_STEP
    c2 = 1.0 - ADAM_B2 ** ADAM_STEP

    def body(w_ref, g_ref, m_ref, v_ref, d_ref, nm_ref, nv_ref):
        gv = g_ref[...]
        nm = ADAM_B1 * m_ref[...] + (1.0 - ADAM_B1) * gv
        nv = ADAM_B2 * v_ref[...] + (1.0 - ADAM_B2) * (gv * gv)
        d_ref[...] = -ADAM_LR * ((nm / c1) / (jnp.sqrt(nv / c2) + ADAM_EPS) + ADAM_WD * w_ref[...])
        nm_ref[...] = nm
        nv_ref[...] = nv

    spec = pl.BlockSpec((tr, c), lambda i: (i, 0))
    return pl.pallas_call(
        body, name=name, grid=(r // tr,), in_specs=[spec] * 4, out_specs=[spec] * 3,
        out_shape=[jax.ShapeDtypeStruct((r, c), F32)] * 3, compiler_params=_params(("parallel",)),
    )(w, g, m, v)


def _adamw_layer(layer, w, g, m, v, bufs, name):
    depth, r, c = w.shape
    tr = _div_tile(r, max(8, (1 << 17) // c // 8 * 8), 8)
    c1 = 1.0 - ADAM_B1 ** ADAM_STEP
    c2 = 1.0 - ADAM_B2 ** ADAM_STEP

    def body(*refs):
        w_ref, g_ref, m_ref, v_ref = refs[:4]
        go_ref, d_ref, nm_ref, nv_ref = refs[-4:]
        gv = g_ref[...]
        nm = ADAM_B1 * m_ref[...] + (1.0 - ADAM_B1) * gv
        nv = ADAM_B2 * v_ref[...] + (1.0 - ADAM_B2) * (gv * gv)
        d_ref[...] = -ADAM_LR * ((nm / c1) / (jnp.sqrt(nv / c2) + ADAM_EPS) + ADAM_WD * w_ref[...])
        nm_ref[...] = nm
        nv_ref[...] = nv
        go_ref[...] = gv

    lay = pl.BlockSpec((None, tr, c), lambda i: (layer, i, 0))
    in_specs = [lay, pl.BlockSpec((tr, c), lambda i: (i, 0)), lay, lay]
    args = [w, g, m, v]
    aliases = {}
    if bufs is not None:
        in_specs += [pl.BlockSpec(memory_space=pl.ANY)] * 4
        args += list(bufs)
        aliases = {4 + k: k for k in range(4)}
    return pl.pallas_call(
        body, name=name, grid=(r // tr,), in_specs=in_specs, out_specs=[lay] * 4,
        out_shape=[jax.ShapeDtypeStruct((depth, r, c), F32)] * 4, input_output_aliases=aliases,
        compiler_params=_params(("parallel",)),
    )(*args)


def _mesh_pos():
    return lax.axis_index("x"), lax.axis_index("y"), lax.axis_index("c")


def _flip(v, bit):
    return 1 - v if bit else v


def _gather_rows(shards, name):
    n = len(shards)
    any_spec = pl.BlockSpec(memory_space=pl.ANY)

    def body(*refs):
        ins, outs = refs[:n], refs[n:2 * n]
        send_sems, recv_sems, local_sems = refs[2 * n:]
        x, y, c = _mesh_pos()
        others = [(1 - x, y), (x, 1 - y), (1 - x, 1 - y)]
        sibling = (x, y, 1 - c)

        def part(t, chip, core):
            half = ins[t].shape[0] // 2
            return outs[t].at[chip[0] * 2 + chip[1], pl.ds(core * half, half), :]

        def copy(t, sem, src, dst, to):
            return pltpu.make_async_remote_copy(src_ref=src, dst_ref=dst, send_sem=send_sems.at[t, sem],
                                                recv_sem=recv_sems.at[t, sem], device_id=to, device_id_type=MESH)

        local = [pltpu.make_async_copy(ins[t], outs[t].at[2 * x + y], local_sems.at[t]) for t in range(n)]
        for cp in local:
            cp.start()
        started = []
        for t in range(n):
            half = ins[t].shape[0] // 2
            for j, chip in enumerate(others):
                cp = copy(t, j, ins[t].at[pl.ds(c * half, half), :], part(t, (x, y), c), (*chip, c))
                cp.start()
                started.append(cp)
        for t in range(n):
            for j, chip in enumerate(others):
                copy(t, j, part(t, chip, c), part(t, chip, c), (*chip, c)).wait_recv()
                cp = copy(t, 3 + j, part(t, chip, c), part(t, chip, c), sibling)
                cp.start()
                started.append(cp)
        for t in range(n):
            for j, chip in enumerate(others):
                copy(t, 3 + j, part(t, chip, 1 - c), part(t, chip, 1 - c), sibling).wait_recv()
        for cp in started:
            cp.wait_send()
        for cp in local:
            cp.wait()

    return pl.pallas_call(
        body, name=name, in_specs=[any_spec] * n, out_specs=[any_spec] * n,
        out_shape=[jax.ShapeDtypeStruct((N_CHIPS,) + sh.shape, sh.dtype) for sh in shards],
        scratch_shapes=[pltpu.SemaphoreType.DMA((n, 6)), pltpu.SemaphoreType.DMA((n, 6)),
                        pltpu.SemaphoreType.DMA((n,))],
    )(*shards)


def _scatter_parts(grads, name):
    n = len(grads)
    any_spec = pl.BlockSpec(memory_space=pl.ANY)

    def body(*refs):
        ins, outs = refs[:n], refs[n:2 * n]
        send_sems, recv_sems, local_sems = refs[2 * n:]
        x, y, c = _mesh_pos()
        me = 4 * x + 2 * y + c
        peers = []
        for d in range(1, N_DEVICES):
            peers.append((_flip(x, d & 4), _flip(y, d & 2), _flip(c, d & 1)))

        def mine_of(t, p):
            half = ins[t].shape[1] // 2
            return ins[t].at[2 * p[0] + p[1], pl.ds(p[2] * half, half), :]

        local = [pltpu.make_async_copy(mine_of(t, (x, y, c)), outs[t].at[me], local_sems.at[t]) for t in range(n)]
        for cp in local:
            cp.start()
        started = []
        for t in range(n):
            for d, p in enumerate(peers):
                cp = pltpu.make_async_remote_copy(src_ref=mine_of(t, p), dst_ref=outs[t].at[me],
                                                  send_sem=send_sems.at[t, d], recv_sem=recv_sems.at[t, d],
                                                  device_id=p, device_id_type=MESH)
                cp.start()
                started.append(cp)
        for t in range(n):
            for d, p in enumerate(peers):
                slot = outs[t].at[4 * p[0] + 2 * p[1] + p[2]]
                pltpu.make_async_remote_copy(src_ref=slot, dst_ref=slot, send_sem=send_sems.at[t, d],
                                             recv_sem=recv_sems.at[t, d], device_id=p,
                                             device_id_type=MESH).wait_recv()
        for cp in started:
            cp.wait_send()
        for cp in local:
            cp.wait()

    return pl.pallas_call(
        body, name=name, in_specs=[any_spec] * n, out_specs=[any_spec] * n,
        out_shape=[jax.ShapeDtypeStruct((N_DEVICES, g.shape[1] // 2, g.shape[2]), g.dtype) for g in grads],
        scratch_shapes=[pltpu.SemaphoreType.DMA((n, N_DEVICES - 1)), pltpu.SemaphoreType.DMA((n, N_DEVICES - 1)),
                        pltpu.SemaphoreType.DMA((n,))],
    )(*grads)


def _sum_parts(parts, name):
    _, r, c = parts.shape
    tr = _div_tile(r, 256, 16)

    def body(p_ref, o_ref):
        acc = p_ref[0].astype(F32)
        for src in range(1, N_DEVICES):
            acc = acc + p_ref[src].astype(F32)
        o_ref[...] = acc

    return pl.pallas_call(
        body, name=name, grid=(r // tr,), in_specs=[pl.BlockSpec((N_DEVICES, tr, c), lambda i: (0, i, 0))],
        out_specs=pl.BlockSpec((tr, c), lambda i: (i, 0)), out_shape=jax.ShapeDtypeStruct((r, c), F32),
        compiler_params=_params(("parallel",)),
    )(parts)


def _swap_halves(halves, name):
    n = len(halves)
    any_spec = pl.BlockSpec(memory_space=pl.ANY)

    def body(*refs):
        ins, outs = refs[:n], refs[n:2 * n]
        send_sems, recv_sems, local_sems = refs[2 * n:]
        x, y, c = _mesh_pos()
        work = []
        for t in range(n):
            h = ins[t].shape[0]
            mine, theirs = outs[t].at[pl.ds(c * h, h), :], outs[t].at[pl.ds((1 - c) * h, h), :]
            loc = pltpu.make_async_copy(ins[t], mine, local_sems.at[t])
            snd = pltpu.make_async_remote_copy(src_ref=ins[t], dst_ref=mine, send_sem=send_sems.at[t],
                                               recv_sem=recv_sems.at[t], device_id=(x, y, 1 - c), device_id_type=MESH)
            rcv = pltpu.make_async_remote_copy(src_ref=theirs, dst_ref=theirs, send_sem=send_sems.at[t],
                                               recv_sem=recv_sems.at[t], device_id=(x, y, 1 - c), device_id_type=MESH)
            loc.start()
            snd.start()
            work.append((loc, snd, rcv))
        for loc, snd, rcv in work:
            rcv.wait_recv()
            snd.wait_send()
            loc.wait()

    return pl.pallas_call(
        body, name=name, in_specs=[any_spec] * n, out_specs=[any_spec] * n,
        out_shape=[jax.ShapeDtypeStruct((2 * h.shape[0], h.shape[1]), h.dtype) for h in halves],
        scratch_shapes=[pltpu.SemaphoreType.DMA((n,)), pltpu.SemaphoreType.DMA((n,)), pltpu.SemaphoreType.DMA((n,))],
    )(*halves)


def _allreduce_small(flat, name):
    r = flat.shape[0]

    def body(x_ref, o_ref, buf, send_sems, recv_sems):
        x, y, c = _mesh_pos()
        me = 4 * x + 2 * y + c
        buf[me] = x_ref[...]
        started = []
        peers = [(_flip(x, d & 4), _flip(y, d & 2), _flip(c, d & 1)) for d in range(1, N_DEVICES)]
        for d, p in enumerate(peers):
            cp = pltpu.make_async_remote_copy(src_ref=x_ref, dst_ref=buf.at[me], send_sem=send_sems.at[d],
                                              recv_sem=recv_sems.at[d], device_id=p, device_id_type=MESH)
            cp.start()
            started.append(cp)
        for d, p in enumerate(peers):
            slot = buf.at[4 * p[0] + 2 * p[1] + p[2]]
            pltpu.make_async_remote_copy(src_ref=slot, dst_ref=slot, send_sem=send_sems.at[d], recv_sem=recv_sems.at[d],
                                         device_id=p, device_id_type=MESH).wait_recv()
        for cp in started:
            cp.wait_send()
        acc = buf[0]
        for src in range(1, N_DEVICES):
            acc = acc + buf[src]
        o_ref[...] = acc

    vm = pl.BlockSpec(memory_space=pltpu.VMEM)
    return pl.pallas_call(
        body, name=name, in_specs=[vm], out_specs=vm, out_shape=jax.ShapeDtypeStruct((r, LANES), F32),
        scratch_shapes=[pltpu.VMEM((N_DEVICES, r, LANES), F32), pltpu.SemaphoreType.DMA((N_DEVICES - 1,)),
                        pltpu.SemaphoreType.DMA((N_DEVICES - 1,))],
        compiler_params=pltpu.CompilerParams(vmem_limit_bytes=VMEM_LIMIT_BYTES),
    )(flat)


def _bucket_ids(dil):
    rel = (np.arange(BLOCK)[:, None] + BLOCK - np.arange(2 * BLOCK)[None, :]) * dil
    max_exact = N_BUCKETS // 2
    d = np.maximum(rel, 0)
    large = max_exact + (np.log(np.maximum(d, 1).astype(np.float32) / max_exact)
                         / np.float32(np.log(T5_MAX_DIST / max_exact)) * (N_BUCKETS - max_exact)).astype(np.int32)
    large = np.minimum(large, N_BUCKETS - 1)
    return np.where(d < max_exact, d, large).astype(np.int32)


def _block_bias(table, dil):
    onehot = (jnp.asarray(_bucket_ids(dil))[:, :, None] == jnp.arange(N_BUCKETS)[None, None, :]).astype(F32)
    return jnp.einsum("ijb,bh->hij", onehot, table.astype(F32), precision=lax.Precision.HIGHEST)


def _tile_gain(g, n):
    return jnp.tile(g.reshape(1, HEAD_DIM), (1, n))


def _layer_fwd(x, p, cfg):
    h1 = _rmsnorm_fwd(x, p["attn_norm"], "attn_norm_fwd")
    proj = _matmul(h1, p["w_in_t"], "nt", F32, "in_proj", tm=1024, tn=768, tk=2048)
    aq, ak, av, bq, bk, bv, cq, ck, cv = _qk_prep(proj, p["gains"], cfg, "qk_prep")
    oa, lse_a = _banded_fwd(aq, ak, av, p["bias_a"], p["sinks"], cfg.nha, cfg.nkva, WINDOW_A - 1, 1, "swa_fwd")
    ob, tot_b = _sb_fwd(bq, bk, bv, "stickbreak_fwd")
    ocs, lses = [], []
    for (window, dil), bias in zip(DILATED_PAIRS, p["bias_c"]):
        o, l = _banded_fwd(cq, ck, cv, bias, None, cfg.nhc, cfg.nhc, window // dil, dil, f"dilated{dil}_fwd")
        ocs.append(o)
        lses.append(l)
    mix, oc, lse_c = _mix_fwd(oa, ob, ocs, lses, p["mix_gain"], cfg, "mix_fwd")
    xm = _matmul(mix, p["w_out"], "nn", F32, "out_proj", tm=1024, tn=512, tk=2048, residual=x)
    h2 = _rmsnorm_fwd(xm, p["ffn_norm"], "ffn_norm_fwd")
    u = _matmul(h2, p["w_up_t"], "nt", F32, "up_proj", tm=1024, tn=512, tk=2048)
    act = _conv_act_fwd(u, p["conv_w"], p["conv_b"], cfg.f, "conv_act_fwd")
    y = _matmul(act, p["w_down"], "nn", F32, "down_proj", tm=1024, tn=1024, tk=512, residual=xm)
    saved = dict(x=x, h1=h1, proj=proj, q=(aq, ak, av, bq, bk, bv, cq, ck, cv), oa=oa, lse_a=lse_a, ob=ob,
                 tot_b=tot_b, oc=oc, lse_c=lse_c, mix=mix, xm=xm, h2=h2, u=u, act=act)
    return y, saved


def _layer_bwd(dy, sv, p, dbias, cfg):
    aq, ak, av, bq, bk, bv, cq, ck, cv = sv["q"]
    g_down = _matmul(sv["act"], dy, "tn", BF16, "down_proj_dw", tm=1408, tn=2048, tk=512)
    dact = _matmul(dy, p["w_down"], "nt", F32, "down_proj_dx", tm=1024, tn=512, tk=2048)
    dug, duu, dwg, dwu, dbg, dbu = _conv_act_bwd(sv["u"], dact, p["conv_w"], p["conv_b"], cfg.f, "conv_act_bwd")
    du = jnp.concatenate([dug, duu], axis=1)
    g_up_t = _matmul(du, sv["h2"], "tn", BF16, "up_proj_dw", tm=1408, tn=2048, tk=512)
    dh2 = _matmul(du, p["w_up_t"], "nn", F32, "up_proj_dx", tm=1024, tn=2048, tk=512)
    dxm, g_ffn_norm = _rmsnorm_bwd(sv["xm"], p["ffn_norm"], dh2, dy, "ffn_norm_bwd")
    g_out = _matmul(sv["mix"], dxm, "tn", BF16, "out_proj_dw", tm=1024, tn=2048, tk=512)
    dmix = _matmul(dxm, p["w_out"], "nt", F32, "out_proj_dx", tm=1024, tn=512, tk=2048)
    doa, dob, doc, g_mix_gain = _mix_bwd(dmix, sv["oa"], sv["ob"], sv["oc"], p["mix_gain"], cfg, "mix_bwd")
    daq, dak, dav, dbias_a, g_sinks = _banded_bwd(aq, ak, av, sv["oa"], sv["lse_a"], doa, p["bias_a"], p["sinks"],
                                                 dbias[0], cfg.nha, cfg.nkva, WINDOW_A - 1, 1, "swa_bwd")
    dbq, dbk, dbv = _sb_bwd(bq, bk, bv, sv["tot_b"], dob, "stickbreak_bwd")
    dcq, dck, dcv, dbias_c = [], [], [], []
    for idx, ((window, dil), bias) in enumerate(zip(DILATED_PAIRS, p["bias_c"])):
        a, b, c, d, _ = _banded_bwd(cq, ck, cv, sv["oc"], sv["lse_c"], doc, bias, None, dbias[1][idx], cfg.nhc,
                                    cfg.nhc, window // dil, dil, f"dilated{dil}_bwd")
        dcq.append(a)
        dck.append(b)
        dcv.append(c)
        dbias_c.append(d)
    dproj, g_aq, g_ak, g_cq, g_ck = _qk_prep_bwd(
        sv["proj"], p["gains"], [[daq], [dak], [dav], [dbq], [dbk], [dbv], dcq, dck, dcv], cfg, "qk_prep_bwd")
    g_in_t = _matmul(dproj, sv["h1"], "tn", BF16, "in_proj_dw", tm=768, tn=2048, tk=512)
    dh1 = _matmul(dproj, p["w_in_t"], "nn", F32, "in_proj_dx", tm=1024, tn=2048, tk=768)
    dx, g_attn_norm = _rmsnorm_bwd(sv["x"], p["attn_norm"], dh1, dxm, "attn_norm_bwd")

    def fold(g):
        return jnp.sum(g.reshape(-1, HEAD_DIM), axis=0)

    small = dict(attn_norm=g_attn_norm[0], a_q_gain=fold(g_aq), a_k_gain=fold(g_ak), a_sinks=g_sinks,
                 c_q_gain=fold(g_cq), c_k_gain=fold(g_ck), mix_out_gain=g_mix_gain[0], ffn_norm=g_ffn_norm[0],
                 conv_w=jnp.concatenate([dwg, dwu], axis=1), conv_b=jnp.concatenate([dbg, dbu], axis=1)[0])
    big = (g_in_t, g_out, g_up_t, g_down)
    return dx, big, small, (dbias_a, dbias_c)


_SMALL = ("attn_norm", "a_q_gain", "a_k_gain", "a_sinks", "c_q_gain", "c_k_gain", "rel_bias_table", "mix_out_gain",
          "ffn_norm", "conv_w", "conv_b")


def _pack(arrays):
    flat = jnp.concatenate([a.reshape(-1).astype(F32) for a in arrays])
    pad = (-flat.shape[0]) % (8 * LANES)
    return jnp.pad(flat, (0, pad)).reshape(-1, LANES)


def _unpack(flat, shapes):
    flat = flat.reshape(-1)
    out, pos = [], 0
    for sh in shapes:
        n = int(np.prod(sh))
        out.append(flat[pos:pos + n].reshape(sh))
        pos += n
    return out


def kernel(x, attn_norm, w_in, a_q_gain, a_k_gain, a_sinks, c_q_gain, c_k_gain, rel_bias_table, mix_out_gain, w_out, ffn_norm, w_up, conv_w, conv_b, w_down, loss_target, m_attn_norm, m_w_in, m_a_q_gain, m_a_k_gain, m_a_sinks, m_c_q_gain, m_c_k_gain, m_rel_bias_table, m_mix_out_gain, m_w_out, m_ffn_norm, m_w_up, m_conv_w, m_conv_b, m_w_down, v_attn_norm, v_w_in, v_a_q_gain, v_a_k_gain, v_a_sinks, v_c_q_gain, v_c_k_gain, v_rel_bias_table, v_mix_out_gain, v_w_out, v_ffn_norm, v_w_up, v_conv_w, v_conv_b, v_w_down):
    depth, d = attn_norm.shape
    f = w_down.shape[1] * N_CHIPS
    cfg = _Cfg(d, f)
    chip = 2 * lax.axis_index("x") + lax.axis_index("y")

    cw_cols = conv_w.shape[2]
    cw_flat = conv_w.reshape(-1)
    cw_rows = -(-cw_flat.shape[0] // (16 * LANES)) * 16
    cw_pad = jnp.pad(cw_flat, (0, cw_rows * LANES - cw_flat.shape[0])).reshape(cw_rows, LANES)
    (cw_all,) = _gather_rows([cw_pad], "gather_conv_w")
    cw_all = cw_all.reshape(N_CHIPS, -1)[:, :cw_flat.shape[0]].reshape(N_CHIPS, depth, CONV_WIDTH, cw_cols)
    conv_w_full = jnp.transpose(cw_all, (1, 2, 0, 3)).reshape(depth, CONV_WIDTH, N_CHIPS * cw_cols)

    table_a, table_c = rel_bias_table[:, :cfg.nha], rel_bias_table[:, cfg.nha:]
    bias_a = _block_bias(table_a, 1)
    bias_c = [_block_bias(table_c, dil) for _, dil in DILATED_PAIRS]

    layers = []
    for l in range(depth):
        shards = [w_in[l].T.astype(BF16), w_out[l].astype(BF16), w_up[l].T.astype(BF16), w_down[l].astype(BF16)]
        w_in_t, w_out_f, w_up_t, w_down_f = [g.reshape(-1, g.shape[-1]) for g in _gather_rows(shards, "gather_weights")]
        layers.append(dict(
            attn_norm=attn_norm[l].reshape(1, d), ffn_norm=ffn_norm[l].reshape(1, d),
            mix_gain=mix_out_gain[l].reshape(1, d),
            gains=(_tile_gain(a_q_gain[l], cfg.nha), _tile_gain(a_k_gain[l], cfg.nkva),
                   _tile_gain(c_q_gain[l], cfg.nhc), _tile_gain(c_k_gain[l], cfg.nhc)),
            sinks=a_sinks[l], bias_a=bias_a, bias_c=bias_c, w_in_t=w_in_t, w_out=w_out_f, w_up_t=w_up_t,
            w_down=w_down_f, conv_w=conv_w_full[l], conv_b=conv_b[l].reshape(1, 2 * f)))

    act = x[0]
    saved = []
    for l in range(depth):
        act, sv = _layer_fwd(act, layers[l], cfg)
        saved.append(sv)
    dact, loss_blk = _loss_head(act, loss_target[0], "loss_head")
    loss = lax.psum(loss_blk[0, 0], ("x", "y", "c"))

    dbias = (jnp.zeros_like(bias_a), [jnp.zeros_like(b) for b in bias_c])
    big_grads = [None] * depth
    small_grads = [None] * depth
    for l in reversed(range(depth)):
        dact, big, small_grads[l], dbias = _layer_bwd(dact, saved[l], layers[l], dbias, cfg)
        parts = _scatter_parts([g.reshape(N_CHIPS, -1, g.shape[-1]) for g in big], "scatter_grads")
        halves = [_sum_parts(pt, f"sum_grads_{t}") for t, pt in enumerate(parts)]
        big_grads[l] = _swap_halves(halves, "swap_grad_halves")
    grad_x = dact[None]

    tabs = _bias_table_grad([dbias[0]] + dbias[1], [jnp.asarray(_bucket_ids(1))]
                            + [jnp.asarray(_bucket_ids(dil)) for _, dil in DILATED_PAIRS], "bias_table_grad")
    g_table_a = tabs[0][:, :N_BUCKETS].T
    g_table_c = (tabs[1] + tabs[2] + tabs[3])[:, :N_BUCKETS].T
    g_table = jnp.concatenate([g_table_a, g_table_c], axis=1)
    small_local = {k: jnp.stack([small_grads[l][k] for l in range(depth)]) for k in _SMALL if k != "rel_bias_table"}
    small_local["rel_bias_table"] = g_table
    shapes = [small_local[k].shape for k in _SMALL]
    reduced = dict(zip(_SMALL, _unpack(_allreduce_small(_pack([small_local[k] for k in _SMALL]), "allreduce_small"),
                                       shapes)))
    reduced["conv_w"] = lax.dynamic_slice_in_dim(reduced["conv_w"], chip * cw_cols, cw_cols, axis=2)

    given = dict(attn_norm=attn_norm, a_q_gain=a_q_gain, a_k_gain=a_k_gain, a_sinks=a_sinks, c_q_gain=c_q_gain,
                 c_k_gain=c_k_gain, rel_bias_table=rel_bias_table, mix_out_gain=mix_out_gain, ffn_norm=ffn_norm,
                 conv_w=conv_w, conv_b=conv_b)
    moms = dict(attn_norm=(m_attn_norm, v_attn_norm), a_q_gain=(m_a_q_gain, v_a_q_gain),
                a_k_gain=(m_a_k_gain, v_a_k_gain), a_sinks=(m_a_sinks, v_a_sinks), c_q_gain=(m_c_q_gain, v_c_q_gain),
                c_k_gain=(m_c_k_gain, v_c_k_gain), rel_bias_table=(m_rel_bias_table, v_rel_bias_table),
                mix_out_gain=(m_mix_out_gain, v_mix_out_gain), ffn_norm=(m_ffn_norm, v_ffn_norm),
                conv_w=(m_conv_w, v_conv_w), conv_b=(m_conv_b, v_conv_b))
    sshapes = [given[k].shape for k in _SMALL]
    s_delta, s_m, s_v = _adamw(_pack([given[k] for k in _SMALL]), _pack([reduced[k] for k in _SMALL]),
                               _pack([moms[k][0] for k in _SMALL]), _pack([moms[k][1] for k in _SMALL]), "adamw_small")
    grads = dict(reduced)
    deltas = dict(zip(_SMALL, _unpack(s_delta, sshapes)))
    new_m = dict(zip(_SMALL, _unpack(s_m, sshapes)))
    new_v = dict(zip(_SMALL, _unpack(s_v, sshapes)))

    big_given = dict(w_in=(w_in, m_w_in, v_w_in, True), w_out=(w_out, m_w_out, v_w_out, False),
                     w_up=(w_up, m_w_up, v_w_up, True), w_down=(w_down, m_w_down, v_w_down, False))
    for t, name in enumerate(("w_in", "w_out", "w_up", "w_down")):
        wt, mt, vt, transposed = big_given[name]
        bufs = None
        for l in range(depth):
            g = big_grads[l][t].T if transposed else big_grads[l][t]
            bufs = _adamw_layer(l, wt, g, mt, vt, bufs, f"adamw_{name}_{l}")
        grads[name], deltas[name], new_m[name], new_v[name] = bufs

    order = ("attn_norm", "w_in", "a_q_gain", "a_k_gain", "a_sinks", "c_q_gain", "c_k_gain", "rel_bias_table",
             "mix_out_gain", "w_out", "ffn_norm", "w_up", "conv_w", "conv_b", "w_down")
    return (loss, grad_x, *[grads[k] for k in order], *[deltas[k] for k in order], *[new_m[k] for k in order],
            *[new_v[k] for k in order])
```

```python
import numpy as np
import jax
import jax.numpy as jnp
from jax import lax
from jax.experimental import pallas as pl
from jax.experimental.pallas import tpu as pltpu

F32 = jnp.float32
BF16 = jnp.bfloat16
MESH = pl.DeviceIdType.MESH

HEAD_DIM = 64
BLOCK = 128
LANES = 128
EPS = 1e-6
NEG_INF = -1e30
WINDOW_A = 128
DILATED_PAIRS = ((128, 1), (512, 4), (2048, 16))
N_BUCKETS = 32
T5_MAX_DIST = 2048
CONV_WIDTH = 3
ADAM_LR = 0.001
ADAM_B1 = 0.9
ADAM_B2 = 0.999
ADAM_EPS = 1e-08
ADAM_WD = 0.01
ADAM_STEP = 10
N_CHIPS = 4
N_DEVICES = 8
VMEM_LIMIT_BYTES = 48 * 1024 * 1024
QK_SCALE = HEAD_DIM ** -0.5


def _params(sem=None):
    return pltpu.CompilerParams(dimension_semantics=sem, vmem_limit_bytes=VMEM_LIMIT_BYTES)


def _div_tile(n, cap, mult):
    best = None
    for t in range(mult, min(n, cap) + 1, mult):
        if n % t == 0:
            best = t
    return n if best is None else best


def _dot(a, b):
    return lax.dot_general(a, b, (((1,), (0,)), ((), ())), preferred_element_type=F32)


def _dot_nt(a, b):
    return lax.dot_general(a, b, (((1,), (1,)), ((), ())), preferred_element_type=F32)


def _dot_tn(a, b):
    return lax.dot_general(a, b, (((0,), (0,)), ((), ())), preferred_element_type=F32)


def _split_dot(x, m):
    hi = x.astype(BF16)
    lo = (x - hi.astype(F32)).astype(BF16)
    return _dot(hi, m) + _dot(lo, m)


class _Cfg:
    def __init__(self, d_model, d_ff):
        nh = d_model // HEAD_DIM
        self.d = d_model
        self.f = d_ff
        self.nha = nh // 4
        self.nkva = self.nha // 4
        self.nhb = nh // 4
        self.nhc = nh // 2
        self.a_q = self.nha * HEAD_DIM
        self.a_kv = self.nkva * HEAD_DIM
        self.b_w = self.nhb * HEAD_DIM
        self.c_w = self.nhc * HEAD_DIM
        sizes = [self.a_q, self.a_kv, self.a_kv, self.b_w, self.b_w, self.b_w, self.c_w, self.c_w, self.c_w]
        starts = [0] + [int(s) for s in np.cumsum(sizes)[:-1]]
        self.sections = list(zip(starts, sizes))
        self.in_width = int(sum(sizes))
        assert all(s % LANES == 0 for s in sizes)


def _matmul(a, b, mode, out_dtype, name, tm=512, tn=512, tk=512, residual=None):
    if mode == "tn":
        kdim, m = a.shape
    else:
        m, kdim = a.shape
    n = b.shape[0] if mode == "nt" else b.shape[1]
    tm, tn, tk = _div_tile(m, tm, LANES), _div_tile(n, tn, LANES), _div_tile(kdim, tk, LANES)
    nk = kdim // tk
    if mode == "tn":
        a_spec = pl.BlockSpec((tk, tm), lambda i, j, k: (k, i))
    else:
        a_spec = pl.BlockSpec((tm, tk), lambda i, j, k: (i, k))
    if mode == "nt":
        b_spec = pl.BlockSpec((tn, tk), lambda i, j, k: (j, k))
    else:
        b_spec = pl.BlockSpec((tk, tn), lambda i, j, k: (k, j))
    dot = {"nn": _dot, "nt": _dot_nt, "tn": _dot_tn}[mode]
    o_spec = pl.BlockSpec((tm, tn), lambda i, j, k: (i, j))
    in_specs = [a_spec, b_spec]
    args = [a, b]
    if residual is not None:
        in_specs.append(o_spec)
        args.append(residual)

    def body(*refs):
        if residual is None:
            a_ref, b_ref, o_ref, acc = refs
        else:
            a_ref, b_ref, r_ref, o_ref, acc = refs
        k = pl.program_id(2)

        @pl.when(k == 0)
        def _():
            acc[...] = jnp.zeros_like(acc)

        acc[...] += dot(a_ref[...].astype(BF16), b_ref[...].astype(BF16))

        @pl.when(k == nk - 1)
        def _():
            r = acc[...]
            if residual is not None:
                r = r + r_ref[...]
            o_ref[...] = r.astype(out_dtype)

    return pl.pallas_call(
        body, name=name, grid=(m // tm, n // tn, nk), in_specs=in_specs, out_specs=o_spec,
        out_shape=jax.ShapeDtypeStruct((m, n), out_dtype), scratch_shapes=[pltpu.VMEM((tm, tn), F32)],
        compiler_params=_params(("parallel", "parallel", "arbitrary")),
    )(*args)


def _rmsnorm_fwd(x, g, name):
    s, d = x.shape
    ts = _div_tile(s, 256, 8)

    def body(x_ref, g_ref, o_ref):
        xv = x_ref[...]
        r = lax.rsqrt(jnp.mean(xv * xv, axis=-1, keepdims=True) + EPS)
        o_ref[...] = (xv * r * g_ref[...]).astype(BF16)

    return pl.pallas_call(
        body, name=name, grid=(s // ts,),
        in_specs=[pl.BlockSpec((ts, d), lambda i: (i, 0)), pl.BlockSpec((1, d), lambda i: (0, 0))],
        out_specs=pl.BlockSpec((ts, d), lambda i: (i, 0)), out_shape=jax.ShapeDtypeStruct((s, d), BF16),
        compiler_params=_params(("parallel",)),
    )(x, g)


def _rmsnorm_bwd(x, g, dh, dres, name):
    s, d = x.shape
    ts = _div_tile(s, 256, 8)

    def body(x_ref, g_ref, dh_ref, dres_ref, dx_ref, dg_ref):
        @pl.when(pl.program_id(0) == 0)
        def _():
            dg_ref[...] = jnp.zeros_like(dg_ref)

        xv = x_ref[...]
        r = lax.rsqrt(jnp.mean(xv * xv, axis=-1, keepdims=True) + EPS)
        xhat = xv * r
        dhv = dh_ref[...]
        dxhat = dhv * g_ref[...]
        dx_ref[...] = dres_ref[...] + r * (dxhat - xhat * jnp.mean(dxhat * xhat, axis=-1, keepdims=True))
        dg_ref[...] += jnp.sum(dhv * xhat, axis=0, keepdims=True)

    row = pl.BlockSpec((ts, d), lambda i: (i, 0))
    vec = pl.BlockSpec((1, d), lambda i: (0, 0))
    return pl.pallas_call(
        body, name=name, grid=(s // ts,), in_specs=[row, vec, row, row], out_specs=[row, vec],
        out_shape=[jax.ShapeDtypeStruct((s, d), F32), jax.ShapeDtypeStruct((1, d), F32)],
        compiler_params=_params(("arbitrary",)),
    )(x, g, dh, dres)


def _head_mean_matrix():
    idx = np.arange(LANES) // HEAD_DIM
    return jnp.asarray((idx[:, None] == idx[None, :]).astype(np.float32) / HEAD_DIM, dtype=BF16)


def _head_mean(y, m128):
    w = y.shape[1]
    parts = [_split_dot(y[:, c * LANES:(c + 1) * LANES], m128) for c in range(w // LANES)]
    return parts[0] if len(parts) == 1 else jnp.concatenate(parts, axis=1)


_NORMED_SECTIONS = (0, 1, 6, 7)


def _qk_prep(proj, gains, cfg, name):
    s = proj.shape[0]
    ts = _div_tile(s, 256, 16)
    m128 = _head_mean_matrix()

    def body(p_ref, m_ref, g0, g1, g6, g7, *outs):
        gref = dict(zip(_NORMED_SECTIONS, (g0, g1, g6, g7)))
        for idx, (st, w) in enumerate(cfg.sections):
            xv = p_ref[:, st:st + w]
            if idx in gref:
                r = lax.rsqrt(_head_mean(xv * xv, m_ref[...]) + EPS)
                xv = xv * r * gref[idx][...]
            outs[idx][...] = xv.astype(BF16)

    in_specs = [pl.BlockSpec((ts, cfg.in_width), lambda i: (i, 0)), pl.BlockSpec((LANES, LANES), lambda i: (0, 0))]
    in_specs += [pl.BlockSpec((1, cfg.sections[k][1]), lambda i: (0, 0)) for k in _NORMED_SECTIONS]
    out_specs = [pl.BlockSpec((ts, w), lambda i: (i, 0)) for _, w in cfg.sections]
    out_shape = [jax.ShapeDtypeStruct((s, w), BF16) for _, w in cfg.sections]
    return pl.pallas_call(
        body, name=name, grid=(s // ts,), in_specs=in_specs, out_specs=out_specs, out_shape=out_shape,
        compiler_params=_params(("parallel",)),
    )(proj, m128, *gains)


def _qk_prep_bwd(proj, gains, grads, cfg, name):
    s = proj.shape[0]
    ts = _div_tile(s, 128, 16)
    m128 = _head_mean_matrix()
    counts = [len(gl) for gl in grads]
    flat = [g for gl in grads for g in gl]

    def body(*refs):
        p_ref, m_ref = refs[0], refs[1]
        gref = dict(zip(_NORMED_SECTIONS, refs[2:6]))
        g_in = refs[6:6 + len(flat)]
        dp_ref = refs[6 + len(flat)]
        dgain = dict(zip(_NORMED_SECTIONS, refs[7 + len(flat):]))

        @pl.when(pl.program_id(0) == 0)
        def _():
            for k in _NORMED_SECTIONS:
                dgain[k][...] = jnp.zeros_like(dgain[k])

        pos = 0
        for idx, (st, w) in enumerate(cfg.sections):
            dy = g_in[pos][...]
            for extra in g_in[pos + 1:pos + counts[idx]]:
                dy = dy + extra[...]
            pos += counts[idx]
            if idx in gref:
                xv = p_ref[:, st:st + w]
                r = lax.rsqrt(_head_mean(xv * xv, m_ref[...]) + EPS)
                xhat = xv * r
                dxhat = dy * gref[idx][...]
                dgain[idx][...] += jnp.sum(dy * xhat, axis=0, keepdims=True)
                dy = r * (dxhat - xhat * _head_mean(dxhat * xhat, m_ref[...]))
            dp_ref[:, st:st + w] = dy.astype(BF16)

    in_specs = [pl.BlockSpec((ts, cfg.in_width), lambda i: (i, 0)), pl.BlockSpec((LANES, LANES), lambda i: (0, 0))]
    in_specs += [pl.BlockSpec((1, cfg.sections[k][1]), lambda i: (0, 0)) for k in _NORMED_SECTIONS]
    for idx, (_, w) in enumerate(cfg.sections):
        in_specs += [pl.BlockSpec((ts, w), lambda i: (i, 0))] * counts[idx]
    out_specs = [pl.BlockSpec((ts, cfg.in_width), lambda i: (i, 0))]
    out_specs += [pl.BlockSpec((1, cfg.sections[k][1]), lambda i: (0, 0)) for k in _NORMED_SECTIONS]
    out_shape = [jax.ShapeDtypeStruct((s, cfg.in_width), BF16)]
    out_shape += [jax.ShapeDtypeStruct((1, cfg.sections[k][1]), F32) for k in _NORMED_SECTIONS]
    return pl.pallas_call(
        body, name=name, grid=(s // ts,), in_specs=in_specs, out_specs=out_specs, out_shape=out_shape,
        compiler_params=_params(("arbitrary",)),
    )(proj, m128, *gains, *flat)


def _band_masks(max_dist):
    row = lax.broadcasted_iota(jnp.int32, (BLOCK, BLOCK), 0)
    col = lax.broadcasted_iota(jnp.int32, (BLOCK, BLOCK), 1)
    return row + BLOCK - col <= max_dist, col <= row


def _banded_fwd(q, k, v, bias, sinks, hq, hk, max_dist, dil, name):
    s = q.shape[0]
    wq, wk, sd, grp = hq * HEAD_DIM, hk * HEAD_DIM, s // dil, hq // hk
    nb = sd // BLOCK
    has_sink = sinks is not None

    def body(*refs):
        if has_sink:
            q_ref, kp_ref, kc_ref, vp_ref, vc_ref, b_ref, s_ref, o_ref, l_ref = refs
        else:
            q_ref, kp_ref, kc_ref, vp_ref, vc_ref, b_ref, o_ref, l_ref = refs
        i = pl.program_id(1)
        mprev, mcur = _band_masks(max_dist)
        mprev = jnp.logical_and(mprev, i > 0)
        for h in range(hq):
            sq = slice(h * HEAD_DIM, (h + 1) * HEAD_DIM)
            sk = slice((h // grp) * HEAD_DIM, (h // grp + 1) * HEAD_DIM)
            qh = q_ref[:, sq]
            sp = jnp.where(mprev, _dot_nt(qh, kp_ref[:, sk]) * QK_SCALE + b_ref[h, :, 0:BLOCK], NEG_INF)
            sc = jnp.where(mcur, _dot_nt(qh, kc_ref[:, sk]) * QK_SCALE + b_ref[h, :, BLOCK:2 * BLOCK], NEG_INF)
            m = jnp.maximum(jnp.max(sp, axis=-1, keepdims=True), jnp.max(sc, axis=-1, keepdims=True))
            if has_sink:
                m = jnp.maximum(m, s_ref[h])
            pp = jnp.exp(sp - m)
            pc = jnp.exp(sc - m)
            den = jnp.sum(pp, axis=-1, keepdims=True) + jnp.sum(pc, axis=-1, keepdims=True)
            if has_sink:
                den = den + jnp.exp(s_ref[h] - m)
            acc = _dot(pp.astype(BF16), vp_ref[:, sk]) + _dot(pc.astype(BF16), vc_ref[:, sk])
            o_ref[:, sq] = acc / den
            l_ref[:, sq] = jnp.broadcast_to(m + jnp.log(den), (BLOCK, HEAD_DIM))

    qspec = pl.BlockSpec((BLOCK, wq), lambda r, i: (i, r))
    kprev = pl.BlockSpec((BLOCK, wk), lambda r, i: (jnp.maximum(i - 1, 0), r))
    kcur = pl.BlockSpec((BLOCK, wk), lambda r, i: (i, r))
    in_specs = [qspec, kprev, kcur, kprev, kcur, pl.BlockSpec((hq, BLOCK, 2 * BLOCK), lambda r, i: (0, 0, 0))]
    args = [q.reshape(sd, dil * wq), k.reshape(sd, dil * wk), k.reshape(sd, dil * wk),
            v.reshape(sd, dil * wk), v.reshape(sd, dil * wk), bias]
    if has_sink:
        in_specs.append(pl.BlockSpec(memory_space=pltpu.SMEM))
        args.append(sinks)
    out, lse = pl.pallas_call(
        body, name=name, grid=(dil, nb), in_specs=in_specs, out_specs=[qspec, qspec],
        out_shape=[jax.ShapeDtypeStruct((sd, dil * wq), F32)] * 2,
        compiler_params=_params(("parallel", "parallel")),
    )(*args)
    return out.reshape(s, wq), lse.reshape(s, wq)


def _banded_bwd(q, k, v, o, lse, do, bias, sinks, dbias_init, hq, hk, max_dist, dil, name):
    s = q.shape[0]
    wq, wk, sd, grp = hq * HEAD_DIM, hk * HEAD_DIM, s // dil, hq // hk
    nb = sd // BLOCK
    has_sink = sinks is not None

    def body(*refs):
        (q_ref, qn_ref, kp_ref, kc_ref, vp_ref, vc_ref, o_ref, on_ref, l_ref, ln_ref, do_ref, don_ref,
         b_ref, dbi_ref) = refs[:14]
        rest = refs[14:]
        if has_sink:
            s_ref, dq_ref, dk_ref, dv_ref, db_ref, ds_ref = rest
        else:
            dq_ref, dk_ref, dv_ref, db_ref = rest
        j = pl.program_id(1)

        @pl.when(jnp.logical_and(pl.program_id(0) == 0, j == 0))
        def _():
            db_ref[...] = dbi_ref[...]
            if has_sink:
                ds_ref[...] = jnp.zeros_like(ds_ref)

        mprev_static, mcur = _band_masks(max_dist)
        mprev = jnp.logical_and(mprev_static, j > 0)
        mnext = jnp.logical_and(mprev_static, j + 1 < nb)
        dk_acc = [jnp.zeros((BLOCK, HEAD_DIM), F32) for _ in range(hk)]
        dv_acc = [jnp.zeros((BLOCK, HEAD_DIM), F32) for _ in range(hk)]
        for h in range(hq):
            g = h // grp
            sq = slice(h * HEAD_DIM, (h + 1) * HEAD_DIM)
            sk = slice(g * HEAD_DIM, (g + 1) * HEAD_DIM)
            kp, kc, vp, vc = kp_ref[:, sk], kc_ref[:, sk], vp_ref[:, sk], vc_ref[:, sk]
            qh = q_ref[:, sq]
            doh = do_ref[:, sq]
            dohb = doh.astype(BF16)
            lcol = l_ref[:, h * HEAD_DIM:h * HEAD_DIM + 1]
            dcol = jnp.sum(doh * o_ref[:, sq], axis=-1, keepdims=True)
            sp = _dot_nt(qh, kp) * QK_SCALE + b_ref[h, :, 0:BLOCK]
            sc = _dot_nt(qh, kc) * QK_SCALE + b_ref[h, :, BLOCK:2 * BLOCK]
            pp = jnp.where(mprev, jnp.exp(sp - lcol), 0.0)
            pc = jnp.where(mcur, jnp.exp(sc - lcol), 0.0)
            dsp = pp * (_dot_nt(dohb, vp) - dcol)
            dsc = pc * (_dot_nt(dohb, vc) - dcol)
            dspb, dscb = dsp.astype(BF16), dsc.astype(BF16)
            dq_ref[:, sq] = (_dot(dspb, kp) + _dot(dscb, kc)) * QK_SCALE
            db_ref[h, :, 0:BLOCK] += dsp
            db_ref[h, :, BLOCK:2 * BLOCK] += dsc
            if has_sink:
                psink = jnp.exp(s_ref[h] - lcol)
                tot = jnp.sum(psink * dcol, axis=0, keepdims=True)
                ds_ref[h:h + 1, :] -= jnp.broadcast_to(tot, (1, LANES))
            qn = qn_ref[:, sq]
            don = don_ref[:, sq]
            donb = don.astype(BF16)
            lncol = ln_ref[:, h * HEAD_DIM:h * HEAD_DIM + 1]
            dncol = jnp.sum(don * on_ref[:, sq], axis=-1, keepdims=True)
            sn = _dot_nt(qn, kc) * QK_SCALE + b_ref[h, :, 0:BLOCK]
            pn = jnp.where(mnext, jnp.exp(sn - lncol), 0.0)
            dsn = pn * (_dot_nt(donb, vc) - dncol)
            dk_acc[g] = dk_acc[g] + (_dot_tn(dscb, qh) + _dot_tn(dsn.astype(BF16), qn)) * QK_SCALE
            dv_acc[g] = dv_acc[g] + _dot_tn(pc.astype(BF16), dohb) + _dot_tn(pn.astype(BF16), donb)
        for g in range(hk):
            sk = slice(g * HEAD_DIM, (g + 1) * HEAD_DIM)
            dk_ref[:, sk] = dk_acc[g]
            dv_ref[:, sk] = dv_acc[g]

    qcur = pl.BlockSpec((BLOCK, wq), lambda r, j: (j, r))
    qnext = pl.BlockSpec((BLOCK, wq), lambda r, j: (jnp.minimum(j + 1, nb - 1), r))
    kprev = pl.BlockSpec((BLOCK, wk), lambda r, j: (jnp.maximum(j - 1, 0), r))
    kcur = pl.BlockSpec((BLOCK, wk), lambda r, j: (j, r))
    bspec = pl.BlockSpec((hq, BLOCK, 2 * BLOCK), lambda r, j: (0, 0, 0))
    q2, k2, v2 = q.reshape(sd, dil * wq), k.reshape(sd, dil * wk), v.reshape(sd, dil * wk)
    o2, l2, do2 = o.reshape(sd, dil * wq), lse.reshape(sd, dil * wq), do.reshape(sd, dil * wq)
    in_specs = [qcur, qnext, kprev, kcur, kprev, kcur, qcur, qnext, qcur, qnext, qcur, qnext, bspec, bspec]
    args = [q2, q2, k2, k2, v2, v2, o2, o2, l2, l2, do2, do2, bias, dbias_init]
    out_specs = [qcur, kcur, kcur, bspec]
    out_shape = [jax.ShapeDtypeStruct((sd, dil * wq), F32), jax.ShapeDtypeStruct((sd, dil * wk), F32),
                 jax.ShapeDtypeStruct((sd, dil * wk), F32), jax.ShapeDtypeStruct((hq, BLOCK, 2 * BLOCK), F32)]
    if has_sink:
        in_specs.append(pl.BlockSpec(memory_space=pltpu.SMEM))
        args.append(sinks)
        out_specs.append(pl.BlockSpec((hq, LANES), lambda r, j: (0, 0)))
        out_shape.append(jax.ShapeDtypeStruct((hq, LANES), F32))
    res = pl.pallas_call(
        body, name=name, grid=(dil, nb), in_specs=in_specs, out_specs=out_specs, out_shape=out_shape,
        compiler_params=_params(("arbitrary", "arbitrary")),
    )(*args)
    dq, dk, dv, dbias = res[0].reshape(s, wq), res[1].reshape(s, wk), res[2].reshape(s, wk), res[3]
    return dq, dk, dv, dbias, (res[4][:, 0] if has_sink else None)


def _neg_softplus(z):
    return -(jnp.maximum(z, 0.0) + jnp.log(1.0 + jnp.exp(-jnp.abs(z))))


SB_CHUNK = 256
HEADS_PER_PAIR = LANES // HEAD_DIM


def _tri(kind):
    row = lax.broadcasted_iota(jnp.int32, (SB_CHUNK, SB_CHUNK), 0)
    col = lax.broadcasted_iota(jnp.int32, (SB_CHUNK, SB_CHUNK), 1)
    return {"ge": row >= col, "lt": row < col, "le": row <= col}[kind].astype(BF16)


def _keys_on_lanes(a, rows):
    s, w = a.shape
    return jnp.transpose(a.reshape(s // rows, rows, w), (0, 2, 1))


def _sb_diag_mask(i, nfull):
    row = lax.broadcasted_iota(jnp.int32, (BLOCK, SB_CHUNK), 0)
    col = lax.broadcasted_iota(jnp.int32, (BLOCK, SB_CHUNK), 1)
    return col < row + (i * BLOCK - nfull * SB_CHUNK)


def _sb_fwd(q, kt, v, name):
    s, w = q.shape
    npair, nb, nc = w // LANES, s // BLOCK, s // SB_CHUNK

    def body(q_ref, kt_ref, v_ref, o_ref, t_ref):
        i = pl.program_id(1)
        nfull = (i * BLOCK) // SB_CHUNK
        strict = _sb_diag_mask(i, nfull)
        lincl = _tri("ge")
        heads = [slice(hh * HEAD_DIM, (hh + 1) * HEAD_DIM) for hh in range(HEADS_PER_PAIR)]
        qs = [q_ref[:, sl] for sl in heads]

        def chunk(jj, diag, carry):
            rows = pl.ds(pl.multiple_of(jj * SB_CHUNK, SB_CHUNK), SB_CHUNK)
            new = []
            for hh, sl in enumerate(heads):
                o_acc, rem = carry[hh]
                z = _dot(qs[hh], kt_ref[jj, sl, :]) * QK_SCALE
                lr = _neg_softplus(z)
                if diag:
                    lr = jnp.where(strict, lr, 0.0)
                c = rem + _split_dot(lr, lincl)
                a = jnp.exp(z + c)
                if diag:
                    a = jnp.where(strict, a, 0.0)
                new.append((o_acc + _dot(a.astype(BF16), v_ref[rows, sl]),
                            rem + jnp.sum(lr, axis=-1, keepdims=True)))
            return tuple(new)

        init = tuple((jnp.zeros((BLOCK, HEAD_DIM), F32), jnp.zeros((BLOCK, 1), F32)) for _ in heads)
        carry = chunk(nfull, True, init)
        carry = lax.fori_loop(0, nfull, lambda t, cr: chunk(nfull - 1 - t, False, cr), carry)
        for hh, sl in enumerate(heads):
            o_ref[:, sl] = carry[hh][0]
            t_ref[:, sl] = jnp.broadcast_to(carry[hh][1], (BLOCK, HEAD_DIM))

    qspec = pl.BlockSpec((BLOCK, LANES), lambda p, i: (i, p))
    return pl.pallas_call(
        body, name=name, grid=(npair, nb),
        in_specs=[qspec, pl.BlockSpec((nc, LANES, SB_CHUNK), lambda p, i: (0, p, 0)),
                  pl.BlockSpec((s, LANES), lambda p, i: (0, p))],
        out_specs=[qspec, qspec], out_shape=[jax.ShapeDtypeStruct((s, w), F32)] * 2,
        compiler_params=_params(("parallel", "parallel")),
    )(q, kt, v)


def _sb_bwd(q, k, kt, v, tot, do, name):
    s, w = q.shape
    npair, nb, nc = w // LANES, s // BLOCK, s // SB_CHUNK
    dob = do.astype(BF16)

    def body(q_ref, qt_ref, k_ref, kt_ref, vt_ref, t_ref, do_ref, dot_ref, dq_ref, dkt_ref, dvt_ref):
        i = pl.program_id(1)

        @pl.when(i == 0)
        def _():
            dkt_ref[...] = jnp.zeros_like(dkt_ref)
            dvt_ref[...] = jnp.zeros_like(dvt_ref)

        nfull = (i * BLOCK) // SB_CHUNK
        strict = _sb_diag_mask(i, nfull)
        lbefore = _tri("lt")
        lupto = _tri("le")
        heads = [slice(hh * HEAD_DIM, (hh + 1) * HEAD_DIM) for hh in range(HEADS_PER_PAIR)]
        qs = [q_ref[:, sl] for sl in heads]
        qts = [qt_ref[sl, :] for sl in heads]
        dos = [do_ref[:, sl] for sl in heads]
        dots = [dot_ref[sl, :] for sl in heads]
        totals = [t_ref[:, sl.start:sl.start + 1] for sl in heads]

        def chunk(jj, diag, carry):
            rows = pl.ds(pl.multiple_of(jj * SB_CHUNK, SB_CHUNK), SB_CHUNK)
            new = []
            for hh, sl in enumerate(heads):
                dq_acc, plr, pg = carry[hh]
                z = _dot(qs[hh], kt_ref[jj, sl, :]) * QK_SCALE
                lr = _neg_softplus(z)
                if diag:
                    lr = jnp.where(strict, lr, 0.0)
                c = totals[hh] - (plr + _split_dot(lr, lbefore))
                a = jnp.exp(z + c)
                if diag:
                    a = jnp.where(strict, a, 0.0)
                g = _dot(dos[hh], vt_ref[jj, sl, :]) * a
                dlr = pg + _split_dot(g, lupto)
                dz = g - jnp.exp(z + lr) * dlr
                if diag:
                    dz = jnp.where(strict, dz, 0.0)
                dzb = (dz * QK_SCALE).astype(BF16)
                dkt_ref[jj, sl, :] += _dot(qts[hh], dzb)
                dvt_ref[jj, sl, :] += _dot(dots[hh], a.astype(BF16))
                new.append((dq_acc + _dot(dzb, k_ref[rows, sl]), plr + jnp.sum(lr, axis=-1, keepdims=True),
                            pg + jnp.sum(g, axis=-1, keepdims=True)))
            return tuple(new)

        zero = jnp.zeros((BLOCK, 1), F32)
        init = tuple((jnp.zeros((BLOCK, HEAD_DIM), F32), zero, zero) for _ in heads)
        carry = lax.fori_loop(0, nfull, lambda t, cr: chunk(t, False, cr), init)
        carry = chunk(nfull, True, carry)
        for hh, sl in enumerate(heads):
            dq_ref[:, sl] = carry[hh][0]

    qspec = pl.BlockSpec((BLOCK, LANES), lambda p, i: (i, p))
    qtspec = pl.BlockSpec((None, LANES, BLOCK), lambda p, i: (i, p, 0))
    kspec = pl.BlockSpec((s, LANES), lambda p, i: (0, p))
    ktspec = pl.BlockSpec((nc, LANES, SB_CHUNK), lambda p, i: (0, p, 0))
    dq, dkt, dvt = pl.pallas_call(
        body, name=name, grid=(npair, nb), in_specs=[qspec, qtspec, kspec, ktspec, ktspec, qspec, qspec, qtspec],
        out_specs=[qspec, ktspec, ktspec],
        out_shape=[jax.ShapeDtypeStruct((s, w), F32)] + [jax.ShapeDtypeStruct((nc, w, SB_CHUNK), F32)] * 2,
        compiler_params=_params(("parallel", "arbitrary")),
    )(q, _keys_on_lanes(q, BLOCK), k, kt, _keys_on_lanes(v, SB_CHUNK), tot, dob, _keys_on_lanes(dob, BLOCK))

    def rows_first(t):
        return jnp.transpose(t, (0, 2, 1)).reshape(s, w)

    return dq, rows_first(dkt), rows_first(dvt)


def _group_norm(xv, g):
    r = lax.rsqrt(jnp.mean(xv * xv, axis=-1, keepdims=True) + EPS)
    return xv * r * g


def _mix_fwd(oa, ob, ocs, lses, gain, cfg, name):
    s = oa.shape[0]
    ts = _div_tile(s, 256, 16)
    aq, bw, cw = cfg.a_q, cfg.b_w, cfg.c_w

    def body(oa_ref, ob_ref, c1, c2, c3, l1, l2, l3, g_ref, mix_ref, oc_ref, lse_ref):
        m = jnp.maximum(jnp.maximum(l1[...], l2[...]), l3[...])
        e1, e2, e3 = jnp.exp(l1[...] - m), jnp.exp(l2[...] - m), jnp.exp(l3[...] - m)
        den = e1 + e2 + e3
        oc = (e1 * c1[...] + e2 * c2[...] + e3 * c3[...]) / den
        oc_ref[...] = oc
        lse_ref[...] = m + jnp.log(den)
        mix_ref[:, 0:aq] = _group_norm(oa_ref[...], g_ref[:, 0:aq]).astype(BF16)
        mix_ref[:, aq:aq + bw] = _group_norm(ob_ref[...], g_ref[:, aq:aq + bw]).astype(BF16)
        mix_ref[:, aq + bw:] = _group_norm(oc, g_ref[:, aq + bw:]).astype(BF16)

    def row(wd):
        return pl.BlockSpec((ts, wd), lambda i: (i, 0))

    return pl.pallas_call(
        body, name=name, grid=(s // ts,),
        in_specs=[row(aq), row(bw)] + [row(cw)] * 6 + [pl.BlockSpec((1, cfg.d), lambda i: (0, 0))],
        out_specs=[row(cfg.d), row(cw), row(cw)],
        out_shape=[jax.ShapeDtypeStruct((s, cfg.d), BF16), jax.ShapeDtypeStruct((s, cw), F32),
                   jax.ShapeDtypeStruct((s, cw), F32)],
        compiler_params=_params(("parallel",)),
    )(oa, ob, *ocs, *lses, gain)


def _mix_bwd(dmix, oa, ob, oc, gain, cfg, name):
    s = oa.shape[0]
    ts = _div_tile(s, 256, 8)
    aq, bw, cw = cfg.a_q, cfg.b_w, cfg.c_w

    def body(dm_ref, oa_ref, ob_ref, oc_ref, g_ref, da_ref, db_ref, dc_ref, dg_ref):
        @pl.when(pl.program_id(0) == 0)
        def _():
            dg_ref[...] = jnp.zeros_like(dg_ref)

        for x_ref, dx_ref, lo, hi in ((oa_ref, da_ref, 0, aq), (ob_ref, db_ref, aq, aq + bw),
                                      (oc_ref, dc_ref, aq + bw, aq + bw + cw)):
            xv = x_ref[...]
            dy = dm_ref[:, lo:hi]
            r = lax.rsqrt(jnp.mean(xv * xv, axis=-1, keepdims=True) + EPS)
            xhat = xv * r
            dxhat = dy * g_ref[:, lo:hi]
            dx_ref[...] = r * (dxhat - xhat * jnp.mean(dxhat * xhat, axis=-1, keepdims=True))
            dg_ref[:, lo:hi] += jnp.sum(dy * xhat, axis=0, keepdims=True)

    def row(wd):
        return pl.BlockSpec((ts, wd), lambda i: (i, 0))

    vec = pl.BlockSpec((1, cfg.d), lambda i: (0, 0))
    return pl.pallas_call(
        body, name=name, grid=(s // ts,), in_specs=[row(cfg.d), row(aq), row(bw), row(cw), vec],
        out_specs=[row(aq), row(bw), row(cw), vec],
        out_shape=[jax.ShapeDtypeStruct((s, aq), F32), jax.ShapeDtypeStruct((s, bw), F32),
                   jax.ShapeDtypeStruct((s, cw), F32), jax.ShapeDtypeStruct((1, cfg.d), F32)],
        compiler_params=_params(("arbitrary",)),
    )(dmix, oa, ob, oc, gain)


def _bias_table_grad(dbiases, buckets, name):
    outs = []
    for idx, (db, bk) in enumerate(zip(dbiases, buckets)):
        h = db.shape[0]

        def body(db_ref, bk_ref, o_ref):
            xv = db_ref[0]
            ids = bk_ref[...]
            lane = lax.broadcasted_iota(jnp.int32, (1, LANES), 1)
            acc = jnp.zeros((1, LANES), F32)
            for b in range(N_BUCKETS):
                tot = jnp.sum(jnp.where(ids == b, xv, 0.0), axis=0, keepdims=True)
                tot = jnp.sum(tot, axis=1, keepdims=True)
                acc = jnp.where(lane == b, tot, acc)
            o_ref[0] = acc

        outs.append(pl.pallas_call(
            body, name=f"{name}_{idx}", grid=(h,),
            in_specs=[pl.BlockSpec((1, BLOCK, 2 * BLOCK), lambda i: (i, 0, 0)),
                      pl.BlockSpec((BLOCK, 2 * BLOCK), lambda i: (0, 0))],
            out_specs=pl.BlockSpec((1, 1, LANES), lambda i: (i, 0, 0)),
            out_shape=jax.ShapeDtypeStruct((h, 1, LANES), F32), compiler_params=_params(("parallel",)),
        )(db, bk)[:, 0, :])
    return outs


def _shift_down(u, n, rows):
    return jnp.where(rows >= n, pltpu.roll(u, n, 0), 0.0)


def _shift_up(u, n, rows, s):
    return jnp.where(rows < s - n, pltpu.roll(u, s - n, 0), 0.0)


def _conv(u, w_ref, b_ref, rows):
    return (b_ref[...] + w_ref[0:1, :] * _shift_down(u, 2, rows) + w_ref[1:2, :] * _shift_down(u, 1, rows)
            + w_ref[2:3, :] * u)


def _conv_act_fwd(u, conv_w, conv_b, f, name):
    s = u.shape[0]
    nf = f // LANES

    def body(ug_ref, uu_ref, wg_ref, wu_ref, bg_ref, bu_ref, act_ref):
        rows = lax.broadcasted_iota(jnp.int32, (s, LANES), 0)
        gate = _conv(ug_ref[...], wg_ref, bg_ref, rows)
        up = _conv(uu_ref[...], wu_ref, bu_ref, rows)
        act_ref[...] = (gate * jax.nn.sigmoid(gate) * up).astype(BF16)

    def col(rws, off):
        return pl.BlockSpec((rws, LANES), lambda j: (0, j + off))

    return pl.pallas_call(
        body, name=name, grid=(nf,),
        in_specs=[col(s, 0), col(s, nf), col(CONV_WIDTH, 0), col(CONV_WIDTH, nf), col(1, 0), col(1, nf)],
        out_specs=col(s, 0), out_shape=jax.ShapeDtypeStruct((s, f), BF16), compiler_params=_params(("parallel",)),
    )(u, u, conv_w, conv_w, conv_b, conv_b)


def _conv_act_bwd(u, dact, conv_w, conv_b, f, name):
    s = u.shape[0]
    nf = f // LANES

    def body(ug_ref, uu_ref, da_ref, wg_ref, wu_ref, bg_ref, bu_ref, dug_ref, duu_ref, dwg_ref, dwu_ref, dbg_ref,
             dbu_ref):
        rows = lax.broadcasted_iota(jnp.int32, (s, LANES), 0)
        ug, uu = ug_ref[...], uu_ref[...]
        gate = _conv(ug, wg_ref, bg_ref, rows)
        up = _conv(uu, wu_ref, bu_ref, rows)
        sg = jax.nn.sigmoid(gate)
        da = da_ref[...]
        dgate = da * up * (sg * (1.0 + gate * (1.0 - sg)))
        dup = da * (gate * sg)
        for du, uv, w_ref, du_ref, dw_ref, db_ref in ((dgate, ug, wg_ref, dug_ref, dwg_ref, dbg_ref),
                                                     (dup, uu, wu_ref, duu_ref, dwu_ref, dbu_ref)):
            du_ref[...] = (w_ref[2:3, :] * du + w_ref[1:2, :] * _shift_up(du, 1, rows, s)
                           + w_ref[0:1, :] * _shift_up(du, 2, rows, s)).astype(BF16)
            dw_ref[0:1, :] = jnp.sum(du * _shift_down(uv, 2, rows), axis=0, keepdims=True)
            dw_ref[1:2, :] = jnp.sum(du * _shift_down(uv, 1, rows), axis=0, keepdims=True)
            dw_ref[2:3, :] = jnp.sum(du * uv, axis=0, keepdims=True)
            db_ref[...] = jnp.sum(du, axis=0, keepdims=True)

    def col(rws, off):
        return pl.BlockSpec((rws, LANES), lambda j: (0, j + off))

    return pl.pallas_call(
        body, name=name, grid=(nf,),
        in_specs=[col(s, 0), col(s, nf), col(s, 0), col(CONV_WIDTH, 0), col(CONV_WIDTH, nf), col(1, 0), col(1, nf)],
        out_specs=[col(s, 0), col(s, 0), col(CONV_WIDTH, 0), col(CONV_WIDTH, 0), col(1, 0), col(1, 0)],
        out_shape=[jax.ShapeDtypeStruct((s, f), BF16)] * 2 + [jax.ShapeDtypeStruct((CONV_WIDTH, f), F32)] * 2
        + [jax.ShapeDtypeStruct((1, f), F32)] * 2,
        compiler_params=_params(("parallel",)),
    )(u, u, dact, conv_w, conv_w, conv_b, conv_b)


def _loss_head(y, target, name):
    s, d = y.shape
    ts = _div_tile(s, 256, 8)

    def body(y_ref, t_ref, dy_ref, l_ref):
        @pl.when(pl.program_id(0) == 0)
        def _():
            l_ref[...] = jnp.zeros_like(l_ref)

        err = y_ref[...] - t_ref[...]
        dy_ref[...] = err * (1.0 / d)
        tot = jnp.sum(jnp.sum(err * err, axis=0, keepdims=True), axis=1, keepdims=True) * (0.5 / d)
        l_ref[...] += jnp.broadcast_to(tot, l_ref.shape)

    row = pl.BlockSpec((ts, d), lambda i: (i, 0))
    return pl.pallas_call(
        body, name=name, grid=(s // ts,), in_specs=[row, row],
        out_specs=[row, pl.BlockSpec((8, LANES), lambda i: (0, 0))],
        out_shape=[jax.ShapeDtypeStruct((s, d), F32), jax.ShapeDtypeStruct((8, LANES), F32)],
        compiler_params=_params(("arbitrary",)),
    )(y, target)


def _adamw(w, g, m, v, name):
    r, c = w.shape
    tr = _div_tile(r, max(8, (1 << 18) // c // 8 * 8), 8)
    c1 = 1.0 - ADAM_B1 ** ADAM_STEP
    c2 = 1.0 - ADAM_B2 ** ADAM_STEP

    def body(w_ref, g_ref, m_ref, v_ref, d_ref, nm_ref, nv_ref):
        gv = g_ref[...]
        nm = ADAM_B1 * m_ref[...] + (1.0 - ADAM_B1) * gv
        nv = ADAM_B2 * v_ref[...] + (1.0 - ADAM_B2) * (gv * gv)
        d_ref[...] = -ADAM_LR * ((nm / c1) / (jnp.sqrt(nv / c2) + ADAM_EPS) + ADAM_WD * w_ref[...])
        nm_ref[...] = nm
        nv_ref[...] = nv

    spec = pl.BlockSpec((tr, c), lambda i: (i, 0))
    return pl.pallas_call(
        body, name=name, grid=(r // tr,), in_specs=[spec] * 4, out_specs=[spec] * 3,
        out_shape=[jax.ShapeDtypeStruct((r, c), F32)] * 3, compiler_params=_params(("parallel",)),
    )(w, g, m, v)


def _adamw_layer(layer, w, g, m, v, bufs, name):
    depth, r, c = w.shape
    tr = _div_tile(r, max(8, (1 << 17) // c // 8 * 8), 8)
    c1 = 1.0 - ADAM_B1 ** ADAM_STEP
    c2 = 1.0 - ADAM_B2 ** ADAM_STEP

    def body(*refs):
        w_ref, g_ref, m_ref, v_ref = refs[:4]
        go_ref, d_ref, nm_ref, nv_ref = refs[-4:]
        gv = g_ref[...]
        nm = ADAM_B1 * m_ref[...] + (1.0 - ADAM_B1) * gv
        nv = ADAM_B2 * v_ref[...] + (1.0 - ADAM_B2) * (gv * gv)
        d_ref[...] = -ADAM_LR * ((nm / c1) / (jnp.sqrt(nv / c2) + ADAM_EPS) + ADAM_WD * w_ref[...])
        nm_ref[...] = nm
        nv_ref[...] = nv
        go_ref[...] = gv

    lay = pl.BlockSpec((None, tr, c), lambda i: (layer, i, 0))
    in_specs = [lay, pl.BlockSpec((tr, c), lambda i: (i, 0)), lay, lay]
    args = [w, g, m, v]
    aliases = {}
    if bufs is not None:
        in_specs += [pl.BlockSpec(memory_space=pl.ANY)] * 4
        args += list(bufs)
        aliases = {4 + k: k for k in range(4)}
    return pl.pallas_call(
        body, name=name, grid=(r // tr,), in_specs=in_specs, out_specs=[lay] * 4,
        out_shape=[jax.ShapeDtypeStruct((depth, r, c), F32)] * 4, input_output_aliases=aliases,
        compiler_params=_params(("parallel",)),
    )(*args)


def _mesh_pos():
    return lax.axis_index("x"), lax.axis_index("y"), lax.axis_index("c")


def _flip(v, bit):
    return 1 - v if bit else v


def _gather_rows(shards, name):
    n = len(shards)
    any_spec = pl.BlockSpec(memory_space=pl.ANY)

    def body(*refs):
        ins, outs = refs[:n], refs[n:2 * n]
        send_sems, recv_sems, local_sems = refs[2 * n:]
        x, y, c = _mesh_pos()
        others = [(1 - x, y), (x, 1 - y), (1 - x, 1 - y)]
        sibling = (x, y, 1 - c)

        def part(t, chip, core):
            half = ins[t].shape[0] // 2
            return outs[t].at[chip[0] * 2 + chip[1], pl.ds(core * half, half), :]

        def copy(t, sem, src, dst, to):
            return pltpu.make_async_remote_copy(src_ref=src, dst_ref=dst, send_sem=send_sems.at[t, sem],
                                                recv_sem=recv_sems.at[t, sem], device_id=to, device_id_type=MESH)

        local = [pltpu.make_async_copy(ins[t], outs[t].at[2 * x + y], local_sems.at[t]) for t in range(n)]
        for cp in local:
            cp.start()
        started = []
        for t in range(n):
            half = ins[t].shape[0] // 2
            for j, chip in enumerate(others):
                cp = copy(t, j, ins[t].at[pl.ds(c * half, half), :], part(t, (x, y), c), (*chip, c))
                cp.start()
                started.append(cp)
        for t in range(n):
            for j, chip in enumerate(others):
                copy(t, j, part(t, chip, c), part(t, chip, c), (*chip, c)).wait_recv()
                cp = copy(t, 3 + j, part(t, chip, c), part(t, chip, c), sibling)
                cp.start()
                started.append(cp)
        for t in range(n):
            for j, chip in enumerate(others):
                copy(t, 3 + j, part(t, chip, 1 - c), part(t, chip, 1 - c), sibling).wait_recv()
        for cp in started:
            cp.wait_send()
        for cp in local:
            cp.wait()

    return pl.pallas_call(
        body, name=name, in_specs=[any_spec] * n, out_specs=[any_spec] * n,
        out_shape=[jax.ShapeDtypeStruct((N_CHIPS,) + sh.shape, sh.dtype) for sh in shards],
        scratch_shapes=[pltpu.SemaphoreType.DMA((n, 6)), pltpu.SemaphoreType.DMA((n, 6)),
                        pltpu.SemaphoreType.DMA((n,))],
    )(*shards)


def _scatter_parts(grads, name):
    n = len(grads)
    any_spec = pl.BlockSpec(memory_space=pl.ANY)

    def body(*refs):
        ins, outs = refs[:n], refs[n:2 * n]
        send_sems, recv_sems, local_sems = refs[2 * n:]
        x, y, c = _mesh_pos()
        me = 4 * x + 2 * y + c
        peers = []
        for d in range(1, N_DEVICES):
            peers.append((_flip(x, d & 4), _flip(y, d & 2), _flip(c, d & 1)))

        def mine_of(t, p):
            half = ins[t].shape[1] // 2
            return ins[t].at[2 * p[0] + p[1], pl.ds(p[2] * half, half), :]

        local = [pltpu.make_async_copy(mine_of(t, (x, y, c)), outs[t].at[me], local_sems.at[t]) for t in range(n)]
        for cp in local:
            cp.start()
        started = []
        for t in range(n):
            for d, p in enumerate(peers):
                cp = pltpu.make_async_remote_copy(src_ref=mine_of(t, p), dst_ref=outs[t].at[me],
                                                  send_sem=send_sems.at[t, d], recv_sem=recv_sems.at[t, d],
                                                  device_id=p, device_id_type=MESH)
                cp.start()
                started.append(cp)
        for t in range(n):
            for d, p in enumerate(peers):
                slot = outs[t].at[4 * p[0] + 2 * p[1] + p[2]]
                pltpu.make_async_remote_copy(src_ref=slot, dst_ref=slot, send_sem=send_sems.at[t, d],
                                             recv_sem=recv_sems.at[t, d], device_id=p,
                                             device_id_type=MESH).wait_recv()
        for cp in started:
            cp.wait_send()
        for cp in local:
            cp.wait()

    return pl.pallas_call(
        body, name=name, in_specs=[any_spec] * n, out_specs=[any_spec] * n,
        out_shape=[jax.ShapeDtypeStruct((N_DEVICES, g.shape[1] // 2, g.shape[2]), g.dtype) for g in grads],
        scratch_shapes=[pltpu.SemaphoreType.DMA((n, N_DEVICES - 1)), pltpu.SemaphoreType.DMA((n, N_DEVICES - 1)),
                        pltpu.SemaphoreType.DMA((n,))],
    )(*grads)


def _sum_parts(parts, name):
    _, r, c = parts.shape
    tr = _div_tile(r, 256, 16)

    def body(p_ref, o_ref):
        acc = p_ref[0].astype(F32)
        for src in range(1, N_DEVICES):
            acc = acc + p_ref[src].astype(F32)
        o_ref[...] = acc

    return pl.pallas_call(
        body, name=name, grid=(r // tr,), in_specs=[pl.BlockSpec((N_DEVICES, tr, c), lambda i: (0, i, 0))],
        out_specs=pl.BlockSpec((tr, c), lambda i: (i, 0)), out_shape=jax.ShapeDtypeStruct((r, c), F32),
        compiler_params=_params(("parallel",)),
    )(parts)


def _swap_halves(halves, name):
    n = len(halves)
    any_spec = pl.BlockSpec(memory_space=pl.ANY)

    def body(*refs):
        ins, outs = refs[:n], refs[n:2 * n]
        send_sems, recv_sems, local_sems = refs[2 * n:]
        x, y, c = _mesh_pos()
        work = []
        for t in range(n):
            h = ins[t].shape[0]
            mine, theirs = outs[t].at[pl.ds(c * h, h), :], outs[t].at[pl.ds((1 - c) * h, h), :]
            loc = pltpu.make_async_copy(ins[t], mine, local_sems.at[t])
            snd = pltpu.make_async_remote_copy(src_ref=ins[t], dst_ref=mine, send_sem=send_sems.at[t],
                                               recv_sem=recv_sems.at[t], device_id=(x, y, 1 - c), device_id_type=MESH)
            rcv = pltpu.make_async_remote_copy(src_ref=theirs, dst_ref=theirs, send_sem=send_sems.at[t],
                                               recv_sem=recv_sems.at[t], device_id=(x, y, 1 - c), device_id_type=MESH)
            loc.start()
            snd.start()
            work.append((loc, snd, rcv))
        for loc, snd, rcv in work:
            rcv.wait_recv()
            snd.wait_send()
            loc.wait()

    return pl.pallas_call(
        body, name=name, in_specs=[any_spec] * n, out_specs=[any_spec] * n,
        out_shape=[jax.ShapeDtypeStruct((2 * h.shape[0], h.shape[1]), h.dtype) for h in halves],
        scratch_shapes=[pltpu.SemaphoreType.DMA((n,)), pltpu.SemaphoreType.DMA((n,)), pltpu.SemaphoreType.DMA((n,))],
    )(*halves)


def _allreduce_small(flat, name):
    r = flat.shape[0]

    def body(x_ref, o_ref, buf, send_sems, recv_sems):
        x, y, c = _mesh_pos()
        me = 4 * x + 2 * y + c
        buf[me] = x_ref[...]
        started = []
        peers = [(_flip(x, d & 4), _flip(y, d & 2), _flip(c, d & 1)) for d in range(1, N_DEVICES)]
        for d, p in enumerate(peers):
            cp = pltpu.make_async_remote_copy(src_ref=x_ref, dst_ref=buf.at[me], send_sem=send_sems.at[d],
                                              recv_sem=recv_sems.at[d], device_id=p, device_id_type=MESH)
            cp.start()
            started.append(cp)
        for d, p in enumerate(peers):
            slot = buf.at[4 * p[0] + 2 * p[1] + p[2]]
            pltpu.make_async_remote_copy(src_ref=slot, dst_ref=slot, send_sem=send_sems.at[d], recv_sem=recv_sems.at[d],
                                         device_id=p, device_id_type=MESH).wait_recv()
        for cp in started:
            cp.wait_send()
        acc = buf[0]
        for src in range(1, N_DEVICES):
            acc = acc + buf[src]
        o_ref[...] = acc

    vm = pl.BlockSpec(memory_space=pltpu.VMEM)
    return pl.pallas_call(
        body, name=name, in_specs=[vm], out_specs=vm, out_shape=jax.ShapeDtypeStruct((r, LANES), F32),
        scratch_shapes=[pltpu.VMEM((N_DEVICES, r, LANES), F32), pltpu.SemaphoreType.DMA((N_DEVICES - 1,)),
                        pltpu.SemaphoreType.DMA((N_DEVICES - 1,))],
        compiler_params=pltpu.CompilerParams(vmem_limit_bytes=VMEM_LIMIT_BYTES),
    )(flat)


def _bucket_ids(dil):
    rel = (np.arange(BLOCK)[:, None] + BLOCK - np.arange(2 * BLOCK)[None, :]) * dil
    max_exact = N_BUCKETS // 2
    d = np.maximum(rel, 0)
    large = max_exact + (np.log(np.maximum(d, 1).astype(np.float32) / max_exact)
                         / np.float32(np.log(T5_MAX_DIST / max_exact)) * (N_BUCKETS - max_exact)).astype(np.int32)
    large = np.minimum(large, N_BUCKETS - 1)
    return np.where(d < max_exact, d, large).astype(np.int32)


def _block_bias(table, dil):
    onehot = (jnp.asarray(_bucket_ids(dil))[:, :, None] == jnp.arange(N_BUCKETS)[None, None, :]).astype(F32)
    return jnp.einsum("ijb,bh->hij", onehot, table.astype(F32), precision=lax.Precision.HIGHEST)


def _tile_gain(g, n):
    return jnp.tile(g.reshape(1, HEAD_DIM), (1, n))


def _layer_fwd(x, p, cfg):
    h1 = _rmsnorm_fwd(x, p["attn_norm"], "attn_norm_fwd")
    proj = _matmul(h1, p["w_in_t"], "nt", F32, "in_proj", tm=1024, tn=768, tk=2048)
    aq, ak, av, bq, bk, bv, cq, ck, cv = _qk_prep(proj, p["gains"], cfg, "qk_prep")
    oa, lse_a = _banded_fwd(aq, ak, av, p["bias_a"], p["sinks"], cfg.nha, cfg.nkva, WINDOW_A - 1, 1, "swa_fwd")
    bkt = _keys_on_lanes(bk, SB_CHUNK)
    ob, tot_b = _sb_fwd(bq, bkt, bv, "stickbreak_fwd")
    ocs, lses = [], []
    for (window, dil), bias in zip(DILATED_PAIRS, p["bias_c"]):
        o, l = _banded_fwd(cq, ck, cv, bias, None, cfg.nhc, cfg.nhc, window // dil, dil, f"dilated{dil}_fwd")
        ocs.append(o)
        lses.append(l)
    mix, oc, lse_c = _mix_fwd(oa, ob, ocs, lses, p["mix_gain"], cfg, "mix_fwd")
    xm = _matmul(mix, p["w_out"], "nn", F32, "out_proj", tm=1024, tn=512, tk=2048, residual=x)
    h2 = _rmsnorm_fwd(xm, p["ffn_norm"], "ffn_norm_fwd")
    u = _matmul(h2, p["w_up_t"], "nt", F32, "up_proj", tm=1024, tn=512, tk=2048)
    act = _conv_act_fwd(u, p["conv_w"], p["conv_b"], cfg.f, "conv_act_fwd")
    y = _matmul(act, p["w_down"], "nn", F32, "down_proj", tm=1024, tn=1024, tk=512, residual=xm)
    saved = dict(x=x, h1=h1, proj=proj, q=(aq, ak, av, bq, bk, bv, cq, ck, cv), oa=oa, lse_a=lse_a, ob=ob,
                 tot_b=tot_b, bkt=bkt, oc=oc, lse_c=lse_c, mix=mix, xm=xm, h2=h2, u=u, act=act)
    return y, saved


def _layer_bwd(dy, sv, p, dbias, cfg):
    aq, ak, av, bq, bk, bv, cq, ck, cv = sv["q"]
    g_down = _matmul(sv["act"], dy, "tn", BF16, "down_proj_dw", tm=1408, tn=2048, tk=512)
    dact = _matmul(dy, p["w_down"], "nt", F32, "down_proj_dx", tm=1024, tn=512, tk=2048)
    dug, duu, dwg, dwu, dbg, dbu = _conv_act_bwd(sv["u"], dact, p["conv_w"], p["conv_b"], cfg.f, "conv_act_bwd")
    du = jnp.concatenate([dug, duu], axis=1)
    g_up_t = _matmul(du, sv["h2"], "tn", BF16, "up_proj_dw", tm=1408, tn=2048, tk=512)
    dh2 = _matmul(du, p["w_up_t"], "nn", F32, "up_proj_dx", tm=1024, tn=2048, tk=512)
    dxm, g_ffn_norm = _rmsnorm_bwd(sv["xm"], p["ffn_norm"], dh2, dy, "ffn_norm_bwd")
    g_out = _matmul(sv["mix"], dxm, "tn", BF16, "out_proj_dw", tm=1024, tn=2048, tk=512)
    dmix = _matmul(dxm, p["w_out"], "nt", F32, "out_proj_dx", tm=1024, tn=512, tk=2048)
    doa, dob, doc, g_mix_gain = _mix_bwd(dmix, sv["oa"], sv["ob"], sv["oc"], p["mix_gain"], cfg, "mix_bwd")
    daq, dak, dav, dbias_a, g_sinks = _banded_bwd(aq, ak, av, sv["oa"], sv["lse_a"], doa, p["bias_a"], p["sinks"],
                                                 dbias[0], cfg.nha, cfg.nkva, WINDOW_A - 1, 1, "swa_bwd")
    dbq, dbk, dbv = _sb_bwd(bq, bk, sv["bkt"], bv, sv["tot_b"], dob, "stickbreak_bwd")
    dcq, dck, dcv, dbias_c = [], [], [], []
    for idx, ((window, dil), bias) in enumerate(zip(DILATED_PAIRS, p["bias_c"])):
        a, b, c, d, _ = _banded_bwd(cq, ck, cv, sv["oc"], sv["lse_c"], doc, bias, None, dbias[1][idx], cfg.nhc,
                                    cfg.nhc, window // dil, dil, f"dilated{dil}_bwd")
        dcq.append(a)
        dck.append(b)
        dcv.append(c)
        dbias_c.append(d)
    dproj, g_aq, g_ak, g_cq, g_ck = _qk_prep_bwd(
        sv["proj"], p["gains"], [[daq], [dak], [dav], [dbq], [dbk], [dbv], dcq, dck, dcv], cfg, "qk_prep_bwd")
    g_in_t = _matmul(dproj, sv["h1"], "tn", BF16, "in_proj_dw", tm=768, tn=2048, tk=512)
    dh1 = _matmul(dproj, p["w_in_t"], "nn", F32, "in_proj_dx", tm=1024, tn=2048, tk=768)
    dx, g_attn_norm = _rmsnorm_bwd(sv["x"], p["attn_norm"], dh1, dxm, "attn_norm_bwd")

    def fold(g):
        return jnp.sum(g.reshape(-1, HEAD_DIM), axis=0)

    small = dict(attn_norm=g_attn_norm[0], a_q_gain=fold(g_aq), a_k_gain=fold(g_ak), a_sinks=g_sinks,
                 c_q_gain=fold(g_cq), c_k_gain=fold(g_ck), mix_out_gain=g_mix_gain[0], ffn_norm=g_ffn_norm[0],
                 conv_w=jnp.concatenate([dwg, dwu], axis=1), conv_b=jnp.concatenate([dbg, dbu], axis=1)[0])
    big = (g_in_t, g_out, g_up_t, g_down)
    return dx, big, small, (dbias_a, dbias_c)


_SMALL = ("attn_norm", "a_q_gain", "a_k_gain", "a_sinks", "c_q_gain", "c_k_gain", "rel_bias_table", "mix_out_gain",
          "ffn_norm", "conv_w", "conv_b")


def _pack(arrays):
    flat = jnp.concatenate([a.reshape(-1).astype(F32) for a in arrays])
    pad = (-flat.shape[0]) % (8 * LANES)
    return jnp.pad(flat, (0, pad)).reshape(-1, LANES)


def _unpack(flat, shapes):
    flat = flat.reshape(-1)
    out, pos = [], 0
    for sh in shapes:
        n = int(np.prod(sh))
        out.append(flat[pos:pos + n].reshape(sh))
        pos += n
    return out


def kernel(x, attn_norm, w_in, a_q_gain, a_k_gain, a_sinks, c_q_gain, c_k_gain, rel_bias_table, mix_out_gain, w_out, ffn_norm, w_up, conv_w, conv_b, w_down, loss_target, m_attn_norm, m_w_in, m_a_q_gain, m_a_k_gain, m_a_sinks, m_c_q_gain, m_c_k_gain, m_rel_bias_table, m_mix_out_gain, m_w_out, m_ffn_norm, m_w_up, m_conv_w, m_conv_b, m_w_down, v_attn_norm, v_w_in, v_a_q_gain, v_a_k_gain, v_a_sinks, v_c_q_gain, v_c_k_gain, v_rel_bias_table, v_mix_out_gain, v_w_out, v_ffn_norm, v_w_up, v_conv_w, v_conv_b, v_w_down):
    depth, d = attn_norm.shape
    f = w_down.shape[1] * N_CHIPS
    cfg = _Cfg(d, f)
    chip = 2 * lax.axis_index("x") + lax.axis_index("y")

    cw_cols = conv_w.shape[2]
    cw_flat = conv_w.reshape(-1)
    cw_rows = -(-cw_flat.shape[0] // (16 * LANES)) * 16
    cw_pad = jnp.pad(cw_flat, (0, cw_rows * LANES - cw_flat.shape[0])).reshape(cw_rows, LANES)
    (cw_all,) = _gather_rows([cw_pad], "gather_conv_w")
    cw_all = cw_all.reshape(N_CHIPS, -1)[:, :cw_flat.shape[0]].reshape(N_CHIPS, depth, CONV_WIDTH, cw_cols)
    conv_w_full = jnp.transpose(cw_all, (1, 2, 0, 3)).reshape(depth, CONV_WIDTH, N_CHIPS * cw_cols)

    table_a, table_c = rel_bias_table[:, :cfg.nha], rel_bias_table[:, cfg.nha:]
    bias_a = _block_bias(table_a, 1)
    bias_c = [_block_bias(table_c, dil) for _, dil in DILATED_PAIRS]

    layers = []
    for l in range(depth):
        shards = [w_in[l].T.astype(BF16), w_out[l].astype(BF16), w_up[l].T.astype(BF16), w_down[l].astype(BF16)]
        w_in_t, w_out_f, w_up_t, w_down_f = [g.reshape(-1, g.shape[-1]) for g in _gather_rows(shards, "gather_weights")]
        layers.append(dict(
            attn_norm=attn_norm[l].reshape(1, d), ffn_norm=ffn_norm[l].reshape(1, d),
            mix_gain=mix_out_gain[l].reshape(1, d),
            gains=(_tile_gain(a_q_gain[l], cfg.nha), _tile_gain(a_k_gain[l], cfg.nkva),
                   _tile_gain(c_q_gain[l], cfg.nhc), _tile_gain(c_k_gain[l], cfg.nhc)),
            sinks=a_sinks[l], bias_a=bias_a, bias_c=bias_c, w_in_t=w_in_t, w_out=w_out_f, w_up_t=w_up_t,
            w_down=w_down_f, conv_w=conv_w_full[l], conv_b=conv_b[l].reshape(1, 2 * f)))

    act = x[0]
    saved = []
    for l in range(depth):
        act, sv = _layer_fwd(act, layers[l], cfg)
        saved.append(sv)
    dact, loss_blk = _loss_head(act, loss_target[0], "loss_head")
    loss = lax.psum(loss_blk[0, 0], ("x", "y", "c"))

    dbias = (jnp.zeros_like(bias_a), [jnp.zeros_like(b) for b in bias_c])
    big_grads = [None] * depth
    small_grads = [None] * depth
    for l in reversed(range(depth)):
        dact, big, small_grads[l], dbias = _layer_bwd(dact, saved[l], layers[l], dbias, cfg)
        parts = _scatter_parts([g.reshape(N_CHIPS, -1, g.shape[-1]) for g in big], "scatter_grads")
        halves = [_sum_parts(pt, f"sum_grads_{t}") for t, pt in enumerate(parts)]
        big_grads[l] = _swap_halves(halves, "swap_grad_halves")
    grad_x = dact[None]

    tabs = _bias_table_grad([dbias[0]] + dbias[1], [jnp.asarray(_bucket_ids(1))]
                            + [jnp.asarray(_bucket_ids(dil)) for _, dil in DILATED_PAIRS], "bias_table_grad")
    g_table_a = tabs[0][:, :N_BUCKETS].T
    g_table_c = (tabs[1] + tabs[2] + tabs[3])[:, :N_BUCKETS].T
    g_table = jnp.concatenate([g_table_a, g_table_c], axis=1)
    small_local = {k: jnp.stack([small_grads[l][k] for l in range(depth)]) for k in _SMALL if k != "rel_bias_table"}
    small_local["rel_bias_table"] = g_table
    shapes = [small_local[k].shape for k in _SMALL]
    reduced = dict(zip(_SMALL, _unpack(_allreduce_small(_pack([small_local[k] for k in _SMALL]), "allreduce_small"),
                                       shapes)))
    reduced["conv_w"] = lax.dynamic_slice_in_dim(reduced["conv_w"], chip * cw_cols, cw_cols, axis=2)

    given = dict(attn_norm=attn_norm, a_q_gain=a_q_gain, a_k_gain=a_k_gain, a_sinks=a_sinks, c_q_gain=c_q_gain,
                 c_k_gain=c_k_gain, rel_bias_table=rel_bias_table, mix_out_gain=mix_out_gain, ffn_norm=ffn_norm,
                 conv_w=conv_w, conv_b=conv_b)
    moms = dict(attn_norm=(m_attn_norm, v_attn_norm), a_q_gain=(m_a_q_gain, v_a_q_gain),
                a_k_gain=(m_a_k_gain, v_a_k_gain), a_sinks=(m_a_sinks, v_a_sinks), c_q_gain=(m_c_q_gain, v_c_q_gain),
                c_k_gain=(m_c_k_gain, v_c_k_gain), rel_bias_table=(m_rel_bias_table, v_rel_bias_table),
                mix_out_gain=(m_mix_out_gain, v_mix_out_gain), ffn_norm=(m_ffn_norm, v_ffn_norm),
                conv_w=(m_conv_w, v_conv_w), conv_b=(m_conv_b, v_conv_b))
    sshapes = [given[k].shape for k in _SMALL]
    s_delta, s_m, s_v = _adamw(_pack([given[k] for k in _SMALL]), _pack([reduced[k] for k in _SMALL]),
                               _pack([moms[k][0] for k in _SMALL]), _pack([moms[k][1] for k in _SMALL]), "adamw_small")
    grads = dict(reduced)
    deltas = dict(zip(_SMALL, _unpack(s_delta, sshapes)))
    new_m = dict(zip(_SMALL, _unpack(s_m, sshapes)))
    new_v = dict(zip(_SMALL, _unpack(s_v, sshapes)))

    big_given = dict(w_in=(w_in, m_w_in, v_w_in, True), w_out=(w_out, m_w_out, v_w_out, False),
                     w_up=(w_up, m_w_up, v_w_up, True), w_down=(w_down, m_w_down, v_w_down, False))
    for t, name in enumerate(("w_in", "w_out", "w_up", "w_down")):
        wt, mt, vt, transposed = big_given[name]
        bufs = None
        for l in range(depth):
            g = big_grads[l][t].T if transposed else big_grads[l][t]
            bufs = _adamw_layer(l, wt, g, mt, vt, bufs, f"adamw_{name}_{l}")
        grads[name], deltas[name], new_m[name], new_v[name] = bufs

    order = ("attn_norm", "w_in", "a_q_gain", "a_k_gain", "a_sinks", "c_q_gain", "c_k_gain", "rel_bias_table",
             "mix_out_gain", "w_out", "ffn_norm", "w_up", "conv_w", "conv_b", "w_down")
    return (loss, grad_x, *[grads[k] for k in order], *[deltas[k] for k in order], *[new_m[k] for k in order],
            *[new_v[k] for k in order])
```

```python
import numpy as np
import jax
import jax.numpy as jnp
from jax import lax
from jax.experimental import pallas as pl
from jax.experimental.pallas import tpu as pltpu

F32 = jnp.float32
BF16 = jnp.bfloat16
MESH = pl.DeviceIdType.MESH

HEAD_DIM = 64
BLOCK = 128
LANES = 128
EPS = 1e-6
NEG_INF = -1e30
WINDOW_A = 128
DILATED_PAIRS = ((128, 1), (512, 4), (2048, 16))
N_BUCKETS = 32
T5_MAX_DIST = 2048
CONV_WIDTH = 3
ADAM_LR = 0.001
ADAM_B1 = 0.9
ADAM_B2 = 0.999
ADAM_EPS = 1e-08
ADAM_WD = 0.01
ADAM_STEP = 10
N_CHIPS = 4
N_DEVICES = 8
VMEM_LIMIT_BYTES = 48 * 1024 * 1024
QK_SCALE = HEAD_DIM ** -0.5


def _params(sem=None):
    return pltpu.CompilerParams(dimension_semantics=sem, vmem_limit_bytes=VMEM_LIMIT_BYTES)


def _div_tile(n, cap, mult):
    best = None
    for t in range(mult, min(n, cap) + 1, mult):
        if n % t == 0:
            best = t
    return n if best is None else best


def _dot(a, b):
    return lax.dot_general(a, b, (((1,), (0,)), ((), ())), preferred_element_type=F32)


def _dot_nt(a, b):
    return lax.dot_general(a, b, (((1,), (1,)), ((), ())), preferred_element_type=F32)


def _dot_tn(a, b):
    return lax.dot_general(a, b, (((0,), (0,)), ((), ())), preferred_element_type=F32)


def _split_dot(x, m):
    hi = x.astype(BF16)
    lo = (x - hi.astype(F32)).astype(BF16)
    return _dot(hi, m) + _dot(lo, m)


class _Cfg:
    def __init__(self, d_model, d_ff):
        nh = d_model // HEAD_DIM
        self.d = d_model
        self.f = d_ff
        self.nha = nh // 4
        self.nkva = self.nha // 4
        self.nhb = nh // 4
        self.nhc = nh // 2
        self.a_q = self.nha * HEAD_DIM
        self.a_kv = self.nkva * HEAD_DIM
        self.b_w = self.nhb * HEAD_DIM
        self.c_w = self.nhc * HEAD_DIM
        sizes = [self.a_q, self.a_kv, self.a_kv, self.b_w, self.b_w, self.b_w, self.c_w, self.c_w, self.c_w]
        starts = [0] + [int(s) for s in np.cumsum(sizes)[:-1]]
        self.sections = list(zip(starts, sizes))
        self.in_width = int(sum(sizes))
        assert all(s % LANES == 0 for s in sizes)


def _matmul(a, b, mode, out_dtype, name, tm=512, tn=512, tk=512, residual=None):
    if mode == "tn":
        kdim, m = a.shape
    else:
        m, kdim = a.shape
    n = b.shape[0] if mode == "nt" else b.shape[1]
    tm, tn, tk = _div_tile(m, tm, LANES), _div_tile(n, tn, LANES), _div_tile(kdim, tk, LANES)
    nk = kdim // tk
    if mode == "tn":
        a_spec = pl.BlockSpec((tk, tm), lambda i, j, k: (k, i))
    else:
        a_spec = pl.BlockSpec((tm, tk), lambda i, j, k: (i, k))
    if mode == "nt":
        b_spec = pl.BlockSpec((tn, tk), lambda i, j, k: (j, k))
    else:
        b_spec = pl.BlockSpec((tk, tn), lambda i, j, k: (k, j))
    dot = {"nn": _dot, "nt": _dot_nt, "tn": _dot_tn}[mode]
    o_spec = pl.BlockSpec((tm, tn), lambda i, j, k: (i, j))
    in_specs = [a_spec, b_spec]
    args = [a, b]
    if residual is not None:
        in_specs.append(o_spec)
        args.append(residual)

    def body(*refs):
        if residual is None:
            a_ref, b_ref, o_ref, acc = refs
        else:
            a_ref, b_ref, r_ref, o_ref, acc = refs
        k = pl.program_id(2)

        @pl.when(k == 0)
        def _():
            acc[...] = jnp.zeros_like(acc)

        acc[...] += dot(a_ref[...].astype(BF16), b_ref[...].astype(BF16))

        @pl.when(k == nk - 1)
        def _():
            r = acc[...]
            if residual is not None:
                r = r + r_ref[...]
            o_ref[...] = r.astype(out_dtype)

    return pl.pallas_call(
        body, name=name, grid=(m // tm, n // tn, nk), in_specs=in_specs, out_specs=o_spec,
        out_shape=jax.ShapeDtypeStruct((m, n), out_dtype), scratch_shapes=[pltpu.VMEM((tm, tn), F32)],
        compiler_params=_params(("parallel", "parallel", "arbitrary")),
    )(*args)


def _rmsnorm_fwd(x, g, name):
    s, d = x.shape
    ts = _div_tile(s, 256, 8)

    def body(x_ref, g_ref, o_ref):
        xv = x_ref[...]
        r = lax.rsqrt(jnp.mean(xv * xv, axis=-1, keepdims=True) + EPS)
        o_ref[...] = (xv * r * g_ref[...]).astype(BF16)

    return pl.pallas_call(
        body, name=name, grid=(s // ts,),
        in_specs=[pl.BlockSpec((ts, d), lambda i: (i, 0)), pl.BlockSpec((1, d), lambda i: (0, 0))],
        out_specs=pl.BlockSpec((ts, d), lambda i: (i, 0)), out_shape=jax.ShapeDtypeStruct((s, d), BF16),
        compiler_params=_params(("parallel",)),
    )(x, g)


def _rmsnorm_bwd(x, g, dh, dres, name):
    s, d = x.shape
    ts = _div_tile(s, 256, 8)

    def body(x_ref, g_ref, dh_ref, dres_ref, dx_ref, dg_ref):
        @pl.when(pl.program_id(0) == 0)
        def _():
            dg_ref[...] = jnp.zeros_like(dg_ref)

        xv = x_ref[...]
        r = lax.rsqrt(jnp.mean(xv * xv, axis=-1, keepdims=True) + EPS)
        xhat = xv * r
        dhv = dh_ref[...]
        dxhat = dhv * g_ref[...]
        dx_ref[...] = dres_ref[...] + r * (dxhat - xhat * jnp.mean(dxhat * xhat, axis=-1, keepdims=True))
        dg_ref[...] += jnp.sum(dhv * xhat, axis=0, keepdims=True)

    row = pl.BlockSpec((ts, d), lambda i: (i, 0))
    vec = pl.BlockSpec((1, d), lambda i: (0, 0))
    return pl.pallas_call(
        body, name=name, grid=(s // ts,), in_specs=[row, vec, row, row], out_specs=[row, vec],
        out_shape=[jax.ShapeDtypeStruct((s, d), F32), jax.ShapeDtypeStruct((1, d), F32)],
        compiler_params=_params(("arbitrary",)),
    )(x, g, dh, dres)


def _head_mean_matrix():
    idx = np.arange(LANES) // HEAD_DIM
    return jnp.asarray((idx[:, None] == idx[None, :]).astype(np.float32) / HEAD_DIM, dtype=BF16)


def _head_mean(y, m128):
    w = y.shape[1]
    parts = [_split_dot(y[:, c * LANES:(c + 1) * LANES], m128) for c in range(w // LANES)]
    return parts[0] if len(parts) == 1 else jnp.concatenate(parts, axis=1)


_NORMED_SECTIONS = (0, 1, 6, 7)


def _qk_prep(proj, gains, cfg, name):
    s = proj.shape[0]
    ts = _div_tile(s, 256, 16)
    m128 = _head_mean_matrix()

    def body(p_ref, m_ref, g0, g1, g6, g7, *outs):
        gref = dict(zip(_NORMED_SECTIONS, (g0, g1, g6, g7)))
        for idx, (st, w) in enumerate(cfg.sections):
            xv = p_ref[:, st:st + w]
            if idx in gref:
                r = lax.rsqrt(_head_mean(xv * xv, m_ref[...]) + EPS)
                xv = xv * r * gref[idx][...]
            outs[idx][...] = xv.astype(BF16)

    in_specs = [pl.BlockSpec((ts, cfg.in_width), lambda i: (i, 0)), pl.BlockSpec((LANES, LANES), lambda i: (0, 0))]
    in_specs += [pl.BlockSpec((1, cfg.sections[k][1]), lambda i: (0, 0)) for k in _NORMED_SECTIONS]
    out_specs = [pl.BlockSpec((ts, w), lambda i: (i, 0)) for _, w in cfg.sections]
    out_shape = [jax.ShapeDtypeStruct((s, w), BF16) for _, w in cfg.sections]
    return pl.pallas_call(
        body, name=name, grid=(s // ts,), in_specs=in_specs, out_specs=out_specs, out_shape=out_shape,
        compiler_params=_params(("parallel",)),
    )(proj, m128, *gains)


def _qk_prep_bwd(proj, gains, grads, cfg, name):
    s = proj.shape[0]
    ts = _div_tile(s, 128, 16)
    m128 = _head_mean_matrix()
    counts = [len(gl) for gl in grads]
    flat = [g for gl in grads for g in gl]

    def body(*refs):
        p_ref, m_ref = refs[0], refs[1]
        gref = dict(zip(_NORMED_SECTIONS, refs[2:6]))
        g_in = refs[6:6 + len(flat)]
        dp_ref = refs[6 + len(flat)]
        dgain = dict(zip(_NORMED_SECTIONS, refs[7 + len(flat):]))

        @pl.when(pl.program_id(0) == 0)
        def _():
            for k in _NORMED_SECTIONS:
                dgain[k][...] = jnp.zeros_like(dgain[k])

        pos = 0
        for idx, (st, w) in enumerate(cfg.sections):
            dy = g_in[pos][...]
            for extra in g_in[pos + 1:pos + counts[idx]]:
                dy = dy + extra[...]
            pos += counts[idx]
            if idx in gref:
                xv = p_ref[:, st:st + w]
                r = lax.rsqrt(_head_mean(xv * xv, m_ref[...]) + EPS)
                xhat = xv * r
                dxhat = dy * gref[idx][...]
                dgain[idx][...] += jnp.sum(dy * xhat, axis=0, keepdims=True)
                dy = r * (dxhat - xhat * _head_mean(dxhat * xhat, m_ref[...]))
            dp_ref[:, st:st + w] = dy.astype(BF16)

    in_specs = [pl.BlockSpec((ts, cfg.in_width), lambda i: (i, 0)), pl.BlockSpec((LANES, LANES), lambda i: (0, 0))]
    in_specs += [pl.BlockSpec((1, cfg.sections[k][1]), lambda i: (0, 0)) for k in _NORMED_SECTIONS]
    for idx, (_, w) in enumerate(cfg.sections):
        in_specs += [pl.BlockSpec((ts, w), lambda i: (i, 0))] * counts[idx]
    out_specs = [pl.BlockSpec((ts, cfg.in_width), lambda i: (i, 0))]
    out_specs += [pl.BlockSpec((1, cfg.sections[k][1]), lambda i: (0, 0)) for k in _NORMED_SECTIONS]
    out_shape = [jax.ShapeDtypeStruct((s, cfg.in_width), BF16)]
    out_shape += [jax.ShapeDtypeStruct((1, cfg.sections[k][1]), F32) for k in _NORMED_SECTIONS]
    return pl.pallas_call(
        body, name=name, grid=(s // ts,), in_specs=in_specs, out_specs=out_specs, out_shape=out_shape,
        compiler_params=_params(("arbitrary",)),
    )(proj, m128, *gains, *flat)


def _band_masks(max_dist):
    row = lax.broadcasted_iota(jnp.int32, (BLOCK, BLOCK), 0)
    col = lax.broadcasted_iota(jnp.int32, (BLOCK, BLOCK), 1)
    return row + BLOCK - col <= max_dist, col <= row


def _banded_fwd(q, k, v, bias, sinks, hq, hk, max_dist, dil, name):
    s = q.shape[0]
    wq, wk, sd, grp = hq * HEAD_DIM, hk * HEAD_DIM, s // dil, hq // hk
    nb = sd // BLOCK
    has_sink = sinks is not None

    def body(*refs):
        if has_sink:
            q_ref, kp_ref, kc_ref, vp_ref, vc_ref, b_ref, s_ref, o_ref, l_ref = refs
        else:
            q_ref, kp_ref, kc_ref, vp_ref, vc_ref, b_ref, o_ref, l_ref = refs
        i = pl.program_id(1)
        mprev, mcur = _band_masks(max_dist)
        mprev = jnp.logical_and(mprev, i > 0)
        for h in range(hq):
            sq = slice(h * HEAD_DIM, (h + 1) * HEAD_DIM)
            sk = slice((h // grp) * HEAD_DIM, (h // grp + 1) * HEAD_DIM)
            qh = q_ref[:, sq]
            sp = jnp.where(mprev, _dot_nt(qh, kp_ref[:, sk]) * QK_SCALE + b_ref[h, :, 0:BLOCK], NEG_INF)
            sc = jnp.where(mcur, _dot_nt(qh, kc_ref[:, sk]) * QK_SCALE + b_ref[h, :, BLOCK:2 * BLOCK], NEG_INF)
            m = jnp.maximum(jnp.max(sp, axis=-1, keepdims=True), jnp.max(sc, axis=-1, keepdims=True))
            if has_sink:
                m = jnp.maximum(m, s_ref[h])
            pp = jnp.exp(sp - m)
            pc = jnp.exp(sc - m)
            den = jnp.sum(pp, axis=-1, keepdims=True) + jnp.sum(pc, axis=-1, keepdims=True)
            if has_sink:
                den = den + jnp.exp(s_ref[h] - m)
            acc = _dot(pp.astype(BF16), vp_ref[:, sk]) + _dot(pc.astype(BF16), vc_ref[:, sk])
            o_ref[:, sq] = acc / den
            l_ref[:, sq] = jnp.broadcast_to(m + jnp.log(den), (BLOCK, HEAD_DIM))

    qspec = pl.BlockSpec((BLOCK, wq), lambda r, i: (i, r))
    kprev = pl.BlockSpec((BLOCK, wk), lambda r, i: (jnp.maximum(i - 1, 0), r))
    kcur = pl.BlockSpec((BLOCK, wk), lambda r, i: (i, r))
    in_specs = [qspec, kprev, kcur, kprev, kcur, pl.BlockSpec((hq, BLOCK, 2 * BLOCK), lambda r, i: (0, 0, 0))]
    args = [q.reshape(sd, dil * wq), k.reshape(sd, dil * wk), k.reshape(sd, dil * wk),
            v.reshape(sd, dil * wk), v.reshape(sd, dil * wk), bias]
    if has_sink:
        in_specs.append(pl.BlockSpec(memory_space=pltpu.SMEM))
        args.append(sinks)
    out, lse = pl.pallas_call(
        body, name=name, grid=(dil, nb), in_specs=in_specs, out_specs=[qspec, qspec],
        out_shape=[jax.ShapeDtypeStruct((sd, dil * wq), F32)] * 2,
        compiler_params=_params(("parallel", "parallel")),
    )(*args)
    return out.reshape(s, wq), lse.reshape(s, wq)


def _banded_bwd(q, k, v, o, lse, do, bias, sinks, dbias_init, hq, hk, max_dist, dil, name):
    s = q.shape[0]
    wq, wk, sd, grp = hq * HEAD_DIM, hk * HEAD_DIM, s // dil, hq // hk
    nb = sd // BLOCK
    has_sink = sinks is not None

    def body(*refs):
        (q_ref, qn_ref, kp_ref, kc_ref, vp_ref, vc_ref, o_ref, on_ref, l_ref, ln_ref, do_ref, don_ref,
         b_ref, dbi_ref) = refs[:14]
        rest = refs[14:]
        if has_sink:
            s_ref, dq_ref, dk_ref, dv_ref, db_ref, ds_ref = rest
        else:
            dq_ref, dk_ref, dv_ref, db_ref = rest
        j = pl.program_id(1)

        @pl.when(jnp.logical_and(pl.program_id(0) == 0, j == 0))
        def _():
            db_ref[...] = dbi_ref[...]
            if has_sink:
                ds_ref[...] = jnp.zeros_like(ds_ref)

        mprev_static, mcur = _band_masks(max_dist)
        mprev = jnp.logical_and(mprev_static, j > 0)
        mnext = jnp.logical_and(mprev_static, j + 1 < nb)
        dk_acc = [jnp.zeros((BLOCK, HEAD_DIM), F32) for _ in range(hk)]
        dv_acc = [jnp.zeros((BLOCK, HEAD_DIM), F32) for _ in range(hk)]
        for h in range(hq):
            g = h // grp
            sq = slice(h * HEAD_DIM, (h + 1) * HEAD_DIM)
            sk = slice(g * HEAD_DIM, (g + 1) * HEAD_DIM)
            kp, kc, vp, vc = kp_ref[:, sk], kc_ref[:, sk], vp_ref[:, sk], vc_ref[:, sk]
            qh = q_ref[:, sq]
            doh = do_ref[:, sq]
            dohb = doh.astype(BF16)
            lcol = l_ref[:, h * HEAD_DIM:h * HEAD_DIM + 1]
            dcol = jnp.sum(doh * o_ref[:, sq], axis=-1, keepdims=True)
            sp = _dot_nt(qh, kp) * QK_SCALE + b_ref[h, :, 0:BLOCK]
            sc = _dot_nt(qh, kc) * QK_SCALE + b_ref[h, :, BLOCK:2 * BLOCK]
            pp = jnp.where(mprev, jnp.exp(sp - lcol), 0.0)
            pc = jnp.where(mcur, jnp.exp(sc - lcol), 0.0)
            dsp = pp * (_dot_nt(dohb, vp) - dcol)
            dsc = pc * (_dot_nt(dohb, vc) - dcol)
            dspb, dscb = dsp.astype(BF16), dsc.astype(BF16)
            dq_ref[:, sq] = (_dot(dspb, kp) + _dot(dscb, kc)) * QK_SCALE
            db_ref[h, :, 0:BLOCK] += dsp
            db_ref[h, :, BLOCK:2 * BLOCK] += dsc
            if has_sink:
                psink = jnp.exp(s_ref[h] - lcol)
                tot = jnp.sum(psink * dcol, axis=0, keepdims=True)
                ds_ref[h:h + 1, :] -= jnp.broadcast_to(tot, (1, LANES))
            qn = qn_ref[:, sq]
            don = don_ref[:, sq]
            donb = don.astype(BF16)
            lncol = ln_ref[:, h * HEAD_DIM:h * HEAD_DIM + 1]
            dncol = jnp.sum(don * on_ref[:, sq], axis=-1, keepdims=True)
            sn = _dot_nt(qn, kc) * QK_SCALE + b_ref[h, :, 0:BLOCK]
            pn = jnp.where(mnext, jnp.exp(sn - lncol), 0.0)
            dsn = pn * (_dot_nt(donb, vc) - dncol)
            dk_acc[g] = dk_acc[g] + (_dot_tn(dscb, qh) + _dot_tn(dsn.astype(BF16), qn)) * QK_SCALE
            dv_acc[g] = dv_acc[g] + _dot_tn(pc.astype(BF16), dohb) + _dot_tn(pn.astype(BF16), donb)
        for g in range(hk):
            sk = slice(g * HEAD_DIM, (g + 1) * HEAD_DIM)
            dk_ref[:, sk] = dk_acc[g]
            dv_ref[:, sk] = dv_acc[g]

    qcur = pl.BlockSpec((BLOCK, wq), lambda r, j: (j, r))
    qnext = pl.BlockSpec((BLOCK, wq), lambda r, j: (jnp.minimum(j + 1, nb - 1), r))
    kprev = pl.BlockSpec((BLOCK, wk), lambda r, j: (jnp.maximum(j - 1, 0), r))
    kcur = pl.BlockSpec((BLOCK, wk), lambda r, j: (j, r))
    bspec = pl.BlockSpec((hq, BLOCK, 2 * BLOCK), lambda r, j: (0, 0, 0))
    q2, k2, v2 = q.reshape(sd, dil * wq), k.reshape(sd, dil * wk), v.reshape(sd, dil * wk)
    o2, l2, do2 = o.reshape(sd, dil * wq), lse.reshape(sd, dil * wq), do.reshape(sd, dil * wq)
    in_specs = [qcur, qnext, kprev, kcur, kprev, kcur, qcur, qnext, qcur, qnext, qcur, qnext, bspec, bspec]
    args = [q2, q2, k2, k2, v2, v2, o2, o2, l2, l2, do2, do2, bias, dbias_init]
    out_specs = [qcur, kcur, kcur, bspec]
    out_shape = [jax.ShapeDtypeStruct((sd, dil * wq), F32), jax.ShapeDtypeStruct((sd, dil * wk), F32),
                 jax.ShapeDtypeStruct((sd, dil * wk), F32), jax.ShapeDtypeStruct((hq, BLOCK, 2 * BLOCK), F32)]
    if has_sink:
        in_specs.append(pl.BlockSpec(memory_space=pltpu.SMEM))
        args.append(sinks)
        out_specs.append(pl.BlockSpec((hq, LANES), lambda r, j: (0, 0)))
        out_shape.append(jax.ShapeDtypeStruct((hq, LANES), F32))
    res = pl.pallas_call(
        body, name=name, grid=(dil, nb), in_specs=in_specs, out_specs=out_specs, out_shape=out_shape,
        compiler_params=_params(("arbitrary", "arbitrary")),
    )(*args)
    dq, dk, dv, dbias = res[0].reshape(s, wq), res[1].reshape(s, wk), res[2].reshape(s, wk), res[3]
    return dq, dk, dv, dbias, (res[4][:, 0] if has_sink else None)


def _neg_softplus(z):
    return -(jnp.maximum(z, 0.0) + jnp.log(1.0 + jnp.exp(-jnp.abs(z))))


SB_CHUNK = 256
HEADS_PER_PAIR = LANES // HEAD_DIM


def _tri(kind):
    row = lax.broadcasted_iota(jnp.int32, (SB_CHUNK, SB_CHUNK), 0)
    col = lax.broadcasted_iota(jnp.int32, (SB_CHUNK, SB_CHUNK), 1)
    return {"ge": row >= col, "lt": row < col, "le": row <= col}[kind].astype(BF16)


def _keys_on_lanes(a, rows):
    s, w = a.shape
    return jnp.transpose(a.reshape(s // rows, rows, w), (0, 2, 1))


def _sb_diag_mask(i, nfull):
    row = lax.broadcasted_iota(jnp.int32, (BLOCK, SB_CHUNK), 0)
    col = lax.broadcasted_iota(jnp.int32, (BLOCK, SB_CHUNK), 1)
    return col < row + (i * BLOCK - nfull * SB_CHUNK)


def _sb_fwd(q, kt, v, name):
    s, w = q.shape
    npair, nb, nc = w // LANES, s // BLOCK, s // SB_CHUNK

    def body(q_ref, kt_ref, v_ref, o_ref, t_ref):
        i = pl.program_id(1)
        nfull = (i * BLOCK) // SB_CHUNK
        strict = _sb_diag_mask(i, nfull)
        lincl = _tri("ge")
        heads = [slice(hh * HEAD_DIM, (hh + 1) * HEAD_DIM) for hh in range(HEADS_PER_PAIR)]
        qs = [q_ref[:, sl] for sl in heads]

        def chunk(jj, diag, carry):
            rows = pl.ds(pl.multiple_of(jj * SB_CHUNK, SB_CHUNK), SB_CHUNK)
            new = []
            for hh, sl in enumerate(heads):
                o_acc, rem = carry[hh]
                z = _dot(qs[hh], kt_ref[jj, sl, :]) * QK_SCALE
                lr = _neg_softplus(z)
                if diag:
                    lr = jnp.where(strict, lr, 0.0)
                c = rem + _split_dot(lr, lincl)
                a = jnp.exp(z + c)
                if diag:
                    a = jnp.where(strict, a, 0.0)
                new.append((o_acc + _dot(a.astype(BF16), v_ref[rows, sl]),
                            rem + jnp.sum(lr, axis=-1, keepdims=True)))
            return tuple(new)

        init = tuple((jnp.zeros((BLOCK, HEAD_DIM), F32), jnp.zeros((BLOCK, 1), F32)) for _ in heads)
        carry = chunk(nfull, True, init)
        carry = lax.fori_loop(0, nfull, lambda t, cr: chunk(nfull - 1 - t, False, cr), carry)
        for hh, sl in enumerate(heads):
            o_ref[:, sl] = carry[hh][0]
            t_ref[:, sl] = jnp.broadcast_to(carry[hh][1], (BLOCK, HEAD_DIM))

    qspec = pl.BlockSpec((BLOCK, LANES), lambda p, i: (i, p))
    return pl.pallas_call(
        body, name=name, grid=(npair, nb),
        in_specs=[qspec, pl.BlockSpec((nc, LANES, SB_CHUNK), lambda p, i: (0, p, 0)),
                  pl.BlockSpec((s, LANES), lambda p, i: (0, p))],
        out_specs=[qspec, qspec], out_shape=[jax.ShapeDtypeStruct((s, w), F32)] * 2,
        compiler_params=_params(("parallel", "parallel")),
    )(q, kt, v)


def _sb_bwd(q, k, kt, v, tot, do, name):
    s, w = q.shape
    npair, nb, nc = w // LANES, s // BLOCK, s // SB_CHUNK
    dob = do.astype(BF16)

    def body(q_ref, qt_ref, k_ref, kt_ref, vt_ref, t_ref, do_ref, dot_ref, dq_ref, dkt_ref, dvt_ref):
        i = pl.program_id(1)

        @pl.when(i == 0)
        def _():
            dkt_ref[...] = jnp.zeros_like(dkt_ref)
            dvt_ref[...] = jnp.zeros_like(dvt_ref)

        nfull = (i * BLOCK) // SB_CHUNK
        strict = _sb_diag_mask(i, nfull)
        lbefore = _tri("lt")
        lupto = _tri("le")
        heads = [slice(hh * HEAD_DIM, (hh + 1) * HEAD_DIM) for hh in range(HEADS_PER_PAIR)]
        qs = [q_ref[:, sl] for sl in heads]
        qts = [qt_ref[sl, :] for sl in heads]
        dos = [do_ref[:, sl] for sl in heads]
        dots = [dot_ref[sl, :] for sl in heads]
        totals = [t_ref[:, sl.start:sl.start + 1] for sl in heads]

        def chunk(jj, diag, carry):
            rows = pl.ds(pl.multiple_of(jj * SB_CHUNK, SB_CHUNK), SB_CHUNK)
            new = []
            for hh, sl in enumerate(heads):
                dq_acc, plr, pg = carry[hh]
                z = _dot(qs[hh], kt_ref[jj, sl, :]) * QK_SCALE
                lr = _neg_softplus(z)
                if diag:
                    lr = jnp.where(strict, lr, 0.0)
                c = totals[hh] - (plr + _split_dot(lr, lbefore))
                a = jnp.exp(z + c)
                if diag:
                    a = jnp.where(strict, a, 0.0)
                g = _dot(dos[hh], vt_ref[jj, sl, :]) * a
                dlr = pg + _split_dot(g, lupto)
                dz = g - jnp.exp(z + lr) * dlr
                if diag:
                    dz = jnp.where(strict, dz, 0.0)
                dzb = (dz * QK_SCALE).astype(BF16)
                dkt_ref[jj, sl, :] += _dot(qts[hh], dzb)
                dvt_ref[jj, sl, :] += _dot(dots[hh], a.astype(BF16))
                new.append((dq_acc + _dot(dzb, k_ref[rows, sl]), plr + jnp.sum(lr, axis=-1, keepdims=True),
                            pg + jnp.sum(g, axis=-1, keepdims=True)))
            return tuple(new)

        zero = jnp.zeros((BLOCK, 1), F32)
        init = tuple((jnp.zeros((BLOCK, HEAD_DIM), F32), zero, zero) for _ in heads)
        carry = lax.fori_loop(0, nfull, lambda t, cr: chunk(t, False, cr), init)
        carry = chunk(nfull, True, carry)
        for hh, sl in enumerate(heads):
            dq_ref[:, sl] = carry[hh][0]

    qspec = pl.BlockSpec((BLOCK, LANES), lambda p, i: (i, p))
    qtspec = pl.BlockSpec((None, LANES, BLOCK), lambda p, i: (i, p, 0))
    kspec = pl.BlockSpec((s, LANES), lambda p, i: (0, p))
    ktspec = pl.BlockSpec((nc, LANES, SB_CHUNK), lambda p, i: (0, p, 0))
    dq, dkt, dvt = pl.pallas_call(
        body, name=name, grid=(npair, nb), in_specs=[qspec, qtspec, kspec, ktspec, ktspec, qspec, qspec, qtspec],
        out_specs=[qspec, ktspec, ktspec],
        out_shape=[jax.ShapeDtypeStruct((s, w), F32)] + [jax.ShapeDtypeStruct((nc, w, SB_CHUNK), F32)] * 2,
        compiler_params=_params(("parallel", "arbitrary")),
    )(q, _keys_on_lanes(q, BLOCK), k, kt, _keys_on_lanes(v, SB_CHUNK), tot, dob, _keys_on_lanes(dob, BLOCK))

    def rows_first(t):
        return jnp.transpose(t, (0, 2, 1)).reshape(s, w)

    return dq, rows_first(dkt), rows_first(dvt)


def _group_norm(xv, g):
    r = lax.rsqrt(jnp.mean(xv * xv, axis=-1, keepdims=True) + EPS)
    return xv * r * g


def _mix_fwd(oa, ob, ocs, lses, gain, cfg, name):
    s = oa.shape[0]
    ts = _div_tile(s, 256, 16)
    aq, bw, cw = cfg.a_q, cfg.b_w, cfg.c_w

    def body(oa_ref, ob_ref, c1, c2, c3, l1, l2, l3, g_ref, mix_ref, oc_ref, lse_ref):
        m = jnp.maximum(jnp.maximum(l1[...], l2[...]), l3[...])
        e1, e2, e3 = jnp.exp(l1[...] - m), jnp.exp(l2[...] - m), jnp.exp(l3[...] - m)
        den = e1 + e2 + e3
        oc = (e1 * c1[...] + e2 * c2[...] + e3 * c3[...]) / den
        oc_ref[...] = oc
        lse_ref[...] = m + jnp.log(den)
        mix_ref[:, 0:aq] = _group_norm(oa_ref[...], g_ref[:, 0:aq]).astype(BF16)
        mix_ref[:, aq:aq + bw] = _group_norm(ob_ref[...], g_ref[:, aq:aq + bw]).astype(BF16)
        mix_ref[:, aq + bw:] = _group_norm(oc, g_ref[:, aq + bw:]).astype(BF16)

    def row(wd):
        return pl.BlockSpec((ts, wd), lambda i: (i, 0))

    return pl.pallas_call(
        body, name=name, grid=(s // ts,),
        in_specs=[row(aq), row(bw)] + [row(cw)] * 6 + [pl.BlockSpec((1, cfg.d), lambda i: (0, 0))],
        out_specs=[row(cfg.d), row(cw), row(cw)],
        out_shape=[jax.ShapeDtypeStruct((s, cfg.d), BF16), jax.ShapeDtypeStruct((s, cw), F32),
                   jax.ShapeDtypeStruct((s, cw), F32)],
        compiler_params=_params(("parallel",)),
    )(oa, ob, *ocs, *lses, gain)


def _mix_bwd(dmix, oa, ob, oc, gain, cfg, name):
    s = oa.shape[0]
    ts = _div_tile(s, 256, 8)
    aq, bw, cw = cfg.a_q, cfg.b_w, cfg.c_w

    def body(dm_ref, oa_ref, ob_ref, oc_ref, g_ref, da_ref, db_ref, dc_ref, dg_ref):
        @pl.when(pl.program_id(0) == 0)
        def _():
            dg_ref[...] = jnp.zeros_like(dg_ref)

        for x_ref, dx_ref, lo, hi in ((oa_ref, da_ref, 0, aq), (ob_ref, db_ref, aq, aq + bw),
                                      (oc_ref, dc_ref, aq + bw, aq + bw + cw)):
            xv = x_ref[...]
            dy = dm_ref[:, lo:hi]
            r = lax.rsqrt(jnp.mean(xv * xv, axis=-1, keepdims=True) + EPS)
            xhat = xv * r
            dxhat = dy * g_ref[:, lo:hi]
            dx_ref[...] = r * (dxhat - xhat * jnp.mean(dxhat * xhat, axis=-1, keepdims=True))
            dg_ref[:, lo:hi] += jnp.sum(dy * xhat, axis=0, keepdims=True)

    def row(wd):
        return pl.BlockSpec((ts, wd), lambda i: (i, 0))

    vec = pl.BlockSpec((1, cfg.d), lambda i: (0, 0))
    return pl.pallas_call(
        body, name=name, grid=(s // ts,), in_specs=[row(cfg.d), row(aq), row(bw), row(cw), vec],
        out_specs=[row(aq), row(bw), row(cw), vec],
        out_shape=[jax.ShapeDtypeStruct((s, aq), F32), jax.ShapeDtypeStruct((s, bw), F32),
                   jax.ShapeDtypeStruct((s, cw), F32), jax.ShapeDtypeStruct((1, cfg.d), F32)],
        compiler_params=_params(("arbitrary",)),
    )(dmix, oa, ob, oc, gain)


def _bias_table_grad(dbiases, buckets, name):
    outs = []
    for idx, (db, bk) in enumerate(zip(dbiases, buckets)):
        h = db.shape[0]

        def body(db_ref, bk_ref, o_ref):
            xv = db_ref[0]
            ids = bk_ref[...]
            lane = lax.broadcasted_iota(jnp.int32, (1, LANES), 1)
            acc = jnp.zeros((1, LANES), F32)
            for b in range(N_BUCKETS):
                tot = jnp.sum(jnp.where(ids == b, xv, 0.0), axis=0, keepdims=True)
                tot = jnp.sum(tot, axis=1, keepdims=True)
                acc = jnp.where(lane == b, tot, acc)
            o_ref[0] = acc

        outs.append(pl.pallas_call(
            body, name=f"{name}_{idx}", grid=(h,),
            in_specs=[pl.BlockSpec((1, BLOCK, 2 * BLOCK), lambda i: (i, 0, 0)),
                      pl.BlockSpec((BLOCK, 2 * BLOCK), lambda i: (0, 0))],
            out_specs=pl.BlockSpec((1, 1, LANES), lambda i: (i, 0, 0)),
            out_shape=jax.ShapeDtypeStruct((h, 1, LANES), F32), compiler_params=_params(("parallel",)),
        )(db, bk)[:, 0, :])
    return outs


def _shift_down(u, n, rows):
    return jnp.where(rows >= n, pltpu.roll(u, n, 0), 0.0)


def _shift_up(u, n, rows, s):
    return jnp.where(rows < s - n, pltpu.roll(u, s - n, 0), 0.0)


def _conv(u, w_ref, b_ref, rows):
    return (b_ref[...] + w_ref[0:1, :] * _shift_down(u, 2, rows) + w_ref[1:2, :] * _shift_down(u, 1, rows)
            + w_ref[2:3, :] * u)


def _conv_act_fwd(u, conv_w, conv_b, f, name):
    s = u.shape[0]
    nf = f // LANES

    def body(ug_ref, uu_ref, wg_ref, wu_ref, bg_ref, bu_ref, act_ref):
        rows = lax.broadcasted_iota(jnp.int32, (s, LANES), 0)
        gate = _conv(ug_ref[...], wg_ref, bg_ref, rows)
        up = _conv(uu_ref[...], wu_ref, bu_ref, rows)
        act_ref[...] = (gate * jax.nn.sigmoid(gate) * up).astype(BF16)

    def col(rws, off):
        return pl.BlockSpec((rws, LANES), lambda j: (0, j + off))

    return pl.pallas_call(
        body, name=name, grid=(nf,),
        in_specs=[col(s, 0), col(s, nf), col(CONV_WIDTH, 0), col(CONV_WIDTH, nf), col(1, 0), col(1, nf)],
        out_specs=col(s, 0), out_shape=jax.ShapeDtypeStruct((s, f), BF16), compiler_params=_params(("parallel",)),
    )(u, u, conv_w, conv_w, conv_b, conv_b)


def _conv_act_bwd(u, dact, conv_w, conv_b, f, name):
    s = u.shape[0]
    nf = f // LANES

    def body(ug_ref, uu_ref, da_ref, wg_ref, wu_ref, bg_ref, bu_ref, dug_ref, duu_ref, dwg_ref, dwu_ref, dbg_ref,
             dbu_ref):
        rows = lax.broadcasted_iota(jnp.int32, (s, LANES), 0)
        ug, uu = ug_ref[...], uu_ref[...]
        gate = _conv(ug, wg_ref, bg_ref, rows)
        up = _conv(uu, wu_ref, bu_ref, rows)
        sg = jax.nn.sigmoid(gate)
        da = da_ref[...]
        dgate = da * up * (sg * (1.0 + gate * (1.0 - sg)))
        dup = da * (gate * sg)
        for du, uv, w_ref, du_ref, dw_ref, db_ref in ((dgate, ug, wg_ref, dug_ref, dwg_ref, dbg_ref),
                                                     (dup, uu, wu_ref, duu_ref, dwu_ref, dbu_ref)):
            du_ref[...] = (w_ref[2:3, :] * du + w_ref[1:2, :] * _shift_up(du, 1, rows, s)
                           + w_ref[0:1, :] * _shift_up(du, 2, rows, s)).astype(BF16)
            dw_ref[0:1, :] = jnp.sum(du * _shift_down(uv, 2, rows), axis=0, keepdims=True)
            dw_ref[1:2, :] = jnp.sum(du * _shift_down(uv, 1, rows), axis=0, keepdims=True)
            dw_ref[2:3, :] = jnp.sum(du * uv, axis=0, keepdims=True)
            db_ref[...] = jnp.sum(du, axis=0, keepdims=True)

    def col(rws, off):
        return pl.BlockSpec((rws, LANES), lambda j: (0, j + off))

    return pl.pallas_call(
        body, name=name, grid=(nf,),
        in_specs=[col(s, 0), col(s, nf), col(s, 0), col(CONV_WIDTH, 0), col(CONV_WIDTH, nf), col(1, 0), col(1, nf)],
        out_specs=[col(s, 0), col(s, 0), col(CONV_WIDTH, 0), col(CONV_WIDTH, 0), col(1, 0), col(1, 0)],
        out_shape=[jax.ShapeDtypeStruct((s, f), BF16)] * 2 + [jax.ShapeDtypeStruct((CONV_WIDTH, f), F32)] * 2
        + [jax.ShapeDtypeStruct((1, f), F32)] * 2,
        compiler_params=_params(("parallel",)),
    )(u, u, dact, conv_w, conv_w, conv_b, conv_b)


def _loss_head(y, target, name):
    s, d = y.shape
    ts = _div_tile(s, 256, 8)

    def body(y_ref, t_ref, dy_ref, l_ref):
        @pl.when(pl.program_id(0) == 0)
        def _():
            l_ref[...] = jnp.zeros_like(l_ref)

        err = y_ref[...] - t_ref[...]
        dy_ref[...] = err * (1.0 / d)
        tot = jnp.sum(jnp.sum(err * err, axis=0, keepdims=True), axis=1, keepdims=True) * (0.5 / d)
        l_ref[...] += jnp.broadcast_to(tot, l_ref.shape)

    row = pl.BlockSpec((ts, d), lambda i: (i, 0))
    return pl.pallas_call(
        body, name=name, grid=(s // ts,), in_specs=[row, row],
        out_specs=[row, pl.BlockSpec((8, LANES), lambda i: (0, 0))],
        out_shape=[jax.ShapeDtypeStruct((s, d), F32), jax.ShapeDtypeStruct((8, LANES), F32)],
        compiler_params=_params(("arbitrary",)),
    )(y, target)


def _adamw(w, g, m, v, name):
    r, c = w.shape
    tr = _div_tile(r, max(8, (1 << 18) // c // 8 * 8), 8)
    c1 = 1.0 - ADAM_B1 ** ADAM_STEP
    c2 = 1.0 - ADAM_B2 ** ADAM_STEP

    def body(w_ref, g_ref, m_ref, v_ref, d_ref, nm_ref, nv_ref):
        gv = g_ref[...]
        nm = ADAM_B1 * m_ref[...] + (1.0 - ADAM_B1) * gv
        nv = ADAM_B2 * v_ref[...] + (1.0 - ADAM_B2) * (gv * gv)
        d_ref[...] = -ADAM_LR * ((nm / c1) / (jnp.sqrt(nv / c2) + ADAM_EPS) + ADAM_WD * w_ref[...])
        nm_ref[...] = nm
        nv_ref[...] = nv

    spec = pl.BlockSpec((tr, c), lambda i: (i, 0))
    return pl.pallas_call(
        body, name=name, grid=(r // tr,), in_specs=[spec] * 4, out_specs=[spec] * 3,
        out_shape=[jax.ShapeDtypeStruct((r, c), F32)] * 3, compiler_params=_params(("parallel",)),
    )(w, g, m, v)


def _adamw_layer(layer, w, g, m, v, bufs, name):
    depth, r, c = w.shape
    tr = _div_tile(r, max(8, (1 << 17) // c // 8 * 8), 8)
    c1 = 1.0 - ADAM_B1 ** ADAM_STEP
    c2 = 1.0 - ADAM_B2 ** ADAM_STEP

    def body(*refs):
        w_ref, g_ref, m_ref, v_ref = refs[:4]
        go_ref, d_ref, nm_ref, nv_ref = refs[-4:]
        gv = g_ref[...]
        nm = ADAM_B1 * m_ref[...] + (1.0 - ADAM_B1) * gv
        nv = ADAM_B2 * v_ref[...] + (1.0 - ADAM_B2) * (gv * gv)
        d_ref[...] = -ADAM_LR * ((nm / c1) / (jnp.sqrt(nv / c2) + ADAM_EPS) + ADAM_WD * w_ref[...])
        nm_ref[...] = nm
        nv_ref[...] = nv
        go_ref[...] = gv

    lay = pl.BlockSpec((None, tr, c), lambda i: (layer, i, 0))
    in_specs = [lay, pl.BlockSpec((tr, c), lambda i: (i, 0)), lay, lay]
    args = [w, g, m, v]
    aliases = {}
    if bufs is not None:
        in_specs += [pl.BlockSpec(memory_space=pl.ANY)] * 4
        args += list(bufs)
        aliases = {4 + k: k for k in range(4)}
    return pl.pallas_call(
        body, name=name, grid=(r // tr,), in_specs=in_specs, out_specs=[lay] * 4,
        out_shape=[jax.ShapeDtypeStruct((depth, r, c), F32)] * 4, input_output_aliases=aliases,
        compiler_params=_params(("parallel",)),
    )(*args)


def _mesh_pos():
    return lax.axis_index("x"), lax.axis_index("y"), lax.axis_index("c")


def _flip(v, bit):
    return 1 - v if bit else v


def _gather_rows(shards, name):
    n = len(shards)
    any_spec = pl.BlockSpec(memory_space=pl.ANY)

    def body(*refs):
        ins, outs = refs[:n], refs[n:2 * n]
        send_sems, recv_sems, local_sems = refs[2 * n:]
        x, y, c = _mesh_pos()
        others = [(1 - x, y), (x, 1 - y), (1 - x, 1 - y)]
        sibling = (x, y, 1 - c)

        def part(t, chip, core):
            half = ins[t].shape[0] // 2
            return outs[t].at[chip[0] * 2 + chip[1], pl.ds(core * half, half), :]

        def copy(t, sem, src, dst, to):
            return pltpu.make_async_remote_copy(src_ref=src, dst_ref=dst, send_sem=send_sems.at[t, sem],
                                                recv_sem=recv_sems.at[t, sem], device_id=to, device_id_type=MESH)

        local = [pltpu.make_async_copy(ins[t], outs[t].at[2 * x + y], local_sems.at[t]) for t in range(n)]
        for cp in local:
            cp.start()
        started = []
        for t in range(n):
            half = ins[t].shape[0] // 2
            for j, chip in enumerate(others):
                cp = copy(t, j, ins[t].at[pl.ds(c * half, half), :], part(t, (x, y), c), (*chip, c))
                cp.start()
                started.append(cp)
        for t in range(n):
            for j, chip in enumerate(others):
                copy(t, j, part(t, chip, c), part(t, chip, c), (*chip, c)).wait_recv()
                cp = copy(t, 3 + j, part(t, chip, c), part(t, chip, c), sibling)
                cp.start()
                started.append(cp)
        for t in range(n):
            for j, chip in enumerate(others):
                copy(t, 3 + j, part(t, chip, 1 - c), part(t, chip, 1 - c), sibling).wait_recv()
        for cp in started:
            cp.wait_send()
        for cp in local:
            cp.wait()

    return pl.pallas_call(
        body, name=name, in_specs=[any_spec] * n, out_specs=[any_spec] * n,
        out_shape=[jax.ShapeDtypeStruct((N_CHIPS,) + sh.shape, sh.dtype) for sh in shards],
        scratch_shapes=[pltpu.SemaphoreType.DMA((n, 6)), pltpu.SemaphoreType.DMA((n, 6)),
                        pltpu.SemaphoreType.DMA((n,))],
    )(*shards)


def _sum_parts(parts, name):
    _, r, c = parts.shape
    tr = _div_tile(r, 256, 16)

    def body(p_ref, o_ref):
        acc = p_ref[0].astype(F32)
        for src in range(1, N_DEVICES):
            acc = acc + p_ref[src].astype(F32)
        o_ref[...] = acc

    return pl.pallas_call(
        body, name=name, grid=(r // tr,), in_specs=[pl.BlockSpec((N_DEVICES, tr, c), lambda i: (0, i, 0))],
        out_specs=pl.BlockSpec((tr, c), lambda i: (i, 0)), out_shape=jax.ShapeDtypeStruct((r, c), F32),
        compiler_params=_params(("parallel",)),
    )(parts)


def _split_start(srcs, lands, plan, ncopies, name):
    nbuf = len(srcs) + len(lands)

    def body(*refs):
        bufs = refs[:nbuf]
        send_sem, recv_sem, token = refs[nbuf], refs[nbuf + 1], refs[-1]
        for k, (src, dst, dev) in enumerate(plan(bufs[:len(srcs)], bufs[len(srcs):])):
            pltpu.make_async_remote_copy(src_ref=src, dst_ref=dst, send_sem=send_sem.at[k], recv_sem=recv_sem.at[k],
                                         device_id=dev, device_id_type=MESH).start()
        token[...] = jnp.zeros_like(token)

    hbm = pl.BlockSpec(memory_space=pltpu.HBM)
    sem = pl.BlockSpec(memory_space=pltpu.SEMAPHORE)
    operands = [pltpu.with_memory_space_constraint(a, pltpu.HBM) for a in (*srcs, *lands)]
    outs = pl.pallas_call(
        body, name=name, in_specs=[hbm] * nbuf,
        out_specs=(sem, sem, *[hbm] * nbuf, pl.BlockSpec(memory_space=pltpu.VMEM)),
        out_shape=(pltpu.SemaphoreType.DMA((ncopies,)), pltpu.SemaphoreType.DMA((ncopies,)),
                   *[pltpu.HBM(a.shape, a.dtype) for a in operands], jax.ShapeDtypeStruct((8, LANES), F32)),
        input_output_aliases={i: 2 + i for i in range(nbuf)},
        compiler_params=pltpu.CompilerParams(has_side_effects=pltpu.SideEffectType.DATAFLOW_SIDE_EFFECTING),
    )(*operands)
    handle = dict(send=outs[0], recv=outs[1], bufs=list(outs[2:2 + nbuf]), nsrc=len(srcs), plan=plan)
    return handle, outs[-1]


def _split_wait(handle, after, name):
    nbuf, nsrc, plan = len(handle["bufs"]), handle["nsrc"], handle["plan"]

    def body(*refs):
        bufs = refs[:nbuf]
        send_sem, recv_sem = refs[nbuf], refs[nbuf + 1]
        for k, (src, dst, dev) in enumerate(plan(bufs[:nsrc], bufs[nsrc:])):
            copy = pltpu.make_async_remote_copy(src_ref=src, dst_ref=dst, send_sem=send_sem.at[k],
                                                recv_sem=recv_sem.at[k], device_id=dev, device_id_type=MESH)
            copy.wait_send()
            copy.wait_recv()

    hbm = pl.BlockSpec(memory_space=pltpu.HBM)
    sem = pl.BlockSpec(memory_space=pltpu.SEMAPHORE)
    outs = pl.pallas_call(
        body, name=name, in_specs=[hbm] * nbuf + [sem, sem, pl.BlockSpec(memory_space=pl.ANY)],
        out_specs=[hbm] * nbuf, out_shape=[pltpu.HBM(a.shape, a.dtype) for a in handle["bufs"]],
        input_output_aliases={i: i for i in range(nbuf)},
        compiler_params=pltpu.CompilerParams(has_side_effects=pltpu.SideEffectType.DATAFLOW_SIDE_EFFECTING),
    )(*handle["bufs"], handle["send"], handle["recv"], after)
    return list(outs[nsrc:])


def _own_slot(shape, dtype, block, index):
    return lax.dynamic_update_slice(lax.empty(shape, dtype), block[None], (index,) + (0,) * block.ndim)


def _gather_plan(srcs, lands):
    x, y, c = _mesh_pos()
    return [(src, land.at[2 * x + y], (*chip, c))
            for src, land in zip(srcs, lands) for chip in ((1 - x, y), (x, 1 - y), (1 - x, 1 - y))]


def _scatter_plan(srcs, lands):
    x, y, c = _mesh_pos()
    out = []
    for src, land in zip(srcs, lands):
        half = src.shape[1] // 2
        for d in range(1, N_DEVICES):
            p = (_flip(x, d & 4), _flip(y, d & 2), _flip(c, d & 1))
            out.append((src.at[2 * p[0] + p[1], pl.ds(p[2] * half, half), :], land.at[4 * x + 2 * y + c], p))
    return out


def _swap_plan(srcs, lands):
    x, y, c = _mesh_pos()
    return [(src, land.at[c], (x, y, 1 - c)) for src, land in zip(srcs, lands)]


def _allreduce_small(flat, name):
    r = flat.shape[0]

    def body(x_ref, o_ref, buf, send_sems, recv_sems):
        x, y, c = _mesh_pos()
        me = 4 * x + 2 * y + c
        buf[me] = x_ref[...]
        started = []
        peers = [(_flip(x, d & 4), _flip(y, d & 2), _flip(c, d & 1)) for d in range(1, N_DEVICES)]
        for d, p in enumerate(peers):
            cp = pltpu.make_async_remote_copy(src_ref=x_ref, dst_ref=buf.at[me], send_sem=send_sems.at[d],
                                              recv_sem=recv_sems.at[d], device_id=p, device_id_type=MESH)
            cp.start()
            started.append(cp)
        for d, p in enumerate(peers):
            slot = buf.at[4 * p[0] + 2 * p[1] + p[2]]
            pltpu.make_async_remote_copy(src_ref=slot, dst_ref=slot, send_sem=send_sems.at[d], recv_sem=recv_sems.at[d],
                                         device_id=p, device_id_type=MESH).wait_recv()
        for cp in started:
            cp.wait_send()
        acc = buf[0]
        for src in range(1, N_DEVICES):
            acc = acc + buf[src]
        o_ref[...] = acc

    vm = pl.BlockSpec(memory_space=pltpu.VMEM)
    return pl.pallas_call(
        body, name=name, in_specs=[vm], out_specs=vm, out_shape=jax.ShapeDtypeStruct((r, LANES), F32),
        scratch_shapes=[pltpu.VMEM((N_DEVICES, r, LANES), F32), pltpu.SemaphoreType.DMA((N_DEVICES - 1,)),
                        pltpu.SemaphoreType.DMA((N_DEVICES - 1,))],
        compiler_params=pltpu.CompilerParams(vmem_limit_bytes=VMEM_LIMIT_BYTES),
    )(flat)


def _bucket_ids(dil):
    rel = (np.arange(BLOCK)[:, None] + BLOCK - np.arange(2 * BLOCK)[None, :]) * dil
    max_exact = N_BUCKETS // 2
    d = np.maximum(rel, 0)
    large = max_exact + (np.log(np.maximum(d, 1).astype(np.float32) / max_exact)
                         / np.float32(np.log(T5_MAX_DIST / max_exact)) * (N_BUCKETS - max_exact)).astype(np.int32)
    large = np.minimum(large, N_BUCKETS - 1)
    return np.where(d < max_exact, d, large).astype(np.int32)


def _block_bias(table, dil):
    onehot = (jnp.asarray(_bucket_ids(dil))[:, :, None] == jnp.arange(N_BUCKETS)[None, None, :]).astype(F32)
    return jnp.einsum("ijb,bh->hij", onehot, table.astype(F32), precision=lax.Precision.HIGHEST)


def _tile_gain(g, n):
    return jnp.tile(g.reshape(1, HEAD_DIM), (1, n))


def _layer_fwd(x, p, cfg):
    h1 = _rmsnorm_fwd(x, p["attn_norm"], "attn_norm_fwd")
    proj = _matmul(h1, p["w_in_t"], "nt", F32, "in_proj", tm=1024, tn=768, tk=2048)
    aq, ak, av, bq, bk, bv, cq, ck, cv = _qk_prep(proj, p["gains"], cfg, "qk_prep")
    oa, lse_a = _banded_fwd(aq, ak, av, p["bias_a"], p["sinks"], cfg.nha, cfg.nkva, WINDOW_A - 1, 1, "swa_fwd")
    bkt = _keys_on_lanes(bk, SB_CHUNK)
    ob, tot_b = _sb_fwd(bq, bkt, bv, "stickbreak_fwd")
    ocs, lses = [], []
    for (window, dil), bias in zip(DILATED_PAIRS, p["bias_c"]):
        o, l = _banded_fwd(cq, ck, cv, bias, None, cfg.nhc, cfg.nhc, window // dil, dil, f"dilated{dil}_fwd")
        ocs.append(o)
        lses.append(l)
    mix, oc, lse_c = _mix_fwd(oa, ob, ocs, lses, p["mix_gain"], cfg, "mix_fwd")
    xm = _matmul(mix, p["w_out"], "nn", F32, "out_proj", tm=1024, tn=512, tk=2048, residual=x)
    h2 = _rmsnorm_fwd(xm, p["ffn_norm"], "ffn_norm_fwd")
    u = _matmul(h2, p["w_up_t"], "nt", F32, "up_proj", tm=1024, tn=512, tk=2048)
    act = _conv_act_fwd(u, p["conv_w"], p["conv_b"], cfg.f, "conv_act_fwd")
    y = _matmul(act, p["w_down"], "nn", F32, "down_proj", tm=1024, tn=1024, tk=512, residual=xm)
    saved = dict(x=x, h1=h1, proj=proj, q=(aq, ak, av, bq, bk, bv, cq, ck, cv), oa=oa, lse_a=lse_a, ob=ob,
                 tot_b=tot_b, bkt=bkt, oc=oc, lse_c=lse_c, mix=mix, xm=xm, h2=h2, u=u, act=act)
    return y, saved


def _layer_bwd(dy, sv, p, dbias, cfg):
    aq, ak, av, bq, bk, bv, cq, ck, cv = sv["q"]
    g_down = _matmul(sv["act"], dy, "tn", BF16, "down_proj_dw", tm=1408, tn=2048, tk=512)
    dact = _matmul(dy, p["w_down"], "nt", F32, "down_proj_dx", tm=1024, tn=512, tk=2048)
    dug, duu, dwg, dwu, dbg, dbu = _conv_act_bwd(sv["u"], dact, p["conv_w"], p["conv_b"], cfg.f, "conv_act_bwd")
    du = jnp.concatenate([dug, duu], axis=1)
    g_up_t = _matmul(du, sv["h2"], "tn", BF16, "up_proj_dw", tm=1408, tn=2048, tk=512)
    dh2 = _matmul(du, p["w_up_t"], "nn", F32, "up_proj_dx", tm=1024, tn=2048, tk=512)
    dxm, g_ffn_norm = _rmsnorm_bwd(sv["xm"], p["ffn_norm"], dh2, dy, "ffn_norm_bwd")
    g_out = _matmul(sv["mix"], dxm, "tn", BF16, "out_proj_dw", tm=1024, tn=2048, tk=512)
    dmix = _matmul(dxm, p["w_out"], "nt", F32, "out_proj_dx", tm=1024, tn=512, tk=2048)
    doa, dob, doc, g_mix_gain = _mix_bwd(dmix, sv["oa"], sv["ob"], sv["oc"], p["mix_gain"], cfg, "mix_bwd")
    daq, dak, dav, dbias_a, g_sinks = _banded_bwd(aq, ak, av, sv["oa"], sv["lse_a"], doa, p["bias_a"], p["sinks"],
                                                 dbias[0], cfg.nha, cfg.nkva, WINDOW_A - 1, 1, "swa_bwd")
    dbq, dbk, dbv = _sb_bwd(bq, bk, sv["bkt"], bv, sv["tot_b"], dob, "stickbreak_bwd")
    dcq, dck, dcv, dbias_c = [], [], [], []
    for idx, ((window, dil), bias) in enumerate(zip(DILATED_PAIRS, p["bias_c"])):
        a, b, c, d, _ = _banded_bwd(cq, ck, cv, sv["oc"], sv["lse_c"], doc, bias, None, dbias[1][idx], cfg.nhc,
                                    cfg.nhc, window // dil, dil, f"dilated{dil}_bwd")
        dcq.append(a)
        dck.append(b)
        dcv.append(c)
        dbias_c.append(d)
    dproj, g_aq, g_ak, g_cq, g_ck = _qk_prep_bwd(
        sv["proj"], p["gains"], [[daq], [dak], [dav], [dbq], [dbk], [dbv], dcq, dck, dcv], cfg, "qk_prep_bwd")
    g_in_t = _matmul(dproj, sv["h1"], "tn", BF16, "in_proj_dw", tm=768, tn=2048, tk=512)
    dh1 = _matmul(dproj, p["w_in_t"], "nn", F32, "in_proj_dx", tm=1024, tn=2048, tk=768)
    dx, g_attn_norm = _rmsnorm_bwd(sv["x"], p["attn_norm"], dh1, dxm, "attn_norm_bwd")

    def fold(g):
        return jnp.sum(g.reshape(-1, HEAD_DIM), axis=0)

    small = dict(attn_norm=g_attn_norm[0], a_q_gain=fold(g_aq), a_k_gain=fold(g_ak), a_sinks=g_sinks,
                 c_q_gain=fold(g_cq), c_k_gain=fold(g_ck), mix_out_gain=g_mix_gain[0], ffn_norm=g_ffn_norm[0],
                 conv_w=jnp.concatenate([dwg, dwu], axis=1), conv_b=jnp.concatenate([dbg, dbu], axis=1)[0])
    big = (g_in_t, g_out, g_up_t, g_down)
    return dx, big, small, (dbias_a, dbias_c)


_SMALL = ("attn_norm", "a_q_gain", "a_k_gain", "a_sinks", "c_q_gain", "c_k_gain", "rel_bias_table", "mix_out_gain",
          "ffn_norm", "conv_w", "conv_b")


def _pack(arrays):
    flat = jnp.concatenate([a.reshape(-1).astype(F32) for a in arrays])
    pad = (-flat.shape[0]) % (8 * LANES)
    return jnp.pad(flat, (0, pad)).reshape(-1, LANES)


def _unpack(flat, shapes):
    flat = flat.reshape(-1)
    out, pos = [], 0
    for sh in shapes:
        n = int(np.prod(sh))
        out.append(flat[pos:pos + n].reshape(sh))
        pos += n
    return out


def kernel(x, attn_norm, w_in, a_q_gain, a_k_gain, a_sinks, c_q_gain, c_k_gain, rel_bias_table, mix_out_gain, w_out, ffn_norm, w_up, conv_w, conv_b, w_down, loss_target, m_attn_norm, m_w_in, m_a_q_gain, m_a_k_gain, m_a_sinks, m_c_q_gain, m_c_k_gain, m_rel_bias_table, m_mix_out_gain, m_w_out, m_ffn_norm, m_w_up, m_conv_w, m_conv_b, m_w_down, v_attn_norm, v_w_in, v_a_q_gain, v_a_k_gain, v_a_sinks, v_c_q_gain, v_c_k_gain, v_rel_bias_table, v_mix_out_gain, v_w_out, v_ffn_norm, v_w_up, v_conv_w, v_conv_b, v_w_down):
    depth, d = attn_norm.shape
    f = w_down.shape[1] * N_CHIPS
    cfg = _Cfg(d, f)
    chip = 2 * lax.axis_index("x") + lax.axis_index("y")

    cw_cols = conv_w.shape[2]
    cw_flat = conv_w.reshape(-1)
    cw_rows = -(-cw_flat.shape[0] // (16 * LANES)) * 16
    cw_pad = jnp.pad(cw_flat, (0, cw_rows * LANES - cw_flat.shape[0])).reshape(cw_rows, LANES)
    (cw_all,) = _gather_rows([cw_pad], "gather_conv_w")
    cw_all = cw_all.reshape(N_CHIPS, -1)[:, :cw_flat.shape[0]].reshape(N_CHIPS, depth, CONV_WIDTH, cw_cols)
    conv_w_full = jnp.transpose(cw_all, (1, 2, 0, 3)).reshape(depth, CONV_WIDTH, N_CHIPS * cw_cols)

    table_a, table_c = rel_bias_table[:, :cfg.nha], rel_bias_table[:, cfg.nha:]
    bias_a = _block_bias(table_a, 1)
    bias_c = [_block_bias(table_c, dil) for _, dil in DILATED_PAIRS]

    layers, gathers, anchor = [], [], 0.0
    for l in range(depth):
        shards = [w_in[l].T.astype(BF16), w_out[l].astype(BF16), w_up[l].T.astype(BF16), w_down[l].astype(BF16)]
        lands = [_own_slot((N_CHIPS,) + sh.shape, BF16, sh, chip) for sh in shards]
        handle, token = _split_start(shards, lands, _gather_plan, 3 * len(shards), f"gather_start_{l}")
        gathers.append(handle)
        anchor = anchor + token[0, 0]
        layers.append(dict(
            attn_norm=attn_norm[l].reshape(1, d), ffn_norm=ffn_norm[l].reshape(1, d),
            mix_gain=mix_out_gain[l].reshape(1, d),
            gains=(_tile_gain(a_q_gain[l], cfg.nha), _tile_gain(a_k_gain[l], cfg.nkva),
                   _tile_gain(c_q_gain[l], cfg.nhc), _tile_gain(c_k_gain[l], cfg.nhc)),
            sinks=a_sinks[l], bias_a=bias_a, bias_c=bias_c, conv_w=conv_w_full[l],
            conv_b=conv_b[l].reshape(1, 2 * f)))

    act = x[0]
    saved = []
    for l in range(depth):
        after = act if l else layers[0]["attn_norm"] + anchor
        w_in_t, w_out_f, w_up_t, w_down_f = [g.reshape(-1, g.shape[-1])
                                             for g in _split_wait(gathers[l], after, f"gather_wait_{l}")]
        layers[l].update(w_in_t=w_in_t, w_out=w_out_f, w_up_t=w_up_t, w_down=w_down_f)
        act, sv = _layer_fwd(act, layers[l], cfg)
        saved.append(sv)
    dact, loss_blk = _loss_head(act, loss_target[0], "loss_head")
    loss = lax.psum(loss_blk[0, 0], ("x", "y", "c"))

    me = 2 * chip + lax.axis_index("c")

    def start_scatter(l, big):
        srcs = [g.reshape(N_CHIPS, -1, g.shape[-1]) for g in big]
        lands = []
        for g in srcs:
            half = g.shape[1] // 2
            own = lax.dynamic_slice(g, (chip, lax.axis_index("c") * half, 0), (1, half, g.shape[2]))[0]
            lands.append(_own_slot((N_DEVICES, half, g.shape[2]), g.dtype, own, me))
        return _split_start(srcs, lands, _scatter_plan, (N_DEVICES - 1) * len(srcs), f"scatter_start_{l}")

    def finish_scatter(l, handle, after):
        parts = _split_wait(handle, after, f"scatter_wait_{l}")
        halves = [_sum_parts(pt, f"sum_grads_{t}") for t, pt in enumerate(parts)]
        lands = [_own_slot((2,) + h.shape, h.dtype, h, lax.axis_index("c")) for h in halves]
        return _split_start(halves, lands, _swap_plan, len(halves), f"swap_start_{l}")[0]

    dbias = (jnp.zeros_like(bias_a), [jnp.zeros_like(b) for b in bias_c])
    small_grads = [None] * depth
    swaps = [None] * depth
    pending = None
    for l in reversed(range(depth)):
        dact, big, small_grads[l], dbias = _layer_bwd(dact, saved[l], layers[l], dbias, cfg)
        if pending is not None:
            swaps[l + 1] = finish_scatter(l + 1, pending, dact)
        pending, token = start_scatter(l, big)
        if l:
            layers[l - 1]["conv_b"] = layers[l - 1]["conv_b"] + token[0, 0]
    grad_x = dact[None]

    tabs = _bias_table_grad([dbias[0]] + dbias[1], [jnp.asarray(_bucket_ids(1))]
                            + [jnp.asarray(_bucket_ids(dil)) for _, dil in DILATED_PAIRS], "bias_table_grad")
    g_table_a = tabs[0][:, :N_BUCKETS].T
    g_table_c = (tabs[1] + tabs[2] + tabs[3])[:, :N_BUCKETS].T
    g_table = jnp.concatenate([g_table_a, g_table_c], axis=1)
    small_local = {k: jnp.stack([small_grads[l][k] for l in range(depth)]) for k in _SMALL if k != "rel_bias_table"}
    small_local["rel_bias_table"] = g_table
    shapes = [small_local[k].shape for k in _SMALL]
    reduced = dict(zip(_SMALL, _unpack(_allreduce_small(_pack([small_local[k] for k in _SMALL]), "allreduce_small"),
                                       shapes)))
    reduced["conv_w"] = lax.dynamic_slice_in_dim(reduced["conv_w"], chip * cw_cols, cw_cols, axis=2)

    given = dict(attn_norm=attn_norm, a_q_gain=a_q_gain, a_k_gain=a_k_gain, a_sinks=a_sinks, c_q_gain=c_q_gain,
                 c_k_gain=c_k_gain, rel_bias_table=rel_bias_table, mix_out_gain=mix_out_gain, ffn_norm=ffn_norm,
                 conv_w=conv_w, conv_b=conv_b)
    moms = dict(attn_norm=(m_attn_norm, v_attn_norm), a_q_gain=(m_a_q_gain, v_a_q_gain),
                a_k_gain=(m_a_k_gain, v_a_k_gain), a_sinks=(m_a_sinks, v_a_sinks), c_q_gain=(m_c_q_gain, v_c_q_gain),
                c_k_gain=(m_c_k_gain, v_c_k_gain), rel_bias_table=(m_rel_bias_table, v_rel_bias_table),
                mix_out_gain=(m_mix_out_gain, v_mix_out_gain), ffn_norm=(m_ffn_norm, v_ffn_norm),
                conv_w=(m_conv_w, v_conv_w), conv_b=(m_conv_b, v_conv_b))
    sshapes = [given[k].shape for k in _SMALL]
    s_delta, s_m, s_v = _adamw(_pack([given[k] for k in _SMALL]), _pack([reduced[k] for k in _SMALL]),
                               _pack([moms[k][0] for k in _SMALL]), _pack([moms[k][1] for k in _SMALL]), "adamw_small")
    grads = dict(reduced)
    deltas = dict(zip(_SMALL, _unpack(s_delta, sshapes)))
    new_m = dict(zip(_SMALL, _unpack(s_m, sshapes)))
    new_v = dict(zip(_SMALL, _unpack(s_v, sshapes)))

    big_given = dict(w_in=(w_in, m_w_in, v_w_in, True), w_out=(w_out, m_w_out, v_w_out, False),
                     w_up=(w_up, m_w_up, v_w_up, True), w_down=(w_down, m_w_down, v_w_down, False))
    names = ("w_in", "w_out", "w_up", "w_down")
    bufs = {name: None for name in names}
    after = s_delta
    for l in reversed(range(depth)):
        if l == 0:
            swaps[0] = finish_scatter(0, pending, after)
        layer_grads = [g.reshape(-1, g.shape[-1]) for g in _split_wait(swaps[l], after, f"swap_wait_{l}")]
        for t, name in enumerate(names):
            wt, mt, vt, transposed = big_given[name]
            g = layer_grads[t].T if transposed else layer_grads[t]
            bufs[name] = _adamw_layer(l, wt, g, mt, vt, bufs[name], f"adamw_{name}_{l}")
            after = bufs[name][1]
    for name in names:
        grads[name], deltas[name], new_m[name], new_v[name] = bufs[name]

    order = ("attn_norm", "w_in", "a_q_gain", "a_k_gain", "a_sinks", "c_q_gain", "c_k_gain", "rel_bias_table",
             "mix_out_gain", "w_out", "ffn_norm", "w_up", "conv_w", "conv_b", "w_down")
    return (loss, grad_x, *[grads[k] for k in order], *[deltas[k] for k in order], *[new_m[k] for k in order],
            *[new_v[k] for k in order])
```

```python
import numpy as np
import jax
import jax.numpy as jnp
from jax import lax
from jax.experimental import pallas as pl
from jax.experimental.pallas import tpu as pltpu

F32 = jnp.float32
BF16 = jnp.bfloat16
MESH = pl.DeviceIdType.MESH

HEAD_DIM = 64
BLOCK = 128
LANES = 128
EPS = 1e-6
NEG_INF = -1e30
WINDOW_A = 128
DILATED_PAIRS = ((128, 1), (512, 4), (2048, 16))
N_BUCKETS = 32
T5_MAX_DIST = 2048
CONV_WIDTH = 3
ADAM_LR = 0.001
ADAM_B1 = 0.9
ADAM_B2 = 0.999
ADAM_EPS = 1e-08
ADAM_WD = 0.01
ADAM_STEP = 10
N_CHIPS = 4
N_DEVICES = 8
VMEM_LIMIT_BYTES = 48 * 1024 * 1024
QK_SCALE = HEAD_DIM ** -0.5


def _params(sem=None):
    return pltpu.CompilerParams(dimension_semantics=sem, vmem_limit_bytes=VMEM_LIMIT_BYTES)


def _div_tile(n, cap, mult):
    best = None
    for t in range(mult, min(n, cap) + 1, mult):
        if n % t == 0:
            best = t
    return n if best is None else best


def _dot(a, b):
    return lax.dot_general(a, b, (((1,), (0,)), ((), ())), preferred_element_type=F32)


def _dot_nt(a, b):
    return lax.dot_general(a, b, (((1,), (1,)), ((), ())), preferred_element_type=F32)


def _dot_tn(a, b):
    return lax.dot_general(a, b, (((0,), (0,)), ((), ())), preferred_element_type=F32)


def _split_dot(x, m):
    hi = x.astype(BF16)
    lo = (x - hi.astype(F32)).astype(BF16)
    return _dot(hi, m) + _dot(lo, m)


class _Cfg:
    def __init__(self, d_model, d_ff):
        nh = d_model // HEAD_DIM
        self.d = d_model
        self.f = d_ff
        self.nha = nh // 4
        self.nkva = self.nha // 4
        self.nhb = nh // 4
        self.nhc = nh // 2
        self.a_q = self.nha * HEAD_DIM
        self.a_kv = self.nkva * HEAD_DIM
        self.b_w = self.nhb * HEAD_DIM
        self.c_w = self.nhc * HEAD_DIM
        sizes = [self.a_q, self.a_kv, self.a_kv, self.b_w, self.b_w, self.b_w, self.c_w, self.c_w, self.c_w]
        starts = [0] + [int(s) for s in np.cumsum(sizes)[:-1]]
        self.sections = list(zip(starts, sizes))
        self.in_width = int(sum(sizes))
        assert all(s % LANES == 0 for s in sizes)


def _matmul(a, b, mode, out_dtype, name, tm=512, tn=512, tk=512, residual=None):
    if mode == "tn":
        kdim, m = a.shape
    else:
        m, kdim = a.shape
    n = b.shape[0] if mode == "nt" else b.shape[1]
    tm, tn, tk = _div_tile(m, tm, LANES), _div_tile(n, tn, LANES), _div_tile(kdim, tk, LANES)
    nk = kdim // tk
    if mode == "tn":
        a_spec = pl.BlockSpec((tk, tm), lambda i, j, k: (k, i))
    else:
        a_spec = pl.BlockSpec((tm, tk), lambda i, j, k: (i, k))
    if mode == "nt":
        b_spec = pl.BlockSpec((tn, tk), lambda i, j, k: (j, k))
    else:
        b_spec = pl.BlockSpec((tk, tn), lambda i, j, k: (k, j))
    dot = {"nn": _dot, "nt": _dot_nt, "tn": _dot_tn}[mode]
    o_spec = pl.BlockSpec((tm, tn), lambda i, j, k: (i, j))
    in_specs = [a_spec, b_spec]
    args = [a, b]
    if residual is not None:
        in_specs.append(o_spec)
        args.append(residual)

    def body(*refs):
        if residual is None:
            a_ref, b_ref, o_ref, acc = refs
        else:
            a_ref, b_ref, r_ref, o_ref, acc = refs
        k = pl.program_id(2)

        @pl.when(k == 0)
        def _():
            acc[...] = jnp.zeros_like(acc)

        acc[...] += dot(a_ref[...].astype(BF16), b_ref[...].astype(BF16))

        @pl.when(k == nk - 1)
        def _():
            r = acc[...]
            if residual is not None:
                r = r + r_ref[...]
            o_ref[...] = r.astype(out_dtype)

    return pl.pallas_call(
        body, name=name, grid=(m // tm, n // tn, nk), in_specs=in_specs, out_specs=o_spec,
        out_shape=jax.ShapeDtypeStruct((m, n), out_dtype), scratch_shapes=[pltpu.VMEM((tm, tn), F32)],
        compiler_params=_params(("parallel", "parallel", "arbitrary")),
    )(*args)


def _rmsnorm_fwd(x, g, name):
    s, d = x.shape
    ts = _div_tile(s, 256, 8)

    def body(x_ref, g_ref, o_ref):
        xv = x_ref[...]
        r = lax.rsqrt(jnp.mean(xv * xv, axis=-1, keepdims=True) + EPS)
        o_ref[...] = (xv * r * g_ref[...]).astype(BF16)

    return pl.pallas_call(
        body, name=name, grid=(s // ts,),
        in_specs=[pl.BlockSpec((ts, d), lambda i: (i, 0)), pl.BlockSpec((1, d), lambda i: (0, 0))],
        out_specs=pl.BlockSpec((ts, d), lambda i: (i, 0)), out_shape=jax.ShapeDtypeStruct((s, d), BF16),
        compiler_params=_params(("parallel",)),
    )(x, g)


def _rmsnorm_bwd(x, g, dh, dres, name):
    s, d = x.shape
    ts = _div_tile(s, 256, 8)

    def body(x_ref, g_ref, dh_ref, dres_ref, dx_ref, dg_ref):
        @pl.when(pl.program_id(0) == 0)
        def _():
            dg_ref[...] = jnp.zeros_like(dg_ref)

        xv = x_ref[...]
        r = lax.rsqrt(jnp.mean(xv * xv, axis=-1, keepdims=True) + EPS)
        xhat = xv * r
        dhv = dh_ref[...]
        dxhat = dhv * g_ref[...]
        dx_ref[...] = dres_ref[...] + r * (dxhat - xhat * jnp.mean(dxhat * xhat, axis=-1, keepdims=True))
        dg_ref[...] += jnp.sum(dhv * xhat, axis=0, keepdims=True)

    row = pl.BlockSpec((ts, d), lambda i: (i, 0))
    vec = pl.BlockSpec((1, d), lambda i: (0, 0))
    return pl.pallas_call(
        body, name=name, grid=(s // ts,), in_specs=[row, vec, row, row], out_specs=[row, vec],
        out_shape=[jax.ShapeDtypeStruct((s, d), F32), jax.ShapeDtypeStruct((1, d), F32)],
        compiler_params=_params(("arbitrary",)),
    )(x, g, dh, dres)


def _head_mean_matrix():
    idx = np.arange(LANES) // HEAD_DIM
    return jnp.asarray((idx[:, None] == idx[None, :]).astype(np.float32) / HEAD_DIM, dtype=BF16)


def _head_mean(y, m128):
    w = y.shape[1]
    parts = [_split_dot(y[:, c * LANES:(c + 1) * LANES], m128) for c in range(w // LANES)]
    return parts[0] if len(parts) == 1 else jnp.concatenate(parts, axis=1)


_NORMED_SECTIONS = (0, 1, 6, 7)


def _qk_prep(proj, gains, cfg, name):
    s = proj.shape[0]
    ts = _div_tile(s, 256, 16)
    m128 = _head_mean_matrix()

    def body(p_ref, m_ref, g0, g1, g6, g7, *outs):
        gref = dict(zip(_NORMED_SECTIONS, (g0, g1, g6, g7)))
        for idx, (st, w) in enumerate(cfg.sections):
            xv = p_ref[:, st:st + w]
            if idx in gref:
                r = lax.rsqrt(_head_mean(xv * xv, m_ref[...]) + EPS)
                xv = xv * r * gref[idx][...]
            outs[idx][...] = xv.astype(BF16)

    in_specs = [pl.BlockSpec((ts, cfg.in_width), lambda i: (i, 0)), pl.BlockSpec((LANES, LANES), lambda i: (0, 0))]
    in_specs += [pl.BlockSpec((1, cfg.sections[k][1]), lambda i: (0, 0)) for k in _NORMED_SECTIONS]
    out_specs = [pl.BlockSpec((ts, w), lambda i: (i, 0)) for _, w in cfg.sections]
    out_shape = [jax.ShapeDtypeStruct((s, w), BF16) for _, w in cfg.sections]
    return pl.pallas_call(
        body, name=name, grid=(s // ts,), in_specs=in_specs, out_specs=out_specs, out_shape=out_shape,
        compiler_params=_params(("parallel",)),
    )(proj, m128, *gains)


def _qk_prep_bwd(proj, gains, grads, cfg, name):
    s = proj.shape[0]
    ts = _div_tile(s, 128, 16)
    m128 = _head_mean_matrix()
    counts = [len(gl) for gl in grads]
    flat = [g for gl in grads for g in gl]

    def body(*refs):
        p_ref, m_ref = refs[0], refs[1]
        gref = dict(zip(_NORMED_SECTIONS, refs[2:6]))
        g_in = refs[6:6 + len(flat)]
        dp_ref = refs[6 + len(flat)]
        dgain = dict(zip(_NORMED_SECTIONS, refs[7 + len(flat):]))

        @pl.when(pl.program_id(0) == 0)
        def _():
            for k in _NORMED_SECTIONS:
                dgain[k][...] = jnp.zeros_like(dgain[k])

        pos = 0
        for idx, (st, w) in enumerate(cfg.sections):
            dy = g_in[pos][...]
            for extra in g_in[pos + 1:pos + counts[idx]]:
                dy = dy + extra[...]
            pos += counts[idx]
            if idx in gref:
                xv = p_ref[:, st:st + w]
                r = lax.rsqrt(_head_mean(xv * xv, m_ref[...]) + EPS)
                xhat = xv * r
                dxhat = dy * gref[idx][...]
                dgain[idx][...] += jnp.sum(dy * xhat, axis=0, keepdims=True)
                dy = r * (dxhat - xhat * _head_mean(dxhat * xhat, m_ref[...]))
            dp_ref[:, st:st + w] = dy.astype(BF16)

    in_specs = [pl.BlockSpec((ts, cfg.in_width), lambda i: (i, 0)), pl.BlockSpec((LANES, LANES), lambda i: (0, 0))]
    in_specs += [pl.BlockSpec((1, cfg.sections[k][1]), lambda i: (0, 0)) for k in _NORMED_SECTIONS]
    for idx, (_, w) in enumerate(cfg.sections):
        in_specs += [pl.BlockSpec((ts, w), lambda i: (i, 0))] * counts[idx]
    out_specs = [pl.BlockSpec((ts, cfg.in_width), lambda i: (i, 0))]
    out_specs += [pl.BlockSpec((1, cfg.sections[k][1]), lambda i: (0, 0)) for k in _NORMED_SECTIONS]
    out_shape = [jax.ShapeDtypeStruct((s, cfg.in_width), BF16)]
    out_shape += [jax.ShapeDtypeStruct((1, cfg.sections[k][1]), F32) for k in _NORMED_SECTIONS]
    return pl.pallas_call(
        body, name=name, grid=(s // ts,), in_specs=in_specs, out_specs=out_specs, out_shape=out_shape,
        compiler_params=_params(("arbitrary",)),
    )(proj, m128, *gains, *flat)


def _band_masks(max_dist):
    row = lax.broadcasted_iota(jnp.int32, (BLOCK, BLOCK), 0)
    col = lax.broadcasted_iota(jnp.int32, (BLOCK, BLOCK), 1)
    return row + BLOCK - col <= max_dist, col <= row


def _banded_fwd(q, k, v, bias, sinks, hq, hk, max_dist, dil, name):
    s = q.shape[0]
    wq, wk, sd, grp = hq * HEAD_DIM, hk * HEAD_DIM, s // dil, hq // hk
    nb = sd // BLOCK
    has_sink = sinks is not None

    def body(*refs):
        if has_sink:
            q_ref, kp_ref, kc_ref, vp_ref, vc_ref, b_ref, s_ref, o_ref, l_ref = refs
        else:
            q_ref, kp_ref, kc_ref, vp_ref, vc_ref, b_ref, o_ref, l_ref = refs
        i = pl.program_id(1)
        mprev, mcur = _band_masks(max_dist)
        mprev = jnp.logical_and(mprev, i > 0)
        for h in range(hq):
            sq = slice(h * HEAD_DIM, (h + 1) * HEAD_DIM)
            sk = slice((h // grp) * HEAD_DIM, (h // grp + 1) * HEAD_DIM)
            qh = q_ref[:, sq]
            sp = jnp.where(mprev, _dot_nt(qh, kp_ref[:, sk]) * QK_SCALE + b_ref[h, :, 0:BLOCK], NEG_INF)
            sc = jnp.where(mcur, _dot_nt(qh, kc_ref[:, sk]) * QK_SCALE + b_ref[h, :, BLOCK:2 * BLOCK], NEG_INF)
            m = jnp.maximum(jnp.max(sp, axis=-1, keepdims=True), jnp.max(sc, axis=-1, keepdims=True))
            if has_sink:
                m = jnp.maximum(m, s_ref[h])
            pp = jnp.exp(sp - m)
            pc = jnp.exp(sc - m)
            den = jnp.sum(pp, axis=-1, keepdims=True) + jnp.sum(pc, axis=-1, keepdims=True)
            if has_sink:
                den = den + jnp.exp(s_ref[h] - m)
            acc = _dot(pp.astype(BF16), vp_ref[:, sk]) + _dot(pc.astype(BF16), vc_ref[:, sk])
            o_ref[:, sq] = acc / den
            l_ref[:, sq] = jnp.broadcast_to(m + jnp.log(den), (BLOCK, HEAD_DIM))

    qspec = pl.BlockSpec((BLOCK, wq), lambda r, i: (i, r))
    kprev = pl.BlockSpec((BLOCK, wk), lambda r, i: (jnp.maximum(i - 1, 0), r))
    kcur = pl.BlockSpec((BLOCK, wk), lambda r, i: (i, r))
    in_specs = [qspec, kprev, kcur, kprev, kcur, pl.BlockSpec((hq, BLOCK, 2 * BLOCK), lambda r, i: (0, 0, 0))]
    args = [q.reshape(sd, dil * wq), k.reshape(sd, dil * wk), k.reshape(sd, dil * wk),
            v.reshape(sd, dil * wk), v.reshape(sd, dil * wk), bias]
    if has_sink:
        in_specs.append(pl.BlockSpec(memory_space=pltpu.SMEM))
        args.append(sinks)
    out, lse = pl.pallas_call(
        body, name=name, grid=(dil, nb), in_specs=in_specs, out_specs=[qspec, qspec],
        out_shape=[jax.ShapeDtypeStruct((sd, dil * wq), F32)] * 2,
        compiler_params=_params(("parallel", "parallel")),
    )(*args)
    return out.reshape(s, wq), lse.reshape(s, wq)


def _banded_bwd(q, k, v, o, lse, do, bias, sinks, dbias_init, hq, hk, max_dist, dil, name):
    s = q.shape[0]
    wq, wk, sd, grp = hq * HEAD_DIM, hk * HEAD_DIM, s // dil, hq // hk
    nb = sd // BLOCK
    has_sink = sinks is not None

    def body(*refs):
        (q_ref, qn_ref, kp_ref, kc_ref, vp_ref, vc_ref, o_ref, on_ref, l_ref, ln_ref, do_ref, don_ref,
         b_ref, dbi_ref) = refs[:14]
        rest = refs[14:]
        if has_sink:
            s_ref, dq_ref, dk_ref, dv_ref, db_ref, ds_ref = rest
        else:
            dq_ref, dk_ref, dv_ref, db_ref = rest
        j = pl.program_id(1)

        @pl.when(jnp.logical_and(pl.program_id(0) == 0, j == 0))
        def _():
            db_ref[...] = dbi_ref[...]
            if has_sink:
                ds_ref[...] = jnp.zeros_like(ds_ref)

        mprev_static, mcur = _band_masks(max_dist)
        mprev = jnp.logical_and(mprev_static, j > 0)
        mnext = jnp.logical_and(mprev_static, j + 1 < nb)
        dk_acc = [jnp.zeros((BLOCK, HEAD_DIM), F32) for _ in range(hk)]
        dv_acc = [jnp.zeros((BLOCK, HEAD_DIM), F32) for _ in range(hk)]
        for h in range(hq):
            g = h // grp
            sq = slice(h * HEAD_DIM, (h + 1) * HEAD_DIM)
            sk = slice(g * HEAD_DIM, (g + 1) * HEAD_DIM)
            kp, kc, vp, vc = kp_ref[:, sk], kc_ref[:, sk], vp_ref[:, sk], vc_ref[:, sk]
            qh = q_ref[:, sq]
            doh = do_ref[:, sq]
            dohb = doh.astype(BF16)
            lcol = l_ref[:, h * HEAD_DIM:h * HEAD_DIM + 1]
            dcol = jnp.sum(doh * o_ref[:, sq], axis=-1, keepdims=True)
            sp = _dot_nt(qh, kp) * QK_SCALE + b_ref[h, :, 0:BLOCK]
            sc = _dot_nt(qh, kc) * QK_SCALE + b_ref[h, :, BLOCK:2 * BLOCK]
            pp = jnp.where(mprev, jnp.exp(sp - lcol), 0.0)
            pc = jnp.where(mcur, jnp.exp(sc - lcol), 0.0)
            dsp = pp * (_dot_nt(dohb, vp) - dcol)
            dsc = pc * (_dot_nt(dohb, vc) - dcol)
            dspb, dscb = dsp.astype(BF16), dsc.astype(BF16)
            dq_ref[:, sq] = (_dot(dspb, kp) + _dot(dscb, kc)) * QK_SCALE
            db_ref[h, :, 0:BLOCK] += dsp
            db_ref[h, :, BLOCK:2 * BLOCK] += dsc
            if has_sink:
                psink = jnp.exp(s_ref[h] - lcol)
                tot = jnp.sum(psink * dcol, axis=0, keepdims=True)
                ds_ref[h:h + 1, :] -= jnp.broadcast_to(tot, (1, LANES))
            qn = qn_ref[:, sq]
            don = don_ref[:, sq]
            donb = don.astype(BF16)
            lncol = ln_ref[:, h * HEAD_DIM:h * HEAD_DIM + 1]
            dncol = jnp.sum(don * on_ref[:, sq], axis=-1, keepdims=True)
            sn = _dot_nt(qn, kc) * QK_SCALE + b_ref[h, :, 0:BLOCK]
            pn = jnp.where(mnext, jnp.exp(sn - lncol), 0.0)
            dsn = pn * (_dot_nt(donb, vc) - dncol)
            dk_acc[g] = dk_acc[g] + (_dot_tn(dscb, qh) + _dot_tn(dsn.astype(BF16), qn)) * QK_SCALE
            dv_acc[g] = dv_acc[g] + _dot_tn(pc.astype(BF16), dohb) + _dot_tn(pn.astype(BF16), donb)
        for g in range(hk):
            sk = slice(g * HEAD_DIM, (g + 1) * HEAD_DIM)
            dk_ref[:, sk] = dk_acc[g]
            dv_ref[:, sk] = dv_acc[g]

    qcur = pl.BlockSpec((BLOCK, wq), lambda r, j: (j, r))
    qnext = pl.BlockSpec((BLOCK, wq), lambda r, j: (jnp.minimum(j + 1, nb - 1), r))
    kprev = pl.BlockSpec((BLOCK, wk), lambda r, j: (jnp.maximum(j - 1, 0), r))
    kcur = pl.BlockSpec((BLOCK, wk), lambda r, j: (j, r))
    bspec = pl.BlockSpec((hq, BLOCK, 2 * BLOCK), lambda r, j: (0, 0, 0))
    q2, k2, v2 = q.reshape(sd, dil * wq), k.reshape(sd, dil * wk), v.reshape(sd, dil * wk)
    o2, l2, do2 = o.reshape(sd, dil * wq), lse.reshape(sd, dil * wq), do.reshape(sd, dil * wq)
    in_specs = [qcur, qnext, kprev, kcur, kprev, kcur, qcur, qnext, qcur, qnext, qcur, qnext, bspec, bspec]
    args = [q2, q2, k2, k2, v2, v2, o2, o2, l2, l2, do2, do2, bias, dbias_init]
    out_specs = [qcur, kcur, kcur, bspec]
    out_shape = [jax.ShapeDtypeStruct((sd, dil * wq), F32), jax.ShapeDtypeStruct((sd, dil * wk), F32),
                 jax.ShapeDtypeStruct((sd, dil * wk), F32), jax.ShapeDtypeStruct((hq, BLOCK, 2 * BLOCK), F32)]
    if has_sink:
        in_specs.append(pl.BlockSpec(memory_space=pltpu.SMEM))
        args.append(sinks)
        out_specs.append(pl.BlockSpec((hq, LANES), lambda r, j: (0, 0)))
        out_shape.append(jax.ShapeDtypeStruct((hq, LANES), F32))
    res = pl.pallas_call(
        body, name=name, grid=(dil, nb), in_specs=in_specs, out_specs=out_specs, out_shape=out_shape,
        compiler_params=_params(("arbitrary", "arbitrary")),
    )(*args)
    dq, dk, dv, dbias = res[0].reshape(s, wq), res[1].reshape(s, wk), res[2].reshape(s, wk), res[3]
    return dq, dk, dv, dbias, (res[4][:, 0] if has_sink else None)


def _neg_softplus(z):
    return -(jnp.maximum(z, 0.0) + jnp.log(1.0 + jnp.exp(-jnp.abs(z))))


SB_CHUNK = 256
HEADS_PER_PAIR = LANES // HEAD_DIM


def _tri(kind):
    row = lax.broadcasted_iota(jnp.int32, (SB_CHUNK, SB_CHUNK), 0)
    col = lax.broadcasted_iota(jnp.int32, (SB_CHUNK, SB_CHUNK), 1)
    return {"ge": row >= col, "lt": row < col, "le": row <= col}[kind].astype(BF16)


def _keys_on_lanes(a, rows):
    s, w = a.shape
    return jnp.transpose(a.reshape(s // rows, rows, w), (0, 2, 1))


def _sb_mask(i, jj):
    row = lax.broadcasted_iota(jnp.int32, (BLOCK, SB_CHUNK), 0)
    col = lax.broadcasted_iota(jnp.int32, (BLOCK, SB_CHUNK), 1)
    return col < row + (i * BLOCK - jj * SB_CHUNK)


def _sb_trips(i):
    return (i * BLOCK) // (2 * SB_CHUNK) + 1


def _sb_rows(jj, n):
    return pl.ds(pl.multiple_of(jj * SB_CHUNK, SB_CHUNK), n * SB_CHUNK)


def _sb_fwd(q, kt, v, name):
    s, w = q.shape
    npair, nb, nc = w // LANES, s // BLOCK, s // SB_CHUNK

    def body(q_ref, kt_ref, v_ref, o_ref, t_ref):
        i = pl.program_id(1)
        lincl = _tri("ge")
        heads = [slice(hh * HEAD_DIM, (hh + 1) * HEAD_DIM) for hh in range(HEADS_PER_PAIR)]
        qs = [q_ref[:, sl] for sl in heads]

        def trip(t, carry):
            lo, hi = 2 * t, 2 * t + 1
            mlo, mhi = _sb_mask(i, lo), _sb_mask(i, hi)
            new = []
            for hh, sl in enumerate(heads):
                o_acc, rem = carry[hh]
                zhi = _dot(qs[hh], kt_ref[hi, sl, :]) * QK_SCALE
                zlo = _dot(qs[hh], kt_ref[lo, sl, :]) * QK_SCALE
                lrhi = jnp.where(mhi, _neg_softplus(zhi), 0.0)
                lrlo = jnp.where(mlo, _neg_softplus(zlo), 0.0)
                tothi = jnp.sum(lrhi, axis=-1, keepdims=True)
                ahi = jnp.where(mhi, jnp.exp(zhi + (rem + _split_dot(lrhi, lincl))), 0.0)
                alo = jnp.where(mlo, jnp.exp(zlo + (rem + tothi + _split_dot(lrlo, lincl))), 0.0)
                a = jnp.concatenate([alo, ahi], axis=1).astype(BF16)
                new.append((o_acc + _dot(a, v_ref[_sb_rows(lo, 2), sl]),
                            rem + tothi + jnp.sum(lrlo, axis=-1, keepdims=True)))
            return tuple(new)

        init = tuple((jnp.zeros((BLOCK, HEAD_DIM), F32), jnp.zeros((BLOCK, 1), F32)) for _ in heads)
        trips = _sb_trips(i)
        carry = lax.fori_loop(0, trips, lambda t, cr: trip(trips - 1 - t, cr), init)
        for hh, sl in enumerate(heads):
            o_ref[:, sl] = carry[hh][0]
            t_ref[:, sl] = jnp.broadcast_to(carry[hh][1], (BLOCK, HEAD_DIM))

    qspec = pl.BlockSpec((BLOCK, LANES), lambda p, i: (i, p))
    return pl.pallas_call(
        body, name=name, grid=(npair, nb),
        in_specs=[qspec, pl.BlockSpec((nc, LANES, SB_CHUNK), lambda p, i: (0, p, 0)),
                  pl.BlockSpec((s, LANES), lambda p, i: (0, p))],
        out_specs=[qspec, qspec], out_shape=[jax.ShapeDtypeStruct((s, w), F32)] * 2,
        compiler_params=_params(("parallel", "parallel")),
    )(q, kt, v)


def _sb_bwd(q, k, kt, v, tot, do, name):
    s, w = q.shape
    npair, nb, nc = w // LANES, s // BLOCK, s // SB_CHUNK
    dob = do.astype(BF16)

    def body(q_ref, qt_ref, k_ref, kt_ref, vt_ref, t_ref, do_ref, dot_ref, dq_ref, dkt_ref, dvt_ref):
        i = pl.program_id(1)

        @pl.when(i == 0)
        def _():
            dkt_ref[...] = jnp.zeros_like(dkt_ref)
            dvt_ref[...] = jnp.zeros_like(dvt_ref)

        lbefore = _tri("lt")
        lupto = _tri("le")
        heads = [slice(hh * HEAD_DIM, (hh + 1) * HEAD_DIM) for hh in range(HEADS_PER_PAIR)]
        qs = [q_ref[:, sl] for sl in heads]
        qts = [qt_ref[sl, :] for sl in heads]
        dos = [do_ref[:, sl] for sl in heads]
        dots = [dot_ref[sl, :] for sl in heads]
        totals = [t_ref[:, sl.start:sl.start + 1] for sl in heads]

        def trip(t, carry):
            chunks = (2 * t, 2 * t + 1)
            masks = [_sb_mask(i, jj) for jj in chunks]
            new = []
            for hh, sl in enumerate(heads):
                dq_acc, plr, pg = carry[hh]
                zs = [_dot(qs[hh], kt_ref[jj, sl, :]) * QK_SCALE for jj in chunks]
                lrs = [jnp.where(m, _neg_softplus(z), 0.0) for m, z in zip(masks, zs)]
                lr_sums = [jnp.sum(lr, axis=-1, keepdims=True) for lr in lrs]
                before = [plr, plr + lr_sums[0]]
                avs = [jnp.where(m, jnp.exp(z + (totals[hh] - (b + _split_dot(lr, lbefore)))), 0.0)
                       for m, z, lr, b in zip(masks, zs, lrs, before)]
                gs = [_dot(dos[hh], vt_ref[jj, sl, :]) * a for jj, a in zip(chunks, avs)]
                g_sums = [jnp.sum(g, axis=-1, keepdims=True) for g in gs]
                upto = [pg, pg + g_sums[0]]
                dzs = [(jnp.where(m, g - jnp.exp(z + lr) * (u + _split_dot(g, lupto)), 0.0) * QK_SCALE).astype(BF16)
                       for m, z, lr, g, u in zip(masks, zs, lrs, gs, upto)]
                for jj, dzb, a in zip(chunks, dzs, avs):
                    dkt_ref[jj, sl, :] += _dot(qts[hh], dzb)
                    dvt_ref[jj, sl, :] += _dot(dots[hh], a.astype(BF16))
                dz2 = jnp.concatenate(dzs, axis=1)
                new.append((dq_acc + _dot(dz2, k_ref[_sb_rows(chunks[0], 2), sl]), plr + lr_sums[0] + lr_sums[1],
                            pg + g_sums[0] + g_sums[1]))
            return tuple(new)

        zero = jnp.zeros((BLOCK, 1), F32)
        init = tuple((jnp.zeros((BLOCK, HEAD_DIM), F32), zero, zero) for _ in heads)
        carry = lax.fori_loop(0, _sb_trips(i), trip, init)
        for hh, sl in enumerate(heads):
            dq_ref[:, sl] = carry[hh][0]

    qspec = pl.BlockSpec((BLOCK, LANES), lambda p, i: (i, p))
    qtspec = pl.BlockSpec((None, LANES, BLOCK), lambda p, i: (i, p, 0))
    kspec = pl.BlockSpec((s, LANES), lambda p, i: (0, p))
    ktspec = pl.BlockSpec((nc, LANES, SB_CHUNK), lambda p, i: (0, p, 0))
    dq, dkt, dvt = pl.pallas_call(
        body, name=name, grid=(npair, nb), in_specs=[qspec, qtspec, kspec, ktspec, ktspec, qspec, qspec, qtspec],
        out_specs=[qspec, ktspec, ktspec],
        out_shape=[jax.ShapeDtypeStruct((s, w), F32)] + [jax.ShapeDtypeStruct((nc, w, SB_CHUNK), F32)] * 2,
        compiler_params=_params(("parallel", "arbitrary")),
    )(q, _keys_on_lanes(q, BLOCK), k, kt, _keys_on_lanes(v, SB_CHUNK), tot, dob, _keys_on_lanes(dob, BLOCK))

    def rows_first(t):
        return jnp.transpose(t, (0, 2, 1)).reshape(s, w)

    return dq, rows_first(dkt), rows_first(dvt)


def _group_norm(xv, g):
    r = lax.rsqrt(jnp.mean(xv * xv, axis=-1, keepdims=True) + EPS)
    return xv * r * g


def _mix_fwd(oa, ob, ocs, lses, gain, cfg, name):
    s = oa.shape[0]
    ts = _div_tile(s, 256, 16)
    aq, bw, cw = cfg.a_q, cfg.b_w, cfg.c_w

    def body(oa_ref, ob_ref, c1, c2, c3, l1, l2, l3, g_ref, mix_ref, oc_ref, lse_ref):
        m = jnp.maximum(jnp.maximum(l1[...], l2[...]), l3[...])
        e1, e2, e3 = jnp.exp(l1[...] - m), jnp.exp(l2[...] - m), jnp.exp(l3[...] - m)
        den = e1 + e2 + e3
        oc = (e1 * c1[...] + e2 * c2[...] + e3 * c3[...]) / den
        oc_ref[...] = oc
        lse_ref[...] = m + jnp.log(den)
        mix_ref[:, 0:aq] = _group_norm(oa_ref[...], g_ref[:, 0:aq]).astype(BF16)
        mix_ref[:, aq:aq + bw] = _group_norm(ob_ref[...], g_ref[:, aq:aq + bw]).astype(BF16)
        mix_ref[:, aq + bw:] = _group_norm(oc, g_ref[:, aq + bw:]).astype(BF16)

    def row(wd):
        return pl.BlockSpec((ts, wd), lambda i: (i, 0))

    return pl.pallas_call(
        body, name=name, grid=(s // ts,),
        in_specs=[row(aq), row(bw)] + [row(cw)] * 6 + [pl.BlockSpec((1, cfg.d), lambda i: (0, 0))],
        out_specs=[row(cfg.d), row(cw), row(cw)],
        out_shape=[jax.ShapeDtypeStruct((s, cfg.d), BF16), jax.ShapeDtypeStruct((s, cw), F32),
                   jax.ShapeDtypeStruct((s, cw), F32)],
        compiler_params=_params(("parallel",)),
    )(oa, ob, *ocs, *lses, gain)


def _mix_bwd(dmix, oa, ob, oc, gain, cfg, name):
    s = oa.shape[0]
    ts = _div_tile(s, 256, 8)
    aq, bw, cw = cfg.a_q, cfg.b_w, cfg.c_w

    def body(dm_ref, oa_ref, ob_ref, oc_ref, g_ref, da_ref, db_ref, dc_ref, dg_ref):
        @pl.when(pl.program_id(0) == 0)
        def _():
            dg_ref[...] = jnp.zeros_like(dg_ref)

        for x_ref, dx_ref, lo, hi in ((oa_ref, da_ref, 0, aq), (ob_ref, db_ref, aq, aq + bw),
                                      (oc_ref, dc_ref, aq + bw, aq + bw + cw)):
            xv = x_ref[...]
            dy = dm_ref[:, lo:hi]
            r = lax.rsqrt(jnp.mean(xv * xv, axis=-1, keepdims=True) + EPS)
            xhat = xv * r
            dxhat = dy * g_ref[:, lo:hi]
            dx_ref[...] = r * (dxhat - xhat * jnp.mean(dxhat * xhat, axis=-1, keepdims=True))
            dg_ref[:, lo:hi] += jnp.sum(dy * xhat, axis=0, keepdims=True)

    def row(wd):
        return pl.BlockSpec((ts, wd), lambda i: (i, 0))

    vec = pl.BlockSpec((1, cfg.d), lambda i: (0, 0))
    return pl.pallas_call(
        body, name=name, grid=(s // ts,), in_specs=[row(cfg.d), row(aq), row(bw), row(cw), vec],
        out_specs=[row(aq), row(bw), row(cw), vec],
        out_shape=[jax.ShapeDtypeStruct((s, aq), F32), jax.ShapeDtypeStruct((s, bw), F32),
                   jax.ShapeDtypeStruct((s, cw), F32), jax.ShapeDtypeStruct((1, cfg.d), F32)],
        compiler_params=_params(("arbitrary",)),
    )(dmix, oa, ob, oc, gain)


def _bias_table_grad(dbiases, buckets, name):
    outs = []
    for idx, (db, bk) in enumerate(zip(dbiases, buckets)):
        h = db.shape[0]

        def body(db_ref, bk_ref, o_ref):
            xv = db_ref[0]
            ids = bk_ref[...]
            lane = lax.broadcasted_iota(jnp.int32, (1, LANES), 1)
            acc = jnp.zeros((1, LANES), F32)
            for b in range(N_BUCKETS):
                tot = jnp.sum(jnp.where(ids == b, xv, 0.0), axis=0, keepdims=True)
                tot = jnp.sum(tot, axis=1, keepdims=True)
                acc = jnp.where(lane == b, tot, acc)
            o_ref[0] = acc

        outs.append(pl.pallas_call(
            body, name=f"{name}_{idx}", grid=(h,),
            in_specs=[pl.BlockSpec((1, BLOCK, 2 * BLOCK), lambda i: (i, 0, 0)),
                      pl.BlockSpec((BLOCK, 2 * BLOCK), lambda i: (0, 0))],
            out_specs=pl.BlockSpec((1, 1, LANES), lambda i: (i, 0, 0)),
            out_shape=jax.ShapeDtypeStruct((h, 1, LANES), F32), compiler_params=_params(("parallel",)),
        )(db, bk)[:, 0, :])
    return outs


def _shift_down(u, n, rows):
    return jnp.where(rows >= n, pltpu.roll(u, n, 0), 0.0)


def _shift_up(u, n, rows, s):
    return jnp.where(rows < s - n, pltpu.roll(u, s - n, 0), 0.0)


def _conv(u, w_ref, b_ref, rows):
    return (b_ref[...] + w_ref[0:1, :] * _shift_down(u, 2, rows) + w_ref[1:2, :] * _shift_down(u, 1, rows)
            + w_ref[2:3, :] * u)


def _conv_act_fwd(u, conv_w, conv_b, f, name):
    s = u.shape[0]
    nf = f // LANES

    def body(ug_ref, uu_ref, wg_ref, wu_ref, bg_ref, bu_ref, act_ref):
        rows = lax.broadcasted_iota(jnp.int32, (s, LANES), 0)
        gate = _conv(ug_ref[...], wg_ref, bg_ref, rows)
        up = _conv(uu_ref[...], wu_ref, bu_ref, rows)
        act_ref[...] = (gate * jax.nn.sigmoid(gate) * up).astype(BF16)

    def col(rws, off):
        return pl.BlockSpec((rws, LANES), lambda j: (0, j + off))

    return pl.pallas_call(
        body, name=name, grid=(nf,),
        in_specs=[col(s, 0), col(s, nf), col(CONV_WIDTH, 0), col(CONV_WIDTH, nf), col(1, 0), col(1, nf)],
        out_specs=col(s, 0), out_shape=jax.ShapeDtypeStruct((s, f), BF16), compiler_params=_params(("parallel",)),
    )(u, u, conv_w, conv_w, conv_b, conv_b)


def _conv_act_bwd(u, dact, conv_w, conv_b, f, name):
    s = u.shape[0]
    nf = f // LANES

    def body(ug_ref, uu_ref, da_ref, wg_ref, wu_ref, bg_ref, bu_ref, dug_ref, duu_ref, dwg_ref, dwu_ref, dbg_ref,
             dbu_ref):
        rows = lax.broadcasted_iota(jnp.int32, (s, LANES), 0)
        ug, uu = ug_ref[...], uu_ref[...]
        gate = _conv(ug, wg_ref, bg_ref, rows)
        up = _conv(uu, wu_ref, bu_ref, rows)
        sg = jax.nn.sigmoid(gate)
        da = da_ref[...]
        dgate = da * up * (sg * (1.0 + gate * (1.0 - sg)))
        dup = da * (gate * sg)
        for du, uv, w_ref, du_ref, dw_ref, db_ref in ((dgate, ug, wg_ref, dug_ref, dwg_ref, dbg_ref),
                                                     (dup, uu, wu_ref, duu_ref, dwu_ref, dbu_ref)):
            du_ref[...] = (w_ref[2:3, :] * du + w_ref[1:2, :] * _shift_up(du, 1, rows, s)
                           + w_ref[0:1, :] * _shift_up(du, 2, rows, s)).astype(BF16)
            dw_ref[0:1, :] = jnp.sum(du * _shift_down(uv, 2, rows), axis=0, keepdims=True)
            dw_ref[1:2, :] = jnp.sum(du * _shift_down(uv, 1, rows), axis=0, keepdims=True)
            dw_ref[2:3, :] = jnp.sum(du * uv, axis=0, keepdims=True)
            db_ref[...] = jnp.sum(du, axis=0, keepdims=True)

    def col(rws, off):
        return pl.BlockSpec((rws, LANES), lambda j: (0, j + off))

    return pl.pallas_call(
        body, name=name, grid=(nf,),
        in_specs=[col(s, 0), col(s, nf), col(s, 0), col(CONV_WIDTH, 0), col(CONV_WIDTH, nf), col(1, 0), col(1, nf)],
        out_specs=[col(s, 0), col(s, 0), col(CONV_WIDTH, 0), col(CONV_WIDTH, 0), col(1, 0), col(1, 0)],
        out_shape=[jax.ShapeDtypeStruct((s, f), BF16)] * 2 + [jax.ShapeDtypeStruct((CONV_WIDTH, f), F32)] * 2
        + [jax.ShapeDtypeStruct((1, f), F32)] * 2,
        compiler_params=_params(("parallel",)),
    )(u, u, dact, conv_w, conv_w, conv_b, conv_b)


def _loss_head(y, target, name):
    s, d = y.shape
    ts = _div_tile(s, 256, 8)

    def body(y_ref, t_ref, dy_ref, l_ref):
        @pl.when(pl.program_id(0) == 0)
        def _():
            l_ref[...] = jnp.zeros_like(l_ref)

        err = y_ref[...] - t_ref[...]
        dy_ref[...] = err * (1.0 / d)
        tot = jnp.sum(jnp.sum(err * err, axis=0, keepdims=True), axis=1, keepdims=True) * (0.5 / d)
        l_ref[...] += jnp.broadcast_to(tot, l_ref.shape)

    row = pl.BlockSpec((ts, d), lambda i: (i, 0))
    return pl.pallas_call(
        body, name=name, grid=(s // ts,), in_specs=[row, row],
        out_specs=[row, pl.BlockSpec((8, LANES), lambda i: (0, 0))],
        out_shape=[jax.ShapeDtypeStruct((s, d), F32), jax.ShapeDtypeStruct((8, LANES), F32)],
        compiler_params=_params(("arbitrary",)),
    )(y, target)


def _adamw(w, g, m, v, name):
    r, c = w.shape
    tr = _div_tile(r, max(8, (1 << 18) // c // 8 * 8), 8)
    c1 = 1.0 - ADAM_B1 ** ADAM_STEP
    c2 = 1.0 - ADAM_B2 ** ADAM_STEP

    def body(w_ref, g_ref, m_ref, v_ref, d_ref, nm_ref, nv_ref):
        gv = g_ref[...]
        nm = ADAM_B1 * m_ref[...] + (1.0 - ADAM_B1) * gv
        nv = ADAM_B2 * v_ref[...] + (1.0 - ADAM_B2) * (gv * gv)
        d_ref[...] = -ADAM_LR * ((nm / c1) / (jnp.sqrt(nv / c2) + ADAM_EPS) + ADAM_WD * w_ref[...])
        nm_ref[...] = nm
        nv_ref[...] = nv

    spec = pl.BlockSpec((tr, c), lambda i: (i, 0))
    return pl.pallas_call(
        body, name=name, grid=(r // tr,), in_specs=[spec] * 4, out_specs=[spec] * 3,
        out_shape=[jax.ShapeDtypeStruct((r, c), F32)] * 3, compiler_params=_params(("parallel",)),
    )(w, g, m, v)


def _adamw_layer(layer, w, g, m, v, bufs, name):
    depth, r, c = w.shape
    tr = _div_tile(r, max(8, (1 << 17) // c // 8 * 8), 8)
    c1 = 1.0 - ADAM_B1 ** ADAM_STEP
    c2 = 1.0 - ADAM_B2 ** ADAM_STEP

    def body(*refs):
        w_ref, g_ref, m_ref, v_ref = refs[:4]
        go_ref, d_ref, nm_ref, nv_ref = refs[-4:]
        gv = g_ref[...]
        nm = ADAM_B1 * m_ref[...] + (1.0 - ADAM_B1) * gv
        nv = ADAM_B2 * v_ref[...] + (1.0 - ADAM_B2) * (gv * gv)
        d_ref[...] = -ADAM_LR * ((nm / c1) / (jnp.sqrt(nv / c2) + ADAM_EPS) + ADAM_WD * w_ref[...])
        nm_ref[...] = nm
        nv_ref[...] = nv
        go_ref[...] = gv

    lay = pl.BlockSpec((None, tr, c), lambda i: (layer, i, 0))
    in_specs = [lay, pl.BlockSpec((tr, c), lambda i: (i, 0)), lay, lay]
    args = [w, g, m, v]
    aliases = {}
    if bufs is not None:
        in_specs += [pl.BlockSpec(memory_space=pl.ANY)] * 4
        args += list(bufs)
        aliases = {4 + k: k for k in range(4)}
    return pl.pallas_call(
        body, name=name, grid=(r // tr,), in_specs=in_specs, out_specs=[lay] * 4,
        out_shape=[jax.ShapeDtypeStruct((depth, r, c), F32)] * 4, input_output_aliases=aliases,
        compiler_params=_params(("parallel",)),
    )(*args)


def _mesh_pos():
    return lax.axis_index("x"), lax.axis_index("y"), lax.axis_index("c")


def _flip(v, bit):
    return 1 - v if bit else v


def _sum_parts(parts, name):
    _, r, c = parts.shape
    tr = _div_tile(r, 256, 16)

    def body(p_ref, o_ref):
        acc = p_ref[0].astype(F32)
        for src in range(1, N_DEVICES):
            acc = acc + p_ref[src].astype(F32)
        o_ref[...] = acc

    return pl.pallas_call(
        body, name=name, grid=(r // tr,), in_specs=[pl.BlockSpec((N_DEVICES, tr, c), lambda i: (0, i, 0))],
        out_specs=pl.BlockSpec((tr, c), lambda i: (i, 0)), out_shape=jax.ShapeDtypeStruct((r, c), F32),
        compiler_params=_params(("parallel",)),
    )(parts)


def _split_start(srcs, lands, plan, ncopies, name):
    nbuf = len(srcs) + len(lands)

    def body(*refs):
        bufs = refs[:nbuf]
        send_sem, recv_sem, token = refs[nbuf], refs[nbuf + 1], refs[-1]
        for k, (src, dst, dev) in enumerate(plan(bufs[:len(srcs)], bufs[len(srcs):])):
            pltpu.make_async_remote_copy(src_ref=src, dst_ref=dst, send_sem=send_sem.at[k], recv_sem=recv_sem.at[k],
                                         device_id=dev, device_id_type=MESH).start()
        token[...] = jnp.zeros_like(token)

    hbm = pl.BlockSpec(memory_space=pltpu.HBM)
    sem = pl.BlockSpec(memory_space=pltpu.SEMAPHORE)
    operands = [pltpu.with_memory_space_constraint(a, pltpu.HBM) for a in (*srcs, *lands)]
    outs = pl.pallas_call(
        body, name=name, in_specs=[hbm] * nbuf,
        out_specs=(sem, sem, *[hbm] * nbuf, pl.BlockSpec(memory_space=pltpu.VMEM)),
        out_shape=(pltpu.SemaphoreType.DMA((ncopies,)), pltpu.SemaphoreType.DMA((ncopies,)),
                   *[pltpu.HBM(a.shape, a.dtype) for a in operands], jax.ShapeDtypeStruct((8, LANES), F32)),
        input_output_aliases={i: 2 + i for i in range(nbuf)},
        compiler_params=pltpu.CompilerParams(has_side_effects=pltpu.SideEffectType.DATAFLOW_SIDE_EFFECTING),
    )(*operands)
    handle = dict(send=outs[0], recv=outs[1], bufs=list(outs[2:2 + nbuf]), nsrc=len(srcs), plan=plan)
    return handle, outs[-1]


def _split_wait(handle, after, name):
    nbuf, nsrc, plan = len(handle["bufs"]), handle["nsrc"], handle["plan"]

    def body(*refs):
        bufs = refs[:nbuf]
        send_sem, recv_sem = refs[nbuf], refs[nbuf + 1]
        for k, (src, dst, dev) in enumerate(plan(bufs[:nsrc], bufs[nsrc:])):
            copy = pltpu.make_async_remote_copy(src_ref=src, dst_ref=dst, send_sem=send_sem.at[k],
                                                recv_sem=recv_sem.at[k], device_id=dev, device_id_type=MESH)
            copy.wait_send()
            copy.wait_recv()

    hbm = pl.BlockSpec(memory_space=pltpu.HBM)
    sem = pl.BlockSpec(memory_space=pltpu.SEMAPHORE)
    outs = pl.pallas_call(
        body, name=name, in_specs=[hbm] * nbuf + [sem, sem, pl.BlockSpec(memory_space=pl.ANY)],
        out_specs=[hbm] * nbuf, out_shape=[pltpu.HBM(a.shape, a.dtype) for a in handle["bufs"]],
        input_output_aliases={i: i for i in range(nbuf)},
        compiler_params=pltpu.CompilerParams(has_side_effects=pltpu.SideEffectType.DATAFLOW_SIDE_EFFECTING),
    )(*handle["bufs"], handle["send"], handle["recv"], after)
    return list(outs[nsrc:])


def _own_slot(shape, dtype, block, index):
    return lax.dynamic_update_slice(lax.empty(shape, dtype), block[None], (index,) + (0,) * block.ndim)


def _gather_plan(srcs, lands):
    x, y, c = _mesh_pos()
    return [(src, land.at[2 * x + y], (*chip, c))
            for src, land in zip(srcs, lands) for chip in ((1 - x, y), (x, 1 - y), (1 - x, 1 - y))]


def _scatter_plan(srcs, lands):
    x, y, c = _mesh_pos()
    out = []
    for src, land in zip(srcs, lands):
        half = src.shape[1] // 2
        for d in range(1, N_DEVICES):
            p = (_flip(x, d & 4), _flip(y, d & 2), _flip(c, d & 1))
            out.append((src.at[2 * p[0] + p[1], pl.ds(p[2] * half, half), :], land.at[4 * x + 2 * y + c], p))
    return out


def _swap_plan(srcs, lands):
    x, y, c = _mesh_pos()
    return [(src, land.at[c], (x, y, 1 - c)) for src, land in zip(srcs, lands)]


class _Gathered:
    def __init__(self, groups):
        self.groups = groups
        self.ready = {}

    def get(self, name, after=None):
        if name not in self.ready:
            handle, names, wait_name = next(g for g in self.groups if name in g[1])
            for n, full in zip(names, _split_wait(handle, after, wait_name)):
                self.ready[n] = full.reshape(-1, full.shape[-1])
        return self.ready[name]


def _allreduce_small(flat, name):
    r = flat.shape[0]

    def body(x_ref, o_ref, buf, send_sems, recv_sems):
        x, y, c = _mesh_pos()
        me = 4 * x + 2 * y + c
        buf[me] = x_ref[...]
        started = []
        peers = [(_flip(x, d & 4), _flip(y, d & 2), _flip(c, d & 1)) for d in range(1, N_DEVICES)]
        for d, p in enumerate(peers):
            cp = pltpu.make_async_remote_copy(src_ref=x_ref, dst_ref=buf.at[me], send_sem=send_sems.at[d],
                                              recv_sem=recv_sems.at[d], device_id=p, device_id_type=MESH)
            cp.start()
            started.append(cp)
        for d, p in enumerate(peers):
            slot = buf.at[4 * p[0] + 2 * p[1] + p[2]]
            pltpu.make_async_remote_copy(src_ref=slot, dst_ref=slot, send_sem=send_sems.at[d], recv_sem=recv_sems.at[d],
                                         device_id=p, device_id_type=MESH).wait_recv()
        for cp in started:
            cp.wait_send()
        acc = buf[0]
        for src in range(1, N_DEVICES):
            acc = acc + buf[src]
        o_ref[...] = acc

    vm = pl.BlockSpec(memory_space=pltpu.VMEM)
    return pl.pallas_call(
        body, name=name, in_specs=[vm], out_specs=vm, out_shape=jax.ShapeDtypeStruct((r, LANES), F32),
        scratch_shapes=[pltpu.VMEM((N_DEVICES, r, LANES), F32), pltpu.SemaphoreType.DMA((N_DEVICES - 1,)),
                        pltpu.SemaphoreType.DMA((N_DEVICES - 1,))],
        compiler_params=pltpu.CompilerParams(vmem_limit_bytes=VMEM_LIMIT_BYTES),
    )(flat)


def _bucket_ids(dil):
    rel = (np.arange(BLOCK)[:, None] + BLOCK - np.arange(2 * BLOCK)[None, :]) * dil
    max_exact = N_BUCKETS // 2
    d = np.maximum(rel, 0)
    large = max_exact + (np.log(np.maximum(d, 1).astype(np.float32) / max_exact)
                         / np.float32(np.log(T5_MAX_DIST / max_exact)) * (N_BUCKETS - max_exact)).astype(np.int32)
    large = np.minimum(large, N_BUCKETS - 1)
    return np.where(d < max_exact, d, large).astype(np.int32)


def _block_bias(table, dil):
    onehot = (jnp.asarray(_bucket_ids(dil))[:, :, None] == jnp.arange(N_BUCKETS)[None, None, :]).astype(F32)
    return jnp.einsum("ijb,bh->hij", onehot, table.astype(F32), precision=lax.Precision.HIGHEST)


def _tile_gain(g, n):
    return jnp.tile(g.reshape(1, HEAD_DIM), (1, n))


def _layer_fwd(x, p, cfg):
    w = p["weights"]
    h1 = _rmsnorm_fwd(x, p["attn_norm"], "attn_norm_fwd")
    proj = _matmul(h1, w.get("w_in_t", h1), "nt", F32, "in_proj", tm=1024, tn=768, tk=2048)
    aq, ak, av, bq, bk, bv, cq, ck, cv = _qk_prep(proj, p["gains"], cfg, "qk_prep")
    oa, lse_a = _banded_fwd(aq, ak, av, p["bias_a"], p["sinks"], cfg.nha, cfg.nkva, WINDOW_A - 1, 1, "swa_fwd")
    bkt = _keys_on_lanes(bk, SB_CHUNK)
    ob, tot_b = _sb_fwd(bq, bkt, bv, "stickbreak_fwd")
    ocs, lses = [], []
    for (window, dil), bias in zip(DILATED_PAIRS, p["bias_c"]):
        o, l = _banded_fwd(cq, ck, cv, bias, None, cfg.nhc, cfg.nhc, window // dil, dil, f"dilated{dil}_fwd")
        ocs.append(o)
        lses.append(l)
    mix, oc, lse_c = _mix_fwd(oa, ob, ocs, lses, p["mix_gain"], cfg, "mix_fwd")
    xm = _matmul(mix, w.get("w_out", mix), "nn", F32, "out_proj", tm=1024, tn=512, tk=2048, residual=x)
    h2 = _rmsnorm_fwd(xm, p["ffn_norm"], "ffn_norm_fwd")
    u = _matmul(h2, w.get("w_up_t", h2), "nt", F32, "up_proj", tm=1024, tn=512, tk=2048)
    act = _conv_act_fwd(u, p["conv_w"], p["conv_b"], cfg.f, "conv_act_fwd")
    y = _matmul(act, w.get("w_down", act), "nn", F32, "down_proj", tm=1024, tn=1024, tk=512, residual=xm)
    saved = dict(x=x, h1=h1, proj=proj, q=(aq, ak, av, bq, bk, bv, cq, ck, cv), oa=oa, lse_a=lse_a, ob=ob,
                 tot_b=tot_b, bkt=bkt, oc=oc, lse_c=lse_c, mix=mix, xm=xm, h2=h2, u=u, act=act)
    return y, saved


def _layer_bwd(dy, sv, p, dbias, cfg, on_grad):
    aq, ak, av, bq, bk, bv, cq, ck, cv = sv["q"]
    w = p["weights"]
    anchor = on_grad(_matmul(sv["act"], dy, "tn", BF16, "down_proj_dw", tm=1408, tn=2048, tk=512))
    dact = _matmul(dy, w.get("w_down"), "nt", F32, "down_proj_dx", tm=1024, tn=512, tk=2048)
    dug, duu, dwg, dwu, dbg, dbu = _conv_act_bwd(sv["u"], dact, p["conv_w"], p["conv_b"] + anchor, cfg.f,
                                                 "conv_act_bwd")
    du = jnp.concatenate([dug, duu], axis=1)
    anchor = on_grad(_matmul(du, sv["h2"], "tn", BF16, "up_proj_dw", tm=1408, tn=2048, tk=512))
    dh2 = _matmul(du, w.get("w_up_t"), "nn", F32, "up_proj_dx", tm=1024, tn=2048, tk=512)
    dxm, g_ffn_norm = _rmsnorm_bwd(sv["xm"], p["ffn_norm"] + anchor, dh2, dy, "ffn_norm_bwd")
    anchor = on_grad(_matmul(sv["mix"], dxm, "tn", BF16, "out_proj_dw", tm=1024, tn=2048, tk=512))
    dmix = _matmul(dxm, w.get("w_out"), "nt", F32, "out_proj_dx", tm=1024, tn=512, tk=2048)
    doa, dob, doc, g_mix_gain = _mix_bwd(dmix, sv["oa"], sv["ob"], sv["oc"], p["mix_gain"] + anchor, cfg, "mix_bwd")
    daq, dak, dav, dbias_a, g_sinks = _banded_bwd(aq, ak, av, sv["oa"], sv["lse_a"], doa, p["bias_a"], p["sinks"],
                                                 dbias[0], cfg.nha, cfg.nkva, WINDOW_A - 1, 1, "swa_bwd")
    dbq, dbk, dbv = _sb_bwd(bq, bk, sv["bkt"], bv, sv["tot_b"], dob, "stickbreak_bwd")
    dcq, dck, dcv, dbias_c = [], [], [], []
    for idx, ((window, dil), bias) in enumerate(zip(DILATED_PAIRS, p["bias_c"])):
        a, b, c, d, _ = _banded_bwd(cq, ck, cv, sv["oc"], sv["lse_c"], doc, bias, None, dbias[1][idx], cfg.nhc,
                                    cfg.nhc, window // dil, dil, f"dilated{dil}_bwd")
        dcq.append(a)
        dck.append(b)
        dcv.append(c)
        dbias_c.append(d)
    dproj, g_aq, g_ak, g_cq, g_ck = _qk_prep_bwd(
        sv["proj"], p["gains"], [[daq], [dak], [dav], [dbq], [dbk], [dbv], dcq, dck, dcv], cfg, "qk_prep_bwd")
    anchor = on_grad(_matmul(dproj, sv["h1"], "tn", BF16, "in_proj_dw", tm=768, tn=2048, tk=512))
    dh1 = _matmul(dproj, w.get("w_in_t"), "nn", F32, "in_proj_dx", tm=1024, tn=2048, tk=768)
    dx, g_attn_norm = _rmsnorm_bwd(sv["x"], p["attn_norm"] + anchor, dh1, dxm, "attn_norm_bwd")

    def fold(g):
        return jnp.sum(g.reshape(-1, HEAD_DIM), axis=0)

    small = dict(attn_norm=g_attn_norm[0], a_q_gain=fold(g_aq), a_k_gain=fold(g_ak), a_sinks=g_sinks,
                 c_q_gain=fold(g_cq), c_k_gain=fold(g_ck), mix_out_gain=g_mix_gain[0], ffn_norm=g_ffn_norm[0],
                 conv_w=jnp.concatenate([dwg, dwu], axis=1), conv_b=jnp.concatenate([dbg, dbu], axis=1)[0])
    return dx, small, (dbias_a, dbias_c)


_SMALL = ("attn_norm", "a_q_gain", "a_k_gain", "a_sinks", "c_q_gain", "c_k_gain", "rel_bias_table", "mix_out_gain",
          "ffn_norm", "conv_w", "conv_b")


def _pack(arrays):
    flat = jnp.concatenate([a.reshape(-1).astype(F32) for a in arrays])
    pad = (-flat.shape[0]) % (8 * LANES)
    return jnp.pad(flat, (0, pad)).reshape(-1, LANES)


def _unpack(flat, shapes):
    flat = flat.reshape(-1)
    out, pos = [], 0
    for sh in shapes:
        n = int(np.prod(sh))
        out.append(flat[pos:pos + n].reshape(sh))
        pos += n
    return out


def kernel(x, attn_norm, w_in, a_q_gain, a_k_gain, a_sinks, c_q_gain, c_k_gain, rel_bias_table, mix_out_gain, w_out, ffn_norm, w_up, conv_w, conv_b, w_down, loss_target, m_attn_norm, m_w_in, m_a_q_gain, m_a_k_gain, m_a_sinks, m_c_q_gain, m_c_k_gain, m_rel_bias_table, m_mix_out_gain, m_w_out, m_ffn_norm, m_w_up, m_conv_w, m_conv_b, m_w_down, v_attn_norm, v_w_in, v_a_q_gain, v_a_k_gain, v_a_sinks, v_c_q_gain, v_c_k_gain, v_rel_bias_table, v_mix_out_gain, v_w_out, v_ffn_norm, v_w_up, v_conv_w, v_conv_b, v_w_down):
    depth, d = attn_norm.shape
    f = w_down.shape[1] * N_CHIPS
    cfg = _Cfg(d, f)
    chip = 2 * lax.axis_index("x") + lax.axis_index("y")

    cw_cols = conv_w.shape[2]
    cw_flat = conv_w.reshape(-1)
    cw_rows = -(-cw_flat.shape[0] // (16 * LANES)) * 16
    cw_pad = jnp.pad(cw_flat, (0, cw_rows * LANES - cw_flat.shape[0])).reshape(cw_rows, LANES)

    table_a, table_c = rel_bias_table[:, :cfg.nha], rel_bias_table[:, cfg.nha:]
    bias_a = _block_bias(table_a, 1)
    bias_c = [_block_bias(table_c, dil) for _, dil in DILATED_PAIRS]

    layers, anchor = [], 0.0
    for l in range(depth):
        shards = [w_in[l].T.astype(BF16), w_out[l].astype(BF16), w_up[l].T.astype(BF16), w_down[l].astype(BF16)]
        names = ["w_in_t", "w_out", "w_up_t", "w_down"]
        if l == 0:
            todo = [([cw_pad, shards[0]], ["conv_w", names[0]])] + [([s], [n]) for s, n in zip(shards[1:], names[1:])]
        else:
            todo = [(shards, names)]
        groups = []
        for k, (srcs, group_names) in enumerate(todo):
            lands = [_own_slot((N_CHIPS,) + s.shape, s.dtype, s, chip) for s in srcs]
            handle, token = _split_start(srcs, lands, _gather_plan, 3 * len(srcs), f"gather_start_{l}_{k}")
            anchor = anchor + token[0, 0]
            groups.append((handle, group_names, f"gather_wait_{l}_{k}"))
        layers.append(dict(
            attn_norm=attn_norm[l].reshape(1, d), ffn_norm=ffn_norm[l].reshape(1, d),
            mix_gain=mix_out_gain[l].reshape(1, d),
            gains=(_tile_gain(a_q_gain[l], cfg.nha), _tile_gain(a_k_gain[l], cfg.nkva),
                   _tile_gain(c_q_gain[l], cfg.nhc), _tile_gain(c_k_gain[l], cfg.nhc)),
            sinks=a_sinks[l], bias_a=bias_a, bias_c=bias_c, conv_b=conv_b[l].reshape(1, 2 * f),
            weights=_Gathered(groups)))
    cw_all = layers[0]["weights"].get("conv_w", layers[0]["attn_norm"] + anchor)
    cw_all = cw_all.reshape(N_CHIPS, -1)[:, :cw_flat.shape[0]].reshape(N_CHIPS, depth, CONV_WIDTH, cw_cols)
    conv_w_full = jnp.transpose(cw_all, (1, 2, 0, 3)).reshape(depth, CONV_WIDTH, N_CHIPS * cw_cols)
    for l in range(depth):
        layers[l]["conv_w"] = conv_w_full[l]

    act = x[0]
    saved = []
    for l in range(depth):
        act, sv = _layer_fwd(act, layers[l], cfg)
        saved.append(sv)
    dact, loss_blk = _loss_head(act, loss_target[0], "loss_head")
    loss = lax.psum(loss_blk[0, 0], ("x", "y", "c"))

    core = lax.axis_index("c")

    def start_scatter(grads, name):
        srcs = [g.reshape(N_CHIPS, -1, g.shape[-1]) for g in grads]
        lands = []
        for g in srcs:
            half = g.shape[1] // 2
            own = lax.dynamic_slice(g, (chip, core * half, 0), (1, half, g.shape[2]))[0]
            lands.append(_own_slot((N_DEVICES, half, g.shape[2]), g.dtype, own, 2 * chip + core))
        return _split_start(srcs, lands, _scatter_plan, (N_DEVICES - 1) * len(srcs), name)

    def finish_scatter(l, handles, after):
        parts = [pt for k, h in enumerate(handles) for pt in _split_wait(h, after, f"scatter_wait_{l}_{k}")][::-1]
        halves = [_sum_parts(pt, f"sum_grads_{t}") for t, pt in enumerate(parts)]
        lands = [_own_slot((2,) + h.shape, h.dtype, h, core) for h in halves]
        return _split_start(halves, lands, _swap_plan, len(halves), f"swap_start_{l}")[0]

    dbias = (jnp.zeros_like(bias_a), [jnp.zeros_like(b) for b in bias_c])
    small_grads = [None] * depth
    swaps = [None] * depth
    pending = None
    for l in reversed(range(depth)):
        made = []

        def on_grad(g, l=l, made=made):
            if l:
                made.append(g)
                return 0.0
            handle, token = start_scatter([g], f"scatter_start_0_{len(made)}")
            made.append(handle)
            return token[0, 0]

        dact, small_grads[l], dbias = _layer_bwd(dact, saved[l], layers[l], dbias, cfg, on_grad)
        if pending is not None:
            swaps[l + 1] = finish_scatter(l + 1, pending, dact)
        if l:
            handle, token = start_scatter(made, f"scatter_start_{l}")
            pending = [handle]
            layers[l - 1]["conv_b"] = layers[l - 1]["conv_b"] + token[0, 0]
        else:
            pending = made
    grad_x = dact[None]

    tabs = _bias_table_grad([dbias[0]] + dbias[1], [jnp.asarray(_bucket_ids(1))]
                            + [jnp.asarray(_bucket_ids(dil)) for _, dil in DILATED_PAIRS], "bias_table_grad")
    g_table_a = tabs[0][:, :N_BUCKETS].T
    g_table_c = (tabs[1] + tabs[2] + tabs[3])[:, :N_BUCKETS].T
    g_table = jnp.concatenate([g_table_a, g_table_c], axis=1)
    small_local = {k: jnp.stack([small_grads[l][k] for l in range(depth)]) for k in _SMALL if k != "rel_bias_table"}
    small_local["rel_bias_table"] = g_table
    shapes = [small_local[k].shape for k in _SMALL]
    reduced = dict(zip(_SMALL, _unpack(_allreduce_small(_pack([small_local[k] for k in _SMALL]), "allreduce_small"),
                                       shapes)))
    reduced["conv_w"] = lax.dynamic_slice_in_dim(reduced["conv_w"], chip * cw_cols, cw_cols, axis=2)

    given = dict(attn_norm=attn_norm, a_q_gain=a_q_gain, a_k_gain=a_k_gain, a_sinks=a_sinks, c_q_gain=c_q_gain,
                 c_k_gain=c_k_gain, rel_bias_table=rel_bias_table, mix_out_gain=mix_out_gain, ffn_norm=ffn_norm,
                 conv_w=conv_w, conv_b=conv_b)
    moms = dict(attn_norm=(m_attn_norm, v_attn_norm), a_q_gain=(m_a_q_gain, v_a_q_gain),
                a_k_gain=(m_a_k_gain, v_a_k_gain), a_sinks=(m_a_sinks, v_a_sinks), c_q_gain=(m_c_q_gain, v_c_q_gain),
                c_k_gain=(m_c_k_gain, v_c_k_gain), rel_bias_table=(m_rel_bias_table, v_rel_bias_table),
                mix_out_gain=(m_mix_out_gain, v_mix_out_gain), ffn_norm=(m_ffn_norm, v_ffn_norm),
                conv_w=(m_conv_w, v_conv_w), conv_b=(m_conv_b, v_conv_b))
    sshapes = [given[k].shape for k in _SMALL]
    s_delta, s_m, s_v = _adamw(_pack([given[k] for k in _SMALL]), _pack([reduced[k] for k in _SMALL]),
                               _pack([moms[k][0] for k in _SMALL]), _pack([moms[k][1] for k in _SMALL]), "adamw_small")
    grads = dict(reduced)
    deltas = dict(zip(_SMALL, _unpack(s_delta, sshapes)))
    new_m = dict(zip(_SMALL, _unpack(s_m, sshapes)))
    new_v = dict(zip(_SMALL, _unpack(s_v, sshapes)))

    big_given = dict(w_in=(w_in, m_w_in, v_w_in, True), w_out=(w_out, m_w_out, v_w_out, False),
                     w_up=(w_up, m_w_up, v_w_up, True), w_down=(w_down, m_w_down, v_w_down, False))
    names = ("w_in", "w_out", "w_up", "w_down")
    bufs = {name: None for name in names}
    after = s_delta
    for l in reversed(range(depth)):
        if l == 0:
            swaps[0] = finish_scatter(0, pending, after)
        layer_grads = [g.reshape(-1, g.shape[-1]) for g in _split_wait(swaps[l], after, f"swap_wait_{l}")]
        for t, name in enumerate(names):
            wt, mt, vt, transposed = big_given[name]
            g = layer_grads[t].T if transposed else layer_grads[t]
            bufs[name] = _adamw_layer(l, wt, g, mt, vt, bufs[name], f"adamw_{name}_{l}")
            after = bufs[name][1]
    for name in names:
        grads[name], deltas[name], new_m[name], new_v[name] = bufs[name]

    order = ("attn_norm", "w_in", "a_q_gain", "a_k_gain", "a_sinks", "c_q_gain", "c_k_gain", "rel_bias_table",
             "mix_out_gain", "w_out", "ffn_norm", "w_up", "conv_w", "conv_b", "w_down")
    return (loss, grad_x, *[grads[k] for k in order], *[deltas[k] for k in order], *[new_m[k] for k in order],
            *[new_v[k] for k in order])
```

```python
import numpy as np
import jax
import jax.numpy as jnp
from jax import lax
from jax.experimental import pallas as pl
from jax.experimental.pallas import tpu as pltpu

F32 = jnp.float32
BF16 = jnp.bfloat16
MESH = pl.DeviceIdType.MESH

HEAD_DIM = 64
BLOCK = 128
LANES = 128
EPS = 1e-6
NEG_INF = -1e30
WINDOW_A = 128
DILATED_PAIRS = ((128, 1), (512, 4), (2048, 16))
N_BUCKETS = 32
T5_MAX_DIST = 2048
CONV_WIDTH = 3
ADAM_LR = 0.001
ADAM_B1 = 0.9
ADAM_B2 = 0.999
ADAM_EPS = 1e-08
ADAM_WD = 0.01
ADAM_STEP = 10
N_CHIPS = 4
N_DEVICES = 8
VMEM_LIMIT_BYTES = 48 * 1024 * 1024
QK_SCALE = HEAD_DIM ** -0.5


def _params(sem=None):
    return pltpu.CompilerParams(dimension_semantics=sem, vmem_limit_bytes=VMEM_LIMIT_BYTES)


def _div_tile(n, cap, mult):
    best = None
    for t in range(mult, min(n, cap) + 1, mult):
        if n % t == 0:
            best = t
    return n if best is None else best


def _dot(a, b):
    return lax.dot_general(a, b, (((1,), (0,)), ((), ())), preferred_element_type=F32)


def _dot_nt(a, b):
    return lax.dot_general(a, b, (((1,), (1,)), ((), ())), preferred_element_type=F32)


def _dot_tn(a, b):
    return lax.dot_general(a, b, (((0,), (0,)), ((), ())), preferred_element_type=F32)


def _split_dot(x, m):
    hi = x.astype(BF16)
    lo = (x - hi.astype(F32)).astype(BF16)
    return _dot(hi, m) + _dot(lo, m)


class _Cfg:
    def __init__(self, d_model, d_ff):
        nh = d_model // HEAD_DIM
        self.d = d_model
        self.f = d_ff
        self.nha = nh // 4
        self.nkva = self.nha // 4
        self.nhb = nh // 4
        self.nhc = nh // 2
        self.a_q = self.nha * HEAD_DIM
        self.a_kv = self.nkva * HEAD_DIM
        self.b_w = self.nhb * HEAD_DIM
        self.c_w = self.nhc * HEAD_DIM
        sizes = [self.a_q, self.a_kv, self.a_kv, self.b_w, self.b_w, self.b_w, self.c_w, self.c_w, self.c_w]
        starts = [0] + [int(s) for s in np.cumsum(sizes)[:-1]]
        self.sections = list(zip(starts, sizes))
        self.in_width = int(sum(sizes))
        assert all(s % LANES == 0 for s in sizes)


def _matmul(a, b, mode, out_dtype, name, tm=512, tn=512, tk=512, residual=None):
    if mode == "tn":
        kdim, m = a.shape
    else:
        m, kdim = a.shape
    n = b.shape[0] if mode == "nt" else b.shape[1]
    tm, tn, tk = _div_tile(m, tm, LANES), _div_tile(n, tn, LANES), _div_tile(kdim, tk, LANES)
    nk = kdim // tk
    if mode == "tn":
        a_spec = pl.BlockSpec((tk, tm), lambda i, j, k: (k, i))
    else:
        a_spec = pl.BlockSpec((tm, tk), lambda i, j, k: (i, k))
    if mode == "nt":
        b_spec = pl.BlockSpec((tn, tk), lambda i, j, k: (j, k))
    else:
        b_spec = pl.BlockSpec((tk, tn), lambda i, j, k: (k, j))
    dot = {"nn": _dot, "nt": _dot_nt, "tn": _dot_tn}[mode]
    o_spec = pl.BlockSpec((tm, tn), lambda i, j, k: (i, j))
    in_specs = [a_spec, b_spec]
    args = [a, b]
    if residual is not None:
        in_specs.append(o_spec)
        args.append(residual)

    def body(*refs):
        if residual is None:
            a_ref, b_ref, o_ref, acc = refs
        else:
            a_ref, b_ref, r_ref, o_ref, acc = refs
        k = pl.program_id(2)

        @pl.when(k == 0)
        def _():
            acc[...] = jnp.zeros_like(acc)

        acc[...] += dot(a_ref[...].astype(BF16), b_ref[...].astype(BF16))

        @pl.when(k == nk - 1)
        def _():
            r = acc[...]
            if residual is not None:
                r = r + r_ref[...]
            o_ref[...] = r.astype(out_dtype)

    return pl.pallas_call(
        body, name=name, grid=(m // tm, n // tn, nk), in_specs=in_specs, out_specs=o_spec,
        out_shape=jax.ShapeDtypeStruct((m, n), out_dtype), scratch_shapes=[pltpu.VMEM((tm, tn), F32)],
        compiler_params=_params(("parallel", "parallel", "arbitrary")),
    )(*args)


def _rmsnorm_fwd(x, g, name):
    s, d = x.shape
    ts = _div_tile(s, 256, 8)

    def body(x_ref, g_ref, o_ref):
        xv = x_ref[...]
        r = lax.rsqrt(jnp.mean(xv * xv, axis=-1, keepdims=True) + EPS)
        o_ref[...] = (xv * r * g_ref[...]).astype(BF16)

    return pl.pallas_call(
        body, name=name, grid=(s // ts,),
        in_specs=[pl.BlockSpec((ts, d), lambda i: (i, 0)), pl.BlockSpec((1, d), lambda i: (0, 0))],
        out_specs=pl.BlockSpec((ts, d), lambda i: (i, 0)), out_shape=jax.ShapeDtypeStruct((s, d), BF16),
        compiler_params=_params(("parallel",)),
    )(x, g)


def _rmsnorm_bwd(x, g, dh, dres, name):
    s, d = x.shape
    ts = _div_tile(s, 256, 8)

    def body(x_ref, g_ref, dh_ref, dres_ref, dx_ref, dg_ref):
        @pl.when(pl.program_id(0) == 0)
        def _():
            dg_ref[...] = jnp.zeros_like(dg_ref)

        xv = x_ref[...]
        r = lax.rsqrt(jnp.mean(xv * xv, axis=-1, keepdims=True) + EPS)
        xhat = xv * r
        dhv = dh_ref[...]
        dxhat = dhv * g_ref[...]
        dx_ref[...] = dres_ref[...] + r * (dxhat - xhat * jnp.mean(dxhat * xhat, axis=-1, keepdims=True))
        dg_ref[...] += jnp.sum(dhv * xhat, axis=0, keepdims=True)

    row = pl.BlockSpec((ts, d), lambda i: (i, 0))
    vec = pl.BlockSpec((1, d), lambda i: (0, 0))
    return pl.pallas_call(
        body, name=name, grid=(s // ts,), in_specs=[row, vec, row, row], out_specs=[row, vec],
        out_shape=[jax.ShapeDtypeStruct((s, d), F32), jax.ShapeDtypeStruct((1, d), F32)],
        compiler_params=_params(("arbitrary",)),
    )(x, g, dh, dres)


def _head_mean_matrix():
    idx = np.arange(LANES) // HEAD_DIM
    return jnp.asarray((idx[:, None] == idx[None, :]).astype(np.float32) / HEAD_DIM, dtype=BF16)


def _head_mean(y, m128):
    w = y.shape[1]
    parts = [_split_dot(y[:, c * LANES:(c + 1) * LANES], m128) for c in range(w // LANES)]
    return parts[0] if len(parts) == 1 else jnp.concatenate(parts, axis=1)


_NORMED_SECTIONS = (0, 1, 6, 7)


def _qk_prep(proj, gains, cfg, name):
    s = proj.shape[0]
    ts = _div_tile(s, 256, 16)
    m128 = _head_mean_matrix()

    def body(p_ref, m_ref, g0, g1, g6, g7, *outs):
        gref = dict(zip(_NORMED_SECTIONS, (g0, g1, g6, g7)))
        for idx, (st, w) in enumerate(cfg.sections):
            xv = p_ref[:, st:st + w]
            if idx in gref:
                r = lax.rsqrt(_head_mean(xv * xv, m_ref[...]) + EPS)
                xv = xv * r * gref[idx][...]
            outs[idx][...] = xv.astype(BF16)

    in_specs = [pl.BlockSpec((ts, cfg.in_width), lambda i: (i, 0)), pl.BlockSpec((LANES, LANES), lambda i: (0, 0))]
    in_specs += [pl.BlockSpec((1, cfg.sections[k][1]), lambda i: (0, 0)) for k in _NORMED_SECTIONS]
    out_specs = [pl.BlockSpec((ts, w), lambda i: (i, 0)) for _, w in cfg.sections]
    out_shape = [jax.ShapeDtypeStruct((s, w), BF16) for _, w in cfg.sections]
    return pl.pallas_call(
        body, name=name, grid=(s // ts,), in_specs=in_specs, out_specs=out_specs, out_shape=out_shape,
        compiler_params=_params(("parallel",)),
    )(proj, m128, *gains)


def _qk_prep_bwd(proj, gains, grads, cfg, name):
    s = proj.shape[0]
    ts = _div_tile(s, 128, 16)
    m128 = _head_mean_matrix()
    counts = [len(gl) for gl in grads]
    flat = [g for gl in grads for g in gl]

    def body(*refs):
        p_ref, m_ref = refs[0], refs[1]
        gref = dict(zip(_NORMED_SECTIONS, refs[2:6]))
        g_in = refs[6:6 + len(flat)]
        dp_ref = refs[6 + len(flat)]
        dgain = dict(zip(_NORMED_SECTIONS, refs[7 + len(flat):]))

        @pl.when(pl.program_id(0) == 0)
        def _():
            for k in _NORMED_SECTIONS:
                dgain[k][...] = jnp.zeros_like(dgain[k])

        pos = 0
        for idx, (st, w) in enumerate(cfg.sections):
            dy = g_in[pos][...]
            for extra in g_in[pos + 1:pos + counts[idx]]:
                dy = dy + extra[...]
            pos += counts[idx]
            if idx in gref:
                xv = p_ref[:, st:st + w]
                r = lax.rsqrt(_head_mean(xv * xv, m_ref[...]) + EPS)
                xhat = xv * r
                dxhat = dy * gref[idx][...]
                dgain[idx][...] += jnp.sum(dy * xhat, axis=0, keepdims=True)
                dy = r * (dxhat - xhat * _head_mean(dxhat * xhat, m_ref[...]))
            dp_ref[:, st:st + w] = dy.astype(BF16)

    in_specs = [pl.BlockSpec((ts, cfg.in_width), lambda i: (i, 0)), pl.BlockSpec((LANES, LANES), lambda i: (0, 0))]
    in_specs += [pl.BlockSpec((1, cfg.sections[k][1]), lambda i: (0, 0)) for k in _NORMED_SECTIONS]
    for idx, (_, w) in enumerate(cfg.sections):
        in_specs += [pl.BlockSpec((ts, w), lambda i: (i, 0))] * counts[idx]
    out_specs = [pl.BlockSpec((ts, cfg.in_width), lambda i: (i, 0))]
    out_specs += [pl.BlockSpec((1, cfg.sections[k][1]), lambda i: (0, 0)) for k in _NORMED_SECTIONS]
    out_shape = [jax.ShapeDtypeStruct((s, cfg.in_width), BF16)]
    out_shape += [jax.ShapeDtypeStruct((1, cfg.sections[k][1]), F32) for k in _NORMED_SECTIONS]
    return pl.pallas_call(
        body, name=name, grid=(s // ts,), in_specs=in_specs, out_specs=out_specs, out_shape=out_shape,
        compiler_params=_params(("arbitrary",)),
    )(proj, m128, *gains, *flat)


def _band_masks(max_dist):
    row = lax.broadcasted_iota(jnp.int32, (BLOCK, BLOCK), 0)
    col = lax.broadcasted_iota(jnp.int32, (BLOCK, BLOCK), 1)
    return row + BLOCK - col <= max_dist, col <= row


def _dilated_t(a, dil):
    s, w = a.shape
    return _keys_on_lanes(a.reshape(s // dil, dil * w), BLOCK)


def _undilated(at, dil):
    nblk, dw, _ = at.shape
    return jnp.transpose(at, (0, 2, 1)).reshape(nblk * BLOCK * dil, dw // dil)


def _banded_fwd(q, kt, v, bias, sinks, hq, hk, max_dist, dil, name):
    s = q.shape[0]
    wq, wk, sd, grp = hq * HEAD_DIM, hk * HEAD_DIM, s // dil, hq // hk
    nb = sd // BLOCK
    has_sink = sinks is not None

    def body(*refs):
        if has_sink:
            q_ref, ktp_ref, ktc_ref, vp_ref, vc_ref, b_ref, s_ref, o_ref, l_ref = refs
        else:
            q_ref, ktp_ref, ktc_ref, vp_ref, vc_ref, b_ref, o_ref, l_ref = refs
        i = pl.program_id(1)
        mprev, mcur = _band_masks(max_dist)
        mask = jnp.concatenate([jnp.logical_and(mprev, i > 0), mcur], axis=1)
        for h in range(hq):
            sq = slice(h * HEAD_DIM, (h + 1) * HEAD_DIM)
            sk = slice((h // grp) * HEAD_DIM, (h // grp + 1) * HEAD_DIM)
            kt = jnp.concatenate([ktp_ref[sk, :], ktc_ref[sk, :]], axis=1)
            vv = jnp.concatenate([vp_ref[:, sk], vc_ref[:, sk]], axis=0)
            sc = jnp.where(mask, _dot(q_ref[:, sq], kt) * QK_SCALE + b_ref[h], NEG_INF)
            m = jnp.max(sc, axis=-1, keepdims=True)
            if has_sink:
                m = jnp.maximum(m, s_ref[h])
            p = jnp.exp(sc - m)
            den = jnp.sum(p, axis=-1, keepdims=True)
            if has_sink:
                den = den + jnp.exp(s_ref[h] - m)
            o_ref[:, sq] = _dot(p.astype(BF16), vv) / den
            l_ref[:, sq] = jnp.broadcast_to(m + jnp.log(den), (BLOCK, HEAD_DIM))

    qspec = pl.BlockSpec((BLOCK, wq), lambda r, i: (i, r))
    kprev = pl.BlockSpec((BLOCK, wk), lambda r, i: (jnp.maximum(i - 1, 0), r))
    kcur = pl.BlockSpec((BLOCK, wk), lambda r, i: (i, r))
    ktprev = pl.BlockSpec((None, wk, BLOCK), lambda r, i: (jnp.maximum(i - 1, 0), r, 0))
    ktcur = pl.BlockSpec((None, wk, BLOCK), lambda r, i: (i, r, 0))
    in_specs = [qspec, ktprev, ktcur, kprev, kcur, pl.BlockSpec((hq, BLOCK, 2 * BLOCK), lambda r, i: (0, 0, 0))]
    v2 = v.reshape(sd, dil * wk)
    args = [q.reshape(sd, dil * wq), kt, kt, v2, v2, bias]
    if has_sink:
        in_specs.append(pl.BlockSpec(memory_space=pltpu.SMEM))
        args.append(sinks)
    out, lse = pl.pallas_call(
        body, name=name, grid=(dil, nb), in_specs=in_specs, out_specs=[qspec, qspec],
        out_shape=[jax.ShapeDtypeStruct((sd, dil * wq), F32)] * 2,
        compiler_params=_params(("parallel", "parallel")),
    )(*args)
    return out.reshape(s, wq), lse.reshape(s, wq)


def _banded_bwd(q, k, kt, v, o, lse, do, bias, sinks, dbias_init, hq, hk, max_dist, dil, name):
    s = q.shape[0]
    wq, wk, sd, grp = hq * HEAD_DIM, hk * HEAD_DIM, s // dil, hq // hk
    nb = sd // BLOCK
    has_sink = sinks is not None

    def body(*refs):
        (q_ref, qn_ref, qt_ref, qtn_ref, kp_ref, kc_ref, ktp_ref, ktc_ref, vtp_ref, vtc_ref, o_ref, on_ref, l_ref,
         ln_ref, do_ref, don_ref, dot_ref, dotn_ref, b_ref, dbi_ref) = refs[:20]
        rest = refs[20:]
        if has_sink:
            s_ref, dq_ref, dkt_ref, dvt_ref, db_ref, ds_ref = rest
        else:
            dq_ref, dkt_ref, dvt_ref, db_ref = rest
        j = pl.program_id(1)

        @pl.when(jnp.logical_and(pl.program_id(0) == 0, j == 0))
        def _():
            db_ref[...] = dbi_ref[...]
            if has_sink:
                ds_ref[...] = jnp.zeros_like(ds_ref)

        mprev_static, mcur = _band_masks(max_dist)
        mask = jnp.concatenate([jnp.logical_and(mprev_static, j > 0), mcur], axis=1)
        mnext = jnp.logical_and(mprev_static, j + 1 < nb)
        dkt_acc = [jnp.zeros((HEAD_DIM, BLOCK), F32) for _ in range(hk)]
        dvt_acc = [jnp.zeros((HEAD_DIM, BLOCK), F32) for _ in range(hk)]
        for h in range(hq):
            g = h // grp
            sq = slice(h * HEAD_DIM, (h + 1) * HEAD_DIM)
            sk = slice(g * HEAD_DIM, (g + 1) * HEAD_DIM)
            kt2 = jnp.concatenate([ktp_ref[sk, :], ktc_ref[sk, :]], axis=1)
            vt2 = jnp.concatenate([vtp_ref[sk, :], vtc_ref[sk, :]], axis=1)
            k2 = jnp.concatenate([kp_ref[:, sk], kc_ref[:, sk]], axis=0)
            doh = do_ref[:, sq]
            dohb = doh.astype(BF16)
            lcol = l_ref[:, h * HEAD_DIM:h * HEAD_DIM + 1]
            dcol = jnp.sum(doh * o_ref[:, sq], axis=-1, keepdims=True)
            sc = _dot(q_ref[:, sq], kt2) * QK_SCALE + b_ref[h]
            p = jnp.where(mask, jnp.exp(sc - lcol), 0.0)
            ds = p * (_dot(dohb, vt2) - dcol)
            dsb = ds.astype(BF16)
            dq_ref[:, sq] = _dot(dsb, k2) * QK_SCALE
            db_ref[h] += ds
            if has_sink:
                psink = jnp.exp(s_ref[h] - lcol)
                tot = jnp.sum(psink * dcol, axis=0, keepdims=True)
                ds_ref[h:h + 1, :] -= jnp.broadcast_to(tot, (1, LANES))
            don = don_ref[:, sq]
            lncol = ln_ref[:, h * HEAD_DIM:h * HEAD_DIM + 1]
            dncol = jnp.sum(don * on_ref[:, sq], axis=-1, keepdims=True)
            sn = _dot(qn_ref[:, sq], ktc_ref[sk, :]) * QK_SCALE + b_ref[h, :, 0:BLOCK]
            pn = jnp.where(mnext, jnp.exp(sn - lncol), 0.0)
            dsn = pn * (_dot(don.astype(BF16), vtc_ref[sk, :]) - dncol)
            dkt_acc[g] = dkt_acc[g] + (_dot(qt_ref[sq, :], dsb[:, BLOCK:])
                                       + _dot(qtn_ref[sq, :], dsn.astype(BF16))) * QK_SCALE
            dvt_acc[g] = dvt_acc[g] + (_dot(dot_ref[sq, :], p[:, BLOCK:].astype(BF16))
                                       + _dot(dotn_ref[sq, :], pn.astype(BF16)))
        for g in range(hk):
            sk = slice(g * HEAD_DIM, (g + 1) * HEAD_DIM)
            dkt_ref[sk, :] = dkt_acc[g]
            dvt_ref[sk, :] = dvt_acc[g]

    qcur = pl.BlockSpec((BLOCK, wq), lambda r, j: (j, r))
    qnext = pl.BlockSpec((BLOCK, wq), lambda r, j: (jnp.minimum(j + 1, nb - 1), r))
    qtcur = pl.BlockSpec((None, wq, BLOCK), lambda r, j: (j, r, 0))
    qtnext = pl.BlockSpec((None, wq, BLOCK), lambda r, j: (jnp.minimum(j + 1, nb - 1), r, 0))
    kprev = pl.BlockSpec((BLOCK, wk), lambda r, j: (jnp.maximum(j - 1, 0), r))
    kcur = pl.BlockSpec((BLOCK, wk), lambda r, j: (j, r))
    ktprev = pl.BlockSpec((None, wk, BLOCK), lambda r, j: (jnp.maximum(j - 1, 0), r, 0))
    ktcur = pl.BlockSpec((None, wk, BLOCK), lambda r, j: (j, r, 0))
    bspec = pl.BlockSpec((hq, BLOCK, 2 * BLOCK), lambda r, j: (0, 0, 0))
    q2, k2 = q.reshape(sd, dil * wq), k.reshape(sd, dil * wk)
    o2, l2, do2 = o.reshape(sd, dil * wq), lse.reshape(sd, dil * wq), do.reshape(sd, dil * wq)
    qt, vt, dot = _dilated_t(q, dil), _dilated_t(v, dil), _dilated_t(do.astype(BF16), dil)
    in_specs = [qcur, qnext, qtcur, qtnext, kprev, kcur, ktprev, ktcur, ktprev, ktcur, qcur, qnext, qcur, qnext,
                qcur, qnext, qtcur, qtnext, bspec, bspec]
    args = [q2, q2, qt, qt, k2, k2, kt, kt, vt, vt, o2, o2, l2, l2, do2, do2, dot, dot, bias, dbias_init]
    out_specs = [qcur, ktcur, ktcur, bspec]
    out_shape = [jax.ShapeDtypeStruct((sd, dil * wq), F32), jax.ShapeDtypeStruct((nb, dil * wk, BLOCK), F32),
                 jax.ShapeDtypeStruct((nb, dil * wk, BLOCK), F32), jax.ShapeDtypeStruct((hq, BLOCK, 2 * BLOCK), F32)]
    if has_sink:
        in_specs.append(pl.BlockSpec(memory_space=pltpu.SMEM))
        args.append(sinks)
        out_specs.append(pl.BlockSpec((hq, LANES), lambda r, j: (0, 0)))
        out_shape.append(jax.ShapeDtypeStruct((hq, LANES), F32))
    res = pl.pallas_call(
        body, name=name, grid=(dil, nb), in_specs=in_specs, out_specs=out_specs, out_shape=out_shape,
        compiler_params=_params(("arbitrary", "arbitrary")),
    )(*args)
    dq, dk, dv, dbias = res[0].reshape(s, wq), _undilated(res[1], dil), _undilated(res[2], dil), res[3]
    return dq, dk, dv, dbias, (res[4][:, 0] if has_sink else None)


def _neg_softplus(z):
    return -(jnp.maximum(z, 0.0) + jnp.log(1.0 + jnp.exp(-jnp.abs(z))))


SB_CHUNK = 256
HEADS_PER_PAIR = LANES // HEAD_DIM


def _tri(kind):
    row = lax.broadcasted_iota(jnp.int32, (SB_CHUNK, SB_CHUNK), 0)
    col = lax.broadcasted_iota(jnp.int32, (SB_CHUNK, SB_CHUNK), 1)
    return {"ge": row >= col, "lt": row < col, "le": row <= col}[kind].astype(BF16)


def _keys_on_lanes(a, rows):
    s, w = a.shape
    return jnp.transpose(a.reshape(s // rows, rows, w), (0, 2, 1))


def _sb_mask(i, jj):
    row = lax.broadcasted_iota(jnp.int32, (BLOCK, SB_CHUNK), 0)
    col = lax.broadcasted_iota(jnp.int32, (BLOCK, SB_CHUNK), 1)
    return col < row + (i * BLOCK - jj * SB_CHUNK)


def _sb_trips(i):
    return (i * BLOCK) // (2 * SB_CHUNK) + 1


def _sb_rows(jj, n):
    return pl.ds(pl.multiple_of(jj * SB_CHUNK, SB_CHUNK), n * SB_CHUNK)


def _sb_fwd(q, kt, v, name):
    s, w = q.shape
    npair, nb, nc = w // LANES, s // BLOCK, s // SB_CHUNK

    def body(q_ref, kt_ref, v_ref, o_ref, t_ref):
        i = pl.program_id(1)
        lincl = _tri("ge")
        heads = [slice(hh * HEAD_DIM, (hh + 1) * HEAD_DIM) for hh in range(HEADS_PER_PAIR)]
        qs = [q_ref[:, sl] for sl in heads]

        def trip(t, carry):
            lo, hi = 2 * t, 2 * t + 1
            mlo, mhi = _sb_mask(i, lo), _sb_mask(i, hi)
            new = []
            for hh, sl in enumerate(heads):
                o_acc, rem = carry[hh]
                zhi = _dot(qs[hh], kt_ref[hi, sl, :]) * QK_SCALE
                zlo = _dot(qs[hh], kt_ref[lo, sl, :]) * QK_SCALE
                lrhi = jnp.where(mhi, _neg_softplus(zhi), 0.0)
                lrlo = jnp.where(mlo, _neg_softplus(zlo), 0.0)
                tothi = jnp.sum(lrhi, axis=-1, keepdims=True)
                ahi = jnp.where(mhi, jnp.exp(zhi + (rem + _split_dot(lrhi, lincl))), 0.0)
                alo = jnp.where(mlo, jnp.exp(zlo + (rem + tothi + _split_dot(lrlo, lincl))), 0.0)
                a = jnp.concatenate([alo, ahi], axis=1).astype(BF16)
                new.append((o_acc + _dot(a, v_ref[_sb_rows(lo, 2), sl]),
                            rem + tothi + jnp.sum(lrlo, axis=-1, keepdims=True)))
            return tuple(new)

        init = tuple((jnp.zeros((BLOCK, HEAD_DIM), F32), jnp.zeros((BLOCK, 1), F32)) for _ in heads)
        trips = _sb_trips(i)
        carry = lax.fori_loop(0, trips, lambda t, cr: trip(trips - 1 - t, cr), init)
        for hh, sl in enumerate(heads):
            o_ref[:, sl] = carry[hh][0]
            t_ref[:, sl] = jnp.broadcast_to(carry[hh][1], (BLOCK, HEAD_DIM))

    qspec = pl.BlockSpec((BLOCK, LANES), lambda p, i: (i, p))
    return pl.pallas_call(
        body, name=name, grid=(npair, nb),
        in_specs=[qspec, pl.BlockSpec((nc, LANES, SB_CHUNK), lambda p, i: (0, p, 0)),
                  pl.BlockSpec((s, LANES), lambda p, i: (0, p))],
        out_specs=[qspec, qspec], out_shape=[jax.ShapeDtypeStruct((s, w), F32)] * 2,
        compiler_params=_params(("parallel", "parallel")),
    )(q, kt, v)


def _sb_bwd(q, k, kt, v, tot, do, name):
    s, w = q.shape
    npair, nb, nc = w // LANES, s // BLOCK, s // SB_CHUNK
    dob = do.astype(BF16)

    def body(q_ref, qt_ref, k_ref, kt_ref, vt_ref, t_ref, do_ref, dot_ref, dq_ref, dkt_ref, dvt_ref):
        i = pl.program_id(1)

        @pl.when(i == 0)
        def _():
            dkt_ref[...] = jnp.zeros_like(dkt_ref)
            dvt_ref[...] = jnp.zeros_like(dvt_ref)

        lbefore = _tri("lt")
        lupto = _tri("le")
        heads = [slice(hh * HEAD_DIM, (hh + 1) * HEAD_DIM) for hh in range(HEADS_PER_PAIR)]
        qs = [q_ref[:, sl] for sl in heads]
        qts = [qt_ref[sl, :] for sl in heads]
        dos = [do_ref[:, sl] for sl in heads]
        dots = [dot_ref[sl, :] for sl in heads]
        totals = [t_ref[:, sl.start:sl.start + 1] for sl in heads]

        def trip(t, carry):
            chunks = (2 * t, 2 * t + 1)
            masks = [_sb_mask(i, jj) for jj in chunks]
            new = []
            for hh, sl in enumerate(heads):
                dq_acc, plr, pg = carry[hh]
                zs = [_dot(qs[hh], kt_ref[jj, sl, :]) * QK_SCALE for jj in chunks]
                lrs = [jnp.where(m, _neg_softplus(z), 0.0) for m, z in zip(masks, zs)]
                lr_sums = [jnp.sum(lr, axis=-1, keepdims=True) for lr in lrs]
                before = [plr, plr + lr_sums[0]]
                avs = [jnp.where(m, jnp.exp(z + (totals[hh] - (b + _split_dot(lr, lbefore)))), 0.0)
                       for m, z, lr, b in zip(masks, zs, lrs, before)]
                gs = [_dot(dos[hh], vt_ref[jj, sl, :]) * a for jj, a in zip(chunks, avs)]
                g_sums = [jnp.sum(g, axis=-1, keepdims=True) for g in gs]
                upto = [pg, pg + g_sums[0]]
                dzs = [(jnp.where(m, g - jnp.exp(z + lr) * (u + _split_dot(g, lupto)), 0.0) * QK_SCALE).astype(BF16)
                       for m, z, lr, g, u in zip(masks, zs, lrs, gs, upto)]
                for jj, dzb, a in zip(chunks, dzs, avs):
                    dkt_ref[jj, sl, :] += _dot(qts[hh], dzb)
                    dvt_ref[jj, sl, :] += _dot(dots[hh], a.astype(BF16))
                dz2 = jnp.concatenate(dzs, axis=1)
                new.append((dq_acc + _dot(dz2, k_ref[_sb_rows(chunks[0], 2), sl]), plr + lr_sums[0] + lr_sums[1],
                            pg + g_sums[0] + g_sums[1]))
            return tuple(new)

        zero = jnp.zeros((BLOCK, 1), F32)
        init = tuple((jnp.zeros((BLOCK, HEAD_DIM), F32), zero, zero) for _ in heads)
        carry = lax.fori_loop(0, _sb_trips(i), trip, init)
        for hh, sl in enumerate(heads):
            dq_ref[:, sl] = carry[hh][0]

    qspec = pl.BlockSpec((BLOCK, LANES), lambda p, i: (i, p))
    qtspec = pl.BlockSpec((None, LANES, BLOCK), lambda p, i: (i, p, 0))
    kspec = pl.BlockSpec((s, LANES), lambda p, i: (0, p))
    ktspec = pl.BlockSpec((nc, LANES, SB_CHUNK), lambda p, i: (0, p, 0))
    dq, dkt, dvt = pl.pallas_call(
        body, name=name, grid=(npair, nb), in_specs=[qspec, qtspec, kspec, ktspec, ktspec, qspec, qspec, qtspec],
        out_specs=[qspec, ktspec, ktspec],
        out_shape=[jax.ShapeDtypeStruct((s, w), F32)] + [jax.ShapeDtypeStruct((nc, w, SB_CHUNK), F32)] * 2,
        compiler_params=_params(("parallel", "arbitrary")),
    )(q, _keys_on_lanes(q, BLOCK), k, kt, _keys_on_lanes(v, SB_CHUNK), tot, dob, _keys_on_lanes(dob, BLOCK))

    def rows_first(t):
        return jnp.transpose(t, (0, 2, 1)).reshape(s, w)

    return dq, rows_first(dkt), rows_first(dvt)


def _group_norm(xv, g):
    r = lax.rsqrt(jnp.mean(xv * xv, axis=-1, keepdims=True) + EPS)
    return xv * r * g


def _mix_fwd(oa, ob, ocs, lses, gain, cfg, name):
    s = oa.shape[0]
    ts = _div_tile(s, 256, 16)
    aq, bw, cw = cfg.a_q, cfg.b_w, cfg.c_w

    def body(oa_ref, ob_ref, c1, c2, c3, l1, l2, l3, g_ref, mix_ref, oc_ref, lse_ref):
        m = jnp.maximum(jnp.maximum(l1[...], l2[...]), l3[...])
        e1, e2, e3 = jnp.exp(l1[...] - m), jnp.exp(l2[...] - m), jnp.exp(l3[...] - m)
        den = e1 + e2 + e3
        oc = (e1 * c1[...] + e2 * c2[...] + e3 * c3[...]) / den
        oc_ref[...] = oc
        lse_ref[...] = m + jnp.log(den)
        mix_ref[:, 0:aq] = _group_norm(oa_ref[...], g_ref[:, 0:aq]).astype(BF16)
        mix_ref[:, aq:aq + bw] = _group_norm(ob_ref[...], g_ref[:, aq:aq + bw]).astype(BF16)
        mix_ref[:, aq + bw:] = _group_norm(oc, g_ref[:, aq + bw:]).astype(BF16)

    def row(wd):
        return pl.BlockSpec((ts, wd), lambda i: (i, 0))

    return pl.pallas_call(
        body, name=name, grid=(s // ts,),
        in_specs=[row(aq), row(bw)] + [row(cw)] * 6 + [pl.BlockSpec((1, cfg.d), lambda i: (0, 0))],
        out_specs=[row(cfg.d), row(cw), row(cw)],
        out_shape=[jax.ShapeDtypeStruct((s, cfg.d), BF16), jax.ShapeDtypeStruct((s, cw), F32),
                   jax.ShapeDtypeStruct((s, cw), F32)],
        compiler_params=_params(("parallel",)),
    )(oa, ob, *ocs, *lses, gain)


def _mix_bwd(dmix, oa, ob, oc, gain, cfg, name):
    s = oa.shape[0]
    ts = _div_tile(s, 256, 8)
    aq, bw, cw = cfg.a_q, cfg.b_w, cfg.c_w

    def body(dm_ref, oa_ref, ob_ref, oc_ref, g_ref, da_ref, db_ref, dc_ref, dg_ref):
        @pl.when(pl.program_id(0) == 0)
        def _():
            dg_ref[...] = jnp.zeros_like(dg_ref)

        for x_ref, dx_ref, lo, hi in ((oa_ref, da_ref, 0, aq), (ob_ref, db_ref, aq, aq + bw),
                                      (oc_ref, dc_ref, aq + bw, aq + bw + cw)):
            xv = x_ref[...]
            dy = dm_ref[:, lo:hi]
            r = lax.rsqrt(jnp.mean(xv * xv, axis=-1, keepdims=True) + EPS)
            xhat = xv * r
            dxhat = dy * g_ref[:, lo:hi]
            dx_ref[...] = r * (dxhat - xhat * jnp.mean(dxhat * xhat, axis=-1, keepdims=True))
            dg_ref[:, lo:hi] += jnp.sum(dy * xhat, axis=0, keepdims=True)

    def row(wd):
        return pl.BlockSpec((ts, wd), lambda i: (i, 0))

    vec = pl.BlockSpec((1, cfg.d), lambda i: (0, 0))
    return pl.pallas_call(
        body, name=name, grid=(s // ts,), in_specs=[row(cfg.d), row(aq), row(bw), row(cw), vec],
        out_specs=[row(aq), row(bw), row(cw), vec],
        out_shape=[jax.ShapeDtypeStruct((s, aq), F32), jax.ShapeDtypeStruct((s, bw), F32),
                   jax.ShapeDtypeStruct((s, cw), F32), jax.ShapeDtypeStruct((1, cfg.d), F32)],
        compiler_params=_params(("arbitrary",)),
    )(dmix, oa, ob, oc, gain)


def _bias_table_grad(dbiases, buckets, name):
    outs = []
    for idx, (db, bk) in enumerate(zip(dbiases, buckets)):
        h = db.shape[0]

        def body(db_ref, bk_ref, o_ref):
            xv = db_ref[0]
            ids = bk_ref[...]
            lane = lax.broadcasted_iota(jnp.int32, (1, LANES), 1)
            acc = jnp.zeros((1, LANES), F32)
            for b in range(N_BUCKETS):
                tot = jnp.sum(jnp.where(ids == b, xv, 0.0), axis=0, keepdims=True)
                tot = jnp.sum(tot, axis=1, keepdims=True)
                acc = jnp.where(lane == b, tot, acc)
            o_ref[0] = acc

        outs.append(pl.pallas_call(
            body, name=f"{name}_{idx}", grid=(h,),
            in_specs=[pl.BlockSpec((1, BLOCK, 2 * BLOCK), lambda i: (i, 0, 0)),
                      pl.BlockSpec((BLOCK, 2 * BLOCK), lambda i: (0, 0))],
            out_specs=pl.BlockSpec((1, 1, LANES), lambda i: (i, 0, 0)),
            out_shape=jax.ShapeDtypeStruct((h, 1, LANES), F32), compiler_params=_params(("parallel",)),
        )(db, bk)[:, 0, :])
    return outs


def _shift_down(u, n, rows):
    return jnp.where(rows >= n, pltpu.roll(u, n, 0), 0.0)


def _shift_up(u, n, rows, s):
    return jnp.where(rows < s - n, pltpu.roll(u, s - n, 0), 0.0)


def _conv(u, w_ref, b_ref, rows):
    return (b_ref[...] + w_ref[0:1, :] * _shift_down(u, 2, rows) + w_ref[1:2, :] * _shift_down(u, 1, rows)
            + w_ref[2:3, :] * u)


def _conv_act_fwd(u, conv_w, conv_b, f, name):
    s = u.shape[0]
    nf = f // LANES

    def body(ug_ref, uu_ref, wg_ref, wu_ref, bg_ref, bu_ref, act_ref):
        rows = lax.broadcasted_iota(jnp.int32, (s, LANES), 0)
        gate = _conv(ug_ref[...], wg_ref, bg_ref, rows)
        up = _conv(uu_ref[...], wu_ref, bu_ref, rows)
        act_ref[...] = (gate * jax.nn.sigmoid(gate) * up).astype(BF16)

    def col(rws, off):
        return pl.BlockSpec((rws, LANES), lambda j: (0, j + off))

    return pl.pallas_call(
        body, name=name, grid=(nf,),
        in_specs=[col(s, 0), col(s, nf), col(CONV_WIDTH, 0), col(CONV_WIDTH, nf), col(1, 0), col(1, nf)],
        out_specs=col(s, 0), out_shape=jax.ShapeDtypeStruct((s, f), BF16), compiler_params=_params(("parallel",)),
    )(u, u, conv_w, conv_w, conv_b, conv_b)


def _conv_act_bwd(u, dact, conv_w, conv_b, f, name):
    s = u.shape[0]
    nf = f // LANES

    def body(ug_ref, uu_ref, da_ref, wg_ref, wu_ref, bg_ref, bu_ref, dug_ref, duu_ref, dwg_ref, dwu_ref, dbg_ref,
             dbu_ref):
        rows = lax.broadcasted_iota(jnp.int32, (s, LANES), 0)
        ug, uu = ug_ref[...], uu_ref[...]
        gate = _conv(ug, wg_ref, bg_ref, rows)
        up = _conv(uu, wu_ref, bu_ref, rows)
        sg = jax.nn.sigmoid(gate)
        da = da_ref[...]
        dgate = da * up * (sg * (1.0 + gate * (1.0 - sg)))
        dup = da * (gate * sg)
        for du, uv, w_ref, du_ref, dw_ref, db_ref in ((dgate, ug, wg_ref, dug_ref, dwg_ref, dbg_ref),
                                                     (dup, uu, wu_ref, duu_ref, dwu_ref, dbu_ref)):
            du_ref[...] = (w_ref[2:3, :] * du + w_ref[1:2, :] * _shift_up(du, 1, rows, s)
                           + w_ref[0:1, :] * _shift_up(du, 2, rows, s)).astype(BF16)
            dw_ref[0:1, :] = jnp.sum(du * _shift_down(uv, 2, rows), axis=0, keepdims=True)
            dw_ref[1:2, :] = jnp.sum(du * _shift_down(uv, 1, rows), axis=0, keepdims=True)
            dw_ref[2:3, :] = jnp.sum(du * uv, axis=0, keepdims=True)
            db_ref[...] = jnp.sum(du, axis=0, keepdims=True)

    def col(rws, off):
        return pl.BlockSpec((rws, LANES), lambda j: (0, j + off))

    return pl.pallas_call(
        body, name=name, grid=(nf,),
        in_specs=[col(s, 0), col(s, nf), col(s, 0), col(CONV_WIDTH, 0), col(CONV_WIDTH, nf), col(1, 0), col(1, nf)],
        out_specs=[col(s, 0), col(s, 0), col(CONV_WIDTH, 0), col(CONV_WIDTH, 0), col(1, 0), col(1, 0)],
        out_shape=[jax.ShapeDtypeStruct((s, f), BF16)] * 2 + [jax.ShapeDtypeStruct((CONV_WIDTH, f), F32)] * 2
        + [jax.ShapeDtypeStruct((1, f), F32)] * 2,
        compiler_params=_params(("parallel",)),
    )(u, u, dact, conv_w, conv_w, conv_b, conv_b)


def _loss_head(y, target, name):
    s, d = y.shape
    ts = _div_tile(s, 256, 8)

    def body(y_ref, t_ref, dy_ref, l_ref):
        @pl.when(pl.program_id(0) == 0)
        def _():
            l_ref[...] = jnp.zeros_like(l_ref)

        err = y_ref[...] - t_ref[...]
        dy_ref[...] = err * (1.0 / d)
        tot = jnp.sum(jnp.sum(err * err, axis=0, keepdims=True), axis=1, keepdims=True) * (0.5 / d)
        l_ref[...] += jnp.broadcast_to(tot, l_ref.shape)

    row = pl.BlockSpec((ts, d), lambda i: (i, 0))
    return pl.pallas_call(
        body, name=name, grid=(s // ts,), in_specs=[row, row],
        out_specs=[row, pl.BlockSpec((8, LANES), lambda i: (0, 0))],
        out_shape=[jax.ShapeDtypeStruct((s, d), F32), jax.ShapeDtypeStruct((8, LANES), F32)],
        compiler_params=_params(("arbitrary",)),
    )(y, target)


def _adamw(w, g, m, v, name):
    r, c = w.shape
    tr = _div_tile(r, max(8, (1 << 18) // c // 8 * 8), 8)
    c1 = 1.0 - ADAM_B1 ** ADAM_STEP
    c2 = 1.0 - ADAM_B2 ** ADAM_STEP

    def body(w_ref, g_ref, m_ref, v_ref, d_ref, nm_ref, nv_ref):
        gv = g_ref[...]
        nm = ADAM_B1 * m_ref[...] + (1.0 - ADAM_B1) * gv
        nv = ADAM_B2 * v_ref[...] + (1.0 - ADAM_B2) * (gv * gv)
        d_ref[...] = -ADAM_LR * ((nm / c1) / (jnp.sqrt(nv / c2) + ADAM_EPS) + ADAM_WD * w_ref[...])
        nm_ref[...] = nm
        nv_ref[...] = nv

    spec = pl.BlockSpec((tr, c), lambda i: (i, 0))
    return pl.pallas_call(
        body, name=name, grid=(r // tr,), in_specs=[spec] * 4, out_specs=[spec] * 3,
        out_shape=[jax.ShapeDtypeStruct((r, c), F32)] * 3, compiler_params=_params(("parallel",)),
    )(w, g, m, v)


def _adamw_layer(layer, w, g, m, v, bufs, name):
    depth, r, c = w.shape
    tr = _div_tile(r, max(8, (1 << 17) // c // 8 * 8), 8)
    c1 = 1.0 - ADAM_B1 ** ADAM_STEP
    c2 = 1.0 - ADAM_B2 ** ADAM_STEP

    def body(*refs):
        w_ref, g_ref, m_ref, v_ref = refs[:4]
        go_ref, d_ref, nm_ref, nv_ref = refs[-4:]
        gv = g_ref[...]
        nm = ADAM_B1 * m_ref[...] + (1.0 - ADAM_B1) * gv
        nv = ADAM_B2 * v_ref[...] + (1.0 - ADAM_B2) * (gv * gv)
        d_ref[...] = -ADAM_LR * ((nm / c1) / (jnp.sqrt(nv / c2) + ADAM_EPS) + ADAM_WD * w_ref[...])
        nm_ref[...] = nm
        nv_ref[...] = nv
        go_ref[...] = gv

    lay = pl.BlockSpec((None, tr, c), lambda i: (layer, i, 0))
    in_specs = [lay, pl.BlockSpec((tr, c), lambda i: (i, 0)), lay, lay]
    args = [w, g, m, v]
    aliases = {}
    if bufs is not None:
        in_specs += [pl.BlockSpec(memory_space=pl.ANY)] * 4
        args += list(bufs)
        aliases = {4 + k: k for k in range(4)}
    return pl.pallas_call(
        body, name=name, grid=(r // tr,), in_specs=in_specs, out_specs=[lay] * 4,
        out_shape=[jax.ShapeDtypeStruct((depth, r, c), F32)] * 4, input_output_aliases=aliases,
        compiler_params=_params(("parallel",)),
    )(*args)


def _mesh_pos():
    return lax.axis_index("x"), lax.axis_index("y"), lax.axis_index("c")


def _flip(v, bit):
    return 1 - v if bit else v


def _sum_parts(parts, name):
    _, r, c = parts.shape
    tr = _div_tile(r, 256, 16)

    def body(p_ref, o_ref):
        acc = p_ref[0].astype(F32)
        for src in range(1, N_DEVICES):
            acc = acc + p_ref[src].astype(F32)
        o_ref[...] = acc

    return pl.pallas_call(
        body, name=name, grid=(r // tr,), in_specs=[pl.BlockSpec((N_DEVICES, tr, c), lambda i: (0, i, 0))],
        out_specs=pl.BlockSpec((tr, c), lambda i: (i, 0)), out_shape=jax.ShapeDtypeStruct((r, c), F32),
        compiler_params=_params(("parallel",)),
    )(parts)


def _split_start(srcs, lands, plan, ncopies, name):
    nbuf = len(srcs) + len(lands)

    def body(*refs):
        bufs = refs[:nbuf]
        send_sem, recv_sem, token = refs[nbuf], refs[nbuf + 1], refs[-1]
        for k, (src, dst, dev) in enumerate(plan(bufs[:len(srcs)], bufs[len(srcs):])):
            pltpu.make_async_remote_copy(src_ref=src, dst_ref=dst, send_sem=send_sem.at[k], recv_sem=recv_sem.at[k],
                                         device_id=dev, device_id_type=MESH).start()
        token[...] = jnp.zeros_like(token)

    hbm = pl.BlockSpec(memory_space=pltpu.HBM)
    sem = pl.BlockSpec(memory_space=pltpu.SEMAPHORE)
    operands = [pltpu.with_memory_space_constraint(a, pltpu.HBM) for a in (*srcs, *lands)]
    outs = pl.pallas_call(
        body, name=name, in_specs=[hbm] * nbuf,
        out_specs=(sem, sem, *[hbm] * nbuf, pl.BlockSpec(memory_space=pltpu.VMEM)),
        out_shape=(pltpu.SemaphoreType.DMA((ncopies,)), pltpu.SemaphoreType.DMA((ncopies,)),
                   *[pltpu.HBM(a.shape, a.dtype) for a in operands], jax.ShapeDtypeStruct((8, LANES), F32)),
        input_output_aliases={i: 2 + i for i in range(nbuf)},
        compiler_params=pltpu.CompilerParams(has_side_effects=pltpu.SideEffectType.DATAFLOW_SIDE_EFFECTING),
    )(*operands)
    handle = dict(send=outs[0], recv=outs[1], bufs=list(outs[2:2 + nbuf]), nsrc=len(srcs), plan=plan)
    return handle, outs[-1]


def _split_wait(handle, after, name):
    nbuf, nsrc, plan = len(handle["bufs"]), handle["nsrc"], handle["plan"]

    def body(*refs):
        bufs = refs[:nbuf]
        send_sem, recv_sem = refs[nbuf], refs[nbuf + 1]
        for k, (src, dst, dev) in enumerate(plan(bufs[:nsrc], bufs[nsrc:])):
            copy = pltpu.make_async_remote_copy(src_ref=src, dst_ref=dst, send_sem=send_sem.at[k],
                                                recv_sem=recv_sem.at[k], device_id=dev, device_id_type=MESH)
            copy.wait_send()
            copy.wait_recv()

    hbm = pl.BlockSpec(memory_space=pltpu.HBM)
    sem = pl.BlockSpec(memory_space=pltpu.SEMAPHORE)
    outs = pl.pallas_call(
        body, name=name, in_specs=[hbm] * nbuf + [sem, sem, pl.BlockSpec(memory_space=pl.ANY)],
        out_specs=[hbm] * nbuf, out_shape=[pltpu.HBM(a.shape, a.dtype) for a in handle["bufs"]],
        input_output_aliases={i: i for i in range(nbuf)},
        compiler_params=pltpu.CompilerParams(has_side_effects=pltpu.SideEffectType.DATAFLOW_SIDE_EFFECTING),
    )(*handle["bufs"], handle["send"], handle["recv"], after)
    return list(outs[nsrc:])


def _own_slot(shape, dtype, block, index):
    return lax.dynamic_update_slice(lax.empty(shape, dtype), block[None], (index,) + (0,) * block.ndim)


def _gather_plan(srcs, lands):
    x, y, c = _mesh_pos()
    return [(src, land.at[2 * x + y], (*chip, c))
            for src, land in zip(srcs, lands) for chip in ((1 - x, y), (x, 1 - y), (1 - x, 1 - y))]


def _scatter_plan(srcs, lands):
    x, y, c = _mesh_pos()
    out = []
    for src, land in zip(srcs, lands):
        half = src.shape[1] // 2
        for d in range(1, N_DEVICES):
            p = (_flip(x, d & 4), _flip(y, d & 2), _flip(c, d & 1))
            out.append((src.at[2 * p[0] + p[1], pl.ds(p[2] * half, half), :], land.at[4 * x + 2 * y + c], p))
    return out


def _swap_plan(srcs, lands):
    x, y, c = _mesh_pos()
    return [(src, land.at[c], (x, y, 1 - c)) for src, land in zip(srcs, lands)]


class _Gathered:
    def __init__(self, groups):
        self.groups = groups
        self.ready = {}

    def get(self, name, after=None):
        if name not in self.ready:
            handle, names, wait_name = next(g for g in self.groups if name in g[1])
            for n, full in zip(names, _split_wait(handle, after, wait_name)):
                self.ready[n] = full.reshape(-1, full.shape[-1])
        return self.ready[name]


def _allreduce_small(flat, name):
    r = flat.shape[0]

    def body(x_ref, o_ref, buf, send_sems, recv_sems):
        x, y, c = _mesh_pos()
        me = 4 * x + 2 * y + c
        buf[me] = x_ref[...]
        started = []
        peers = [(_flip(x, d & 4), _flip(y, d & 2), _flip(c, d & 1)) for d in range(1, N_DEVICES)]
        for d, p in enumerate(peers):
            cp = pltpu.make_async_remote_copy(src_ref=x_ref, dst_ref=buf.at[me], send_sem=send_sems.at[d],
                                              recv_sem=recv_sems.at[d], device_id=p, device_id_type=MESH)
            cp.start()
            started.append(cp)
        for d, p in enumerate(peers):
            slot = buf.at[4 * p[0] + 2 * p[1] + p[2]]
            pltpu.make_async_remote_copy(src_ref=slot, dst_ref=slot, send_sem=send_sems.at[d], recv_sem=recv_sems.at[d],
                                         device_id=p, device_id_type=MESH).wait_recv()
        for cp in started:
            cp.wait_send()
        acc = buf[0]
        for src in range(1, N_DEVICES):
            acc = acc + buf[src]
        o_ref[...] = acc

    vm = pl.BlockSpec(memory_space=pltpu.VMEM)
    return pl.pallas_call(
        body, name=name, in_specs=[vm], out_specs=vm, out_shape=jax.ShapeDtypeStruct((r, LANES), F32),
        scratch_shapes=[pltpu.VMEM((N_DEVICES, r, LANES), F32), pltpu.SemaphoreType.DMA((N_DEVICES - 1,)),
                        pltpu.SemaphoreType.DMA((N_DEVICES - 1,))],
        compiler_params=pltpu.CompilerParams(vmem_limit_bytes=VMEM_LIMIT_BYTES),
    )(flat)


def _bucket_ids(dil):
    rel = (np.arange(BLOCK)[:, None] + BLOCK - np.arange(2 * BLOCK)[None, :]) * dil
    max_exact = N_BUCKETS // 2
    d = np.maximum(rel, 0)
    large = max_exact + (np.log(np.maximum(d, 1).astype(np.float32) / max_exact)
                         / np.float32(np.log(T5_MAX_DIST / max_exact)) * (N_BUCKETS - max_exact)).astype(np.int32)
    large = np.minimum(large, N_BUCKETS - 1)
    return np.where(d < max_exact, d, large).astype(np.int32)


def _block_bias(table, dil):
    onehot = (jnp.asarray(_bucket_ids(dil))[:, :, None] == jnp.arange(N_BUCKETS)[None, None, :]).astype(F32)
    return jnp.einsum("ijb,bh->hij", onehot, table.astype(F32), precision=lax.Precision.HIGHEST)


def _tile_gain(g, n):
    return jnp.tile(g.reshape(1, HEAD_DIM), (1, n))


def _layer_fwd(x, p, cfg):
    w = p["weights"]
    h1 = _rmsnorm_fwd(x, p["attn_norm"], "attn_norm_fwd")
    proj = _matmul(h1, w.get("w_in_t", h1), "nt", F32, "in_proj", tm=1024, tn=768, tk=2048)
    aq, ak, av, bq, bk, bv, cq, ck, cv = _qk_prep(proj, p["gains"], cfg, "qk_prep")
    akt = _dilated_t(ak, 1)
    oa, lse_a = _banded_fwd(aq, akt, av, p["bias_a"], p["sinks"], cfg.nha, cfg.nkva, WINDOW_A - 1, 1, "swa_fwd")
    bkt = _keys_on_lanes(bk, SB_CHUNK)
    ob, tot_b = _sb_fwd(bq, bkt, bv, "stickbreak_fwd")
    ocs, lses, ckts = [], [], []
    for (window, dil), bias in zip(DILATED_PAIRS, p["bias_c"]):
        ckts.append(_dilated_t(ck, dil))
        o, l = _banded_fwd(cq, ckts[-1], cv, bias, None, cfg.nhc, cfg.nhc, window // dil, dil, f"dilated{dil}_fwd")
        ocs.append(o)
        lses.append(l)
    mix, oc, lse_c = _mix_fwd(oa, ob, ocs, lses, p["mix_gain"], cfg, "mix_fwd")
    xm = _matmul(mix, w.get("w_out", mix), "nn", F32, "out_proj", tm=1024, tn=512, tk=2048, residual=x)
    h2 = _rmsnorm_fwd(xm, p["ffn_norm"], "ffn_norm_fwd")
    u = _matmul(h2, w.get("w_up_t", h2), "nt", F32, "up_proj", tm=1024, tn=512, tk=2048)
    act = _conv_act_fwd(u, p["conv_w"], p["conv_b"], cfg.f, "conv_act_fwd")
    y = _matmul(act, w.get("w_down", act), "nn", F32, "down_proj", tm=1024, tn=1024, tk=512, residual=xm)
    saved = dict(x=x, h1=h1, proj=proj, q=(aq, ak, av, bq, bk, bv, cq, ck, cv), oa=oa, lse_a=lse_a, ob=ob,
                 tot_b=tot_b, akt=akt, bkt=bkt, ckts=ckts, oc=oc, lse_c=lse_c, mix=mix, xm=xm, h2=h2, u=u, act=act)
    return y, saved


def _layer_bwd(dy, sv, p, dbias, cfg, on_grad):
    aq, ak, av, bq, bk, bv, cq, ck, cv = sv["q"]
    w = p["weights"]
    anchor = on_grad(_matmul(sv["act"], dy, "tn", BF16, "down_proj_dw", tm=1408, tn=2048, tk=512))
    dact = _matmul(dy, w.get("w_down"), "nt", F32, "down_proj_dx", tm=1024, tn=512, tk=2048)
    dug, duu, dwg, dwu, dbg, dbu = _conv_act_bwd(sv["u"], dact, p["conv_w"], p["conv_b"] + anchor, cfg.f,
                                                 "conv_act_bwd")
    du = jnp.concatenate([dug, duu], axis=1)
    anchor = on_grad(_matmul(du, sv["h2"], "tn", BF16, "up_proj_dw", tm=1408, tn=2048, tk=512))
    dh2 = _matmul(du, w.get("w_up_t"), "nn", F32, "up_proj_dx", tm=1024, tn=2048, tk=512)
    dxm, g_ffn_norm = _rmsnorm_bwd(sv["xm"], p["ffn_norm"] + anchor, dh2, dy, "ffn_norm_bwd")
    anchor = on_grad(_matmul(sv["mix"], dxm, "tn", BF16, "out_proj_dw", tm=1024, tn=2048, tk=512))
    dmix = _matmul(dxm, w.get("w_out"), "nt", F32, "out_proj_dx", tm=1024, tn=512, tk=2048)
    doa, dob, doc, g_mix_gain = _mix_bwd(dmix, sv["oa"], sv["ob"], sv["oc"], p["mix_gain"] + anchor, cfg, "mix_bwd")
    daq, dak, dav, dbias_a, g_sinks = _banded_bwd(aq, ak, sv["akt"], av, sv["oa"], sv["lse_a"], doa, p["bias_a"],
                                                 p["sinks"], dbias[0], cfg.nha, cfg.nkva, WINDOW_A - 1, 1, "swa_bwd")
    dbq, dbk, dbv = _sb_bwd(bq, bk, sv["bkt"], bv, sv["tot_b"], dob, "stickbreak_bwd")
    dcq, dck, dcv, dbias_c = [], [], [], []
    for idx, ((window, dil), bias) in enumerate(zip(DILATED_PAIRS, p["bias_c"])):
        a, b, c, d, _ = _banded_bwd(cq, ck, sv["ckts"][idx], cv, sv["oc"], sv["lse_c"], doc, bias, None, dbias[1][idx],
                                    cfg.nhc, cfg.nhc, window // dil, dil, f"dilated{dil}_bwd")
        dcq.append(a)
        dck.append(b)
        dcv.append(c)
        dbias_c.append(d)
    dproj, g_aq, g_ak, g_cq, g_ck = _qk_prep_bwd(
        sv["proj"], p["gains"], [[daq], [dak], [dav], [dbq], [dbk], [dbv], dcq, dck, dcv], cfg, "qk_prep_bwd")
    anchor = on_grad(_matmul(dproj, sv["h1"], "tn", BF16, "in_proj_dw", tm=768, tn=2048, tk=512))
    dh1 = _matmul(dproj, w.get("w_in_t"), "nn", F32, "in_proj_dx", tm=1024, tn=2048, tk=768)
    dx, g_attn_norm = _rmsnorm_bwd(sv["x"], p["attn_norm"] + anchor, dh1, dxm, "attn_norm_bwd")

    def fold(g):
        return jnp.sum(g.reshape(-1, HEAD_DIM), axis=0)

    small = dict(attn_norm=g_attn_norm[0], a_q_gain=fold(g_aq), a_k_gain=fold(g_ak), a_sinks=g_sinks,
                 c_q_gain=fold(g_cq), c_k_gain=fold(g_ck), mix_out_gain=g_mix_gain[0], ffn_norm=g_ffn_norm[0],
                 conv_w=jnp.concatenate([dwg, dwu], axis=1), conv_b=jnp.concatenate([dbg, dbu], axis=1)[0])
    return dx, small, (dbias_a, dbias_c)


_SMALL = ("attn_norm", "a_q_gain", "a_k_gain", "a_sinks", "c_q_gain", "c_k_gain", "rel_bias_table", "mix_out_gain",
          "ffn_norm", "conv_w", "conv_b")


def _pack(arrays):
    flat = jnp.concatenate([a.reshape(-1).astype(F32) for a in arrays])
    pad = (-flat.shape[0]) % (8 * LANES)
    return jnp.pad(flat, (0, pad)).reshape(-1, LANES)


def _unpack(flat, shapes):
    flat = flat.reshape(-1)
    out, pos = [], 0
    for sh in shapes:
        n = int(np.prod(sh))
        out.append(flat[pos:pos + n].reshape(sh))
        pos += n
    return out


def kernel(x, attn_norm, w_in, a_q_gain, a_k_gain, a_sinks, c_q_gain, c_k_gain, rel_bias_table, mix_out_gain, w_out, ffn_norm, w_up, conv_w, conv_b, w_down, loss_target, m_attn_norm, m_w_in, m_a_q_gain, m_a_k_gain, m_a_sinks, m_c_q_gain, m_c_k_gain, m_rel_bias_table, m_mix_out_gain, m_w_out, m_ffn_norm, m_w_up, m_conv_w, m_conv_b, m_w_down, v_attn_norm, v_w_in, v_a_q_gain, v_a_k_gain, v_a_sinks, v_c_q_gain, v_c_k_gain, v_rel_bias_table, v_mix_out_gain, v_w_out, v_ffn_norm, v_w_up, v_conv_w, v_conv_b, v_w_down):
    depth, d = attn_norm.shape
    f = w_down.shape[1] * N_CHIPS
    cfg = _Cfg(d, f)
    chip = 2 * lax.axis_index("x") + lax.axis_index("y")

    cw_cols = conv_w.shape[2]
    cw_flat = conv_w.reshape(-1)
    cw_rows = -(-cw_flat.shape[0] // (16 * LANES)) * 16
    cw_pad = jnp.pad(cw_flat, (0, cw_rows * LANES - cw_flat.shape[0])).reshape(cw_rows, LANES)

    table_a, table_c = rel_bias_table[:, :cfg.nha], rel_bias_table[:, cfg.nha:]
    bias_a = _block_bias(table_a, 1)
    bias_c = [_block_bias(table_c, dil) for _, dil in DILATED_PAIRS]

    layers, anchor = [], 0.0
    for l in range(depth):
        shards = [w_in[l].T.astype(BF16), w_out[l].astype(BF16), w_up[l].T.astype(BF16), w_down[l].astype(BF16)]
        names = ["w_in_t", "w_out", "w_up_t", "w_down"]
        if l == 0:
            todo = [([cw_pad, shards[0]], ["conv_w", names[0]])] + [([s], [n]) for s, n in zip(shards[1:], names[1:])]
        else:
            todo = [(shards, names)]
        groups = []
        for k, (srcs, group_names) in enumerate(todo):
            lands = [_own_slot((N_CHIPS,) + s.shape, s.dtype, s, chip) for s in srcs]
            handle, token = _split_start(srcs, lands, _gather_plan, 3 * len(srcs), f"gather_start_{l}_{k}")
            anchor = anchor + token[0, 0]
            groups.append((handle, group_names, f"gather_wait_{l}_{k}"))
        layers.append(dict(
            attn_norm=attn_norm[l].reshape(1, d), ffn_norm=ffn_norm[l].reshape(1, d),
            mix_gain=mix_out_gain[l].reshape(1, d),
            gains=(_tile_gain(a_q_gain[l], cfg.nha), _tile_gain(a_k_gain[l], cfg.nkva),
                   _tile_gain(c_q_gain[l], cfg.nhc), _tile_gain(c_k_gain[l], cfg.nhc)),
            sinks=a_sinks[l], bias_a=bias_a, bias_c=bias_c, conv_b=conv_b[l].reshape(1, 2 * f),
            weights=_Gathered(groups)))
    cw_all = layers[0]["weights"].get("conv_w", layers[0]["attn_norm"] + anchor)
    cw_all = cw_all.reshape(N_CHIPS, -1)[:, :cw_flat.shape[0]].reshape(N_CHIPS, depth, CONV_WIDTH, cw_cols)
    conv_w_full = jnp.transpose(cw_all, (1, 2, 0, 3)).reshape(depth, CONV_WIDTH, N_CHIPS * cw_cols)
    for l in range(depth):
        layers[l]["conv_w"] = conv_w_full[l]

    act = x[0]
    saved = []
    for l in range(depth):
        act, sv = _layer_fwd(act, layers[l], cfg)
        saved.append(sv)
    dact, loss_blk = _loss_head(act, loss_target[0], "loss_head")
    loss = lax.psum(loss_blk[0, 0], ("x", "y", "c"))

    core = lax.axis_index("c")

    def start_scatter(grads, name):
        srcs = [g.reshape(N_CHIPS, -1, g.shape[-1]) for g in grads]
        lands = []
        for g in srcs:
            half = g.shape[1] // 2
            own = lax.dynamic_slice(g, (chip, core * half, 0), (1, half, g.shape[2]))[0]
            lands.append(_own_slot((N_DEVICES, half, g.shape[2]), g.dtype, own, 2 * chip + core))
        return _split_start(srcs, lands, _scatter_plan, (N_DEVICES - 1) * len(srcs), name)

    def finish_scatter(l, handles, after):
        parts = [pt for k, h in enumerate(handles) for pt in _split_wait(h, after, f"scatter_wait_{l}_{k}")][::-1]
        halves = [_sum_parts(pt, f"sum_grads_{t}") for t, pt in enumerate(parts)]
        lands = [_own_slot((2,) + h.shape, h.dtype, h, core) for h in halves]
        return _split_start(halves, lands, _swap_plan, len(halves), f"swap_start_{l}")[0]

    dbias = (jnp.zeros_like(bias_a), [jnp.zeros_like(b) for b in bias_c])
    small_grads = [None] * depth
    swaps = [None] * depth
    pending = None
    for l in reversed(range(depth)):
        made = []

        def on_grad(g, l=l, made=made):
            if l:
                made.append(g)
                return 0.0
            handle, token = start_scatter([g], f"scatter_start_0_{len(made)}")
            made.append(handle)
            return token[0, 0]

        dact, small_grads[l], dbias = _layer_bwd(dact, saved[l], layers[l], dbias, cfg, on_grad)
        if pending is not None:
            swaps[l + 1] = finish_scatter(l + 1, pending, dact)
        if l:
            handle, token = start_scatter(made, f"scatter_start_{l}")
            pending = [handle]
            layers[l - 1]["conv_b"] = layers[l - 1]["conv_b"] + token[0, 0]
        else:
            pending = made
    grad_x = dact[None]

    tabs = _bias_table_grad([dbias[0]] + dbias[1], [jnp.asarray(_bucket_ids(1))]
                            + [jnp.asarray(_bucket_ids(dil)) for _, dil in DILATED_PAIRS], "bias_table_grad")
    g_table_a = tabs[0][:, :N_BUCKETS].T
    g_table_c = (tabs[1] + tabs[2] + tabs[3])[:, :N_BUCKETS].T
    g_table = jnp.concatenate([g_table_a, g_table_c], axis=1)
    small_local = {k: jnp.stack([small_grads[l][k] for l in range(depth)]) for k in _SMALL if k != "rel_bias_table"}
    small_local["rel_bias_table"] = g_table
    shapes = [small_local[k].shape for k in _SMALL]
    reduced = dict(zip(_SMALL, _unpack(_allreduce_small(_pack([small_local[k] for k in _SMALL]), "allreduce_small"),
                                       shapes)))
    reduced["conv_w"] = lax.dynamic_slice_in_dim(reduced["conv_w"], chip * cw_cols, cw_cols, axis=2)

    given = dict(attn_norm=attn_norm, a_q_gain=a_q_gain, a_k_gain=a_k_gain, a_sinks=a_sinks, c_q_gain=c_q_gain,
                 c_k_gain=c_k_gain, rel_bias_table=rel_bias_table, mix_out_gain=mix_out_gain, ffn_norm=ffn_norm,
                 conv_w=conv_w, conv_b=conv_b)
    moms = dict(attn_norm=(m_attn_norm, v_attn_norm), a_q_gain=(m_a_q_gain, v_a_q_gain),
                a_k_gain=(m_a_k_gain, v_a_k_gain), a_sinks=(m_a_sinks, v_a_sinks), c_q_gain=(m_c_q_gain, v_c_q_gain),
                c_k_gain=(m_c_k_gain, v_c_k_gain), rel_bias_table=(m_rel_bias_table, v_rel_bias_table),
                mix_out_gain=(m_mix_out_gain, v_mix_out_gain), ffn_norm=(m_ffn_norm, v_ffn_norm),
                conv_w=(m_conv_w, v_conv_w), conv_b=(m_conv_b, v_conv_b))
    sshapes = [given[k].shape for k in _SMALL]
    s_delta, s_m, s_v = _adamw(_pack([given[k] for k in _SMALL]), _pack([reduced[k] for k in _SMALL]),
                               _pack([moms[k][0] for k in _SMALL]), _pack([moms[k][1] for k in _SMALL]), "adamw_small")
    grads = dict(reduced)
    deltas = dict(zip(_SMALL, _unpack(s_delta, sshapes)))
    new_m = dict(zip(_SMALL, _unpack(s_m, sshapes)))
    new_v = dict(zip(_SMALL, _unpack(s_v, sshapes)))

    big_given = dict(w_in=(w_in, m_w_in, v_w_in, True), w_out=(w_out, m_w_out, v_w_out, False),
                     w_up=(w_up, m_w_up, v_w_up, True), w_down=(w_down, m_w_down, v_w_down, False))
    names = ("w_in", "w_out", "w_up", "w_down")
    bufs = {name: None for name in names}
    after = s_delta
    for l in reversed(range(depth)):
        if l == 0:
            swaps[0] = finish_scatter(0, pending, after)
        layer_grads = [g.reshape(-1, g.shape[-1]) for g in _split_wait(swaps[l], after, f"swap_wait_{l}")]
        for t, name in enumerate(names):
            wt, mt, vt, transposed = big_given[name]
            g = layer_grads[t].T if transposed else layer_grads[t]
            bufs[name] = _adamw_layer(l, wt, g, mt, vt, bufs[name], f"adamw_{name}_{l}")
            after = bufs[name][1]
    for name in names:
        grads[name], deltas[name], new_m[name], new_v[name] = bufs[name]

    order = ("attn_norm", "w_in", "a_q_gain", "a_k_gain", "a_sinks", "c_q_gain", "c_k_gain", "rel_bias_table",
             "mix_out_gain", "w_out", "ffn_norm", "w_up", "conv_w", "conv_b", "w_down")
    return (loss, grad_x, *[grads[k] for k in order], *[deltas[k] for k in order], *[new_m[k] for k in order],
            *[new_v[k] for k in order])
```

```python
import numpy as np
import jax
import jax.numpy as jnp
from jax import lax
from jax.experimental import pallas as pl
from jax.experimental.pallas import tpu as pltpu

F32 = jnp.float32
BF16 = jnp.bfloat16
MESH = pl.DeviceIdType.MESH

HEAD_DIM = 64
BLOCK = 128
LANES = 128
EPS = 1e-6
NEG_INF = -1e30
WINDOW_A = 128
DILATED_PAIRS = ((128, 1), (512, 4), (2048, 16))
N_BUCKETS = 32
T5_MAX_DIST = 2048
CONV_WIDTH = 3
ADAM_LR = 0.001
ADAM_B1 = 0.9
ADAM_B2 = 0.999
ADAM_EPS = 1e-08
ADAM_WD = 0.01
ADAM_STEP = 10
N_CHIPS = 4
N_DEVICES = 8
VMEM_LIMIT_BYTES = 48 * 1024 * 1024
QK_SCALE = HEAD_DIM ** -0.5


def _params(sem=None):
    return pltpu.CompilerParams(dimension_semantics=sem, vmem_limit_bytes=VMEM_LIMIT_BYTES)


def _div_tile(n, cap, mult):
    best = None
    for t in range(mult, min(n, cap) + 1, mult):
        if n % t == 0:
            best = t
    return n if best is None else best


def _dot(a, b):
    return lax.dot_general(a, b, (((1,), (0,)), ((), ())), preferred_element_type=F32)


def _dot_nt(a, b):
    return lax.dot_general(a, b, (((1,), (1,)), ((), ())), preferred_element_type=F32)


def _dot_tn(a, b):
    return lax.dot_general(a, b, (((0,), (0,)), ((), ())), preferred_element_type=F32)


def _split_dot(x, m):
    hi = x.astype(BF16)
    lo = (x - hi.astype(F32)).astype(BF16)
    return _dot(hi, m) + _dot(lo, m)


class _Cfg:
    def __init__(self, d_model, d_ff):
        nh = d_model // HEAD_DIM
        self.d = d_model
        self.f = d_ff
        self.nha = nh // 4
        self.nkva = self.nha // 4
        self.nhb = nh // 4
        self.nhc = nh // 2
        self.a_q = self.nha * HEAD_DIM
        self.a_kv = self.nkva * HEAD_DIM
        self.b_w = self.nhb * HEAD_DIM
        self.c_w = self.nhc * HEAD_DIM
        sizes = [self.a_q, self.a_kv, self.a_kv, self.b_w, self.b_w, self.b_w, self.c_w, self.c_w, self.c_w]
        starts = [0] + [int(s) for s in np.cumsum(sizes)[:-1]]
        self.sections = list(zip(starts, sizes))
        self.in_width = int(sum(sizes))
        assert all(s % LANES == 0 for s in sizes)


def _matmul(a, b, mode, out_dtype, name, tm=512, tn=512, tk=512, residual=None):
    if mode == "tn":
        kdim, m = a.shape
    else:
        m, kdim = a.shape
    n = b.shape[0] if mode == "nt" else b.shape[1]
    tm, tn, tk = _div_tile(m, tm, LANES), _div_tile(n, tn, LANES), _div_tile(kdim, tk, LANES)
    nk = kdim // tk
    if mode == "tn":
        a_spec = pl.BlockSpec((tk, tm), lambda i, j, k: (k, i))
    else:
        a_spec = pl.BlockSpec((tm, tk), lambda i, j, k: (i, k))
    if mode == "nt":
        b_spec = pl.BlockSpec((tn, tk), lambda i, j, k: (j, k))
    else:
        b_spec = pl.BlockSpec((tk, tn), lambda i, j, k: (k, j))
    dot = {"nn": _dot, "nt": _dot_nt, "tn": _dot_tn}[mode]
    o_spec = pl.BlockSpec((tm, tn), lambda i, j, k: (i, j))
    in_specs = [a_spec, b_spec]
    args = [a, b]
    if residual is not None:
        in_specs.append(o_spec)
        args.append(residual)

    def body(*refs):
        if residual is None:
            a_ref, b_ref, o_ref, acc = refs
        else:
            a_ref, b_ref, r_ref, o_ref, acc = refs
        k = pl.program_id(2)

        @pl.when(k == 0)
        def _():
            acc[...] = jnp.zeros_like(acc)

        acc[...] += dot(a_ref[...].astype(BF16), b_ref[...].astype(BF16))

        @pl.when(k == nk - 1)
        def _():
            r = acc[...]
            if residual is not None:
                r = r + r_ref[...]
            o_ref[...] = r.astype(out_dtype)

    return pl.pallas_call(
        body, name=name, grid=(m // tm, n // tn, nk), in_specs=in_specs, out_specs=o_spec,
        out_shape=jax.ShapeDtypeStruct((m, n), out_dtype), scratch_shapes=[pltpu.VMEM((tm, tn), F32)],
        compiler_params=_params(("parallel", "parallel", "arbitrary")),
    )(*args)


def _rmsnorm_fwd(x, g, name):
    s, d = x.shape
    ts = _div_tile(s, 256, 8)

    def body(x_ref, g_ref, o_ref):
        xv = x_ref[...]
        r = lax.rsqrt(jnp.mean(xv * xv, axis=-1, keepdims=True) + EPS)
        o_ref[...] = (xv * r * g_ref[...]).astype(BF16)

    return pl.pallas_call(
        body, name=name, grid=(s // ts,),
        in_specs=[pl.BlockSpec((ts, d), lambda i: (i, 0)), pl.BlockSpec((1, d), lambda i: (0, 0))],
        out_specs=pl.BlockSpec((ts, d), lambda i: (i, 0)), out_shape=jax.ShapeDtypeStruct((s, d), BF16),
        compiler_params=_params(("parallel",)),
    )(x, g)


def _rmsnorm_bwd(x, g, dh, dres, name):
    s, d = x.shape
    ts = _div_tile(s, 256, 8)

    def body(x_ref, g_ref, dh_ref, dres_ref, dx_ref, dg_ref):
        @pl.when(pl.program_id(0) == 0)
        def _():
            dg_ref[...] = jnp.zeros_like(dg_ref)

        xv = x_ref[...]
        r = lax.rsqrt(jnp.mean(xv * xv, axis=-1, keepdims=True) + EPS)
        xhat = xv * r
        dhv = dh_ref[...]
        dxhat = dhv * g_ref[...]
        dx_ref[...] = dres_ref[...] + r * (dxhat - xhat * jnp.mean(dxhat * xhat, axis=-1, keepdims=True))
        dg_ref[...] += jnp.sum(dhv * xhat, axis=0, keepdims=True)

    row = pl.BlockSpec((ts, d), lambda i: (i, 0))
    vec = pl.BlockSpec((1, d), lambda i: (0, 0))
    return pl.pallas_call(
        body, name=name, grid=(s // ts,), in_specs=[row, vec, row, row], out_specs=[row, vec],
        out_shape=[jax.ShapeDtypeStruct((s, d), F32), jax.ShapeDtypeStruct((1, d), F32)],
        compiler_params=_params(("arbitrary",)),
    )(x, g, dh, dres)


def _head_mean_matrix():
    idx = np.arange(LANES) // HEAD_DIM
    return jnp.asarray((idx[:, None] == idx[None, :]).astype(np.float32) / HEAD_DIM, dtype=BF16)


def _head_mean(y, m128):
    w = y.shape[1]
    parts = [_split_dot(y[:, c * LANES:(c + 1) * LANES], m128) for c in range(w // LANES)]
    return parts[0] if len(parts) == 1 else jnp.concatenate(parts, axis=1)


_NORMED_SECTIONS = (0, 1, 6, 7)
_QUERY_SECTIONS = (0, 3, 6)


def _qk_prep(proj, gains, cfg, name):
    s = proj.shape[0]
    ts = _div_tile(s, 256, 16)
    m128 = _head_mean_matrix()

    def body(p_ref, m_ref, g0, g1, g6, g7, *outs):
        gref = dict(zip(_NORMED_SECTIONS, (g0, g1, g6, g7)))
        for idx, (st, w) in enumerate(cfg.sections):
            xv = p_ref[:, st:st + w]
            if idx in gref:
                r = lax.rsqrt(_head_mean(xv * xv, m_ref[...]) + EPS)
                xv = xv * r * gref[idx][...]
            if idx in _QUERY_SECTIONS:
                xv = xv * QK_SCALE
            outs[idx][...] = xv.astype(BF16)

    in_specs = [pl.BlockSpec((ts, cfg.in_width), lambda i: (i, 0)), pl.BlockSpec((LANES, LANES), lambda i: (0, 0))]
    in_specs += [pl.BlockSpec((1, cfg.sections[k][1]), lambda i: (0, 0)) for k in _NORMED_SECTIONS]
    out_specs = [pl.BlockSpec((ts, w), lambda i: (i, 0)) for _, w in cfg.sections]
    out_shape = [jax.ShapeDtypeStruct((s, w), BF16) for _, w in cfg.sections]
    return pl.pallas_call(
        body, name=name, grid=(s // ts,), in_specs=in_specs, out_specs=out_specs, out_shape=out_shape,
        compiler_params=_params(("parallel",)),
    )(proj, m128, *gains)


def _qk_prep_bwd(proj, gains, grads, cfg, name):
    s = proj.shape[0]
    ts = _div_tile(s, 128, 16)
    m128 = _head_mean_matrix()
    counts = [len(gl) for gl in grads]
    flat = [g for gl in grads for g in gl]

    def body(*refs):
        p_ref, m_ref = refs[0], refs[1]
        gref = dict(zip(_NORMED_SECTIONS, refs[2:6]))
        g_in = refs[6:6 + len(flat)]
        dp_ref = refs[6 + len(flat)]
        dgain = dict(zip(_NORMED_SECTIONS, refs[7 + len(flat):]))

        @pl.when(pl.program_id(0) == 0)
        def _():
            for k in _NORMED_SECTIONS:
                dgain[k][...] = jnp.zeros_like(dgain[k])

        pos = 0
        for idx, (st, w) in enumerate(cfg.sections):
            dy = g_in[pos][...]
            for extra in g_in[pos + 1:pos + counts[idx]]:
                dy = dy + extra[...]
            pos += counts[idx]
            if idx in gref:
                xv = p_ref[:, st:st + w]
                r = lax.rsqrt(_head_mean(xv * xv, m_ref[...]) + EPS)
                xhat = xv * r
                dxhat = dy * gref[idx][...]
                dgain[idx][...] += jnp.sum(dy * xhat, axis=0, keepdims=True)
                dy = r * (dxhat - xhat * _head_mean(dxhat * xhat, m_ref[...]))
            dp_ref[:, st:st + w] = dy.astype(BF16)

    in_specs = [pl.BlockSpec((ts, cfg.in_width), lambda i: (i, 0)), pl.BlockSpec((LANES, LANES), lambda i: (0, 0))]
    in_specs += [pl.BlockSpec((1, cfg.sections[k][1]), lambda i: (0, 0)) for k in _NORMED_SECTIONS]
    for idx, (_, w) in enumerate(cfg.sections):
        in_specs += [pl.BlockSpec((ts, w), lambda i: (i, 0))] * counts[idx]
    out_specs = [pl.BlockSpec((ts, cfg.in_width), lambda i: (i, 0))]
    out_specs += [pl.BlockSpec((1, cfg.sections[k][1]), lambda i: (0, 0)) for k in _NORMED_SECTIONS]
    out_shape = [jax.ShapeDtypeStruct((s, cfg.in_width), BF16)]
    out_shape += [jax.ShapeDtypeStruct((1, cfg.sections[k][1]), F32) for k in _NORMED_SECTIONS]
    return pl.pallas_call(
        body, name=name, grid=(s // ts,), in_specs=in_specs, out_specs=out_specs, out_shape=out_shape,
        compiler_params=_params(("arbitrary",)),
    )(proj, m128, *gains, *flat)


def _band_masks(max_dist):
    row = lax.broadcasted_iota(jnp.int32, (BLOCK, BLOCK), 0)
    col = lax.broadcasted_iota(jnp.int32, (BLOCK, BLOCK), 1)
    return row + BLOCK - col <= max_dist, col <= row


def _dilated_t(a, dil):
    s, w = a.shape
    return _keys_on_lanes(a.reshape(s // dil, dil * w), BLOCK)


def _undilated(at, dil):
    nblk, dw, _ = at.shape
    return jnp.transpose(at, (0, 2, 1)).reshape(nblk * BLOCK * dil, dw // dil)


def _banded_fwd(q, kt, v, bias, sinks, hq, hk, max_dist, dil, name):
    s = q.shape[0]
    wq, wk, sd, grp = hq * HEAD_DIM, hk * HEAD_DIM, s // dil, hq // hk
    nb = sd // BLOCK
    has_sink = sinks is not None

    def body(*refs):
        if has_sink:
            q_ref, ktp_ref, ktc_ref, vp_ref, vc_ref, b_ref, s_ref, o_ref, l_ref = refs
        else:
            q_ref, ktp_ref, ktc_ref, vp_ref, vc_ref, b_ref, o_ref, l_ref = refs
        i = pl.program_id(1)
        mprev, mcur = _band_masks(max_dist)
        mask = jnp.concatenate([jnp.logical_and(mprev, i > 0), mcur], axis=1)
        for h in range(hq):
            sq = slice(h * HEAD_DIM, (h + 1) * HEAD_DIM)
            sk = slice((h // grp) * HEAD_DIM, (h // grp + 1) * HEAD_DIM)
            kt = jnp.concatenate([ktp_ref[sk, :], ktc_ref[sk, :]], axis=1)
            vv = jnp.concatenate([vp_ref[:, sk], vc_ref[:, sk]], axis=0)
            sc = jnp.where(mask, _dot(q_ref[:, sq], kt) + b_ref[h], NEG_INF)
            m = jnp.max(sc, axis=-1, keepdims=True)
            if has_sink:
                m = jnp.maximum(m, s_ref[h])
            p = jnp.exp(sc - m)
            den = jnp.sum(p, axis=-1, keepdims=True)
            if has_sink:
                den = den + jnp.exp(s_ref[h] - m)
            o_ref[:, sq] = _dot(p.astype(BF16), vv) / den
            l_ref[:, sq] = jnp.broadcast_to(m + jnp.log(den), (BLOCK, HEAD_DIM))

    qspec = pl.BlockSpec((BLOCK, wq), lambda r, i: (i, r))
    kprev = pl.BlockSpec((BLOCK, wk), lambda r, i: (jnp.maximum(i - 1, 0), r))
    kcur = pl.BlockSpec((BLOCK, wk), lambda r, i: (i, r))
    ktprev = pl.BlockSpec((None, wk, BLOCK), lambda r, i: (jnp.maximum(i - 1, 0), r, 0))
    ktcur = pl.BlockSpec((None, wk, BLOCK), lambda r, i: (i, r, 0))
    in_specs = [qspec, ktprev, ktcur, kprev, kcur, pl.BlockSpec((hq, BLOCK, 2 * BLOCK), lambda r, i: (0, 0, 0))]
    v2 = v.reshape(sd, dil * wk)
    args = [q.reshape(sd, dil * wq), kt, kt, v2, v2, bias]
    if has_sink:
        in_specs.append(pl.BlockSpec(memory_space=pltpu.SMEM))
        args.append(sinks)
    out, lse = pl.pallas_call(
        body, name=name, grid=(dil, nb), in_specs=in_specs, out_specs=[qspec, qspec],
        out_shape=[jax.ShapeDtypeStruct((sd, dil * wq), F32)] * 2,
        compiler_params=_params(("parallel", "parallel")),
    )(*args)
    return out.reshape(s, wq), lse.reshape(s, wq)


def _banded_bwd(q, k, kt, v, o, lse, do, bias, sinks, dbias_init, hq, hk, max_dist, dil, name):
    s = q.shape[0]
    wq, wk, sd, grp = hq * HEAD_DIM, hk * HEAD_DIM, s // dil, hq // hk
    nb = sd // BLOCK
    has_sink = sinks is not None

    def body(*refs):
        (q_ref, qn_ref, qt_ref, qtn_ref, kp_ref, kc_ref, ktp_ref, ktc_ref, vtp_ref, vtc_ref, o_ref, on_ref, l_ref,
         ln_ref, do_ref, don_ref, dot_ref, dotn_ref, b_ref, dbi_ref) = refs[:20]
        rest = refs[20:]
        if has_sink:
            s_ref, dq_ref, dkt_ref, dvt_ref, db_ref, ds_ref = rest
        else:
            dq_ref, dkt_ref, dvt_ref, db_ref = rest
        j = pl.program_id(1)

        @pl.when(jnp.logical_and(pl.program_id(0) == 0, j == 0))
        def _():
            db_ref[...] = dbi_ref[...]
            if has_sink:
                ds_ref[...] = jnp.zeros_like(ds_ref)

        mprev_static, mcur = _band_masks(max_dist)
        mask = jnp.concatenate([jnp.logical_and(mprev_static, j > 0), mcur], axis=1)
        mnext = jnp.logical_and(mprev_static, j + 1 < nb)
        dkt_acc = [jnp.zeros((HEAD_DIM, BLOCK), F32) for _ in range(hk)]
        dvt_acc = [jnp.zeros((HEAD_DIM, BLOCK), F32) for _ in range(hk)]
        for h in range(hq):
            g = h // grp
            sq = slice(h * HEAD_DIM, (h + 1) * HEAD_DIM)
            sk = slice(g * HEAD_DIM, (g + 1) * HEAD_DIM)
            kt2 = jnp.concatenate([ktp_ref[sk, :], ktc_ref[sk, :]], axis=1)
            vt2 = jnp.concatenate([vtp_ref[sk, :], vtc_ref[sk, :]], axis=1)
            k2 = jnp.concatenate([kp_ref[:, sk], kc_ref[:, sk]], axis=0)
            doh = do_ref[:, sq]
            dohb = doh.astype(BF16)
            lcol = l_ref[:, h * HEAD_DIM:h * HEAD_DIM + 1]
            dcol = jnp.sum(doh * o_ref[:, sq], axis=-1, keepdims=True)
            sc = _dot(q_ref[:, sq], kt2) + b_ref[h]
            p = jnp.where(mask, jnp.exp(sc - lcol), 0.0)
            ds = p * (_dot(dohb, vt2) - dcol)
            dsb = ds.astype(BF16)
            dq_ref[:, sq] = _dot(dsb, k2) * QK_SCALE
            db_ref[h] += ds
            if has_sink:
                psink = jnp.exp(s_ref[h] - lcol)
                tot = jnp.sum(psink * dcol, axis=0, keepdims=True)
                ds_ref[h:h + 1, :] -= jnp.broadcast_to(tot, (1, LANES))
            don = don_ref[:, sq]
            lncol = ln_ref[:, h * HEAD_DIM:h * HEAD_DIM + 1]
            dncol = jnp.sum(don * on_ref[:, sq], axis=-1, keepdims=True)
            sn = _dot(qn_ref[:, sq], ktc_ref[sk, :]) + b_ref[h, :, 0:BLOCK]
            pn = jnp.where(mnext, jnp.exp(sn - lncol), 0.0)
            dsn = pn * (_dot(don.astype(BF16), vtc_ref[sk, :]) - dncol)
            dkt_acc[g] = dkt_acc[g] + (_dot(qt_ref[sq, :], dsb[:, BLOCK:]) + _dot(qtn_ref[sq, :], dsn.astype(BF16)))
            dvt_acc[g] = dvt_acc[g] + (_dot(dot_ref[sq, :], p[:, BLOCK:].astype(BF16))
                                       + _dot(dotn_ref[sq, :], pn.astype(BF16)))
        for g in range(hk):
            sk = slice(g * HEAD_DIM, (g + 1) * HEAD_DIM)
            dkt_ref[sk, :] = dkt_acc[g]
            dvt_ref[sk, :] = dvt_acc[g]

    qcur = pl.BlockSpec((BLOCK, wq), lambda r, j: (j, r))
    qnext = pl.BlockSpec((BLOCK, wq), lambda r, j: (jnp.minimum(j + 1, nb - 1), r))
    qtcur = pl.BlockSpec((None, wq, BLOCK), lambda r, j: (j, r, 0))
    qtnext = pl.BlockSpec((None, wq, BLOCK), lambda r, j: (jnp.minimum(j + 1, nb - 1), r, 0))
    kprev = pl.BlockSpec((BLOCK, wk), lambda r, j: (jnp.maximum(j - 1, 0), r))
    kcur = pl.BlockSpec((BLOCK, wk), lambda r, j: (j, r))
    ktprev = pl.BlockSpec((None, wk, BLOCK), lambda r, j: (jnp.maximum(j - 1, 0), r, 0))
    ktcur = pl.BlockSpec((None, wk, BLOCK), lambda r, j: (j, r, 0))
    bspec = pl.BlockSpec((hq, BLOCK, 2 * BLOCK), lambda r, j: (0, 0, 0))
    q2, k2 = q.reshape(sd, dil * wq), k.reshape(sd, dil * wk)
    o2, l2, do2 = o.reshape(sd, dil * wq), lse.reshape(sd, dil * wq), do.reshape(sd, dil * wq)
    qt, vt, dot = _dilated_t(q, dil), _dilated_t(v, dil), _dilated_t(do.astype(BF16), dil)
    in_specs = [qcur, qnext, qtcur, qtnext, kprev, kcur, ktprev, ktcur, ktprev, ktcur, qcur, qnext, qcur, qnext,
                qcur, qnext, qtcur, qtnext, bspec, bspec]
    args = [q2, q2, qt, qt, k2, k2, kt, kt, vt, vt, o2, o2, l2, l2, do2, do2, dot, dot, bias, dbias_init]
    out_specs = [qcur, ktcur, ktcur, bspec]
    out_shape = [jax.ShapeDtypeStruct((sd, dil * wq), F32), jax.ShapeDtypeStruct((nb, dil * wk, BLOCK), F32),
                 jax.ShapeDtypeStruct((nb, dil * wk, BLOCK), F32), jax.ShapeDtypeStruct((hq, BLOCK, 2 * BLOCK), F32)]
    if has_sink:
        in_specs.append(pl.BlockSpec(memory_space=pltpu.SMEM))
        args.append(sinks)
        out_specs.append(pl.BlockSpec((hq, LANES), lambda r, j: (0, 0)))
        out_shape.append(jax.ShapeDtypeStruct((hq, LANES), F32))
    res = pl.pallas_call(
        body, name=name, grid=(dil, nb), in_specs=in_specs, out_specs=out_specs, out_shape=out_shape,
        compiler_params=_params(("arbitrary", "arbitrary")),
    )(*args)
    dq, dk, dv, dbias = res[0].reshape(s, wq), _undilated(res[1], dil), _undilated(res[2], dil), res[3]
    return dq, dk, dv, dbias, (res[4][:, 0] if has_sink else None)


def _neg_softplus(z):
    return -(jnp.maximum(z, 0.0) + jnp.log(1.0 + jnp.exp(-jnp.abs(z))))


SB_CHUNK = 256
HEADS_PER_PAIR = LANES // HEAD_DIM


def _tri(kind):
    row = lax.broadcasted_iota(jnp.int32, (SB_CHUNK, SB_CHUNK), 0)
    col = lax.broadcasted_iota(jnp.int32, (SB_CHUNK, SB_CHUNK), 1)
    return {"ge": row >= col, "lt": row < col, "le": row <= col}[kind].astype(BF16)


def _keys_on_lanes(a, rows):
    s, w = a.shape
    return jnp.transpose(a.reshape(s // rows, rows, w), (0, 2, 1))


def _sb_mask(i, jj):
    row = lax.broadcasted_iota(jnp.int32, (BLOCK, SB_CHUNK), 0)
    col = lax.broadcasted_iota(jnp.int32, (BLOCK, SB_CHUNK), 1)
    return col < row + (i * BLOCK - jj * SB_CHUNK)


def _sb_trips(i):
    return (i * BLOCK) // (2 * SB_CHUNK) + 1


def _sb_rows(jj, n):
    return pl.ds(pl.multiple_of(jj * SB_CHUNK, SB_CHUNK), n * SB_CHUNK)


def _sb_fwd(q, kt, v, name):
    s, w = q.shape
    npair, nb, nc = w // LANES, s // BLOCK, s // SB_CHUNK

    def body(q_ref, kt_ref, v_ref, o_ref, t_ref):
        i = pl.program_id(1)
        lincl = _tri("ge")
        heads = [slice(hh * HEAD_DIM, (hh + 1) * HEAD_DIM) for hh in range(HEADS_PER_PAIR)]
        qs = [q_ref[:, sl] for sl in heads]

        def trip(t, carry, masked):
            lo, hi = 2 * t, 2 * t + 1
            mlo, mhi = (_sb_mask(i, lo), _sb_mask(i, hi)) if masked else (None, None)

            def keep(m, val):
                return val if m is None else jnp.where(m, val, 0.0)

            new = []
            for hh, sl in enumerate(heads):
                o_acc, rem = carry[hh]
                zhi = _dot(qs[hh], kt_ref[hi, sl, :])
                zlo = _dot(qs[hh], kt_ref[lo, sl, :])
                lrhi = keep(mhi, _neg_softplus(zhi))
                lrlo = keep(mlo, _neg_softplus(zlo))
                tothi = jnp.sum(lrhi, axis=-1, keepdims=True)
                ahi = keep(mhi, jnp.exp(zhi + (rem + _split_dot(lrhi, lincl))))
                alo = keep(mlo, jnp.exp(zlo + (rem + tothi + _split_dot(lrlo, lincl))))
                a = jnp.concatenate([alo, ahi], axis=1).astype(BF16)
                new.append((o_acc + _dot(a, v_ref[_sb_rows(lo, 2), sl]),
                            rem + tothi + jnp.sum(lrlo, axis=-1, keepdims=True)))
            return tuple(new)

        init = tuple((jnp.zeros((BLOCK, HEAD_DIM), F32), jnp.zeros((BLOCK, 1), F32)) for _ in heads)
        trips = _sb_trips(i)
        carry = trip(trips - 1, init, True)
        carry = lax.fori_loop(0, trips - 1, lambda t, cr: trip(trips - 2 - t, cr, False), carry)
        for hh, sl in enumerate(heads):
            o_ref[:, sl] = carry[hh][0]
            t_ref[:, sl] = jnp.broadcast_to(carry[hh][1], (BLOCK, HEAD_DIM))

    qspec = pl.BlockSpec((BLOCK, LANES), lambda p, i: (i, p))
    return pl.pallas_call(
        body, name=name, grid=(npair, nb),
        in_specs=[qspec, pl.BlockSpec((nc, LANES, SB_CHUNK), lambda p, i: (0, p, 0)),
                  pl.BlockSpec((s, LANES), lambda p, i: (0, p))],
        out_specs=[qspec, qspec], out_shape=[jax.ShapeDtypeStruct((s, w), F32)] * 2,
        compiler_params=_params(("parallel", "parallel")),
    )(q, kt, v)


def _sb_bwd(q, k, kt, v, tot, do, name):
    s, w = q.shape
    npair, nb, nc = w // LANES, s // BLOCK, s // SB_CHUNK
    dob = do.astype(BF16)

    def body(q_ref, qt_ref, k_ref, kt_ref, vt_ref, t_ref, do_ref, dot_ref, dq_ref, dkt_ref, dvt_ref):
        i = pl.program_id(1)

        @pl.when(i == 0)
        def _():
            dkt_ref[...] = jnp.zeros_like(dkt_ref)
            dvt_ref[...] = jnp.zeros_like(dvt_ref)

        lbefore = _tri("lt")
        lupto = _tri("le")
        heads = [slice(hh * HEAD_DIM, (hh + 1) * HEAD_DIM) for hh in range(HEADS_PER_PAIR)]
        qs = [q_ref[:, sl] for sl in heads]
        qts = [qt_ref[sl, :] for sl in heads]
        dos = [do_ref[:, sl] for sl in heads]
        dots = [dot_ref[sl, :] for sl in heads]
        totals = [t_ref[:, sl.start:sl.start + 1] for sl in heads]

        def trip(t, carry, masked):
            chunks = (2 * t, 2 * t + 1)
            masks = [_sb_mask(i, jj) if masked else None for jj in chunks]

            def keep(m, val):
                return val if m is None else jnp.where(m, val, 0.0)

            new = []
            for hh, sl in enumerate(heads):
                dq_acc, plr, pg = carry[hh]
                zs = [_dot(qs[hh], kt_ref[jj, sl, :]) for jj in chunks]
                lrs = [keep(m, _neg_softplus(z)) for m, z in zip(masks, zs)]
                lr_sums = [jnp.sum(lr, axis=-1, keepdims=True) for lr in lrs]
                before = [plr, plr + lr_sums[0]]
                avs = [keep(m, jnp.exp(z + (totals[hh] - (b + _split_dot(lr, lbefore)))))
                       for m, z, lr, b in zip(masks, zs, lrs, before)]
                gs = [_dot(dos[hh], vt_ref[jj, sl, :]) * a for jj, a in zip(chunks, avs)]
                g_sums = [jnp.sum(g, axis=-1, keepdims=True) for g in gs]
                upto = [pg, pg + g_sums[0]]
                dzs = [keep(m, g - jnp.exp(z + lr) * (u + _split_dot(g, lupto))).astype(BF16)
                       for m, z, lr, g, u in zip(masks, zs, lrs, gs, upto)]
                for jj, dzb, a in zip(chunks, dzs, avs):
                    dkt_ref[jj, sl, :] += _dot(qts[hh], dzb)
                    dvt_ref[jj, sl, :] += _dot(dots[hh], a.astype(BF16))
                dz2 = jnp.concatenate(dzs, axis=1)
                new.append((dq_acc + _dot(dz2, k_ref[_sb_rows(chunks[0], 2), sl]), plr + lr_sums[0] + lr_sums[1],
                            pg + g_sums[0] + g_sums[1]))
            return tuple(new)

        zero = jnp.zeros((BLOCK, 1), F32)
        init = tuple((jnp.zeros((BLOCK, HEAD_DIM), F32), zero, zero) for _ in heads)
        trips = _sb_trips(i)
        carry = lax.fori_loop(0, trips - 1, lambda t, cr: trip(t, cr, False), init)
        carry = trip(trips - 1, carry, True)
        for hh, sl in enumerate(heads):
            dq_ref[:, sl] = carry[hh][0] * QK_SCALE

    qspec = pl.BlockSpec((BLOCK, LANES), lambda p, i: (i, p))
    qtspec = pl.BlockSpec((None, LANES, BLOCK), lambda p, i: (i, p, 0))
    kspec = pl.BlockSpec((s, LANES), lambda p, i: (0, p))
    ktspec = pl.BlockSpec((nc, LANES, SB_CHUNK), lambda p, i: (0, p, 0))
    dq, dkt, dvt = pl.pallas_call(
        body, name=name, grid=(npair, nb), in_specs=[qspec, qtspec, kspec, ktspec, ktspec, qspec, qspec, qtspec],
        out_specs=[qspec, ktspec, ktspec],
        out_shape=[jax.ShapeDtypeStruct((s, w), F32)] + [jax.ShapeDtypeStruct((nc, w, SB_CHUNK), F32)] * 2,
        compiler_params=_params(("parallel", "arbitrary")),
    )(q, _keys_on_lanes(q, BLOCK), k, kt, _keys_on_lanes(v, SB_CHUNK), tot, dob, _keys_on_lanes(dob, BLOCK))

    def rows_first(t):
        return jnp.transpose(t, (0, 2, 1)).reshape(s, w)

    return dq, rows_first(dkt), rows_first(dvt)


def _group_norm(xv, g):
    r = lax.rsqrt(jnp.mean(xv * xv, axis=-1, keepdims=True) + EPS)
    return xv * r * g


def _mix_fwd(oa, ob, ocs, lses, gain, cfg, name):
    s = oa.shape[0]
    ts = _div_tile(s, 256, 16)
    aq, bw, cw = cfg.a_q, cfg.b_w, cfg.c_w

    def body(oa_ref, ob_ref, c1, c2, c3, l1, l2, l3, g_ref, mix_ref, oc_ref, lse_ref):
        m = jnp.maximum(jnp.maximum(l1[...], l2[...]), l3[...])
        e1, e2, e3 = jnp.exp(l1[...] - m), jnp.exp(l2[...] - m), jnp.exp(l3[...] - m)
        den = e1 + e2 + e3
        oc = (e1 * c1[...] + e2 * c2[...] + e3 * c3[...]) / den
        oc_ref[...] = oc
        lse_ref[...] = m + jnp.log(den)
        mix_ref[:, 0:aq] = _group_norm(oa_ref[...], g_ref[:, 0:aq]).astype(BF16)
        mix_ref[:, aq:aq + bw] = _group_norm(ob_ref[...], g_ref[:, aq:aq + bw]).astype(BF16)
        mix_ref[:, aq + bw:] = _group_norm(oc, g_ref[:, aq + bw:]).astype(BF16)

    def row(wd):
        return pl.BlockSpec((ts, wd), lambda i: (i, 0))

    return pl.pallas_call(
        body, name=name, grid=(s // ts,),
        in_specs=[row(aq), row(bw)] + [row(cw)] * 6 + [pl.BlockSpec((1, cfg.d), lambda i: (0, 0))],
        out_specs=[row(cfg.d), row(cw), row(cw)],
        out_shape=[jax.ShapeDtypeStruct((s, cfg.d), BF16), jax.ShapeDtypeStruct((s, cw), F32),
                   jax.ShapeDtypeStruct((s, cw), F32)],
        compiler_params=_params(("parallel",)),
    )(oa, ob, *ocs, *lses, gain)


def _mix_bwd(dmix, oa, ob, oc, gain, cfg, name):
    s = oa.shape[0]
    ts = _div_tile(s, 256, 8)
    aq, bw, cw = cfg.a_q, cfg.b_w, cfg.c_w

    def body(dm_ref, oa_ref, ob_ref, oc_ref, g_ref, da_ref, db_ref, dc_ref, dg_ref):
        @pl.when(pl.program_id(0) == 0)
        def _():
            dg_ref[...] = jnp.zeros_like(dg_ref)

        for x_ref, dx_ref, lo, hi in ((oa_ref, da_ref, 0, aq), (ob_ref, db_ref, aq, aq + bw),
                                      (oc_ref, dc_ref, aq + bw, aq + bw + cw)):
            xv = x_ref[...]
            dy = dm_ref[:, lo:hi]
            r = lax.rsqrt(jnp.mean(xv * xv, axis=-1, keepdims=True) + EPS)
            xhat = xv * r
            dxhat = dy * g_ref[:, lo:hi]
            dx_ref[...] = r * (dxhat - xhat * jnp.mean(dxhat * xhat, axis=-1, keepdims=True))
            dg_ref[:, lo:hi] += jnp.sum(dy * xhat, axis=0, keepdims=True)

    def row(wd):
        return pl.BlockSpec((ts, wd), lambda i: (i, 0))

    vec = pl.BlockSpec((1, cfg.d), lambda i: (0, 0))
    return pl.pallas_call(
        body, name=name, grid=(s // ts,), in_specs=[row(cfg.d), row(aq), row(bw), row(cw), vec],
        out_specs=[row(aq), row(bw), row(cw), vec],
        out_shape=[jax.ShapeDtypeStruct((s, aq), F32), jax.ShapeDtypeStruct((s, bw), F32),
                   jax.ShapeDtypeStruct((s, cw), F32), jax.ShapeDtypeStruct((1, cfg.d), F32)],
        compiler_params=_params(("arbitrary",)),
    )(dmix, oa, ob, oc, gain)


def _bias_table_grad(dbiases, buckets, name):
    outs = []
    for idx, (db, bk) in enumerate(zip(dbiases, buckets)):
        h = db.shape[0]

        def body(db_ref, bk_ref, o_ref):
            xv = db_ref[0]
            ids = bk_ref[...]
            lane = lax.broadcasted_iota(jnp.int32, (1, LANES), 1)
            acc = jnp.zeros((1, LANES), F32)
            for b in range(N_BUCKETS):
                tot = jnp.sum(jnp.where(ids == b, xv, 0.0), axis=0, keepdims=True)
                tot = jnp.sum(tot, axis=1, keepdims=True)
                acc = jnp.where(lane == b, tot, acc)
            o_ref[0] = acc

        outs.append(pl.pallas_call(
            body, name=f"{name}_{idx}", grid=(h,),
            in_specs=[pl.BlockSpec((1, BLOCK, 2 * BLOCK), lambda i: (i, 0, 0)),
                      pl.BlockSpec((BLOCK, 2 * BLOCK), lambda i: (0, 0))],
            out_specs=pl.BlockSpec((1, 1, LANES), lambda i: (i, 0, 0)),
            out_shape=jax.ShapeDtypeStruct((h, 1, LANES), F32), compiler_params=_params(("parallel",)),
        )(db, bk)[:, 0, :])
    return outs


def _shift_down(u, n, rows):
    return jnp.where(rows >= n, pltpu.roll(u, n, 0), 0.0)


def _shift_up(u, n, rows, s):
    return jnp.where(rows < s - n, pltpu.roll(u, s - n, 0), 0.0)


def _conv(u, w_ref, b_ref, rows):
    return (b_ref[...] + w_ref[0:1, :] * _shift_down(u, 2, rows) + w_ref[1:2, :] * _shift_down(u, 1, rows)
            + w_ref[2:3, :] * u)


def _conv_act_fwd(u, conv_w, conv_b, f, name):
    s = u.shape[0]
    nf = f // LANES

    def body(ug_ref, uu_ref, wg_ref, wu_ref, bg_ref, bu_ref, act_ref):
        rows = lax.broadcasted_iota(jnp.int32, (s, LANES), 0)
        gate = _conv(ug_ref[...], wg_ref, bg_ref, rows)
        up = _conv(uu_ref[...], wu_ref, bu_ref, rows)
        act_ref[...] = (gate * jax.nn.sigmoid(gate) * up).astype(BF16)

    def col(rws, off):
        return pl.BlockSpec((rws, LANES), lambda j: (0, j + off))

    return pl.pallas_call(
        body, name=name, grid=(nf,),
        in_specs=[col(s, 0), col(s, nf), col(CONV_WIDTH, 0), col(CONV_WIDTH, nf), col(1, 0), col(1, nf)],
        out_specs=col(s, 0), out_shape=jax.ShapeDtypeStruct((s, f), BF16), compiler_params=_params(("parallel",)),
    )(u, u, conv_w, conv_w, conv_b, conv_b)


def _conv_act_bwd(u, dact, conv_w, conv_b, f, name):
    s = u.shape[0]
    nf = f // LANES

    def body(ug_ref, uu_ref, da_ref, wg_ref, wu_ref, bg_ref, bu_ref, dug_ref, duu_ref, dwg_ref, dwu_ref, dbg_ref,
             dbu_ref):
        rows = lax.broadcasted_iota(jnp.int32, (s, LANES), 0)
        ug, uu = ug_ref[...], uu_ref[...]
        gate = _conv(ug, wg_ref, bg_ref, rows)
        up = _conv(uu, wu_ref, bu_ref, rows)
        sg = jax.nn.sigmoid(gate)
        da = da_ref[...]
        dgate = da * up * (sg * (1.0 + gate * (1.0 - sg)))
        dup = da * (gate * sg)
        for du, uv, w_ref, du_ref, dw_ref, db_ref in ((dgate, ug, wg_ref, dug_ref, dwg_ref, dbg_ref),
                                                     (dup, uu, wu_ref, duu_ref, dwu_ref, dbu_ref)):
            du_ref[...] = (w_ref[2:3, :] * du + w_ref[1:2, :] * _shift_up(du, 1, rows, s)
                           + w_ref[0:1, :] * _shift_up(du, 2, rows, s)).astype(BF16)
            dw_ref[0:1, :] = jnp.sum(du * _shift_down(uv, 2, rows), axis=0, keepdims=True)
            dw_ref[1:2, :] = jnp.sum(du * _shift_down(uv, 1, rows), axis=0, keepdims=True)
            dw_ref[2:3, :] = jnp.sum(du * uv, axis=0, keepdims=True)
            db_ref[...] = jnp.sum(du, axis=0, keepdims=True)

    def col(rws, off):
        return pl.BlockSpec((rws, LANES), lambda j: (0, j + off))

    return pl.pallas_call(
        body, name=name, grid=(nf,),
        in_specs=[col(s, 0), col(s, nf), col(s, 0), col(CONV_WIDTH, 0), col(CONV_WIDTH, nf), col(1, 0), col(1, nf)],
        out_specs=[col(s, 0), col(s, 0), col(CONV_WIDTH, 0), col(CONV_WIDTH, 0), col(1, 0), col(1, 0)],
        out_shape=[jax.ShapeDtypeStruct((s, f), BF16)] * 2 + [jax.ShapeDtypeStruct((CONV_WIDTH, f), F32)] * 2
        + [jax.ShapeDtypeStruct((1, f), F32)] * 2,
        compiler_params=_params(("parallel",)),
    )(u, u, dact, conv_w, conv_w, conv_b, conv_b)


def _loss_head(y, target, name):
    s, d = y.shape
    ts = _div_tile(s, 256, 8)

    def body(y_ref, t_ref, dy_ref, l_ref):
        @pl.when(pl.program_id(0) == 0)
        def _():
            l_ref[...] = jnp.zeros_like(l_ref)

        err = y_ref[...] - t_ref[...]
        dy_ref[...] = err * (1.0 / d)
        tot = jnp.sum(jnp.sum(err * err, axis=0, keepdims=True), axis=1, keepdims=True) * (0.5 / d)
        l_ref[...] += jnp.broadcast_to(tot, l_ref.shape)

    row = pl.BlockSpec((ts, d), lambda i: (i, 0))
    return pl.pallas_call(
        body, name=name, grid=(s // ts,), in_specs=[row, row],
        out_specs=[row, pl.BlockSpec((8, LANES), lambda i: (0, 0))],
        out_shape=[jax.ShapeDtypeStruct((s, d), F32), jax.ShapeDtypeStruct((8, LANES), F32)],
        compiler_params=_params(("arbitrary",)),
    )(y, target)


def _adamw(w, g, m, v, name):
    r, c = w.shape
    tr = _div_tile(r, max(8, (1 << 18) // c // 8 * 8), 8)
    c1 = 1.0 - ADAM_B1 ** ADAM_STEP
    c2 = 1.0 - ADAM_B2 ** ADAM_STEP

    def body(w_ref, g_ref, m_ref, v_ref, d_ref, nm_ref, nv_ref):
        gv = g_ref[...]
        nm = ADAM_B1 * m_ref[...] + (1.0 - ADAM_B1) * gv
        nv = ADAM_B2 * v_ref[...] + (1.0 - ADAM_B2) * (gv * gv)
        d_ref[...] = -ADAM_LR * ((nm / c1) / (jnp.sqrt(nv / c2) + ADAM_EPS) + ADAM_WD * w_ref[...])
        nm_ref[...] = nm
        nv_ref[...] = nv

    spec = pl.BlockSpec((tr, c), lambda i: (i, 0))
    return pl.pallas_call(
        body, name=name, grid=(r // tr,), in_specs=[spec] * 4, out_specs=[spec] * 3,
        out_shape=[jax.ShapeDtypeStruct((r, c), F32)] * 3, compiler_params=_params(("parallel",)),
    )(w, g, m, v)


def _adamw_layer(layer, w, g, m, v, bufs, name):
    depth, r, c = w.shape
    tr = _div_tile(r, max(8, (1 << 17) // c // 8 * 8), 8)
    c1 = 1.0 - ADAM_B1 ** ADAM_STEP
    c2 = 1.0 - ADAM_B2 ** ADAM_STEP

    def body(*refs):
        w_ref, g_ref, m_ref, v_ref = refs[:4]
        go_ref, d_ref, nm_ref, nv_ref = refs[-4:]
        gv = g_ref[...]
        nm = ADAM_B1 * m_ref[...] + (1.0 - ADAM_B1) * gv
        nv = ADAM_B2 * v_ref[...] + (1.0 - ADAM_B2) * (gv * gv)
        d_ref[...] = -ADAM_LR * ((nm / c1) / (jnp.sqrt(nv / c2) + ADAM_EPS) + ADAM_WD * w_ref[...])
        nm_ref[...] = nm
        nv_ref[...] = nv
        go_ref[...] = gv

    lay = pl.BlockSpec((None, tr, c), lambda i: (layer, i, 0))
    in_specs = [lay, pl.BlockSpec((tr, c), lambda i: (i, 0)), lay, lay]
    args = [w, g, m, v]
    aliases = {}
    if bufs is not None:
        in_specs += [pl.BlockSpec(memory_space=pl.ANY)] * 4
        args += list(bufs)
        aliases = {4 + k: k for k in range(4)}
    return pl.pallas_call(
        body, name=name, grid=(r // tr,), in_specs=in_specs, out_specs=[lay] * 4,
        out_shape=[jax.ShapeDtypeStruct((depth, r, c), F32)] * 4, input_output_aliases=aliases,
        compiler_params=_params(("parallel",)),
    )(*args)


def _mesh_pos():
    return lax.axis_index("x"), lax.axis_index("y"), lax.axis_index("c")


def _flip(v, bit):
    return 1 - v if bit else v


def _sum_parts(parts, name):
    _, r, c = parts.shape
    tr = _div_tile(r, 256, 16)

    def body(p_ref, o_ref):
        acc = p_ref[0].astype(F32)
        for src in range(1, N_DEVICES):
            acc = acc + p_ref[src].astype(F32)
        o_ref[...] = acc

    return pl.pallas_call(
        body, name=name, grid=(r // tr,), in_specs=[pl.BlockSpec((N_DEVICES, tr, c), lambda i: (0, i, 0))],
        out_specs=pl.BlockSpec((tr, c), lambda i: (i, 0)), out_shape=jax.ShapeDtypeStruct((r, c), F32),
        compiler_params=_params(("parallel",)),
    )(parts)


def _split_start(srcs, lands, plan, ncopies, name):
    nbuf = len(srcs) + len(lands)

    def body(*refs):
        bufs = refs[:nbuf]
        send_sem, recv_sem, token = refs[nbuf], refs[nbuf + 1], refs[-1]
        for k, (src, dst, dev) in enumerate(plan(bufs[:len(srcs)], bufs[len(srcs):])):
            pltpu.make_async_remote_copy(src_ref=src, dst_ref=dst, send_sem=send_sem.at[k], recv_sem=recv_sem.at[k],
                                         device_id=dev, device_id_type=MESH).start()
        token[...] = jnp.zeros_like(token)

    hbm = pl.BlockSpec(memory_space=pltpu.HBM)
    sem = pl.BlockSpec(memory_space=pltpu.SEMAPHORE)
    operands = [pltpu.with_memory_space_constraint(a, pltpu.HBM) for a in (*srcs, *lands)]
    outs = pl.pallas_call(
        body, name=name, in_specs=[hbm] * nbuf,
        out_specs=(sem, sem, *[hbm] * nbuf, pl.BlockSpec(memory_space=pltpu.VMEM)),
        out_shape=(pltpu.SemaphoreType.DMA((ncopies,)), pltpu.SemaphoreType.DMA((ncopies,)),
                   *[pltpu.HBM(a.shape, a.dtype) for a in operands], jax.ShapeDtypeStruct((8, LANES), F32)),
        input_output_aliases={i: 2 + i for i in range(nbuf)},
        compiler_params=pltpu.CompilerParams(has_side_effects=pltpu.SideEffectType.DATAFLOW_SIDE_EFFECTING),
    )(*operands)
    handle = dict(send=outs[0], recv=outs[1], bufs=list(outs[2:2 + nbuf]), nsrc=len(srcs), plan=plan)
    return handle, outs[-1]


def _split_wait(handle, after, name):
    nbuf, nsrc, plan = len(handle["bufs"]), handle["nsrc"], handle["plan"]

    def body(*refs):
        bufs = refs[:nbuf]
        send_sem, recv_sem = refs[nbuf], refs[nbuf + 1]
        for k, (src, dst, dev) in enumerate(plan(bufs[:nsrc], bufs[nsrc:])):
            copy = pltpu.make_async_remote_copy(src_ref=src, dst_ref=dst, send_sem=send_sem.at[k],
                                                recv_sem=recv_sem.at[k], device_id=dev, device_id_type=MESH)
            copy.wait_send()
            copy.wait_recv()

    hbm = pl.BlockSpec(memory_space=pltpu.HBM)
    sem = pl.BlockSpec(memory_space=pltpu.SEMAPHORE)
    outs = pl.pallas_call(
        body, name=name, in_specs=[hbm] * nbuf + [sem, sem, pl.BlockSpec(memory_space=pl.ANY)],
        out_specs=[hbm] * nbuf, out_shape=[pltpu.HBM(a.shape, a.dtype) for a in handle["bufs"]],
        input_output_aliases={i: i for i in range(nbuf)},
        compiler_params=pltpu.CompilerParams(has_side_effects=pltpu.SideEffectType.DATAFLOW_SIDE_EFFECTING),
    )(*handle["bufs"], handle["send"], handle["recv"], after)
    return list(outs[nsrc:])


def _own_slot(shape, dtype, block, index):
    return lax.dynamic_update_slice(lax.empty(shape, dtype), block[None], (index,) + (0,) * block.ndim)


def _gather_plan(srcs, lands):
    x, y, c = _mesh_pos()
    return [(land.at[2 * x + y], land.at[2 * x + y], (*chip, c))
            for land in lands for chip in ((1 - x, y), (x, 1 - y), (1 - x, 1 - y))]


def _scatter_plan(srcs, lands):
    x, y, c = _mesh_pos()
    out = []
    for src, land in zip(srcs, lands):
        half = src.shape[1] // 2
        for d in range(1, N_DEVICES):
            p = (_flip(x, d & 4), _flip(y, d & 2), _flip(c, d & 1))
            out.append((src.at[2 * p[0] + p[1], pl.ds(p[2] * half, half), :], land.at[4 * x + 2 * y + c], p))
    return out


def _swap_plan(srcs, lands):
    x, y, c = _mesh_pos()
    return [(src, land.at[c], (x, y, 1 - c)) for src, land in zip(srcs, lands)]


class _Gathered:
    def __init__(self, groups):
        self.groups = groups
        self.ready = {}

    def get(self, name, after=None):
        if name not in self.ready:
            handle, names, wait_name = next(g for g in self.groups if name in g[1])
            for n, full in zip(names, _split_wait(handle, after, wait_name)):
                self.ready[n] = full.reshape(-1, full.shape[-1])
        return self.ready[name]


def _allreduce_small(flat, name):
    r = flat.shape[0]

    def body(x_ref, o_ref, buf, send_sems, recv_sems):
        x, y, c = _mesh_pos()
        me = 4 * x + 2 * y + c
        buf[me] = x_ref[...]
        started = []
        peers = [(_flip(x, d & 4), _flip(y, d & 2), _flip(c, d & 1)) for d in range(1, N_DEVICES)]
        for d, p in enumerate(peers):
            cp = pltpu.make_async_remote_copy(src_ref=x_ref, dst_ref=buf.at[me], send_sem=send_sems.at[d],
                                              recv_sem=recv_sems.at[d], device_id=p, device_id_type=MESH)
            cp.start()
            started.append(cp)
        for d, p in enumerate(peers):
            slot = buf.at[4 * p[0] + 2 * p[1] + p[2]]
            pltpu.make_async_remote_copy(src_ref=slot, dst_ref=slot, send_sem=send_sems.at[d], recv_sem=recv_sems.at[d],
                                         device_id=p, device_id_type=MESH).wait_recv()
        for cp in started:
            cp.wait_send()
        acc = buf[0]
        for src in range(1, N_DEVICES):
            acc = acc + buf[src]
        o_ref[...] = acc

    vm = pl.BlockSpec(memory_space=pltpu.VMEM)
    return pl.pallas_call(
        body, name=name, in_specs=[vm], out_specs=vm, out_shape=jax.ShapeDtypeStruct((r, LANES), F32),
        scratch_shapes=[pltpu.VMEM((N_DEVICES, r, LANES), F32), pltpu.SemaphoreType.DMA((N_DEVICES - 1,)),
                        pltpu.SemaphoreType.DMA((N_DEVICES - 1,))],
        compiler_params=pltpu.CompilerParams(vmem_limit_bytes=VMEM_LIMIT_BYTES),
    )(flat)


def _bucket_ids(dil):
    rel = (np.arange(BLOCK)[:, None] + BLOCK - np.arange(2 * BLOCK)[None, :]) * dil
    max_exact = N_BUCKETS // 2
    d = np.maximum(rel, 0)
    large = max_exact + (np.log(np.maximum(d, 1).astype(np.float32) / max_exact)
                         / np.float32(np.log(T5_MAX_DIST / max_exact)) * (N_BUCKETS - max_exact)).astype(np.int32)
    large = np.minimum(large, N_BUCKETS - 1)
    return np.where(d < max_exact, d, large).astype(np.int32)


def _block_bias(table, dil):
    onehot = (jnp.asarray(_bucket_ids(dil))[:, :, None] == jnp.arange(N_BUCKETS)[None, None, :]).astype(F32)
    return jnp.einsum("ijb,bh->hij", onehot, table.astype(F32), precision=lax.Precision.HIGHEST)


def _tile_gain(g, n):
    return jnp.tile(g.reshape(1, HEAD_DIM), (1, n))


def _layer_fwd(x, p, cfg):
    w = p["weights"]
    h1 = _rmsnorm_fwd(x, p["attn_norm"], "attn_norm_fwd")
    proj = _matmul(h1, w.get("w_in_t", h1), "nt", F32, "in_proj", tm=1024, tn=768, tk=2048)
    aq, ak, av, bq, bk, bv, cq, ck, cv = _qk_prep(proj, p["gains"], cfg, "qk_prep")
    akt = _dilated_t(ak, 1)
    oa, lse_a = _banded_fwd(aq, akt, av, p["bias_a"], p["sinks"], cfg.nha, cfg.nkva, WINDOW_A - 1, 1, "swa_fwd")
    bkt = _keys_on_lanes(bk, SB_CHUNK)
    ob, tot_b = _sb_fwd(bq, bkt, bv, "stickbreak_fwd")
    ocs, lses, ckts = [], [], []
    for (window, dil), bias in zip(DILATED_PAIRS, p["bias_c"]):
        ckts.append(_dilated_t(ck, dil))
        o, l = _banded_fwd(cq, ckts[-1], cv, bias, None, cfg.nhc, cfg.nhc, window // dil, dil, f"dilated{dil}_fwd")
        ocs.append(o)
        lses.append(l)
    mix, oc, lse_c = _mix_fwd(oa, ob, ocs, lses, p["mix_gain"], cfg, "mix_fwd")
    xm = _matmul(mix, w.get("w_out", mix), "nn", F32, "out_proj", tm=1024, tn=512, tk=2048, residual=x)
    h2 = _rmsnorm_fwd(xm, p["ffn_norm"], "ffn_norm_fwd")
    u = _matmul(h2, w.get("w_up_t", h2), "nt", F32, "up_proj", tm=1024, tn=512, tk=2048)
    act = _conv_act_fwd(u, p["conv_w"], p["conv_b"], cfg.f, "conv_act_fwd")
    y = _matmul(act, w.get("w_down", act), "nn", F32, "down_proj", tm=1024, tn=1024, tk=512, residual=xm)
    saved = dict(x=x, h1=h1, proj=proj, q=(aq, ak, av, bq, bk, bv, cq, ck, cv), oa=oa, lse_a=lse_a, ob=ob,
                 tot_b=tot_b, akt=akt, bkt=bkt, ckts=ckts, oc=oc, lse_c=lse_c, mix=mix, xm=xm, h2=h2, u=u, act=act)
    return y, saved


def _layer_bwd(dy, sv, p, dbias, cfg, on_grad):
    aq, ak, av, bq, bk, bv, cq, ck, cv = sv["q"]
    w = p["weights"]
    anchor = on_grad(_matmul(sv["act"], dy, "tn", BF16, "down_proj_dw", tm=1408, tn=2048, tk=512))
    dact = _matmul(dy, w.get("w_down"), "nt", F32, "down_proj_dx", tm=1024, tn=512, tk=2048)
    dug, duu, dwg, dwu, dbg, dbu = _conv_act_bwd(sv["u"], dact, p["conv_w"], p["conv_b"] + anchor, cfg.f,
                                                 "conv_act_bwd")
    du = jnp.concatenate([dug, duu], axis=1)
    anchor = on_grad(_matmul(du, sv["h2"], "tn", BF16, "up_proj_dw", tm=1408, tn=2048, tk=512))
    dh2 = _matmul(du, w.get("w_up_t"), "nn", F32, "up_proj_dx", tm=1024, tn=2048, tk=512)
    dxm, g_ffn_norm = _rmsnorm_bwd(sv["xm"], p["ffn_norm"] + anchor, dh2, dy, "ffn_norm_bwd")
    anchor = on_grad(_matmul(sv["mix"], dxm, "tn", BF16, "out_proj_dw", tm=1024, tn=2048, tk=512))
    dmix = _matmul(dxm, w.get("w_out"), "nt", F32, "out_proj_dx", tm=1024, tn=512, tk=2048)
    doa, dob, doc, g_mix_gain = _mix_bwd(dmix, sv["oa"], sv["ob"], sv["oc"], p["mix_gain"] + anchor, cfg, "mix_bwd")
    daq, dak, dav, dbias_a, g_sinks = _banded_bwd(aq, ak, sv["akt"], av, sv["oa"], sv["lse_a"], doa, p["bias_a"],
                                                 p["sinks"], dbias[0], cfg.nha, cfg.nkva, WINDOW_A - 1, 1, "swa_bwd")
    dbq, dbk, dbv = _sb_bwd(bq, bk, sv["bkt"], bv, sv["tot_b"], dob, "stickbreak_bwd")
    dcq, dck, dcv, dbias_c = [], [], [], []
    for idx, ((window, dil), bias) in enumerate(zip(DILATED_PAIRS, p["bias_c"])):
        a, b, c, d, _ = _banded_bwd(cq, ck, sv["ckts"][idx], cv, sv["oc"], sv["lse_c"], doc, bias, None, dbias[1][idx],
                                    cfg.nhc, cfg.nhc, window // dil, dil, f"dilated{dil}_bwd")
        dcq.append(a)
        dck.append(b)
        dcv.append(c)
        dbias_c.append(d)
    dproj, g_aq, g_ak, g_cq, g_ck = _qk_prep_bwd(
        sv["proj"], p["gains"], [[daq], [dak], [dav], [dbq], [dbk], [dbv], dcq, dck, dcv], cfg, "qk_prep_bwd")
    anchor = on_grad(_matmul(dproj, sv["h1"], "tn", BF16, "in_proj_dw", tm=768, tn=2048, tk=512))
    dh1 = _matmul(dproj, w.get("w_in_t"), "nn", F32, "in_proj_dx", tm=1024, tn=2048, tk=768)
    dx, g_attn_norm = _rmsnorm_bwd(sv["x"], p["attn_norm"] + anchor, dh1, dxm, "attn_norm_bwd")

    def fold(g):
        return jnp.sum(g.reshape(-1, HEAD_DIM), axis=0)

    small = dict(attn_norm=g_attn_norm[0], a_q_gain=fold(g_aq), a_k_gain=fold(g_ak), a_sinks=g_sinks,
                 c_q_gain=fold(g_cq), c_k_gain=fold(g_ck), mix_out_gain=g_mix_gain[0], ffn_norm=g_ffn_norm[0],
                 conv_w=jnp.concatenate([dwg, dwu], axis=1), conv_b=jnp.concatenate([dbg, dbu], axis=1)[0])
    return dx, small, (dbias_a, dbias_c)


_SMALL = ("attn_norm", "a_q_gain", "a_k_gain", "a_sinks", "c_q_gain", "c_k_gain", "rel_bias_table", "mix_out_gain",
          "ffn_norm", "conv_w", "conv_b")


def _pack(arrays):
    flat = jnp.concatenate([a.reshape(-1).astype(F32) for a in arrays])
    pad = (-flat.shape[0]) % (8 * LANES)
    return jnp.pad(flat, (0, pad)).reshape(-1, LANES)


def _unpack(flat, shapes):
    flat = flat.reshape(-1)
    out, pos = [], 0
    for sh in shapes:
        n = int(np.prod(sh))
        out.append(flat[pos:pos + n].reshape(sh))
        pos += n
    return out


def kernel(x, attn_norm, w_in, a_q_gain, a_k_gain, a_sinks, c_q_gain, c_k_gain, rel_bias_table, mix_out_gain, w_out, ffn_norm, w_up, conv_w, conv_b, w_down, loss_target, m_attn_norm, m_w_in, m_a_q_gain, m_a_k_gain, m_a_sinks, m_c_q_gain, m_c_k_gain, m_rel_bias_table, m_mix_out_gain, m_w_out, m_ffn_norm, m_w_up, m_conv_w, m_conv_b, m_w_down, v_attn_norm, v_w_in, v_a_q_gain, v_a_k_gain, v_a_sinks, v_c_q_gain, v_c_k_gain, v_rel_bias_table, v_mix_out_gain, v_w_out, v_ffn_norm, v_w_up, v_conv_w, v_conv_b, v_w_down):
    depth, d = attn_norm.shape
    f = w_down.shape[1] * N_CHIPS
    cfg = _Cfg(d, f)
    chip = 2 * lax.axis_index("x") + lax.axis_index("y")

    cw_cols = conv_w.shape[2]
    cw_flat = conv_w.reshape(-1)
    cw_rows = -(-cw_flat.shape[0] // (16 * LANES)) * 16
    cw_pad = jnp.pad(cw_flat, (0, cw_rows * LANES - cw_flat.shape[0])).reshape(cw_rows, LANES)

    table_a, table_c = rel_bias_table[:, :cfg.nha], rel_bias_table[:, cfg.nha:]
    bias_a = _block_bias(table_a, 1)
    bias_c = [_block_bias(table_c, dil) for _, dil in DILATED_PAIRS]

    layers, anchor = [], 0.0
    for l in range(depth):
        shards = [w_in[l].T.astype(BF16), w_out[l].astype(BF16), w_up[l].T.astype(BF16), w_down[l].astype(BF16)]
        names = ["w_in_t", "w_out", "w_up_t", "w_down"]
        if l == 0:
            todo = [([cw_pad, shards[0]], ["conv_w", names[0]])] + [([s], [n]) for s, n in zip(shards[1:], names[1:])]
        else:
            todo = [(shards, names)]
        groups = []
        for k, (srcs, group_names) in enumerate(todo):
            lands = [_own_slot((N_CHIPS,) + s.shape, s.dtype, s, chip) for s in srcs]
            handle, token = _split_start([], lands, _gather_plan, 3 * len(lands), f"gather_start_{l}_{k}")
            anchor = anchor + token[0, 0]
            groups.append((handle, group_names, f"gather_wait_{l}_{k}"))
        layers.append(dict(
            attn_norm=attn_norm[l].reshape(1, d), ffn_norm=ffn_norm[l].reshape(1, d),
            mix_gain=mix_out_gain[l].reshape(1, d),
            gains=(_tile_gain(a_q_gain[l], cfg.nha), _tile_gain(a_k_gain[l], cfg.nkva),
                   _tile_gain(c_q_gain[l], cfg.nhc), _tile_gain(c_k_gain[l], cfg.nhc)),
            sinks=a_sinks[l], bias_a=bias_a, bias_c=bias_c, conv_b=conv_b[l].reshape(1, 2 * f),
            weights=_Gathered(groups)))
    cw_all = layers[0]["weights"].get("conv_w", layers[0]["attn_norm"] + anchor)
    cw_all = cw_all.reshape(N_CHIPS, -1)[:, :cw_flat.shape[0]].reshape(N_CHIPS, depth, CONV_WIDTH, cw_cols)
    conv_w_full = jnp.transpose(cw_all, (1, 2, 0, 3)).reshape(depth, CONV_WIDTH, N_CHIPS * cw_cols)
    for l in range(depth):
        layers[l]["conv_w"] = conv_w_full[l]

    act = x[0]
    saved = []
    for l in range(depth):
        act, sv = _layer_fwd(act, layers[l], cfg)
        saved.append(sv)
    dact, loss_blk = _loss_head(act, loss_target[0], "loss_head")
    loss = lax.psum(loss_blk[0, 0], ("x", "y", "c"))

    core = lax.axis_index("c")

    def start_scatter(grads, name):
        srcs = [g.reshape(N_CHIPS, -1, g.shape[-1]) for g in grads]
        lands = []
        for g in srcs:
            half = g.shape[1] // 2
            own = lax.dynamic_slice(g, (chip, core * half, 0), (1, half, g.shape[2]))[0]
            lands.append(_own_slot((N_DEVICES, half, g.shape[2]), g.dtype, own, 2 * chip + core))
        return _split_start(srcs, lands, _scatter_plan, (N_DEVICES - 1) * len(srcs), name)

    def finish_scatter(l, handles, after):
        parts = [pt for k, h in enumerate(handles) for pt in _split_wait(h, after, f"scatter_wait_{l}_{k}")][::-1]
        halves = [_sum_parts(pt, f"sum_grads_{t}") for t, pt in enumerate(parts)]
        lands = [_own_slot((2,) + h.shape, h.dtype, h, core) for h in halves]
        return _split_start(halves, lands, _swap_plan, len(halves), f"swap_start_{l}")[0]

    dbias = (jnp.zeros_like(bias_a), [jnp.zeros_like(b) for b in bias_c])
    small_grads = [None] * depth
    swaps = [None] * depth
    pending = None
    for l in reversed(range(depth)):
        made = []

        def on_grad(g, l=l, made=made):
            if l:
                made.append(g)
                return 0.0
            handle, token = start_scatter([g], f"scatter_start_0_{len(made)}")
            made.append(handle)
            return token[0, 0]

        dact, small_grads[l], dbias = _layer_bwd(dact, saved[l], layers[l], dbias, cfg, on_grad)
        if pending is not None:
            swaps[l + 1] = finish_scatter(l + 1, pending, dact)
        if l:
            handle, token = start_scatter(made, f"scatter_start_{l}")
            pending = [handle]
            layers[l - 1]["conv_b"] = layers[l - 1]["conv_b"] + token[0, 0]
        else:
            pending = made
    grad_x = dact[None]

    tabs = _bias_table_grad([dbias[0]] + dbias[1], [jnp.asarray(_bucket_ids(1))]
                            + [jnp.asarray(_bucket_ids(dil)) for _, dil in DILATED_PAIRS], "bias_table_grad")
    g_table_a = tabs[0][:, :N_BUCKETS].T
    g_table_c = (tabs[1] + tabs[2] + tabs[3])[:, :N_BUCKETS].T
    g_table = jnp.concatenate([g_table_a, g_table_c], axis=1)
    small_local = {k: jnp.stack([small_grads[l][k] for l in range(depth)]) for k in _SMALL if k != "rel_bias_table"}
    small_local["rel_bias_table"] = g_table
    shapes = [small_local[k].shape for k in _SMALL]
    reduced = dict(zip(_SMALL, _unpack(_allreduce_small(_pack([small_local[k] for k in _SMALL]), "allreduce_small"),
                                       shapes)))
    reduced["conv_w"] = lax.dynamic_slice_in_dim(reduced["conv_w"], chip * cw_cols, cw_cols, axis=2)

    given = dict(attn_norm=attn_norm, a_q_gain=a_q_gain, a_k_gain=a_k_gain, a_sinks=a_sinks, c_q_gain=c_q_gain,
                 c_k_gain=c_k_gain, rel_bias_table=rel_bias_table, mix_out_gain=mix_out_gain, ffn_norm=ffn_norm,
                 conv_w=conv_w, conv_b=conv_b)
    moms = dict(attn_norm=(m_attn_norm, v_attn_norm), a_q_gain=(m_a_q_gain, v_a_q_gain),
                a_k_gain=(m_a_k_gain, v_a_k_gain), a_sinks=(m_a_sinks, v_a_sinks), c_q_gain=(m_c_q_gain, v_c_q_gain),
                c_k_gain=(m_c_k_gain, v_c_k_gain), rel_bias_table=(m_rel_bias_table, v_rel_bias_table),
                mix_out_gain=(m_mix_out_gain, v_mix_out_gain), ffn_norm=(m_ffn_norm, v_ffn_norm),
                conv_w=(m_conv_w, v_conv_w), conv_b=(m_conv_b, v_conv_b))
    sshapes = [given[k].shape for k in _SMALL]
    s_delta, s_m, s_v = _adamw(_pack([given[k] for k in _SMALL]), _pack([reduced[k] for k in _SMALL]),
                               _pack([moms[k][0] for k in _SMALL]), _pack([moms[k][1] for k in _SMALL]), "adamw_small")
    grads = dict(reduced)
    deltas = dict(zip(_SMALL, _unpack(s_delta, sshapes)))
    new_m = dict(zip(_SMALL, _unpack(s_m, sshapes)))
    new_v = dict(zip(_SMALL, _unpack(s_v, sshapes)))

    big_given = dict(w_in=(w_in, m_w_in, v_w_in, True), w_out=(w_out, m_w_out, v_w_out, False),
                     w_up=(w_up, m_w_up, v_w_up, True), w_down=(w_down, m_w_down, v_w_down, False))
    names = ("w_in", "w_out", "w_up", "w_down")
    bufs = {name: None for name in names}
    after = s_delta
    for l in reversed(range(depth)):
        if l == 0:
            swaps[0] = finish_scatter(0, pending, after)
        layer_grads = [g.reshape(-1, g.shape[-1]) for g in _split_wait(swaps[l], after, f"swap_wait_{l}")]
        for t, name in enumerate(names):
            wt, mt, vt, transposed = big_given[name]
            g = layer_grads[t].T if transposed else layer_grads[t]
            bufs[name] = _adamw_layer(l, wt, g, mt, vt, bufs[name], f"adamw_{name}_{l}")
            after = bufs[name][1]
    for name in names:
        grads[name], deltas[name], new_m[name], new_v[name] = bufs[name]

    order = ("attn_norm", "w_in", "a_q_gain", "a_k_gain", "a_sinks", "c_q_gain", "c_k_gain", "rel_bias_table",
             "mix_out_gain", "w_out", "ffn_norm", "w_up", "conv_w", "conv_b", "w_down")
    return (loss, grad_x, *[grads[k] for k in order], *[deltas[k] for k in order], *[new_m[k] for k in order],
            *[new_v[k] for k in order])
```

```python
import numpy as np
import jax
import jax.numpy as jnp
from jax import lax
from jax.experimental import pallas as pl
from jax.experimental.pallas import tpu as pltpu

F32 = jnp.float32
BF16 = jnp.bfloat16
MESH = pl.DeviceIdType.MESH

HEAD_DIM = 64
BLOCK = 128
LANES = 128
EPS = 1e-6
NEG_INF = -1e30
WINDOW_A = 128
DILATED_PAIRS = ((128, 1), (512, 4), (2048, 16))
N_BUCKETS = 32
T5_MAX_DIST = 2048
CONV_WIDTH = 3
ADAM_LR = 0.001
ADAM_B1 = 0.9
ADAM_B2 = 0.999
ADAM_EPS = 1e-08
ADAM_WD = 0.01
ADAM_STEP = 10
N_CHIPS = 4
N_DEVICES = 8
VMEM_LIMIT_BYTES = 48 * 1024 * 1024
QK_SCALE = HEAD_DIM ** -0.5


def _params(sem=None):
    return pltpu.CompilerParams(dimension_semantics=sem, vmem_limit_bytes=VMEM_LIMIT_BYTES)


def _div_tile(n, cap, mult):
    best = None
    for t in range(mult, min(n, cap) + 1, mult):
        if n % t == 0:
            best = t
    return n if best is None else best


def _dot(a, b):
    return lax.dot_general(a, b, (((1,), (0,)), ((), ())), preferred_element_type=F32)


def _dot_nt(a, b):
    return lax.dot_general(a, b, (((1,), (1,)), ((), ())), preferred_element_type=F32)


def _dot_tn(a, b):
    return lax.dot_general(a, b, (((0,), (0,)), ((), ())), preferred_element_type=F32)


def _split_dot(x, m):
    hi = x.astype(BF16)
    lo = (x - hi.astype(F32)).astype(BF16)
    return _dot(hi, m) + _dot(lo, m)


class _Cfg:
    def __init__(self, d_model, d_ff):
        nh = d_model // HEAD_DIM
        self.d = d_model
        self.f = d_ff
        self.nha = nh // 4
        self.nkva = self.nha // 4
        self.nhb = nh // 4
        self.nhc = nh // 2
        self.a_q = self.nha * HEAD_DIM
        self.a_kv = self.nkva * HEAD_DIM
        self.b_w = self.nhb * HEAD_DIM
        self.c_w = self.nhc * HEAD_DIM
        sizes = [self.a_q, self.a_kv, self.a_kv, self.b_w, self.b_w, self.b_w, self.c_w, self.c_w, self.c_w]
        starts = [0] + [int(s) for s in np.cumsum(sizes)[:-1]]
        self.sections = list(zip(starts, sizes))
        self.in_width = int(sum(sizes))
        assert all(s % LANES == 0 for s in sizes)


def _matmul(a, b, mode, out_dtype, name, tm=512, tn=512, tk=512, residual=None):
    if mode == "tn":
        kdim, m = a.shape
    else:
        m, kdim = a.shape
    n = b.shape[0] if mode == "nt" else b.shape[1]
    tm, tn, tk = _div_tile(m, tm, LANES), _div_tile(n, tn, LANES), _div_tile(kdim, tk, LANES)
    nk = kdim // tk
    if mode == "tn":
        a_spec = pl.BlockSpec((tk, tm), lambda i, j, k: (k, i))
    else:
        a_spec = pl.BlockSpec((tm, tk), lambda i, j, k: (i, k))
    if mode == "nt":
        b_spec = pl.BlockSpec((tn, tk), lambda i, j, k: (j, k))
    else:
        b_spec = pl.BlockSpec((tk, tn), lambda i, j, k: (k, j))
    dot = {"nn": _dot, "nt": _dot_nt, "tn": _dot_tn}[mode]
    o_spec = pl.BlockSpec((tm, tn), lambda i, j, k: (i, j))
    in_specs = [a_spec, b_spec]
    args = [a, b]
    if residual is not None:
        in_specs.append(o_spec)
        args.append(residual)

    def body(*refs):
        a_ref, b_ref = refs[:2]
        r_ref = refs[2] if residual is not None else None
        o_ref = refs[3] if residual is not None else refs[2]
        part = dot(a_ref[...].astype(BF16), b_ref[...].astype(BF16))
        if nk == 1:
            o_ref[...] = (part if r_ref is None else part + r_ref[...]).astype(out_dtype)
            return
        acc = refs[-1]
        k = pl.program_id(2)

        @pl.when(k == 0)
        def _():
            acc[...] = part

        @pl.when(k > 0)
        def _():
            acc[...] += part

        @pl.when(k == nk - 1)
        def _():
            r = acc[...]
            if r_ref is not None:
                r = r + r_ref[...]
            o_ref[...] = r.astype(out_dtype)

    return pl.pallas_call(
        body, name=name, grid=(m // tm, n // tn, nk), in_specs=in_specs, out_specs=o_spec,
        out_shape=jax.ShapeDtypeStruct((m, n), out_dtype),
        scratch_shapes=[pltpu.VMEM((tm, tn), F32)] if nk > 1 else [],
        compiler_params=_params(("parallel", "parallel", "arbitrary")),
    )(*args)


def _rmsnorm_fwd(x, g, name):
    s, d = x.shape
    ts = _div_tile(s, 256, 8)

    def body(x_ref, g_ref, o_ref):
        xv = x_ref[...]
        r = lax.rsqrt(jnp.mean(xv * xv, axis=-1, keepdims=True) + EPS)
        o_ref[...] = (xv * r * g_ref[...]).astype(BF16)

    return pl.pallas_call(
        body, name=name, grid=(s // ts,),
        in_specs=[pl.BlockSpec((ts, d), lambda i: (i, 0)), pl.BlockSpec((1, d), lambda i: (0, 0))],
        out_specs=pl.BlockSpec((ts, d), lambda i: (i, 0)), out_shape=jax.ShapeDtypeStruct((s, d), BF16),
        compiler_params=_params(("parallel",)),
    )(x, g)


def _rmsnorm_bwd(x, g, dh, dres, name):
    s, d = x.shape
    ts = _div_tile(s, 256, 8)

    def body(x_ref, g_ref, dh_ref, dres_ref, dx_ref, dg_ref):
        @pl.when(pl.program_id(0) == 0)
        def _():
            dg_ref[...] = jnp.zeros_like(dg_ref)

        xv = x_ref[...]
        r = lax.rsqrt(jnp.mean(xv * xv, axis=-1, keepdims=True) + EPS)
        xhat = xv * r
        dhv = dh_ref[...]
        dxhat = dhv * g_ref[...]
        dx_ref[...] = dres_ref[...] + r * (dxhat - xhat * jnp.mean(dxhat * xhat, axis=-1, keepdims=True))
        dg_ref[...] += jnp.sum(dhv * xhat, axis=0, keepdims=True)

    row = pl.BlockSpec((ts, d), lambda i: (i, 0))
    vec = pl.BlockSpec((1, d), lambda i: (0, 0))
    return pl.pallas_call(
        body, name=name, grid=(s // ts,), in_specs=[row, vec, row, row], out_specs=[row, vec],
        out_shape=[jax.ShapeDtypeStruct((s, d), F32), jax.ShapeDtypeStruct((1, d), F32)],
        compiler_params=_params(("arbitrary",)),
    )(x, g, dh, dres)


def _head_mean_matrix():
    idx = np.arange(LANES) // HEAD_DIM
    return jnp.asarray((idx[:, None] == idx[None, :]).astype(np.float32) / HEAD_DIM, dtype=BF16)


def _head_mean(y, m128):
    w = y.shape[1]
    parts = [_split_dot(y[:, c * LANES:(c + 1) * LANES], m128) for c in range(w // LANES)]
    return parts[0] if len(parts) == 1 else jnp.concatenate(parts, axis=1)


_NORMED_SECTIONS = (0, 1, 6, 7)
_QUERY_SECTIONS = (0, 3, 6)


def _qk_prep(proj, gains, cfg, name):
    s = proj.shape[0]
    ts = _div_tile(s, 256, 16)
    m128 = _head_mean_matrix()

    def body(p_ref, m_ref, g0, g1, g6, g7, *outs):
        gref = dict(zip(_NORMED_SECTIONS, (g0, g1, g6, g7)))
        for idx, (st, w) in enumerate(cfg.sections):
            xv = p_ref[:, st:st + w]
            if idx in gref:
                r = lax.rsqrt(_head_mean(xv * xv, m_ref[...]) + EPS)
                xv = xv * r * gref[idx][...]
            if idx in _QUERY_SECTIONS:
                xv = xv * QK_SCALE
            outs[idx][...] = xv.astype(BF16)

    in_specs = [pl.BlockSpec((ts, cfg.in_width), lambda i: (i, 0)), pl.BlockSpec((LANES, LANES), lambda i: (0, 0))]
    in_specs += [pl.BlockSpec((1, cfg.sections[k][1]), lambda i: (0, 0)) for k in _NORMED_SECTIONS]
    out_specs = [pl.BlockSpec((ts, w), lambda i: (i, 0)) for _, w in cfg.sections]
    out_shape = [jax.ShapeDtypeStruct((s, w), BF16) for _, w in cfg.sections]
    return pl.pallas_call(
        body, name=name, grid=(s // ts,), in_specs=in_specs, out_specs=out_specs, out_shape=out_shape,
        compiler_params=_params(("parallel",)),
    )(proj, m128, *gains)


def _qk_prep_bwd(proj, gains, grads, cfg, name):
    s = proj.shape[0]
    ts = _div_tile(s, 128, 16)
    m128 = _head_mean_matrix()
    counts = [len(gl) for gl in grads]
    flat = [g for gl in grads for g in gl]

    def body(*refs):
        p_ref, m_ref = refs[0], refs[1]
        gref = dict(zip(_NORMED_SECTIONS, refs[2:6]))
        g_in = refs[6:6 + len(flat)]
        dp_ref = refs[6 + len(flat)]
        dgain = dict(zip(_NORMED_SECTIONS, refs[7 + len(flat):]))

        @pl.when(pl.program_id(0) == 0)
        def _():
            for k in _NORMED_SECTIONS:
                dgain[k][...] = jnp.zeros_like(dgain[k])

        pos = 0
        for idx, (st, w) in enumerate(cfg.sections):
            dy = g_in[pos][...]
            for extra in g_in[pos + 1:pos + counts[idx]]:
                dy = dy + extra[...]
            pos += counts[idx]
            if idx in gref:
                xv = p_ref[:, st:st + w]
                r = lax.rsqrt(_head_mean(xv * xv, m_ref[...]) + EPS)
                xhat = xv * r
                dxhat = dy * gref[idx][...]
                dgain[idx][...] += jnp.sum(dy * xhat, axis=0, keepdims=True)
                dy = r * (dxhat - xhat * _head_mean(dxhat * xhat, m_ref[...]))
            dp_ref[:, st:st + w] = dy.astype(BF16)

    in_specs = [pl.BlockSpec((ts, cfg.in_width), lambda i: (i, 0)), pl.BlockSpec((LANES, LANES), lambda i: (0, 0))]
    in_specs += [pl.BlockSpec((1, cfg.sections[k][1]), lambda i: (0, 0)) for k in _NORMED_SECTIONS]
    for idx, (_, w) in enumerate(cfg.sections):
        in_specs += [pl.BlockSpec((ts, w), lambda i: (i, 0))] * counts[idx]
    out_specs = [pl.BlockSpec((ts, cfg.in_width), lambda i: (i, 0))]
    out_specs += [pl.BlockSpec((1, cfg.sections[k][1]), lambda i: (0, 0)) for k in _NORMED_SECTIONS]
    out_shape = [jax.ShapeDtypeStruct((s, cfg.in_width), BF16)]
    out_shape += [jax.ShapeDtypeStruct((1, cfg.sections[k][1]), F32) for k in _NORMED_SECTIONS]
    return pl.pallas_call(
        body, name=name, grid=(s // ts,), in_specs=in_specs, out_specs=out_specs, out_shape=out_shape,
        compiler_params=_params(("arbitrary",)),
    )(proj, m128, *gains, *flat)


def _band_masks(max_dist):
    row = lax.broadcasted_iota(jnp.int32, (BLOCK, BLOCK), 0)
    col = lax.broadcasted_iota(jnp.int32, (BLOCK, BLOCK), 1)
    return row + BLOCK - col <= max_dist, col <= row


def _dilated_t(a, dil):
    s, w = a.shape
    return _keys_on_lanes(a.reshape(s // dil, dil * w), BLOCK)


def _undilated(at, dil):
    nblk, dw, _ = at.shape
    return jnp.transpose(at, (0, 2, 1)).reshape(nblk * BLOCK * dil, dw // dil)


def _banded_fwd(q, kt, v, bias, sinks, hq, hk, max_dist, dil, name):
    s = q.shape[0]
    wq, wk, sd, grp = hq * HEAD_DIM, hk * HEAD_DIM, s // dil, hq // hk
    nb = sd // BLOCK
    has_sink = sinks is not None

    def body(*refs):
        if has_sink:
            q_ref, ktp_ref, ktc_ref, vp_ref, vc_ref, b_ref, s_ref, o_ref, l_ref = refs
        else:
            q_ref, ktp_ref, ktc_ref, vp_ref, vc_ref, b_ref, o_ref, l_ref = refs
        i = pl.program_id(1)
        mprev, mcur = _band_masks(max_dist)
        mask = jnp.concatenate([jnp.logical_and(mprev, i > 0), mcur], axis=1)
        for h in range(hq):
            sq = slice(h * HEAD_DIM, (h + 1) * HEAD_DIM)
            sk = slice((h // grp) * HEAD_DIM, (h // grp + 1) * HEAD_DIM)
            kt = jnp.concatenate([ktp_ref[sk, :], ktc_ref[sk, :]], axis=1)
            vv = jnp.concatenate([vp_ref[:, sk], vc_ref[:, sk]], axis=0)
            sc = jnp.where(mask, _dot(q_ref[:, sq], kt) + b_ref[h], NEG_INF)
            m = jnp.max(sc, axis=-1, keepdims=True)
            if has_sink:
                m = jnp.maximum(m, s_ref[h])
            p = jnp.exp(sc - m)
            den = jnp.sum(p, axis=-1, keepdims=True)
            if has_sink:
                den = den + jnp.exp(s_ref[h] - m)
            o_ref[:, sq] = _dot(p.astype(BF16), vv) / den
            l_ref[:, h:h + 1] = m + jnp.log(den)

    qspec = pl.BlockSpec((BLOCK, wq), lambda r, i: (i, r))
    kprev = pl.BlockSpec((BLOCK, wk), lambda r, i: (jnp.maximum(i - 1, 0), r))
    kcur = pl.BlockSpec((BLOCK, wk), lambda r, i: (i, r))
    ktprev = pl.BlockSpec((None, wk, BLOCK), lambda r, i: (jnp.maximum(i - 1, 0), r, 0))
    ktcur = pl.BlockSpec((None, wk, BLOCK), lambda r, i: (i, r, 0))
    in_specs = [qspec, ktprev, ktcur, kprev, kcur, pl.BlockSpec((hq, BLOCK, 2 * BLOCK), lambda r, i: (0, 0, 0))]
    v2 = v.reshape(sd, dil * wk)
    args = [q.reshape(sd, dil * wq), kt, kt, v2, v2, bias]
    if has_sink:
        in_specs.append(pl.BlockSpec(memory_space=pltpu.SMEM))
        args.append(sinks)
    out, lse = pl.pallas_call(
        body, name=name, grid=(dil, nb), in_specs=in_specs,
        out_specs=[qspec, pl.BlockSpec((None, BLOCK, hq), lambda r, i: (r, i, 0))],
        out_shape=[jax.ShapeDtypeStruct((sd, dil * wq), F32), jax.ShapeDtypeStruct((dil, sd, hq), F32)],
        compiler_params=_params(("parallel", "parallel")),
    )(*args)
    return out.reshape(s, wq), jnp.transpose(lse, (1, 0, 2)).reshape(s, hq)


def _per_head_dilated(a, dil):
    s, h = a.shape
    return jnp.transpose(a.reshape(s // dil, dil, h), (1, 0, 2))


def _banded_bwd(q, k, kt, v, lse, dsum, do, bias, sinks, dbias_init, hq, hk, max_dist, dil, name):
    s = q.shape[0]
    wq, wk, sd, grp = hq * HEAD_DIM, hk * HEAD_DIM, s // dil, hq // hk
    nb = sd // BLOCK
    has_sink = sinks is not None

    def body(*refs):
        (q_ref, qn_ref, qt_ref, qtn_ref, kp_ref, kc_ref, ktp_ref, ktc_ref, vtp_ref, vtc_ref, l_ref, ln_ref, d_ref,
         dn_ref, do_ref, don_ref, dot_ref, dotn_ref, b_ref, dbi_ref) = refs[:20]
        rest = refs[20:]
        if has_sink:
            s_ref, dq_ref, dkt_ref, dvt_ref, db_ref, ds_ref = rest
        else:
            dq_ref, dkt_ref, dvt_ref, db_ref = rest
        j = pl.program_id(1)

        @pl.when(jnp.logical_and(pl.program_id(0) == 0, j == 0))
        def _():
            db_ref[...] = dbi_ref[...]
            if has_sink:
                ds_ref[...] = jnp.zeros_like(ds_ref)

        mprev_static, mcur = _band_masks(max_dist)
        mask = jnp.concatenate([jnp.logical_and(mprev_static, j > 0), mcur], axis=1)
        mnext = jnp.logical_and(mprev_static, j + 1 < nb)
        dkt_acc = [jnp.zeros((HEAD_DIM, BLOCK), F32) for _ in range(hk)]
        dvt_acc = [jnp.zeros((HEAD_DIM, BLOCK), F32) for _ in range(hk)]
        for h in range(hq):
            g = h // grp
            sq = slice(h * HEAD_DIM, (h + 1) * HEAD_DIM)
            sk = slice(g * HEAD_DIM, (g + 1) * HEAD_DIM)
            kt2 = jnp.concatenate([ktp_ref[sk, :], ktc_ref[sk, :]], axis=1)
            vt2 = jnp.concatenate([vtp_ref[sk, :], vtc_ref[sk, :]], axis=1)
            k2 = jnp.concatenate([kp_ref[:, sk], kc_ref[:, sk]], axis=0)
            lcol = l_ref[:, h:h + 1]
            dcol = d_ref[:, h:h + 1]
            sc = _dot(q_ref[:, sq], kt2) + b_ref[h]
            p = jnp.where(mask, jnp.exp(sc - lcol), 0.0)
            ds = p * (_dot(do_ref[:, sq], vt2) - dcol)
            dsb = ds.astype(BF16)
            dq_ref[:, sq] = _dot(dsb, k2) * QK_SCALE
            db_ref[h] += ds
            if has_sink:
                psink = jnp.exp(s_ref[h] - lcol)
                tot = jnp.sum(psink * dcol, axis=0, keepdims=True)
                ds_ref[h:h + 1, :] -= jnp.broadcast_to(tot, (1, LANES))
            lncol = ln_ref[:, h:h + 1]
            dncol = dn_ref[:, h:h + 1]
            sn = _dot(qn_ref[:, sq], ktc_ref[sk, :]) + b_ref[h, :, 0:BLOCK]
            pn = jnp.where(mnext, jnp.exp(sn - lncol), 0.0)
            dsn = pn * (_dot(don_ref[:, sq], vtc_ref[sk, :]) - dncol)
            dkt_acc[g] = dkt_acc[g] + (_dot(qt_ref[sq, :], dsb[:, BLOCK:]) + _dot(qtn_ref[sq, :], dsn.astype(BF16)))
            dvt_acc[g] = dvt_acc[g] + (_dot(dot_ref[sq, :], p[:, BLOCK:].astype(BF16))
                                       + _dot(dotn_ref[sq, :], pn.astype(BF16)))
        for g in range(hk):
            sk = slice(g * HEAD_DIM, (g + 1) * HEAD_DIM)
            dkt_ref[sk, :] = dkt_acc[g]
            dvt_ref[sk, :] = dvt_acc[g]

    qcur = pl.BlockSpec((BLOCK, wq), lambda r, j: (j, r))
    qnext = pl.BlockSpec((BLOCK, wq), lambda r, j: (jnp.minimum(j + 1, nb - 1), r))
    qtcur = pl.BlockSpec((None, wq, BLOCK), lambda r, j: (j, r, 0))
    qtnext = pl.BlockSpec((None, wq, BLOCK), lambda r, j: (jnp.minimum(j + 1, nb - 1), r, 0))
    kprev = pl.BlockSpec((BLOCK, wk), lambda r, j: (jnp.maximum(j - 1, 0), r))
    kcur = pl.BlockSpec((BLOCK, wk), lambda r, j: (j, r))
    ktprev = pl.BlockSpec((None, wk, BLOCK), lambda r, j: (jnp.maximum(j - 1, 0), r, 0))
    ktcur = pl.BlockSpec((None, wk, BLOCK), lambda r, j: (j, r, 0))
    bspec = pl.BlockSpec((hq, BLOCK, 2 * BLOCK), lambda r, j: (0, 0, 0))
    hcur = pl.BlockSpec((None, BLOCK, hq), lambda r, j: (r, j, 0))
    hnext = pl.BlockSpec((None, BLOCK, hq), lambda r, j: (r, jnp.minimum(j + 1, nb - 1), 0))
    dob = do.astype(BF16)
    q2, k2, do2 = q.reshape(sd, dil * wq), k.reshape(sd, dil * wk), dob.reshape(sd, dil * wq)
    l3, d3 = _per_head_dilated(lse, dil), _per_head_dilated(dsum, dil)
    qt, vt, dot = _dilated_t(q, dil), _dilated_t(v, dil), _dilated_t(dob, dil)
    in_specs = [qcur, qnext, qtcur, qtnext, kprev, kcur, ktprev, ktcur, ktprev, ktcur, hcur, hnext, hcur, hnext,
                qcur, qnext, qtcur, qtnext, bspec, bspec]
    args = [q2, q2, qt, qt, k2, k2, kt, kt, vt, vt, l3, l3, d3, d3, do2, do2, dot, dot, bias, dbias_init]
    out_specs = [qcur, ktcur, ktcur, bspec]
    out_shape = [jax.ShapeDtypeStruct((sd, dil * wq), F32), jax.ShapeDtypeStruct((nb, dil * wk, BLOCK), F32),
                 jax.ShapeDtypeStruct((nb, dil * wk, BLOCK), F32), jax.ShapeDtypeStruct((hq, BLOCK, 2 * BLOCK), F32)]
    if has_sink:
        in_specs.append(pl.BlockSpec(memory_space=pltpu.SMEM))
        args.append(sinks)
        out_specs.append(pl.BlockSpec((hq, LANES), lambda r, j: (0, 0)))
        out_shape.append(jax.ShapeDtypeStruct((hq, LANES), F32))
    res = pl.pallas_call(
        body, name=name, grid=(dil, nb), in_specs=in_specs, out_specs=out_specs, out_shape=out_shape,
        compiler_params=_params(("arbitrary", "arbitrary")),
    )(*args)
    dq, dk, dv, dbias = res[0].reshape(s, wq), _undilated(res[1], dil), _undilated(res[2], dil), res[3]
    return dq, dk, dv, dbias, (res[4][:, 0] if has_sink else None)


def _neg_softplus(z):
    return -(jnp.maximum(z, 0.0) + jnp.log(1.0 + jnp.exp(-jnp.abs(z))))


SB_CHUNK = 256
HEADS_PER_PAIR = LANES // HEAD_DIM


def _tri(kind):
    row = lax.broadcasted_iota(jnp.int32, (SB_CHUNK, SB_CHUNK), 0)
    col = lax.broadcasted_iota(jnp.int32, (SB_CHUNK, SB_CHUNK), 1)
    return {"ge": row >= col, "lt": row < col, "le": row <= col}[kind].astype(BF16)


def _keys_on_lanes(a, rows):
    s, w = a.shape
    return jnp.transpose(a.reshape(s // rows, rows, w), (0, 2, 1))


def _sb_mask(i, jj):
    row = lax.broadcasted_iota(jnp.int32, (BLOCK, SB_CHUNK), 0)
    col = lax.broadcasted_iota(jnp.int32, (BLOCK, SB_CHUNK), 1)
    return col < row + (i * BLOCK - jj * SB_CHUNK)


def _sb_trips(i):
    return (i * BLOCK) // (2 * SB_CHUNK) + 1


def _sb_rows(jj, n):
    return pl.ds(pl.multiple_of(jj * SB_CHUNK, SB_CHUNK), n * SB_CHUNK)


def _sb_fwd(q, kt, v, name):
    s, w = q.shape
    npair, nb, nc = w // LANES, s // BLOCK, s // SB_CHUNK

    def body(q_ref, kt_ref, v_ref, o_ref, t_ref):
        i = pl.program_id(1)
        lincl = _tri("ge")
        heads = [slice(hh * HEAD_DIM, (hh + 1) * HEAD_DIM) for hh in range(HEADS_PER_PAIR)]
        qs = [q_ref[:, sl] for sl in heads]

        def trip(t, carry, masked):
            lo, hi = 2 * t, 2 * t + 1
            mlo, mhi = (_sb_mask(i, lo), _sb_mask(i, hi)) if masked else (None, None)

            def keep(m, val):
                return val if m is None else jnp.where(m, val, 0.0)

            new = []
            for hh, sl in enumerate(heads):
                o_acc, rem = carry[hh]
                zhi = _dot(qs[hh], kt_ref[hi, sl, :])
                zlo = _dot(qs[hh], kt_ref[lo, sl, :])
                lrhi = keep(mhi, _neg_softplus(zhi))
                lrlo = keep(mlo, _neg_softplus(zlo))
                tothi = jnp.sum(lrhi, axis=-1, keepdims=True)
                ahi = keep(mhi, jnp.exp(zhi + (rem + _split_dot(lrhi, lincl))))
                alo = keep(mlo, jnp.exp(zlo + (rem + tothi + _split_dot(lrlo, lincl))))
                a = jnp.concatenate([alo, ahi], axis=1).astype(BF16)
                new.append((o_acc + _dot(a, v_ref[_sb_rows(lo, 2), sl]),
                            rem + tothi + jnp.sum(lrlo, axis=-1, keepdims=True)))
            return tuple(new)

        init = tuple((jnp.zeros((BLOCK, HEAD_DIM), F32), jnp.zeros((BLOCK, 1), F32)) for _ in heads)
        trips = _sb_trips(i)
        carry = trip(trips - 1, init, True)
        carry = lax.fori_loop(0, trips - 1, lambda t, cr: trip(trips - 2 - t, cr, False), carry)
        for hh, sl in enumerate(heads):
            o_ref[:, sl] = carry[hh][0]
            t_ref[:, sl] = jnp.broadcast_to(carry[hh][1], (BLOCK, HEAD_DIM))

    qspec = pl.BlockSpec((BLOCK, LANES), lambda p, i: (i, p))
    return pl.pallas_call(
        body, name=name, grid=(npair, nb),
        in_specs=[qspec, pl.BlockSpec((nc, LANES, SB_CHUNK), lambda p, i: (0, p, 0)),
                  pl.BlockSpec((s, LANES), lambda p, i: (0, p))],
        out_specs=[qspec, qspec], out_shape=[jax.ShapeDtypeStruct((s, w), F32)] * 2,
        compiler_params=_params(("parallel", "parallel")),
    )(q, kt, v)


def _sb_bwd(q, k, kt, v, tot, do, name):
    s, w = q.shape
    npair, nb, nc = w // LANES, s // BLOCK, s // SB_CHUNK
    dob = do.astype(BF16)

    def body(q_ref, qt_ref, k_ref, kt_ref, vt_ref, t_ref, do_ref, dot_ref, dq_ref, dkt_ref, dvt_ref):
        i = pl.program_id(1)

        @pl.when(i == 0)
        def _():
            dkt_ref[...] = jnp.zeros_like(dkt_ref)
            dvt_ref[...] = jnp.zeros_like(dvt_ref)

        lbefore = _tri("lt")
        lupto = _tri("le")
        heads = [slice(hh * HEAD_DIM, (hh + 1) * HEAD_DIM) for hh in range(HEADS_PER_PAIR)]
        qs = [q_ref[:, sl] for sl in heads]
        qts = [qt_ref[sl, :] for sl in heads]
        dos = [do_ref[:, sl] for sl in heads]
        dots = [dot_ref[sl, :] for sl in heads]
        totals = [t_ref[:, sl.start:sl.start + 1] for sl in heads]

        def trip(t, carry, masked):
            chunks = (2 * t, 2 * t + 1)
            masks = [_sb_mask(i, jj) if masked else None for jj in chunks]

            def keep(m, val):
                return val if m is None else jnp.where(m, val, 0.0)

            new = []
            for hh, sl in enumerate(heads):
                dq_acc, plr, pg = carry[hh]
                zs = [_dot(qs[hh], kt_ref[jj, sl, :]) for jj in chunks]
                lrs = [keep(m, _neg_softplus(z)) for m, z in zip(masks, zs)]
                lr_sums = [jnp.sum(lr, axis=-1, keepdims=True) for lr in lrs]
                before = [plr, plr + lr_sums[0]]
                avs = [keep(m, jnp.exp(z + (totals[hh] - (b + _split_dot(lr, lbefore)))))
                       for m, z, lr, b in zip(masks, zs, lrs, before)]
                gs = [_dot(dos[hh], vt_ref[jj, sl, :]) * a for jj, a in zip(chunks, avs)]
                g_sums = [jnp.sum(g, axis=-1, keepdims=True) for g in gs]
                upto = [pg, pg + g_sums[0]]
                dzs = [keep(m, g - jnp.exp(z + lr) * (u + _split_dot(g, lupto))).astype(BF16)
                       for m, z, lr, g, u in zip(masks, zs, lrs, gs, upto)]
                for jj, dzb, a in zip(chunks, dzs, avs):
                    dkt_ref[jj, sl, :] += _dot(qts[hh], dzb)
                    dvt_ref[jj, sl, :] += _dot(dots[hh], a.astype(BF16))
                dz2 = jnp.concatenate(dzs, axis=1)
                new.append((dq_acc + _dot(dz2, k_ref[_sb_rows(chunks[0], 2), sl]), plr + lr_sums[0] + lr_sums[1],
                            pg + g_sums[0] + g_sums[1]))
            return tuple(new)

        zero = jnp.zeros((BLOCK, 1), F32)
        init = tuple((jnp.zeros((BLOCK, HEAD_DIM), F32), zero, zero) for _ in heads)
        trips = _sb_trips(i)
        carry = lax.fori_loop(0, trips - 1, lambda t, cr: trip(t, cr, False), init)
        carry = trip(trips - 1, carry, True)
        for hh, sl in enumerate(heads):
            dq_ref[:, sl] = carry[hh][0] * QK_SCALE

    qspec = pl.BlockSpec((BLOCK, LANES), lambda p, i: (i, p))
    qtspec = pl.BlockSpec((None, LANES, BLOCK), lambda p, i: (i, p, 0))
    kspec = pl.BlockSpec((s, LANES), lambda p, i: (0, p))
    ktspec = pl.BlockSpec((nc, LANES, SB_CHUNK), lambda p, i: (0, p, 0))
    dq, dkt, dvt = pl.pallas_call(
        body, name=name, grid=(npair, nb), in_specs=[qspec, qtspec, kspec, ktspec, ktspec, qspec, qspec, qtspec],
        out_specs=[qspec, ktspec, ktspec],
        out_shape=[jax.ShapeDtypeStruct((s, w), F32)] + [jax.ShapeDtypeStruct((nc, w, SB_CHUNK), F32)] * 2,
        compiler_params=_params(("parallel", "arbitrary")),
    )(q, _keys_on_lanes(q, BLOCK), k, kt, _keys_on_lanes(v, SB_CHUNK), tot, dob, _keys_on_lanes(dob, BLOCK))

    def rows_first(t):
        return jnp.transpose(t, (0, 2, 1)).reshape(s, w)

    return dq, rows_first(dkt), rows_first(dvt)


def _group_norm(xv, g):
    r = lax.rsqrt(jnp.mean(xv * xv, axis=-1, keepdims=True) + EPS)
    return xv * r * g


def _head_spread(nheads):
    return jnp.asarray(np.repeat(np.eye(nheads, dtype=np.float32), HEAD_DIM, axis=1), dtype=BF16)


def _mix_fwd(oa, ob, ocs, lses, gain, cfg, name):
    s = oa.shape[0]
    ts = _div_tile(s, 256, 16)
    aq, bw, cw, nhc = cfg.a_q, cfg.b_w, cfg.c_w, cfg.nhc

    def body(oa_ref, ob_ref, c1, c2, c3, l1, l2, l3, sp_ref, g_ref, mix_ref, oc_ref, lse_ref):
        m = jnp.maximum(jnp.maximum(l1[...], l2[...]), l3[...])
        es = [jnp.exp(l[...] - m) for l in (l1, l2, l3)]
        den = es[0] + es[1] + es[2]
        oc = sum(_split_dot(e / den, sp_ref[...]) * c[...] for e, c in zip(es, (c1, c2, c3)))
        oc_ref[...] = oc
        lse_ref[...] = m + jnp.log(den)
        mix_ref[:, 0:aq] = _group_norm(oa_ref[...], g_ref[:, 0:aq]).astype(BF16)
        mix_ref[:, aq:aq + bw] = _group_norm(ob_ref[...], g_ref[:, aq:aq + bw]).astype(BF16)
        mix_ref[:, aq + bw:] = _group_norm(oc, g_ref[:, aq + bw:]).astype(BF16)

    def row(wd):
        return pl.BlockSpec((ts, wd), lambda i: (i, 0))

    return pl.pallas_call(
        body, name=name, grid=(s // ts,),
        in_specs=[row(aq), row(bw)] + [row(cw)] * 3 + [row(nhc)] * 3
        + [pl.BlockSpec((nhc, cw), lambda i: (0, 0)), pl.BlockSpec((1, cfg.d), lambda i: (0, 0))],
        out_specs=[row(cfg.d), row(cw), row(nhc)],
        out_shape=[jax.ShapeDtypeStruct((s, cfg.d), BF16), jax.ShapeDtypeStruct((s, cw), F32),
                   jax.ShapeDtypeStruct((s, nhc), F32)],
        compiler_params=_params(("parallel",)),
    )(oa, ob, *ocs, *lses, _head_spread(nhc), gain)


def _mix_bwd(dmix, oa, ob, oc, gain, cfg, name):
    s = oa.shape[0]
    ts = _div_tile(s, 256, 8)
    aq, bw, cw = cfg.a_q, cfg.b_w, cfg.c_w

    def body(dm_ref, oa_ref, ob_ref, oc_ref, g_ref, fa_ref, fc_ref, da_ref, db_ref, dc_ref, dg_ref, sa_ref, sc_ref):
        @pl.when(pl.program_id(0) == 0)
        def _():
            dg_ref[...] = jnp.zeros_like(dg_ref)

        for x_ref, dx_ref, lo, hi, fold in ((oa_ref, da_ref, 0, aq, (fa_ref, sa_ref)), (ob_ref, db_ref, aq, aq + bw, None),
                                            (oc_ref, dc_ref, aq + bw, aq + bw + cw, (fc_ref, sc_ref))):
            xv = x_ref[...]
            dy = dm_ref[:, lo:hi]
            r = lax.rsqrt(jnp.mean(xv * xv, axis=-1, keepdims=True) + EPS)
            xhat = xv * r
            dxhat = dy * g_ref[:, lo:hi]
            dx = r * (dxhat - xhat * jnp.mean(dxhat * xhat, axis=-1, keepdims=True))
            dx_ref[...] = dx
            dg_ref[:, lo:hi] += jnp.sum(dy * xhat, axis=0, keepdims=True)
            if fold is not None:
                fold[1][...] = _split_dot(dx * xv, fold[0][...])

    def row(wd):
        return pl.BlockSpec((ts, wd), lambda i: (i, 0))

    vec = pl.BlockSpec((1, cfg.d), lambda i: (0, 0))
    return pl.pallas_call(
        body, name=name, grid=(s // ts,),
        in_specs=[row(cfg.d), row(aq), row(bw), row(cw), vec, pl.BlockSpec((aq, cfg.nha), lambda i: (0, 0)),
                  pl.BlockSpec((cw, cfg.nhc), lambda i: (0, 0))],
        out_specs=[row(aq), row(bw), row(cw), vec, row(cfg.nha), row(cfg.nhc)],
        out_shape=[jax.ShapeDtypeStruct((s, aq), F32), jax.ShapeDtypeStruct((s, bw), F32),
                   jax.ShapeDtypeStruct((s, cw), F32), jax.ShapeDtypeStruct((1, cfg.d), F32),
                   jax.ShapeDtypeStruct((s, cfg.nha), F32), jax.ShapeDtypeStruct((s, cfg.nhc), F32)],
        compiler_params=_params(("arbitrary",)),
    )(dmix, oa, ob, oc, gain, _head_spread(cfg.nha).T, _head_spread(cfg.nhc).T)


def _bias_table_grad(dbiases, buckets, name):
    outs = []
    for idx, (db, bk) in enumerate(zip(dbiases, buckets)):
        h = db.shape[0]

        def body(db_ref, bk_ref, o_ref):
            xv = db_ref[0]
            ids = bk_ref[...]
            lane = lax.broadcasted_iota(jnp.int32, (1, LANES), 1)
            acc = jnp.zeros((1, LANES), F32)
            for b in range(N_BUCKETS):
                tot = jnp.sum(jnp.where(ids == b, xv, 0.0), axis=0, keepdims=True)
                tot = jnp.sum(tot, axis=1, keepdims=True)
                acc = jnp.where(lane == b, tot, acc)
            o_ref[0] = acc

        outs.append(pl.pallas_call(
            body, name=f"{name}_{idx}", grid=(h,),
            in_specs=[pl.BlockSpec((1, BLOCK, 2 * BLOCK), lambda i: (i, 0, 0)),
                      pl.BlockSpec((BLOCK, 2 * BLOCK), lambda i: (0, 0))],
            out_specs=pl.BlockSpec((1, 1, LANES), lambda i: (i, 0, 0)),
            out_shape=jax.ShapeDtypeStruct((h, 1, LANES), F32), compiler_params=_params(("parallel",)),
        )(db, bk)[:, 0, :])
    return outs


def _shift_down(u, n, rows):
    return jnp.where(rows >= n, pltpu.roll(u, n, 0), 0.0)


def _shift_up(u, n, rows, s):
    return jnp.where(rows < s - n, pltpu.roll(u, s - n, 0), 0.0)


def _conv(u, w_ref, b_ref, rows):
    return (b_ref[...] + w_ref[0:1, :] * _shift_down(u, 2, rows) + w_ref[1:2, :] * _shift_down(u, 1, rows)
            + w_ref[2:3, :] * u)


def _conv_act_fwd(u, conv_w, conv_b, f, name):
    s = u.shape[0]
    nf = f // LANES

    def body(ug_ref, uu_ref, wg_ref, wu_ref, bg_ref, bu_ref, act_ref):
        rows = lax.broadcasted_iota(jnp.int32, (s, LANES), 0)
        gate = _conv(ug_ref[...], wg_ref, bg_ref, rows)
        up = _conv(uu_ref[...], wu_ref, bu_ref, rows)
        act_ref[...] = (gate * jax.nn.sigmoid(gate) * up).astype(BF16)

    def col(rws, off):
        return pl.BlockSpec((rws, LANES), lambda j: (0, j + off))

    return pl.pallas_call(
        body, name=name, grid=(nf,),
        in_specs=[col(s, 0), col(s, nf), col(CONV_WIDTH, 0), col(CONV_WIDTH, nf), col(1, 0), col(1, nf)],
        out_specs=col(s, 0), out_shape=jax.ShapeDtypeStruct((s, f), BF16), compiler_params=_params(("parallel",)),
    )(u, u, conv_w, conv_w, conv_b, conv_b)


def _conv_act_bwd(u, dact, conv_w, conv_b, f, name):
    s = u.shape[0]
    nf = f // LANES

    def body(ug_ref, uu_ref, da_ref, wg_ref, wu_ref, bg_ref, bu_ref, dug_ref, duu_ref, dwg_ref, dwu_ref, dbg_ref,
             dbu_ref):
        rows = lax.broadcasted_iota(jnp.int32, (s, LANES), 0)
        ug, uu = ug_ref[...], uu_ref[...]
        gate = _conv(ug, wg_ref, bg_ref, rows)
        up = _conv(uu, wu_ref, bu_ref, rows)
        sg = jax.nn.sigmoid(gate)
        da = da_ref[...]
        dgate = da * up * (sg * (1.0 + gate * (1.0 - sg)))
        dup = da * (gate * sg)
        for du, uv, w_ref, du_ref, dw_ref, db_ref in ((dgate, ug, wg_ref, dug_ref, dwg_ref, dbg_ref),
                                                     (dup, uu, wu_ref, duu_ref, dwu_ref, dbu_ref)):
            du_ref[...] = (w_ref[2:3, :] * du + w_ref[1:2, :] * _shift_up(du, 1, rows, s)
                           + w_ref[0:1, :] * _shift_up(du, 2, rows, s)).astype(BF16)
            dw_ref[0:1, :] = jnp.sum(du * _shift_down(uv, 2, rows), axis=0, keepdims=True)
            dw_ref[1:2, :] = jnp.sum(du * _shift_down(uv, 1, rows), axis=0, keepdims=True)
            dw_ref[2:3, :] = jnp.sum(du * uv, axis=0, keepdims=True)
            db_ref[...] = jnp.sum(du, axis=0, keepdims=True)

    def col(rws, off):
        return pl.BlockSpec((rws, LANES), lambda j: (0, j + off))

    return pl.pallas_call(
        body, name=name, grid=(nf,),
        in_specs=[col(s, 0), col(s, nf), col(s, 0), col(CONV_WIDTH, 0), col(CONV_WIDTH, nf), col(1, 0), col(1, nf)],
        out_specs=[col(s, 0), col(s, 0), col(CONV_WIDTH, 0), col(CONV_WIDTH, 0), col(1, 0), col(1, 0)],
        out_shape=[jax.ShapeDtypeStruct((s, f), BF16)] * 2 + [jax.ShapeDtypeStruct((CONV_WIDTH, f), F32)] * 2
        + [jax.ShapeDtypeStruct((1, f), F32)] * 2,
        compiler_params=_params(("parallel",)),
    )(u, u, dact, conv_w, conv_w, conv_b, conv_b)


def _loss_head(y, target, name):
    s, d = y.shape
    ts = _div_tile(s, 256, 8)

    def body(y_ref, t_ref, dy_ref, l_ref):
        @pl.when(pl.program_id(0) == 0)
        def _():
            l_ref[...] = jnp.zeros_like(l_ref)

        err = y_ref[...] - t_ref[...]
        dy_ref[...] = err * (1.0 / d)
        tot = jnp.sum(jnp.sum(err * err, axis=0, keepdims=True), axis=1, keepdims=True) * (0.5 / d)
        l_ref[...] += jnp.broadcast_to(tot, l_ref.shape)

    row = pl.BlockSpec((ts, d), lambda i: (i, 0))
    return pl.pallas_call(
        body, name=name, grid=(s // ts,), in_specs=[row, row],
        out_specs=[row, pl.BlockSpec((8, LANES), lambda i: (0, 0))],
        out_shape=[jax.ShapeDtypeStruct((s, d), F32), jax.ShapeDtypeStruct((8, LANES), F32)],
        compiler_params=_params(("arbitrary",)),
    )(y, target)


def _adamw(w, g, m, v, name):
    r, c = w.shape
    tr = _div_tile(r, max(8, (1 << 18) // c // 8 * 8), 8)
    c1 = 1.0 - ADAM_B1 ** ADAM_STEP
    c2 = 1.0 - ADAM_B2 ** ADAM_STEP

    def body(w_ref, g_ref, m_ref, v_ref, d_ref, nm_ref, nv_ref):
        gv = g_ref[...]
        nm = ADAM_B1 * m_ref[...] + (1.0 - ADAM_B1) * gv
        nv = ADAM_B2 * v_ref[...] + (1.0 - ADAM_B2) * (gv * gv)
        d_ref[...] = -ADAM_LR * ((nm / c1) / (jnp.sqrt(nv / c2) + ADAM_EPS) + ADAM_WD * w_ref[...])
        nm_ref[...] = nm
        nv_ref[...] = nv

    spec = pl.BlockSpec((tr, c), lambda i: (i, 0))
    return pl.pallas_call(
        body, name=name, grid=(r // tr,), in_specs=[spec] * 4, out_specs=[spec] * 3,
        out_shape=[jax.ShapeDtypeStruct((r, c), F32)] * 3, compiler_params=_params(("parallel",)),
    )(w, g, m, v)


def _adamw_layer(layer, w, g, m, v, bufs, name):
    depth, r, c = w.shape
    tr = _div_tile(r, max(8, (1 << 17) // c // 8 * 8), 8)
    c1 = 1.0 - ADAM_B1 ** ADAM_STEP
    c2 = 1.0 - ADAM_B2 ** ADAM_STEP

    def body(*refs):
        w_ref, g_ref, m_ref, v_ref = refs[:4]
        go_ref, d_ref, nm_ref, nv_ref = refs[-4:]
        gv = g_ref[...]
        nm = ADAM_B1 * m_ref[...] + (1.0 - ADAM_B1) * gv
        nv = ADAM_B2 * v_ref[...] + (1.0 - ADAM_B2) * (gv * gv)
        d_ref[...] = -ADAM_LR * ((nm / c1) / (jnp.sqrt(nv / c2) + ADAM_EPS) + ADAM_WD * w_ref[...])
        nm_ref[...] = nm
        nv_ref[...] = nv
        go_ref[...] = gv

    lay = pl.BlockSpec((None, tr, c), lambda i: (layer, i, 0))
    in_specs = [lay, pl.BlockSpec((tr, c), lambda i: (i, 0)), lay, lay]
    args = [w, g, m, v]
    aliases = {}
    if bufs is not None:
        in_specs += [pl.BlockSpec(memory_space=pl.ANY)] * 4
        args += list(bufs)
        aliases = {4 + k: k for k in range(4)}
    return pl.pallas_call(
        body, name=name, grid=(r // tr,), in_specs=in_specs, out_specs=[lay] * 4,
        out_shape=[jax.ShapeDtypeStruct((depth, r, c), F32)] * 4, input_output_aliases=aliases,
        compiler_params=_params(("parallel",)),
    )(*args)


def _mesh_pos():
    return lax.axis_index("x"), lax.axis_index("y"), lax.axis_index("c")


def _flip(v, bit):
    return 1 - v if bit else v


def _sum_parts(parts, name):
    _, r, c = parts.shape
    tr = _div_tile(r, 256, 16)

    def body(p_ref, o_ref):
        acc = p_ref[0].astype(F32)
        for src in range(1, N_DEVICES):
            acc = acc + p_ref[src].astype(F32)
        o_ref[...] = acc

    return pl.pallas_call(
        body, name=name, grid=(r // tr,), in_specs=[pl.BlockSpec((N_DEVICES, tr, c), lambda i: (0, i, 0))],
        out_specs=pl.BlockSpec((tr, c), lambda i: (i, 0)), out_shape=jax.ShapeDtypeStruct((r, c), F32),
        compiler_params=_params(("parallel",)),
    )(parts)


def _split_start(srcs, lands, plan, ncopies, name):
    nbuf = len(srcs) + len(lands)

    def body(*refs):
        bufs = refs[:nbuf]
        send_sem, recv_sem, token = refs[nbuf], refs[nbuf + 1], refs[-1]
        for k, (src, dst, dev) in enumerate(plan(bufs[:len(srcs)], bufs[len(srcs):])):
            pltpu.make_async_remote_copy(src_ref=src, dst_ref=dst, send_sem=send_sem.at[k], recv_sem=recv_sem.at[k],
                                         device_id=dev, device_id_type=MESH).start()
        token[...] = jnp.zeros_like(token)

    hbm = pl.BlockSpec(memory_space=pltpu.HBM)
    sem = pl.BlockSpec(memory_space=pltpu.SEMAPHORE)
    operands = [pltpu.with_memory_space_constraint(a, pltpu.HBM) for a in (*srcs, *lands)]
    outs = pl.pallas_call(
        body, name=name, in_specs=[hbm] * nbuf,
        out_specs=(sem, sem, *[hbm] * nbuf, pl.BlockSpec(memory_space=pltpu.VMEM)),
        out_shape=(pltpu.SemaphoreType.DMA((ncopies,)), pltpu.SemaphoreType.DMA((ncopies,)),
                   *[pltpu.HBM(a.shape, a.dtype) for a in operands], jax.ShapeDtypeStruct((8, LANES), F32)),
        input_output_aliases={i: 2 + i for i in range(nbuf)},
        compiler_params=pltpu.CompilerParams(has_side_effects=pltpu.SideEffectType.DATAFLOW_SIDE_EFFECTING),
    )(*operands)
    handle = dict(send=outs[0], recv=outs[1], bufs=list(outs[2:2 + nbuf]), nsrc=len(srcs), plan=plan)
    return handle, outs[-1]


def _split_wait(handle, after, name):
    nbuf, nsrc, plan = len(handle["bufs"]), handle["nsrc"], handle["plan"]

    def body(*refs):
        bufs = refs[:nbuf]
        send_sem, recv_sem = refs[nbuf], refs[nbuf + 1]
        for k, (src, dst, dev) in enumerate(plan(bufs[:nsrc], bufs[nsrc:])):
            copy = pltpu.make_async_remote_copy(src_ref=src, dst_ref=dst, send_sem=send_sem.at[k],
                                                recv_sem=recv_sem.at[k], device_id=dev, device_id_type=MESH)
            copy.wait_send()
            copy.wait_recv()

    hbm = pl.BlockSpec(memory_space=pltpu.HBM)
    sem = pl.BlockSpec(memory_space=pltpu.SEMAPHORE)
    outs = pl.pallas_call(
        body, name=name, in_specs=[hbm] * nbuf + [sem, sem, pl.BlockSpec(memory_space=pl.ANY)],
        out_specs=[hbm] * nbuf, out_shape=[pltpu.HBM(a.shape, a.dtype) for a in handle["bufs"]],
        input_output_aliases={i: i for i in range(nbuf)},
        compiler_params=pltpu.CompilerParams(has_side_effects=pltpu.SideEffectType.DATAFLOW_SIDE_EFFECTING),
    )(*handle["bufs"], handle["send"], handle["recv"], after)
    return list(outs[nsrc:])


def _own_slot(shape, dtype, block, index):
    return lax.dynamic_update_slice(lax.empty(shape, dtype), block[None], (index,) + (0,) * block.ndim)


def _gather_plan(srcs, lands):
    x, y, c = _mesh_pos()
    return [(land.at[2 * x + y], land.at[2 * x + y], (*chip, c))
            for land in lands for chip in ((1 - x, y), (x, 1 - y), (1 - x, 1 - y))]


def _scatter_plan(srcs, lands):
    x, y, c = _mesh_pos()
    out = []
    for src, land in zip(srcs, lands):
        half = src.shape[1] // 2
        for d in range(1, N_DEVICES):
            p = (_flip(x, d & 4), _flip(y, d & 2), _flip(c, d & 1))
            out.append((src.at[2 * p[0] + p[1], pl.ds(p[2] * half, half), :], land.at[4 * x + 2 * y + c], p))
    return out


def _swap_plan(srcs, lands):
    x, y, c = _mesh_pos()
    return [(src, land.at[c], (x, y, 1 - c)) for src, land in zip(srcs, lands)]


class _Gathered:
    def __init__(self, groups):
        self.groups = groups
        self.ready = {}

    def get(self, name, after=None):
        if name not in self.ready:
            handle, names, wait_name = next(g for g in self.groups if name in g[1])
            for n, full in zip(names, _split_wait(handle, after, wait_name)):
                self.ready[n] = full.reshape(-1, full.shape[-1])
        return self.ready[name]


def _allreduce_small(flat, name):
    r = flat.shape[0]

    def body(x_ref, o_ref, buf, send_sems, recv_sems):
        x, y, c = _mesh_pos()
        me = 4 * x + 2 * y + c
        buf[me] = x_ref[...]
        started = []
        peers = [(_flip(x, d & 4), _flip(y, d & 2), _flip(c, d & 1)) for d in range(1, N_DEVICES)]
        for d, p in enumerate(peers):
            cp = pltpu.make_async_remote_copy(src_ref=x_ref, dst_ref=buf.at[me], send_sem=send_sems.at[d],
                                              recv_sem=recv_sems.at[d], device_id=p, device_id_type=MESH)
            cp.start()
            started.append(cp)
        for d, p in enumerate(peers):
            slot = buf.at[4 * p[0] + 2 * p[1] + p[2]]
            pltpu.make_async_remote_copy(src_ref=slot, dst_ref=slot, send_sem=send_sems.at[d], recv_sem=recv_sems.at[d],
                                         device_id=p, device_id_type=MESH).wait_recv()
        for cp in started:
            cp.wait_send()
        acc = buf[0]
        for src in range(1, N_DEVICES):
            acc = acc + buf[src]
        o_ref[...] = acc

    vm = pl.BlockSpec(memory_space=pltpu.VMEM)
    return pl.pallas_call(
        body, name=name, in_specs=[vm], out_specs=vm, out_shape=jax.ShapeDtypeStruct((r, LANES), F32),
        scratch_shapes=[pltpu.VMEM((N_DEVICES, r, LANES), F32), pltpu.SemaphoreType.DMA((N_DEVICES - 1,)),
                        pltpu.SemaphoreType.DMA((N_DEVICES - 1,))],
        compiler_params=pltpu.CompilerParams(vmem_limit_bytes=VMEM_LIMIT_BYTES),
    )(flat)


def _bucket_ids(dil):
    rel = (np.arange(BLOCK)[:, None] + BLOCK - np.arange(2 * BLOCK)[None, :]) * dil
    max_exact = N_BUCKETS // 2
    d = np.maximum(rel, 0)
    large = max_exact + (np.log(np.maximum(d, 1).astype(np.float32) / max_exact)
                         / np.float32(np.log(T5_MAX_DIST / max_exact)) * (N_BUCKETS - max_exact)).astype(np.int32)
    large = np.minimum(large, N_BUCKETS - 1)
    return np.where(d < max_exact, d, large).astype(np.int32)


def _block_bias(table, dil):
    onehot = (jnp.asarray(_bucket_ids(dil))[:, :, None] == jnp.arange(N_BUCKETS)[None, None, :]).astype(F32)
    return jnp.einsum("ijb,bh->hij", onehot, table.astype(F32), precision=lax.Precision.HIGHEST)


def _tile_gain(g, n):
    return jnp.tile(g.reshape(1, HEAD_DIM), (1, n))


def _layer_fwd(x, p, cfg):
    w = p["weights"]
    h1 = _rmsnorm_fwd(x, p["attn_norm"], "attn_norm_fwd")
    proj = _matmul(h1, w.get("w_in_t", h1), "nt", F32, "in_proj", tm=1024, tn=768, tk=2048)
    aq, ak, av, bq, bk, bv, cq, ck, cv = _qk_prep(proj, p["gains"], cfg, "qk_prep")
    akt = _dilated_t(ak, 1)
    oa, lse_a = _banded_fwd(aq, akt, av, p["bias_a"], p["sinks"], cfg.nha, cfg.nkva, WINDOW_A - 1, 1, "swa_fwd")
    bkt = _keys_on_lanes(bk, SB_CHUNK)
    ob, tot_b = _sb_fwd(bq, bkt, bv, "stickbreak_fwd")
    ocs, lses, ckts = [], [], []
    for (window, dil), bias in zip(DILATED_PAIRS, p["bias_c"]):
        ckts.append(_dilated_t(ck, dil))
        o, l = _banded_fwd(cq, ckts[-1], cv, bias, None, cfg.nhc, cfg.nhc, window // dil, dil, f"dilated{dil}_fwd")
        ocs.append(o)
        lses.append(l)
    mix, oc, lse_c = _mix_fwd(oa, ob, ocs, lses, p["mix_gain"], cfg, "mix_fwd")
    xm = _matmul(mix, w.get("w_out", mix), "nn", F32, "out_proj", tm=1024, tn=512, tk=2048, residual=x)
    h2 = _rmsnorm_fwd(xm, p["ffn_norm"], "ffn_norm_fwd")
    u = _matmul(h2, w.get("w_up_t", h2), "nt", F32, "up_proj", tm=1024, tn=512, tk=2048)
    act = _conv_act_fwd(u, p["conv_w"], p["conv_b"], cfg.f, "conv_act_fwd")
    y = _matmul(act, w.get("w_down", act), "nn", F32, "down_proj", tm=1024, tn=1024, tk=512, residual=xm)
    saved = dict(x=x, h1=h1, proj=proj, q=(aq, ak, av, bq, bk, bv, cq, ck, cv), oa=oa, lse_a=lse_a, ob=ob,
                 tot_b=tot_b, akt=akt, bkt=bkt, ckts=ckts, oc=oc, lse_c=lse_c, mix=mix, xm=xm, h2=h2, u=u, act=act)
    return y, saved


def _layer_bwd(dy, sv, p, dbias, cfg, on_grad):
    aq, ak, av, bq, bk, bv, cq, ck, cv = sv["q"]
    w = p["weights"]
    anchor = on_grad(_matmul(sv["act"], dy, "tn", BF16, "down_proj_dw", tm=1408, tn=2048, tk=512))
    dact = _matmul(dy, w.get("w_down"), "nt", F32, "down_proj_dx", tm=1024, tn=512, tk=2048)
    dug, duu, dwg, dwu, dbg, dbu = _conv_act_bwd(sv["u"], dact, p["conv_w"], p["conv_b"] + anchor, cfg.f,
                                                 "conv_act_bwd")
    du = jnp.concatenate([dug, duu], axis=1)
    anchor = on_grad(_matmul(du, sv["h2"], "tn", BF16, "up_proj_dw", tm=1408, tn=2048, tk=512))
    dh2 = _matmul(du, w.get("w_up_t"), "nn", F32, "up_proj_dx", tm=1024, tn=2048, tk=512)
    dxm, g_ffn_norm = _rmsnorm_bwd(sv["xm"], p["ffn_norm"] + anchor, dh2, dy, "ffn_norm_bwd")
    anchor = on_grad(_matmul(sv["mix"], dxm, "tn", BF16, "out_proj_dw", tm=1024, tn=2048, tk=512))
    dmix = _matmul(dxm, w.get("w_out"), "nt", F32, "out_proj_dx", tm=1024, tn=512, tk=2048)
    doa, dob, doc, g_mix_gain, dsum_a, dsum_c = _mix_bwd(dmix, sv["oa"], sv["ob"], sv["oc"], p["mix_gain"] + anchor,
                                                         cfg, "mix_bwd")
    daq, dak, dav, dbias_a, g_sinks = _banded_bwd(aq, ak, sv["akt"], av, sv["lse_a"], dsum_a, doa, p["bias_a"],
                                                 p["sinks"], dbias[0], cfg.nha, cfg.nkva, WINDOW_A - 1, 1, "swa_bwd")
    dbq, dbk, dbv = _sb_bwd(bq, bk, sv["bkt"], bv, sv["tot_b"], dob, "stickbreak_bwd")
    dcq, dck, dcv, dbias_c = [], [], [], []
    for idx, ((window, dil), bias) in enumerate(zip(DILATED_PAIRS, p["bias_c"])):
        a, b, c, d, _ = _banded_bwd(cq, ck, sv["ckts"][idx], cv, sv["lse_c"], dsum_c, doc, bias, None, dbias[1][idx],
                                    cfg.nhc, cfg.nhc, window // dil, dil, f"dilated{dil}_bwd")
        dcq.append(a)
        dck.append(b)
        dcv.append(c)
        dbias_c.append(d)
    dproj, g_aq, g_ak, g_cq, g_ck = _qk_prep_bwd(
        sv["proj"], p["gains"], [[daq], [dak], [dav], [dbq], [dbk], [dbv], dcq, dck, dcv], cfg, "qk_prep_bwd")
    anchor = on_grad(_matmul(dproj, sv["h1"], "tn", BF16, "in_proj_dw", tm=768, tn=2048, tk=512))
    dh1 = _matmul(dproj, w.get("w_in_t"), "nn", F32, "in_proj_dx", tm=1024, tn=2048, tk=768)
    dx, g_attn_norm = _rmsnorm_bwd(sv["x"], p["attn_norm"] + anchor, dh1, dxm, "attn_norm_bwd")

    def fold(g):
        return jnp.sum(g.reshape(-1, HEAD_DIM), axis=0)

    small = dict(attn_norm=g_attn_norm[0], a_q_gain=fold(g_aq), a_k_gain=fold(g_ak), a_sinks=g_sinks,
                 c_q_gain=fold(g_cq), c_k_gain=fold(g_ck), mix_out_gain=g_mix_gain[0], ffn_norm=g_ffn_norm[0],
                 conv_w=jnp.concatenate([dwg, dwu], axis=1), conv_b=jnp.concatenate([dbg, dbu], axis=1)[0])
    return dx, small, (dbias_a, dbias_c)


_SMALL = ("attn_norm", "a_q_gain", "a_k_gain", "a_sinks", "c_q_gain", "c_k_gain", "rel_bias_table", "mix_out_gain",
          "ffn_norm", "conv_w", "conv_b")


def _pack(arrays):
    flat = jnp.concatenate([a.reshape(-1).astype(F32) for a in arrays])
    pad = (-flat.shape[0]) % (8 * LANES)
    return jnp.pad(flat, (0, pad)).reshape(-1, LANES)


def _unpack(flat, shapes):
    flat = flat.reshape(-1)
    out, pos = [], 0
    for sh in shapes:
        n = int(np.prod(sh))
        out.append(flat[pos:pos + n].reshape(sh))
        pos += n
    return out


def kernel(x, attn_norm, w_in, a_q_gain, a_k_gain, a_sinks, c_q_gain, c_k_gain, rel_bias_table, mix_out_gain, w_out, ffn_norm, w_up, conv_w, conv_b, w_down, loss_target, m_attn_norm, m_w_in, m_a_q_gain, m_a_k_gain, m_a_sinks, m_c_q_gain, m_c_k_gain, m_rel_bias_table, m_mix_out_gain, m_w_out, m_ffn_norm, m_w_up, m_conv_w, m_conv_b, m_w_down, v_attn_norm, v_w_in, v_a_q_gain, v_a_k_gain, v_a_sinks, v_c_q_gain, v_c_k_gain, v_rel_bias_table, v_mix_out_gain, v_w_out, v_ffn_norm, v_w_up, v_conv_w, v_conv_b, v_w_down):
    depth, d = attn_norm.shape
    f = w_down.shape[1] * N_CHIPS
    cfg = _Cfg(d, f)
    chip = 2 * lax.axis_index("x") + lax.axis_index("y")

    cw_cols = conv_w.shape[2]
    cw_flat = conv_w.reshape(-1)
    cw_rows = -(-cw_flat.shape[0] // (16 * LANES)) * 16
    cw_pad = jnp.pad(cw_flat, (0, cw_rows * LANES - cw_flat.shape[0])).reshape(cw_rows, LANES)

    table_a, table_c = rel_bias_table[:, :cfg.nha], rel_bias_table[:, cfg.nha:]
    bias_a = _block_bias(table_a, 1)
    bias_c = [_block_bias(table_c, dil) for _, dil in DILATED_PAIRS]

    layers, anchor = [], 0.0
    for l in range(depth):
        shards = [w_in[l].T.astype(BF16), w_out[l].astype(BF16), w_up[l].T.astype(BF16), w_down[l].astype(BF16)]
        names = ["w_in_t", "w_out", "w_up_t", "w_down"]
        if l == 0:
            todo = [([cw_pad, shards[0]], ["conv_w", names[0]])] + [([s], [n]) for s, n in zip(shards[1:], names[1:])]
        else:
            todo = [(shards, names)]
        groups = []
        for k, (srcs, group_names) in enumerate(todo):
            lands = [_own_slot((N_CHIPS,) + s.shape, s.dtype, s, chip) for s in srcs]
            handle, token = _split_start([], lands, _gather_plan, 3 * len(lands), f"gather_start_{l}_{k}")
            anchor = anchor + token[0, 0]
            groups.append((handle, group_names, f"gather_wait_{l}_{k}"))
        layers.append(dict(
            attn_norm=attn_norm[l].reshape(1, d), ffn_norm=ffn_norm[l].reshape(1, d),
            mix_gain=mix_out_gain[l].reshape(1, d),
            gains=(_tile_gain(a_q_gain[l], cfg.nha), _tile_gain(a_k_gain[l], cfg.nkva),
                   _tile_gain(c_q_gain[l], cfg.nhc), _tile_gain(c_k_gain[l], cfg.nhc)),
            sinks=a_sinks[l], bias_a=bias_a, bias_c=bias_c, conv_b=conv_b[l].reshape(1, 2 * f),
            weights=_Gathered(groups)))
    cw_all = layers[0]["weights"].get("conv_w", layers[0]["attn_norm"] + anchor)
    cw_all = cw_all.reshape(N_CHIPS, -1)[:, :cw_flat.shape[0]].reshape(N_CHIPS, depth, CONV_WIDTH, cw_cols)
    conv_w_full = jnp.transpose(cw_all, (1, 2, 0, 3)).reshape(depth, CONV_WIDTH, N_CHIPS * cw_cols)
    for l in range(depth):
        layers[l]["conv_w"] = conv_w_full[l]

    act = x[0]
    saved = []
    for l in range(depth):
        act, sv = _layer_fwd(act, layers[l], cfg)
        saved.append(sv)
    dact, loss_blk = _loss_head(act, loss_target[0], "loss_head")
    loss = lax.psum(loss_blk[0, 0], ("x", "y", "c"))

    core = lax.axis_index("c")

    def start_scatter(grads, name):
        srcs = [g.reshape(N_CHIPS, -1, g.shape[-1]) for g in grads]
        lands = []
        for g in srcs:
            half = g.shape[1] // 2
            own = lax.dynamic_slice(g, (chip, core * half, 0), (1, half, g.shape[2]))[0]
            lands.append(_own_slot((N_DEVICES, half, g.shape[2]), g.dtype, own, 2 * chip + core))
        return _split_start(srcs, lands, _scatter_plan, (N_DEVICES - 1) * len(srcs), name)

    def finish_scatter(l, handles, after):
        parts = [pt for k, h in enumerate(handles) for pt in _split_wait(h, after, f"scatter_wait_{l}_{k}")][::-1]
        halves = [_sum_parts(pt, f"sum_grads_{t}") for t, pt in enumerate(parts)]
        lands = [_own_slot((2,) + h.shape, h.dtype, h, core) for h in halves]
        return _split_start(halves, lands, _swap_plan, len(halves), f"swap_start_{l}")[0]

    dbias = (jnp.zeros_like(bias_a), [jnp.zeros_like(b) for b in bias_c])
    small_grads = [None] * depth
    swaps = [None] * depth
    pending = None
    for l in reversed(range(depth)):
        made = []

        def on_grad(g, l=l, made=made):
            if l:
                made.append(g)
                return 0.0
            handle, token = start_scatter([g], f"scatter_start_0_{len(made)}")
            made.append(handle)
            return token[0, 0]

        dact, small_grads[l], dbias = _layer_bwd(dact, saved[l], layers[l], dbias, cfg, on_grad)
        if pending is not None:
            swaps[l + 1] = finish_scatter(l + 1, pending, dact)
        if l:
            handle, token = start_scatter(made, f"scatter_start_{l}")
            pending = [handle]
            layers[l - 1]["conv_b"] = layers[l - 1]["conv_b"] + token[0, 0]
        else:
            pending = made
    grad_x = dact[None]

    tabs = _bias_table_grad([dbias[0]] + dbias[1], [jnp.asarray(_bucket_ids(1))]
                            + [jnp.asarray(_bucket_ids(dil)) for _, dil in DILATED_PAIRS], "bias_table_grad")
    g_table_a = tabs[0][:, :N_BUCKETS].T
    g_table_c = (tabs[1] + tabs[2] + tabs[3])[:, :N_BUCKETS].T
    g_table = jnp.concatenate([g_table_a, g_table_c], axis=1)
    small_local = {k: jnp.stack([small_grads[l][k] for l in range(depth)]) for k in _SMALL if k != "rel_bias_table"}
    small_local["rel_bias_table"] = g_table
    shapes = [small_local[k].shape for k in _SMALL]
    reduced = dict(zip(_SMALL, _unpack(_allreduce_small(_pack([small_local[k] for k in _SMALL]), "allreduce_small"),
                                       shapes)))
    reduced["conv_w"] = lax.dynamic_slice_in_dim(reduced["conv_w"], chip * cw_cols, cw_cols, axis=2)

    given = dict(attn_norm=attn_norm, a_q_gain=a_q_gain, a_k_gain=a_k_gain, a_sinks=a_sinks, c_q_gain=c_q_gain,
                 c_k_gain=c_k_gain, rel_bias_table=rel_bias_table, mix_out_gain=mix_out_gain, ffn_norm=ffn_norm,
                 conv_w=conv_w, conv_b=conv_b)
    moms = dict(attn_norm=(m_attn_norm, v_attn_norm), a_q_gain=(m_a_q_gain, v_a_q_gain),
                a_k_gain=(m_a_k_gain, v_a_k_gain), a_sinks=(m_a_sinks, v_a_sinks), c_q_gain=(m_c_q_gain, v_c_q_gain),
                c_k_gain=(m_c_k_gain, v_c_k_gain), rel_bias_table=(m_rel_bias_table, v_rel_bias_table),
                mix_out_gain=(m_mix_out_gain, v_mix_out_gain), ffn_norm=(m_ffn_norm, v_ffn_norm),
                conv_w=(m_conv_w, v_conv_w), conv_b=(m_conv_b, v_conv_b))
    sshapes = [given[k].shape for k in _SMALL]
    s_delta, s_m, s_v = _adamw(_pack([given[k] for k in _SMALL]), _pack([reduced[k] for k in _SMALL]),
                               _pack([moms[k][0] for k in _SMALL]), _pack([moms[k][1] for k in _SMALL]), "adamw_small")
    grads = dict(reduced)
    deltas = dict(zip(_SMALL, _unpack(s_delta, sshapes)))
    new_m = dict(zip(_SMALL, _unpack(s_m, sshapes)))
    new_v = dict(zip(_SMALL, _unpack(s_v, sshapes)))

    big_given = dict(w_in=(w_in, m_w_in, v_w_in, True), w_out=(w_out, m_w_out, v_w_out, False),
                     w_up=(w_up, m_w_up, v_w_up, True), w_down=(w_down, m_w_down, v_w_down, False))
    names = ("w_in", "w_out", "w_up", "w_down")
    bufs = {name: None for name in names}
    after = s_delta
    for l in reversed(range(depth)):
        if l == 0:
            swaps[0] = finish_scatter(0, pending, after)
        layer_grads = [g.reshape(-1, g.shape[-1]) for g in _split_wait(swaps[l], after, f"swap_wait_{l}")]
        for t, name in enumerate(names):
            wt, mt, vt, transposed = big_given[name]
            g = layer_grads[t].T if transposed else layer_grads[t]
            bufs[name] = _adamw_layer(l, wt, g, mt, vt, bufs[name], f"adamw_{name}_{l}")
            after = bufs[name][1]
    for name in names:
        grads[name], deltas[name], new_m[name], new_v[name] = bufs[name]

    order = ("attn_norm", "w_in", "a_q_gain", "a_k_gain", "a_sinks", "c_q_gain", "c_k_gain", "rel_bias_table",
             "mix_out_gain", "w_out", "ffn_norm", "w_up", "conv_w", "conv_b", "w_down")
    return (loss, grad_x, *[grads[k] for k in order], *[deltas[k] for k in order], *[new_m[k] for k in order],
            *[new_v[k] for k in order])
```

```python
import numpy as np
import jax
import jax.numpy as jnp
from jax import lax
from jax.experimental import pallas as pl
from jax.experimental.pallas import tpu as pltpu

F32 = jnp.float32
BF16 = jnp.bfloat16
MESH = pl.DeviceIdType.MESH

HEAD_DIM = 64
BLOCK = 128
LANES = 128
EPS = 1e-6
NEG_INF = -1e30
WINDOW_A = 128
DILATED_PAIRS = ((128, 1), (512, 4), (2048, 16))
N_BUCKETS = 32
T5_MAX_DIST = 2048
CONV_WIDTH = 3
ADAM_LR = 0.001
ADAM_B1 = 0.9
ADAM_B2 = 0.999
ADAM_EPS = 1e-08
ADAM_WD = 0.01
ADAM_STEP = 10
N_CHIPS = 4
N_DEVICES = 8
VMEM_LIMIT_BYTES = 48 * 1024 * 1024
QK_SCALE = HEAD_DIM ** -0.5


def _params(sem=None):
    return pltpu.CompilerParams(dimension_semantics=sem, vmem_limit_bytes=VMEM_LIMIT_BYTES)


def _div_tile(n, cap, mult):
    best = None
    for t in range(mult, min(n, cap) + 1, mult):
        if n % t == 0:
            best = t
    return n if best is None else best


def _dot(a, b):
    return lax.dot_general(a, b, (((1,), (0,)), ((), ())), preferred_element_type=F32)


def _dot_nt(a, b):
    return lax.dot_general(a, b, (((1,), (1,)), ((), ())), preferred_element_type=F32)


def _dot_tn(a, b):
    return lax.dot_general(a, b, (((0,), (0,)), ((), ())), preferred_element_type=F32)


def _split_dot(x, m):
    hi = x.astype(BF16)
    lo = (x - hi.astype(F32)).astype(BF16)
    return _dot(hi, m) + _dot(lo, m)


class _Cfg:
    def __init__(self, d_model, d_ff):
        nh = d_model // HEAD_DIM
        self.d = d_model
        self.f = d_ff
        self.nha = nh // 4
        self.nkva = self.nha // 4
        self.nhb = nh // 4
        self.nhc = nh // 2
        self.a_q = self.nha * HEAD_DIM
        self.a_kv = self.nkva * HEAD_DIM
        self.b_w = self.nhb * HEAD_DIM
        self.c_w = self.nhc * HEAD_DIM
        sizes = [self.a_q, self.a_kv, self.a_kv, self.b_w, self.b_w, self.b_w, self.c_w, self.c_w, self.c_w]
        starts = [0] + [int(s) for s in np.cumsum(sizes)[:-1]]
        self.sections = list(zip(starts, sizes))
        self.in_width = int(sum(sizes))
        assert all(s % LANES == 0 for s in sizes)


def _matmul(a, b, mode, out_dtype, name, tm=512, tn=512, tk=512, residual=None):
    if mode == "tn":
        kdim, m = a.shape
    else:
        m, kdim = a.shape
    n = b.shape[0] if mode == "nt" else b.shape[1]
    tm, tn, tk = _div_tile(m, tm, LANES), _div_tile(n, tn, LANES), _div_tile(kdim, tk, LANES)
    nk = kdim // tk
    if mode == "tn":
        a_spec = pl.BlockSpec((tk, tm), lambda i, j, k: (k, i))
    else:
        a_spec = pl.BlockSpec((tm, tk), lambda i, j, k: (i, k))
    if mode == "nt":
        b_spec = pl.BlockSpec((tn, tk), lambda i, j, k: (j, k))
    else:
        b_spec = pl.BlockSpec((tk, tn), lambda i, j, k: (k, j))
    dot = {"nn": _dot, "nt": _dot_nt, "tn": _dot_tn}[mode]
    o_spec = pl.BlockSpec((tm, tn), lambda i, j, k: (i, j))
    in_specs = [a_spec, b_spec]
    args = [a, b]
    if residual is not None:
        in_specs.append(o_spec)
        args.append(residual)

    def body(*refs):
        if residual is None:
            a_ref, b_ref, o_ref, acc = refs
        else:
            a_ref, b_ref, r_ref, o_ref, acc = refs
        k = pl.program_id(2)

        @pl.when(k == 0)
        def _():
            acc[...] = jnp.zeros_like(acc)

        acc[...] += dot(a_ref[...].astype(BF16), b_ref[...].astype(BF16))

        @pl.when(k == nk - 1)
        def _():
            r = acc[...]
            if residual is not None:
                r = r + r_ref[...]
            o_ref[...] = r.astype(out_dtype)

    return pl.pallas_call(
        body, name=name, grid=(m // tm, n // tn, nk), in_specs=in_specs, out_specs=o_spec,
        out_shape=jax.ShapeDtypeStruct((m, n), out_dtype), scratch_shapes=[pltpu.VMEM((tm, tn), F32)],
        compiler_params=_params(("parallel", "parallel", "arbitrary")),
    )(*args)


def _rmsnorm_fwd(x, g, name):
    s, d = x.shape
    ts = _div_tile(s, 256, 8)

    def body(x_ref, g_ref, o_ref):
        xv = x_ref[...]
        r = lax.rsqrt(jnp.mean(xv * xv, axis=-1, keepdims=True) + EPS)
        o_ref[...] = (xv * r * g_ref[...]).astype(BF16)

    return pl.pallas_call(
        body, name=name, grid=(s // ts,),
        in_specs=[pl.BlockSpec((ts, d), lambda i: (i, 0)), pl.BlockSpec((1, d), lambda i: (0, 0))],
        out_specs=pl.BlockSpec((ts, d), lambda i: (i, 0)), out_shape=jax.ShapeDtypeStruct((s, d), BF16),
        compiler_params=_params(("parallel",)),
    )(x, g)


def _rmsnorm_bwd(x, g, dh, dres, name):
    s, d = x.shape
    ts = _div_tile(s, 256, 8)

    def body(x_ref, g_ref, dh_ref, dres_ref, dx_ref, dg_ref):
        @pl.when(pl.program_id(0) == 0)
        def _():
            dg_ref[...] = jnp.zeros_like(dg_ref)

        xv = x_ref[...]
        r = lax.rsqrt(jnp.mean(xv * xv, axis=-1, keepdims=True) + EPS)
        xhat = xv * r
        dhv = dh_ref[...]
        dxhat = dhv * g_ref[...]
        dx_ref[...] = dres_ref[...] + r * (dxhat - xhat * jnp.mean(dxhat * xhat, axis=-1, keepdims=True))
        dg_ref[...] += jnp.sum(dhv * xhat, axis=0, keepdims=True)

    row = pl.BlockSpec((ts, d), lambda i: (i, 0))
    vec = pl.BlockSpec((1, d), lambda i: (0, 0))
    return pl.pallas_call(
        body, name=name, grid=(s // ts,), in_specs=[row, vec, row, row], out_specs=[row, vec],
        out_shape=[jax.ShapeDtypeStruct((s, d), F32), jax.ShapeDtypeStruct((1, d), F32)],
        compiler_params=_params(("arbitrary",)),
    )(x, g, dh, dres)


def _head_mean_matrix():
    idx = np.arange(LANES) // HEAD_DIM
    return jnp.asarray((idx[:, None] == idx[None, :]).astype(np.float32) / HEAD_DIM, dtype=BF16)


def _head_mean(y, m128):
    w = y.shape[1]
    parts = [_split_dot(y[:, c * LANES:(c + 1) * LANES], m128) for c in range(w // LANES)]
    return parts[0] if len(parts) == 1 else jnp.concatenate(parts, axis=1)


_NORMED_SECTIONS = (0, 1, 6, 7)
_QUERY_SECTIONS = (0, 3, 6)


def _qk_prep(proj, gains, cfg, name):
    s = proj.shape[0]
    ts = _div_tile(s, 256, 16)
    m128 = _head_mean_matrix()

    def body(p_ref, m_ref, g0, g1, g6, g7, *outs):
        gref = dict(zip(_NORMED_SECTIONS, (g0, g1, g6, g7)))
        for idx, (st, w) in enumerate(cfg.sections):
            xv = p_ref[:, st:st + w]
            if idx in gref:
                r = lax.rsqrt(_head_mean(xv * xv, m_ref[...]) + EPS)
                xv = xv * r * gref[idx][...]
            if idx in _QUERY_SECTIONS:
                xv = xv * QK_SCALE
            outs[idx][...] = xv.astype(BF16)

    in_specs = [pl.BlockSpec((ts, cfg.in_width), lambda i: (i, 0)), pl.BlockSpec((LANES, LANES), lambda i: (0, 0))]
    in_specs += [pl.BlockSpec((1, cfg.sections[k][1]), lambda i: (0, 0)) for k in _NORMED_SECTIONS]
    out_specs = [pl.BlockSpec((ts, w), lambda i: (i, 0)) for _, w in cfg.sections]
    out_shape = [jax.ShapeDtypeStruct((s, w), BF16) for _, w in cfg.sections]
    return pl.pallas_call(
        body, name=name, grid=(s // ts,), in_specs=in_specs, out_specs=out_specs, out_shape=out_shape,
        compiler_params=_params(("parallel",)),
    )(proj, m128, *gains)


def _qk_prep_bwd(proj, gains, grads, cfg, name):
    s = proj.shape[0]
    ts = _div_tile(s, 128, 16)
    m128 = _head_mean_matrix()
    counts = [len(gl) for gl in grads]
    flat = [g for gl in grads for g in gl]

    def body(*refs):
        p_ref, m_ref = refs[0], refs[1]
        gref = dict(zip(_NORMED_SECTIONS, refs[2:6]))
        g_in = refs[6:6 + len(flat)]
        dp_ref = refs[6 + len(flat)]
        dgain = dict(zip(_NORMED_SECTIONS, refs[7 + len(flat):]))

        @pl.when(pl.program_id(0) == 0)
        def _():
            for k in _NORMED_SECTIONS:
                dgain[k][...] = jnp.zeros_like(dgain[k])

        pos = 0
        for idx, (st, w) in enumerate(cfg.sections):
            dy = g_in[pos][...]
            for extra in g_in[pos + 1:pos + counts[idx]]:
                dy = dy + extra[...]
            pos += counts[idx]
            if idx in gref:
                xv = p_ref[:, st:st + w]
                r = lax.rsqrt(_head_mean(xv * xv, m_ref[...]) + EPS)
                xhat = xv * r
                dxhat = dy * gref[idx][...]
                dgain[idx][...] += jnp.sum(dy * xhat, axis=0, keepdims=True)
                dy = r * (dxhat - xhat * _head_mean(dxhat * xhat, m_ref[...]))
            dp_ref[:, st:st + w] = dy.astype(BF16)

    in_specs = [pl.BlockSpec((ts, cfg.in_width), lambda i: (i, 0)), pl.BlockSpec((LANES, LANES), lambda i: (0, 0))]
    in_specs += [pl.BlockSpec((1, cfg.sections[k][1]), lambda i: (0, 0)) for k in _NORMED_SECTIONS]
    for idx, (_, w) in enumerate(cfg.sections):
        in_specs += [pl.BlockSpec((ts, w), lambda i: (i, 0))] * counts[idx]
    out_specs = [pl.BlockSpec((ts, cfg.in_width), lambda i: (i, 0))]
    out_specs += [pl.BlockSpec((1, cfg.sections[k][1]), lambda i: (0, 0)) for k in _NORMED_SECTIONS]
    out_shape = [jax.ShapeDtypeStruct((s, cfg.in_width), BF16)]
    out_shape += [jax.ShapeDtypeStruct((1, cfg.sections[k][1]), F32) for k in _NORMED_SECTIONS]
    return pl.pallas_call(
        body, name=name, grid=(s // ts,), in_specs=in_specs, out_specs=out_specs, out_shape=out_shape,
        compiler_params=_params(("arbitrary",)),
    )(proj, m128, *gains, *flat)


def _band_masks(max_dist):
    row = lax.broadcasted_iota(jnp.int32, (BLOCK, BLOCK), 0)
    col = lax.broadcasted_iota(jnp.int32, (BLOCK, BLOCK), 1)
    return row + BLOCK - col <= max_dist, col <= row


def _dilated_t(a, dil):
    s, w = a.shape
    return _keys_on_lanes(a.reshape(s // dil, dil * w), BLOCK)


def _undilated(at, dil):
    nblk, dw, _ = at.shape
    return jnp.transpose(at, (0, 2, 1)).reshape(nblk * BLOCK * dil, dw // dil)


def _banded_fwd(q, kt, v, bias, sinks, hq, hk, max_dist, dil, name):
    s = q.shape[0]
    wq, wk, sd, grp = hq * HEAD_DIM, hk * HEAD_DIM, s // dil, hq // hk
    nb = sd // BLOCK
    has_sink = sinks is not None

    def body(*refs):
        if has_sink:
            q_ref, ktp_ref, ktc_ref, vp_ref, vc_ref, b_ref, s_ref, o_ref, l_ref = refs
        else:
            q_ref, ktp_ref, ktc_ref, vp_ref, vc_ref, b_ref, o_ref, l_ref = refs
        i = pl.program_id(1)
        mprev, mcur = _band_masks(max_dist)
        mask = jnp.concatenate([jnp.logical_and(mprev, i > 0), mcur], axis=1)
        for h in range(hq):
            sq = slice(h * HEAD_DIM, (h + 1) * HEAD_DIM)
            sk = slice((h // grp) * HEAD_DIM, (h // grp + 1) * HEAD_DIM)
            kt = jnp.concatenate([ktp_ref[sk, :], ktc_ref[sk, :]], axis=1)
            vv = jnp.concatenate([vp_ref[:, sk], vc_ref[:, sk]], axis=0)
            sc = jnp.where(mask, _dot(q_ref[:, sq], kt) + b_ref[h], NEG_INF)
            m = jnp.max(sc, axis=-1, keepdims=True)
            if has_sink:
                m = jnp.maximum(m, s_ref[h])
            p = jnp.exp(sc - m)
            den = jnp.sum(p, axis=-1, keepdims=True)
            if has_sink:
                den = den + jnp.exp(s_ref[h] - m)
            o_ref[:, sq] = _dot(p.astype(BF16), vv) / den
            l_ref[:, h:h + 1] = m + jnp.log(den)

    qspec = pl.BlockSpec((BLOCK, wq), lambda r, i: (i, r))
    kprev = pl.BlockSpec((BLOCK, wk), lambda r, i: (jnp.maximum(i - 1, 0), r))
    kcur = pl.BlockSpec((BLOCK, wk), lambda r, i: (i, r))
    ktprev = pl.BlockSpec((None, wk, BLOCK), lambda r, i: (jnp.maximum(i - 1, 0), r, 0))
    ktcur = pl.BlockSpec((None, wk, BLOCK), lambda r, i: (i, r, 0))
    in_specs = [qspec, ktprev, ktcur, kprev, kcur, pl.BlockSpec((hq, BLOCK, 2 * BLOCK), lambda r, i: (0, 0, 0))]
    v2 = v.reshape(sd, dil * wk)
    args = [q.reshape(sd, dil * wq), kt, kt, v2, v2, bias]
    if has_sink:
        in_specs.append(pl.BlockSpec(memory_space=pltpu.SMEM))
        args.append(sinks)
    out, lse = pl.pallas_call(
        body, name=name, grid=(dil, nb), in_specs=in_specs,
        out_specs=[qspec, pl.BlockSpec((None, BLOCK, hq), lambda r, i: (r, i, 0))],
        out_shape=[jax.ShapeDtypeStruct((sd, dil * wq), F32), jax.ShapeDtypeStruct((dil, sd, hq), F32)],
        compiler_params=_params(("parallel", "parallel")),
    )(*args)
    return out.reshape(s, wq), jnp.transpose(lse, (1, 0, 2)).reshape(s, hq)


def _per_head_dilated(a, dil):
    s, h = a.shape
    return jnp.transpose(a.reshape(s // dil, dil, h), (1, 0, 2))


def _banded_bwd(q, k, kt, v, lse, dsum, do, bias, sinks, dbias_init, hq, hk, max_dist, dil, name):
    s = q.shape[0]
    wq, wk, sd, grp = hq * HEAD_DIM, hk * HEAD_DIM, s // dil, hq // hk
    nb = sd // BLOCK
    has_sink = sinks is not None

    def body(*refs):
        (q_ref, qn_ref, qt_ref, qtn_ref, kp_ref, kc_ref, ktp_ref, ktc_ref, vtp_ref, vtc_ref, l_ref, ln_ref, d_ref,
         dn_ref, do_ref, don_ref, dot_ref, dotn_ref, b_ref, dbi_ref) = refs[:20]
        rest = refs[20:]
        if has_sink:
            s_ref, dq_ref, dkt_ref, dvt_ref, db_ref, ds_ref = rest
        else:
            dq_ref, dkt_ref, dvt_ref, db_ref = rest
        j = pl.program_id(1)

        @pl.when(jnp.logical_and(pl.program_id(0) == 0, j == 0))
        def _():
            db_ref[...] = dbi_ref[...]
            if has_sink:
                ds_ref[...] = jnp.zeros_like(ds_ref)

        mprev_static, mcur = _band_masks(max_dist)
        mask = jnp.concatenate([jnp.logical_and(mprev_static, j > 0), mcur], axis=1)
        mnext = jnp.logical_and(mprev_static, j + 1 < nb)
        dkt_acc = [jnp.zeros((HEAD_DIM, BLOCK), F32) for _ in range(hk)]
        dvt_acc = [jnp.zeros((HEAD_DIM, BLOCK), F32) for _ in range(hk)]
        for h in range(hq):
            g = h // grp
            sq = slice(h * HEAD_DIM, (h + 1) * HEAD_DIM)
            sk = slice(g * HEAD_DIM, (g + 1) * HEAD_DIM)
            kt2 = jnp.concatenate([ktp_ref[sk, :], ktc_ref[sk, :]], axis=1)
            vt2 = jnp.concatenate([vtp_ref[sk, :], vtc_ref[sk, :]], axis=1)
            k2 = jnp.concatenate([kp_ref[:, sk], kc_ref[:, sk]], axis=0)
            lcol = l_ref[:, h:h + 1]
            dcol = d_ref[:, h:h + 1]
            sc = _dot(q_ref[:, sq], kt2) + b_ref[h]
            p = jnp.where(mask, jnp.exp(sc - lcol), 0.0)
            ds = p * (_dot(do_ref[:, sq], vt2) - dcol)
            dsb = ds.astype(BF16)
            dq_ref[:, sq] = _dot(dsb, k2) * QK_SCALE
            db_ref[h] += ds
            if has_sink:
                psink = jnp.exp(s_ref[h] - lcol)
                tot = jnp.sum(psink * dcol, axis=0, keepdims=True)
                ds_ref[h:h + 1, :] -= jnp.broadcast_to(tot, (1, LANES))
            lncol = ln_ref[:, h:h + 1]
            dncol = dn_ref[:, h:h + 1]
            sn = _dot(qn_ref[:, sq], ktc_ref[sk, :]) + b_ref[h, :, 0:BLOCK]
            pn = jnp.where(mnext, jnp.exp(sn - lncol), 0.0)
            dsn = pn * (_dot(don_ref[:, sq], vtc_ref[sk, :]) - dncol)
            dkt_acc[g] = dkt_acc[g] + (_dot(qt_ref[sq, :], dsb[:, BLOCK:]) + _dot(qtn_ref[sq, :], dsn.astype(BF16)))
            dvt_acc[g] = dvt_acc[g] + (_dot(dot_ref[sq, :], p[:, BLOCK:].astype(BF16))
                                       + _dot(dotn_ref[sq, :], pn.astype(BF16)))
        for g in range(hk):
            sk = slice(g * HEAD_DIM, (g + 1) * HEAD_DIM)
            dkt_ref[sk, :] = dkt_acc[g]
            dvt_ref[sk, :] = dvt_acc[g]

    qcur = pl.BlockSpec((BLOCK, wq), lambda r, j: (j, r))
    qnext = pl.BlockSpec((BLOCK, wq), lambda r, j: (jnp.minimum(j + 1, nb - 1), r))
    qtcur = pl.BlockSpec((None, wq, BLOCK), lambda r, j: (j, r, 0))
    qtnext = pl.BlockSpec((None, wq, BLOCK), lambda r, j: (jnp.minimum(j + 1, nb - 1), r, 0))
    kprev = pl.BlockSpec((BLOCK, wk), lambda r, j: (jnp.maximum(j - 1, 0), r))
    kcur = pl.BlockSpec((BLOCK, wk), lambda r, j: (j, r))
    ktprev = pl.BlockSpec((None, wk, BLOCK), lambda r, j: (jnp.maximum(j - 1, 0), r, 0))
    ktcur = pl.BlockSpec((None, wk, BLOCK), lambda r, j: (j, r, 0))
    bspec = pl.BlockSpec((hq, BLOCK, 2 * BLOCK), lambda r, j: (0, 0, 0))
    hcur = pl.BlockSpec((None, BLOCK, hq), lambda r, j: (r, j, 0))
    hnext = pl.BlockSpec((None, BLOCK, hq), lambda r, j: (r, jnp.minimum(j + 1, nb - 1), 0))
    dob = do.astype(BF16)
    q2, k2, do2 = q.reshape(sd, dil * wq), k.reshape(sd, dil * wk), dob.reshape(sd, dil * wq)
    l3, d3 = _per_head_dilated(lse, dil), _per_head_dilated(dsum, dil)
    qt, vt, dot = _dilated_t(q, dil), _dilated_t(v, dil), _dilated_t(dob, dil)
    in_specs = [qcur, qnext, qtcur, qtnext, kprev, kcur, ktprev, ktcur, ktprev, ktcur, hcur, hnext, hcur, hnext,
                qcur, qnext, qtcur, qtnext, bspec, bspec]
    args = [q2, q2, qt, qt, k2, k2, kt, kt, vt, vt, l3, l3, d3, d3, do2, do2, dot, dot, bias, dbias_init]
    out_specs = [qcur, ktcur, ktcur, bspec]
    out_shape = [jax.ShapeDtypeStruct((sd, dil * wq), F32), jax.ShapeDtypeStruct((nb, dil * wk, BLOCK), F32),
                 jax.ShapeDtypeStruct((nb, dil * wk, BLOCK), F32), jax.ShapeDtypeStruct((hq, BLOCK, 2 * BLOCK), F32)]
    if has_sink:
        in_specs.append(pl.BlockSpec(memory_space=pltpu.SMEM))
        args.append(sinks)
        out_specs.append(pl.BlockSpec((hq, LANES), lambda r, j: (0, 0)))
        out_shape.append(jax.ShapeDtypeStruct((hq, LANES), F32))
    res = pl.pallas_call(
        body, name=name, grid=(dil, nb), in_specs=in_specs, out_specs=out_specs, out_shape=out_shape,
        compiler_params=_params(("arbitrary", "arbitrary")),
    )(*args)
    dq, dk, dv, dbias = res[0].reshape(s, wq), _undilated(res[1], dil), _undilated(res[2], dil), res[3]
    return dq, dk, dv, dbias, (res[4][:, 0] if has_sink else None)


def _neg_softplus(z):
    return -(jnp.maximum(z, 0.0) + jnp.log(1.0 + jnp.exp(-jnp.abs(z))))


SB_CHUNK = 256
HEADS_PER_PAIR = LANES // HEAD_DIM


def _tri(kind):
    row = lax.broadcasted_iota(jnp.int32, (SB_CHUNK, SB_CHUNK), 0)
    col = lax.broadcasted_iota(jnp.int32, (SB_CHUNK, SB_CHUNK), 1)
    return {"ge": row >= col, "lt": row < col, "le": row <= col}[kind].astype(BF16)


def _keys_on_lanes(a, rows):
    s, w = a.shape
    return jnp.transpose(a.reshape(s // rows, rows, w), (0, 2, 1))


def _sb_mask(i, jj):
    row = lax.broadcasted_iota(jnp.int32, (BLOCK, SB_CHUNK), 0)
    col = lax.broadcasted_iota(jnp.int32, (BLOCK, SB_CHUNK), 1)
    return col < row + (i * BLOCK - jj * SB_CHUNK)


def _sb_trips(i):
    return (i * BLOCK) // (2 * SB_CHUNK) + 1


def _sb_rows(jj, n):
    return pl.ds(pl.multiple_of(jj * SB_CHUNK, SB_CHUNK), n * SB_CHUNK)


def _sb_fwd(q, kt, v, name):
    s, w = q.shape
    npair, nb, nc = w // LANES, s // BLOCK, s // SB_CHUNK

    def body(q_ref, kt_ref, v_ref, o_ref, t_ref):
        i = pl.program_id(1)
        lincl = _tri("ge")
        heads = [slice(hh * HEAD_DIM, (hh + 1) * HEAD_DIM) for hh in range(HEADS_PER_PAIR)]
        qs = [q_ref[:, sl] for sl in heads]

        def trip(t, carry, masked):
            lo, hi = 2 * t, 2 * t + 1
            mlo, mhi = (_sb_mask(i, lo), _sb_mask(i, hi)) if masked else (None, None)

            def keep(m, val):
                return val if m is None else jnp.where(m, val, 0.0)

            new = []
            for hh, sl in enumerate(heads):
                o_acc, rem = carry[hh]
                zhi = _dot(qs[hh], kt_ref[hi, sl, :])
                zlo = _dot(qs[hh], kt_ref[lo, sl, :])
                lrhi = keep(mhi, _neg_softplus(zhi))
                lrlo = keep(mlo, _neg_softplus(zlo))
                tothi = jnp.sum(lrhi, axis=-1, keepdims=True)
                ahi = keep(mhi, jnp.exp(zhi + (rem + _split_dot(lrhi, lincl))))
                alo = keep(mlo, jnp.exp(zlo + (rem + tothi + _split_dot(lrlo, lincl))))
                a = jnp.concatenate([alo, ahi], axis=1).astype(BF16)
                new.append((o_acc + _dot(a, v_ref[_sb_rows(lo, 2), sl]),
                            rem + tothi + jnp.sum(lrlo, axis=-1, keepdims=True)))
            return tuple(new)

        init = tuple((jnp.zeros((BLOCK, HEAD_DIM), F32), jnp.zeros((BLOCK, 1), F32)) for _ in heads)
        trips = _sb_trips(i)
        carry = trip(trips - 1, init, True)
        carry = lax.fori_loop(0, trips - 1, lambda t, cr: trip(trips - 2 - t, cr, False), carry)
        for hh, sl in enumerate(heads):
            o_ref[:, sl] = carry[hh][0]
            t_ref[:, sl] = jnp.broadcast_to(carry[hh][1], (BLOCK, HEAD_DIM))

    qspec = pl.BlockSpec((BLOCK, LANES), lambda p, i: (i, p))
    return pl.pallas_call(
        body, name=name, grid=(npair, nb),
        in_specs=[qspec, pl.BlockSpec((nc, LANES, SB_CHUNK), lambda p, i: (0, p, 0)),
                  pl.BlockSpec((s, LANES), lambda p, i: (0, p))],
        out_specs=[qspec, qspec], out_shape=[jax.ShapeDtypeStruct((s, w), F32)] * 2,
        compiler_params=_params(("parallel", "parallel")),
    )(q, kt, v)


def _sb_bwd(q, k, kt, v, tot, do, name):
    s, w = q.shape
    npair, nb, nc = w // LANES, s // BLOCK, s // SB_CHUNK
    dob = do.astype(BF16)

    def body(q_ref, qt_ref, k_ref, kt_ref, vt_ref, t_ref, do_ref, dot_ref, dq_ref, dkt_ref, dvt_ref):
        i = pl.program_id(1)

        @pl.when(i == 0)
        def _():
            dkt_ref[...] = jnp.zeros_like(dkt_ref)
            dvt_ref[...] = jnp.zeros_like(dvt_ref)

        lbefore = _tri("lt")
        lupto = _tri("le")
        heads = [slice(hh * HEAD_DIM, (hh + 1) * HEAD_DIM) for hh in range(HEADS_PER_PAIR)]
        qs = [q_ref[:, sl] for sl in heads]
        qts = [qt_ref[sl, :] for sl in heads]
        dos = [do_ref[:, sl] for sl in heads]
        dots = [dot_ref[sl, :] for sl in heads]
        totals = [t_ref[:, sl.start:sl.start + 1] for sl in heads]

        def trip(t, carry, masked):
            chunks = (2 * t, 2 * t + 1)
            masks = [_sb_mask(i, jj) if masked else None for jj in chunks]

            def keep(m, val):
                return val if m is None else jnp.where(m, val, 0.0)

            new = []
            for hh, sl in enumerate(heads):
                dq_acc, plr, pg = carry[hh]
                zs = [_dot(qs[hh], kt_ref[jj, sl, :]) for jj in chunks]
                lrs = [keep(m, _neg_softplus(z)) for m, z in zip(masks, zs)]
                lr_sums = [jnp.sum(lr, axis=-1, keepdims=True) for lr in lrs]
                before = [plr, plr + lr_sums[0]]
                avs = [keep(m, jnp.exp(z + (totals[hh] - (b + _split_dot(lr, lbefore)))))
                       for m, z, lr, b in zip(masks, zs, lrs, before)]
                gs = [_dot(dos[hh], vt_ref[jj, sl, :]) * a for jj, a in zip(chunks, avs)]
                g_sums = [jnp.sum(g, axis=-1, keepdims=True) for g in gs]
                upto = [pg, pg + g_sums[0]]
                dzs = [keep(m, g - jnp.exp(z + lr) * (u + _split_dot(g, lupto))).astype(BF16)
                       for m, z, lr, g, u in zip(masks, zs, lrs, gs, upto)]
                for jj, dzb, a in zip(chunks, dzs, avs):
                    dkt_ref[jj, sl, :] += _dot(qts[hh], dzb)
                    dvt_ref[jj, sl, :] += _dot(dots[hh], a.astype(BF16))
                dz2 = jnp.concatenate(dzs, axis=1)
                new.append((dq_acc + _dot(dz2, k_ref[_sb_rows(chunks[0], 2), sl]), plr + lr_sums[0] + lr_sums[1],
                            pg + g_sums[0] + g_sums[1]))
            return tuple(new)

        zero = jnp.zeros((BLOCK, 1), F32)
        init = tuple((jnp.zeros((BLOCK, HEAD_DIM), F32), zero, zero) for _ in heads)
        trips = _sb_trips(i)
        carry = lax.fori_loop(0, trips - 1, lambda t, cr: trip(t, cr, False), init)
        carry = trip(trips - 1, carry, True)
        for hh, sl in enumerate(heads):
            dq_ref[:, sl] = carry[hh][0] * QK_SCALE

    qspec = pl.BlockSpec((BLOCK, LANES), lambda p, i: (i, p))
    qtspec = pl.BlockSpec((None, LANES, BLOCK), lambda p, i: (i, p, 0))
    kspec = pl.BlockSpec((s, LANES), lambda p, i: (0, p))
    ktspec = pl.BlockSpec((nc, LANES, SB_CHUNK), lambda p, i: (0, p, 0))
    dq, dkt, dvt = pl.pallas_call(
        body, name=name, grid=(npair, nb), in_specs=[qspec, qtspec, kspec, ktspec, ktspec, qspec, qspec, qtspec],
        out_specs=[qspec, ktspec, ktspec],
        out_shape=[jax.ShapeDtypeStruct((s, w), F32)] + [jax.ShapeDtypeStruct((nc, w, SB_CHUNK), F32)] * 2,
        compiler_params=_params(("parallel", "arbitrary")),
    )(q, _keys_on_lanes(q, BLOCK), k, kt, _keys_on_lanes(v, SB_CHUNK), tot, dob, _keys_on_lanes(dob, BLOCK))

    def rows_first(t):
        return jnp.transpose(t, (0, 2, 1)).reshape(s, w)

    return dq, rows_first(dkt), rows_first(dvt)


def _group_norm(xv, g):
    r = lax.rsqrt(jnp.mean(xv * xv, axis=-1, keepdims=True) + EPS)
    return xv * r * g


def _head_spread(nheads):
    return jnp.asarray(np.repeat(np.eye(nheads, dtype=np.float32), HEAD_DIM, axis=1), dtype=BF16)


def _mix_fwd(oa, ob, ocs, lses, gain, cfg, name):
    s = oa.shape[0]
    ts = _div_tile(s, 256, 16)
    aq, bw, cw, nhc = cfg.a_q, cfg.b_w, cfg.c_w, cfg.nhc

    def body(oa_ref, ob_ref, c1, c2, c3, l1, l2, l3, sp_ref, g_ref, mix_ref, oc_ref, lse_ref):
        m = jnp.maximum(jnp.maximum(l1[...], l2[...]), l3[...])
        es = [jnp.exp(l[...] - m) for l in (l1, l2, l3)]
        den = es[0] + es[1] + es[2]
        oc = sum(_split_dot(e / den, sp_ref[...]) * c[...] for e, c in zip(es, (c1, c2, c3)))
        oc_ref[...] = oc
        lse_ref[...] = m + jnp.log(den)
        mix_ref[:, 0:aq] = _group_norm(oa_ref[...], g_ref[:, 0:aq]).astype(BF16)
        mix_ref[:, aq:aq + bw] = _group_norm(ob_ref[...], g_ref[:, aq:aq + bw]).astype(BF16)
        mix_ref[:, aq + bw:] = _group_norm(oc, g_ref[:, aq + bw:]).astype(BF16)

    def row(wd):
        return pl.BlockSpec((ts, wd), lambda i: (i, 0))

    return pl.pallas_call(
        body, name=name, grid=(s // ts,),
        in_specs=[row(aq), row(bw)] + [row(cw)] * 3 + [row(nhc)] * 3
        + [pl.BlockSpec((nhc, cw), lambda i: (0, 0)), pl.BlockSpec((1, cfg.d), lambda i: (0, 0))],
        out_specs=[row(cfg.d), row(cw), row(nhc)],
        out_shape=[jax.ShapeDtypeStruct((s, cfg.d), BF16), jax.ShapeDtypeStruct((s, cw), F32),
                   jax.ShapeDtypeStruct((s, nhc), F32)],
        compiler_params=_params(("parallel",)),
    )(oa, ob, *ocs, *lses, _head_spread(nhc), gain)


def _mix_bwd(dmix, oa, ob, oc, gain, cfg, name):
    s = oa.shape[0]
    ts = _div_tile(s, 256, 8)
    aq, bw, cw = cfg.a_q, cfg.b_w, cfg.c_w

    def body(dm_ref, oa_ref, ob_ref, oc_ref, g_ref, fa_ref, fc_ref, da_ref, db_ref, dc_ref, dg_ref, sa_ref, sc_ref):
        @pl.when(pl.program_id(0) == 0)
        def _():
            dg_ref[...] = jnp.zeros_like(dg_ref)

        for x_ref, dx_ref, lo, hi, fold in ((oa_ref, da_ref, 0, aq, (fa_ref, sa_ref)), (ob_ref, db_ref, aq, aq + bw, None),
                                            (oc_ref, dc_ref, aq + bw, aq + bw + cw, (fc_ref, sc_ref))):
            xv = x_ref[...]
            dy = dm_ref[:, lo:hi]
            r = lax.rsqrt(jnp.mean(xv * xv, axis=-1, keepdims=True) + EPS)
            xhat = xv * r
            dxhat = dy * g_ref[:, lo:hi]
            dx = r * (dxhat - xhat * jnp.mean(dxhat * xhat, axis=-1, keepdims=True))
            dx_ref[...] = dx
            dg_ref[:, lo:hi] += jnp.sum(dy * xhat, axis=0, keepdims=True)
            if fold is not None:
                fold[1][...] = _split_dot(dx * xv, fold[0][...])

    def row(wd):
        return pl.BlockSpec((ts, wd), lambda i: (i, 0))

    vec = pl.BlockSpec((1, cfg.d), lambda i: (0, 0))
    return pl.pallas_call(
        body, name=name, grid=(s // ts,),
        in_specs=[row(cfg.d), row(aq), row(bw), row(cw), vec, pl.BlockSpec((aq, cfg.nha), lambda i: (0, 0)),
                  pl.BlockSpec((cw, cfg.nhc), lambda i: (0, 0))],
        out_specs=[row(aq), row(bw), row(cw), vec, row(cfg.nha), row(cfg.nhc)],
        out_shape=[jax.ShapeDtypeStruct((s, aq), F32), jax.ShapeDtypeStruct((s, bw), F32),
                   jax.ShapeDtypeStruct((s, cw), F32), jax.ShapeDtypeStruct((1, cfg.d), F32),
                   jax.ShapeDtypeStruct((s, cfg.nha), F32), jax.ShapeDtypeStruct((s, cfg.nhc), F32)],
        compiler_params=_params(("arbitrary",)),
    )(dmix, oa, ob, oc, gain, _head_spread(cfg.nha).T, _head_spread(cfg.nhc).T)


def _bias_table_grad(dbiases, buckets, name):
    outs = []
    for idx, (db, bk) in enumerate(zip(dbiases, buckets)):
        h = db.shape[0]

        def body(db_ref, bk_ref, o_ref):
            xv = db_ref[0]
            ids = bk_ref[...]
            lane = lax.broadcasted_iota(jnp.int32, (1, LANES), 1)
            acc = jnp.zeros((1, LANES), F32)
            for b in range(N_BUCKETS):
                tot = jnp.sum(jnp.where(ids == b, xv, 0.0), axis=0, keepdims=True)
                tot = jnp.sum(tot, axis=1, keepdims=True)
                acc = jnp.where(lane == b, tot, acc)
            o_ref[0] = acc

        outs.append(pl.pallas_call(
            body, name=f"{name}_{idx}", grid=(h,),
            in_specs=[pl.BlockSpec((1, BLOCK, 2 * BLOCK), lambda i: (i, 0, 0)),
                      pl.BlockSpec((BLOCK, 2 * BLOCK), lambda i: (0, 0))],
            out_specs=pl.BlockSpec((1, 1, LANES), lambda i: (i, 0, 0)),
            out_shape=jax.ShapeDtypeStruct((h, 1, LANES), F32), compiler_params=_params(("parallel",)),
        )(db, bk)[:, 0, :])
    return outs


def _shift_down(u, n, rows):
    return jnp.where(rows >= n, pltpu.roll(u, n, 0), 0.0)


def _shift_up(u, n, rows, s):
    return jnp.where(rows < s - n, pltpu.roll(u, s - n, 0), 0.0)


def _conv(u, w_ref, b_ref, rows):
    return (b_ref[...] + w_ref[0:1, :] * _shift_down(u, 2, rows) + w_ref[1:2, :] * _shift_down(u, 1, rows)
            + w_ref[2:3, :] * u)


def _conv_act_fwd(u, conv_w, conv_b, f, name):
    s = u.shape[0]
    nf = f // LANES

    def body(ug_ref, uu_ref, wg_ref, wu_ref, bg_ref, bu_ref, act_ref):
        rows = lax.broadcasted_iota(jnp.int32, (s, LANES), 0)
        gate = _conv(ug_ref[...], wg_ref, bg_ref, rows)
        up = _conv(uu_ref[...], wu_ref, bu_ref, rows)
        act_ref[...] = (gate * jax.nn.sigmoid(gate) * up).astype(BF16)

    def col(rws, off):
        return pl.BlockSpec((rws, LANES), lambda j: (0, j + off))

    return pl.pallas_call(
        body, name=name, grid=(nf,),
        in_specs=[col(s, 0), col(s, nf), col(CONV_WIDTH, 0), col(CONV_WIDTH, nf), col(1, 0), col(1, nf)],
        out_specs=col(s, 0), out_shape=jax.ShapeDtypeStruct((s, f), BF16), compiler_params=_params(("parallel",)),
    )(u, u, conv_w, conv_w, conv_b, conv_b)


def _conv_act_bwd(u, dact, conv_w, conv_b, f, name):
    s = u.shape[0]
    nf = f // LANES

    def body(ug_ref, uu_ref, da_ref, wg_ref, wu_ref, bg_ref, bu_ref, dug_ref, duu_ref, dwg_ref, dwu_ref, dbg_ref,
             dbu_ref):
        rows = lax.broadcasted_iota(jnp.int32, (s, LANES), 0)
        ug, uu = ug_ref[...], uu_ref[...]
        gate = _conv(ug, wg_ref, bg_ref, rows)
        up = _conv(uu, wu_ref, bu_ref, rows)
        sg = jax.nn.sigmoid(gate)
        da = da_ref[...]
        dgate = da * up * (sg * (1.0 + gate * (1.0 - sg)))
        dup = da * (gate * sg)
        for du, uv, w_ref, du_ref, dw_ref, db_ref in ((dgate, ug, wg_ref, dug_ref, dwg_ref, dbg_ref),
                                                     (dup, uu, wu_ref, duu_ref, dwu_ref, dbu_ref)):
            du_ref[...] = (w_ref[2:3, :] * du + w_ref[1:2, :] * _shift_up(du, 1, rows, s)
                           + w_ref[0:1, :] * _shift_up(du, 2, rows, s)).astype(BF16)
            dw_ref[0:1, :] = jnp.sum(du * _shift_down(uv, 2, rows), axis=0, keepdims=True)
            dw_ref[1:2, :] = jnp.sum(du * _shift_down(uv, 1, rows), axis=0, keepdims=True)
            dw_ref[2:3, :] = jnp.sum(du * uv, axis=0, keepdims=True)
            db_ref[...] = jnp.sum(du, axis=0, keepdims=True)

    def col(rws, off):
        return pl.BlockSpec((rws, LANES), lambda j: (0, j + off))

    return pl.pallas_call(
        body, name=name, grid=(nf,),
        in_specs=[col(s, 0), col(s, nf), col(s, 0), col(CONV_WIDTH, 0), col(CONV_WIDTH, nf), col(1, 0), col(1, nf)],
        out_specs=[col(s, 0), col(s, 0), col(CONV_WIDTH, 0), col(CONV_WIDTH, 0), col(1, 0), col(1, 0)],
        out_shape=[jax.ShapeDtypeStruct((s, f), BF16)] * 2 + [jax.ShapeDtypeStruct((CONV_WIDTH, f), F32)] * 2
        + [jax.ShapeDtypeStruct((1, f), F32)] * 2,
        compiler_params=_params(("parallel",)),
    )(u, u, dact, conv_w, conv_w, conv_b, conv_b)


def _loss_head(y, target, name):
    s, d = y.shape
    ts = _div_tile(s, 256, 8)

    def body(y_ref, t_ref, dy_ref, l_ref):
        @pl.when(pl.program_id(0) == 0)
        def _():
            l_ref[...] = jnp.zeros_like(l_ref)

        err = y_ref[...] - t_ref[...]
        dy_ref[...] = err * (1.0 / d)
        tot = jnp.sum(jnp.sum(err * err, axis=0, keepdims=True), axis=1, keepdims=True) * (0.5 / d)
        l_ref[...] += jnp.broadcast_to(tot, l_ref.shape)

    row = pl.BlockSpec((ts, d), lambda i: (i, 0))
    return pl.pallas_call(
        body, name=name, grid=(s // ts,), in_specs=[row, row],
        out_specs=[row, pl.BlockSpec((8, LANES), lambda i: (0, 0))],
        out_shape=[jax.ShapeDtypeStruct((s, d), F32), jax.ShapeDtypeStruct((8, LANES), F32)],
        compiler_params=_params(("arbitrary",)),
    )(y, target)


def _adamw(w, g, m, v, name):
    r, c = w.shape
    tr = _div_tile(r, max(8, (1 << 18) // c // 8 * 8), 8)
    c1 = 1.0 - ADAM_B1 ** ADAM_STEP
    c2 = 1.0 - ADAM_B2 ** ADAM_STEP

    def body(w_ref, g_ref, m_ref, v_ref, d_ref, nm_ref, nv_ref):
        gv = g_ref[...]
        nm = ADAM_B1 * m_ref[...] + (1.0 - ADAM_B1) * gv
        nv = ADAM_B2 * v_ref[...] + (1.0 - ADAM_B2) * (gv * gv)
        d_ref[...] = -ADAM_LR * ((nm / c1) / (jnp.sqrt(nv / c2) + ADAM_EPS) + ADAM_WD * w_ref[...])
        nm_ref[...] = nm
        nv_ref[...] = nv

    spec = pl.BlockSpec((tr, c), lambda i: (i, 0))
    return pl.pallas_call(
        body, name=name, grid=(r // tr,), in_specs=[spec] * 4, out_specs=[spec] * 3,
        out_shape=[jax.ShapeDtypeStruct((r, c), F32)] * 3, compiler_params=_params(("parallel",)),
    )(w, g, m, v)


def _adamw_layer(layer, w, g, m, v, bufs, name):
    depth, r, c = w.shape
    tr = _div_tile(r, max(8, (1 << 17) // c // 8 * 8), 8)
    c1 = 1.0 - ADAM_B1 ** ADAM_STEP
    c2 = 1.0 - ADAM_B2 ** ADAM_STEP

    def body(*refs):
        w_ref, g_ref, m_ref, v_ref = refs[:4]
        go_ref, d_ref, nm_ref, nv_ref = refs[-4:]
        gv = g_ref[...]
        nm = ADAM_B1 * m_ref[...] + (1.0 - ADAM_B1) * gv
        nv = ADAM_B2 * v_ref[...] + (1.0 - ADAM_B2) * (gv * gv)
        d_ref[...] = -ADAM_LR * ((nm / c1) / (jnp.sqrt(nv / c2) + ADAM_EPS) + ADAM_WD * w_ref[...])
        nm_ref[...] = nm
        nv_ref[...] = nv
        go_ref[...] = gv

    lay = pl.BlockSpec((None, tr, c), lambda i: (layer, i, 0))
    in_specs = [lay, pl.BlockSpec((tr, c), lambda i: (i, 0)), lay, lay]
    args = [w, g, m, v]
    aliases = {}
    if bufs is not None:
        in_specs += [pl.BlockSpec(memory_space=pl.ANY)] * 4
        args += list(bufs)
        aliases = {4 + k: k for k in range(4)}
    return pl.pallas_call(
        body, name=name, grid=(r // tr,), in_specs=in_specs, out_specs=[lay] * 4,
        out_shape=[jax.ShapeDtypeStruct((depth, r, c), F32)] * 4, input_output_aliases=aliases,
        compiler_params=_params(("parallel",)),
    )(*args)


def _mesh_pos():
    return lax.axis_index("x"), lax.axis_index("y"), lax.axis_index("c")


def _flip(v, bit):
    return 1 - v if bit else v


def _sum_parts(parts, name):
    _, r, c = parts.shape
    tr = _div_tile(r, 256, 16)

    def body(p_ref, o_ref):
        acc = p_ref[0].astype(F32)
        for src in range(1, N_DEVICES):
            acc = acc + p_ref[src].astype(F32)
        o_ref[...] = acc

    return pl.pallas_call(
        body, name=name, grid=(r // tr,), in_specs=[pl.BlockSpec((N_DEVICES, tr, c), lambda i: (0, i, 0))],
        out_specs=pl.BlockSpec((tr, c), lambda i: (i, 0)), out_shape=jax.ShapeDtypeStruct((r, c), F32),
        compiler_params=_params(("parallel",)),
    )(parts)


def _split_start(srcs, lands, plan, ncopies, name):
    nbuf = len(srcs) + len(lands)

    def body(*refs):
        bufs = refs[:nbuf]
        send_sem, recv_sem, token = refs[nbuf], refs[nbuf + 1], refs[-1]
        for k, (src, dst, dev) in enumerate(plan(bufs[:len(srcs)], bufs[len(srcs):])):
            pltpu.make_async_remote_copy(src_ref=src, dst_ref=dst, send_sem=send_sem.at[k], recv_sem=recv_sem.at[k],
                                         device_id=dev, device_id_type=MESH).start()
        token[...] = jnp.zeros_like(token)

    hbm = pl.BlockSpec(memory_space=pltpu.HBM)
    sem = pl.BlockSpec(memory_space=pltpu.SEMAPHORE)
    operands = [pltpu.with_memory_space_constraint(a, pltpu.HBM) for a in (*srcs, *lands)]
    outs = pl.pallas_call(
        body, name=name, in_specs=[hbm] * nbuf,
        out_specs=(sem, sem, *[hbm] * nbuf, pl.BlockSpec(memory_space=pltpu.VMEM)),
        out_shape=(pltpu.SemaphoreType.DMA((ncopies,)), pltpu.SemaphoreType.DMA((ncopies,)),
                   *[pltpu.HBM(a.shape, a.dtype) for a in operands], jax.ShapeDtypeStruct((8, LANES), F32)),
        input_output_aliases={i: 2 + i for i in range(nbuf)},
        compiler_params=pltpu.CompilerParams(has_side_effects=pltpu.SideEffectType.DATAFLOW_SIDE_EFFECTING),
    )(*operands)
    handle = dict(send=outs[0], recv=outs[1], bufs=list(outs[2:2 + nbuf]), nsrc=len(srcs), plan=plan)
    return handle, outs[-1]


def _split_wait(handle, after, name):
    nbuf, nsrc, plan = len(handle["bufs"]), handle["nsrc"], handle["plan"]

    def body(*refs):
        bufs = refs[:nbuf]
        send_sem, recv_sem = refs[nbuf], refs[nbuf + 1]
        for k, (src, dst, dev) in enumerate(plan(bufs[:nsrc], bufs[nsrc:])):
            copy = pltpu.make_async_remote_copy(src_ref=src, dst_ref=dst, send_sem=send_sem.at[k],
                                                recv_sem=recv_sem.at[k], device_id=dev, device_id_type=MESH)
            copy.wait_send()
            copy.wait_recv()

    hbm = pl.BlockSpec(memory_space=pltpu.HBM)
    sem = pl.BlockSpec(memory_space=pltpu.SEMAPHORE)
    outs = pl.pallas_call(
        body, name=name, in_specs=[hbm] * nbuf + [sem, sem, pl.BlockSpec(memory_space=pl.ANY)],
        out_specs=[hbm] * nbuf, out_shape=[pltpu.HBM(a.shape, a.dtype) for a in handle["bufs"]],
        input_output_aliases={i: i for i in range(nbuf)},
        compiler_params=pltpu.CompilerParams(has_side_effects=pltpu.SideEffectType.DATAFLOW_SIDE_EFFECTING),
    )(*handle["bufs"], handle["send"], handle["recv"], after)
    return list(outs[nsrc:])


def _own_slot(shape, dtype, block, index):
    return lax.dynamic_update_slice(lax.empty(shape, dtype), block[None], (index,) + (0,) * block.ndim)


def _gather_plan(srcs, lands):
    x, y, c = _mesh_pos()
    return [(land.at[2 * x + y], land.at[2 * x + y], (*chip, c))
            for land in lands for chip in ((1 - x, y), (x, 1 - y), (1 - x, 1 - y))]


def _scatter_plan(srcs, lands):
    x, y, c = _mesh_pos()
    out = []
    for src, land in zip(srcs, lands):
        half = src.shape[1] // 2
        for d in range(1, N_DEVICES):
            p = (_flip(x, d & 4), _flip(y, d & 2), _flip(c, d & 1))
            out.append((src.at[2 * p[0] + p[1], pl.ds(p[2] * half, half), :], land.at[4 * x + 2 * y + c], p))
    return out


def _swap_plan(srcs, lands):
    x, y, c = _mesh_pos()
    return [(src, land.at[c], (x, y, 1 - c)) for src, land in zip(srcs, lands)]


class _Gathered:
    def __init__(self, groups):
        self.groups = groups
        self.ready = {}

    def get(self, name, after=None):
        if name not in self.ready:
            handle, names, wait_name = next(g for g in self.groups if name in g[1])
            for n, full in zip(names, _split_wait(handle, after, wait_name)):
                self.ready[n] = full.reshape(-1, full.shape[-1])
        return self.ready[name]


def _allreduce_small(flat, name):
    r = flat.shape[0]

    def body(x_ref, o_ref, buf, send_sems, recv_sems):
        x, y, c = _mesh_pos()
        me = 4 * x + 2 * y + c
        buf[me] = x_ref[...]
        started = []
        peers = [(_flip(x, d & 4), _flip(y, d & 2), _flip(c, d & 1)) for d in range(1, N_DEVICES)]
        for d, p in enumerate(peers):
            cp = pltpu.make_async_remote_copy(src_ref=x_ref, dst_ref=buf.at[me], send_sem=send_sems.at[d],
                                              recv_sem=recv_sems.at[d], device_id=p, device_id_type=MESH)
            cp.start()
            started.append(cp)
        for d, p in enumerate(peers):
            slot = buf.at[4 * p[0] + 2 * p[1] + p[2]]
            pltpu.make_async_remote_copy(src_ref=slot, dst_ref=slot, send_sem=send_sems.at[d], recv_sem=recv_sems.at[d],
                                         device_id=p, device_id_type=MESH).wait_recv()
        for cp in started:
            cp.wait_send()
        acc = buf[0]
        for src in range(1, N_DEVICES):
            acc = acc + buf[src]
        o_ref[...] = acc

    vm = pl.BlockSpec(memory_space=pltpu.VMEM)
    return pl.pallas_call(
        body, name=name, in_specs=[vm], out_specs=vm, out_shape=jax.ShapeDtypeStruct((r, LANES), F32),
        scratch_shapes=[pltpu.VMEM((N_DEVICES, r, LANES), F32), pltpu.SemaphoreType.DMA((N_DEVICES - 1,)),
                        pltpu.SemaphoreType.DMA((N_DEVICES - 1,))],
        compiler_params=pltpu.CompilerParams(vmem_limit_bytes=VMEM_LIMIT_BYTES),
    )(flat)


def _bucket_ids(dil):
    rel = (np.arange(BLOCK)[:, None] + BLOCK - np.arange(2 * BLOCK)[None, :]) * dil
    max_exact = N_BUCKETS // 2
    d = np.maximum(rel, 0)
    large = max_exact + (np.log(np.maximum(d, 1).astype(np.float32) / max_exact)
                         / np.float32(np.log(T5_MAX_DIST / max_exact)) * (N_BUCKETS - max_exact)).astype(np.int32)
    large = np.minimum(large, N_BUCKETS - 1)
    return np.where(d < max_exact, d, large).astype(np.int32)


def _block_bias(table, dil):
    onehot = (jnp.asarray(_bucket_ids(dil))[:, :, None] == jnp.arange(N_BUCKETS)[None, None, :]).astype(F32)
    return jnp.einsum("ijb,bh->hij", onehot, table.astype(F32), precision=lax.Precision.HIGHEST)


def _tile_gain(g, n):
    return jnp.tile(g.reshape(1, HEAD_DIM), (1, n))


def _layer_fwd(x, p, cfg):
    w = p["weights"]
    h1 = _rmsnorm_fwd(x, p["attn_norm"], "attn_norm_fwd")
    proj = _matmul(h1, w.get("w_in_t", h1), "nt", F32, "in_proj", tm=1024, tn=768, tk=2048)
    aq, ak, av, bq, bk, bv, cq, ck, cv = _qk_prep(proj, p["gains"], cfg, "qk_prep")
    akt = _dilated_t(ak, 1)
    oa, lse_a = _banded_fwd(aq, akt, av, p["bias_a"], p["sinks"], cfg.nha, cfg.nkva, WINDOW_A - 1, 1, "swa_fwd")
    bkt = _keys_on_lanes(bk, SB_CHUNK)
    ob, tot_b = _sb_fwd(bq, bkt, bv, "stickbreak_fwd")
    ocs, lses, ckts = [], [], []
    for (window, dil), bias in zip(DILATED_PAIRS, p["bias_c"]):
        ckts.append(_dilated_t(ck, dil))
        o, l = _banded_fwd(cq, ckts[-1], cv, bias, None, cfg.nhc, cfg.nhc, window // dil, dil, f"dilated{dil}_fwd")
        ocs.append(o)
        lses.append(l)
    mix, oc, lse_c = _mix_fwd(oa, ob, ocs, lses, p["mix_gain"], cfg, "mix_fwd")
    xm = _matmul(mix, w.get("w_out", mix), "nn", F32, "out_proj", tm=1024, tn=512, tk=2048, residual=x)
    h2 = _rmsnorm_fwd(xm, p["ffn_norm"], "ffn_norm_fwd")
    u = _matmul(h2, w.get("w_up_t", h2), "nt", F32, "up_proj", tm=1024, tn=512, tk=2048)
    act = _conv_act_fwd(u, p["conv_w"], p["conv_b"], cfg.f, "conv_act_fwd")
    y = _matmul(act, w.get("w_down", act), "nn", F32, "down_proj", tm=1024, tn=1024, tk=1408, residual=xm)
    saved = dict(x=x, h1=h1, proj=proj, q=(aq, ak, av, bq, bk, bv, cq, ck, cv), oa=oa, lse_a=lse_a, ob=ob,
                 tot_b=tot_b, akt=akt, bkt=bkt, ckts=ckts, oc=oc, lse_c=lse_c, mix=mix, xm=xm, h2=h2, u=u, act=act)
    return y, saved


def _layer_bwd(dy, sv, p, dbias, cfg, on_grad):
    aq, ak, av, bq, bk, bv, cq, ck, cv = sv["q"]
    w = p["weights"]
    anchor = on_grad(_matmul(sv["act"], dy, "tn", BF16, "down_proj_dw", tm=1408, tn=2048, tk=512))
    dact = _matmul(dy, w.get("w_down"), "nt", F32, "down_proj_dx", tm=1024, tn=512, tk=2048)
    dug, duu, dwg, dwu, dbg, dbu = _conv_act_bwd(sv["u"], dact, p["conv_w"], p["conv_b"] + anchor, cfg.f,
                                                 "conv_act_bwd")
    du = jnp.concatenate([dug, duu], axis=1)
    anchor = on_grad(_matmul(du, sv["h2"], "tn", BF16, "up_proj_dw", tm=1408, tn=2048, tk=1024))
    dh2 = _matmul(du, w.get("w_up_t"), "nn", F32, "up_proj_dx", tm=1024, tn=2048, tk=1024)
    dxm, g_ffn_norm = _rmsnorm_bwd(sv["xm"], p["ffn_norm"] + anchor, dh2, dy, "ffn_norm_bwd")
    anchor = on_grad(_matmul(sv["mix"], dxm, "tn", BF16, "out_proj_dw", tm=1024, tn=2048, tk=1024))
    dmix = _matmul(dxm, w.get("w_out"), "nt", F32, "out_proj_dx", tm=1024, tn=512, tk=2048)
    doa, dob, doc, g_mix_gain, dsum_a, dsum_c = _mix_bwd(dmix, sv["oa"], sv["ob"], sv["oc"], p["mix_gain"] + anchor,
                                                         cfg, "mix_bwd")
    daq, dak, dav, dbias_a, g_sinks = _banded_bwd(aq, ak, sv["akt"], av, sv["lse_a"], dsum_a, doa, p["bias_a"],
                                                 p["sinks"], dbias[0], cfg.nha, cfg.nkva, WINDOW_A - 1, 1, "swa_bwd")
    dbq, dbk, dbv = _sb_bwd(bq, bk, sv["bkt"], bv, sv["tot_b"], dob, "stickbreak_bwd")
    dcq, dck, dcv, dbias_c = [], [], [], []
    for idx, ((window, dil), bias) in enumerate(zip(DILATED_PAIRS, p["bias_c"])):
        a, b, c, d, _ = _banded_bwd(cq, ck, sv["ckts"][idx], cv, sv["lse_c"], dsum_c, doc, bias, None, dbias[1][idx],
                                    cfg.nhc, cfg.nhc, window // dil, dil, f"dilated{dil}_bwd")
        dcq.append(a)
        dck.append(b)
        dcv.append(c)
        dbias_c.append(d)
    dproj, g_aq, g_ak, g_cq, g_ck = _qk_prep_bwd(
        sv["proj"], p["gains"], [[daq], [dak], [dav], [dbq], [dbk], [dbv], dcq, dck, dcv], cfg, "qk_prep_bwd")
    anchor = on_grad(_matmul(dproj, sv["h1"], "tn", BF16, "in_proj_dw", tm=768, tn=2048, tk=1024))
    dh1 = _matmul(dproj, w.get("w_in_t"), "nn", F32, "in_proj_dx", tm=1024, tn=2048, tk=768)
    dx, g_attn_norm = _rmsnorm_bwd(sv["x"], p["attn_norm"] + anchor, dh1, dxm, "attn_norm_bwd")

    def fold(g):
        return jnp.sum(g.reshape(-1, HEAD_DIM), axis=0)

    small = dict(attn_norm=g_attn_norm[0], a_q_gain=fold(g_aq), a_k_gain=fold(g_ak), a_sinks=g_sinks,
                 c_q_gain=fold(g_cq), c_k_gain=fold(g_ck), mix_out_gain=g_mix_gain[0], ffn_norm=g_ffn_norm[0],
                 conv_w=jnp.concatenate([dwg, dwu], axis=1), conv_b=jnp.concatenate([dbg, dbu], axis=1)[0])
    return dx, small, (dbias_a, dbias_c)


_SMALL = ("attn_norm", "a_q_gain", "a_k_gain", "a_sinks", "c_q_gain", "c_k_gain", "rel_bias_table", "mix_out_gain",
          "ffn_norm", "conv_w", "conv_b")


def _pack(arrays):
    flat = jnp.concatenate([a.reshape(-1).astype(F32) for a in arrays])
    pad = (-flat.shape[0]) % (8 * LANES)
    return jnp.pad(flat, (0, pad)).reshape(-1, LANES)


def _unpack(flat, shapes):
    flat = flat.reshape(-1)
    out, pos = [], 0
    for sh in shapes:
        n = int(np.prod(sh))
        out.append(flat[pos:pos + n].reshape(sh))
        pos += n
    return out


def kernel(x, attn_norm, w_in, a_q_gain, a_k_gain, a_sinks, c_q_gain, c_k_gain, rel_bias_table, mix_out_gain, w_out, ffn_norm, w_up, conv_w, conv_b, w_down, loss_target, m_attn_norm, m_w_in, m_a_q_gain, m_a_k_gain, m_a_sinks, m_c_q_gain, m_c_k_gain, m_rel_bias_table, m_mix_out_gain, m_w_out, m_ffn_norm, m_w_up, m_conv_w, m_conv_b, m_w_down, v_attn_norm, v_w_in, v_a_q_gain, v_a_k_gain, v_a_sinks, v_c_q_gain, v_c_k_gain, v_rel_bias_table, v_mix_out_gain, v_w_out, v_ffn_norm, v_w_up, v_conv_w, v_conv_b, v_w_down):
    depth, d = attn_norm.shape
    f = w_down.shape[1] * N_CHIPS
    cfg = _Cfg(d, f)
    chip = 2 * lax.axis_index("x") + lax.axis_index("y")

    cw_cols = conv_w.shape[2]
    cw_flat = conv_w.reshape(-1)
    cw_rows = -(-cw_flat.shape[0] // (16 * LANES)) * 16
    cw_pad = jnp.pad(cw_flat, (0, cw_rows * LANES - cw_flat.shape[0])).reshape(cw_rows, LANES)

    table_a, table_c = rel_bias_table[:, :cfg.nha], rel_bias_table[:, cfg.nha:]
    bias_a = _block_bias(table_a, 1)
    bias_c = [_block_bias(table_c, dil) for _, dil in DILATED_PAIRS]

    layers, anchor = [], 0.0
    for l in range(depth):
        shards = [w_in[l].T.astype(BF16), w_out[l].astype(BF16), w_up[l].T.astype(BF16), w_down[l].astype(BF16)]
        names = ["w_in_t", "w_out", "w_up_t", "w_down"]
        if l == 0:
            todo = [([cw_pad, shards[0]], ["conv_w", names[0]])] + [([s], [n]) for s, n in zip(shards[1:], names[1:])]
        else:
            todo = [(shards, names)]
        groups = []
        for k, (srcs, group_names) in enumerate(todo):
            lands = [_own_slot((N_CHIPS,) + s.shape, s.dtype, s, chip) for s in srcs]
            handle, token = _split_start([], lands, _gather_plan, 3 * len(lands), f"gather_start_{l}_{k}")
            anchor = anchor + token[0, 0]
            groups.append((handle, group_names, f"gather_wait_{l}_{k}"))
        layers.append(dict(
            attn_norm=attn_norm[l].reshape(1, d), ffn_norm=ffn_norm[l].reshape(1, d),
            mix_gain=mix_out_gain[l].reshape(1, d),
            gains=(_tile_gain(a_q_gain[l], cfg.nha), _tile_gain(a_k_gain[l], cfg.nkva),
                   _tile_gain(c_q_gain[l], cfg.nhc), _tile_gain(c_k_gain[l], cfg.nhc)),
            sinks=a_sinks[l], bias_a=bias_a, bias_c=bias_c, conv_b=conv_b[l].reshape(1, 2 * f),
            weights=_Gathered(groups)))
    cw_all = layers[0]["weights"].get("conv_w", layers[0]["attn_norm"] + anchor)
    cw_all = cw_all.reshape(N_CHIPS, -1)[:, :cw_flat.shape[0]].reshape(N_CHIPS, depth, CONV_WIDTH, cw_cols)
    conv_w_full = jnp.transpose(cw_all, (1, 2, 0, 3)).reshape(depth, CONV_WIDTH, N_CHIPS * cw_cols)
    for l in range(depth):
        layers[l]["conv_w"] = conv_w_full[l]

    act = x[0]
    saved = []
    for l in range(depth):
        act, sv = _layer_fwd(act, layers[l], cfg)
        saved.append(sv)
    dact, loss_blk = _loss_head(act, loss_target[0], "loss_head")
    loss = lax.psum(loss_blk[0, 0], ("x", "y", "c"))

    core = lax.axis_index("c")

    def start_scatter(grads, name):
        srcs = [g.reshape(N_CHIPS, -1, g.shape[-1]) for g in grads]
        lands = []
        for g in srcs:
            half = g.shape[1] // 2
            own = lax.dynamic_slice(g, (chip, core * half, 0), (1, half, g.shape[2]))[0]
            lands.append(_own_slot((N_DEVICES, half, g.shape[2]), g.dtype, own, 2 * chip + core))
        return _split_start(srcs, lands, _scatter_plan, (N_DEVICES - 1) * len(srcs), name)

    def finish_scatter(l, handles, after):
        parts = [pt for k, h in enumerate(handles) for pt in _split_wait(h, after, f"scatter_wait_{l}_{k}")][::-1]
        halves = [_sum_parts(pt, f"sum_grads_{t}") for t, pt in enumerate(parts)]
        lands = [_own_slot((2,) + h.shape, h.dtype, h, core) for h in halves]
        return _split_start(halves, lands, _swap_plan, len(halves), f"swap_start_{l}")[0]

    dbias = (jnp.zeros_like(bias_a), [jnp.zeros_like(b) for b in bias_c])
    small_grads = [None] * depth
    swaps = [None] * depth
    pending = None
    for l in reversed(range(depth)):
        made = []

        def on_grad(g, l=l, made=made):
            if l:
                made.append(g)
                return 0.0
            handle, token = start_scatter([g], f"scatter_start_0_{len(made)}")
            made.append(handle)
            return token[0, 0]

        dact, small_grads[l], dbias = _layer_bwd(dact, saved[l], layers[l], dbias, cfg, on_grad)
        if pending is not None:
            swaps[l + 1] = finish_scatter(l + 1, pending, dact)
        if l:
            handle, token = start_scatter(made, f"scatter_start_{l}")
            pending = [handle]
            layers[l - 1]["conv_b"] = layers[l - 1]["conv_b"] + token[0, 0]
        else:
            pending = made
    grad_x = dact[None]

    tabs = _bias_table_grad([dbias[0]] + dbias[1], [jnp.asarray(_bucket_ids(1))]
                            + [jnp.asarray(_bucket_ids(dil)) for _, dil in DILATED_PAIRS], "bias_table_grad")
    g_table_a = tabs[0][:, :N_BUCKETS].T
    g_table_c = (tabs[1] + tabs[2] + tabs[3])[:, :N_BUCKETS].T
    g_table = jnp.concatenate([g_table_a, g_table_c], axis=1)
    small_local = {k: jnp.stack([small_grads[l][k] for l in range(depth)]) for k in _SMALL if k != "rel_bias_table"}
    small_local["rel_bias_table"] = g_table
    shapes = [small_local[k].shape for k in _SMALL]
    reduced = dict(zip(_SMALL, _unpack(_allreduce_small(_pack([small_local[k] for k in _SMALL]), "allreduce_small"),
                                       shapes)))
    reduced["conv_w"] = lax.dynamic_slice_in_dim(reduced["conv_w"], chip * cw_cols, cw_cols, axis=2)

    given = dict(attn_norm=attn_norm, a_q_gain=a_q_gain, a_k_gain=a_k_gain, a_sinks=a_sinks, c_q_gain=c_q_gain,
                 c_k_gain=c_k_gain, rel_bias_table=rel_bias_table, mix_out_gain=mix_out_gain, ffn_norm=ffn_norm,
                 conv_w=conv_w, conv_b=conv_b)
    moms = dict(attn_norm=(m_attn_norm, v_attn_norm), a_q_gain=(m_a_q_gain, v_a_q_gain),
                a_k_gain=(m_a_k_gain, v_a_k_gain), a_sinks=(m_a_sinks, v_a_sinks), c_q_gain=(m_c_q_gain, v_c_q_gain),
                c_k_gain=(m_c_k_gain, v_c_k_gain), rel_bias_table=(m_rel_bias_table, v_rel_bias_table),
                mix_out_gain=(m_mix_out_gain, v_mix_out_gain), ffn_norm=(m_ffn_norm, v_ffn_norm),
                conv_w=(m_conv_w, v_conv_w), conv_b=(m_conv_b, v_conv_b))
    sshapes = [given[k].shape for k in _SMALL]
    s_delta, s_m, s_v = _adamw(_pack([given[k] for k in _SMALL]), _pack([reduced[k] for k in _SMALL]),
                               _pack([moms[k][0] for k in _SMALL]), _pack([moms[k][1] for k in _SMALL]), "adamw_small")
    grads = dict(reduced)
    deltas = dict(zip(_SMALL, _unpack(s_delta, sshapes)))
    new_m = dict(zip(_SMALL, _unpack(s_m, sshapes)))
    new_v = dict(zip(_SMALL, _unpack(s_v, sshapes)))

    big_given = dict(w_in=(w_in, m_w_in, v_w_in, True), w_out=(w_out, m_w_out, v_w_out, False),
                     w_up=(w_up, m_w_up, v_w_up, True), w_down=(w_down, m_w_down, v_w_down, False))
    names = ("w_in", "w_out", "w_up", "w_down")
    bufs = {name: None for name in names}
    after = s_delta
    for l in reversed(range(depth)):
        if l == 0:
            swaps[0] = finish_scatter(0, pending, after)
        layer_grads = [g.reshape(-1, g.shape[-1]) for g in _split_wait(swaps[l], after, f"swap_wait_{l}")]
        for t, name in enumerate(names):
            wt, mt, vt, transposed = big_given[name]
            g = layer_grads[t].T if transposed else layer_grads[t]
            bufs[name] = _adamw_layer(l, wt, g, mt, vt, bufs[name], f"adamw_{name}_{l}")
            after = bufs[name][1]
    for name in names:
        grads[name], deltas[name], new_m[name], new_v[name] = bufs[name]

    order = ("attn_norm", "w_in", "a_q_gain", "a_k_gain", "a_sinks", "c_q_gain", "c_k_gain", "rel_bias_table",
             "mix_out_gain", "w_out", "ffn_norm", "w_up", "conv_w", "conv_b", "w_down")
    return (loss, grad_x, *[grads[k] for k in order], *[deltas[k] for k in order], *[new_m[k] for k in order],
            *[new_v[k] for k in order])
```

```python
import numpy as np
import jax
import jax.numpy as jnp
from jax import lax
from jax.experimental import pallas as pl
from jax.experimental.pallas import tpu as pltpu

F32 = jnp.float32
BF16 = jnp.bfloat16
MESH = pl.DeviceIdType.MESH

HEAD_DIM = 64
BLOCK = 128
LANES = 128
EPS = 1e-6
NEG_INF = -1e30
WINDOW_A = 128
DILATED_PAIRS = ((128, 1), (512, 4), (2048, 16))
N_BUCKETS = 32
T5_MAX_DIST = 2048
CONV_WIDTH = 3
ADAM_LR = 0.001
ADAM_B1 = 0.9
ADAM_B2 = 0.999
ADAM_EPS = 1e-08
ADAM_WD = 0.01
ADAM_STEP = 10
N_CHIPS = 4
N_DEVICES = 8
VMEM_LIMIT_BYTES = 48 * 1024 * 1024
QK_SCALE = HEAD_DIM ** -0.5


def _params(sem=None):
    return pltpu.CompilerParams(dimension_semantics=sem, vmem_limit_bytes=VMEM_LIMIT_BYTES)


def _div_tile(n, cap, mult):
    best = None
    for t in range(mult, min(n, cap) + 1, mult):
        if n % t == 0:
            best = t
    return n if best is None else best


def _dot(a, b):
    return lax.dot_general(a, b, (((1,), (0,)), ((), ())), preferred_element_type=F32)


def _dot_nt(a, b):
    return lax.dot_general(a, b, (((1,), (1,)), ((), ())), preferred_element_type=F32)


def _dot_tn(a, b):
    return lax.dot_general(a, b, (((0,), (0,)), ((), ())), preferred_element_type=F32)


def _split_dot(x, m):
    hi = x.astype(BF16)
    lo = (x - hi.astype(F32)).astype(BF16)
    return _dot(hi, m) + _dot(lo, m)


class _Cfg:
    def __init__(self, d_model, d_ff):
        nh = d_model // HEAD_DIM
        self.d = d_model
        self.f = d_ff
        self.nha = nh // 4
        self.nkva = self.nha // 4
        self.nhb = nh // 4
        self.nhc = nh // 2
        self.a_q = self.nha * HEAD_DIM
        self.a_kv = self.nkva * HEAD_DIM
        self.b_w = self.nhb * HEAD_DIM
        self.c_w = self.nhc * HEAD_DIM
        sizes = [self.a_q, self.a_kv, self.a_kv, self.b_w, self.b_w, self.b_w, self.c_w, self.c_w, self.c_w]
        starts = [0] + [int(s) for s in np.cumsum(sizes)[:-1]]
        self.sections = list(zip(starts, sizes))
        self.in_width = int(sum(sizes))
        assert all(s % LANES == 0 for s in sizes)


def _matmul(a, b, mode, out_dtype, name, tm=512, tn=512, tk=512, residual=None):
    if mode == "tn":
        kdim, m = a.shape
    else:
        m, kdim = a.shape
    n = b.shape[0] if mode == "nt" else b.shape[1]
    tm, tn, tk = _div_tile(m, tm, LANES), _div_tile(n, tn, LANES), _div_tile(kdim, tk, LANES)
    nk = kdim // tk
    if mode == "tn":
        a_spec = pl.BlockSpec((tk, tm), lambda i, j, k: (k, i))
    else:
        a_spec = pl.BlockSpec((tm, tk), lambda i, j, k: (i, k))
    if mode == "nt":
        b_spec = pl.BlockSpec((tn, tk), lambda i, j, k: (j, k))
    else:
        b_spec = pl.BlockSpec((tk, tn), lambda i, j, k: (k, j))
    dot = {"nn": _dot, "nt": _dot_nt, "tn": _dot_tn}[mode]
    o_spec = pl.BlockSpec((tm, tn), lambda i, j, k: (i, j))
    in_specs = [a_spec, b_spec]
    args = [a, b]
    if residual is not None:
        in_specs.append(o_spec)
        args.append(residual)

    def body(*refs):
        if residual is None:
            a_ref, b_ref, o_ref, acc = refs
        else:
            a_ref, b_ref, r_ref, o_ref, acc = refs
        k = pl.program_id(2)

        @pl.when(k == 0)
        def _():
            acc[...] = jnp.zeros_like(acc)

        acc[...] += dot(a_ref[...].astype(BF16), b_ref[...].astype(BF16))

        @pl.when(k == nk - 1)
        def _():
            r = acc[...]
            if residual is not None:
                r = r + r_ref[...]
            o_ref[...] = r.astype(out_dtype)

    return pl.pallas_call(
        body, name=name, grid=(m // tm, n // tn, nk), in_specs=in_specs, out_specs=o_spec,
        out_shape=jax.ShapeDtypeStruct((m, n), out_dtype), scratch_shapes=[pltpu.VMEM((tm, tn), F32)],
        compiler_params=_params(("parallel", "parallel", "arbitrary")),
    )(*args)


def _rmsnorm_fwd(x, g, name):
    s, d = x.shape
    ts = _div_tile(s, 256, 8)

    def body(x_ref, g_ref, o_ref):
        xv = x_ref[...]
        r = lax.rsqrt(jnp.mean(xv * xv, axis=-1, keepdims=True) + EPS)
        o_ref[...] = (xv * r * g_ref[...]).astype(BF16)

    return pl.pallas_call(
        body, name=name, grid=(s // ts,),
        in_specs=[pl.BlockSpec((ts, d), lambda i: (i, 0)), pl.BlockSpec((1, d), lambda i: (0, 0))],
        out_specs=pl.BlockSpec((ts, d), lambda i: (i, 0)), out_shape=jax.ShapeDtypeStruct((s, d), BF16),
        compiler_params=_params(("parallel",)),
    )(x, g)


def _rmsnorm_bwd(x, g, dh, dres, name):
    s, d = x.shape
    ts = _div_tile(s, 256, 16)

    def body(x_ref, g_ref, dh_ref, dres_ref, dx_ref, dxb_ref, dg_ref):
        @pl.when(pl.program_id(0) == 0)
        def _():
            dg_ref[...] = jnp.zeros_like(dg_ref)

        xv = x_ref[...]
        r = lax.rsqrt(jnp.mean(xv * xv, axis=-1, keepdims=True) + EPS)
        xhat = xv * r
        dhv = dh_ref[...]
        dxhat = dhv * g_ref[...]
        dx = dres_ref[...] + r * (dxhat - xhat * jnp.mean(dxhat * xhat, axis=-1, keepdims=True))
        dx_ref[...] = dx
        dxb_ref[...] = dx.astype(BF16)
        dg_ref[...] += jnp.sum(dhv * xhat, axis=0, keepdims=True)

    row = pl.BlockSpec((ts, d), lambda i: (i, 0))
    vec = pl.BlockSpec((1, d), lambda i: (0, 0))
    return pl.pallas_call(
        body, name=name, grid=(s // ts,), in_specs=[row, vec, row, row], out_specs=[row, row, vec],
        out_shape=[jax.ShapeDtypeStruct((s, d), F32), jax.ShapeDtypeStruct((s, d), BF16),
                   jax.ShapeDtypeStruct((1, d), F32)],
        compiler_params=_params(("arbitrary",)),
    )(x, g, dh, dres)


def _head_mean_matrix():
    idx = np.arange(LANES) // HEAD_DIM
    return jnp.asarray((idx[:, None] == idx[None, :]).astype(np.float32) / HEAD_DIM, dtype=BF16)


def _head_mean(y, m128):
    w = y.shape[1]
    parts = [_split_dot(y[:, c * LANES:(c + 1) * LANES], m128) for c in range(w // LANES)]
    return parts[0] if len(parts) == 1 else jnp.concatenate(parts, axis=1)


_NORMED_SECTIONS = (0, 1, 6, 7)
_QUERY_SECTIONS = (0, 3, 6)


def _qk_prep(proj, gains, cfg, name):
    s = proj.shape[0]
    ts = _div_tile(s, 256, 16)
    m128 = _head_mean_matrix()

    def body(p_ref, m_ref, g0, g1, g6, g7, *outs):
        gref = dict(zip(_NORMED_SECTIONS, (g0, g1, g6, g7)))
        for idx, (st, w) in enumerate(cfg.sections):
            xv = p_ref[:, st:st + w]
            if idx in gref:
                r = lax.rsqrt(_head_mean(xv * xv, m_ref[...]) + EPS)
                xv = xv * r * gref[idx][...]
            if idx in _QUERY_SECTIONS:
                xv = xv * QK_SCALE
            outs[idx][...] = xv.astype(BF16)

    in_specs = [pl.BlockSpec((ts, cfg.in_width), lambda i: (i, 0)), pl.BlockSpec((LANES, LANES), lambda i: (0, 0))]
    in_specs += [pl.BlockSpec((1, cfg.sections[k][1]), lambda i: (0, 0)) for k in _NORMED_SECTIONS]
    out_specs = [pl.BlockSpec((ts, w), lambda i: (i, 0)) for _, w in cfg.sections]
    out_shape = [jax.ShapeDtypeStruct((s, w), BF16) for _, w in cfg.sections]
    return pl.pallas_call(
        body, name=name, grid=(s // ts,), in_specs=in_specs, out_specs=out_specs, out_shape=out_shape,
        compiler_params=_params(("parallel",)),
    )(proj, m128, *gains)


def _qk_prep_bwd(proj, gains, grads, cfg, name):
    s = proj.shape[0]
    ts = _div_tile(s, 128, 16)
    m128 = _head_mean_matrix()
    counts = [len(gl) for gl in grads]
    flat = [g for gl in grads for g in gl]

    def body(*refs):
        p_ref, m_ref = refs[0], refs[1]
        gref = dict(zip(_NORMED_SECTIONS, refs[2:6]))
        g_in = refs[6:6 + len(flat)]
        dp_ref = refs[6 + len(flat)]
        dgain = dict(zip(_NORMED_SECTIONS, refs[7 + len(flat):]))

        @pl.when(pl.program_id(0) == 0)
        def _():
            for k in _NORMED_SECTIONS:
                dgain[k][...] = jnp.zeros_like(dgain[k])

        pos = 0
        for idx, (st, w) in enumerate(cfg.sections):
            dy = g_in[pos][...].astype(F32)
            for extra in g_in[pos + 1:pos + counts[idx]]:
                dy = dy + extra[...].astype(F32)
            pos += counts[idx]
            if idx in gref:
                xv = p_ref[:, st:st + w]
                r = lax.rsqrt(_head_mean(xv * xv, m_ref[...]) + EPS)
                xhat = xv * r
                dxhat = dy * gref[idx][...]
                dgain[idx][...] += jnp.sum(dy * xhat, axis=0, keepdims=True)
                dy = r * (dxhat - xhat * _head_mean(dxhat * xhat, m_ref[...]))
            dp_ref[:, st:st + w] = dy.astype(BF16)

    in_specs = [pl.BlockSpec((ts, cfg.in_width), lambda i: (i, 0)), pl.BlockSpec((LANES, LANES), lambda i: (0, 0))]
    in_specs += [pl.BlockSpec((1, cfg.sections[k][1]), lambda i: (0, 0)) for k in _NORMED_SECTIONS]
    for idx, (_, w) in enumerate(cfg.sections):
        in_specs += [pl.BlockSpec((ts, w), lambda i: (i, 0))] * counts[idx]
    out_specs = [pl.BlockSpec((ts, cfg.in_width), lambda i: (i, 0))]
    out_specs += [pl.BlockSpec((1, cfg.sections[k][1]), lambda i: (0, 0)) for k in _NORMED_SECTIONS]
    out_shape = [jax.ShapeDtypeStruct((s, cfg.in_width), BF16)]
    out_shape += [jax.ShapeDtypeStruct((1, cfg.sections[k][1]), F32) for k in _NORMED_SECTIONS]
    return pl.pallas_call(
        body, name=name, grid=(s // ts,), in_specs=in_specs, out_specs=out_specs, out_shape=out_shape,
        compiler_params=_params(("arbitrary",)),
    )(proj, m128, *gains, *flat)


def _band_masks(max_dist):
    row = lax.broadcasted_iota(jnp.int32, (BLOCK, BLOCK), 0)
    col = lax.broadcasted_iota(jnp.int32, (BLOCK, BLOCK), 1)
    return row + BLOCK - col <= max_dist, col <= row


def _dilated_t(a, dil):
    s, w = a.shape
    return _keys_on_lanes(a.reshape(s // dil, dil * w), BLOCK)


def _undilated(at, dil):
    nblk, dw, _ = at.shape
    return jnp.transpose(at, (0, 2, 1)).reshape(nblk * BLOCK * dil, dw // dil)


def _banded_fwd(q, kt, v, bias, sinks, hq, hk, max_dist, dil, name):
    s = q.shape[0]
    wq, wk, sd, grp = hq * HEAD_DIM, hk * HEAD_DIM, s // dil, hq // hk
    nb = sd // BLOCK
    has_sink = sinks is not None

    def body(*refs):
        if has_sink:
            q_ref, ktp_ref, ktc_ref, vp_ref, vc_ref, b_ref, s_ref, o_ref, l_ref = refs
        else:
            q_ref, ktp_ref, ktc_ref, vp_ref, vc_ref, b_ref, o_ref, l_ref = refs
        i = pl.program_id(1)
        mprev, mcur = _band_masks(max_dist)
        mask = jnp.concatenate([jnp.logical_and(mprev, i > 0), mcur], axis=1)
        for h in range(hq):
            sq = slice(h * HEAD_DIM, (h + 1) * HEAD_DIM)
            sk = slice((h // grp) * HEAD_DIM, (h // grp + 1) * HEAD_DIM)
            kt = jnp.concatenate([ktp_ref[sk, :], ktc_ref[sk, :]], axis=1)
            vv = jnp.concatenate([vp_ref[:, sk], vc_ref[:, sk]], axis=0)
            sc = jnp.where(mask, _dot(q_ref[:, sq], kt) + b_ref[h], NEG_INF)
            m = jnp.max(sc, axis=-1, keepdims=True)
            if has_sink:
                m = jnp.maximum(m, s_ref[h])
            p = jnp.exp(sc - m)
            den = jnp.sum(p, axis=-1, keepdims=True)
            if has_sink:
                den = den + jnp.exp(s_ref[h] - m)
            o_ref[:, sq] = _dot(p.astype(BF16), vv) / den
            l_ref[:, h:h + 1] = m + jnp.log(den)

    qspec = pl.BlockSpec((BLOCK, wq), lambda r, i: (i, r))
    kprev = pl.BlockSpec((BLOCK, wk), lambda r, i: (jnp.maximum(i - 1, 0), r))
    kcur = pl.BlockSpec((BLOCK, wk), lambda r, i: (i, r))
    ktprev = pl.BlockSpec((None, wk, BLOCK), lambda r, i: (jnp.maximum(i - 1, 0), r, 0))
    ktcur = pl.BlockSpec((None, wk, BLOCK), lambda r, i: (i, r, 0))
    in_specs = [qspec, ktprev, ktcur, kprev, kcur, pl.BlockSpec((hq, BLOCK, 2 * BLOCK), lambda r, i: (0, 0, 0))]
    v2 = v.reshape(sd, dil * wk)
    args = [q.reshape(sd, dil * wq), kt, kt, v2, v2, bias]
    if has_sink:
        in_specs.append(pl.BlockSpec(memory_space=pltpu.SMEM))
        args.append(sinks)
    out, lse = pl.pallas_call(
        body, name=name, grid=(dil, nb), in_specs=in_specs,
        out_specs=[qspec, pl.BlockSpec((None, BLOCK, hq), lambda r, i: (r, i, 0))],
        out_shape=[jax.ShapeDtypeStruct((sd, dil * wq), F32), jax.ShapeDtypeStruct((dil, sd, hq), F32)],
        compiler_params=_params(("parallel", "parallel")),
    )(*args)
    return out.reshape(s, wq), jnp.transpose(lse, (1, 0, 2)).reshape(s, hq)


def _per_head_dilated(a, dil):
    s, h = a.shape
    return jnp.transpose(a.reshape(s // dil, dil, h), (1, 0, 2))


def _banded_bwd(q, k, kt, v, lse, dsum, do, bias, sinks, dbias_init, hq, hk, max_dist, dil, name):
    s = q.shape[0]
    wq, wk, sd, grp = hq * HEAD_DIM, hk * HEAD_DIM, s // dil, hq // hk
    nb = sd // BLOCK
    has_sink = sinks is not None

    def body(*refs):
        (q_ref, qn_ref, qt_ref, qtn_ref, kp_ref, kc_ref, ktp_ref, ktc_ref, vtp_ref, vtc_ref, l_ref, ln_ref, d_ref,
         dn_ref, do_ref, don_ref, dot_ref, dotn_ref, b_ref, dbi_ref) = refs[:20]
        rest = refs[20:]
        if has_sink:
            s_ref, dq_ref, dkt_ref, dvt_ref, db_ref, ds_ref = rest
        else:
            dq_ref, dkt_ref, dvt_ref, db_ref = rest
        j = pl.program_id(1)

        @pl.when(jnp.logical_and(pl.program_id(0) == 0, j == 0))
        def _():
            db_ref[...] = dbi_ref[...]
            if has_sink:
                ds_ref[...] = jnp.zeros_like(ds_ref)

        mprev_static, mcur = _band_masks(max_dist)
        mask = jnp.concatenate([jnp.logical_and(mprev_static, j > 0), mcur], axis=1)
        mnext = jnp.logical_and(mprev_static, j + 1 < nb)
        dkt_acc = [jnp.zeros((HEAD_DIM, BLOCK), F32) for _ in range(hk)]
        dvt_acc = [jnp.zeros((HEAD_DIM, BLOCK), F32) for _ in range(hk)]
        for h in range(hq):
            g = h // grp
            sq = slice(h * HEAD_DIM, (h + 1) * HEAD_DIM)
            sk = slice(g * HEAD_DIM, (g + 1) * HEAD_DIM)
            kt2 = jnp.concatenate([ktp_ref[sk, :], ktc_ref[sk, :]], axis=1)
            vt2 = jnp.concatenate([vtp_ref[sk, :], vtc_ref[sk, :]], axis=1)
            k2 = jnp.concatenate([kp_ref[:, sk], kc_ref[:, sk]], axis=0)
            lcol = l_ref[:, h:h + 1]
            dcol = d_ref[:, h:h + 1]
            sc = _dot(q_ref[:, sq], kt2) + b_ref[h]
            p = jnp.where(mask, jnp.exp(sc - lcol), 0.0)
            ds = p * (_dot(do_ref[:, sq], vt2) - dcol)
            dsb = ds.astype(BF16)
            dq_ref[:, sq] = (_dot(dsb, k2) * QK_SCALE).astype(BF16)
            db_ref[h] += ds
            if has_sink:
                psink = jnp.exp(s_ref[h] - lcol)
                tot = jnp.sum(psink * dcol, axis=0, keepdims=True)
                ds_ref[h:h + 1, :] -= jnp.broadcast_to(tot, (1, LANES))
            lncol = ln_ref[:, h:h + 1]
            dncol = dn_ref[:, h:h + 1]
            sn = _dot(qn_ref[:, sq], ktc_ref[sk, :]) + b_ref[h, :, 0:BLOCK]
            pn = jnp.where(mnext, jnp.exp(sn - lncol), 0.0)
            dsn = pn * (_dot(don_ref[:, sq], vtc_ref[sk, :]) - dncol)
            dkt_acc[g] = dkt_acc[g] + (_dot(qt_ref[sq, :], dsb[:, BLOCK:]) + _dot(qtn_ref[sq, :], dsn.astype(BF16)))
            dvt_acc[g] = dvt_acc[g] + (_dot(dot_ref[sq, :], p[:, BLOCK:].astype(BF16))
                                       + _dot(dotn_ref[sq, :], pn.astype(BF16)))
        for g in range(hk):
            sk = slice(g * HEAD_DIM, (g + 1) * HEAD_DIM)
            dkt_ref[sk, :] = dkt_acc[g].astype(BF16)
            dvt_ref[sk, :] = dvt_acc[g].astype(BF16)

    qcur = pl.BlockSpec((BLOCK, wq), lambda r, j: (j, r))
    qnext = pl.BlockSpec((BLOCK, wq), lambda r, j: (jnp.minimum(j + 1, nb - 1), r))
    qtcur = pl.BlockSpec((None, wq, BLOCK), lambda r, j: (j, r, 0))
    qtnext = pl.BlockSpec((None, wq, BLOCK), lambda r, j: (jnp.minimum(j + 1, nb - 1), r, 0))
    kprev = pl.BlockSpec((BLOCK, wk), lambda r, j: (jnp.maximum(j - 1, 0), r))
    kcur = pl.BlockSpec((BLOCK, wk), lambda r, j: (j, r))
    ktprev = pl.BlockSpec((None, wk, BLOCK), lambda r, j: (jnp.maximum(j - 1, 0), r, 0))
    ktcur = pl.BlockSpec((None, wk, BLOCK), lambda r, j: (j, r, 0))
    bspec = pl.BlockSpec((hq, BLOCK, 2 * BLOCK), lambda r, j: (0, 0, 0))
    hcur = pl.BlockSpec((None, BLOCK, hq), lambda r, j: (r, j, 0))
    hnext = pl.BlockSpec((None, BLOCK, hq), lambda r, j: (r, jnp.minimum(j + 1, nb - 1), 0))
    dob = do.astype(BF16)
    q2, k2, do2 = q.reshape(sd, dil * wq), k.reshape(sd, dil * wk), dob.reshape(sd, dil * wq)
    l3, d3 = _per_head_dilated(lse, dil), _per_head_dilated(dsum, dil)
    qt, vt, dot = _dilated_t(q, dil), _dilated_t(v, dil), _dilated_t(dob, dil)
    in_specs = [qcur, qnext, qtcur, qtnext, kprev, kcur, ktprev, ktcur, ktprev, ktcur, hcur, hnext, hcur, hnext,
                qcur, qnext, qtcur, qtnext, bspec, bspec]
    args = [q2, q2, qt, qt, k2, k2, kt, kt, vt, vt, l3, l3, d3, d3, do2, do2, dot, dot, bias, dbias_init]
    out_specs = [qcur, ktcur, ktcur, bspec]
    out_shape = [jax.ShapeDtypeStruct((sd, dil * wq), BF16), jax.ShapeDtypeStruct((nb, dil * wk, BLOCK), BF16),
                 jax.ShapeDtypeStruct((nb, dil * wk, BLOCK), BF16), jax.ShapeDtypeStruct((hq, BLOCK, 2 * BLOCK), F32)]
    if has_sink:
        in_specs.append(pl.BlockSpec(memory_space=pltpu.SMEM))
        args.append(sinks)
        out_specs.append(pl.BlockSpec((hq, LANES), lambda r, j: (0, 0)))
        out_shape.append(jax.ShapeDtypeStruct((hq, LANES), F32))
    res = pl.pallas_call(
        body, name=name, grid=(dil, nb), in_specs=in_specs, out_specs=out_specs, out_shape=out_shape,
        compiler_params=_params(("arbitrary", "arbitrary")),
    )(*args)
    dq, dk, dv, dbias = res[0].reshape(s, wq), _undilated(res[1], dil), _undilated(res[2], dil), res[3]
    return dq, dk, dv, dbias, (res[4][:, 0] if has_sink else None)


def _neg_softplus(z):
    return -(jnp.maximum(z, 0.0) + jnp.log(1.0 + jnp.exp(-jnp.abs(z))))


SB_CHUNK = 256
HEADS_PER_PAIR = LANES // HEAD_DIM


def _tri(kind):
    row = lax.broadcasted_iota(jnp.int32, (SB_CHUNK, SB_CHUNK), 0)
    col = lax.broadcasted_iota(jnp.int32, (SB_CHUNK, SB_CHUNK), 1)
    return {"ge": row >= col, "lt": row < col, "le": row <= col}[kind].astype(BF16)


def _keys_on_lanes(a, rows):
    s, w = a.shape
    return jnp.transpose(a.reshape(s // rows, rows, w), (0, 2, 1))


def _sb_mask(i, jj):
    row = lax.broadcasted_iota(jnp.int32, (BLOCK, SB_CHUNK), 0)
    col = lax.broadcasted_iota(jnp.int32, (BLOCK, SB_CHUNK), 1)
    return col < row + (i * BLOCK - jj * SB_CHUNK)


def _sb_trips(i):
    return (i * BLOCK) // (2 * SB_CHUNK) + 1


def _sb_rows(jj, n):
    return pl.ds(pl.multiple_of(jj * SB_CHUNK, SB_CHUNK), n * SB_CHUNK)


def _sb_fwd(q, kt, v, name):
    s, w = q.shape
    npair, nb, nc = w // LANES, s // BLOCK, s // SB_CHUNK

    def body(q_ref, kt_ref, v_ref, o_ref, t_ref):
        i = pl.program_id(1)
        lincl = _tri("ge")
        heads = [slice(hh * HEAD_DIM, (hh + 1) * HEAD_DIM) for hh in range(HEADS_PER_PAIR)]
        qs = [q_ref[:, sl] for sl in heads]

        def trip(t, carry, masked):
            lo, hi = 2 * t, 2 * t + 1
            mlo, mhi = (_sb_mask(i, lo), _sb_mask(i, hi)) if masked else (None, None)

            def keep(m, val):
                return val if m is None else jnp.where(m, val, 0.0)

            new = []
            for hh, sl in enumerate(heads):
                o_acc, rem = carry[hh]
                zhi = _dot(qs[hh], kt_ref[hi, sl, :])
                zlo = _dot(qs[hh], kt_ref[lo, sl, :])
                lrhi = keep(mhi, _neg_softplus(zhi))
                lrlo = keep(mlo, _neg_softplus(zlo))
                tothi = jnp.sum(lrhi, axis=-1, keepdims=True)
                ahi = keep(mhi, jnp.exp(zhi + (rem + _split_dot(lrhi, lincl))))
                alo = keep(mlo, jnp.exp(zlo + (rem + tothi + _split_dot(lrlo, lincl))))
                a = jnp.concatenate([alo, ahi], axis=1).astype(BF16)
                new.append((o_acc + _dot(a, v_ref[_sb_rows(lo, 2), sl]),
                            rem + tothi + jnp.sum(lrlo, axis=-1, keepdims=True)))
            return tuple(new)

        init = tuple((jnp.zeros((BLOCK, HEAD_DIM), F32), jnp.zeros((BLOCK, 1), F32)) for _ in heads)
        trips = _sb_trips(i)
        carry = trip(trips - 1, init, True)
        carry = lax.fori_loop(0, trips - 1, lambda t, cr: trip(trips - 2 - t, cr, False), carry)
        for hh, sl in enumerate(heads):
            o_ref[:, sl] = carry[hh][0]
            t_ref[:, sl] = jnp.broadcast_to(carry[hh][1], (BLOCK, HEAD_DIM))

    qspec = pl.BlockSpec((BLOCK, LANES), lambda p, i: (i, p))
    return pl.pallas_call(
        body, name=name, grid=(npair, nb),
        in_specs=[qspec, pl.BlockSpec((nc, LANES, SB_CHUNK), lambda p, i: (0, p, 0)),
                  pl.BlockSpec((s, LANES), lambda p, i: (0, p))],
        out_specs=[qspec, qspec], out_shape=[jax.ShapeDtypeStruct((s, w), F32)] * 2,
        compiler_params=_params(("parallel", "parallel")),
    )(q, kt, v)


def _sb_bwd(q, k, kt, v, tot, do, name):
    s, w = q.shape
    npair, nb, nc = w // LANES, s // BLOCK, s // SB_CHUNK
    dob = do.astype(BF16)

    def body(q_ref, qt_ref, k_ref, kt_ref, vt_ref, t_ref, do_ref, dot_ref, dq_ref, dkt_ref, dvt_ref):
        i = pl.program_id(1)

        @pl.when(i == 0)
        def _():
            dkt_ref[...] = jnp.zeros_like(dkt_ref)
            dvt_ref[...] = jnp.zeros_like(dvt_ref)

        lbefore = _tri("lt")
        lupto = _tri("le")
        heads = [slice(hh * HEAD_DIM, (hh + 1) * HEAD_DIM) for hh in range(HEADS_PER_PAIR)]
        qs = [q_ref[:, sl] for sl in heads]
        qts = [qt_ref[sl, :] for sl in heads]
        dos = [do_ref[:, sl] for sl in heads]
        dots = [dot_ref[sl, :] for sl in heads]
        totals = [t_ref[:, sl.start:sl.start + 1] for sl in heads]

        def trip(t, carry, masked):
            chunks = (2 * t, 2 * t + 1)
            masks = [_sb_mask(i, jj) if masked else None for jj in chunks]

            def keep(m, val):
                return val if m is None else jnp.where(m, val, 0.0)

            new = []
            for hh, sl in enumerate(heads):
                dq_acc, plr, pg = carry[hh]
                zs = [_dot(qs[hh], kt_ref[jj, sl, :]) for jj in chunks]
                lrs = [keep(m, _neg_softplus(z)) for m, z in zip(masks, zs)]
                lr_sums = [jnp.sum(lr, axis=-1, keepdims=True) for lr in lrs]
                before = [plr, plr + lr_sums[0]]
                avs = [keep(m, jnp.exp(z + (totals[hh] - (b + _split_dot(lr, lbefore)))))
                       for m, z, lr, b in zip(masks, zs, lrs, before)]
                gs = [_dot(dos[hh], vt_ref[jj, sl, :]) * a for jj, a in zip(chunks, avs)]
                g_sums = [jnp.sum(g, axis=-1, keepdims=True) for g in gs]
                upto = [pg, pg + g_sums[0]]
                dzs = [keep(m, g - jnp.exp(z + lr) * (u + _split_dot(g, lupto))).astype(BF16)
                       for m, z, lr, g, u in zip(masks, zs, lrs, gs, upto)]
                for jj, dzb, a in zip(chunks, dzs, avs):
                    dkt_ref[jj, sl, :] += _dot(qts[hh], dzb)
                    dvt_ref[jj, sl, :] += _dot(dots[hh], a.astype(BF16))
                dz2 = jnp.concatenate(dzs, axis=1)
                new.append((dq_acc + _dot(dz2, k_ref[_sb_rows(chunks[0], 2), sl]), plr + lr_sums[0] + lr_sums[1],
                            pg + g_sums[0] + g_sums[1]))
            return tuple(new)

        zero = jnp.zeros((BLOCK, 1), F32)
        init = tuple((jnp.zeros((BLOCK, HEAD_DIM), F32), zero, zero) for _ in heads)
        trips = _sb_trips(i)
        carry = lax.fori_loop(0, trips - 1, lambda t, cr: trip(t, cr, False), init)
        carry = trip(trips - 1, carry, True)
        for hh, sl in enumerate(heads):
            dq_ref[:, sl] = (carry[hh][0] * QK_SCALE).astype(BF16)

    qspec = pl.BlockSpec((BLOCK, LANES), lambda p, i: (i, p))
    qtspec = pl.BlockSpec((None, LANES, BLOCK), lambda p, i: (i, p, 0))
    kspec = pl.BlockSpec((s, LANES), lambda p, i: (0, p))
    ktspec = pl.BlockSpec((nc, LANES, SB_CHUNK), lambda p, i: (0, p, 0))
    dq, dkt, dvt = pl.pallas_call(
        body, name=name, grid=(npair, nb), in_specs=[qspec, qtspec, kspec, ktspec, ktspec, qspec, qspec, qtspec],
        out_specs=[qspec, ktspec, ktspec],
        out_shape=[jax.ShapeDtypeStruct((s, w), BF16)] + [jax.ShapeDtypeStruct((nc, w, SB_CHUNK), F32)] * 2,
        compiler_params=_params(("parallel", "arbitrary")),
    )(q, _keys_on_lanes(q, BLOCK), k, kt, _keys_on_lanes(v, SB_CHUNK), tot, dob, _keys_on_lanes(dob, BLOCK))

    def rows_first(t):
        return jnp.transpose(t, (0, 2, 1)).reshape(s, w)

    return dq, rows_first(dkt), rows_first(dvt)


def _group_norm(xv, g):
    r = lax.rsqrt(jnp.mean(xv * xv, axis=-1, keepdims=True) + EPS)
    return xv * r * g


def _head_spread(nheads):
    return jnp.asarray(np.repeat(np.eye(nheads, dtype=np.float32), HEAD_DIM, axis=1), dtype=BF16)


def _mix_fwd(oa, ob, ocs, lses, gain, cfg, name):
    s = oa.shape[0]
    ts = _div_tile(s, 256, 16)
    aq, bw, cw, nhc = cfg.a_q, cfg.b_w, cfg.c_w, cfg.nhc

    def body(oa_ref, ob_ref, c1, c2, c3, l1, l2, l3, sp_ref, g_ref, mix_ref, oc_ref, lse_ref):
        m = jnp.maximum(jnp.maximum(l1[...], l2[...]), l3[...])
        es = [jnp.exp(l[...] - m) for l in (l1, l2, l3)]
        den = es[0] + es[1] + es[2]
        oc = sum(_split_dot(e / den, sp_ref[...]) * c[...] for e, c in zip(es, (c1, c2, c3)))
        oc_ref[...] = oc
        lse_ref[...] = m + jnp.log(den)
        mix_ref[:, 0:aq] = _group_norm(oa_ref[...], g_ref[:, 0:aq]).astype(BF16)
        mix_ref[:, aq:aq + bw] = _group_norm(ob_ref[...], g_ref[:, aq:aq + bw]).astype(BF16)
        mix_ref[:, aq + bw:] = _group_norm(oc, g_ref[:, aq + bw:]).astype(BF16)

    def row(wd):
        return pl.BlockSpec((ts, wd), lambda i: (i, 0))

    return pl.pallas_call(
        body, name=name, grid=(s // ts,),
        in_specs=[row(aq), row(bw)] + [row(cw)] * 3 + [row(nhc)] * 3
        + [pl.BlockSpec((nhc, cw), lambda i: (0, 0)), pl.BlockSpec((1, cfg.d), lambda i: (0, 0))],
        out_specs=[row(cfg.d), row(cw), row(nhc)],
        out_shape=[jax.ShapeDtypeStruct((s, cfg.d), BF16), jax.ShapeDtypeStruct((s, cw), F32),
                   jax.ShapeDtypeStruct((s, nhc), F32)],
        compiler_params=_params(("parallel",)),
    )(oa, ob, *ocs, *lses, _head_spread(nhc), gain)


def _mix_bwd(dmix, oa, ob, oc, gain, cfg, name):
    s = oa.shape[0]
    ts = _div_tile(s, 256, 8)
    aq, bw, cw = cfg.a_q, cfg.b_w, cfg.c_w

    def body(dm_ref, oa_ref, ob_ref, oc_ref, g_ref, fa_ref, fc_ref, da_ref, db_ref, dc_ref, dg_ref, sa_ref, sc_ref):
        @pl.when(pl.program_id(0) == 0)
        def _():
            dg_ref[...] = jnp.zeros_like(dg_ref)

        for x_ref, dx_ref, lo, hi, fold in ((oa_ref, da_ref, 0, aq, (fa_ref, sa_ref)), (ob_ref, db_ref, aq, aq + bw, None),
                                            (oc_ref, dc_ref, aq + bw, aq + bw + cw, (fc_ref, sc_ref))):
            xv = x_ref[...]
            dy = dm_ref[:, lo:hi]
            r = lax.rsqrt(jnp.mean(xv * xv, axis=-1, keepdims=True) + EPS)
            xhat = xv * r
            dxhat = dy * g_ref[:, lo:hi]
            dx = r * (dxhat - xhat * jnp.mean(dxhat * xhat, axis=-1, keepdims=True))
            dx_ref[...] = dx
            dg_ref[:, lo:hi] += jnp.sum(dy * xhat, axis=0, keepdims=True)
            if fold is not None:
                fold[1][...] = _split_dot(dx * xv, fold[0][...])

    def row(wd):
        return pl.BlockSpec((ts, wd), lambda i: (i, 0))

    vec = pl.BlockSpec((1, cfg.d), lambda i: (0, 0))
    return pl.pallas_call(
        body, name=name, grid=(s // ts,),
        in_specs=[row(cfg.d), row(aq), row(bw), row(cw), vec, pl.BlockSpec((aq, cfg.nha), lambda i: (0, 0)),
                  pl.BlockSpec((cw, cfg.nhc), lambda i: (0, 0))],
        out_specs=[row(aq), row(bw), row(cw), vec, row(cfg.nha), row(cfg.nhc)],
        out_shape=[jax.ShapeDtypeStruct((s, aq), F32), jax.ShapeDtypeStruct((s, bw), F32),
                   jax.ShapeDtypeStruct((s, cw), F32), jax.ShapeDtypeStruct((1, cfg.d), F32),
                   jax.ShapeDtypeStruct((s, cfg.nha), F32), jax.ShapeDtypeStruct((s, cfg.nhc), F32)],
        compiler_params=_params(("arbitrary",)),
    )(dmix, oa, ob, oc, gain, _head_spread(cfg.nha).T, _head_spread(cfg.nhc).T)


def _bias_table_grad(dbiases, buckets, name):
    outs = []
    for idx, (db, bk) in enumerate(zip(dbiases, buckets)):
        h = db.shape[0]

        def body(db_ref, bk_ref, o_ref):
            xv = db_ref[0]
            ids = bk_ref[...]
            lane = lax.broadcasted_iota(jnp.int32, (1, LANES), 1)
            acc = jnp.zeros((1, LANES), F32)
            for b in range(N_BUCKETS):
                tot = jnp.sum(jnp.where(ids == b, xv, 0.0), axis=0, keepdims=True)
                tot = jnp.sum(tot, axis=1, keepdims=True)
                acc = jnp.where(lane == b, tot, acc)
            o_ref[0] = acc

        outs.append(pl.pallas_call(
            body, name=f"{name}_{idx}", grid=(h,),
            in_specs=[pl.BlockSpec((1, BLOCK, 2 * BLOCK), lambda i: (i, 0, 0)),
                      pl.BlockSpec((BLOCK, 2 * BLOCK), lambda i: (0, 0))],
            out_specs=pl.BlockSpec((1, 1, LANES), lambda i: (i, 0, 0)),
            out_shape=jax.ShapeDtypeStruct((h, 1, LANES), F32), compiler_params=_params(("parallel",)),
        )(db, bk)[:, 0, :])
    return outs


SUBLANES = 8


def _shift_down(u, n, rows):
    r = pltpu.roll(u, n, 0)
    return jnp.concatenate([jnp.where(rows[:SUBLANES] >= n, r[:SUBLANES], 0.0), r[SUBLANES:]], axis=0)


def _shift_up(u, n, rows, s):
    r = pltpu.roll(u, s - n, 0)
    return jnp.concatenate([r[:s - SUBLANES], jnp.where(rows[s - SUBLANES:] < s - n, r[s - SUBLANES:], 0.0)], axis=0)


def _conv(u, w_ref, b_ref, rows):
    return (b_ref[...] + w_ref[0:1, :] * _shift_down(u, 2, rows) + w_ref[1:2, :] * _shift_down(u, 1, rows)
            + w_ref[2:3, :] * u)


def _conv_act_fwd(u, conv_w, conv_b, f, name):
    s = u.shape[0]
    nf = f // LANES

    def body(ug_ref, uu_ref, wg_ref, wu_ref, bg_ref, bu_ref, act_ref):
        rows = lax.broadcasted_iota(jnp.int32, (s, LANES), 0)
        gate = _conv(ug_ref[...], wg_ref, bg_ref, rows)
        up = _conv(uu_ref[...], wu_ref, bu_ref, rows)
        act_ref[...] = (gate * jax.nn.sigmoid(gate) * up).astype(BF16)

    def col(rws, off):
        return pl.BlockSpec((rws, LANES), lambda j: (0, j + off))

    return pl.pallas_call(
        body, name=name, grid=(nf,),
        in_specs=[col(s, 0), col(s, nf), col(CONV_WIDTH, 0), col(CONV_WIDTH, nf), col(1, 0), col(1, nf)],
        out_specs=col(s, 0), out_shape=jax.ShapeDtypeStruct((s, f), BF16), compiler_params=_params(("parallel",)),
    )(u, u, conv_w, conv_w, conv_b, conv_b)


def _conv_act_bwd(u, dact, conv_w, conv_b, f, name):
    s = u.shape[0]
    nf = f // LANES

    def body(ug_ref, uu_ref, da_ref, wg_ref, wu_ref, bg_ref, bu_ref, dug_ref, duu_ref, dwg_ref, dwu_ref, dbg_ref,
             dbu_ref):
        rows = lax.broadcasted_iota(jnp.int32, (s, LANES), 0)
        ug, uu = ug_ref[...], uu_ref[...]
        gate = _conv(ug, wg_ref, bg_ref, rows)
        up = _conv(uu, wu_ref, bu_ref, rows)
        sg = jax.nn.sigmoid(gate)
        da = da_ref[...]
        dgate = da * up * (sg * (1.0 + gate * (1.0 - sg)))
        dup = da * (gate * sg)
        for du, uv, w_ref, du_ref, dw_ref, db_ref in ((dgate, ug, wg_ref, dug_ref, dwg_ref, dbg_ref),
                                                     (dup, uu, wu_ref, duu_ref, dwu_ref, dbu_ref)):
            du_ref[...] = (w_ref[2:3, :] * du + w_ref[1:2, :] * _shift_up(du, 1, rows, s)
                           + w_ref[0:1, :] * _shift_up(du, 2, rows, s)).astype(BF16)
            dw_ref[0:1, :] = jnp.sum(du * _shift_down(uv, 2, rows), axis=0, keepdims=True)
            dw_ref[1:2, :] = jnp.sum(du * _shift_down(uv, 1, rows), axis=0, keepdims=True)
            dw_ref[2:3, :] = jnp.sum(du * uv, axis=0, keepdims=True)
            db_ref[...] = jnp.sum(du, axis=0, keepdims=True)

    def col(rws, off):
        return pl.BlockSpec((rws, LANES), lambda j: (0, j + off))

    return pl.pallas_call(
        body, name=name, grid=(nf,),
        in_specs=[col(s, 0), col(s, nf), col(s, 0), col(CONV_WIDTH, 0), col(CONV_WIDTH, nf), col(1, 0), col(1, nf)],
        out_specs=[col(s, 0), col(s, 0), col(CONV_WIDTH, 0), col(CONV_WIDTH, 0), col(1, 0), col(1, 0)],
        out_shape=[jax.ShapeDtypeStruct((s, f), BF16)] * 2 + [jax.ShapeDtypeStruct((CONV_WIDTH, f), F32)] * 2
        + [jax.ShapeDtypeStruct((1, f), F32)] * 2,
        compiler_params=_params(("parallel",)),
    )(u, u, dact, conv_w, conv_w, conv_b, conv_b)


def _loss_head(y, target, name):
    s, d = y.shape
    ts = _div_tile(s, 256, 16)

    def body(y_ref, t_ref, dy_ref, dyb_ref, l_ref):
        @pl.when(pl.program_id(0) == 0)
        def _():
            l_ref[...] = jnp.zeros_like(l_ref)

        err = y_ref[...] - t_ref[...]
        dy = err * (1.0 / d)
        dy_ref[...] = dy
        dyb_ref[...] = dy.astype(BF16)
        tot = jnp.sum(jnp.sum(err * err, axis=0, keepdims=True), axis=1, keepdims=True) * (0.5 / d)
        l_ref[...] += jnp.broadcast_to(tot, l_ref.shape)

    row = pl.BlockSpec((ts, d), lambda i: (i, 0))
    return pl.pallas_call(
        body, name=name, grid=(s // ts,), in_specs=[row, row],
        out_specs=[row, row, pl.BlockSpec((8, LANES), lambda i: (0, 0))],
        out_shape=[jax.ShapeDtypeStruct((s, d), F32), jax.ShapeDtypeStruct((s, d), BF16),
                   jax.ShapeDtypeStruct((8, LANES), F32)],
        compiler_params=_params(("arbitrary",)),
    )(y, target)


def _adamw(w, g, m, v, name):
    r, c = w.shape
    tr = _div_tile(r, max(8, (1 << 18) // c // 8 * 8), 8)
    c1 = 1.0 - ADAM_B1 ** ADAM_STEP
    c2 = 1.0 - ADAM_B2 ** ADAM_STEP

    def body(w_ref, g_ref, m_ref, v_ref, d_ref, nm_ref, nv_ref):
        gv = g_ref[...]
        nm = ADAM_B1 * m_ref[...] + (1.0 - ADAM_B1) * gv
        nv = ADAM_B2 * v_ref[...] + (1.0 - ADAM_B2) * (gv * gv)
        d_ref[...] = -ADAM_LR * ((nm / c1) / (jnp.sqrt(nv / c2) + ADAM_EPS) + ADAM_WD * w_ref[...])
        nm_ref[...] = nm
        nv_ref[...] = nv

    spec = pl.BlockSpec((tr, c), lambda i: (i, 0))
    return pl.pallas_call(
        body, name=name, grid=(r // tr,), in_specs=[spec] * 4, out_specs=[spec] * 3,
        out_shape=[jax.ShapeDtypeStruct((r, c), F32)] * 3, compiler_params=_params(("parallel",)),
    )(w, g, m, v)


def _adamw_layer(layer, w, g, m, v, bufs, name):
    depth, r, c = w.shape
    tr = _div_tile(r, max(8, (1 << 17) // c // 8 * 8), 8)
    c1 = 1.0 - ADAM_B1 ** ADAM_STEP
    c2 = 1.0 - ADAM_B2 ** ADAM_STEP

    def body(*refs):
        w_ref, g_ref, m_ref, v_ref = refs[:4]
        go_ref, d_ref, nm_ref, nv_ref = refs[-4:]
        gv = g_ref[...]
        nm = ADAM_B1 * m_ref[...] + (1.0 - ADAM_B1) * gv
        nv = ADAM_B2 * v_ref[...] + (1.0 - ADAM_B2) * (gv * gv)
        d_ref[...] = -ADAM_LR * ((nm / c1) / (jnp.sqrt(nv / c2) + ADAM_EPS) + ADAM_WD * w_ref[...])
        nm_ref[...] = nm
        nv_ref[...] = nv
        go_ref[...] = gv

    lay = pl.BlockSpec((None, tr, c), lambda i: (layer, i, 0))
    in_specs = [lay, pl.BlockSpec((tr, c), lambda i: (i, 0)), lay, lay]
    args = [w, g, m, v]
    aliases = {}
    if bufs is not None:
        in_specs += [pl.BlockSpec(memory_space=pl.ANY)] * 4
        args += list(bufs)
        aliases = {4 + k: k for k in range(4)}
    return pl.pallas_call(
        body, name=name, grid=(r // tr,), in_specs=in_specs, out_specs=[lay] * 4,
        out_shape=[jax.ShapeDtypeStruct((depth, r, c), F32)] * 4, input_output_aliases=aliases,
        compiler_params=_params(("parallel",)),
    )(*args)


def _mesh_pos():
    return lax.axis_index("x"), lax.axis_index("y"), lax.axis_index("c")


def _flip(v, bit):
    return 1 - v if bit else v


def _sum_parts(parts, name):
    _, r, c = parts.shape
    tr = _div_tile(r, 256, 16)

    def body(p_ref, o_ref):
        acc = p_ref[0].astype(F32)
        for src in range(1, N_DEVICES):
            acc = acc + p_ref[src].astype(F32)
        o_ref[...] = acc

    return pl.pallas_call(
        body, name=name, grid=(r // tr,), in_specs=[pl.BlockSpec((N_DEVICES, tr, c), lambda i: (0, i, 0))],
        out_specs=pl.BlockSpec((tr, c), lambda i: (i, 0)), out_shape=jax.ShapeDtypeStruct((r, c), F32),
        compiler_params=_params(("parallel",)),
    )(parts)


def _split_start(srcs, lands, plan, ncopies, name):
    nbuf = len(srcs) + len(lands)

    def body(*refs):
        bufs = refs[:nbuf]
        send_sem, recv_sem, token = refs[nbuf], refs[nbuf + 1], refs[-1]
        for k, (src, dst, dev) in enumerate(plan(bufs[:len(srcs)], bufs[len(srcs):])):
            pltpu.make_async_remote_copy(src_ref=src, dst_ref=dst, send_sem=send_sem.at[k], recv_sem=recv_sem.at[k],
                                         device_id=dev, device_id_type=MESH).start()
        token[...] = jnp.zeros_like(token)

    hbm = pl.BlockSpec(memory_space=pltpu.HBM)
    sem = pl.BlockSpec(memory_space=pltpu.SEMAPHORE)
    operands = [pltpu.with_memory_space_constraint(a, pltpu.HBM) for a in (*srcs, *lands)]
    outs = pl.pallas_call(
        body, name=name, in_specs=[hbm] * nbuf,
        out_specs=(sem, sem, *[hbm] * nbuf, pl.BlockSpec(memory_space=pltpu.VMEM)),
        out_shape=(pltpu.SemaphoreType.DMA((ncopies,)), pltpu.SemaphoreType.DMA((ncopies,)),
                   *[pltpu.HBM(a.shape, a.dtype) for a in operands], jax.ShapeDtypeStruct((8, LANES), F32)),
        input_output_aliases={i: 2 + i for i in range(nbuf)},
        compiler_params=pltpu.CompilerParams(has_side_effects=pltpu.SideEffectType.DATAFLOW_SIDE_EFFECTING),
    )(*operands)
    handle = dict(send=outs[0], recv=outs[1], bufs=list(outs[2:2 + nbuf]), nsrc=len(srcs), plan=plan)
    return handle, outs[-1]


def _split_wait(handle, after, name):
    nbuf, nsrc, plan = len(handle["bufs"]), handle["nsrc"], handle["plan"]

    def body(*refs):
        bufs = refs[:nbuf]
        send_sem, recv_sem = refs[nbuf], refs[nbuf + 1]
        for k, (src, dst, dev) in enumerate(plan(bufs[:nsrc], bufs[nsrc:])):
            copy = pltpu.make_async_remote_copy(src_ref=src, dst_ref=dst, send_sem=send_sem.at[k],
                                                recv_sem=recv_sem.at[k], device_id=dev, device_id_type=MESH)
            copy.wait_send()
            copy.wait_recv()

    hbm = pl.BlockSpec(memory_space=pltpu.HBM)
    sem = pl.BlockSpec(memory_space=pltpu.SEMAPHORE)
    outs = pl.pallas_call(
        body, name=name, in_specs=[hbm] * nbuf + [sem, sem, pl.BlockSpec(memory_space=pl.ANY)],
        out_specs=[hbm] * nbuf, out_shape=[pltpu.HBM(a.shape, a.dtype) for a in handle["bufs"]],
        input_output_aliases={i: i for i in range(nbuf)},
        compiler_params=pltpu.CompilerParams(has_side_effects=pltpu.SideEffectType.DATAFLOW_SIDE_EFFECTING),
    )(*handle["bufs"], handle["send"], handle["recv"], after)
    return list(outs[nsrc:])


def _own_slot(shape, dtype, block, index):
    return lax.dynamic_update_slice(lax.empty(shape, dtype), block[None], (index,) + (0,) * block.ndim)


def _gather_plan(srcs, lands):
    x, y, c = _mesh_pos()
    return [(land.at[2 * x + y], land.at[2 * x + y], (*chip, c))
            for land in lands for chip in ((1 - x, y), (x, 1 - y), (1 - x, 1 - y))]


def _scatter_plan(srcs, lands):
    x, y, c = _mesh_pos()
    out = []
    for src, land in zip(srcs, lands):
        half = src.shape[1] // 2
        for d in range(1, N_DEVICES):
            p = (_flip(x, d & 4), _flip(y, d & 2), _flip(c, d & 1))
            out.append((src.at[2 * p[0] + p[1], pl.ds(p[2] * half, half), :], land.at[4 * x + 2 * y + c], p))
    return out


def _swap_plan(srcs, lands):
    x, y, c = _mesh_pos()
    return [(src, land.at[c], (x, y, 1 - c)) for src, land in zip(srcs, lands)]


class _Gathered:
    def __init__(self, groups):
        self.groups = groups
        self.ready = {}

    def get(self, name, after=None):
        if name not in self.ready:
            handle, names, wait_name = next(g for g in self.groups if name in g[1])
            for n, full in zip(names, _split_wait(handle, after, wait_name)):
                self.ready[n] = full.reshape(-1, full.shape[-1])
        return self.ready[name]


def _allreduce_small(flat, name):
    r = flat.shape[0]

    def body(x_ref, o_ref, buf, send_sems, recv_sems):
        x, y, c = _mesh_pos()
        me = 4 * x + 2 * y + c
        buf[me] = x_ref[...]
        started = []
        peers = [(_flip(x, d & 4), _flip(y, d & 2), _flip(c, d & 1)) for d in range(1, N_DEVICES)]
        for d, p in enumerate(peers):
            cp = pltpu.make_async_remote_copy(src_ref=x_ref, dst_ref=buf.at[me], send_sem=send_sems.at[d],
                                              recv_sem=recv_sems.at[d], device_id=p, device_id_type=MESH)
            cp.start()
            started.append(cp)
        for d, p in enumerate(peers):
            slot = buf.at[4 * p[0] + 2 * p[1] + p[2]]
            pltpu.make_async_remote_copy(src_ref=slot, dst_ref=slot, send_sem=send_sems.at[d], recv_sem=recv_sems.at[d],
                                         device_id=p, device_id_type=MESH).wait_recv()
        for cp in started:
            cp.wait_send()
        acc = buf[0]
        for src in range(1, N_DEVICES):
            acc = acc + buf[src]
        o_ref[...] = acc

    vm = pl.BlockSpec(memory_space=pltpu.VMEM)
    return pl.pallas_call(
        body, name=name, in_specs=[vm], out_specs=vm, out_shape=jax.ShapeDtypeStruct((r, LANES), F32),
        scratch_shapes=[pltpu.VMEM((N_DEVICES, r, LANES), F32), pltpu.SemaphoreType.DMA((N_DEVICES - 1,)),
                        pltpu.SemaphoreType.DMA((N_DEVICES - 1,))],
        compiler_params=pltpu.CompilerParams(vmem_limit_bytes=VMEM_LIMIT_BYTES),
    )(flat)


def _bucket_ids(dil):
    rel = (np.arange(BLOCK)[:, None] + BLOCK - np.arange(2 * BLOCK)[None, :]) * dil
    max_exact = N_BUCKETS // 2
    d = np.maximum(rel, 0)
    large = max_exact + (np.log(np.maximum(d, 1).astype(np.float32) / max_exact)
                         / np.float32(np.log(T5_MAX_DIST / max_exact)) * (N_BUCKETS - max_exact)).astype(np.int32)
    large = np.minimum(large, N_BUCKETS - 1)
    return np.where(d < max_exact, d, large).astype(np.int32)


def _block_bias(table, dil):
    onehot = (jnp.asarray(_bucket_ids(dil))[:, :, None] == jnp.arange(N_BUCKETS)[None, None, :]).astype(F32)
    return jnp.einsum("ijb,bh->hij", onehot, table.astype(F32), precision=lax.Precision.HIGHEST)


def _tile_gain(g, n):
    return jnp.tile(g.reshape(1, HEAD_DIM), (1, n))


def _layer_fwd(x, p, cfg):
    w = p["weights"]
    h1 = _rmsnorm_fwd(x, p["attn_norm"], "attn_norm_fwd")
    proj = _matmul(h1, w.get("w_in_t", h1), "nt", F32, "in_proj", tm=1024, tn=768, tk=2048)
    aq, ak, av, bq, bk, bv, cq, ck, cv = _qk_prep(proj, p["gains"], cfg, "qk_prep")
    akt = _dilated_t(ak, 1)
    oa, lse_a = _banded_fwd(aq, akt, av, p["bias_a"], p["sinks"], cfg.nha, cfg.nkva, WINDOW_A - 1, 1, "swa_fwd")
    bkt = _keys_on_lanes(bk, SB_CHUNK)
    ob, tot_b = _sb_fwd(bq, bkt, bv, "stickbreak_fwd")
    ocs, lses, ckts = [], [], []
    for (window, dil), bias in zip(DILATED_PAIRS, p["bias_c"]):
        ckts.append(_dilated_t(ck, dil))
        o, l = _banded_fwd(cq, ckts[-1], cv, bias, None, cfg.nhc, cfg.nhc, window // dil, dil, f"dilated{dil}_fwd")
        ocs.append(o)
        lses.append(l)
    mix, oc, lse_c = _mix_fwd(oa, ob, ocs, lses, p["mix_gain"], cfg, "mix_fwd")
    xm = _matmul(mix, w.get("w_out", mix), "nn", F32, "out_proj", tm=1024, tn=512, tk=2048, residual=x)
    h2 = _rmsnorm_fwd(xm, p["ffn_norm"], "ffn_norm_fwd")
    u = _matmul(h2, w.get("w_up_t", h2), "nt", F32, "up_proj", tm=1024, tn=512, tk=2048)
    act = _conv_act_fwd(u, p["conv_w"], p["conv_b"], cfg.f, "conv_act_fwd")
    y = _matmul(act, w.get("w_down", act), "nn", F32, "down_proj", tm=1024, tn=1024, tk=1408, residual=xm)
    saved = dict(x=x, h1=h1, proj=proj, q=(aq, ak, av, bq, bk, bv, cq, ck, cv), oa=oa, lse_a=lse_a, ob=ob,
                 tot_b=tot_b, akt=akt, bkt=bkt, ckts=ckts, oc=oc, lse_c=lse_c, mix=mix, xm=xm, h2=h2, u=u, act=act)
    return y, saved


def _layer_bwd(dy, dyb, sv, p, dbias, cfg, on_grad):
    aq, ak, av, bq, bk, bv, cq, ck, cv = sv["q"]
    w = p["weights"]
    anchor = on_grad(_matmul(sv["act"], dyb, "tn", BF16, "down_proj_dw", tm=1408, tn=2048, tk=1024))
    dact = _matmul(dyb, w.get("w_down"), "nt", F32, "down_proj_dx", tm=1024, tn=512, tk=2048)
    dug, duu, dwg, dwu, dbg, dbu = _conv_act_bwd(sv["u"], dact, p["conv_w"], p["conv_b"] + anchor, cfg.f,
                                                 "conv_act_bwd")
    du = jnp.concatenate([dug, duu], axis=1)
    anchor = on_grad(_matmul(du, sv["h2"], "tn", BF16, "up_proj_dw", tm=1408, tn=2048, tk=1024))
    dh2 = _matmul(du, w.get("w_up_t"), "nn", F32, "up_proj_dx", tm=1024, tn=2048, tk=1024)
    dxm, dxmb, g_ffn_norm = _rmsnorm_bwd(sv["xm"], p["ffn_norm"] + anchor, dh2, dy, "ffn_norm_bwd")
    anchor = on_grad(_matmul(sv["mix"], dxmb, "tn", BF16, "out_proj_dw", tm=1024, tn=2048, tk=1024))
    dmix = _matmul(dxmb, w.get("w_out"), "nt", F32, "out_proj_dx", tm=1024, tn=512, tk=2048)
    doa, dob, doc, g_mix_gain, dsum_a, dsum_c = _mix_bwd(dmix, sv["oa"], sv["ob"], sv["oc"], p["mix_gain"] + anchor,
                                                         cfg, "mix_bwd")
    daq, dak, dav, dbias_a, g_sinks = _banded_bwd(aq, ak, sv["akt"], av, sv["lse_a"], dsum_a, doa, p["bias_a"],
                                                 p["sinks"], dbias[0], cfg.nha, cfg.nkva, WINDOW_A - 1, 1, "swa_bwd")
    dbq, dbk, dbv = _sb_bwd(bq, bk, sv["bkt"], bv, sv["tot_b"], dob, "stickbreak_bwd")
    dcq, dck, dcv, dbias_c = [], [], [], []
    for idx, ((window, dil), bias) in enumerate(zip(DILATED_PAIRS, p["bias_c"])):
        a, b, c, d, _ = _banded_bwd(cq, ck, sv["ckts"][idx], cv, sv["lse_c"], dsum_c, doc, bias, None, dbias[1][idx],
                                    cfg.nhc, cfg.nhc, window // dil, dil, f"dilated{dil}_bwd")
        dcq.append(a)
        dck.append(b)
        dcv.append(c)
        dbias_c.append(d)
    dproj, g_aq, g_ak, g_cq, g_ck = _qk_prep_bwd(
        sv["proj"], p["gains"], [[daq], [dak], [dav], [dbq], [dbk], [dbv], dcq, dck, dcv], cfg, "qk_prep_bwd")
    anchor = on_grad(_matmul(dproj, sv["h1"], "tn", BF16, "in_proj_dw", tm=768, tn=2048, tk=1024))
    dh1 = _matmul(dproj, w.get("w_in_t"), "nn", F32, "in_proj_dx", tm=1024, tn=2048, tk=768)
    dx, dxb, g_attn_norm = _rmsnorm_bwd(sv["x"], p["attn_norm"] + anchor, dh1, dxm, "attn_norm_bwd")

    def fold(g):
        return jnp.sum(g.reshape(-1, HEAD_DIM), axis=0)

    small = dict(attn_norm=g_attn_norm[0], a_q_gain=fold(g_aq), a_k_gain=fold(g_ak), a_sinks=g_sinks,
                 c_q_gain=fold(g_cq), c_k_gain=fold(g_ck), mix_out_gain=g_mix_gain[0], ffn_norm=g_ffn_norm[0],
                 conv_w=jnp.concatenate([dwg, dwu], axis=1), conv_b=jnp.concatenate([dbg, dbu], axis=1)[0])
    return dx, dxb, small, (dbias_a, dbias_c)


_SMALL = ("attn_norm", "a_q_gain", "a_k_gain", "a_sinks", "c_q_gain", "c_k_gain", "rel_bias_table", "mix_out_gain",
          "ffn_norm", "conv_w", "conv_b")


def _pack(arrays):
    flat = jnp.concatenate([a.reshape(-1).astype(F32) for a in arrays])
    pad = (-flat.shape[0]) % (8 * LANES)
    return jnp.pad(flat, (0, pad)).reshape(-1, LANES)


def _unpack(flat, shapes):
    flat = flat.reshape(-1)
    out, pos = [], 0
    for sh in shapes:
        n = int(np.prod(sh))
        out.append(flat[pos:pos + n].reshape(sh))
        pos += n
    return out


def kernel(x, attn_norm, w_in, a_q_gain, a_k_gain, a_sinks, c_q_gain, c_k_gain, rel_bias_table, mix_out_gain, w_out, ffn_norm, w_up, conv_w, conv_b, w_down, loss_target, m_attn_norm, m_w_in, m_a_q_gain, m_a_k_gain, m_a_sinks, m_c_q_gain, m_c_k_gain, m_rel_bias_table, m_mix_out_gain, m_w_out, m_ffn_norm, m_w_up, m_conv_w, m_conv_b, m_w_down, v_attn_norm, v_w_in, v_a_q_gain, v_a_k_gain, v_a_sinks, v_c_q_gain, v_c_k_gain, v_rel_bias_table, v_mix_out_gain, v_w_out, v_ffn_norm, v_w_up, v_conv_w, v_conv_b, v_w_down):
    depth, d = attn_norm.shape
    f = w_down.shape[1] * N_CHIPS
    cfg = _Cfg(d, f)
    chip = 2 * lax.axis_index("x") + lax.axis_index("y")

    cw_cols = conv_w.shape[2]
    cw_flat = conv_w.reshape(-1)
    cw_rows = -(-cw_flat.shape[0] // (16 * LANES)) * 16
    cw_pad = jnp.pad(cw_flat, (0, cw_rows * LANES - cw_flat.shape[0])).reshape(cw_rows, LANES)

    table_a, table_c = rel_bias_table[:, :cfg.nha], rel_bias_table[:, cfg.nha:]
    bias_a = _block_bias(table_a, 1)
    bias_c = [_block_bias(table_c, dil) for _, dil in DILATED_PAIRS]

    layers, anchor = [], 0.0
    for l in range(depth):
        shards = [w_in[l].T.astype(BF16), w_out[l].astype(BF16), w_up[l].T.astype(BF16), w_down[l].astype(BF16)]
        names = ["w_in_t", "w_out", "w_up_t", "w_down"]
        if l == 0:
            todo = [([cw_pad, shards[0]], ["conv_w", names[0]])] + [([s], [n]) for s, n in zip(shards[1:], names[1:])]
        else:
            todo = [(shards, names)]
        groups = []
        for k, (srcs, group_names) in enumerate(todo):
            lands = [_own_slot((N_CHIPS,) + s.shape, s.dtype, s, chip) for s in srcs]
            handle, token = _split_start([], lands, _gather_plan, 3 * len(lands), f"gather_start_{l}_{k}")
            anchor = anchor + token[0, 0]
            groups.append((handle, group_names, f"gather_wait_{l}_{k}"))
        layers.append(dict(
            attn_norm=attn_norm[l].reshape(1, d), ffn_norm=ffn_norm[l].reshape(1, d),
            mix_gain=mix_out_gain[l].reshape(1, d),
            gains=(_tile_gain(a_q_gain[l], cfg.nha), _tile_gain(a_k_gain[l], cfg.nkva),
                   _tile_gain(c_q_gain[l], cfg.nhc), _tile_gain(c_k_gain[l], cfg.nhc)),
            sinks=a_sinks[l], bias_a=bias_a, bias_c=bias_c, conv_b=conv_b[l].reshape(1, 2 * f),
            weights=_Gathered(groups)))
    cw_all = layers[0]["weights"].get("conv_w", layers[0]["attn_norm"] + anchor)
    cw_all = cw_all.reshape(N_CHIPS, -1)[:, :cw_flat.shape[0]].reshape(N_CHIPS, depth, CONV_WIDTH, cw_cols)
    conv_w_full = jnp.transpose(cw_all, (1, 2, 0, 3)).reshape(depth, CONV_WIDTH, N_CHIPS * cw_cols)
    for l in range(depth):
        layers[l]["conv_w"] = conv_w_full[l]

    act = x[0]
    saved = []
    for l in range(depth):
        act, sv = _layer_fwd(act, layers[l], cfg)
        saved.append(sv)
    dact, dactb, loss_blk = _loss_head(act, loss_target[0], "loss_head")
    loss = lax.psum(loss_blk[0, 0], ("x", "y", "c"))

    core = lax.axis_index("c")

    def start_scatter(grads, name):
        srcs = [g.reshape(N_CHIPS, -1, g.shape[-1]) for g in grads]
        lands = []
        for g in srcs:
            half = g.shape[1] // 2
            own = lax.dynamic_slice(g, (chip, core * half, 0), (1, half, g.shape[2]))[0]
            lands.append(_own_slot((N_DEVICES, half, g.shape[2]), g.dtype, own, 2 * chip + core))
        return _split_start(srcs, lands, _scatter_plan, (N_DEVICES - 1) * len(srcs), name)

    def finish_scatter(l, handles, after):
        parts = [pt for k, h in enumerate(handles) for pt in _split_wait(h, after, f"scatter_wait_{l}_{k}")][::-1]
        halves = [_sum_parts(pt, f"sum_grads_{t}") for t, pt in enumerate(parts)]
        lands = [_own_slot((2,) + h.shape, h.dtype, h, core) for h in halves]
        return _split_start(halves, lands, _swap_plan, len(halves), f"swap_start_{l}")[0]

    dbias = (jnp.zeros_like(bias_a), [jnp.zeros_like(b) for b in bias_c])
    small_grads = [None] * depth
    swaps = [None] * depth
    pending = None
    for l in reversed(range(depth)):
        made = []

        def on_grad(g, l=l, made=made):
            if l:
                made.append(g)
                return 0.0
            handle, token = start_scatter([g], f"scatter_start_0_{len(made)}")
            made.append(handle)
            return token[0, 0]

        dact, dactb, small_grads[l], dbias = _layer_bwd(dact, dactb, saved[l], layers[l], dbias, cfg, on_grad)
        if pending is not None:
            swaps[l + 1] = finish_scatter(l + 1, pending, dact)
        if l:
            handle, token = start_scatter(made, f"scatter_start_{l}")
            pending = [handle]
            layers[l - 1]["conv_b"] = layers[l - 1]["conv_b"] + token[0, 0]
        else:
            pending = made
    grad_x = dact[None]

    tabs = _bias_table_grad([dbias[0]] + dbias[1], [jnp.asarray(_bucket_ids(1))]
                            + [jnp.asarray(_bucket_ids(dil)) for _, dil in DILATED_PAIRS], "bias_table_grad")
    g_table_a = tabs[0][:, :N_BUCKETS].T
    g_table_c = (tabs[1] + tabs[2] + tabs[3])[:, :N_BUCKETS].T
    g_table = jnp.concatenate([g_table_a, g_table_c], axis=1)
    small_local = {k: jnp.stack([small_grads[l][k] for l in range(depth)]) for k in _SMALL if k != "rel_bias_table"}
    small_local["rel_bias_table"] = g_table
    shapes = [small_local[k].shape for k in _SMALL]
    reduced = dict(zip(_SMALL, _unpack(_allreduce_small(_pack([small_local[k] for k in _SMALL]), "allreduce_small"),
                                       shapes)))
    reduced["conv_w"] = lax.dynamic_slice_in_dim(reduced["conv_w"], chip * cw_cols, cw_cols, axis=2)

    given = dict(attn_norm=attn_norm, a_q_gain=a_q_gain, a_k_gain=a_k_gain, a_sinks=a_sinks, c_q_gain=c_q_gain,
                 c_k_gain=c_k_gain, rel_bias_table=rel_bias_table, mix_out_gain=mix_out_gain, ffn_norm=ffn_norm,
                 conv_w=conv_w, conv_b=conv_b)
    moms = dict(attn_norm=(m_attn_norm, v_attn_norm), a_q_gain=(m_a_q_gain, v_a_q_gain),
                a_k_gain=(m_a_k_gain, v_a_k_gain), a_sinks=(m_a_sinks, v_a_sinks), c_q_gain=(m_c_q_gain, v_c_q_gain),
                c_k_gain=(m_c_k_gain, v_c_k_gain), rel_bias_table=(m_rel_bias_table, v_rel_bias_table),
                mix_out_gain=(m_mix_out_gain, v_mix_out_gain), ffn_norm=(m_ffn_norm, v_ffn_norm),
                conv_w=(m_conv_w, v_conv_w), conv_b=(m_conv_b, v_conv_b))
    sshapes = [given[k].shape for k in _SMALL]
    s_delta, s_m, s_v = _adamw(_pack([given[k] for k in _SMALL]), _pack([reduced[k] for k in _SMALL]),
                               _pack([moms[k][0] for k in _SMALL]), _pack([moms[k][1] for k in _SMALL]), "adamw_small")
    grads = dict(reduced)
    deltas = dict(zip(_SMALL, _unpack(s_delta, sshapes)))
    new_m = dict(zip(_SMALL, _unpack(s_m, sshapes)))
    new_v = dict(zip(_SMALL, _unpack(s_v, sshapes)))

    big_given = dict(w_in=(w_in, m_w_in, v_w_in, True), w_out=(w_out, m_w_out, v_w_out, False),
                     w_up=(w_up, m_w_up, v_w_up, True), w_down=(w_down, m_w_down, v_w_down, False))
    names = ("w_in", "w_out", "w_up", "w_down")
    bufs = {name: None for name in names}
    after = s_delta
    for l in reversed(range(depth)):
        if l == 0:
            swaps[0] = finish_scatter(0, pending, after)
        layer_grads = [g.reshape(-1, g.shape[-1]) for g in _split_wait(swaps[l], after, f"swap_wait_{l}")]
        for t, name in enumerate(names):
            wt, mt, vt, transposed = big_given[name]
            g = layer_grads[t].T if transposed else layer_grads[t]
            bufs[name] = _adamw_layer(l, wt, g, mt, vt, bufs[name], f"adamw_{name}_{l}")
            after = bufs[name][1]
    for name in names:
        grads[name], deltas[name], new_m[name], new_v[name] = bufs[name]

    order = ("attn_norm", "w_in", "a_q_gain", "a_k_gain", "a_sinks", "c_q_gain", "c_k_gain", "rel_bias_table",
             "mix_out_gain", "w_out", "ffn_norm", "w_up", "conv_w", "conv_b", "w_down")
    return (loss, grad_x, *[grads[k] for k in order], *[deltas[k] for k in order], *[new_m[k] for k in order],
            *[new_v[k] for k in order])
```

```python
import numpy as np
import jax
import jax.numpy as jnp
from jax import lax
from jax.experimental import pallas as pl
from jax.experimental.pallas import tpu as pltpu

F32 = jnp.float32
BF16 = jnp.bfloat16
MESH = pl.DeviceIdType.MESH

HEAD_DIM = 64
BLOCK = 128
LANES = 128
EPS = 1e-6
NEG_INF = -1e30
WINDOW_A = 128
DILATED_PAIRS = ((128, 1), (512, 4), (2048, 16))
N_BUCKETS = 32
T5_MAX_DIST = 2048
CONV_WIDTH = 3
ADAM_LR = 0.001
ADAM_B1 = 0.9
ADAM_B2 = 0.999
ADAM_EPS = 1e-08
ADAM_WD = 0.01
ADAM_STEP = 10
N_CHIPS = 4
N_DEVICES = 8
VMEM_LIMIT_BYTES = 48 * 1024 * 1024
QK_SCALE = HEAD_DIM ** -0.5


def _params(sem=None):
    return pltpu.CompilerParams(dimension_semantics=sem, vmem_limit_bytes=VMEM_LIMIT_BYTES)


def _div_tile(n, cap, mult):
    best = None
    for t in range(mult, min(n, cap) + 1, mult):
        if n % t == 0:
            best = t
    return n if best is None else best


def _dot(a, b):
    return lax.dot_general(a, b, (((1,), (0,)), ((), ())), preferred_element_type=F32)


def _dot_nt(a, b):
    return lax.dot_general(a, b, (((1,), (1,)), ((), ())), preferred_element_type=F32)


def _dot_tn(a, b):
    return lax.dot_general(a, b, (((0,), (0,)), ((), ())), preferred_element_type=F32)


def _split_dot(x, m):
    hi = x.astype(BF16)
    lo = (x - hi.astype(F32)).astype(BF16)
    return _dot(hi, m) + _dot(lo, m)


class _Cfg:
    def __init__(self, d_model, d_ff):
        nh = d_model // HEAD_DIM
        self.d = d_model
        self.f = d_ff
        self.nha = nh // 4
        self.nkva = self.nha // 4
        self.nhb = nh // 4
        self.nhc = nh // 2
        self.a_q = self.nha * HEAD_DIM
        self.a_kv = self.nkva * HEAD_DIM
        self.b_w = self.nhb * HEAD_DIM
        self.c_w = self.nhc * HEAD_DIM
        sizes = [self.a_q, self.a_kv, self.a_kv, self.b_w, self.b_w, self.b_w, self.c_w, self.c_w, self.c_w]
        starts = [0] + [int(s) for s in np.cumsum(sizes)[:-1]]
        self.sections = list(zip(starts, sizes))
        self.in_width = int(sum(sizes))
        assert all(s % LANES == 0 for s in sizes)


def _matmul(a, b, mode, out_dtype, name, tm=512, tn=512, tk=512, residual=None):
    if mode == "tn":
        kdim, m = a.shape
    else:
        m, kdim = a.shape
    n = b.shape[0] if mode == "nt" else b.shape[1]
    tm, tn, tk = _div_tile(m, tm, LANES), _div_tile(n, tn, LANES), _div_tile(kdim, tk, LANES)
    nk = kdim // tk
    if mode == "tn":
        a_spec = pl.BlockSpec((tk, tm), lambda i, j, k: (k, i))
    else:
        a_spec = pl.BlockSpec((tm, tk), lambda i, j, k: (i, k))
    if mode == "nt":
        b_spec = pl.BlockSpec((tn, tk), lambda i, j, k: (j, k))
    else:
        b_spec = pl.BlockSpec((tk, tn), lambda i, j, k: (k, j))
    dot = {"nn": _dot, "nt": _dot_nt, "tn": _dot_tn}[mode]
    o_spec = pl.BlockSpec((tm, tn), lambda i, j, k: (i, j))
    in_specs = [a_spec, b_spec]
    args = [a, b]
    if residual is not None:
        in_specs.append(o_spec)
        args.append(residual)

    def body(*refs):
        if residual is None:
            a_ref, b_ref, o_ref, acc = refs
        else:
            a_ref, b_ref, r_ref, o_ref, acc = refs
        k = pl.program_id(2)

        @pl.when(k == 0)
        def _():
            acc[...] = jnp.zeros_like(acc)

        acc[...] += dot(a_ref[...].astype(BF16), b_ref[...].astype(BF16))

        @pl.when(k == nk - 1)
        def _():
            r = acc[...]
            if residual is not None:
                r = r + r_ref[...]
            o_ref[...] = r.astype(out_dtype)

    return pl.pallas_call(
        body, name=name, grid=(m // tm, n // tn, nk), in_specs=in_specs, out_specs=o_spec,
        out_shape=jax.ShapeDtypeStruct((m, n), out_dtype), scratch_shapes=[pltpu.VMEM((tm, tn), F32)],
        compiler_params=_params(("parallel", "parallel", "arbitrary")),
    )(*args)


def _rmsnorm_fwd(x, g, name):
    s, d = x.shape
    ts = _div_tile(s, 256, 8)

    def body(x_ref, g_ref, o_ref):
        xv = x_ref[...]
        r = lax.rsqrt(jnp.mean(xv * xv, axis=-1, keepdims=True) + EPS)
        o_ref[...] = (xv * r * g_ref[...]).astype(BF16)

    return pl.pallas_call(
        body, name=name, grid=(s // ts,),
        in_specs=[pl.BlockSpec((ts, d), lambda i: (i, 0)), pl.BlockSpec((1, d), lambda i: (0, 0))],
        out_specs=pl.BlockSpec((ts, d), lambda i: (i, 0)), out_shape=jax.ShapeDtypeStruct((s, d), BF16),
        compiler_params=_params(("parallel",)),
    )(x, g)


def _rmsnorm_bwd(x, g, dh, dres, name):
    s, d = x.shape
    ts = _div_tile(s, 256, 16)

    def body(x_ref, g_ref, dh_ref, dres_ref, dx_ref, dxb_ref, dg_ref):
        @pl.when(pl.program_id(0) == 0)
        def _():
            dg_ref[...] = jnp.zeros_like(dg_ref)

        xv = x_ref[...]
        r = lax.rsqrt(jnp.mean(xv * xv, axis=-1, keepdims=True) + EPS)
        xhat = xv * r
        dhv = dh_ref[...]
        dxhat = dhv * g_ref[...]
        dx = dres_ref[...] + r * (dxhat - xhat * jnp.mean(dxhat * xhat, axis=-1, keepdims=True))
        dx_ref[...] = dx
        dxb_ref[...] = dx.astype(BF16)
        dg_ref[...] += jnp.sum(dhv * xhat, axis=0, keepdims=True)

    row = pl.BlockSpec((ts, d), lambda i: (i, 0))
    vec = pl.BlockSpec((1, d), lambda i: (0, 0))
    return pl.pallas_call(
        body, name=name, grid=(s // ts,), in_specs=[row, vec, row, row], out_specs=[row, row, vec],
        out_shape=[jax.ShapeDtypeStruct((s, d), F32), jax.ShapeDtypeStruct((s, d), BF16),
                   jax.ShapeDtypeStruct((1, d), F32)],
        compiler_params=_params(("arbitrary",)),
    )(x, g, dh, dres)


def _head_mean_matrix():
    idx = np.arange(LANES) // HEAD_DIM
    return jnp.asarray((idx[:, None] == idx[None, :]).astype(np.float32) / HEAD_DIM, dtype=BF16)


def _head_mean(y, m128):
    w = y.shape[1]
    parts = [_split_dot(y[:, c * LANES:(c + 1) * LANES], m128) for c in range(w // LANES)]
    return parts[0] if len(parts) == 1 else jnp.concatenate(parts, axis=1)


_NORMED_SECTIONS = (0, 1, 6, 7)
_QUERY_SECTIONS = (0, 3, 6)


def _qk_prep(proj, gains, cfg, name):
    s = proj.shape[0]
    ts = _div_tile(s, 256, 16)
    m128 = _head_mean_matrix()

    def body(p_ref, m_ref, g0, g1, g6, g7, *outs):
        gref = dict(zip(_NORMED_SECTIONS, (g0, g1, g6, g7)))
        for idx, (st, w) in enumerate(cfg.sections):
            xv = p_ref[:, st:st + w]
            if idx in gref:
                r = lax.rsqrt(_head_mean(xv * xv, m_ref[...]) + EPS)
                xv = xv * r * gref[idx][...]
            if idx in _QUERY_SECTIONS:
                xv = xv * QK_SCALE
            outs[idx][...] = xv.astype(BF16)

    in_specs = [pl.BlockSpec((ts, cfg.in_width), lambda i: (i, 0)), pl.BlockSpec((LANES, LANES), lambda i: (0, 0))]
    in_specs += [pl.BlockSpec((1, cfg.sections[k][1]), lambda i: (0, 0)) for k in _NORMED_SECTIONS]
    out_specs = [pl.BlockSpec((ts, w), lambda i: (i, 0)) for _, w in cfg.sections]
    out_shape = [jax.ShapeDtypeStruct((s, w), BF16) for _, w in cfg.sections]
    return pl.pallas_call(
        body, name=name, grid=(s // ts,), in_specs=in_specs, out_specs=out_specs, out_shape=out_shape,
        compiler_params=_params(("parallel",)),
    )(proj, m128, *gains)


def _qk_prep_bwd(proj, gains, grads, cfg, name):
    s = proj.shape[0]
    ts = _div_tile(s, 128, 16)
    m128 = _head_mean_matrix()
    counts = [len(gl) for gl in grads]
    flat = [g for gl in grads for g in gl]

    def body(*refs):
        p_ref, m_ref = refs[0], refs[1]
        gref = dict(zip(_NORMED_SECTIONS, refs[2:6]))
        g_in = refs[6:6 + len(flat)]
        dp_ref = refs[6 + len(flat)]
        dgain = dict(zip(_NORMED_SECTIONS, refs[7 + len(flat):]))

        @pl.when(pl.program_id(0) == 0)
        def _():
            for k in _NORMED_SECTIONS:
                dgain[k][...] = jnp.zeros_like(dgain[k])

        pos = 0
        for idx, (st, w) in enumerate(cfg.sections):
            dy = g_in[pos][...].astype(F32)
            for extra in g_in[pos + 1:pos + counts[idx]]:
                dy = dy + extra[...].astype(F32)
            pos += counts[idx]
            if idx in gref:
                xv = p_ref[:, st:st + w]
                r = lax.rsqrt(_head_mean(xv * xv, m_ref[...]) + EPS)
                xhat = xv * r
                dxhat = dy * gref[idx][...]
                dgain[idx][...] += jnp.sum(dy * xhat, axis=0, keepdims=True)
                dy = r * (dxhat - xhat * _head_mean(dxhat * xhat, m_ref[...]))
            dp_ref[:, st:st + w] = dy.astype(BF16)

    in_specs = [pl.BlockSpec((ts, cfg.in_width), lambda i: (i, 0)), pl.BlockSpec((LANES, LANES), lambda i: (0, 0))]
    in_specs += [pl.BlockSpec((1, cfg.sections[k][1]), lambda i: (0, 0)) for k in _NORMED_SECTIONS]
    for idx, (_, w) in enumerate(cfg.sections):
        in_specs += [pl.BlockSpec((ts, w), lambda i: (i, 0))] * counts[idx]
    out_specs = [pl.BlockSpec((ts, cfg.in_width), lambda i: (i, 0))]
    out_specs += [pl.BlockSpec((1, cfg.sections[k][1]), lambda i: (0, 0)) for k in _NORMED_SECTIONS]
    out_shape = [jax.ShapeDtypeStruct((s, cfg.in_width), BF16)]
    out_shape += [jax.ShapeDtypeStruct((1, cfg.sections[k][1]), F32) for k in _NORMED_SECTIONS]
    return pl.pallas_call(
        body, name=name, grid=(s // ts,), in_specs=in_specs, out_specs=out_specs, out_shape=out_shape,
        compiler_params=_params(("arbitrary",)),
    )(proj, m128, *gains, *flat)


def _band_masks(max_dist):
    row = lax.broadcasted_iota(jnp.int32, (BLOCK, BLOCK), 0)
    col = lax.broadcasted_iota(jnp.int32, (BLOCK, BLOCK), 1)
    return row + BLOCK - col <= max_dist, col <= row


def _dilated_t(a, dil):
    s, w = a.shape
    return _keys_on_lanes(a.reshape(s // dil, dil * w), BLOCK)


def _undilated(at, dil):
    nblk, dw, _ = at.shape
    return jnp.transpose(at, (0, 2, 1)).reshape(nblk * BLOCK * dil, dw // dil)


def _banded_fwd(q, kt, v, bias, sinks, hq, hk, max_dist, dil, name):
    s = q.shape[0]
    wq, wk, sd, grp = hq * HEAD_DIM, hk * HEAD_DIM, s // dil, hq // hk
    nb = sd // BLOCK
    has_sink = sinks is not None

    def body(*refs):
        if has_sink:
            q_ref, ktp_ref, ktc_ref, vp_ref, vc_ref, b_ref, s_ref, o_ref, l_ref = refs
        else:
            q_ref, ktp_ref, ktc_ref, vp_ref, vc_ref, b_ref, o_ref, l_ref = refs
        i = pl.program_id(1)
        mprev, mcur = _band_masks(max_dist)
        mask = jnp.concatenate([jnp.logical_and(mprev, i > 0), mcur], axis=1)
        for h in range(hq):
            sq = slice(h * HEAD_DIM, (h + 1) * HEAD_DIM)
            sk = slice((h // grp) * HEAD_DIM, (h // grp + 1) * HEAD_DIM)
            kt = jnp.concatenate([ktp_ref[sk, :], ktc_ref[sk, :]], axis=1)
            vv = jnp.concatenate([vp_ref[:, sk], vc_ref[:, sk]], axis=0)
            sc = jnp.where(mask, _dot(q_ref[:, sq], kt) + b_ref[h], NEG_INF)
            m = jnp.max(sc, axis=-1, keepdims=True)
            if has_sink:
                m = jnp.maximum(m, s_ref[h])
            p = jnp.exp(sc - m)
            den = jnp.sum(p, axis=-1, keepdims=True)
            if has_sink:
                den = den + jnp.exp(s_ref[h] - m)
            o_ref[:, sq] = _dot(p.astype(BF16), vv) / den
            l_ref[:, h:h + 1] = m + jnp.log(den)

    qspec = pl.BlockSpec((BLOCK, wq), lambda r, i: (i, r))
    kprev = pl.BlockSpec((BLOCK, wk), lambda r, i: (jnp.maximum(i - 1, 0), r))
    kcur = pl.BlockSpec((BLOCK, wk), lambda r, i: (i, r))
    ktprev = pl.BlockSpec((None, wk, BLOCK), lambda r, i: (jnp.maximum(i - 1, 0), r, 0))
    ktcur = pl.BlockSpec((None, wk, BLOCK), lambda r, i: (i, r, 0))
    in_specs = [qspec, ktprev, ktcur, kprev, kcur, pl.BlockSpec((hq, BLOCK, 2 * BLOCK), lambda r, i: (0, 0, 0))]
    v2 = v.reshape(sd, dil * wk)
    args = [q.reshape(sd, dil * wq), kt, kt, v2, v2, bias]
    if has_sink:
        in_specs.append(pl.BlockSpec(memory_space=pltpu.SMEM))
        args.append(sinks)
    out, lse = pl.pallas_call(
        body, name=name, grid=(dil, nb), in_specs=in_specs,
        out_specs=[qspec, pl.BlockSpec((None, BLOCK, hq), lambda r, i: (r, i, 0))],
        out_shape=[jax.ShapeDtypeStruct((sd, dil * wq), F32), jax.ShapeDtypeStruct((dil, sd, hq), F32)],
        compiler_params=_params(("parallel", "parallel")),
    )(*args)
    return out.reshape(s, wq), jnp.transpose(lse, (1, 0, 2)).reshape(s, hq)


def _per_head_dilated(a, dil):
    s, h = a.shape
    return jnp.transpose(a.reshape(s // dil, dil, h), (1, 0, 2))


def _banded_bwd(q, k, kt, v, lse, dsum, do, bias, sinks, dbias_init, hq, hk, max_dist, dil, name):
    s = q.shape[0]
    wq, wk, sd, grp = hq * HEAD_DIM, hk * HEAD_DIM, s // dil, hq // hk
    nb = sd // BLOCK
    has_sink = sinks is not None

    def body(*refs):
        (q_ref, qn_ref, qt_ref, qtn_ref, kp_ref, kc_ref, ktp_ref, ktc_ref, vtp_ref, vtc_ref, l_ref, ln_ref, d_ref,
         dn_ref, do_ref, don_ref, dot_ref, dotn_ref, b_ref, dbi_ref) = refs[:20]
        rest = refs[20:]
        if has_sink:
            s_ref, dq_ref, dkt_ref, dvt_ref, db_ref, ds_ref = rest
        else:
            dq_ref, dkt_ref, dvt_ref, db_ref = rest
        j = pl.program_id(1)

        @pl.when(jnp.logical_and(pl.program_id(0) == 0, j == 0))
        def _():
            db_ref[...] = dbi_ref[...]
            if has_sink:
                ds_ref[...] = jnp.zeros_like(ds_ref)

        mprev_static, mcur = _band_masks(max_dist)
        mask = jnp.concatenate([jnp.logical_and(mprev_static, j > 0), mcur], axis=1)
        mnext = jnp.logical_and(mprev_static, j + 1 < nb)
        dkt_acc = [jnp.zeros((HEAD_DIM, BLOCK), F32) for _ in range(hk)]
        dvt_acc = [jnp.zeros((HEAD_DIM, BLOCK), F32) for _ in range(hk)]
        for h in range(hq):
            g = h // grp
            sq = slice(h * HEAD_DIM, (h + 1) * HEAD_DIM)
            sk = slice(g * HEAD_DIM, (g + 1) * HEAD_DIM)
            kt2 = jnp.concatenate([ktp_ref[sk, :], ktc_ref[sk, :]], axis=1)
            vt2 = jnp.concatenate([vtp_ref[sk, :], vtc_ref[sk, :]], axis=1)
            k2 = jnp.concatenate([kp_ref[:, sk], kc_ref[:, sk]], axis=0)
            lcol = l_ref[:, h:h + 1]
            dcol = d_ref[:, h:h + 1]
            sc = _dot(q_ref[:, sq], kt2) + b_ref[h]
            p = jnp.where(mask, jnp.exp(sc - lcol), 0.0)
            ds = p * (_dot(do_ref[:, sq], vt2) - dcol)
            dsb = ds.astype(BF16)
            dq_ref[:, sq] = (_dot(dsb, k2) * QK_SCALE).astype(BF16)
            db_ref[h] += ds
            if has_sink:
                psink = jnp.exp(s_ref[h] - lcol)
                tot = jnp.sum(psink * dcol, axis=0, keepdims=True)
                ds_ref[h:h + 1, :] -= jnp.broadcast_to(tot, (1, LANES))
            lncol = ln_ref[:, h:h + 1]
            dncol = dn_ref[:, h:h + 1]
            sn = _dot(qn_ref[:, sq], ktc_ref[sk, :]) + b_ref[h, :, 0:BLOCK]
            pn = jnp.where(mnext, jnp.exp(sn - lncol), 0.0)
            dsn = pn * (_dot(don_ref[:, sq], vtc_ref[sk, :]) - dncol)
            dkt_acc[g] = dkt_acc[g] + (_dot(qt_ref[sq, :], dsb[:, BLOCK:]) + _dot(qtn_ref[sq, :], dsn.astype(BF16)))
            dvt_acc[g] = dvt_acc[g] + (_dot(dot_ref[sq, :], p[:, BLOCK:].astype(BF16))
                                       + _dot(dotn_ref[sq, :], pn.astype(BF16)))
        for g in range(hk):
            sk = slice(g * HEAD_DIM, (g + 1) * HEAD_DIM)
            dkt_ref[sk, :] = dkt_acc[g].astype(BF16)
            dvt_ref[sk, :] = dvt_acc[g].astype(BF16)

    qcur = pl.BlockSpec((BLOCK, wq), lambda r, j: (j, r))
    qnext = pl.BlockSpec((BLOCK, wq), lambda r, j: (jnp.minimum(j + 1, nb - 1), r))
    qtcur = pl.BlockSpec((None, wq, BLOCK), lambda r, j: (j, r, 0))
    qtnext = pl.BlockSpec((None, wq, BLOCK), lambda r, j: (jnp.minimum(j + 1, nb - 1), r, 0))
    kprev = pl.BlockSpec((BLOCK, wk), lambda r, j: (jnp.maximum(j - 1, 0), r))
    kcur = pl.BlockSpec((BLOCK, wk), lambda r, j: (j, r))
    ktprev = pl.BlockSpec((None, wk, BLOCK), lambda r, j: (jnp.maximum(j - 1, 0), r, 0))
    ktcur = pl.BlockSpec((None, wk, BLOCK), lambda r, j: (j, r, 0))
    bspec = pl.BlockSpec((hq, BLOCK, 2 * BLOCK), lambda r, j: (0, 0, 0))
    hcur = pl.BlockSpec((None, BLOCK, hq), lambda r, j: (r, j, 0))
    hnext = pl.BlockSpec((None, BLOCK, hq), lambda r, j: (r, jnp.minimum(j + 1, nb - 1), 0))
    dob = do.astype(BF16)
    q2, k2, do2 = q.reshape(sd, dil * wq), k.reshape(sd, dil * wk), dob.reshape(sd, dil * wq)
    l3, d3 = _per_head_dilated(lse, dil), _per_head_dilated(dsum, dil)
    qt, vt, dot = _dilated_t(q, dil), _dilated_t(v, dil), _dilated_t(dob, dil)
    in_specs = [qcur, qnext, qtcur, qtnext, kprev, kcur, ktprev, ktcur, ktprev, ktcur, hcur, hnext, hcur, hnext,
                qcur, qnext, qtcur, qtnext, bspec, bspec]
    args = [q2, q2, qt, qt, k2, k2, kt, kt, vt, vt, l3, l3, d3, d3, do2, do2, dot, dot, bias, dbias_init]
    out_specs = [qcur, ktcur, ktcur, bspec]
    out_shape = [jax.ShapeDtypeStruct((sd, dil * wq), BF16), jax.ShapeDtypeStruct((nb, dil * wk, BLOCK), BF16),
                 jax.ShapeDtypeStruct((nb, dil * wk, BLOCK), BF16), jax.ShapeDtypeStruct((hq, BLOCK, 2 * BLOCK), F32)]
    if has_sink:
        in_specs.append(pl.BlockSpec(memory_space=pltpu.SMEM))
        args.append(sinks)
        out_specs.append(pl.BlockSpec((hq, LANES), lambda r, j: (0, 0)))
        out_shape.append(jax.ShapeDtypeStruct((hq, LANES), F32))
    res = pl.pallas_call(
        body, name=name, grid=(dil, nb), in_specs=in_specs, out_specs=out_specs, out_shape=out_shape,
        compiler_params=_params(("arbitrary", "arbitrary")),
    )(*args)
    dq, dk, dv, dbias = res[0].reshape(s, wq), _undilated(res[1], dil), _undilated(res[2], dil), res[3]
    return dq, dk, dv, dbias, (res[4][:, 0] if has_sink else None)


def _neg_softplus(z):
    return -(jnp.maximum(z, 0.0) + jnp.log(1.0 + jnp.exp(-jnp.abs(z))))


SB_CHUNK = 256
HEADS_PER_PAIR = LANES // HEAD_DIM


def _tri(kind):
    row = lax.broadcasted_iota(jnp.int32, (SB_CHUNK, SB_CHUNK), 0)
    col = lax.broadcasted_iota(jnp.int32, (SB_CHUNK, SB_CHUNK), 1)
    return {"ge": row >= col, "lt": row < col, "le": row <= col}[kind].astype(BF16)


def _keys_on_lanes(a, rows):
    s, w = a.shape
    return jnp.transpose(a.reshape(s // rows, rows, w), (0, 2, 1))


def _sb_mask(i, jj):
    row = lax.broadcasted_iota(jnp.int32, (BLOCK, SB_CHUNK), 0)
    col = lax.broadcasted_iota(jnp.int32, (BLOCK, SB_CHUNK), 1)
    return col < row + (i * BLOCK - jj * SB_CHUNK)


def _sb_trips(i):
    return (i * BLOCK) // (2 * SB_CHUNK) + 1


def _sb_rows(jj, n):
    return pl.ds(pl.multiple_of(jj * SB_CHUNK, SB_CHUNK), n * SB_CHUNK)


def _sb_fwd(q, kt, v, name):
    s, w = q.shape
    npair, nb, nc = w // LANES, s // BLOCK, s // SB_CHUNK

    def body(q_ref, kt_ref, v_ref, o_ref, t_ref):
        i = pl.program_id(1)
        lincl = _tri("ge")
        heads = [slice(hh * HEAD_DIM, (hh + 1) * HEAD_DIM) for hh in range(HEADS_PER_PAIR)]
        qs = [q_ref[:, sl] for sl in heads]

        def trip(t, carry, masked):
            lo, hi = 2 * t, 2 * t + 1
            mlo, mhi = (_sb_mask(i, lo), _sb_mask(i, hi)) if masked else (None, None)

            def keep(m, val):
                return val if m is None else jnp.where(m, val, 0.0)

            new = []
            for hh, sl in enumerate(heads):
                o_acc, rem = carry[hh]
                zhi = _dot(qs[hh], kt_ref[hi, sl, :])
                zlo = _dot(qs[hh], kt_ref[lo, sl, :])
                lrhi = keep(mhi, _neg_softplus(zhi))
                lrlo = keep(mlo, _neg_softplus(zlo))
                tothi = jnp.sum(lrhi, axis=-1, keepdims=True)
                ahi = keep(mhi, jnp.exp(zhi + (rem + _split_dot(lrhi, lincl))))
                alo = keep(mlo, jnp.exp(zlo + (rem + tothi + _split_dot(lrlo, lincl))))
                a = jnp.concatenate([alo, ahi], axis=1).astype(BF16)
                new.append((o_acc + _dot(a, v_ref[_sb_rows(lo, 2), sl]),
                            rem + tothi + jnp.sum(lrlo, axis=-1, keepdims=True)))
            return tuple(new)

        init = tuple((jnp.zeros((BLOCK, HEAD_DIM), F32), jnp.zeros((BLOCK, 1), F32)) for _ in heads)
        trips = _sb_trips(i)
        carry = trip(trips - 1, init, True)
        carry = lax.fori_loop(0, trips - 1, lambda t, cr: trip(trips - 2 - t, cr, False), carry)
        for hh, sl in enumerate(heads):
            o_ref[:, sl] = carry[hh][0]
            t_ref[:, sl] = jnp.broadcast_to(carry[hh][1], (BLOCK, HEAD_DIM))

    qspec = pl.BlockSpec((BLOCK, LANES), lambda p, i: (i, p))
    return pl.pallas_call(
        body, name=name, grid=(npair, nb),
        in_specs=[qspec, pl.BlockSpec((nc, LANES, SB_CHUNK), lambda p, i: (0, p, 0)),
                  pl.BlockSpec((s, LANES), lambda p, i: (0, p))],
        out_specs=[qspec, qspec], out_shape=[jax.ShapeDtypeStruct((s, w), F32)] * 2,
        compiler_params=_params(("parallel", "parallel")),
    )(q, kt, v)


def _sb_bwd(q, k, kt, v, tot, do, name):
    s, w = q.shape
    npair, nb, nc = w // LANES, s // BLOCK, s // SB_CHUNK
    dob = do.astype(BF16)

    def body(q_ref, qt_ref, k_ref, kt_ref, vt_ref, t_ref, do_ref, dot_ref, dq_ref, dkt_ref, dvt_ref):
        i = pl.program_id(1)

        @pl.when(i == 0)
        def _():
            dkt_ref[...] = jnp.zeros_like(dkt_ref)
            dvt_ref[...] = jnp.zeros_like(dvt_ref)

        lbefore = _tri("lt")
        lupto = _tri("le")
        heads = [slice(hh * HEAD_DIM, (hh + 1) * HEAD_DIM) for hh in range(HEADS_PER_PAIR)]
        qs = [q_ref[:, sl] for sl in heads]
        qts = [qt_ref[sl, :] for sl in heads]
        dos = [do_ref[:, sl] for sl in heads]
        dots = [dot_ref[sl, :] for sl in heads]
        totals = [t_ref[:, sl.start:sl.start + 1] for sl in heads]

        def trip(t, carry, masked):
            chunks = (2 * t, 2 * t + 1)
            masks = [_sb_mask(i, jj) if masked else None for jj in chunks]

            def keep(m, val):
                return val if m is None else jnp.where(m, val, 0.0)

            new = []
            for hh, sl in enumerate(heads):
                dq_acc, plr, pg = carry[hh]
                zs = [_dot(qs[hh], kt_ref[jj, sl, :]) for jj in chunks]
                lrs = [keep(m, _neg_softplus(z)) for m, z in zip(masks, zs)]
                lr_sums = [jnp.sum(lr, axis=-1, keepdims=True) for lr in lrs]
                before = [plr, plr + lr_sums[0]]
                avs = [keep(m, jnp.exp(z + (totals[hh] - (b + _split_dot(lr, lbefore)))))
                       for m, z, lr, b in zip(masks, zs, lrs, before)]
                gs = [_dot(dos[hh], vt_ref[jj, sl, :]) * a for jj, a in zip(chunks, avs)]
                g_sums = [jnp.sum(g, axis=-1, keepdims=True) for g in gs]
                upto = [pg, pg + g_sums[0]]
                dzs = [keep(m, g - jnp.exp(z + lr) * (u + _split_dot(g, lupto))).astype(BF16)
                       for m, z, lr, g, u in zip(masks, zs, lrs, gs, upto)]
                for jj, dzb, a in zip(chunks, dzs, avs):
                    dkt_ref[jj, sl, :] += _dot(qts[hh], dzb)
                    dvt_ref[jj, sl, :] += _dot(dots[hh], a.astype(BF16))
                dz2 = jnp.concatenate(dzs, axis=1)
                new.append((dq_acc + _dot(dz2, k_ref[_sb_rows(chunks[0], 2), sl]), plr + lr_sums[0] + lr_sums[1],
                            pg + g_sums[0] + g_sums[1]))
            return tuple(new)

        zero = jnp.zeros((BLOCK, 1), F32)
        init = tuple((jnp.zeros((BLOCK, HEAD_DIM), F32), zero, zero) for _ in heads)
        trips = _sb_trips(i)
        carry = lax.fori_loop(0, trips - 1, lambda t, cr: trip(t, cr, False), init)
        carry = trip(trips - 1, carry, True)
        for hh, sl in enumerate(heads):
            dq_ref[:, sl] = (carry[hh][0] * QK_SCALE).astype(BF16)

    qspec = pl.BlockSpec((BLOCK, LANES), lambda p, i: (i, p))
    qtspec = pl.BlockSpec((None, LANES, BLOCK), lambda p, i: (i, p, 0))
    kspec = pl.BlockSpec((s, LANES), lambda p, i: (0, p))
    ktspec = pl.BlockSpec((nc, LANES, SB_CHUNK), lambda p, i: (0, p, 0))
    dq, dkt, dvt = pl.pallas_call(
        body, name=name, grid=(npair, nb), in_specs=[qspec, qtspec, kspec, ktspec, ktspec, qspec, qspec, qtspec],
        out_specs=[qspec, ktspec, ktspec],
        out_shape=[jax.ShapeDtypeStruct((s, w), BF16)] + [jax.ShapeDtypeStruct((nc, w, SB_CHUNK), F32)] * 2,
        compiler_params=_params(("parallel", "arbitrary")),
    )(q, _keys_on_lanes(q, BLOCK), k, kt, _keys_on_lanes(v, SB_CHUNK), tot, dob, _keys_on_lanes(dob, BLOCK))

    def rows_first(t):
        return jnp.transpose(t, (0, 2, 1)).reshape(s, w)

    return dq, rows_first(dkt), rows_first(dvt)


def _group_norm(xv, g):
    r = lax.rsqrt(jnp.mean(xv * xv, axis=-1, keepdims=True) + EPS)
    return xv * r * g


def _head_spread(nheads):
    return jnp.asarray(np.repeat(np.eye(nheads, dtype=np.float32), HEAD_DIM, axis=1), dtype=BF16)


def _mix_fwd(oa, ob, ocs, lses, gain, cfg, name):
    s = oa.shape[0]
    ts = _div_tile(s, 256, 16)
    aq, bw, cw, nhc = cfg.a_q, cfg.b_w, cfg.c_w, cfg.nhc

    def body(oa_ref, ob_ref, c1, c2, c3, l1, l2, l3, sp_ref, g_ref, mix_ref, oc_ref, lse_ref):
        m = jnp.maximum(jnp.maximum(l1[...], l2[...]), l3[...])
        es = [jnp.exp(l[...] - m) for l in (l1, l2, l3)]
        den = es[0] + es[1] + es[2]
        oc = sum(_split_dot(e / den, sp_ref[...]) * c[...] for e, c in zip(es, (c1, c2, c3)))
        oc_ref[...] = oc
        lse_ref[...] = m + jnp.log(den)
        mix_ref[:, 0:aq] = _group_norm(oa_ref[...], g_ref[:, 0:aq]).astype(BF16)
        mix_ref[:, aq:aq + bw] = _group_norm(ob_ref[...], g_ref[:, aq:aq + bw]).astype(BF16)
        mix_ref[:, aq + bw:] = _group_norm(oc, g_ref[:, aq + bw:]).astype(BF16)

    def row(wd):
        return pl.BlockSpec((ts, wd), lambda i: (i, 0))

    return pl.pallas_call(
        body, name=name, grid=(s // ts,),
        in_specs=[row(aq), row(bw)] + [row(cw)] * 3 + [row(nhc)] * 3
        + [pl.BlockSpec((nhc, cw), lambda i: (0, 0)), pl.BlockSpec((1, cfg.d), lambda i: (0, 0))],
        out_specs=[row(cfg.d), row(cw), row(nhc)],
        out_shape=[jax.ShapeDtypeStruct((s, cfg.d), BF16), jax.ShapeDtypeStruct((s, cw), F32),
                   jax.ShapeDtypeStruct((s, nhc), F32)],
        compiler_params=_params(("parallel",)),
    )(oa, ob, *ocs, *lses, _head_spread(nhc), gain)


def _mix_bwd(dmix, oa, ob, oc, gain, cfg, name):
    s = oa.shape[0]
    ts = _div_tile(s, 256, 8)
    aq, bw, cw = cfg.a_q, cfg.b_w, cfg.c_w

    def body(dm_ref, oa_ref, ob_ref, oc_ref, g_ref, fa_ref, fc_ref, da_ref, db_ref, dc_ref, dg_ref, sa_ref, sc_ref):
        @pl.when(pl.program_id(0) == 0)
        def _():
            dg_ref[...] = jnp.zeros_like(dg_ref)

        for x_ref, dx_ref, lo, hi, fold in ((oa_ref, da_ref, 0, aq, (fa_ref, sa_ref)), (ob_ref, db_ref, aq, aq + bw, None),
                                            (oc_ref, dc_ref, aq + bw, aq + bw + cw, (fc_ref, sc_ref))):
            xv = x_ref[...]
            dy = dm_ref[:, lo:hi]
            r = lax.rsqrt(jnp.mean(xv * xv, axis=-1, keepdims=True) + EPS)
            xhat = xv * r
            dxhat = dy * g_ref[:, lo:hi]
            dx = r * (dxhat - xhat * jnp.mean(dxhat * xhat, axis=-1, keepdims=True))
            dx_ref[...] = dx
            dg_ref[:, lo:hi] += jnp.sum(dy * xhat, axis=0, keepdims=True)
            if fold is not None:
                fold[1][...] = _split_dot(dx * xv, fold[0][...])

    def row(wd):
        return pl.BlockSpec((ts, wd), lambda i: (i, 0))

    vec = pl.BlockSpec((1, cfg.d), lambda i: (0, 0))
    return pl.pallas_call(
        body, name=name, grid=(s // ts,),
        in_specs=[row(cfg.d), row(aq), row(bw), row(cw), vec, pl.BlockSpec((aq, cfg.nha), lambda i: (0, 0)),
                  pl.BlockSpec((cw, cfg.nhc), lambda i: (0, 0))],
        out_specs=[row(aq), row(bw), row(cw), vec, row(cfg.nha), row(cfg.nhc)],
        out_shape=[jax.ShapeDtypeStruct((s, aq), F32), jax.ShapeDtypeStruct((s, bw), F32),
                   jax.ShapeDtypeStruct((s, cw), F32), jax.ShapeDtypeStruct((1, cfg.d), F32),
                   jax.ShapeDtypeStruct((s, cfg.nha), F32), jax.ShapeDtypeStruct((s, cfg.nhc), F32)],
        compiler_params=_params(("arbitrary",)),
    )(dmix, oa, ob, oc, gain, _head_spread(cfg.nha).T, _head_spread(cfg.nhc).T)


def _bias_table_grad(dbiases, buckets, name):
    outs = []
    for idx, (db, bk) in enumerate(zip(dbiases, buckets)):
        h = db.shape[0]

        def body(db_ref, bk_ref, o_ref):
            xv = db_ref[0]
            ids = bk_ref[...]
            lane = lax.broadcasted_iota(jnp.int32, (1, LANES), 1)
            acc = jnp.zeros((1, LANES), F32)
            for b in range(N_BUCKETS):
                tot = jnp.sum(jnp.where(ids == b, xv, 0.0), axis=0, keepdims=True)
                tot = jnp.sum(tot, axis=1, keepdims=True)
                acc = jnp.where(lane == b, tot, acc)
            o_ref[0] = acc

        outs.append(pl.pallas_call(
            body, name=f"{name}_{idx}", grid=(h,),
            in_specs=[pl.BlockSpec((1, BLOCK, 2 * BLOCK), lambda i: (i, 0, 0)),
                      pl.BlockSpec((BLOCK, 2 * BLOCK), lambda i: (0, 0))],
            out_specs=pl.BlockSpec((1, 1, LANES), lambda i: (i, 0, 0)),
            out_shape=jax.ShapeDtypeStruct((h, 1, LANES), F32), compiler_params=_params(("parallel",)),
        )(db, bk)[:, 0, :])
    return outs


SUBLANES = 8


def _shift_down(u, n, rows):
    r = pltpu.roll(u, n, 0)
    return jnp.concatenate([jnp.where(rows[:SUBLANES] >= n, r[:SUBLANES], 0.0), r[SUBLANES:]], axis=0)


def _shift_up(u, n, rows, s):
    r = pltpu.roll(u, s - n, 0)
    return jnp.concatenate([r[:s - SUBLANES], jnp.where(rows[s - SUBLANES:] < s - n, r[s - SUBLANES:], 0.0)], axis=0)


def _conv(u, w_ref, b_ref, rows):
    return (b_ref[...] + w_ref[0:1, :] * _shift_down(u, 2, rows) + w_ref[1:2, :] * _shift_down(u, 1, rows)
            + w_ref[2:3, :] * u)


def _conv_act_fwd(u, conv_w, conv_b, f, name):
    s = u.shape[0]
    nf = f // LANES

    def body(ug_ref, uu_ref, wg_ref, wu_ref, bg_ref, bu_ref, act_ref):
        rows = lax.broadcasted_iota(jnp.int32, (s, LANES), 0)
        gate = _conv(ug_ref[...].astype(F32), wg_ref, bg_ref, rows)
        up = _conv(uu_ref[...].astype(F32), wu_ref, bu_ref, rows)
        act_ref[...] = (gate * jax.nn.sigmoid(gate) * up).astype(BF16)

    def col(rws, off):
        return pl.BlockSpec((rws, LANES), lambda j: (0, j + off))

    return pl.pallas_call(
        body, name=name, grid=(nf,),
        in_specs=[col(s, 0), col(s, nf), col(CONV_WIDTH, 0), col(CONV_WIDTH, nf), col(1, 0), col(1, nf)],
        out_specs=col(s, 0), out_shape=jax.ShapeDtypeStruct((s, f), BF16), compiler_params=_params(("parallel",)),
    )(u, u, conv_w, conv_w, conv_b, conv_b)


def _conv_act_bwd(u, dact, conv_w, conv_b, f, name):
    s = u.shape[0]
    nf = f // LANES

    def body(ug_ref, uu_ref, da_ref, wg_ref, wu_ref, bg_ref, bu_ref, dug_ref, duu_ref, dwg_ref, dwu_ref, dbg_ref,
             dbu_ref):
        rows = lax.broadcasted_iota(jnp.int32, (s, LANES), 0)
        ug, uu = ug_ref[...].astype(F32), uu_ref[...].astype(F32)
        gate = _conv(ug, wg_ref, bg_ref, rows)
        up = _conv(uu, wu_ref, bu_ref, rows)
        sg = jax.nn.sigmoid(gate)
        da = da_ref[...].astype(F32)
        dgate = da * up * (sg * (1.0 + gate * (1.0 - sg)))
        dup = da * (gate * sg)
        for du, uv, w_ref, du_ref, dw_ref, db_ref in ((dgate, ug, wg_ref, dug_ref, dwg_ref, dbg_ref),
                                                     (dup, uu, wu_ref, duu_ref, dwu_ref, dbu_ref)):
            du_ref[...] = (w_ref[2:3, :] * du + w_ref[1:2, :] * _shift_up(du, 1, rows, s)
                           + w_ref[0:1, :] * _shift_up(du, 2, rows, s)).astype(BF16)
            dw_ref[0:1, :] = jnp.sum(du * _shift_down(uv, 2, rows), axis=0, keepdims=True)
            dw_ref[1:2, :] = jnp.sum(du * _shift_down(uv, 1, rows), axis=0, keepdims=True)
            dw_ref[2:3, :] = jnp.sum(du * uv, axis=0, keepdims=True)
            db_ref[...] = jnp.sum(du, axis=0, keepdims=True)

    def col(rws, off):
        return pl.BlockSpec((rws, LANES), lambda j: (0, j + off))

    return pl.pallas_call(
        body, name=name, grid=(nf,),
        in_specs=[col(s, 0), col(s, nf), col(s, 0), col(CONV_WIDTH, 0), col(CONV_WIDTH, nf), col(1, 0), col(1, nf)],
        out_specs=[col(s, 0), col(s, 0), col(CONV_WIDTH, 0), col(CONV_WIDTH, 0), col(1, 0), col(1, 0)],
        out_shape=[jax.ShapeDtypeStruct((s, f), BF16)] * 2 + [jax.ShapeDtypeStruct((CONV_WIDTH, f), F32)] * 2
        + [jax.ShapeDtypeStruct((1, f), F32)] * 2,
        compiler_params=_params(("parallel",)),
    )(u, u, dact, conv_w, conv_w, conv_b, conv_b)


def _loss_head(y, target, name):
    s, d = y.shape
    ts = _div_tile(s, 256, 16)

    def body(y_ref, t_ref, dy_ref, dyb_ref, l_ref):
        @pl.when(pl.program_id(0) == 0)
        def _():
            l_ref[...] = jnp.zeros_like(l_ref)

        err = y_ref[...] - t_ref[...]
        dy = err * (1.0 / d)
        dy_ref[...] = dy
        dyb_ref[...] = dy.astype(BF16)
        tot = jnp.sum(jnp.sum(err * err, axis=0, keepdims=True), axis=1, keepdims=True) * (0.5 / d)
        l_ref[...] += jnp.broadcast_to(tot, l_ref.shape)

    row = pl.BlockSpec((ts, d), lambda i: (i, 0))
    return pl.pallas_call(
        body, name=name, grid=(s // ts,), in_specs=[row, row],
        out_specs=[row, row, pl.BlockSpec((8, LANES), lambda i: (0, 0))],
        out_shape=[jax.ShapeDtypeStruct((s, d), F32), jax.ShapeDtypeStruct((s, d), BF16),
                   jax.ShapeDtypeStruct((8, LANES), F32)],
        compiler_params=_params(("arbitrary",)),
    )(y, target)


def _adamw(w, g, m, v, name):
    r, c = w.shape
    tr = _div_tile(r, max(8, (1 << 18) // c // 8 * 8), 8)
    c1 = 1.0 - ADAM_B1 ** ADAM_STEP
    c2 = 1.0 - ADAM_B2 ** ADAM_STEP

    def body(w_ref, g_ref, m_ref, v_ref, d_ref, nm_ref, nv_ref):
        gv = g_ref[...]
        nm = ADAM_B1 * m_ref[...] + (1.0 - ADAM_B1) * gv
        nv = ADAM_B2 * v_ref[...] + (1.0 - ADAM_B2) * (gv * gv)
        d_ref[...] = -ADAM_LR * ((nm / c1) / (jnp.sqrt(nv / c2) + ADAM_EPS) + ADAM_WD * w_ref[...])
        nm_ref[...] = nm
        nv_ref[...] = nv

    spec = pl.BlockSpec((tr, c), lambda i: (i, 0))
    return pl.pallas_call(
        body, name=name, grid=(r // tr,), in_specs=[spec] * 4, out_specs=[spec] * 3,
        out_shape=[jax.ShapeDtypeStruct((r, c), F32)] * 3, compiler_params=_params(("parallel",)),
    )(w, g, m, v)


def _adamw_layer(layer, w, g, m, v, bufs, name):
    depth, r, c = w.shape
    tr = _div_tile(r, max(8, (1 << 17) // c // 8 * 8), 8)
    c1 = 1.0 - ADAM_B1 ** ADAM_STEP
    c2 = 1.0 - ADAM_B2 ** ADAM_STEP

    def body(*refs):
        w_ref, g_ref, m_ref, v_ref = refs[:4]
        go_ref, d_ref, nm_ref, nv_ref = refs[-4:]
        gv = g_ref[...]
        nm = ADAM_B1 * m_ref[...] + (1.0 - ADAM_B1) * gv
        nv = ADAM_B2 * v_ref[...] + (1.0 - ADAM_B2) * (gv * gv)
        d_ref[...] = -ADAM_LR * ((nm / c1) / (jnp.sqrt(nv / c2) + ADAM_EPS) + ADAM_WD * w_ref[...])
        nm_ref[...] = nm
        nv_ref[...] = nv
        go_ref[...] = gv

    lay = pl.BlockSpec((None, tr, c), lambda i: (layer, i, 0))
    in_specs = [lay, pl.BlockSpec((tr, c), lambda i: (i, 0)), lay, lay]
    args = [w, g, m, v]
    aliases = {}
    if bufs is not None:
        in_specs += [pl.BlockSpec(memory_space=pl.ANY)] * 4
        args += list(bufs)
        aliases = {4 + k: k for k in range(4)}
    return pl.pallas_call(
        body, name=name, grid=(r // tr,), in_specs=in_specs, out_specs=[lay] * 4,
        out_shape=[jax.ShapeDtypeStruct((depth, r, c), F32)] * 4, input_output_aliases=aliases,
        compiler_params=_params(("parallel",)),
    )(*args)


def _mesh_pos():
    return lax.axis_index("x"), lax.axis_index("y"), lax.axis_index("c")


def _flip(v, bit):
    return 1 - v if bit else v


def _sum_parts(parts, name):
    _, r, c = parts.shape
    tr = _div_tile(r, 256, 16)

    def body(p_ref, o_ref):
        acc = p_ref[0].astype(F32)
        for src in range(1, N_DEVICES):
            acc = acc + p_ref[src].astype(F32)
        o_ref[...] = acc

    return pl.pallas_call(
        body, name=name, grid=(r // tr,), in_specs=[pl.BlockSpec((N_DEVICES, tr, c), lambda i: (0, i, 0))],
        out_specs=pl.BlockSpec((tr, c), lambda i: (i, 0)), out_shape=jax.ShapeDtypeStruct((r, c), F32),
        compiler_params=_params(("parallel",)),
    )(parts)


def _split_start(srcs, lands, plan, ncopies, name):
    nbuf = len(srcs) + len(lands)

    def body(*refs):
        bufs = refs[:nbuf]
        send_sem, recv_sem, token = refs[nbuf], refs[nbuf + 1], refs[-1]
        for k, (src, dst, dev) in enumerate(plan(bufs[:len(srcs)], bufs[len(srcs):])):
            pltpu.make_async_remote_copy(src_ref=src, dst_ref=dst, send_sem=send_sem.at[k], recv_sem=recv_sem.at[k],
                                         device_id=dev, device_id_type=MESH).start()
        token[...] = jnp.zeros_like(token)

    hbm = pl.BlockSpec(memory_space=pltpu.HBM)
    sem = pl.BlockSpec(memory_space=pltpu.SEMAPHORE)
    operands = [pltpu.with_memory_space_constraint(a, pltpu.HBM) for a in (*srcs, *lands)]
    outs = pl.pallas_call(
        body, name=name, in_specs=[hbm] * nbuf,
        out_specs=(sem, sem, *[hbm] * nbuf, pl.BlockSpec(memory_space=pltpu.VMEM)),
        out_shape=(pltpu.SemaphoreType.DMA((ncopies,)), pltpu.SemaphoreType.DMA((ncopies,)),
                   *[pltpu.HBM(a.shape, a.dtype) for a in operands], jax.ShapeDtypeStruct((8, LANES), F32)),
        input_output_aliases={i: 2 + i for i in range(nbuf)},
        compiler_params=pltpu.CompilerParams(has_side_effects=pltpu.SideEffectType.DATAFLOW_SIDE_EFFECTING),
    )(*operands)
    handle = dict(send=outs[0], recv=outs[1], bufs=list(outs[2:2 + nbuf]), nsrc=len(srcs), plan=plan)
    return handle, outs[-1]


def _split_wait(handle, after, name):
    nbuf, nsrc, plan = len(handle["bufs"]), handle["nsrc"], handle["plan"]

    def body(*refs):
        bufs = refs[:nbuf]
        send_sem, recv_sem = refs[nbuf], refs[nbuf + 1]
        for k, (src, dst, dev) in enumerate(plan(bufs[:nsrc], bufs[nsrc:])):
            copy = pltpu.make_async_remote_copy(src_ref=src, dst_ref=dst, send_sem=send_sem.at[k],
                                                recv_sem=recv_sem.at[k], device_id=dev, device_id_type=MESH)
            copy.wait_send()
            copy.wait_recv()

    hbm = pl.BlockSpec(memory_space=pltpu.HBM)
    sem = pl.BlockSpec(memory_space=pltpu.SEMAPHORE)
    outs = pl.pallas_call(
        body, name=name, in_specs=[hbm] * nbuf + [sem, sem, pl.BlockSpec(memory_space=pl.ANY)],
        out_specs=[hbm] * nbuf, out_shape=[pltpu.HBM(a.shape, a.dtype) for a in handle["bufs"]],
        input_output_aliases={i: i for i in range(nbuf)},
        compiler_params=pltpu.CompilerParams(has_side_effects=pltpu.SideEffectType.DATAFLOW_SIDE_EFFECTING),
    )(*handle["bufs"], handle["send"], handle["recv"], after)
    return list(outs[nsrc:])


def _own_slot(shape, dtype, block, index):
    return lax.dynamic_update_slice(lax.empty(shape, dtype), block[None], (index,) + (0,) * block.ndim)


def _gather_plan(srcs, lands):
    x, y, c = _mesh_pos()
    return [(land.at[2 * x + y], land.at[2 * x + y], (*chip, c))
            for land in lands for chip in ((1 - x, y), (x, 1 - y), (1 - x, 1 - y))]


def _scatter_plan(srcs, lands):
    x, y, c = _mesh_pos()
    out = []
    for src, land in zip(srcs, lands):
        half = src.shape[1] // 2
        for d in range(1, N_DEVICES):
            p = (_flip(x, d & 4), _flip(y, d & 2), _flip(c, d & 1))
            out.append((src.at[2 * p[0] + p[1], pl.ds(p[2] * half, half), :], land.at[4 * x + 2 * y + c], p))
    return out


def _swap_plan(srcs, lands):
    x, y, c = _mesh_pos()
    return [(src, land.at[c], (x, y, 1 - c)) for src, land in zip(srcs, lands)]


class _Gathered:
    def __init__(self, groups):
        self.groups = groups
        self.ready = {}

    def get(self, name, after=None):
        if name not in self.ready:
            handle, names, wait_name = next(g for g in self.groups if name in g[1])
            for n, full in zip(names, _split_wait(handle, after, wait_name)):
                self.ready[n] = full.reshape(-1, full.shape[-1])
        return self.ready[name]


def _allreduce_small(flat, name):
    r = flat.shape[0]

    def body(x_ref, o_ref, buf, send_sems, recv_sems):
        x, y, c = _mesh_pos()
        me = 4 * x + 2 * y + c
        buf[me] = x_ref[...]
        started = []
        peers = [(_flip(x, d & 4), _flip(y, d & 2), _flip(c, d & 1)) for d in range(1, N_DEVICES)]
        for d, p in enumerate(peers):
            cp = pltpu.make_async_remote_copy(src_ref=x_ref, dst_ref=buf.at[me], send_sem=send_sems.at[d],
                                              recv_sem=recv_sems.at[d], device_id=p, device_id_type=MESH)
            cp.start()
            started.append(cp)
        for d, p in enumerate(peers):
            slot = buf.at[4 * p[0] + 2 * p[1] + p[2]]
            pltpu.make_async_remote_copy(src_ref=slot, dst_ref=slot, send_sem=send_sems.at[d], recv_sem=recv_sems.at[d],
                                         device_id=p, device_id_type=MESH).wait_recv()
        for cp in started:
            cp.wait_send()
        acc = buf[0]
        for src in range(1, N_DEVICES):
            acc = acc + buf[src]
        o_ref[...] = acc

    vm = pl.BlockSpec(memory_space=pltpu.VMEM)
    return pl.pallas_call(
        body, name=name, in_specs=[vm], out_specs=vm, out_shape=jax.ShapeDtypeStruct((r, LANES), F32),
        scratch_shapes=[pltpu.VMEM((N_DEVICES, r, LANES), F32), pltpu.SemaphoreType.DMA((N_DEVICES - 1,)),
                        pltpu.SemaphoreType.DMA((N_DEVICES - 1,))],
        compiler_params=pltpu.CompilerParams(vmem_limit_bytes=VMEM_LIMIT_BYTES),
    )(flat)


def _bucket_ids(dil):
    rel = (np.arange(BLOCK)[:, None] + BLOCK - np.arange(2 * BLOCK)[None, :]) * dil
    max_exact = N_BUCKETS // 2
    d = np.maximum(rel, 0)
    large = max_exact + (np.log(np.maximum(d, 1).astype(np.float32) / max_exact)
                         / np.float32(np.log(T5_MAX_DIST / max_exact)) * (N_BUCKETS - max_exact)).astype(np.int32)
    large = np.minimum(large, N_BUCKETS - 1)
    return np.where(d < max_exact, d, large).astype(np.int32)


def _block_bias(table, dil):
    onehot = (jnp.asarray(_bucket_ids(dil))[:, :, None] == jnp.arange(N_BUCKETS)[None, None, :]).astype(F32)
    return jnp.einsum("ijb,bh->hij", onehot, table.astype(F32), precision=lax.Precision.HIGHEST)


def _tile_gain(g, n):
    return jnp.tile(g.reshape(1, HEAD_DIM), (1, n))


def _layer_fwd(x, p, cfg):
    w = p["weights"]
    h1 = _rmsnorm_fwd(x, p["attn_norm"], "attn_norm_fwd")
    proj = _matmul(h1, w.get("w_in_t", h1), "nt", F32, "in_proj", tm=1024, tn=768, tk=2048)
    aq, ak, av, bq, bk, bv, cq, ck, cv = _qk_prep(proj, p["gains"], cfg, "qk_prep")
    akt = _dilated_t(ak, 1)
    oa, lse_a = _banded_fwd(aq, akt, av, p["bias_a"], p["sinks"], cfg.nha, cfg.nkva, WINDOW_A - 1, 1, "swa_fwd")
    bkt = _keys_on_lanes(bk, SB_CHUNK)
    ob, tot_b = _sb_fwd(bq, bkt, bv, "stickbreak_fwd")
    ocs, lses, ckts = [], [], []
    for (window, dil), bias in zip(DILATED_PAIRS, p["bias_c"]):
        ckts.append(_dilated_t(ck, dil))
        o, l = _banded_fwd(cq, ckts[-1], cv, bias, None, cfg.nhc, cfg.nhc, window // dil, dil, f"dilated{dil}_fwd")
        ocs.append(o)
        lses.append(l)
    mix, oc, lse_c = _mix_fwd(oa, ob, ocs, lses, p["mix_gain"], cfg, "mix_fwd")
    xm = _matmul(mix, w.get("w_out", mix), "nn", F32, "out_proj", tm=1024, tn=512, tk=2048, residual=x)
    h2 = _rmsnorm_fwd(xm, p["ffn_norm"], "ffn_norm_fwd")
    u = _matmul(h2, w.get("w_up_t", h2), "nt", BF16, "up_proj", tm=1024, tn=512, tk=2048)
    act = _conv_act_fwd(u, p["conv_w"], p["conv_b"], cfg.f, "conv_act_fwd")
    y = _matmul(act, w.get("w_down", act), "nn", F32, "down_proj", tm=1024, tn=1024, tk=1408, residual=xm)
    saved = dict(x=x, h1=h1, proj=proj, q=(aq, ak, av, bq, bk, bv, cq, ck, cv), oa=oa, lse_a=lse_a, ob=ob,
                 tot_b=tot_b, akt=akt, bkt=bkt, ckts=ckts, oc=oc, lse_c=lse_c, mix=mix, xm=xm, h2=h2, u=u, act=act)
    return y, saved


def _layer_bwd(dy, dyb, sv, p, dbias, cfg, on_grad):
    aq, ak, av, bq, bk, bv, cq, ck, cv = sv["q"]
    w = p["weights"]
    anchor = on_grad(_matmul(sv["act"], dyb, "tn", BF16, "down_proj_dw", tm=1408, tn=2048, tk=1024))
    dact = _matmul(dyb, w.get("w_down"), "nt", BF16, "down_proj_dx", tm=1024, tn=512, tk=2048)
    dug, duu, dwg, dwu, dbg, dbu = _conv_act_bwd(sv["u"], dact, p["conv_w"], p["conv_b"] + anchor, cfg.f,
                                                 "conv_act_bwd")
    du = jnp.concatenate([dug, duu], axis=1)
    anchor = on_grad(_matmul(du, sv["h2"], "tn", BF16, "up_proj_dw", tm=1408, tn=2048, tk=1024))
    dh2 = _matmul(du, w.get("w_up_t"), "nn", F32, "up_proj_dx", tm=1024, tn=2048, tk=1024)
    dxm, dxmb, g_ffn_norm = _rmsnorm_bwd(sv["xm"], p["ffn_norm"] + anchor, dh2, dy, "ffn_norm_bwd")
    anchor = on_grad(_matmul(sv["mix"], dxmb, "tn", BF16, "out_proj_dw", tm=1024, tn=2048, tk=1024))
    dmix = _matmul(dxmb, w.get("w_out"), "nt", F32, "out_proj_dx", tm=1024, tn=512, tk=2048)
    doa, dob, doc, g_mix_gain, dsum_a, dsum_c = _mix_bwd(dmix, sv["oa"], sv["ob"], sv["oc"], p["mix_gain"] + anchor,
                                                         cfg, "mix_bwd")
    daq, dak, dav, dbias_a, g_sinks = _banded_bwd(aq, ak, sv["akt"], av, sv["lse_a"], dsum_a, doa, p["bias_a"],
                                                 p["sinks"], dbias[0], cfg.nha, cfg.nkva, WINDOW_A - 1, 1, "swa_bwd")
    dbq, dbk, dbv = _sb_bwd(bq, bk, sv["bkt"], bv, sv["tot_b"], dob, "stickbreak_bwd")
    dcq, dck, dcv, dbias_c = [], [], [], []
    for idx, ((window, dil), bias) in enumerate(zip(DILATED_PAIRS, p["bias_c"])):
        a, b, c, d, _ = _banded_bwd(cq, ck, sv["ckts"][idx], cv, sv["lse_c"], dsum_c, doc, bias, None, dbias[1][idx],
                                    cfg.nhc, cfg.nhc, window // dil, dil, f"dilated{dil}_bwd")
        dcq.append(a)
        dck.append(b)
        dcv.append(c)
        dbias_c.append(d)
    dproj, g_aq, g_ak, g_cq, g_ck = _qk_prep_bwd(
        sv["proj"], p["gains"], [[daq], [dak], [dav], [dbq], [dbk], [dbv], dcq, dck, dcv], cfg, "qk_prep_bwd")
    anchor = on_grad(_matmul(dproj, sv["h1"], "tn", BF16, "in_proj_dw", tm=768, tn=2048, tk=1024))
    dh1 = _matmul(dproj, w.get("w_in_t"), "nn", F32, "in_proj_dx", tm=1024, tn=2048, tk=768)
    dx, dxb, g_attn_norm = _rmsnorm_bwd(sv["x"], p["attn_norm"] + anchor, dh1, dxm, "attn_norm_bwd")

    def fold(g):
        return jnp.sum(g.reshape(-1, HEAD_DIM), axis=0)

    small = dict(attn_norm=g_attn_norm[0], a_q_gain=fold(g_aq), a_k_gain=fold(g_ak), a_sinks=g_sinks,
                 c_q_gain=fold(g_cq), c_k_gain=fold(g_ck), mix_out_gain=g_mix_gain[0], ffn_norm=g_ffn_norm[0],
                 conv_w=jnp.concatenate([dwg, dwu], axis=1), conv_b=jnp.concatenate([dbg, dbu], axis=1)[0])
    return dx, dxb, small, (dbias_a, dbias_c)


_SMALL = ("attn_norm", "a_q_gain", "a_k_gain", "a_sinks", "c_q_gain", "c_k_gain", "rel_bias_table", "mix_out_gain",
          "ffn_norm", "conv_w", "conv_b")


def _pack(arrays):
    flat = jnp.concatenate([a.reshape(-1).astype(F32) for a in arrays])
    pad = (-flat.shape[0]) % (8 * LANES)
    return jnp.pad(flat, (0, pad)).reshape(-1, LANES)


def _unpack(flat, shapes):
    flat = flat.reshape(-1)
    out, pos = [], 0
    for sh in shapes:
        n = int(np.prod(sh))
        out.append(flat[pos:pos + n].reshape(sh))
        pos += n
    return out


def kernel(x, attn_norm, w_in, a_q_gain, a_k_gain, a_sinks, c_q_gain, c_k_gain, rel_bias_table, mix_out_gain, w_out, ffn_norm, w_up, conv_w, conv_b, w_down, loss_target, m_attn_norm, m_w_in, m_a_q_gain, m_a_k_gain, m_a_sinks, m_c_q_gain, m_c_k_gain, m_rel_bias_table, m_mix_out_gain, m_w_out, m_ffn_norm, m_w_up, m_conv_w, m_conv_b, m_w_down, v_attn_norm, v_w_in, v_a_q_gain, v_a_k_gain, v_a_sinks, v_c_q_gain, v_c_k_gain, v_rel_bias_table, v_mix_out_gain, v_w_out, v_ffn_norm, v_w_up, v_conv_w, v_conv_b, v_w_down):
    depth, d = attn_norm.shape
    f = w_down.shape[1] * N_CHIPS
    cfg = _Cfg(d, f)
    chip = 2 * lax.axis_index("x") + lax.axis_index("y")

    cw_cols = conv_w.shape[2]
    cw_flat = conv_w.reshape(-1)
    cw_rows = -(-cw_flat.shape[0] // (16 * LANES)) * 16
    cw_pad = jnp.pad(cw_flat, (0, cw_rows * LANES - cw_flat.shape[0])).reshape(cw_rows, LANES)

    table_a, table_c = rel_bias_table[:, :cfg.nha], rel_bias_table[:, cfg.nha:]
    bias_a = _block_bias(table_a, 1)
    bias_c = [_block_bias(table_c, dil) for _, dil in DILATED_PAIRS]

    layers, anchor = [], 0.0
    for l in range(depth):
        shards = [w_in[l].T.astype(BF16), w_out[l].astype(BF16), w_up[l].T.astype(BF16), w_down[l].astype(BF16)]
        names = ["w_in_t", "w_out", "w_up_t", "w_down"]
        if l == 0:
            todo = [([cw_pad, shards[0]], ["conv_w", names[0]])] + [([s], [n]) for s, n in zip(shards[1:], names[1:])]
        else:
            todo = [(shards, names)]
        groups = []
        for k, (srcs, group_names) in enumerate(todo):
            lands = [_own_slot((N_CHIPS,) + s.shape, s.dtype, s, chip) for s in srcs]
            handle, token = _split_start([], lands, _gather_plan, 3 * len(lands), f"gather_start_{l}_{k}")
            anchor = anchor + token[0, 0]
            groups.append((handle, group_names, f"gather_wait_{l}_{k}"))
        layers.append(dict(
            attn_norm=attn_norm[l].reshape(1, d), ffn_norm=ffn_norm[l].reshape(1, d),
            mix_gain=mix_out_gain[l].reshape(1, d),
            gains=(_tile_gain(a_q_gain[l], cfg.nha), _tile_gain(a_k_gain[l], cfg.nkva),
                   _tile_gain(c_q_gain[l], cfg.nhc), _tile_gain(c_k_gain[l], cfg.nhc)),
            sinks=a_sinks[l], bias_a=bias_a, bias_c=bias_c, conv_b=conv_b[l].reshape(1, 2 * f),
            weights=_Gathered(groups)))
    cw_all = layers[0]["weights"].get("conv_w", layers[0]["attn_norm"] + anchor)
    cw_all = cw_all.reshape(N_CHIPS, -1)[:, :cw_flat.shape[0]].reshape(N_CHIPS, depth, CONV_WIDTH, cw_cols)
    conv_w_full = jnp.transpose(cw_all, (1, 2, 0, 3)).reshape(depth, CONV_WIDTH, N_CHIPS * cw_cols)
    for l in range(depth):
        layers[l]["conv_w"] = conv_w_full[l]

    act = x[0]
    saved = []
    for l in range(depth):
        act, sv = _layer_fwd(act, layers[l], cfg)
        saved.append(sv)
    dact, dactb, loss_blk = _loss_head(act, loss_target[0], "loss_head")
    loss = lax.psum(loss_blk[0, 0], ("x", "y", "c"))

    core = lax.axis_index("c")

    def start_scatter(grads, name):
        srcs = [g.reshape(N_CHIPS, -1, g.shape[-1]) for g in grads]
        lands = []
        for g in srcs:
            half = g.shape[1] // 2
            own = lax.dynamic_slice(g, (chip, core * half, 0), (1, half, g.shape[2]))[0]
            lands.append(_own_slot((N_DEVICES, half, g.shape[2]), g.dtype, own, 2 * chip + core))
        return _split_start(srcs, lands, _scatter_plan, (N_DEVICES - 1) * len(srcs), name)

    def finish_scatter(l, handles, after):
        parts = [pt for k, h in enumerate(handles) for pt in _split_wait(h, after, f"scatter_wait_{l}_{k}")][::-1]
        halves = [_sum_parts(pt, f"sum_grads_{t}") for t, pt in enumerate(parts)]
        lands = [_own_slot((2,) + h.shape, h.dtype, h, core) for h in halves]
        return _split_start(halves, lands, _swap_plan, len(halves), f"swap_start_{l}")[0]

    dbias = (jnp.zeros_like(bias_a), [jnp.zeros_like(b) for b in bias_c])
    small_grads = [None] * depth
    swaps = [None] * depth
    pending = None
    for l in reversed(range(depth)):
        made = []

        def on_grad(g, l=l, made=made):
            if l:
                made.append(g)
                return 0.0
            handle, token = start_scatter([g], f"scatter_start_0_{len(made)}")
            made.append(handle)
            return token[0, 0]

        dact, dactb, small_grads[l], dbias = _layer_bwd(dact, dactb, saved[l], layers[l], dbias, cfg, on_grad)
        if pending is not None:
            swaps[l + 1] = finish_scatter(l + 1, pending, dact)
        if l:
            handle, token = start_scatter(made, f"scatter_start_{l}")
            pending = [handle]
            layers[l - 1]["conv_b"] = layers[l - 1]["conv_b"] + token[0, 0]
        else:
            pending = made
    grad_x = dact[None]

    tabs = _bias_table_grad([dbias[0]] + dbias[1], [jnp.asarray(_bucket_ids(1))]
                            + [jnp.asarray(_bucket_ids(dil)) for _, dil in DILATED_PAIRS], "bias_table_grad")
    g_table_a = tabs[0][:, :N_BUCKETS].T
    g_table_c = (tabs[1] + tabs[2] + tabs[3])[:, :N_BUCKETS].T
    g_table = jnp.concatenate([g_table_a, g_table_c], axis=1)
    small_local = {k: jnp.stack([small_grads[l][k] for l in range(depth)]) for k in _SMALL if k != "rel_bias_table"}
    small_local["rel_bias_table"] = g_table
    shapes = [small_local[k].shape for k in _SMALL]
    reduced = dict(zip(_SMALL, _unpack(_allreduce_small(_pack([small_local[k] for k in _SMALL]), "allreduce_small"),
                                       shapes)))
    reduced["conv_w"] = lax.dynamic_slice_in_dim(reduced["conv_w"], chip * cw_cols, cw_cols, axis=2)

    given = dict(attn_norm=attn_norm, a_q_gain=a_q_gain, a_k_gain=a_k_gain, a_sinks=a_sinks, c_q_gain=c_q_gain,
                 c_k_gain=c_k_gain, rel_bias_table=rel_bias_table, mix_out_gain=mix_out_gain, ffn_norm=ffn_norm,
                 conv_w=conv_w, conv_b=conv_b)
    moms = dict(attn_norm=(m_attn_norm, v_attn_norm), a_q_gain=(m_a_q_gain, v_a_q_gain),
                a_k_gain=(m_a_k_gain, v_a_k_gain), a_sinks=(m_a_sinks, v_a_sinks), c_q_gain=(m_c_q_gain, v_c_q_gain),
                c_k_gain=(m_c_k_gain, v_c_k_gain), rel_bias_table=(m_rel_bias_table, v_rel_bias_table),
                mix_out_gain=(m_mix_out_gain, v_mix_out_gain), ffn_norm=(m_ffn_norm, v_ffn_norm),
                conv_w=(m_conv_w, v_conv_w), conv_b=(m_conv_b, v_conv_b))
    sshapes = [given[k].shape for k in _SMALL]
    s_delta, s_m, s_v = _adamw(_pack([given[k] for k in _SMALL]), _pack([reduced[k] for k in _SMALL]),
                               _pack([moms[k][0] for k in _SMALL]), _pack([moms[k][1] for k in _SMALL]), "adamw_small")
    grads = dict(reduced)
    deltas = dict(zip(_SMALL, _unpack(s_delta, sshapes)))
    new_m = dict(zip(_SMALL, _unpack(s_m, sshapes)))
    new_v = dict(zip(_SMALL, _unpack(s_v, sshapes)))

    big_given = dict(w_in=(w_in, m_w_in, v_w_in, True), w_out=(w_out, m_w_out, v_w_out, False),
                     w_up=(w_up, m_w_up, v_w_up, True), w_down=(w_down, m_w_down, v_w_down, False))
    names = ("w_in", "w_out", "w_up", "w_down")
    bufs = {name: None for name in names}
    after = s_delta
    for l in reversed(range(depth)):
        if l == 0:
            swaps[0] = finish_scatter(0, pending, after)
        layer_grads = [g.reshape(-1, g.shape[-1]) for g in _split_wait(swaps[l], after, f"swap_wait_{l}")]
        for t, name in enumerate(names):
            wt, mt, vt, transposed = big_given[name]
            g = layer_grads[t].T if transposed else layer_grads[t]
            bufs[name] = _adamw_layer(l, wt, g, mt, vt, bufs[name], f"adamw_{name}_{l}")
            after = bufs[name][1]
    for name in names:
        grads[name], deltas[name], new_m[name], new_v[name] = bufs[name]

    order = ("attn_norm", "w_in", "a_q_gain", "a_k_gain", "a_sinks", "c_q_gain", "c_k_gain", "rel_bias_table",
             "mix_out_gain", "w_out", "ffn_norm", "w_up", "conv_w", "conv_b", "w_down")
    return (loss, grad_x, *[grads[k] for k in order], *[deltas[k] for k in order], *[new_m[k] for k in order],
            *[new_v[k] for k in order])
```

```python
import numpy as np
import jax
import jax.numpy as jnp
from jax import lax
from jax.experimental import pallas as pl
from jax.experimental.pallas import tpu as pltpu

F32 = jnp.float32
BF16 = jnp.bfloat16
MESH = pl.DeviceIdType.MESH

HEAD_DIM = 64
BLOCK = 128
LANES = 128
EPS = 1e-6
NEG_INF = -1e30
WINDOW_A = 128
DILATED_PAIRS = ((128, 1), (512, 4), (2048, 16))
N_BUCKETS = 32
T5_MAX_DIST = 2048
CONV_WIDTH = 3
ADAM_LR = 0.001
ADAM_B1 = 0.9
ADAM_B2 = 0.999
ADAM_EPS = 1e-08
ADAM_WD = 0.01
ADAM_STEP = 10
N_CHIPS = 4
N_DEVICES = 8
VMEM_LIMIT_BYTES = 48 * 1024 * 1024
QK_SCALE = HEAD_DIM ** -0.5


def _params(sem=None):
    return pltpu.CompilerParams(dimension_semantics=sem, vmem_limit_bytes=VMEM_LIMIT_BYTES)


def _div_tile(n, cap, mult):
    best = None
    for t in range(mult, min(n, cap) + 1, mult):
        if n % t == 0:
            best = t
    return n if best is None else best


def _dot(a, b):
    return lax.dot_general(a, b, (((1,), (0,)), ((), ())), preferred_element_type=F32)


def _dot_nt(a, b):
    return lax.dot_general(a, b, (((1,), (1,)), ((), ())), preferred_element_type=F32)


def _dot_tn(a, b):
    return lax.dot_general(a, b, (((0,), (0,)), ((), ())), preferred_element_type=F32)


def _split_dot(x, m):
    hi = x.astype(BF16)
    lo = (x - hi.astype(F32)).astype(BF16)
    return _dot(hi, m) + _dot(lo, m)


class _Cfg:
    def __init__(self, d_model, d_ff):
        nh = d_model // HEAD_DIM
        self.d = d_model
        self.f = d_ff
        self.nha = nh // 4
        self.nkva = self.nha // 4
        self.nhb = nh // 4
        self.nhc = nh // 2
        self.a_q = self.nha * HEAD_DIM
        self.a_kv = self.nkva * HEAD_DIM
        self.b_w = self.nhb * HEAD_DIM
        self.c_w = self.nhc * HEAD_DIM
        sizes = [self.a_q, self.a_kv, self.a_kv, self.b_w, self.b_w, self.b_w, self.c_w, self.c_w, self.c_w]
        starts = [0] + [int(s) for s in np.cumsum(sizes)[:-1]]
        self.sections = list(zip(starts, sizes))
        self.in_width = int(sum(sizes))
        assert all(s % LANES == 0 for s in sizes)


def _matmul(a, b, mode, out_dtype, name, tm=512, tn=512, tk=512, residual=None):
    if mode == "tn":
        kdim, m = a.shape
    else:
        m, kdim = a.shape
    n = b.shape[0] if mode == "nt" else b.shape[1]
    tm, tn, tk = _div_tile(m, tm, LANES), _div_tile(n, tn, LANES), _div_tile(kdim, tk, LANES)
    nk = kdim // tk
    if mode == "tn":
        a_spec = pl.BlockSpec((tk, tm), lambda i, j, k: (k, i))
    else:
        a_spec = pl.BlockSpec((tm, tk), lambda i, j, k: (i, k))
    if mode == "nt":
        b_spec = pl.BlockSpec((tn, tk), lambda i, j, k: (j, k))
    else:
        b_spec = pl.BlockSpec((tk, tn), lambda i, j, k: (k, j))
    dot = {"nn": _dot, "nt": _dot_nt, "tn": _dot_tn}[mode]
    o_spec = pl.BlockSpec((tm, tn), lambda i, j, k: (i, j))
    in_specs = [a_spec, b_spec]
    args = [a, b]
    if residual is not None:
        in_specs.append(o_spec)
        args.append(residual)

    def body(*refs):
        if residual is None:
            a_ref, b_ref, o_ref, acc = refs
        else:
            a_ref, b_ref, r_ref, o_ref, acc = refs
        k = pl.program_id(2)

        @pl.when(k == 0)
        def _():
            acc[...] = jnp.zeros_like(acc)

        acc[...] += dot(a_ref[...].astype(BF16), b_ref[...].astype(BF16))

        @pl.when(k == nk - 1)
        def _():
            r = acc[...]
            if residual is not None:
                r = r + r_ref[...]
            o_ref[...] = r.astype(out_dtype)

    return pl.pallas_call(
        body, name=name, grid=(m // tm, n // tn, nk), in_specs=in_specs, out_specs=o_spec,
        out_shape=jax.ShapeDtypeStruct((m, n), out_dtype), scratch_shapes=[pltpu.VMEM((tm, tn), F32)],
        compiler_params=_params(("parallel", "parallel", "arbitrary")),
    )(*args)


def _rmsnorm_fwd(x, g, name):
    s, d = x.shape
    ts = _div_tile(s, 256, 8)

    def body(x_ref, g_ref, o_ref):
        xv = x_ref[...]
        r = lax.rsqrt(jnp.mean(xv * xv, axis=-1, keepdims=True) + EPS)
        o_ref[...] = (xv * r * g_ref[...]).astype(BF16)

    return pl.pallas_call(
        body, name=name, grid=(s // ts,),
        in_specs=[pl.BlockSpec((ts, d), lambda i: (i, 0)), pl.BlockSpec((1, d), lambda i: (0, 0))],
        out_specs=pl.BlockSpec((ts, d), lambda i: (i, 0)), out_shape=jax.ShapeDtypeStruct((s, d), BF16),
        compiler_params=_params(("parallel",)),
    )(x, g)


def _rmsnorm_bwd(x, g, dh, dres, name):
    s, d = x.shape
    ts = _div_tile(s, 256, 16)

    def body(x_ref, g_ref, dh_ref, dres_ref, dx_ref, dxb_ref, dg_ref):
        @pl.when(pl.program_id(0) == 0)
        def _():
            dg_ref[...] = jnp.zeros_like(dg_ref)

        xv = x_ref[...]
        r = lax.rsqrt(jnp.mean(xv * xv, axis=-1, keepdims=True) + EPS)
        xhat = xv * r
        dhv = dh_ref[...]
        dxhat = dhv * g_ref[...]
        dx = dres_ref[...] + r * (dxhat - xhat * jnp.mean(dxhat * xhat, axis=-1, keepdims=True))
        dx_ref[...] = dx
        dxb_ref[...] = dx.astype(BF16)
        dg_ref[...] += jnp.sum(dhv * xhat, axis=0, keepdims=True)

    row = pl.BlockSpec((ts, d), lambda i: (i, 0))
    vec = pl.BlockSpec((1, d), lambda i: (0, 0))
    return pl.pallas_call(
        body, name=name, grid=(s // ts,), in_specs=[row, vec, row, row], out_specs=[row, row, vec],
        out_shape=[jax.ShapeDtypeStruct((s, d), F32), jax.ShapeDtypeStruct((s, d), BF16),
                   jax.ShapeDtypeStruct((1, d), F32)],
        compiler_params=_params(("arbitrary",)),
    )(x, g, dh, dres)


def _head_mean_matrix():
    idx = np.arange(LANES) // HEAD_DIM
    return jnp.asarray((idx[:, None] == idx[None, :]).astype(np.float32) / HEAD_DIM, dtype=BF16)


def _head_mean(y, m128):
    w = y.shape[1]
    parts = [_split_dot(y[:, c * LANES:(c + 1) * LANES], m128) for c in range(w // LANES)]
    return parts[0] if len(parts) == 1 else jnp.concatenate(parts, axis=1)


_NORMED_SECTIONS = (0, 1, 6, 7)
_QUERY_SECTIONS = (0, 3, 6)


def _qk_prep(proj, gains, cfg, name):
    s = proj.shape[0]
    ts = _div_tile(s, 256, 16)
    m128 = _head_mean_matrix()

    def body(p_ref, m_ref, g0, g1, g6, g7, *outs):
        gref = dict(zip(_NORMED_SECTIONS, (g0, g1, g6, g7)))
        for idx, (st, w) in enumerate(cfg.sections):
            xv = p_ref[:, st:st + w]
            if idx in gref:
                r = lax.rsqrt(_head_mean(xv * xv, m_ref[...]) + EPS)
                xv = xv * r * gref[idx][...]
            if idx in _QUERY_SECTIONS:
                xv = xv * QK_SCALE
            outs[idx][...] = xv.astype(BF16)

    in_specs = [pl.BlockSpec((ts, cfg.in_width), lambda i: (i, 0)), pl.BlockSpec((LANES, LANES), lambda i: (0, 0))]
    in_specs += [pl.BlockSpec((1, cfg.sections[k][1]), lambda i: (0, 0)) for k in _NORMED_SECTIONS]
    out_specs = [pl.BlockSpec((ts, w), lambda i: (i, 0)) for _, w in cfg.sections]
    out_shape = [jax.ShapeDtypeStruct((s, w), BF16) for _, w in cfg.sections]
    return pl.pallas_call(
        body, name=name, grid=(s // ts,), in_specs=in_specs, out_specs=out_specs, out_shape=out_shape,
        compiler_params=_params(("parallel",)),
    )(proj, m128, *gains)


def _qk_prep_bwd(proj, gains, grads, cfg, name):
    s = proj.shape[0]
    ts = _div_tile(s, 128, 16)
    m128 = _head_mean_matrix()
    counts = [len(gl) for gl in grads]
    flat = [g for gl in grads for g in gl]

    def body(*refs):
        p_ref, m_ref = refs[0], refs[1]
        gref = dict(zip(_NORMED_SECTIONS, refs[2:6]))
        g_in = refs[6:6 + len(flat)]
        dp_ref = refs[6 + len(flat)]
        dgain = dict(zip(_NORMED_SECTIONS, refs[7 + len(flat):]))

        @pl.when(pl.program_id(0) == 0)
        def _():
            for k in _NORMED_SECTIONS:
                dgain[k][...] = jnp.zeros_like(dgain[k])

        pos = 0
        for idx, (st, w) in enumerate(cfg.sections):
            dy = g_in[pos][...].astype(F32)
            for extra in g_in[pos + 1:pos + counts[idx]]:
                dy = dy + extra[...].astype(F32)
            pos += counts[idx]
            if idx in gref:
                xv = p_ref[:, st:st + w]
                r = lax.rsqrt(_head_mean(xv * xv, m_ref[...]) + EPS)
                xhat = xv * r
                dxhat = dy * gref[idx][...]
                dgain[idx][...] += jnp.sum(dy * xhat, axis=0, keepdims=True)
                dy = r * (dxhat - xhat * _head_mean(dxhat * xhat, m_ref[...]))
            dp_ref[:, st:st + w] = dy.astype(BF16)

    in_specs = [pl.BlockSpec((ts, cfg.in_width), lambda i: (i, 0)), pl.BlockSpec((LANES, LANES), lambda i: (0, 0))]
    in_specs += [pl.BlockSpec((1, cfg.sections[k][1]), lambda i: (0, 0)) for k in _NORMED_SECTIONS]
    for idx, (_, w) in enumerate(cfg.sections):
        in_specs += [pl.BlockSpec((ts, w), lambda i: (i, 0))] * counts[idx]
    out_specs = [pl.BlockSpec((ts, cfg.in_width), lambda i: (i, 0))]
    out_specs += [pl.BlockSpec((1, cfg.sections[k][1]), lambda i: (0, 0)) for k in _NORMED_SECTIONS]
    out_shape = [jax.ShapeDtypeStruct((s, cfg.in_width), BF16)]
    out_shape += [jax.ShapeDtypeStruct((1, cfg.sections[k][1]), F32) for k in _NORMED_SECTIONS]
    return pl.pallas_call(
        body, name=name, grid=(s // ts,), in_specs=in_specs, out_specs=out_specs, out_shape=out_shape,
        compiler_params=_params(("arbitrary",)),
    )(proj, m128, *gains, *flat)


def _band_masks(max_dist):
    row = lax.broadcasted_iota(jnp.int32, (BLOCK, BLOCK), 0)
    col = lax.broadcasted_iota(jnp.int32, (BLOCK, BLOCK), 1)
    return row + BLOCK - col <= max_dist, col <= row


def _dilated_t(a, dil):
    s, w = a.shape
    return _keys_on_lanes(a.reshape(s // dil, dil * w), BLOCK)


def _undilated(at, dil):
    nblk, dw, _ = at.shape
    return jnp.transpose(at, (0, 2, 1)).reshape(nblk * BLOCK * dil, dw // dil)


def _banded_fwd(q, kt, v, bias, sinks, hq, hk, max_dist, dil, name):
    s = q.shape[0]
    wq, wk, sd, grp = hq * HEAD_DIM, hk * HEAD_DIM, s // dil, hq // hk
    nb = sd // BLOCK
    has_sink = sinks is not None

    def body(*refs):
        if has_sink:
            q_ref, ktp_ref, ktc_ref, vp_ref, vc_ref, b_ref, s_ref, o_ref, l_ref = refs
        else:
            q_ref, ktp_ref, ktc_ref, vp_ref, vc_ref, b_ref, o_ref, l_ref = refs
        i = pl.program_id(1)
        mprev, mcur = _band_masks(max_dist)
        mask = jnp.concatenate([jnp.logical_and(mprev, i > 0), mcur], axis=1)
        for h in range(hq):
            sq = slice(h * HEAD_DIM, (h + 1) * HEAD_DIM)
            sk = slice((h // grp) * HEAD_DIM, (h // grp + 1) * HEAD_DIM)
            kt = jnp.concatenate([ktp_ref[sk, :], ktc_ref[sk, :]], axis=1)
            vv = jnp.concatenate([vp_ref[:, sk], vc_ref[:, sk]], axis=0)
            sc = jnp.where(mask, _dot(q_ref[:, sq], kt) + b_ref[h], NEG_INF)
            m = jnp.max(sc, axis=-1, keepdims=True)
            if has_sink:
                m = jnp.maximum(m, s_ref[h])
            p = jnp.exp(sc - m)
            den = jnp.sum(p, axis=-1, keepdims=True)
            if has_sink:
                den = den + jnp.exp(s_ref[h] - m)
            o_ref[:, sq] = _dot(p.astype(BF16), vv) / den
            l_ref[:, h:h + 1] = m + jnp.log(den)

    qspec = pl.BlockSpec((BLOCK, wq), lambda r, i: (i, r))
    kprev = pl.BlockSpec((BLOCK, wk), lambda r, i: (jnp.maximum(i - 1, 0), r))
    kcur = pl.BlockSpec((BLOCK, wk), lambda r, i: (i, r))
    ktprev = pl.BlockSpec((None, wk, BLOCK), lambda r, i: (jnp.maximum(i - 1, 0), r, 0))
    ktcur = pl.BlockSpec((None, wk, BLOCK), lambda r, i: (i, r, 0))
    in_specs = [qspec, ktprev, ktcur, kprev, kcur, pl.BlockSpec((hq, BLOCK, 2 * BLOCK), lambda r, i: (0, 0, 0))]
    v2 = v.reshape(sd, dil * wk)
    args = [q.reshape(sd, dil * wq), kt, kt, v2, v2, bias]
    if has_sink:
        in_specs.append(pl.BlockSpec(memory_space=pltpu.SMEM))
        args.append(sinks)
    out, lse = pl.pallas_call(
        body, name=name, grid=(dil, nb), in_specs=in_specs,
        out_specs=[qspec, pl.BlockSpec((None, BLOCK, hq), lambda r, i: (r, i, 0))],
        out_shape=[jax.ShapeDtypeStruct((sd, dil * wq), F32), jax.ShapeDtypeStruct((dil, sd, hq), F32)],
        compiler_params=_params(("parallel", "parallel")),
    )(*args)
    return out.reshape(s, wq), jnp.transpose(lse, (1, 0, 2)).reshape(s, hq)


def _per_head_dilated(a, dil):
    s, h = a.shape
    return jnp.transpose(a.reshape(s // dil, dil, h), (1, 0, 2))


def _banded_bwd(q, k, kt, v, lse, dsum, do, bias, sinks, dbias_init, hq, hk, max_dist, dil, name):
    s = q.shape[0]
    wq, wk, sd, grp = hq * HEAD_DIM, hk * HEAD_DIM, s // dil, hq // hk
    nb = sd // BLOCK
    has_sink = sinks is not None

    def body(*refs):
        (q_ref, qn_ref, qt_ref, qtn_ref, kp_ref, kc_ref, ktp_ref, ktc_ref, vtp_ref, vtc_ref, l_ref, ln_ref, d_ref,
         dn_ref, do_ref, don_ref, dot_ref, dotn_ref, b_ref, dbi_ref) = refs[:20]
        rest = refs[20:]
        if has_sink:
            s_ref, dq_ref, dkt_ref, dvt_ref, db_ref, ds_ref = rest
        else:
            dq_ref, dkt_ref, dvt_ref, db_ref = rest
        j = pl.program_id(1)

        @pl.when(jnp.logical_and(pl.program_id(0) == 0, j == 0))
        def _():
            db_ref[...] = dbi_ref[...]
            if has_sink:
                ds_ref[...] = jnp.zeros_like(ds_ref)

        mprev_static, mcur = _band_masks(max_dist)
        mask = jnp.concatenate([jnp.logical_and(mprev_static, j > 0), mcur], axis=1)
        mnext = jnp.logical_and(mprev_static, j + 1 < nb)
        dkt_acc = [jnp.zeros((HEAD_DIM, BLOCK), F32) for _ in range(hk)]
        dvt_acc = [jnp.zeros((HEAD_DIM, BLOCK), F32) for _ in range(hk)]
        for h in range(hq):
            g = h // grp
            sq = slice(h * HEAD_DIM, (h + 1) * HEAD_DIM)
            sk = slice(g * HEAD_DIM, (g + 1) * HEAD_DIM)
            kt2 = jnp.concatenate([ktp_ref[sk, :], ktc_ref[sk, :]], axis=1)
            vt2 = jnp.concatenate([vtp_ref[sk, :], vtc_ref[sk, :]], axis=1)
            k2 = jnp.concatenate([kp_ref[:, sk], kc_ref[:, sk]], axis=0)
            lcol = l_ref[:, h:h + 1]
            dcol = d_ref[:, h:h + 1]
            sc = _dot(q_ref[:, sq], kt2) + b_ref[h]
            p = jnp.where(mask, jnp.exp(sc - lcol), 0.0)
            ds = p * (_dot(do_ref[:, sq], vt2) - dcol)
            dsb = ds.astype(BF16)
            dq_ref[:, sq] = (_dot(dsb, k2) * QK_SCALE).astype(BF16)
            db_ref[h] += ds
            if has_sink:
                psink = jnp.exp(s_ref[h] - lcol)
                tot = jnp.sum(psink * dcol, axis=0, keepdims=True)
                ds_ref[h:h + 1, :] -= jnp.broadcast_to(tot, (1, LANES))
            lncol = ln_ref[:, h:h + 1]
            dncol = dn_ref[:, h:h + 1]
            sn = _dot(qn_ref[:, sq], ktc_ref[sk, :]) + b_ref[h, :, 0:BLOCK]
            pn = jnp.where(mnext, jnp.exp(sn - lncol), 0.0)
            dsn = pn * (_dot(don_ref[:, sq], vtc_ref[sk, :]) - dncol)
            dkt_acc[g] = dkt_acc[g] + (_dot(qt_ref[sq, :], dsb[:, BLOCK:]) + _dot(qtn_ref[sq, :], dsn.astype(BF16)))
            dvt_acc[g] = dvt_acc[g] + (_dot(dot_ref[sq, :], p[:, BLOCK:].astype(BF16))
                                       + _dot(dotn_ref[sq, :], pn.astype(BF16)))
        for g in range(hk):
            sk = slice(g * HEAD_DIM, (g + 1) * HEAD_DIM)
            dkt_ref[sk, :] = dkt_acc[g].astype(BF16)
            dvt_ref[sk, :] = dvt_acc[g].astype(BF16)

    qcur = pl.BlockSpec((BLOCK, wq), lambda r, j: (j, r))
    qnext = pl.BlockSpec((BLOCK, wq), lambda r, j: (jnp.minimum(j + 1, nb - 1), r))
    qtcur = pl.BlockSpec((None, wq, BLOCK), lambda r, j: (j, r, 0))
    qtnext = pl.BlockSpec((None, wq, BLOCK), lambda r, j: (jnp.minimum(j + 1, nb - 1), r, 0))
    kprev = pl.BlockSpec((BLOCK, wk), lambda r, j: (jnp.maximum(j - 1, 0), r))
    kcur = pl.BlockSpec((BLOCK, wk), lambda r, j: (j, r))
    ktprev = pl.BlockSpec((None, wk, BLOCK), lambda r, j: (jnp.maximum(j - 1, 0), r, 0))
    ktcur = pl.BlockSpec((None, wk, BLOCK), lambda r, j: (j, r, 0))
    bspec = pl.BlockSpec((hq, BLOCK, 2 * BLOCK), lambda r, j: (0, 0, 0))
    hcur = pl.BlockSpec((None, BLOCK, hq), lambda r, j: (r, j, 0))
    hnext = pl.BlockSpec((None, BLOCK, hq), lambda r, j: (r, jnp.minimum(j + 1, nb - 1), 0))
    dob = do.astype(BF16)
    q2, k2, do2 = q.reshape(sd, dil * wq), k.reshape(sd, dil * wk), dob.reshape(sd, dil * wq)
    l3, d3 = _per_head_dilated(lse, dil), _per_head_dilated(dsum, dil)
    qt, vt, dot = _dilated_t(q, dil), _dilated_t(v, dil), _dilated_t(dob, dil)
    in_specs = [qcur, qnext, qtcur, qtnext, kprev, kcur, ktprev, ktcur, ktprev, ktcur, hcur, hnext, hcur, hnext,
                qcur, qnext, qtcur, qtnext, bspec, bspec]
    args = [q2, q2, qt, qt, k2, k2, kt, kt, vt, vt, l3, l3, d3, d3, do2, do2, dot, dot, bias, dbias_init]
    out_specs = [qcur, ktcur, ktcur, bspec]
    out_shape = [jax.ShapeDtypeStruct((sd, dil * wq), BF16), jax.ShapeDtypeStruct((nb, dil * wk, BLOCK), BF16),
                 jax.ShapeDtypeStruct((nb, dil * wk, BLOCK), BF16), jax.ShapeDtypeStruct((hq, BLOCK, 2 * BLOCK), F32)]
    if has_sink:
        in_specs.append(pl.BlockSpec(memory_space=pltpu.SMEM))
        args.append(sinks)
        out_specs.append(pl.BlockSpec((hq, LANES), lambda r, j: (0, 0)))
        out_shape.append(jax.ShapeDtypeStruct((hq, LANES), F32))
    res = pl.pallas_call(
        body, name=name, grid=(dil, nb), in_specs=in_specs, out_specs=out_specs, out_shape=out_shape,
        compiler_params=_params(("arbitrary", "arbitrary")),
    )(*args)
    dq, dk, dv, dbias = res[0].reshape(s, wq), _undilated(res[1], dil), _undilated(res[2], dil), res[3]
    return dq, dk, dv, dbias, (res[4][:, 0] if has_sink else None)


def _neg_softplus(z):
    return -(jnp.maximum(z, 0.0) + jnp.log(1.0 + jnp.exp(-jnp.abs(z))))


SB_CHUNK = 256
HEADS_PER_PAIR = LANES // HEAD_DIM


def _tri(kind):
    row = lax.broadcasted_iota(jnp.int32, (SB_CHUNK, SB_CHUNK), 0)
    col = lax.broadcasted_iota(jnp.int32, (SB_CHUNK, SB_CHUNK), 1)
    return {"ge": row >= col, "lt": row < col, "le": row <= col}[kind].astype(BF16)


def _keys_on_lanes(a, rows):
    s, w = a.shape
    return jnp.transpose(a.reshape(s // rows, rows, w), (0, 2, 1))


def _sb_mask(i, jj):
    row = lax.broadcasted_iota(jnp.int32, (BLOCK, SB_CHUNK), 0)
    col = lax.broadcasted_iota(jnp.int32, (BLOCK, SB_CHUNK), 1)
    return col < row + (i * BLOCK - jj * SB_CHUNK)


def _sb_trips(i):
    return (i * BLOCK) // (2 * SB_CHUNK) + 1


def _sb_rows(jj, n):
    return pl.ds(pl.multiple_of(jj * SB_CHUNK, SB_CHUNK), n * SB_CHUNK)


def _sb_fwd(q, kt, v, name):
    s, w = q.shape
    npair, nb, nc = w // LANES, s // BLOCK, s // SB_CHUNK

    def body(q_ref, kt_ref, v_ref, o_ref, t_ref):
        i = pl.program_id(1)
        lincl = _tri("ge")
        heads = [slice(hh * HEAD_DIM, (hh + 1) * HEAD_DIM) for hh in range(HEADS_PER_PAIR)]
        qs = [q_ref[:, sl] for sl in heads]

        def trip(t, carry, masked):
            lo, hi = 2 * t, 2 * t + 1
            mlo, mhi = (_sb_mask(i, lo), _sb_mask(i, hi)) if masked else (None, None)

            def keep(m, val):
                return val if m is None else jnp.where(m, val, 0.0)

            new = []
            for hh, sl in enumerate(heads):
                o_acc, rem = carry[hh]
                zhi = _dot(qs[hh], kt_ref[hi, sl, :])
                zlo = _dot(qs[hh], kt_ref[lo, sl, :])
                lrhi = keep(mhi, _neg_softplus(zhi))
                lrlo = keep(mlo, _neg_softplus(zlo))
                tothi = jnp.sum(lrhi, axis=-1, keepdims=True)
                ahi = keep(mhi, jnp.exp(zhi + (rem + _split_dot(lrhi, lincl))))
                alo = keep(mlo, jnp.exp(zlo + (rem + tothi + _split_dot(lrlo, lincl))))
                a = jnp.concatenate([alo, ahi], axis=1).astype(BF16)
                new.append((o_acc + _dot(a, v_ref[_sb_rows(lo, 2), sl]),
                            rem + tothi + jnp.sum(lrlo, axis=-1, keepdims=True)))
            return tuple(new)

        init = tuple((jnp.zeros((BLOCK, HEAD_DIM), F32), jnp.zeros((BLOCK, 1), F32)) for _ in heads)
        trips = _sb_trips(i)
        carry = trip(trips - 1, init, True)
        carry = lax.fori_loop(0, trips - 1, lambda t, cr: trip(trips - 2 - t, cr, False), carry)
        for hh, sl in enumerate(heads):
            o_ref[:, sl] = carry[hh][0]
            t_ref[:, sl] = jnp.broadcast_to(carry[hh][1], (BLOCK, HEAD_DIM))

    qspec = pl.BlockSpec((BLOCK, LANES), lambda p, i: (i, p))
    return pl.pallas_call(
        body, name=name, grid=(npair, nb),
        in_specs=[qspec, pl.BlockSpec((nc, LANES, SB_CHUNK), lambda p, i: (0, p, 0)),
                  pl.BlockSpec((s, LANES), lambda p, i: (0, p))],
        out_specs=[qspec, qspec], out_shape=[jax.ShapeDtypeStruct((s, w), F32)] * 2,
        compiler_params=_params(("parallel", "parallel")),
    )(q, kt, v)


def _sb_bwd(q, k, kt, v, tot, do, name):
    s, w = q.shape
    npair, nb, nc = w // LANES, s // BLOCK, s // SB_CHUNK
    dob = do.astype(BF16)

    def body(q_ref, qt_ref, k_ref, kt_ref, vt_ref, t_ref, do_ref, dot_ref, dq_ref, dkt_ref, dvt_ref):
        i = pl.program_id(1)

        @pl.when(i == 0)
        def _():
            dkt_ref[...] = jnp.zeros_like(dkt_ref)
            dvt_ref[...] = jnp.zeros_like(dvt_ref)

        lbefore = _tri("lt")
        lupto = _tri("le")
        heads = [slice(hh * HEAD_DIM, (hh + 1) * HEAD_DIM) for hh in range(HEADS_PER_PAIR)]
        qs = [q_ref[:, sl] for sl in heads]
        qts = [qt_ref[sl, :] for sl in heads]
        dos = [do_ref[:, sl] for sl in heads]
        dots = [dot_ref[sl, :] for sl in heads]
        totals = [t_ref[:, sl.start:sl.start + 1] for sl in heads]

        def trip(t, carry, masked):
            chunks = (2 * t, 2 * t + 1)
            masks = [_sb_mask(i, jj) if masked else None for jj in chunks]

            def keep(m, val):
                return val if m is None else jnp.where(m, val, 0.0)

            new = []
            for hh, sl in enumerate(heads):
                dq_acc, plr, pg = carry[hh]
                zs = [_dot(qs[hh], kt_ref[jj, sl, :]) for jj in chunks]
                lrs = [keep(m, _neg_softplus(z)) for m, z in zip(masks, zs)]
                lr_sums = [jnp.sum(lr, axis=-1, keepdims=True) for lr in lrs]
                before = [plr, plr + lr_sums[0]]
                avs = [keep(m, jnp.exp(z + (totals[hh] - (b + _split_dot(lr, lbefore)))))
                       for m, z, lr, b in zip(masks, zs, lrs, before)]
                gs = [_dot(dos[hh], vt_ref[jj, sl, :]) * a for jj, a in zip(chunks, avs)]
                g_sums = [jnp.sum(g, axis=-1, keepdims=True) for g in gs]
                upto = [pg, pg + g_sums[0]]
                dzs = [keep(m, g - jnp.exp(z + lr) * (u + _split_dot(g, lupto))).astype(BF16)
                       for m, z, lr, g, u in zip(masks, zs, lrs, gs, upto)]
                for jj, dzb, a in zip(chunks, dzs, avs):
                    dkt_ref[jj, sl, :] += _dot(qts[hh], dzb)
                    dvt_ref[jj, sl, :] += _dot(dots[hh], a.astype(BF16))
                dz2 = jnp.concatenate(dzs, axis=1)
                new.append((dq_acc + _dot(dz2, k_ref[_sb_rows(chunks[0], 2), sl]), plr + lr_sums[0] + lr_sums[1],
                            pg + g_sums[0] + g_sums[1]))
            return tuple(new)

        zero = jnp.zeros((BLOCK, 1), F32)
        init = tuple((jnp.zeros((BLOCK, HEAD_DIM), F32), zero, zero) for _ in heads)
        trips = _sb_trips(i)
        carry = lax.fori_loop(0, trips - 1, lambda t, cr: trip(t, cr, False), init)
        carry = trip(trips - 1, carry, True)
        for hh, sl in enumerate(heads):
            dq_ref[:, sl] = (carry[hh][0] * QK_SCALE).astype(BF16)

    qspec = pl.BlockSpec((BLOCK, LANES), lambda p, i: (i, p))
    qtspec = pl.BlockSpec((None, LANES, BLOCK), lambda p, i: (i, p, 0))
    kspec = pl.BlockSpec((s, LANES), lambda p, i: (0, p))
    ktspec = pl.BlockSpec((nc, LANES, SB_CHUNK), lambda p, i: (0, p, 0))
    dq, dkt, dvt = pl.pallas_call(
        body, name=name, grid=(npair, nb), in_specs=[qspec, qtspec, kspec, ktspec, ktspec, qspec, qspec, qtspec],
        out_specs=[qspec, ktspec, ktspec],
        out_shape=[jax.ShapeDtypeStruct((s, w), BF16)] + [jax.ShapeDtypeStruct((nc, w, SB_CHUNK), F32)] * 2,
        compiler_params=_params(("parallel", "arbitrary")),
    )(q, _keys_on_lanes(q, BLOCK), k, kt, _keys_on_lanes(v, SB_CHUNK), tot, dob, _keys_on_lanes(dob, BLOCK))

    def rows_first(t):
        return jnp.transpose(t, (0, 2, 1)).reshape(s, w)

    return dq, rows_first(dkt), rows_first(dvt)


def _group_norm(xv, g):
    r = lax.rsqrt(jnp.mean(xv * xv, axis=-1, keepdims=True) + EPS)
    return xv * r * g


def _head_spread(nheads):
    return jnp.asarray(np.repeat(np.eye(nheads, dtype=np.float32), HEAD_DIM, axis=1), dtype=BF16)


def _mix_fwd(oa, ob, ocs, lses, gain, cfg, name):
    s = oa.shape[0]
    ts = _div_tile(s, 256, 16)
    aq, bw, cw, nhc = cfg.a_q, cfg.b_w, cfg.c_w, cfg.nhc

    def body(oa_ref, ob_ref, c1, c2, c3, l1, l2, l3, sp_ref, g_ref, mix_ref, oc_ref, lse_ref):
        m = jnp.maximum(jnp.maximum(l1[...], l2[...]), l3[...])
        es = [jnp.exp(l[...] - m) for l in (l1, l2, l3)]
        den = es[0] + es[1] + es[2]
        oc = sum(_split_dot(e / den, sp_ref[...]) * c[...] for e, c in zip(es, (c1, c2, c3)))
        oc_ref[...] = oc
        lse_ref[...] = m + jnp.log(den)
        mix_ref[:, 0:aq] = _group_norm(oa_ref[...], g_ref[:, 0:aq]).astype(BF16)
        mix_ref[:, aq:aq + bw] = _group_norm(ob_ref[...], g_ref[:, aq:aq + bw]).astype(BF16)
        mix_ref[:, aq + bw:] = _group_norm(oc, g_ref[:, aq + bw:]).astype(BF16)

    def row(wd):
        return pl.BlockSpec((ts, wd), lambda i: (i, 0))

    return pl.pallas_call(
        body, name=name, grid=(s // ts,),
        in_specs=[row(aq), row(bw)] + [row(cw)] * 3 + [row(nhc)] * 3
        + [pl.BlockSpec((nhc, cw), lambda i: (0, 0)), pl.BlockSpec((1, cfg.d), lambda i: (0, 0))],
        out_specs=[row(cfg.d), row(cw), row(nhc)],
        out_shape=[jax.ShapeDtypeStruct((s, cfg.d), BF16), jax.ShapeDtypeStruct((s, cw), F32),
                   jax.ShapeDtypeStruct((s, nhc), F32)],
        compiler_params=_params(("parallel",)),
    )(oa, ob, *ocs, *lses, _head_spread(nhc), gain)


def _mix_bwd(dmix, oa, ob, oc, gain, cfg, name):
    s = oa.shape[0]
    ts = _div_tile(s, 256, 8)
    aq, bw, cw = cfg.a_q, cfg.b_w, cfg.c_w

    def body(dm_ref, oa_ref, ob_ref, oc_ref, g_ref, fa_ref, fc_ref, da_ref, db_ref, dc_ref, dg_ref, sa_ref, sc_ref):
        @pl.when(pl.program_id(0) == 0)
        def _():
            dg_ref[...] = jnp.zeros_like(dg_ref)

        for x_ref, dx_ref, lo, hi, fold in ((oa_ref, da_ref, 0, aq, (fa_ref, sa_ref)), (ob_ref, db_ref, aq, aq + bw, None),
                                            (oc_ref, dc_ref, aq + bw, aq + bw + cw, (fc_ref, sc_ref))):
            xv = x_ref[...]
            dy = dm_ref[:, lo:hi]
            r = lax.rsqrt(jnp.mean(xv * xv, axis=-1, keepdims=True) + EPS)
            xhat = xv * r
            dxhat = dy * g_ref[:, lo:hi]
            dx = r * (dxhat - xhat * jnp.mean(dxhat * xhat, axis=-1, keepdims=True))
            dx_ref[...] = dx
            dg_ref[:, lo:hi] += jnp.sum(dy * xhat, axis=0, keepdims=True)
            if fold is not None:
                fold[1][...] = _split_dot(dx * xv, fold[0][...])

    def row(wd):
        return pl.BlockSpec((ts, wd), lambda i: (i, 0))

    vec = pl.BlockSpec((1, cfg.d), lambda i: (0, 0))
    return pl.pallas_call(
        body, name=name, grid=(s // ts,),
        in_specs=[row(cfg.d), row(aq), row(bw), row(cw), vec, pl.BlockSpec((aq, cfg.nha), lambda i: (0, 0)),
                  pl.BlockSpec((cw, cfg.nhc), lambda i: (0, 0))],
        out_specs=[row(aq), row(bw), row(cw), vec, row(cfg.nha), row(cfg.nhc)],
        out_shape=[jax.ShapeDtypeStruct((s, aq), F32), jax.ShapeDtypeStruct((s, bw), F32),
                   jax.ShapeDtypeStruct((s, cw), F32), jax.ShapeDtypeStruct((1, cfg.d), F32),
                   jax.ShapeDtypeStruct((s, cfg.nha), F32), jax.ShapeDtypeStruct((s, cfg.nhc), F32)],
        compiler_params=_params(("arbitrary",)),
    )(dmix, oa, ob, oc, gain, _head_spread(cfg.nha).T, _head_spread(cfg.nhc).T)


def _bias_table_grad(dbiases, buckets, name):
    outs = []
    for idx, (db, bk) in enumerate(zip(dbiases, buckets)):
        h = db.shape[0]

        def body(db_ref, bk_ref, o_ref):
            xv = db_ref[0]
            ids = bk_ref[...]
            lane = lax.broadcasted_iota(jnp.int32, (1, LANES), 1)
            acc = jnp.zeros((1, LANES), F32)
            for b in range(N_BUCKETS):
                tot = jnp.sum(jnp.where(ids == b, xv, 0.0), axis=0, keepdims=True)
                tot = jnp.sum(tot, axis=1, keepdims=True)
                acc = jnp.where(lane == b, tot, acc)
            o_ref[0] = acc

        outs.append(pl.pallas_call(
            body, name=f"{name}_{idx}", grid=(h,),
            in_specs=[pl.BlockSpec((1, BLOCK, 2 * BLOCK), lambda i: (i, 0, 0)),
                      pl.BlockSpec((BLOCK, 2 * BLOCK), lambda i: (0, 0))],
            out_specs=pl.BlockSpec((1, 1, LANES), lambda i: (i, 0, 0)),
            out_shape=jax.ShapeDtypeStruct((h, 1, LANES), F32), compiler_params=_params(("parallel",)),
        )(db, bk)[:, 0, :])
    return outs


SUBLANES = 8


def _shift_down(u, n, rows):
    r = pltpu.roll(u, n, 0)
    return jnp.concatenate([jnp.where(rows[:SUBLANES] >= n, r[:SUBLANES], 0.0), r[SUBLANES:]], axis=0)


def _shift_up(u, n, rows, s):
    r = pltpu.roll(u, s - n, 0)
    return jnp.concatenate([r[:s - SUBLANES], jnp.where(rows[s - SUBLANES:] < s - n, r[s - SUBLANES:], 0.0)], axis=0)


def _conv(u, w_ref, b_ref, rows):
    return (b_ref[...] + w_ref[0:1, :] * _shift_down(u, 2, rows) + w_ref[1:2, :] * _shift_down(u, 1, rows)
            + w_ref[2:3, :] * u)


def _conv_act_fwd(u, conv_w, conv_b, f, name):
    s = u.shape[0]
    nf = f // LANES

    def body(ug_ref, uu_ref, wg_ref, wu_ref, bg_ref, bu_ref, act_ref):
        rows = lax.broadcasted_iota(jnp.int32, (s, LANES), 0)
        gate = _conv(ug_ref[...], wg_ref, bg_ref, rows)
        up = _conv(uu_ref[...], wu_ref, bu_ref, rows)
        act_ref[...] = (gate * jax.nn.sigmoid(gate) * up).astype(BF16)

    def col(rws, off):
        return pl.BlockSpec((rws, LANES), lambda j: (0, j + off))

    return pl.pallas_call(
        body, name=name, grid=(nf,),
        in_specs=[col(s, 0), col(s, nf), col(CONV_WIDTH, 0), col(CONV_WIDTH, nf), col(1, 0), col(1, nf)],
        out_specs=col(s, 0), out_shape=jax.ShapeDtypeStruct((s, f), BF16), compiler_params=_params(("parallel",)),
    )(u, u, conv_w, conv_w, conv_b, conv_b)


def _conv_act_bwd(u, dact, conv_w, conv_b, f, name):
    s = u.shape[0]
    nf = f // LANES

    def body(ug_ref, uu_ref, da_ref, wg_ref, wu_ref, bg_ref, bu_ref, dug_ref, duu_ref, dwg_ref, dwu_ref, dbg_ref,
             dbu_ref):
        rows = lax.broadcasted_iota(jnp.int32, (s, LANES), 0)
        ug, uu = ug_ref[...], uu_ref[...]
        gate = _conv(ug, wg_ref, bg_ref, rows)
        up = _conv(uu, wu_ref, bu_ref, rows)
        sg = jax.nn.sigmoid(gate)
        da = da_ref[...]
        dgate = da * up * (sg * (1.0 + gate * (1.0 - sg)))
        dup = da * (gate * sg)
        for du, uv, w_ref, du_ref, dw_ref, db_ref in ((dgate, ug, wg_ref, dug_ref, dwg_ref, dbg_ref),
                                                     (dup, uu, wu_ref, duu_ref, dwu_ref, dbu_ref)):
            du_ref[...] = (w_ref[2:3, :] * du + w_ref[1:2, :] * _shift_up(du, 1, rows, s)
                           + w_ref[0:1, :] * _shift_up(du, 2, rows, s)).astype(BF16)
            dw_ref[0:1, :] = jnp.sum(du * _shift_down(uv, 2, rows), axis=0, keepdims=True)
            dw_ref[1:2, :] = jnp.sum(du * _shift_down(uv, 1, rows), axis=0, keepdims=True)
            dw_ref[2:3, :] = jnp.sum(du * uv, axis=0, keepdims=True)
            db_ref[...] = jnp.sum(du, axis=0, keepdims=True)

    def col(rws, off):
        return pl.BlockSpec((rws, LANES), lambda j: (0, j + off))

    return pl.pallas_call(
        body, name=name, grid=(nf,),
        in_specs=[col(s, 0), col(s, nf), col(s, 0), col(CONV_WIDTH, 0), col(CONV_WIDTH, nf), col(1, 0), col(1, nf)],
        out_specs=[col(s, 0), col(s, 0), col(CONV_WIDTH, 0), col(CONV_WIDTH, 0), col(1, 0), col(1, 0)],
        out_shape=[jax.ShapeDtypeStruct((s, f), BF16)] * 2 + [jax.ShapeDtypeStruct((CONV_WIDTH, f), F32)] * 2
        + [jax.ShapeDtypeStruct((1, f), F32)] * 2,
        compiler_params=_params(("parallel",)),
    )(u, u, dact, conv_w, conv_w, conv_b, conv_b)


def _loss_head(y, target, name):
    s, d = y.shape
    ts = _div_tile(s, 256, 16)

    def body(y_ref, t_ref, dy_ref, dyb_ref, l_ref):
        @pl.when(pl.program_id(0) == 0)
        def _():
            l_ref[...] = jnp.zeros_like(l_ref)

        err = y_ref[...] - t_ref[...]
        dy = err * (1.0 / d)
        dy_ref[...] = dy
        dyb_ref[...] = dy.astype(BF16)
        tot = jnp.sum(jnp.sum(err * err, axis=0, keepdims=True), axis=1, keepdims=True) * (0.5 / d)
        l_ref[...] += jnp.broadcast_to(tot, l_ref.shape)

    row = pl.BlockSpec((ts, d), lambda i: (i, 0))
    return pl.pallas_call(
        body, name=name, grid=(s // ts,), in_specs=[row, row],
        out_specs=[row, row, pl.BlockSpec((8, LANES), lambda i: (0, 0))],
        out_shape=[jax.ShapeDtypeStruct((s, d), F32), jax.ShapeDtypeStruct((s, d), BF16),
                   jax.ShapeDtypeStruct((8, LANES), F32)],
        compiler_params=_params(("arbitrary",)),
    )(y, target)


def _adamw(w, g, m, v, name):
    r, c = w.shape
    tr = _div_tile(r, max(8, (1 << 18) // c // 8 * 8), 8)
    c1 = 1.0 - ADAM_B1 ** ADAM_STEP
    c2 = 1.0 - ADAM_B2 ** ADAM_STEP

    def body(w_ref, g_ref, m_ref, v_ref, d_ref, nm_ref, nv_ref):
        gv = g_ref[...]
        nm = ADAM_B1 * m_ref[...] + (1.0 - ADAM_B1) * gv
        nv = ADAM_B2 * v_ref[...] + (1.0 - ADAM_B2) * (gv * gv)
        d_ref[...] = -ADAM_LR * ((nm / c1) / (jnp.sqrt(nv / c2) + ADAM_EPS) + ADAM_WD * w_ref[...])
        nm_ref[...] = nm
        nv_ref[...] = nv

    spec = pl.BlockSpec((tr, c), lambda i: (i, 0))
    return pl.pallas_call(
        body, name=name, grid=(r // tr,), in_specs=[spec] * 4, out_specs=[spec] * 3,
        out_shape=[jax.ShapeDtypeStruct((r, c), F32)] * 3, compiler_params=_params(("parallel",)),
    )(w, g, m, v)


def _adamw_layer(layer, w, g, m, v, bufs, name):
    depth, r, c = w.shape
    tr = _div_tile(r, max(8, (1 << 17) // c // 8 * 8), 8)
    c1 = 1.0 - ADAM_B1 ** ADAM_STEP
    c2 = 1.0 - ADAM_B2 ** ADAM_STEP

    def body(*refs):
        w_ref, g_ref, m_ref, v_ref = refs[:4]
        go_ref, d_ref, nm_ref, nv_ref = refs[-4:]
        gv = g_ref[...]
        nm = ADAM_B1 * m_ref[...] + (1.0 - ADAM_B1) * gv
        nv = ADAM_B2 * v_ref[...] + (1.0 - ADAM_B2) * (gv * gv)
        d_ref[...] = -ADAM_LR * ((nm / c1) / (jnp.sqrt(nv / c2) + ADAM_EPS) + ADAM_WD * w_ref[...])
        nm_ref[...] = nm
        nv_ref[...] = nv
        go_ref[...] = gv

    lay = pl.BlockSpec((None, tr, c), lambda i: (layer, i, 0))
    in_specs = [lay, pl.BlockSpec((tr, c), lambda i: (i, 0)), lay, lay]
    args = [w, g, m, v]
    aliases = {}
    if bufs is not None:
        in_specs += [pl.BlockSpec(memory_space=pl.ANY)] * 4
        args += list(bufs)
        aliases = {4 + k: k for k in range(4)}
    return pl.pallas_call(
        body, name=name, grid=(r // tr,), in_specs=in_specs, out_specs=[lay] * 4,
        out_shape=[jax.ShapeDtypeStruct((depth, r, c), F32)] * 4, input_output_aliases=aliases,
        compiler_params=_params(("parallel",)),
    )(*args)


def _mesh_pos():
    return lax.axis_index("x"), lax.axis_index("y"), lax.axis_index("c")


def _flip(v, bit):
    return 1 - v if bit else v


def _sum_parts(parts, name):
    _, r, c = parts.shape
    tr = _div_tile(r, 256, 16)

    def body(p_ref, o_ref):
        acc = p_ref[0].astype(F32)
        for src in range(1, N_DEVICES):
            acc = acc + p_ref[src].astype(F32)
        o_ref[...] = acc

    return pl.pallas_call(
        body, name=name, grid=(r // tr,), in_specs=[pl.BlockSpec((N_DEVICES, tr, c), lambda i: (0, i, 0))],
        out_specs=pl.BlockSpec((tr, c), lambda i: (i, 0)), out_shape=jax.ShapeDtypeStruct((r, c), F32),
        compiler_params=_params(("parallel",)),
    )(parts)


def _split_start(srcs, lands, plan, ncopies, name):
    nbuf = len(srcs) + len(lands)

    def body(*refs):
        bufs = refs[:nbuf]
        send_sem, recv_sem, token = refs[nbuf], refs[nbuf + 1], refs[-1]
        for k, (src, dst, dev) in enumerate(plan(bufs[:len(srcs)], bufs[len(srcs):])):
            pltpu.make_async_remote_copy(src_ref=src, dst_ref=dst, send_sem=send_sem.at[k], recv_sem=recv_sem.at[k],
                                         device_id=dev, device_id_type=MESH).start()
        token[...] = jnp.zeros_like(token)

    hbm = pl.BlockSpec(memory_space=pltpu.HBM)
    sem = pl.BlockSpec(memory_space=pltpu.SEMAPHORE)
    operands = [pltpu.with_memory_space_constraint(a, pltpu.HBM) for a in (*srcs, *lands)]
    outs = pl.pallas_call(
        body, name=name, in_specs=[hbm] * nbuf,
        out_specs=(sem, sem, *[hbm] * nbuf, pl.BlockSpec(memory_space=pltpu.VMEM)),
        out_shape=(pltpu.SemaphoreType.DMA((ncopies,)), pltpu.SemaphoreType.DMA((ncopies,)),
                   *[pltpu.HBM(a.shape, a.dtype) for a in operands], jax.ShapeDtypeStruct((8, LANES), F32)),
        input_output_aliases={i: 2 + i for i in range(nbuf)},
        compiler_params=pltpu.CompilerParams(has_side_effects=pltpu.SideEffectType.DATAFLOW_SIDE_EFFECTING),
    )(*operands)
    handle = dict(send=outs[0], recv=outs[1], bufs=list(outs[2:2 + nbuf]), nsrc=len(srcs), plan=plan)
    return handle, outs[-1]


def _split_wait(handle, after, name):
    nbuf, nsrc, plan = len(handle["bufs"]), handle["nsrc"], handle["plan"]

    def body(*refs):
        bufs = refs[:nbuf]
        send_sem, recv_sem = refs[nbuf], refs[nbuf + 1]
        for k, (src, dst, dev) in enumerate(plan(bufs[:nsrc], bufs[nsrc:])):
            copy = pltpu.make_async_remote_copy(src_ref=src, dst_ref=dst, send_sem=send_sem.at[k],
                                                recv_sem=recv_sem.at[k], device_id=dev, device_id_type=MESH)
            copy.wait_send()
            copy.wait_recv()

    hbm = pl.BlockSpec(memory_space=pltpu.HBM)
    sem = pl.BlockSpec(memory_space=pltpu.SEMAPHORE)
    outs = pl.pallas_call(
        body, name=name, in_specs=[hbm] * nbuf + [sem, sem, pl.BlockSpec(memory_space=pl.ANY)],
        out_specs=[hbm] * nbuf, out_shape=[pltpu.HBM(a.shape, a.dtype) for a in handle["bufs"]],
        input_output_aliases={i: i for i in range(nbuf)},
        compiler_params=pltpu.CompilerParams(has_side_effects=pltpu.SideEffectType.DATAFLOW_SIDE_EFFECTING),
    )(*handle["bufs"], handle["send"], handle["recv"], after)
    return list(outs[nsrc:])


def _own_slot(shape, dtype, block, index):
    return lax.dynamic_update_slice(lax.empty(shape, dtype), block[None], (index,) + (0,) * block.ndim)


def _gather_plan(srcs, lands):
    x, y, c = _mesh_pos()
    return [(land.at[2 * x + y], land.at[2 * x + y], (*chip, c))
            for land in lands for chip in ((1 - x, y), (x, 1 - y), (1 - x, 1 - y))]


def _scatter_plan(srcs, lands):
    x, y, c = _mesh_pos()
    out = []
    for src, land in zip(srcs, lands):
        half = src.shape[1] // 2
        for d in range(1, N_DEVICES):
            p = (_flip(x, d & 4), _flip(y, d & 2), _flip(c, d & 1))
            out.append((src.at[2 * p[0] + p[1], pl.ds(p[2] * half, half), :], land.at[4 * x + 2 * y + c], p))
    return out


def _swap_plan(srcs, lands):
    x, y, c = _mesh_pos()
    return [(src, land.at[c], (x, y, 1 - c)) for src, land in zip(srcs, lands)]


class _Gathered:
    def __init__(self, groups):
        self.groups = groups
        self.ready = {}

    def get(self, name, after=None):
        if name not in self.ready:
            handle, names, wait_name = next(g for g in self.groups if name in g[1])
            for n, full in zip(names, _split_wait(handle, after, wait_name)):
                self.ready[n] = full.reshape(-1, full.shape[-1])
        return self.ready[name]


def _allreduce_small(flat, name):
    r = flat.shape[0]

    def body(x_ref, o_ref, buf, send_sems, recv_sems):
        x, y, c = _mesh_pos()
        me = 4 * x + 2 * y + c
        buf[me] = x_ref[...]
        started = []
        peers = [(_flip(x, d & 4), _flip(y, d & 2), _flip(c, d & 1)) for d in range(1, N_DEVICES)]
        for d, p in enumerate(peers):
            cp = pltpu.make_async_remote_copy(src_ref=x_ref, dst_ref=buf.at[me], send_sem=send_sems.at[d],
                                              recv_sem=recv_sems.at[d], device_id=p, device_id_type=MESH)
            cp.start()
            started.append(cp)
        for d, p in enumerate(peers):
            slot = buf.at[4 * p[0] + 2 * p[1] + p[2]]
            pltpu.make_async_remote_copy(src_ref=slot, dst_ref=slot, send_sem=send_sems.at[d], recv_sem=recv_sems.at[d],
                                         device_id=p, device_id_type=MESH).wait_recv()
        for cp in started:
            cp.wait_send()
        acc = buf[0]
        for src in range(1, N_DEVICES):
            acc = acc + buf[src]
        o_ref[...] = acc

    vm = pl.BlockSpec(memory_space=pltpu.VMEM)
    return pl.pallas_call(
        body, name=name, in_specs=[vm], out_specs=vm, out_shape=jax.ShapeDtypeStruct((r, LANES), F32),
        scratch_shapes=[pltpu.VMEM((N_DEVICES, r, LANES), F32), pltpu.SemaphoreType.DMA((N_DEVICES - 1,)),
                        pltpu.SemaphoreType.DMA((N_DEVICES - 1,))],
        compiler_params=pltpu.CompilerParams(vmem_limit_bytes=VMEM_LIMIT_BYTES),
    )(flat)


def _bucket_ids(dil):
    rel = (np.arange(BLOCK)[:, None] + BLOCK - np.arange(2 * BLOCK)[None, :]) * dil
    max_exact = N_BUCKETS // 2
    d = np.maximum(rel, 0)
    large = max_exact + (np.log(np.maximum(d, 1).astype(np.float32) / max_exact)
                         / np.float32(np.log(T5_MAX_DIST / max_exact)) * (N_BUCKETS - max_exact)).astype(np.int32)
    large = np.minimum(large, N_BUCKETS - 1)
    return np.where(d < max_exact, d, large).astype(np.int32)


def _block_bias(table, dil):
    onehot = (jnp.asarray(_bucket_ids(dil))[:, :, None] == jnp.arange(N_BUCKETS)[None, None, :]).astype(F32)
    return jnp.einsum("ijb,bh->hij", onehot, table.astype(F32), precision=lax.Precision.HIGHEST)


def _tile_gain(g, n):
    return jnp.tile(g.reshape(1, HEAD_DIM), (1, n))


def _layer_fwd(x, p, cfg):
    w = p["weights"]
    h1 = _rmsnorm_fwd(x, p["attn_norm"], "attn_norm_fwd")
    proj = _matmul(h1, w.get("w_in_t", h1), "nt", F32, "in_proj", tm=1024, tn=896, tk=2048)
    aq, ak, av, bq, bk, bv, cq, ck, cv = _qk_prep(proj, p["gains"], cfg, "qk_prep")
    akt = _dilated_t(ak, 1)
    oa, lse_a = _banded_fwd(aq, akt, av, p["bias_a"], p["sinks"], cfg.nha, cfg.nkva, WINDOW_A - 1, 1, "swa_fwd")
    bkt = _keys_on_lanes(bk, SB_CHUNK)
    ob, tot_b = _sb_fwd(bq, bkt, bv, "stickbreak_fwd")
    ocs, lses, ckts = [], [], []
    for (window, dil), bias in zip(DILATED_PAIRS, p["bias_c"]):
        ckts.append(_dilated_t(ck, dil))
        o, l = _banded_fwd(cq, ckts[-1], cv, bias, None, cfg.nhc, cfg.nhc, window // dil, dil, f"dilated{dil}_fwd")
        ocs.append(o)
        lses.append(l)
    mix, oc, lse_c = _mix_fwd(oa, ob, ocs, lses, p["mix_gain"], cfg, "mix_fwd")
    xm = _matmul(mix, w.get("w_out", mix), "nn", F32, "out_proj", tm=1024, tn=1024, tk=2048, residual=x)
    h2 = _rmsnorm_fwd(xm, p["ffn_norm"], "ffn_norm_fwd")
    u = _matmul(h2, w.get("w_up_t", h2), "nt", F32, "up_proj", tm=1024, tn=1024, tk=2048)
    act = _conv_act_fwd(u, p["conv_w"], p["conv_b"], cfg.f, "conv_act_fwd")
    y = _matmul(act, w.get("w_down", act), "nn", F32, "down_proj", tm=1024, tn=1024, tk=1408, residual=xm)
    saved = dict(x=x, h1=h1, proj=proj, q=(aq, ak, av, bq, bk, bv, cq, ck, cv), oa=oa, lse_a=lse_a, ob=ob,
                 tot_b=tot_b, akt=akt, bkt=bkt, ckts=ckts, oc=oc, lse_c=lse_c, mix=mix, xm=xm, h2=h2, u=u, act=act)
    return y, saved


def _layer_bwd(dy, dyb, sv, p, dbias, cfg, on_grad):
    aq, ak, av, bq, bk, bv, cq, ck, cv = sv["q"]
    w = p["weights"]
    anchor = on_grad(_matmul(sv["act"], dyb, "tn", BF16, "down_proj_dw", tm=1408, tn=2048, tk=1024))
    dact = _matmul(dyb, w.get("w_down"), "nt", F32, "down_proj_dx", tm=1024, tn=1408, tk=2048)
    dug, duu, dwg, dwu, dbg, dbu = _conv_act_bwd(sv["u"], dact, p["conv_w"], p["conv_b"] + anchor, cfg.f,
                                                 "conv_act_bwd")
    du = jnp.concatenate([dug, duu], axis=1)
    anchor = on_grad(_matmul(du, sv["h2"], "tn", BF16, "up_proj_dw", tm=1408, tn=2048, tk=1024))
    dh2 = _matmul(du, w.get("w_up_t"), "nn", F32, "up_proj_dx", tm=1024, tn=2048, tk=1024)
    dxm, dxmb, g_ffn_norm = _rmsnorm_bwd(sv["xm"], p["ffn_norm"] + anchor, dh2, dy, "ffn_norm_bwd")
    anchor = on_grad(_matmul(sv["mix"], dxmb, "tn", BF16, "out_proj_dw", tm=1024, tn=2048, tk=1024))
    dmix = _matmul(dxmb, w.get("w_out"), "nt", F32, "out_proj_dx", tm=1024, tn=1024, tk=2048)
    doa, dob, doc, g_mix_gain, dsum_a, dsum_c = _mix_bwd(dmix, sv["oa"], sv["ob"], sv["oc"], p["mix_gain"] + anchor,
                                                         cfg, "mix_bwd")
    daq, dak, dav, dbias_a, g_sinks = _banded_bwd(aq, ak, sv["akt"], av, sv["lse_a"], dsum_a, doa, p["bias_a"],
                                                 p["sinks"], dbias[0], cfg.nha, cfg.nkva, WINDOW_A - 1, 1, "swa_bwd")
    dbq, dbk, dbv = _sb_bwd(bq, bk, sv["bkt"], bv, sv["tot_b"], dob, "stickbreak_bwd")
    dcq, dck, dcv, dbias_c = [], [], [], []
    for idx, ((window, dil), bias) in enumerate(zip(DILATED_PAIRS, p["bias_c"])):
        a, b, c, d, _ = _banded_bwd(cq, ck, sv["ckts"][idx], cv, sv["lse_c"], dsum_c, doc, bias, None, dbias[1][idx],
                                    cfg.nhc, cfg.nhc, window // dil, dil, f"dilated{dil}_bwd")
        dcq.append(a)
        dck.append(b)
        dcv.append(c)
        dbias_c.append(d)
    dproj, g_aq, g_ak, g_cq, g_ck = _qk_prep_bwd(
        sv["proj"], p["gains"], [[daq], [dak], [dav], [dbq], [dbk], [dbv], dcq, dck, dcv], cfg, "qk_prep_bwd")
    anchor = on_grad(_matmul(dproj, sv["h1"], "tn", BF16, "in_proj_dw", tm=768, tn=2048, tk=1024))
    dh1 = _matmul(dproj, w.get("w_in_t"), "nn", F32, "in_proj_dx", tm=1024, tn=2048, tk=768)
    dx, dxb, g_attn_norm = _rmsnorm_bwd(sv["x"], p["attn_norm"] + anchor, dh1, dxm, "attn_norm_bwd")

    def fold(g):
        return jnp.sum(g.reshape(-1, HEAD_DIM), axis=0)

    small = dict(attn_norm=g_attn_norm[0], a_q_gain=fold(g_aq), a_k_gain=fold(g_ak), a_sinks=g_sinks,
                 c_q_gain=fold(g_cq), c_k_gain=fold(g_ck), mix_out_gain=g_mix_gain[0], ffn_norm=g_ffn_norm[0],
                 conv_w=jnp.concatenate([dwg, dwu], axis=1), conv_b=jnp.concatenate([dbg, dbu], axis=1)[0])
    return dx, dxb, small, (dbias_a, dbias_c)


_SMALL = ("attn_norm", "a_q_gain", "a_k_gain", "a_sinks", "c_q_gain", "c_k_gain", "rel_bias_table", "mix_out_gain",
          "ffn_norm", "conv_w", "conv_b")


def _pack(arrays):
    flat = jnp.concatenate([a.reshape(-1).astype(F32) for a in arrays])
    pad = (-flat.shape[0]) % (8 * LANES)
    return jnp.pad(flat, (0, pad)).reshape(-1, LANES)


def _unpack(flat, shapes):
    flat = flat.reshape(-1)
    out, pos = [], 0
    for sh in shapes:
        n = int(np.prod(sh))
        out.append(flat[pos:pos + n].reshape(sh))
        pos += n
    return out


def kernel(x, attn_norm, w_in, a_q_gain, a_k_gain, a_sinks, c_q_gain, c_k_gain, rel_bias_table, mix_out_gain, w_out, ffn_norm, w_up, conv_w, conv_b, w_down, loss_target, m_attn_norm, m_w_in, m_a_q_gain, m_a_k_gain, m_a_sinks, m_c_q_gain, m_c_k_gain, m_rel_bias_table, m_mix_out_gain, m_w_out, m_ffn_norm, m_w_up, m_conv_w, m_conv_b, m_w_down, v_attn_norm, v_w_in, v_a_q_gain, v_a_k_gain, v_a_sinks, v_c_q_gain, v_c_k_gain, v_rel_bias_table, v_mix_out_gain, v_w_out, v_ffn_norm, v_w_up, v_conv_w, v_conv_b, v_w_down):
    depth, d = attn_norm.shape
    f = w_down.shape[1] * N_CHIPS
    cfg = _Cfg(d, f)
    chip = 2 * lax.axis_index("x") + lax.axis_index("y")

    cw_cols = conv_w.shape[2]
    cw_flat = conv_w.reshape(-1)
    cw_rows = -(-cw_flat.shape[0] // (16 * LANES)) * 16
    cw_pad = jnp.pad(cw_flat, (0, cw_rows * LANES - cw_flat.shape[0])).reshape(cw_rows, LANES)

    table_a, table_c = rel_bias_table[:, :cfg.nha], rel_bias_table[:, cfg.nha:]
    bias_a = _block_bias(table_a, 1)
    bias_c = [_block_bias(table_c, dil) for _, dil in DILATED_PAIRS]

    layers, anchor = [], 0.0
    for l in range(depth):
        shards = [w_in[l].T.astype(BF16), w_out[l].astype(BF16), w_up[l].T.astype(BF16), w_down[l].astype(BF16)]
        names = ["w_in_t", "w_out", "w_up_t", "w_down"]
        if l == 0:
            todo = [([cw_pad, shards[0]], ["conv_w", names[0]])] + [([s], [n]) for s, n in zip(shards[1:], names[1:])]
        else:
            todo = [(shards, names)]
        groups = []
        for k, (srcs, group_names) in enumerate(todo):
            lands = [_own_slot((N_CHIPS,) + s.shape, s.dtype, s, chip) for s in srcs]
            handle, token = _split_start([], lands, _gather_plan, 3 * len(lands), f"gather_start_{l}_{k}")
            anchor = anchor + token[0, 0]
            groups.append((handle, group_names, f"gather_wait_{l}_{k}"))
        layers.append(dict(
            attn_norm=attn_norm[l].reshape(1, d), ffn_norm=ffn_norm[l].reshape(1, d),
            mix_gain=mix_out_gain[l].reshape(1, d),
            gains=(_tile_gain(a_q_gain[l], cfg.nha), _tile_gain(a_k_gain[l], cfg.nkva),
                   _tile_gain(c_q_gain[l], cfg.nhc), _tile_gain(c_k_gain[l], cfg.nhc)),
            sinks=a_sinks[l], bias_a=bias_a, bias_c=bias_c, conv_b=conv_b[l].reshape(1, 2 * f),
            weights=_Gathered(groups)))
    cw_all = layers[0]["weights"].get("conv_w", layers[0]["attn_norm"] + anchor)
    cw_all = cw_all.reshape(N_CHIPS, -1)[:, :cw_flat.shape[0]].reshape(N_CHIPS, depth, CONV_WIDTH, cw_cols)
    conv_w_full = jnp.transpose(cw_all, (1, 2, 0, 3)).reshape(depth, CONV_WIDTH, N_CHIPS * cw_cols)
    for l in range(depth):
        layers[l]["conv_w"] = conv_w_full[l]

    act = x[0]
    saved = []
    for l in range(depth):
        act, sv = _layer_fwd(act, layers[l], cfg)
        saved.append(sv)
    dact, dactb, loss_blk = _loss_head(act, loss_target[0], "loss_head")
    loss = lax.psum(loss_blk[0, 0], ("x", "y", "c"))

    core = lax.axis_index("c")

    def start_scatter(grads, name):
        srcs = [g.reshape(N_CHIPS, -1, g.shape[-1]) for g in grads]
        lands = []
        for g in srcs:
            half = g.shape[1] // 2
            own = lax.dynamic_slice(g, (chip, core * half, 0), (1, half, g.shape[2]))[0]
            lands.append(_own_slot((N_DEVICES, half, g.shape[2]), g.dtype, own, 2 * chip + core))
        return _split_start(srcs, lands, _scatter_plan, (N_DEVICES - 1) * len(srcs), name)

    def finish_scatter(l, handles, after):
        parts = [pt for k, h in enumerate(handles) for pt in _split_wait(h, after, f"scatter_wait_{l}_{k}")][::-1]
        halves = [_sum_parts(pt, f"sum_grads_{t}") for t, pt in enumerate(parts)]
        lands = [_own_slot((2,) + h.shape, h.dtype, h, core) for h in halves]
        return _split_start(halves, lands, _swap_plan, len(halves), f"swap_start_{l}")[0]

    dbias = (jnp.zeros_like(bias_a), [jnp.zeros_like(b) for b in bias_c])
    small_grads = [None] * depth
    swaps = [None] * depth
    pending = None
    for l in reversed(range(depth)):
        made = []

        def on_grad(g, l=l, made=made):
            if l:
                made.append(g)
                return 0.0
            handle, token = start_scatter([g], f"scatter_start_0_{len(made)}")
            made.append(handle)
            return token[0, 0]

        dact, dactb, small_grads[l], dbias = _layer_bwd(dact, dactb, saved[l], layers[l], dbias, cfg, on_grad)
        if pending is not None:
            swaps[l + 1] = finish_scatter(l + 1, pending, dact)
        if l:
            handle, token = start_scatter(made, f"scatter_start_{l}")
            pending = [handle]
            layers[l - 1]["conv_b"] = layers[l - 1]["conv_b"] + token[0, 0]
        else:
            pending = made
    grad_x = dact[None]

    tabs = _bias_table_grad([dbias[0]] + dbias[1], [jnp.asarray(_bucket_ids(1))]
                            + [jnp.asarray(_bucket_ids(dil)) for _, dil in DILATED_PAIRS], "bias_table_grad")
    g_table_a = tabs[0][:, :N_BUCKETS].T
    g_table_c = (tabs[1] + tabs[2] + tabs[3])[:, :N_BUCKETS].T
    g_table = jnp.concatenate([g_table_a, g_table_c], axis=1)
    small_local = {k: jnp.stack([small_grads[l][k] for l in range(depth)]) for k in _SMALL if k != "rel_bias_table"}
    small_local["rel_bias_table"] = g_table
    shapes = [small_local[k].shape for k in _SMALL]
    reduced = dict(zip(_SMALL, _unpack(_allreduce_small(_pack([small_local[k] for k in _SMALL]), "allreduce_small"),
                                       shapes)))
    reduced["conv_w"] = lax.dynamic_slice_in_dim(reduced["conv_w"], chip * cw_cols, cw_cols, axis=2)

    given = dict(attn_norm=attn_norm, a_q_gain=a_q_gain, a_k_gain=a_k_gain, a_sinks=a_sinks, c_q_gain=c_q_gain,
                 c_k_gain=c_k_gain, rel_bias_table=rel_bias_table, mix_out_gain=mix_out_gain, ffn_norm=ffn_norm,
                 conv_w=conv_w, conv_b=conv_b)
    moms = dict(attn_norm=(m_attn_norm, v_attn_norm), a_q_gain=(m_a_q_gain, v_a_q_gain),
                a_k_gain=(m_a_k_gain, v_a_k_gain), a_sinks=(m_a_sinks, v_a_sinks), c_q_gain=(m_c_q_gain, v_c_q_gain),
                c_k_gain=(m_c_k_gain, v_c_k_gain), rel_bias_table=(m_rel_bias_table, v_rel_bias_table),
                mix_out_gain=(m_mix_out_gain, v_mix_out_gain), ffn_norm=(m_ffn_norm, v_ffn_norm),
                conv_w=(m_conv_w, v_conv_w), conv_b=(m_conv_b, v_conv_b))
    sshapes = [given[k].shape for k in _SMALL]
    s_delta, s_m, s_v = _adamw(_pack([given[k] for k in _SMALL]), _pack([reduced[k] for k in _SMALL]),
                               _pack([moms[k][0] for k in _SMALL]), _pack([moms[k][1] for k in _SMALL]), "adamw_small")
    grads = dict(reduced)
    deltas = dict(zip(_SMALL, _unpack(s_delta, sshapes)))
    new_m = dict(zip(_SMALL, _unpack(s_m, sshapes)))
    new_v = dict(zip(_SMALL, _unpack(s_v, sshapes)))

    big_given = dict(w_in=(w_in, m_w_in, v_w_in, True), w_out=(w_out, m_w_out, v_w_out, False),
                     w_up=(w_up, m_w_up, v_w_up, True), w_down=(w_down, m_w_down, v_w_down, False))
    names = ("w_in", "w_out", "w_up", "w_down")
    bufs = {name: None for name in names}
    after = s_delta
    for l in reversed(range(depth)):
        if l == 0:
            swaps[0] = finish_scatter(0, pending, after)
        layer_grads = [g.reshape(-1, g.shape[-1]) for g in _split_wait(swaps[l], after, f"swap_wait_{l}")]
        for t, name in enumerate(names):
            wt, mt, vt, transposed = big_given[name]
            g = layer_grads[t].T if transposed else layer_grads[t]
            bufs[name] = _adamw_layer(l, wt, g, mt, vt, bufs[name], f"adamw_{name}_{l}")
            after = bufs[name][1]
    for name in names:
        grads[name], deltas[name], new_m[name], new_v[name] = bufs[name]

    order = ("attn_norm", "w_in", "a_q_gain", "a_k_gain", "a_sinks", "c_q_gain", "c_k_gain", "rel_bias_table",
             "mix_out_gain", "w_out", "ffn_norm", "w_up", "conv_w", "conv_b", "w_down")
    return (loss, grad_x, *[grads[k] for k in order], *[deltas[k] for k in order], *[new_m[k] for k in order],
            *[new_v[k] for k in order])
```

```python
import numpy as np
import jax
import jax.numpy as jnp
from jax import lax
from jax.experimental import pallas as pl
from jax.experimental.pallas import tpu as pltpu

F32 = jnp.float32
BF16 = jnp.bfloat16
MESH = pl.DeviceIdType.MESH

HEAD_DIM = 64
BLOCK = 128
LANES = 128
EPS = 1e-6
NEG_INF = -1e30
WINDOW_A = 128
DILATED_PAIRS = ((128, 1), (512, 4), (2048, 16))
N_BUCKETS = 32
T5_MAX_DIST = 2048
CONV_WIDTH = 3
ADAM_LR = 0.001
ADAM_B1 = 0.9
ADAM_B2 = 0.999
ADAM_EPS = 1e-08
ADAM_WD = 0.01
ADAM_STEP = 10
N_CHIPS = 4
N_DEVICES = 8
VMEM_LIMIT_BYTES = 48 * 1024 * 1024
QK_SCALE = HEAD_DIM ** -0.5


def _params(sem=None):
    return pltpu.CompilerParams(dimension_semantics=sem, vmem_limit_bytes=VMEM_LIMIT_BYTES)


def _div_tile(n, cap, mult):
    best = None
    for t in range(mult, min(n, cap) + 1, mult):
        if n % t == 0:
            best = t
    return n if best is None else best


def _dot(a, b):
    return lax.dot_general(a, b, (((1,), (0,)), ((), ())), preferred_element_type=F32)


def _dot_nt(a, b):
    return lax.dot_general(a, b, (((1,), (1,)), ((), ())), preferred_element_type=F32)


def _dot_tn(a, b):
    return lax.dot_general(a, b, (((0,), (0,)), ((), ())), preferred_element_type=F32)


def _split_dot(x, m):
    hi = x.astype(BF16)
    lo = (x - hi.astype(F32)).astype(BF16)
    return _dot(hi, m) + _dot(lo, m)


class _Cfg:
    def __init__(self, d_model, d_ff):
        nh = d_model // HEAD_DIM
        self.d = d_model
        self.f = d_ff
        self.nha = nh // 4
        self.nkva = self.nha // 4
        self.nhb = nh // 4
        self.nhc = nh // 2
        self.a_q = self.nha * HEAD_DIM
        self.a_kv = self.nkva * HEAD_DIM
        self.b_w = self.nhb * HEAD_DIM
        self.c_w = self.nhc * HEAD_DIM
        sizes = [self.a_q, self.a_kv, self.a_kv, self.b_w, self.b_w, self.b_w, self.c_w, self.c_w, self.c_w]
        starts = [0] + [int(s) for s in np.cumsum(sizes)[:-1]]
        self.sections = list(zip(starts, sizes))
        self.in_width = int(sum(sizes))
        assert all(s % LANES == 0 for s in sizes)


def _matmul(a, b, mode, out_dtype, name, tm=512, tn=512, tk=512, residual=None):
    if mode == "tn":
        kdim, m = a.shape
    else:
        m, kdim = a.shape
    n = b.shape[0] if mode == "nt" else b.shape[1]
    tm, tn, tk = _div_tile(m, tm, LANES), _div_tile(n, tn, LANES), _div_tile(kdim, tk, LANES)
    nk = kdim // tk
    if mode == "tn":
        a_spec = pl.BlockSpec((tk, tm), lambda i, j, k: (k, i))
    else:
        a_spec = pl.BlockSpec((tm, tk), lambda i, j, k: (i, k))
    if mode == "nt":
        b_spec = pl.BlockSpec((tn, tk), lambda i, j, k: (j, k))
    else:
        b_spec = pl.BlockSpec((tk, tn), lambda i, j, k: (k, j))
    dot = {"nn": _dot, "nt": _dot_nt, "tn": _dot_tn}[mode]
    o_spec = pl.BlockSpec((tm, tn), lambda i, j, k: (i, j))
    in_specs = [a_spec, b_spec]
    args = [a, b]
    if residual is not None:
        in_specs.append(o_spec)
        args.append(residual)

    def body(*refs):
        if residual is None:
            a_ref, b_ref, o_ref, acc = refs
        else:
            a_ref, b_ref, r_ref, o_ref, acc = refs
        k = pl.program_id(2)

        @pl.when(k == 0)
        def _():
            acc[...] = jnp.zeros_like(acc)

        acc[...] += dot(a_ref[...].astype(BF16), b_ref[...].astype(BF16))

        @pl.when(k == nk - 1)
        def _():
            r = acc[...]
            if residual is not None:
                r = r + r_ref[...]
            o_ref[...] = r.astype(out_dtype)

    return pl.pallas_call(
        body, name=name, grid=(m // tm, n // tn, nk), in_specs=in_specs, out_specs=o_spec,
        out_shape=jax.ShapeDtypeStruct((m, n), out_dtype), scratch_shapes=[pltpu.VMEM((tm, tn), F32)],
        compiler_params=_params(("parallel", "parallel", "arbitrary")),
    )(*args)


def _rmsnorm_fwd(x, g, name):
    s, d = x.shape
    ts = _div_tile(s, 512, 16)

    def body(x_ref, g_ref, o_ref):
        xv = x_ref[...]
        r = lax.rsqrt(jnp.mean(xv * xv, axis=-1, keepdims=True) + EPS)
        o_ref[...] = (xv * r * g_ref[...]).astype(BF16)

    return pl.pallas_call(
        body, name=name, grid=(s // ts,),
        in_specs=[pl.BlockSpec((ts, d), lambda i: (i, 0)), pl.BlockSpec((1, d), lambda i: (0, 0))],
        out_specs=pl.BlockSpec((ts, d), lambda i: (i, 0)), out_shape=jax.ShapeDtypeStruct((s, d), BF16),
        compiler_params=_params(("parallel",)),
    )(x, g)


def _rmsnorm_bwd(x, g, dh, dres, name):
    s, d = x.shape
    ts = _div_tile(s, 256, 16)

    def body(x_ref, g_ref, dh_ref, dres_ref, dx_ref, dxb_ref, dg_ref):
        @pl.when(pl.program_id(0) == 0)
        def _():
            dg_ref[...] = jnp.zeros_like(dg_ref)

        xv = x_ref[...]
        r = lax.rsqrt(jnp.mean(xv * xv, axis=-1, keepdims=True) + EPS)
        xhat = xv * r
        dhv = dh_ref[...]
        dxhat = dhv * g_ref[...]
        dx = dres_ref[...] + r * (dxhat - xhat * jnp.mean(dxhat * xhat, axis=-1, keepdims=True))
        dx_ref[...] = dx
        dxb_ref[...] = dx.astype(BF16)
        dg_ref[...] += jnp.sum(dhv * xhat, axis=0, keepdims=True)

    row = pl.BlockSpec((ts, d), lambda i: (i, 0))
    vec = pl.BlockSpec((1, d), lambda i: (0, 0))
    return pl.pallas_call(
        body, name=name, grid=(s // ts,), in_specs=[row, vec, row, row], out_specs=[row, row, vec],
        out_shape=[jax.ShapeDtypeStruct((s, d), F32), jax.ShapeDtypeStruct((s, d), BF16),
                   jax.ShapeDtypeStruct((1, d), F32)],
        compiler_params=_params(("arbitrary",)),
    )(x, g, dh, dres)


def _head_mean_matrix():
    idx = np.arange(LANES) // HEAD_DIM
    return jnp.asarray((idx[:, None] == idx[None, :]).astype(np.float32) / HEAD_DIM, dtype=BF16)


def _head_mean(y, m128):
    w = y.shape[1]
    parts = [_split_dot(y[:, c * LANES:(c + 1) * LANES], m128) for c in range(w // LANES)]
    return parts[0] if len(parts) == 1 else jnp.concatenate(parts, axis=1)


_NORMED_SECTIONS = (0, 1, 6, 7)
_QUERY_SECTIONS = (0, 3, 6)


def _qk_prep(proj, gains, cfg, name):
    s = proj.shape[0]
    ts = _div_tile(s, 256, 16)
    m128 = _head_mean_matrix()

    def body(p_ref, m_ref, g0, g1, g6, g7, *outs):
        gref = dict(zip(_NORMED_SECTIONS, (g0, g1, g6, g7)))
        for idx, (st, w) in enumerate(cfg.sections):
            xv = p_ref[:, st:st + w]
            if idx in gref:
                r = lax.rsqrt(_head_mean(xv * xv, m_ref[...]) + EPS)
                xv = xv * r * gref[idx][...]
            if idx in _QUERY_SECTIONS:
                xv = xv * QK_SCALE
            outs[idx][...] = xv.astype(BF16)

    in_specs = [pl.BlockSpec((ts, cfg.in_width), lambda i: (i, 0)), pl.BlockSpec((LANES, LANES), lambda i: (0, 0))]
    in_specs += [pl.BlockSpec((1, cfg.sections[k][1]), lambda i: (0, 0)) for k in _NORMED_SECTIONS]
    out_specs = [pl.BlockSpec((ts, w), lambda i: (i, 0)) for _, w in cfg.sections]
    out_shape = [jax.ShapeDtypeStruct((s, w), BF16) for _, w in cfg.sections]
    return pl.pallas_call(
        body, name=name, grid=(s // ts,), in_specs=in_specs, out_specs=out_specs, out_shape=out_shape,
        compiler_params=_params(("parallel",)),
    )(proj, m128, *gains)


def _qk_prep_bwd(proj, gains, grads, cfg, name):
    s = proj.shape[0]
    ts = _div_tile(s, 128, 16)
    m128 = _head_mean_matrix()
    counts = [len(gl) for gl in grads]
    flat = [g for gl in grads for g in gl]

    def body(*refs):
        p_ref, m_ref = refs[0], refs[1]
        gref = dict(zip(_NORMED_SECTIONS, refs[2:6]))
        g_in = refs[6:6 + len(flat)]
        dp_ref = refs[6 + len(flat)]
        dgain = dict(zip(_NORMED_SECTIONS, refs[7 + len(flat):]))

        @pl.when(pl.program_id(0) == 0)
        def _():
            for k in _NORMED_SECTIONS:
                dgain[k][...] = jnp.zeros_like(dgain[k])

        pos = 0
        for idx, (st, w) in enumerate(cfg.sections):
            dy = g_in[pos][...].astype(F32)
            for extra in g_in[pos + 1:pos + counts[idx]]:
                dy = dy + extra[...].astype(F32)
            pos += counts[idx]
            if idx in gref:
                xv = p_ref[:, st:st + w]
                r = lax.rsqrt(_head_mean(xv * xv, m_ref[...]) + EPS)
                xhat = xv * r
                dxhat = dy * gref[idx][...]
                dgain[idx][...] += jnp.sum(dy * xhat, axis=0, keepdims=True)
                dy = r * (dxhat - xhat * _head_mean(dxhat * xhat, m_ref[...]))
            dp_ref[:, st:st + w] = dy.astype(BF16)

    in_specs = [pl.BlockSpec((ts, cfg.in_width), lambda i: (i, 0)), pl.BlockSpec((LANES, LANES), lambda i: (0, 0))]
    in_specs += [pl.BlockSpec((1, cfg.sections[k][1]), lambda i: (0, 0)) for k in _NORMED_SECTIONS]
    for idx, (_, w) in enumerate(cfg.sections):
        in_specs += [pl.BlockSpec((ts, w), lambda i: (i, 0))] * counts[idx]
    out_specs = [pl.BlockSpec((ts, cfg.in_width), lambda i: (i, 0))]
    out_specs += [pl.BlockSpec((1, cfg.sections[k][1]), lambda i: (0, 0)) for k in _NORMED_SECTIONS]
    out_shape = [jax.ShapeDtypeStruct((s, cfg.in_width), BF16)]
    out_shape += [jax.ShapeDtypeStruct((1, cfg.sections[k][1]), F32) for k in _NORMED_SECTIONS]
    return pl.pallas_call(
        body, name=name, grid=(s // ts,), in_specs=in_specs, out_specs=out_specs, out_shape=out_shape,
        compiler_params=_params(("arbitrary",)),
    )(proj, m128, *gains, *flat)


def _band_masks(max_dist):
    row = lax.broadcasted_iota(jnp.int32, (BLOCK, BLOCK), 0)
    col = lax.broadcasted_iota(jnp.int32, (BLOCK, BLOCK), 1)
    return row + BLOCK - col <= max_dist, col <= row


def _dilated_t(a, dil):
    s, w = a.shape
    return _keys_on_lanes(a.reshape(s // dil, dil * w), BLOCK)


def _undilated(at, dil):
    nblk, dw, _ = at.shape
    return jnp.transpose(at, (0, 2, 1)).reshape(nblk * BLOCK * dil, dw // dil)


def _banded_fwd(q, kt, v, bias, sinks, hq, hk, max_dist, dil, name):
    s = q.shape[0]
    wq, wk, sd, grp = hq * HEAD_DIM, hk * HEAD_DIM, s // dil, hq // hk
    nb = sd // BLOCK
    has_sink = sinks is not None

    def body(*refs):
        if has_sink:
            q_ref, ktp_ref, ktc_ref, vp_ref, vc_ref, b_ref, s_ref, o_ref, l_ref = refs
        else:
            q_ref, ktp_ref, ktc_ref, vp_ref, vc_ref, b_ref, o_ref, l_ref = refs
        i = pl.program_id(1)
        mprev, mcur = _band_masks(max_dist)
        mask = jnp.concatenate([jnp.logical_and(mprev, i > 0), mcur], axis=1)
        for h in range(hq):
            sq = slice(h * HEAD_DIM, (h + 1) * HEAD_DIM)
            sk = slice((h // grp) * HEAD_DIM, (h // grp + 1) * HEAD_DIM)
            kt = jnp.concatenate([ktp_ref[sk, :], ktc_ref[sk, :]], axis=1)
            vv = jnp.concatenate([vp_ref[:, sk], vc_ref[:, sk]], axis=0)
            sc = jnp.where(mask, _dot(q_ref[:, sq], kt) + b_ref[h], NEG_INF)
            m = jnp.max(sc, axis=-1, keepdims=True)
            if has_sink:
                m = jnp.maximum(m, s_ref[h])
            p = jnp.exp(sc - m)
            den = jnp.sum(p, axis=-1, keepdims=True)
            if has_sink:
                den = den + jnp.exp(s_ref[h] - m)
            o_ref[:, sq] = _dot(p.astype(BF16), vv) / den
            l_ref[:, h:h + 1] = m + jnp.log(den)

    qspec = pl.BlockSpec((BLOCK, wq), lambda r, i: (i, r))
    kprev = pl.BlockSpec((BLOCK, wk), lambda r, i: (jnp.maximum(i - 1, 0), r))
    kcur = pl.BlockSpec((BLOCK, wk), lambda r, i: (i, r))
    ktprev = pl.BlockSpec((None, wk, BLOCK), lambda r, i: (jnp.maximum(i - 1, 0), r, 0))
    ktcur = pl.BlockSpec((None, wk, BLOCK), lambda r, i: (i, r, 0))
    in_specs = [qspec, ktprev, ktcur, kprev, kcur, pl.BlockSpec((hq, BLOCK, 2 * BLOCK), lambda r, i: (0, 0, 0))]
    v2 = v.reshape(sd, dil * wk)
    args = [q.reshape(sd, dil * wq), kt, kt, v2, v2, bias]
    if has_sink:
        in_specs.append(pl.BlockSpec(memory_space=pltpu.SMEM))
        args.append(sinks)
    out, lse = pl.pallas_call(
        body, name=name, grid=(dil, nb), in_specs=in_specs,
        out_specs=[qspec, pl.BlockSpec((None, BLOCK, hq), lambda r, i: (r, i, 0))],
        out_shape=[jax.ShapeDtypeStruct((sd, dil * wq), F32), jax.ShapeDtypeStruct((dil, sd, hq), F32)],
        compiler_params=_params(("parallel", "parallel")),
    )(*args)
    return out.reshape(s, wq), jnp.transpose(lse, (1, 0, 2)).reshape(s, hq)


def _per_head_dilated(a, dil):
    s, h = a.shape
    return jnp.transpose(a.reshape(s // dil, dil, h), (1, 0, 2))


def _banded_bwd(q, k, kt, v, lse, dsum, do, bias, sinks, dbias_init, hq, hk, max_dist, dil, name):
    s = q.shape[0]
    wq, wk, sd, grp = hq * HEAD_DIM, hk * HEAD_DIM, s // dil, hq // hk
    nb = sd // BLOCK
    has_sink = sinks is not None

    def body(*refs):
        (q_ref, qn_ref, qt_ref, qtn_ref, kp_ref, kc_ref, ktp_ref, ktc_ref, vtp_ref, vtc_ref, l_ref, ln_ref, d_ref,
         dn_ref, do_ref, don_ref, dot_ref, dotn_ref, b_ref, dbi_ref) = refs[:20]
        rest = refs[20:]
        if has_sink:
            s_ref, dq_ref, dkt_ref, dvt_ref, db_ref, ds_ref = rest
        else:
            dq_ref, dkt_ref, dvt_ref, db_ref = rest
        j = pl.program_id(1)

        @pl.when(jnp.logical_and(pl.program_id(0) == 0, j == 0))
        def _():
            db_ref[...] = dbi_ref[...]
            if has_sink:
                ds_ref[...] = jnp.zeros_like(ds_ref)

        mprev_static, mcur = _band_masks(max_dist)
        mask = jnp.concatenate([jnp.logical_and(mprev_static, j > 0), mcur], axis=1)
        mnext = jnp.logical_and(mprev_static, j + 1 < nb)
        dkt_acc = [jnp.zeros((HEAD_DIM, BLOCK), F32) for _ in range(hk)]
        dvt_acc = [jnp.zeros((HEAD_DIM, BLOCK), F32) for _ in range(hk)]
        for h in range(hq):
            g = h // grp
            sq = slice(h * HEAD_DIM, (h + 1) * HEAD_DIM)
            sk = slice(g * HEAD_DIM, (g + 1) * HEAD_DIM)
            kt2 = jnp.concatenate([ktp_ref[sk, :], ktc_ref[sk, :]], axis=1)
            vt2 = jnp.concatenate([vtp_ref[sk, :], vtc_ref[sk, :]], axis=1)
            k2 = jnp.concatenate([kp_ref[:, sk], kc_ref[:, sk]], axis=0)
            lcol = l_ref[:, h:h + 1]
            dcol = d_ref[:, h:h + 1]
            sc = _dot(q_ref[:, sq], kt2) + b_ref[h]
            p = jnp.where(mask, jnp.exp(sc - lcol), 0.0)
            ds = p * (_dot(do_ref[:, sq], vt2) - dcol)
            dsb = ds.astype(BF16)
            dq_ref[:, sq] = (_dot(dsb, k2) * QK_SCALE).astype(BF16)
            db_ref[h] += ds
            if has_sink:
                psink = jnp.exp(s_ref[h] - lcol)
                tot = jnp.sum(psink * dcol, axis=0, keepdims=True)
                ds_ref[h:h + 1, :] -= jnp.broadcast_to(tot, (1, LANES))
            lncol = ln_ref[:, h:h + 1]
            dncol = dn_ref[:, h:h + 1]
            sn = _dot(qn_ref[:, sq], ktc_ref[sk, :]) + b_ref[h, :, 0:BLOCK]
            pn = jnp.where(mnext, jnp.exp(sn - lncol), 0.0)
            dsn = pn * (_dot(don_ref[:, sq], vtc_ref[sk, :]) - dncol)
            dkt_acc[g] = dkt_acc[g] + (_dot(qt_ref[sq, :], dsb[:, BLOCK:]) + _dot(qtn_ref[sq, :], dsn.astype(BF16)))
            dvt_acc[g] = dvt_acc[g] + (_dot(dot_ref[sq, :], p[:, BLOCK:].astype(BF16))
                                       + _dot(dotn_ref[sq, :], pn.astype(BF16)))
        for g in range(hk):
            sk = slice(g * HEAD_DIM, (g + 1) * HEAD_DIM)
            dkt_ref[sk, :] = dkt_acc[g].astype(BF16)
            dvt_ref[sk, :] = dvt_acc[g].astype(BF16)

    qcur = pl.BlockSpec((BLOCK, wq), lambda r, j: (j, r))
    qnext = pl.BlockSpec((BLOCK, wq), lambda r, j: (jnp.minimum(j + 1, nb - 1), r))
    qtcur = pl.BlockSpec((None, wq, BLOCK), lambda r, j: (j, r, 0))
    qtnext = pl.BlockSpec((None, wq, BLOCK), lambda r, j: (jnp.minimum(j + 1, nb - 1), r, 0))
    kprev = pl.BlockSpec((BLOCK, wk), lambda r, j: (jnp.maximum(j - 1, 0), r))
    kcur = pl.BlockSpec((BLOCK, wk), lambda r, j: (j, r))
    ktprev = pl.BlockSpec((None, wk, BLOCK), lambda r, j: (jnp.maximum(j - 1, 0), r, 0))
    ktcur = pl.BlockSpec((None, wk, BLOCK), lambda r, j: (j, r, 0))
    bspec = pl.BlockSpec((hq, BLOCK, 2 * BLOCK), lambda r, j: (0, 0, 0))
    hcur = pl.BlockSpec((None, BLOCK, hq), lambda r, j: (r, j, 0))
    hnext = pl.BlockSpec((None, BLOCK, hq), lambda r, j: (r, jnp.minimum(j + 1, nb - 1), 0))
    dob = do.astype(BF16)
    q2, k2, do2 = q.reshape(sd, dil * wq), k.reshape(sd, dil * wk), dob.reshape(sd, dil * wq)
    l3, d3 = _per_head_dilated(lse, dil), _per_head_dilated(dsum, dil)
    qt, vt, dot = _dilated_t(q, dil), _dilated_t(v, dil), _dilated_t(dob, dil)
    in_specs = [qcur, qnext, qtcur, qtnext, kprev, kcur, ktprev, ktcur, ktprev, ktcur, hcur, hnext, hcur, hnext,
                qcur, qnext, qtcur, qtnext, bspec, bspec]
    args = [q2, q2, qt, qt, k2, k2, kt, kt, vt, vt, l3, l3, d3, d3, do2, do2, dot, dot, bias, dbias_init]
    out_specs = [qcur, ktcur, ktcur, bspec]
    out_shape = [jax.ShapeDtypeStruct((sd, dil * wq), BF16), jax.ShapeDtypeStruct((nb, dil * wk, BLOCK), BF16),
                 jax.ShapeDtypeStruct((nb, dil * wk, BLOCK), BF16), jax.ShapeDtypeStruct((hq, BLOCK, 2 * BLOCK), F32)]
    if has_sink:
        in_specs.append(pl.BlockSpec(memory_space=pltpu.SMEM))
        args.append(sinks)
        out_specs.append(pl.BlockSpec((hq, LANES), lambda r, j: (0, 0)))
        out_shape.append(jax.ShapeDtypeStruct((hq, LANES), F32))
    res = pl.pallas_call(
        body, name=name, grid=(dil, nb), in_specs=in_specs, out_specs=out_specs, out_shape=out_shape,
        compiler_params=_params(("arbitrary", "arbitrary")),
    )(*args)
    dq, dk, dv, dbias = res[0].reshape(s, wq), _undilated(res[1], dil), _undilated(res[2], dil), res[3]
    return dq, dk, dv, dbias, (res[4][:, 0] if has_sink else None)


def _neg_softplus(z):
    return -(jnp.maximum(z, 0.0) + jnp.log(1.0 + jnp.exp(-jnp.abs(z))))


SB_CHUNK = 256
HEADS_PER_PAIR = LANES // HEAD_DIM


def _tri(kind):
    row = lax.broadcasted_iota(jnp.int32, (SB_CHUNK, SB_CHUNK), 0)
    col = lax.broadcasted_iota(jnp.int32, (SB_CHUNK, SB_CHUNK), 1)
    return {"ge": row >= col, "lt": row < col, "le": row <= col}[kind].astype(BF16)


def _keys_on_lanes(a, rows):
    s, w = a.shape
    return jnp.transpose(a.reshape(s // rows, rows, w), (0, 2, 1))


def _sb_mask(i, jj):
    row = lax.broadcasted_iota(jnp.int32, (BLOCK, SB_CHUNK), 0)
    col = lax.broadcasted_iota(jnp.int32, (BLOCK, SB_CHUNK), 1)
    return col < row + (i * BLOCK - jj * SB_CHUNK)


def _sb_trips(i):
    return (i * BLOCK) // (2 * SB_CHUNK) + 1


def _sb_rows(jj, n):
    return pl.ds(pl.multiple_of(jj * SB_CHUNK, SB_CHUNK), n * SB_CHUNK)


def _sb_fwd(q, kt, v, name):
    s, w = q.shape
    npair, nb, nc = w // LANES, s // BLOCK, s // SB_CHUNK

    def body(q_ref, kt_ref, v_ref, o_ref, t_ref):
        i = pl.program_id(1)
        lincl = _tri("ge")
        heads = [slice(hh * HEAD_DIM, (hh + 1) * HEAD_DIM) for hh in range(HEADS_PER_PAIR)]
        qs = [q_ref[:, sl] for sl in heads]

        def trip(t, carry, masked):
            lo, hi = 2 * t, 2 * t + 1
            mlo, mhi = (_sb_mask(i, lo), _sb_mask(i, hi)) if masked else (None, None)

            def keep(m, val):
                return val if m is None else jnp.where(m, val, 0.0)

            new = []
            for hh, sl in enumerate(heads):
                o_acc, rem = carry[hh]
                zhi = _dot(qs[hh], kt_ref[hi, sl, :])
                zlo = _dot(qs[hh], kt_ref[lo, sl, :])
                lrhi = keep(mhi, _neg_softplus(zhi))
                lrlo = keep(mlo, _neg_softplus(zlo))
                tothi = jnp.sum(lrhi, axis=-1, keepdims=True)
                ahi = keep(mhi, jnp.exp(zhi + (rem + _split_dot(lrhi, lincl))))
                alo = keep(mlo, jnp.exp(zlo + (rem + tothi + _split_dot(lrlo, lincl))))
                a = jnp.concatenate([alo, ahi], axis=1).astype(BF16)
                new.append((o_acc + _dot(a, v_ref[_sb_rows(lo, 2), sl]),
                            rem + tothi + jnp.sum(lrlo, axis=-1, keepdims=True)))
            return tuple(new)

        init = tuple((jnp.zeros((BLOCK, HEAD_DIM), F32), jnp.zeros((BLOCK, 1), F32)) for _ in heads)
        trips = _sb_trips(i)
        carry = trip(trips - 1, init, True)
        carry = lax.fori_loop(0, trips - 1, lambda t, cr: trip(trips - 2 - t, cr, False), carry)
        for hh, sl in enumerate(heads):
            o_ref[:, sl] = carry[hh][0]
            t_ref[:, sl] = jnp.broadcast_to(carry[hh][1], (BLOCK, HEAD_DIM))

    qspec = pl.BlockSpec((BLOCK, LANES), lambda p, i: (i, p))
    return pl.pallas_call(
        body, name=name, grid=(npair, nb),
        in_specs=[qspec, pl.BlockSpec((nc, LANES, SB_CHUNK), lambda p, i: (0, p, 0)),
                  pl.BlockSpec((s, LANES), lambda p, i: (0, p))],
        out_specs=[qspec, qspec], out_shape=[jax.ShapeDtypeStruct((s, w), F32)] * 2,
        compiler_params=_params(("parallel", "parallel")),
    )(q, kt, v)


def _sb_bwd(q, k, kt, v, tot, do, name):
    s, w = q.shape
    npair, nb, nc = w // LANES, s // BLOCK, s // SB_CHUNK
    dob = do.astype(BF16)

    def body(q_ref, qt_ref, k_ref, kt_ref, vt_ref, t_ref, do_ref, dot_ref, dq_ref, dkt_ref, dvt_ref):
        i = pl.program_id(1)

        @pl.when(i == 0)
        def _():
            dkt_ref[...] = jnp.zeros_like(dkt_ref)
            dvt_ref[...] = jnp.zeros_like(dvt_ref)

        lbefore = _tri("lt")
        lupto = _tri("le")
        heads = [slice(hh * HEAD_DIM, (hh + 1) * HEAD_DIM) for hh in range(HEADS_PER_PAIR)]
        qs = [q_ref[:, sl] for sl in heads]
        qts = [qt_ref[sl, :] for sl in heads]
        dos = [do_ref[:, sl] for sl in heads]
        dots = [dot_ref[sl, :] for sl in heads]
        totals = [t_ref[:, sl.start:sl.start + 1] for sl in heads]

        def trip(t, carry, masked):
            chunks = (2 * t, 2 * t + 1)
            masks = [_sb_mask(i, jj) if masked else None for jj in chunks]

            def keep(m, val):
                return val if m is None else jnp.where(m, val, 0.0)

            new = []
            for hh, sl in enumerate(heads):
                dq_acc, plr, pg = carry[hh]
                zs = [_dot(qs[hh], kt_ref[jj, sl, :]) for jj in chunks]
                lrs = [keep(m, _neg_softplus(z)) for m, z in zip(masks, zs)]
                lr_sums = [jnp.sum(lr, axis=-1, keepdims=True) for lr in lrs]
                before = [plr, plr + lr_sums[0]]
                avs = [keep(m, jnp.exp(z + (totals[hh] - (b + _split_dot(lr, lbefore)))))
                       for m, z, lr, b in zip(masks, zs, lrs, before)]
                gs = [_dot(dos[hh], vt_ref[jj, sl, :]) * a for jj, a in zip(chunks, avs)]
                g_sums = [jnp.sum(g, axis=-1, keepdims=True) for g in gs]
                upto = [pg, pg + g_sums[0]]
                dzs = [keep(m, g - jnp.exp(z + lr) * (u + _split_dot(g, lupto))).astype(BF16)
                       for m, z, lr, g, u in zip(masks, zs, lrs, gs, upto)]
                for jj, dzb, a in zip(chunks, dzs, avs):
                    dkt_ref[jj, sl, :] += _dot(qts[hh], dzb)
                    dvt_ref[jj, sl, :] += _dot(dots[hh], a.astype(BF16))
                dz2 = jnp.concatenate(dzs, axis=1)
                new.append((dq_acc + _dot(dz2, k_ref[_sb_rows(chunks[0], 2), sl]), plr + lr_sums[0] + lr_sums[1],
                            pg + g_sums[0] + g_sums[1]))
            return tuple(new)

        zero = jnp.zeros((BLOCK, 1), F32)
        init = tuple((jnp.zeros((BLOCK, HEAD_DIM), F32), zero, zero) for _ in heads)
        trips = _sb_trips(i)
        carry = lax.fori_loop(0, trips - 1, lambda t, cr: trip(t, cr, False), init)
        carry = trip(trips - 1, carry, True)
        for hh, sl in enumerate(heads):
            dq_ref[:, sl] = (carry[hh][0] * QK_SCALE).astype(BF16)

    qspec = pl.BlockSpec((BLOCK, LANES), lambda p, i: (i, p))
    qtspec = pl.BlockSpec((None, LANES, BLOCK), lambda p, i: (i, p, 0))
    kspec = pl.BlockSpec((s, LANES), lambda p, i: (0, p))
    ktspec = pl.BlockSpec((nc, LANES, SB_CHUNK), lambda p, i: (0, p, 0))
    dq, dkt, dvt = pl.pallas_call(
        body, name=name, grid=(npair, nb), in_specs=[qspec, qtspec, kspec, ktspec, ktspec, qspec, qspec, qtspec],
        out_specs=[qspec, ktspec, ktspec],
        out_shape=[jax.ShapeDtypeStruct((s, w), BF16)] + [jax.ShapeDtypeStruct((nc, w, SB_CHUNK), F32)] * 2,
        compiler_params=_params(("parallel", "arbitrary")),
    )(q, _keys_on_lanes(q, BLOCK), k, kt, _keys_on_lanes(v, SB_CHUNK), tot, dob, _keys_on_lanes(dob, BLOCK))

    def rows_first(t):
        return jnp.transpose(t, (0, 2, 1)).reshape(s, w)

    return dq, rows_first(dkt), rows_first(dvt)


def _group_norm(xv, g):
    r = lax.rsqrt(jnp.mean(xv * xv, axis=-1, keepdims=True) + EPS)
    return xv * r * g


def _head_spread(nheads):
    return jnp.asarray(np.repeat(np.eye(nheads, dtype=np.float32), HEAD_DIM, axis=1), dtype=BF16)


def _mix_fwd(oa, ob, ocs, lses, gain, cfg, name):
    s = oa.shape[0]
    ts = _div_tile(s, 512, 16)
    aq, bw, cw, nhc = cfg.a_q, cfg.b_w, cfg.c_w, cfg.nhc

    def body(oa_ref, ob_ref, c1, c2, c3, l1, l2, l3, sp_ref, g_ref, mix_ref, oc_ref, lse_ref):
        m = jnp.maximum(jnp.maximum(l1[...], l2[...]), l3[...])
        es = [jnp.exp(l[...] - m) for l in (l1, l2, l3)]
        den = es[0] + es[1] + es[2]
        oc = sum(_split_dot(e / den, sp_ref[...]) * c[...] for e, c in zip(es, (c1, c2, c3)))
        oc_ref[...] = oc
        lse_ref[...] = m + jnp.log(den)
        mix_ref[:, 0:aq] = _group_norm(oa_ref[...], g_ref[:, 0:aq]).astype(BF16)
        mix_ref[:, aq:aq + bw] = _group_norm(ob_ref[...], g_ref[:, aq:aq + bw]).astype(BF16)
        mix_ref[:, aq + bw:] = _group_norm(oc, g_ref[:, aq + bw:]).astype(BF16)

    def row(wd):
        return pl.BlockSpec((ts, wd), lambda i: (i, 0))

    return pl.pallas_call(
        body, name=name, grid=(s // ts,),
        in_specs=[row(aq), row(bw)] + [row(cw)] * 3 + [row(nhc)] * 3
        + [pl.BlockSpec((nhc, cw), lambda i: (0, 0)), pl.BlockSpec((1, cfg.d), lambda i: (0, 0))],
        out_specs=[row(cfg.d), row(cw), row(nhc)],
        out_shape=[jax.ShapeDtypeStruct((s, cfg.d), BF16), jax.ShapeDtypeStruct((s, cw), F32),
                   jax.ShapeDtypeStruct((s, nhc), F32)],
        compiler_params=_params(("parallel",)),
    )(oa, ob, *ocs, *lses, _head_spread(nhc), gain)


def _mix_bwd(dmix, oa, ob, oc, gain, cfg, name):
    s = oa.shape[0]
    ts = _div_tile(s, 256, 8)
    aq, bw, cw = cfg.a_q, cfg.b_w, cfg.c_w

    def body(dm_ref, oa_ref, ob_ref, oc_ref, g_ref, fa_ref, fc_ref, da_ref, db_ref, dc_ref, dg_ref, sa_ref, sc_ref):
        @pl.when(pl.program_id(0) == 0)
        def _():
            dg_ref[...] = jnp.zeros_like(dg_ref)

        for x_ref, dx_ref, lo, hi, fold in ((oa_ref, da_ref, 0, aq, (fa_ref, sa_ref)), (ob_ref, db_ref, aq, aq + bw, None),
                                            (oc_ref, dc_ref, aq + bw, aq + bw + cw, (fc_ref, sc_ref))):
            xv = x_ref[...]
            dy = dm_ref[:, lo:hi]
            r = lax.rsqrt(jnp.mean(xv * xv, axis=-1, keepdims=True) + EPS)
            xhat = xv * r
            dxhat = dy * g_ref[:, lo:hi]
            dx = r * (dxhat - xhat * jnp.mean(dxhat * xhat, axis=-1, keepdims=True))
            dx_ref[...] = dx
            dg_ref[:, lo:hi] += jnp.sum(dy * xhat, axis=0, keepdims=True)
            if fold is not None:
                fold[1][...] = _split_dot(dx * xv, fold[0][...])

    def row(wd):
        return pl.BlockSpec((ts, wd), lambda i: (i, 0))

    vec = pl.BlockSpec((1, cfg.d), lambda i: (0, 0))
    return pl.pallas_call(
        body, name=name, grid=(s // ts,),
        in_specs=[row(cfg.d), row(aq), row(bw), row(cw), vec, pl.BlockSpec((aq, cfg.nha), lambda i: (0, 0)),
                  pl.BlockSpec((cw, cfg.nhc), lambda i: (0, 0))],
        out_specs=[row(aq), row(bw), row(cw), vec, row(cfg.nha), row(cfg.nhc)],
        out_shape=[jax.ShapeDtypeStruct((s, aq), F32), jax.ShapeDtypeStruct((s, bw), F32),
                   jax.ShapeDtypeStruct((s, cw), F32), jax.ShapeDtypeStruct((1, cfg.d), F32),
                   jax.ShapeDtypeStruct((s, cfg.nha), F32), jax.ShapeDtypeStruct((s, cfg.nhc), F32)],
        compiler_params=_params(("arbitrary",)),
    )(dmix, oa, ob, oc, gain, _head_spread(cfg.nha).T, _head_spread(cfg.nhc).T)


def _bias_table_grad(dbiases, buckets, name):
    outs = []
    for idx, (db, bk) in enumerate(zip(dbiases, buckets)):
        h = db.shape[0]

        def body(db_ref, bk_ref, o_ref):
            xv = db_ref[0]
            ids = bk_ref[...]
            lane = lax.broadcasted_iota(jnp.int32, (1, LANES), 1)
            acc = jnp.zeros((1, LANES), F32)
            for b in range(N_BUCKETS):
                tot = jnp.sum(jnp.where(ids == b, xv, 0.0), axis=0, keepdims=True)
                tot = jnp.sum(tot, axis=1, keepdims=True)
                acc = jnp.where(lane == b, tot, acc)
            o_ref[0] = acc

        outs.append(pl.pallas_call(
            body, name=f"{name}_{idx}", grid=(h,),
            in_specs=[pl.BlockSpec((1, BLOCK, 2 * BLOCK), lambda i: (i, 0, 0)),
                      pl.BlockSpec((BLOCK, 2 * BLOCK), lambda i: (0, 0))],
            out_specs=pl.BlockSpec((1, 1, LANES), lambda i: (i, 0, 0)),
            out_shape=jax.ShapeDtypeStruct((h, 1, LANES), F32), compiler_params=_params(("parallel",)),
        )(db, bk)[:, 0, :])
    return outs


SUBLANES = 8


def _shift_down(u, n, rows):
    r = pltpu.roll(u, n, 0)
    return jnp.concatenate([jnp.where(rows[:SUBLANES] >= n, r[:SUBLANES], 0.0), r[SUBLANES:]], axis=0)


def _shift_up(u, n, rows, s):
    r = pltpu.roll(u, s - n, 0)
    return jnp.concatenate([r[:s - SUBLANES], jnp.where(rows[s - SUBLANES:] < s - n, r[s - SUBLANES:], 0.0)], axis=0)


def _conv(u, w_ref, b_ref, rows):
    return (b_ref[...] + w_ref[0:1, :] * _shift_down(u, 2, rows) + w_ref[1:2, :] * _shift_down(u, 1, rows)
            + w_ref[2:3, :] * u)


def _conv_act_fwd(u, conv_w, conv_b, f, name):
    s = u.shape[0]
    nf = f // LANES

    def body(ug_ref, uu_ref, wg_ref, wu_ref, bg_ref, bu_ref, act_ref):
        rows = lax.broadcasted_iota(jnp.int32, (s, LANES), 0)
        gate = _conv(ug_ref[...], wg_ref, bg_ref, rows)
        up = _conv(uu_ref[...], wu_ref, bu_ref, rows)
        act_ref[...] = (gate * jax.nn.sigmoid(gate) * up).astype(BF16)

    def col(rws, off):
        return pl.BlockSpec((rws, LANES), lambda j: (0, j + off))

    return pl.pallas_call(
        body, name=name, grid=(nf,),
        in_specs=[col(s, 0), col(s, nf), col(CONV_WIDTH, 0), col(CONV_WIDTH, nf), col(1, 0), col(1, nf)],
        out_specs=col(s, 0), out_shape=jax.ShapeDtypeStruct((s, f), BF16), compiler_params=_params(("parallel",)),
    )(u, u, conv_w, conv_w, conv_b, conv_b)


def _conv_act_bwd(u, dact, conv_w, conv_b, f, name):
    s = u.shape[0]
    nf = f // LANES

    def body(ug_ref, uu_ref, da_ref, wg_ref, wu_ref, bg_ref, bu_ref, dug_ref, duu_ref, dwg_ref, dwu_ref, dbg_ref,
             dbu_ref):
        rows = lax.broadcasted_iota(jnp.int32, (s, LANES), 0)
        ug, uu = ug_ref[...], uu_ref[...]
        gate = _conv(ug, wg_ref, bg_ref, rows)
        up = _conv(uu, wu_ref, bu_ref, rows)
        sg = jax.nn.sigmoid(gate)
        da = da_ref[...]
        dgate = da * up * (sg * (1.0 + gate * (1.0 - sg)))
        dup = da * (gate * sg)
        for du, uv, w_ref, du_ref, dw_ref, db_ref in ((dgate, ug, wg_ref, dug_ref, dwg_ref, dbg_ref),
                                                     (dup, uu, wu_ref, duu_ref, dwu_ref, dbu_ref)):
            du_ref[...] = (w_ref[2:3, :] * du + w_ref[1:2, :] * _shift_up(du, 1, rows, s)
                           + w_ref[0:1, :] * _shift_up(du, 2, rows, s)).astype(BF16)
            dw_ref[0:1, :] = jnp.sum(du * _shift_down(uv, 2, rows), axis=0, keepdims=True)
            dw_ref[1:2, :] = jnp.sum(du * _shift_down(uv, 1, rows), axis=0, keepdims=True)
            dw_ref[2:3, :] = jnp.sum(du * uv, axis=0, keepdims=True)
            db_ref[...] = jnp.sum(du, axis=0, keepdims=True)

    def col(rws, off):
        return pl.BlockSpec((rws, LANES), lambda j: (0, j + off))

    return pl.pallas_call(
        body, name=name, grid=(nf,),
        in_specs=[col(s, 0), col(s, nf), col(s, 0), col(CONV_WIDTH, 0), col(CONV_WIDTH, nf), col(1, 0), col(1, nf)],
        out_specs=[col(s, 0), col(s, 0), col(CONV_WIDTH, 0), col(CONV_WIDTH, 0), col(1, 0), col(1, 0)],
        out_shape=[jax.ShapeDtypeStruct((s, f), BF16)] * 2 + [jax.ShapeDtypeStruct((CONV_WIDTH, f), F32)] * 2
        + [jax.ShapeDtypeStruct((1, f), F32)] * 2,
        compiler_params=_params(("parallel",)),
    )(u, u, dact, conv_w, conv_w, conv_b, conv_b)


def _loss_head(y, target, name):
    s, d = y.shape
    ts = _div_tile(s, 256, 16)

    def body(y_ref, t_ref, dy_ref, dyb_ref, l_ref):
        @pl.when(pl.program_id(0) == 0)
        def _():
            l_ref[...] = jnp.zeros_like(l_ref)

        err = y_ref[...] - t_ref[...]
        dy = err * (1.0 / d)
        dy_ref[...] = dy
        dyb_ref[...] = dy.astype(BF16)
        tot = jnp.sum(jnp.sum(err * err, axis=0, keepdims=True), axis=1, keepdims=True) * (0.5 / d)
        l_ref[...] += jnp.broadcast_to(tot, l_ref.shape)

    row = pl.BlockSpec((ts, d), lambda i: (i, 0))
    return pl.pallas_call(
        body, name=name, grid=(s // ts,), in_specs=[row, row],
        out_specs=[row, row, pl.BlockSpec((8, LANES), lambda i: (0, 0))],
        out_shape=[jax.ShapeDtypeStruct((s, d), F32), jax.ShapeDtypeStruct((s, d), BF16),
                   jax.ShapeDtypeStruct((8, LANES), F32)],
        compiler_params=_params(("arbitrary",)),
    )(y, target)


def _adamw(w, g, m, v, name):
    r, c = w.shape
    tr = _div_tile(r, max(8, (1 << 18) // c // 8 * 8), 8)
    c1 = 1.0 - ADAM_B1 ** ADAM_STEP
    c2 = 1.0 - ADAM_B2 ** ADAM_STEP

    def body(w_ref, g_ref, m_ref, v_ref, d_ref, nm_ref, nv_ref):
        gv = g_ref[...]
        nm = ADAM_B1 * m_ref[...] + (1.0 - ADAM_B1) * gv
        nv = ADAM_B2 * v_ref[...] + (1.0 - ADAM_B2) * (gv * gv)
        d_ref[...] = -ADAM_LR * ((nm / c1) / (jnp.sqrt(nv / c2) + ADAM_EPS) + ADAM_WD * w_ref[...])
        nm_ref[...] = nm
        nv_ref[...] = nv

    spec = pl.BlockSpec((tr, c), lambda i: (i, 0))
    return pl.pallas_call(
        body, name=name, grid=(r // tr,), in_specs=[spec] * 4, out_specs=[spec] * 3,
        out_shape=[jax.ShapeDtypeStruct((r, c), F32)] * 3, compiler_params=_params(("parallel",)),
    )(w, g, m, v)


def _adamw_layer(layer, w, g, m, v, bufs, name):
    depth, r, c = w.shape
    tr = _div_tile(r, max(8, (1 << 19) // c // 8 * 8), 8)
    c1 = 1.0 - ADAM_B1 ** ADAM_STEP
    c2 = 1.0 - ADAM_B2 ** ADAM_STEP

    def body(*refs):
        w_ref, g_ref, m_ref, v_ref = refs[:4]
        go_ref, d_ref, nm_ref, nv_ref = refs[-4:]
        gv = g_ref[...]
        nm = ADAM_B1 * m_ref[...] + (1.0 - ADAM_B1) * gv
        nv = ADAM_B2 * v_ref[...] + (1.0 - ADAM_B2) * (gv * gv)
        d_ref[...] = -ADAM_LR * ((nm / c1) / (jnp.sqrt(nv / c2) + ADAM_EPS) + ADAM_WD * w_ref[...])
        nm_ref[...] = nm
        nv_ref[...] = nv
        go_ref[...] = gv

    lay = pl.BlockSpec((None, tr, c), lambda i: (layer, i, 0))
    in_specs = [lay, pl.BlockSpec((tr, c), lambda i: (i, 0)), lay, lay]
    args = [w, g, m, v]
    aliases = {}
    if bufs is not None:
        in_specs += [pl.BlockSpec(memory_space=pl.ANY)] * 4
        args += list(bufs)
        aliases = {4 + k: k for k in range(4)}
    return pl.pallas_call(
        body, name=name, grid=(r // tr,), in_specs=in_specs, out_specs=[lay] * 4,
        out_shape=[jax.ShapeDtypeStruct((depth, r, c), F32)] * 4, input_output_aliases=aliases,
        compiler_params=_params(("parallel",)),
    )(*args)


def _mesh_pos():
    return lax.axis_index("x"), lax.axis_index("y"), lax.axis_index("c")


def _flip(v, bit):
    return 1 - v if bit else v


def _sum_parts(parts, name):
    _, r, c = parts.shape
    tr = _div_tile(r, 256, 16)

    def body(p_ref, o_ref):
        acc = p_ref[0].astype(F32)
        for src in range(1, N_DEVICES):
            acc = acc + p_ref[src].astype(F32)
        o_ref[...] = acc

    return pl.pallas_call(
        body, name=name, grid=(r // tr,), in_specs=[pl.BlockSpec((N_DEVICES, tr, c), lambda i: (0, i, 0))],
        out_specs=pl.BlockSpec((tr, c), lambda i: (i, 0)), out_shape=jax.ShapeDtypeStruct((r, c), F32),
        compiler_params=_params(("parallel",)),
    )(parts)


def _split_start(srcs, lands, plan, ncopies, name):
    nbuf = len(srcs) + len(lands)

    def body(*refs):
        bufs = refs[:nbuf]
        send_sem, recv_sem, token = refs[nbuf], refs[nbuf + 1], refs[-1]
        for k, (src, dst, dev) in enumerate(plan(bufs[:len(srcs)], bufs[len(srcs):])):
            pltpu.make_async_remote_copy(src_ref=src, dst_ref=dst, send_sem=send_sem.at[k], recv_sem=recv_sem.at[k],
                                         device_id=dev, device_id_type=MESH).start()
        token[...] = jnp.zeros_like(token)

    hbm = pl.BlockSpec(memory_space=pltpu.HBM)
    sem = pl.BlockSpec(memory_space=pltpu.SEMAPHORE)
    operands = [pltpu.with_memory_space_constraint(a, pltpu.HBM) for a in (*srcs, *lands)]
    outs = pl.pallas_call(
        body, name=name, in_specs=[hbm] * nbuf,
        out_specs=(sem, sem, *[hbm] * nbuf, pl.BlockSpec(memory_space=pltpu.VMEM)),
        out_shape=(pltpu.SemaphoreType.DMA((ncopies,)), pltpu.SemaphoreType.DMA((ncopies,)),
                   *[pltpu.HBM(a.shape, a.dtype) for a in operands], jax.ShapeDtypeStruct((8, LANES), F32)),
        input_output_aliases={i: 2 + i for i in range(nbuf)},
        compiler_params=pltpu.CompilerParams(has_side_effects=pltpu.SideEffectType.DATAFLOW_SIDE_EFFECTING),
    )(*operands)
    handle = dict(send=outs[0], recv=outs[1], bufs=list(outs[2:2 + nbuf]), nsrc=len(srcs), plan=plan)
    return handle, outs[-1]


def _split_wait(handle, after, name):
    nbuf, nsrc, plan = len(handle["bufs"]), handle["nsrc"], handle["plan"]

    def body(*refs):
        bufs = refs[:nbuf]
        send_sem, recv_sem = refs[nbuf], refs[nbuf + 1]
        for k, (src, dst, dev) in enumerate(plan(bufs[:nsrc], bufs[nsrc:])):
            copy = pltpu.make_async_remote_copy(src_ref=src, dst_ref=dst, send_sem=send_sem.at[k],
                                                recv_sem=recv_sem.at[k], device_id=dev, device_id_type=MESH)
            copy.wait_send()
            copy.wait_recv()

    hbm = pl.BlockSpec(memory_space=pltpu.HBM)
    sem = pl.BlockSpec(memory_space=pltpu.SEMAPHORE)
    outs = pl.pallas_call(
        body, name=name, in_specs=[hbm] * nbuf + [sem, sem, pl.BlockSpec(memory_space=pl.ANY)],
        out_specs=[hbm] * nbuf, out_shape=[pltpu.HBM(a.shape, a.dtype) for a in handle["bufs"]],
        input_output_aliases={i: i for i in range(nbuf)},
        compiler_params=pltpu.CompilerParams(has_side_effects=pltpu.SideEffectType.DATAFLOW_SIDE_EFFECTING),
    )(*handle["bufs"], handle["send"], handle["recv"], after)
    return list(outs[nsrc:])


def _own_slot(shape, dtype, block, index):
    return lax.dynamic_update_slice(lax.empty(shape, dtype), block[None], (index,) + (0,) * block.ndim)


def _gather_plan(srcs, lands):
    x, y, c = _mesh_pos()
    return [(land.at[2 * x + y], land.at[2 * x + y], (*chip, c))
            for land in lands for chip in ((1 - x, y), (x, 1 - y), (1 - x, 1 - y))]


def _scatter_plan(srcs, lands):
    x, y, c = _mesh_pos()
    out = []
    for src, land in zip(srcs, lands):
        half = src.shape[1] // 2
        for d in range(1, N_DEVICES):
            p = (_flip(x, d & 4), _flip(y, d & 2), _flip(c, d & 1))
            out.append((src.at[2 * p[0] + p[1], pl.ds(p[2] * half, half), :], land.at[4 * x + 2 * y + c], p))
    return out


def _swap_plan(srcs, lands):
    x, y, c = _mesh_pos()
    return [(src, land.at[c], (x, y, 1 - c)) for src, land in zip(srcs, lands)]


class _Gathered:
    def __init__(self, groups):
        self.groups = groups
        self.ready = {}

    def get(self, name, after=None):
        if name not in self.ready:
            handle, names, wait_name = next(g for g in self.groups if name in g[1])
            for n, full in zip(names, _split_wait(handle, after, wait_name)):
                self.ready[n] = full.reshape(-1, full.shape[-1])
        return self.ready[name]


def _allreduce_small(flat, name):
    r = flat.shape[0]

    def body(x_ref, o_ref, buf, send_sems, recv_sems):
        x, y, c = _mesh_pos()
        me = 4 * x + 2 * y + c
        buf[me] = x_ref[...]
        started = []
        peers = [(_flip(x, d & 4), _flip(y, d & 2), _flip(c, d & 1)) for d in range(1, N_DEVICES)]
        for d, p in enumerate(peers):
            cp = pltpu.make_async_remote_copy(src_ref=x_ref, dst_ref=buf.at[me], send_sem=send_sems.at[d],
                                              recv_sem=recv_sems.at[d], device_id=p, device_id_type=MESH)
            cp.start()
            started.append(cp)
        for d, p in enumerate(peers):
            slot = buf.at[4 * p[0] + 2 * p[1] + p[2]]
            pltpu.make_async_remote_copy(src_ref=slot, dst_ref=slot, send_sem=send_sems.at[d], recv_sem=recv_sems.at[d],
                                         device_id=p, device_id_type=MESH).wait_recv()
        for cp in started:
            cp.wait_send()
        acc = buf[0]
        for src in range(1, N_DEVICES):
            acc = acc + buf[src]
        o_ref[...] = acc

    vm = pl.BlockSpec(memory_space=pltpu.VMEM)
    return pl.pallas_call(
        body, name=name, in_specs=[vm], out_specs=vm, out_shape=jax.ShapeDtypeStruct((r, LANES), F32),
        scratch_shapes=[pltpu.VMEM((N_DEVICES, r, LANES), F32), pltpu.SemaphoreType.DMA((N_DEVICES - 1,)),
                        pltpu.SemaphoreType.DMA((N_DEVICES - 1,))],
        compiler_params=pltpu.CompilerParams(vmem_limit_bytes=VMEM_LIMIT_BYTES),
    )(flat)


def _bucket_ids(dil):
    rel = (np.arange(BLOCK)[:, None] + BLOCK - np.arange(2 * BLOCK)[None, :]) * dil
    max_exact = N_BUCKETS // 2
    d = np.maximum(rel, 0)
    large = max_exact + (np.log(np.maximum(d, 1).astype(np.float32) / max_exact)
                         / np.float32(np.log(T5_MAX_DIST / max_exact)) * (N_BUCKETS - max_exact)).astype(np.int32)
    large = np.minimum(large, N_BUCKETS - 1)
    return np.where(d < max_exact, d, large).astype(np.int32)


def _block_bias(table, dil):
    onehot = (jnp.asarray(_bucket_ids(dil))[:, :, None] == jnp.arange(N_BUCKETS)[None, None, :]).astype(F32)
    return jnp.einsum("ijb,bh->hij", onehot, table.astype(F32), precision=lax.Precision.HIGHEST)


def _tile_gain(g, n):
    return jnp.tile(g.reshape(1, HEAD_DIM), (1, n))


def _layer_fwd(x, p, cfg):
    w = p["weights"]
    h1 = _rmsnorm_fwd(x, p["attn_norm"], "attn_norm_fwd")
    proj = _matmul(h1, w.get("w_in_t", h1), "nt", F32, "in_proj", tm=1024, tn=896, tk=2048)
    aq, ak, av, bq, bk, bv, cq, ck, cv = _qk_prep(proj, p["gains"], cfg, "qk_prep")
    akt = _dilated_t(ak, 1)
    oa, lse_a = _banded_fwd(aq, akt, av, p["bias_a"], p["sinks"], cfg.nha, cfg.nkva, WINDOW_A - 1, 1, "swa_fwd")
    bkt = _keys_on_lanes(bk, SB_CHUNK)
    ob, tot_b = _sb_fwd(bq, bkt, bv, "stickbreak_fwd")
    ocs, lses, ckts = [], [], []
    for (window, dil), bias in zip(DILATED_PAIRS, p["bias_c"]):
        ckts.append(_dilated_t(ck, dil))
        o, l = _banded_fwd(cq, ckts[-1], cv, bias, None, cfg.nhc, cfg.nhc, window // dil, dil, f"dilated{dil}_fwd")
        ocs.append(o)
        lses.append(l)
    mix, oc, lse_c = _mix_fwd(oa, ob, ocs, lses, p["mix_gain"], cfg, "mix_fwd")
    xm = _matmul(mix, w.get("w_out", mix), "nn", F32, "out_proj", tm=1024, tn=1024, tk=2048, residual=x)
    h2 = _rmsnorm_fwd(xm, p["ffn_norm"], "ffn_norm_fwd")
    u = _matmul(h2, w.get("w_up_t", h2), "nt", F32, "up_proj", tm=1024, tn=1024, tk=2048)
    act = _conv_act_fwd(u, p["conv_w"], p["conv_b"], cfg.f, "conv_act_fwd")
    y = _matmul(act, w.get("w_down", act), "nn", F32, "down_proj", tm=1024, tn=1024, tk=1408, residual=xm)
    saved = dict(x=x, h1=h1, proj=proj, q=(aq, ak, av, bq, bk, bv, cq, ck, cv), oa=oa, lse_a=lse_a, ob=ob,
                 tot_b=tot_b, akt=akt, bkt=bkt, ckts=ckts, oc=oc, lse_c=lse_c, mix=mix, xm=xm, h2=h2, u=u, act=act)
    return y, saved


def _layer_bwd(dy, dyb, sv, p, dbias, cfg, on_grad):
    aq, ak, av, bq, bk, bv, cq, ck, cv = sv["q"]
    w = p["weights"]
    anchor = on_grad(_matmul(sv["act"], dyb, "tn", BF16, "down_proj_dw", tm=1408, tn=2048, tk=1024))
    dact = _matmul(dyb, w.get("w_down"), "nt", F32, "down_proj_dx", tm=1024, tn=1408, tk=2048)
    dug, duu, dwg, dwu, dbg, dbu = _conv_act_bwd(sv["u"], dact, p["conv_w"], p["conv_b"] + anchor, cfg.f,
                                                 "conv_act_bwd")
    du = jnp.concatenate([dug, duu], axis=1)
    anchor = on_grad(_matmul(du, sv["h2"], "tn", BF16, "up_proj_dw", tm=1408, tn=2048, tk=1024))
    dh2 = _matmul(du, w.get("w_up_t"), "nn", F32, "up_proj_dx", tm=1024, tn=2048, tk=1024)
    dxm, dxmb, g_ffn_norm = _rmsnorm_bwd(sv["xm"], p["ffn_norm"] + anchor, dh2, dy, "ffn_norm_bwd")
    anchor = on_grad(_matmul(sv["mix"], dxmb, "tn", BF16, "out_proj_dw", tm=1024, tn=2048, tk=1024))
    dmix = _matmul(dxmb, w.get("w_out"), "nt", F32, "out_proj_dx", tm=1024, tn=1024, tk=2048)
    doa, dob, doc, g_mix_gain, dsum_a, dsum_c = _mix_bwd(dmix, sv["oa"], sv["ob"], sv["oc"], p["mix_gain"] + anchor,
                                                         cfg, "mix_bwd")
    daq, dak, dav, dbias_a, g_sinks = _banded_bwd(aq, ak, sv["akt"], av, sv["lse_a"], dsum_a, doa, p["bias_a"],
                                                 p["sinks"], dbias[0], cfg.nha, cfg.nkva, WINDOW_A - 1, 1, "swa_bwd")
    dbq, dbk, dbv = _sb_bwd(bq, bk, sv["bkt"], bv, sv["tot_b"], dob, "stickbreak_bwd")
    dcq, dck, dcv, dbias_c = [], [], [], []
    for idx, ((window, dil), bias) in enumerate(zip(DILATED_PAIRS, p["bias_c"])):
        a, b, c, d, _ = _banded_bwd(cq, ck, sv["ckts"][idx], cv, sv["lse_c"], dsum_c, doc, bias, None, dbias[1][idx],
                                    cfg.nhc, cfg.nhc, window // dil, dil, f"dilated{dil}_bwd")
        dcq.append(a)
        dck.append(b)
        dcv.append(c)
        dbias_c.append(d)
    dproj, g_aq, g_ak, g_cq, g_ck = _qk_prep_bwd(
        sv["proj"], p["gains"], [[daq], [dak], [dav], [dbq], [dbk], [dbv], dcq, dck, dcv], cfg, "qk_prep_bwd")
    anchor = on_grad(_matmul(dproj, sv["h1"], "tn", BF16, "in_proj_dw", tm=768, tn=2048, tk=1024))
    dh1 = _matmul(dproj, w.get("w_in_t"), "nn", F32, "in_proj_dx", tm=1024, tn=2048, tk=768)
    dx, dxb, g_attn_norm = _rmsnorm_bwd(sv["x"], p["attn_norm"] + anchor, dh1, dxm, "attn_norm_bwd")

    def fold(g):
        return jnp.sum(g.reshape(-1, HEAD_DIM), axis=0)

    small = dict(attn_norm=g_attn_norm[0], a_q_gain=fold(g_aq), a_k_gain=fold(g_ak), a_sinks=g_sinks,
                 c_q_gain=fold(g_cq), c_k_gain=fold(g_ck), mix_out_gain=g_mix_gain[0], ffn_norm=g_ffn_norm[0],
                 conv_w=jnp.concatenate([dwg, dwu], axis=1), conv_b=jnp.concatenate([dbg, dbu], axis=1)[0])
    return dx, dxb, small, (dbias_a, dbias_c)


_SMALL = ("attn_norm", "a_q_gain", "a_k_gain", "a_sinks", "c_q_gain", "c_k_gain", "rel_bias_table", "mix_out_gain",
          "ffn_norm", "conv_w", "conv_b")


def _pack(arrays):
    flat = jnp.concatenate([a.reshape(-1).astype(F32) for a in arrays])
    pad = (-flat.shape[0]) % (8 * LANES)
    return jnp.pad(flat, (0, pad)).reshape(-1, LANES)


def _unpack(flat, shapes):
    flat = flat.reshape(-1)
    out, pos = [], 0
    for sh in shapes:
        n = int(np.prod(sh))
        out.append(flat[pos:pos + n].reshape(sh))
        pos += n
    return out


def kernel(x, attn_norm, w_in, a_q_gain, a_k_gain, a_sinks, c_q_gain, c_k_gain, rel_bias_table, mix_out_gain, w_out, ffn_norm, w_up, conv_w, conv_b, w_down, loss_target, m_attn_norm, m_w_in, m_a_q_gain, m_a_k_gain, m_a_sinks, m_c_q_gain, m_c_k_gain, m_rel_bias_table, m_mix_out_gain, m_w_out, m_ffn_norm, m_w_up, m_conv_w, m_conv_b, m_w_down, v_attn_norm, v_w_in, v_a_q_gain, v_a_k_gain, v_a_sinks, v_c_q_gain, v_c_k_gain, v_rel_bias_table, v_mix_out_gain, v_w_out, v_ffn_norm, v_w_up, v_conv_w, v_conv_b, v_w_down):
    depth, d = attn_norm.shape
    f = w_down.shape[1] * N_CHIPS
    cfg = _Cfg(d, f)
    chip = 2 * lax.axis_index("x") + lax.axis_index("y")

    cw_cols = conv_w.shape[2]
    cw_flat = conv_w.reshape(-1)
    cw_rows = -(-cw_flat.shape[0] // (16 * LANES)) * 16
    cw_pad = jnp.pad(cw_flat, (0, cw_rows * LANES - cw_flat.shape[0])).reshape(cw_rows, LANES)

    table_a, table_c = rel_bias_table[:, :cfg.nha], rel_bias_table[:, cfg.nha:]
    bias_a = _block_bias(table_a, 1)
    bias_c = [_block_bias(table_c, dil) for _, dil in DILATED_PAIRS]

    layers, anchor = [], 0.0
    for l in range(depth):
        shards = [w_in[l].T.astype(BF16), w_out[l].astype(BF16), w_up[l].T.astype(BF16), w_down[l].astype(BF16)]
        names = ["w_in_t", "w_out", "w_up_t", "w_down"]
        if l == 0:
            todo = [([cw_pad, shards[0]], ["conv_w", names[0]])] + [([s], [n]) for s, n in zip(shards[1:], names[1:])]
        else:
            todo = [(shards, names)]
        groups = []
        for k, (srcs, group_names) in enumerate(todo):
            lands = [_own_slot((N_CHIPS,) + s.shape, s.dtype, s, chip) for s in srcs]
            handle, token = _split_start([], lands, _gather_plan, 3 * len(lands), f"gather_start_{l}_{k}")
            anchor = anchor + token[0, 0]
            groups.append((handle, group_names, f"gather_wait_{l}_{k}"))
        layers.append(dict(
            attn_norm=attn_norm[l].reshape(1, d), ffn_norm=ffn_norm[l].reshape(1, d),
            mix_gain=mix_out_gain[l].reshape(1, d),
            gains=(_tile_gain(a_q_gain[l], cfg.nha), _tile_gain(a_k_gain[l], cfg.nkva),
                   _tile_gain(c_q_gain[l], cfg.nhc), _tile_gain(c_k_gain[l], cfg.nhc)),
            sinks=a_sinks[l], bias_a=bias_a, bias_c=bias_c, conv_b=conv_b[l].reshape(1, 2 * f),
            weights=_Gathered(groups)))
    cw_all = layers[0]["weights"].get("conv_w", layers[0]["attn_norm"] + anchor)
    cw_all = cw_all.reshape(N_CHIPS, -1)[:, :cw_flat.shape[0]].reshape(N_CHIPS, depth, CONV_WIDTH, cw_cols)
    conv_w_full = jnp.transpose(cw_all, (1, 2, 0, 3)).reshape(depth, CONV_WIDTH, N_CHIPS * cw_cols)
    for l in range(depth):
        layers[l]["conv_w"] = conv_w_full[l]

    act = x[0]
    saved = []
    for l in range(depth):
        act, sv = _layer_fwd(act, layers[l], cfg)
        saved.append(sv)
    dact, dactb, loss_blk = _loss_head(act, loss_target[0], "loss_head")
    loss = lax.psum(loss_blk[0, 0], ("x", "y", "c"))

    core = lax.axis_index("c")

    def start_scatter(grads, name):
        srcs = [g.reshape(N_CHIPS, -1, g.shape[-1]) for g in grads]
        lands = []
        for g in srcs:
            half = g.shape[1] // 2
            own = lax.dynamic_slice(g, (chip, core * half, 0), (1, half, g.shape[2]))[0]
            lands.append(_own_slot((N_DEVICES, half, g.shape[2]), g.dtype, own, 2 * chip + core))
        return _split_start(srcs, lands, _scatter_plan, (N_DEVICES - 1) * len(srcs), name)

    def finish_scatter(l, handles, after):
        parts = [pt for k, h in enumerate(handles) for pt in _split_wait(h, after, f"scatter_wait_{l}_{k}")][::-1]
        halves = [_sum_parts(pt, f"sum_grads_{t}") for t, pt in enumerate(parts)]
        lands = [_own_slot((2,) + h.shape, h.dtype, h, core) for h in halves]
        return _split_start(halves, lands, _swap_plan, len(halves), f"swap_start_{l}")[0]

    dbias = (jnp.zeros_like(bias_a), [jnp.zeros_like(b) for b in bias_c])
    small_grads = [None] * depth
    swaps = [None] * depth
    pending = None
    for l in reversed(range(depth)):
        made = []

        def on_grad(g, l=l, made=made):
            if l:
                made.append(g)
                return 0.0
            handle, token = start_scatter([g], f"scatter_start_0_{len(made)}")
            made.append(handle)
            return token[0, 0]

        dact, dactb, small_grads[l], dbias = _layer_bwd(dact, dactb, saved[l], layers[l], dbias, cfg, on_grad)
        if pending is not None:
            swaps[l + 1] = finish_scatter(l + 1, pending, dact)
        if l:
            handle, token = start_scatter(made, f"scatter_start_{l}")
            pending = [handle]
            layers[l - 1]["conv_b"] = layers[l - 1]["conv_b"] + token[0, 0]
        else:
            pending = made
    grad_x = dact[None]

    tabs = _bias_table_grad([dbias[0]] + dbias[1], [jnp.asarray(_bucket_ids(1))]
                            + [jnp.asarray(_bucket_ids(dil)) for _, dil in DILATED_PAIRS], "bias_table_grad")
    g_table_a = tabs[0][:, :N_BUCKETS].T
    g_table_c = (tabs[1] + tabs[2] + tabs[3])[:, :N_BUCKETS].T
    g_table = jnp.concatenate([g_table_a, g_table_c], axis=1)
    small_local = {k: jnp.stack([small_grads[l][k] for l in range(depth)]) for k in _SMALL if k != "rel_bias_table"}
    small_local["rel_bias_table"] = g_table
    shapes = [small_local[k].shape for k in _SMALL]
    reduced = dict(zip(_SMALL, _unpack(_allreduce_small(_pack([small_local[k] for k in _SMALL]), "allreduce_small"),
                                       shapes)))
    reduced["conv_w"] = lax.dynamic_slice_in_dim(reduced["conv_w"], chip * cw_cols, cw_cols, axis=2)

    given = dict(attn_norm=attn_norm, a_q_gain=a_q_gain, a_k_gain=a_k_gain, a_sinks=a_sinks, c_q_gain=c_q_gain,
                 c_k_gain=c_k_gain, rel_bias_table=rel_bias_table, mix_out_gain=mix_out_gain, ffn_norm=ffn_norm,
                 conv_w=conv_w, conv_b=conv_b)
    moms = dict(attn_norm=(m_attn_norm, v_attn_norm), a_q_gain=(m_a_q_gain, v_a_q_gain),
                a_k_gain=(m_a_k_gain, v_a_k_gain), a_sinks=(m_a_sinks, v_a_sinks), c_q_gain=(m_c_q_gain, v_c_q_gain),
                c_k_gain=(m_c_k_gain, v_c_k_gain), rel_bias_table=(m_rel_bias_table, v_rel_bias_table),
                mix_out_gain=(m_mix_out_gain, v_mix_out_gain), ffn_norm=(m_ffn_norm, v_ffn_norm),
                conv_w=(m_conv_w, v_conv_w), conv_b=(m_conv_b, v_conv_b))
    sshapes = [given[k].shape for k in _SMALL]
    s_delta, s_m, s_v = _adamw(_pack([given[k] for k in _SMALL]), _pack([reduced[k] for k in _SMALL]),
                               _pack([moms[k][0] for k in _SMALL]), _pack([moms[k][1] for k in _SMALL]), "adamw_small")
    grads = dict(reduced)
    deltas = dict(zip(_SMALL, _unpack(s_delta, sshapes)))
    new_m = dict(zip(_SMALL, _unpack(s_m, sshapes)))
    new_v = dict(zip(_SMALL, _unpack(s_v, sshapes)))

    big_given = dict(w_in=(w_in, m_w_in, v_w_in, True), w_out=(w_out, m_w_out, v_w_out, False),
                     w_up=(w_up, m_w_up, v_w_up, True), w_down=(w_down, m_w_down, v_w_down, False))
    names = ("w_in", "w_out", "w_up", "w_down")
    bufs = {name: None for name in names}
    after = s_delta
    for l in reversed(range(depth)):
        if l == 0:
            swaps[0] = finish_scatter(0, pending, after)
        layer_grads = [g.reshape(-1, g.shape[-1]) for g in _split_wait(swaps[l], after, f"swap_wait_{l}")]
        for t, name in enumerate(names):
            wt, mt, vt, transposed = big_given[name]
            g = layer_grads[t].T if transposed else layer_grads[t]
            bufs[name] = _adamw_layer(l, wt, g, mt, vt, bufs[name], f"adamw_{name}_{l}")
            after = bufs[name][1]
    for name in names:
        grads[name], deltas[name], new_m[name], new_v[name] = bufs[name]

    order = ("attn_norm", "w_in", "a_q_gain", "a_k_gain", "a_sinks", "c_q_gain", "c_k_gain", "rel_bias_table",
             "mix_out_gain", "w_out", "ffn_norm", "w_up", "conv_w", "conv_b", "w_down")
    return (loss, grad_x, *[grads[k] for k in order], *[deltas[k] for k in order], *[new_m[k] for k in order],
            *[new_v[k] for k in order])
```

```python
import numpy as np
import jax
import jax.numpy as jnp
from jax import lax
from jax.experimental import pallas as pl
from jax.experimental.pallas import tpu as pltpu

F32 = jnp.float32
BF16 = jnp.bfloat16
MESH = pl.DeviceIdType.MESH

HEAD_DIM = 64
BLOCK = 128
LANES = 128
EPS = 1e-6
NEG_INF = -1e30
WINDOW_A = 128
DILATED_PAIRS = ((128, 1), (512, 4), (2048, 16))
N_BUCKETS = 32
T5_MAX_DIST = 2048
CONV_WIDTH = 3
ADAM_LR = 0.001
ADAM_B1 = 0.9
ADAM_B2 = 0.999
ADAM_EPS = 1e-08
ADAM_WD = 0.01
ADAM_STEP = 10
N_CHIPS = 4
N_DEVICES = 8
VMEM_LIMIT_BYTES = 48 * 1024 * 1024
QK_SCALE = HEAD_DIM ** -0.5


def _params(sem=None):
    return pltpu.CompilerParams(dimension_semantics=sem, vmem_limit_bytes=VMEM_LIMIT_BYTES)


def _div_tile(n, cap, mult):
    best = None
    for t in range(mult, min(n, cap) + 1, mult):
        if n % t == 0:
            best = t
    return n if best is None else best


def _dot(a, b):
    return lax.dot_general(a, b, (((1,), (0,)), ((), ())), preferred_element_type=F32)


def _dot_nt(a, b):
    return lax.dot_general(a, b, (((1,), (1,)), ((), ())), preferred_element_type=F32)


def _dot_tn(a, b):
    return lax.dot_general(a, b, (((0,), (0,)), ((), ())), preferred_element_type=F32)


def _split_dot(x, m):
    hi = x.astype(BF16)
    lo = (x - hi.astype(F32)).astype(BF16)
    return _dot(hi, m) + _dot(lo, m)


class _Cfg:
    def __init__(self, d_model, d_ff):
        nh = d_model // HEAD_DIM
        self.d = d_model
        self.f = d_ff
        self.nha = nh // 4
        self.nkva = self.nha // 4
        self.nhb = nh // 4
        self.nhc = nh // 2
        self.a_q = self.nha * HEAD_DIM
        self.a_kv = self.nkva * HEAD_DIM
        self.b_w = self.nhb * HEAD_DIM
        self.c_w = self.nhc * HEAD_DIM
        sizes = [self.a_q, self.a_kv, self.a_kv, self.b_w, self.b_w, self.b_w, self.c_w, self.c_w, self.c_w]
        starts = [0] + [int(s) for s in np.cumsum(sizes)[:-1]]
        self.sections = list(zip(starts, sizes))
        self.in_width = int(sum(sizes))
        assert all(s % LANES == 0 for s in sizes)


def _matmul(a, b, mode, out_dtype, name, tm=512, tn=512, tk=512, residual=None):
    if mode == "tn":
        kdim, m = a.shape
    else:
        m, kdim = a.shape
    n = b.shape[0] if mode == "nt" else b.shape[1]
    tm, tn, tk = _div_tile(m, tm, LANES), _div_tile(n, tn, LANES), _div_tile(kdim, tk, LANES)
    nk = kdim // tk
    if mode == "tn":
        a_spec = pl.BlockSpec((tk, tm), lambda i, j, k: (k, i))
    else:
        a_spec = pl.BlockSpec((tm, tk), lambda i, j, k: (i, k))
    if mode == "nt":
        b_spec = pl.BlockSpec((tn, tk), lambda i, j, k: (j, k))
    else:
        b_spec = pl.BlockSpec((tk, tn), lambda i, j, k: (k, j))
    dot = {"nn": _dot, "nt": _dot_nt, "tn": _dot_tn}[mode]
    o_spec = pl.BlockSpec((tm, tn), lambda i, j, k: (i, j))
    in_specs = [a_spec, b_spec]
    args = [a, b]
    if residual is not None:
        in_specs.append(o_spec)
        args.append(residual)

    def body(*refs):
        if residual is None:
            a_ref, b_ref, o_ref, acc = refs
        else:
            a_ref, b_ref, r_ref, o_ref, acc = refs
        k = pl.program_id(2)

        @pl.when(k == 0)
        def _():
            acc[...] = jnp.zeros_like(acc)

        acc[...] += dot(a_ref[...].astype(BF16), b_ref[...].astype(BF16))

        @pl.when(k == nk - 1)
        def _():
            r = acc[...]
            if residual is not None:
                r = r + r_ref[...]
            o_ref[...] = r.astype(out_dtype)

    return pl.pallas_call(
        body, name=name, grid=(m // tm, n // tn, nk), in_specs=in_specs, out_specs=o_spec,
        out_shape=jax.ShapeDtypeStruct((m, n), out_dtype), scratch_shapes=[pltpu.VMEM((tm, tn), F32)],
        compiler_params=_params(("parallel", "parallel", "arbitrary")),
    )(*args)


def _rmsnorm_fwd(x, g, name):
    s, d = x.shape
    ts = _div_tile(s, 512, 16)

    def body(x_ref, g_ref, o_ref):
        xv = x_ref[...]
        r = lax.rsqrt(jnp.mean(xv * xv, axis=-1, keepdims=True) + EPS)
        o_ref[...] = (xv * r * g_ref[...]).astype(BF16)

    return pl.pallas_call(
        body, name=name, grid=(s // ts,),
        in_specs=[pl.BlockSpec((ts, d), lambda i: (i, 0)), pl.BlockSpec((1, d), lambda i: (0, 0))],
        out_specs=pl.BlockSpec((ts, d), lambda i: (i, 0)), out_shape=jax.ShapeDtypeStruct((s, d), BF16),
        compiler_params=_params(("parallel",)),
    )(x, g)


def _rmsnorm_bwd(x, g, dh, dres, name):
    s, d = x.shape
    ts = _div_tile(s, 256, 16)

    def body(x_ref, g_ref, dh_ref, dres_ref, dx_ref, dxb_ref, dg_ref):
        @pl.when(pl.program_id(0) == 0)
        def _():
            dg_ref[...] = jnp.zeros_like(dg_ref)

        xv = x_ref[...]
        r = lax.rsqrt(jnp.mean(xv * xv, axis=-1, keepdims=True) + EPS)
        xhat = xv * r
        dhv = dh_ref[...]
        dxhat = dhv * g_ref[...]
        dx = dres_ref[...] + r * (dxhat - xhat * jnp.mean(dxhat * xhat, axis=-1, keepdims=True))
        dx_ref[...] = dx
        dxb_ref[...] = dx.astype(BF16)
        dg_ref[...] += jnp.sum(dhv * xhat, axis=0, keepdims=True)

    row = pl.BlockSpec((ts, d), lambda i: (i, 0))
    vec = pl.BlockSpec((1, d), lambda i: (0, 0))
    return pl.pallas_call(
        body, name=name, grid=(s // ts,), in_specs=[row, vec, row, row], out_specs=[row, row, vec],
        out_shape=[jax.ShapeDtypeStruct((s, d), F32), jax.ShapeDtypeStruct((s, d), BF16),
                   jax.ShapeDtypeStruct((1, d), F32)],
        compiler_params=_params(("arbitrary",)),
    )(x, g, dh, dres)


def _head_mean_matrix():
    idx = np.arange(LANES) // HEAD_DIM
    return jnp.asarray((idx[:, None] == idx[None, :]).astype(np.float32) / HEAD_DIM, dtype=BF16)


def _head_mean(y, m128):
    w = y.shape[1]
    parts = [_split_dot(y[:, c * LANES:(c + 1) * LANES], m128) for c in range(w // LANES)]
    return parts[0] if len(parts) == 1 else jnp.concatenate(parts, axis=1)


_NORMED_SECTIONS = (0, 1, 6, 7)
_QUERY_SECTIONS = (0, 3, 6)


def _qk_prep(proj, gains, cfg, name):
    s = proj.shape[0]
    ts = _div_tile(s, 256, 16)
    m128 = _head_mean_matrix()

    def body(p_ref, m_ref, g0, g1, g6, g7, *outs):
        gref = dict(zip(_NORMED_SECTIONS, (g0, g1, g6, g7)))
        for idx, (st, w) in enumerate(cfg.sections):
            xv = p_ref[:, st:st + w]
            if idx in gref:
                r = lax.rsqrt(_head_mean(xv * xv, m_ref[...]) + EPS)
                xv = xv * r * gref[idx][...]
            if idx in _QUERY_SECTIONS:
                xv = xv * QK_SCALE
            outs[idx][...] = xv.astype(BF16)

    in_specs = [pl.BlockSpec((ts, cfg.in_width), lambda i: (i, 0)), pl.BlockSpec((LANES, LANES), lambda i: (0, 0))]
    in_specs += [pl.BlockSpec((1, cfg.sections[k][1]), lambda i: (0, 0)) for k in _NORMED_SECTIONS]
    out_specs = [pl.BlockSpec((ts, w), lambda i: (i, 0)) for _, w in cfg.sections]
    out_shape = [jax.ShapeDtypeStruct((s, w), BF16) for _, w in cfg.sections]
    return pl.pallas_call(
        body, name=name, grid=(s // ts,), in_specs=in_specs, out_specs=out_specs, out_shape=out_shape,
        compiler_params=_params(("parallel",)),
    )(proj, m128, *gains)


def _qk_prep_bwd(proj, gains, grads, cfg, name):
    s = proj.shape[0]
    ts = _div_tile(s, 128, 16)
    m128 = _head_mean_matrix()
    counts = [len(gl) for gl in grads]
    flat = [g for gl in grads for g in gl]

    def body(*refs):
        p_ref, m_ref = refs[0], refs[1]
        gref = dict(zip(_NORMED_SECTIONS, refs[2:6]))
        g_in = refs[6:6 + len(flat)]
        dp_ref = refs[6 + len(flat)]
        dgain = dict(zip(_NORMED_SECTIONS, refs[7 + len(flat):]))

        @pl.when(pl.program_id(0) == 0)
        def _():
            for k in _NORMED_SECTIONS:
                dgain[k][...] = jnp.zeros_like(dgain[k])

        pos = 0
        for idx, (st, w) in enumerate(cfg.sections):
            dy = g_in[pos][...].astype(F32)
            for extra in g_in[pos + 1:pos + counts[idx]]:
                dy = dy + extra[...].astype(F32)
            pos += counts[idx]
            if idx in gref:
                xv = p_ref[:, st:st + w]
                r = lax.rsqrt(_head_mean(xv * xv, m_ref[...]) + EPS)
                xhat = xv * r
                dxhat = dy * gref[idx][...]
                dgain[idx][...] += jnp.sum(dy * xhat, axis=0, keepdims=True)
                dy = r * (dxhat - xhat * _head_mean(dxhat * xhat, m_ref[...]))
            dp_ref[:, st:st + w] = dy.astype(BF16)

    in_specs = [pl.BlockSpec((ts, cfg.in_width), lambda i: (i, 0)), pl.BlockSpec((LANES, LANES), lambda i: (0, 0))]
    in_specs += [pl.BlockSpec((1, cfg.sections[k][1]), lambda i: (0, 0)) for k in _NORMED_SECTIONS]
    for idx, (_, w) in enumerate(cfg.sections):
        in_specs += [pl.BlockSpec((ts, w), lambda i: (i, 0))] * counts[idx]
    out_specs = [pl.BlockSpec((ts, cfg.in_width), lambda i: (i, 0))]
    out_specs += [pl.BlockSpec((1, cfg.sections[k][1]), lambda i: (0, 0)) for k in _NORMED_SECTIONS]
    out_shape = [jax.ShapeDtypeStruct((s, cfg.in_width), BF16)]
    out_shape += [jax.ShapeDtypeStruct((1, cfg.sections[k][1]), F32) for k in _NORMED_SECTIONS]
    return pl.pallas_call(
        body, name=name, grid=(s // ts,), in_specs=in_specs, out_specs=out_specs, out_shape=out_shape,
        compiler_params=_params(("arbitrary",)),
    )(proj, m128, *gains, *flat)


def _band_masks(max_dist):
    row = lax.broadcasted_iota(jnp.int32, (BLOCK, BLOCK), 0)
    col = lax.broadcasted_iota(jnp.int32, (BLOCK, BLOCK), 1)
    return row + BLOCK - col <= max_dist, col <= row


def _dilated_t(a, dil):
    s, w = a.shape
    return _keys_on_lanes(a.reshape(s // dil, dil * w), BLOCK)


def _undilated(at, dil):
    nblk, dw, _ = at.shape
    return jnp.transpose(at, (0, 2, 1)).reshape(nblk * BLOCK * dil, dw // dil)


def _banded_fwd(q, kt, v, bias, sinks, hq, hk, max_dist, dil, name):
    s = q.shape[0]
    wq, wk, sd, grp = hq * HEAD_DIM, hk * HEAD_DIM, s // dil, hq // hk
    nb = sd // BLOCK
    has_sink = sinks is not None

    def body(*refs):
        if has_sink:
            q_ref, ktp_ref, ktc_ref, vp_ref, vc_ref, b_ref, s_ref, o_ref, l_ref = refs
        else:
            q_ref, ktp_ref, ktc_ref, vp_ref, vc_ref, b_ref, o_ref, l_ref = refs
        i = pl.program_id(1)
        mprev, mcur = _band_masks(max_dist)
        mask = jnp.concatenate([jnp.logical_and(mprev, i > 0), mcur], axis=1)
        for h in range(hq):
            sq = slice(h * HEAD_DIM, (h + 1) * HEAD_DIM)
            sk = slice((h // grp) * HEAD_DIM, (h // grp + 1) * HEAD_DIM)
            kt = jnp.concatenate([ktp_ref[sk, :], ktc_ref[sk, :]], axis=1)
            vv = jnp.concatenate([vp_ref[:, sk], vc_ref[:, sk]], axis=0)
            sc = jnp.where(mask, _dot(q_ref[:, sq], kt) + b_ref[h], NEG_INF)
            m = jnp.max(sc, axis=-1, keepdims=True)
            if has_sink:
                m = jnp.maximum(m, s_ref[h])
            p = jnp.exp(sc - m)
            den = jnp.sum(p, axis=-1, keepdims=True)
            if has_sink:
                den = den + jnp.exp(s_ref[h] - m)
            o_ref[:, sq] = _dot(p.astype(BF16), vv) / den
            l_ref[:, h:h + 1] = m + jnp.log(den)

    qspec = pl.BlockSpec((BLOCK, wq), lambda r, i: (i, r))
    kprev = pl.BlockSpec((BLOCK, wk), lambda r, i: (jnp.maximum(i - 1, 0), r))
    kcur = pl.BlockSpec((BLOCK, wk), lambda r, i: (i, r))
    ktprev = pl.BlockSpec((None, wk, BLOCK), lambda r, i: (jnp.maximum(i - 1, 0), r, 0))
    ktcur = pl.BlockSpec((None, wk, BLOCK), lambda r, i: (i, r, 0))
    in_specs = [qspec, ktprev, ktcur, kprev, kcur, pl.BlockSpec((hq, BLOCK, 2 * BLOCK), lambda r, i: (0, 0, 0))]
    v2 = v.reshape(sd, dil * wk)
    args = [q.reshape(sd, dil * wq), kt, kt, v2, v2, bias]
    if has_sink:
        in_specs.append(pl.BlockSpec(memory_space=pltpu.SMEM))
        args.append(sinks)
    out, lse = pl.pallas_call(
        body, name=name, grid=(dil, nb), in_specs=in_specs,
        out_specs=[qspec, pl.BlockSpec((None, BLOCK, hq), lambda r, i: (r, i, 0))],
        out_shape=[jax.ShapeDtypeStruct((sd, dil * wq), F32), jax.ShapeDtypeStruct((dil, sd, hq), F32)],
        compiler_params=_params(("parallel", "parallel")),
    )(*args)
    return out.reshape(s, wq), jnp.transpose(lse, (1, 0, 2)).reshape(s, hq)


def _per_head_dilated(a, dil):
    s, h = a.shape
    return jnp.transpose(a.reshape(s // dil, dil, h), (1, 0, 2))


def _banded_bwd(q, k, kt, v, lse, dsum, do, bias, sinks, dbias_init, hq, hk, max_dist, dil, name):
    s = q.shape[0]
    wq, wk, sd, grp = hq * HEAD_DIM, hk * HEAD_DIM, s // dil, hq // hk
    nb = sd // BLOCK
    has_sink = sinks is not None

    def body(*refs):
        (q_ref, qn_ref, qt_ref, qtn_ref, kp_ref, kc_ref, ktp_ref, ktc_ref, vtp_ref, vtc_ref, l_ref, ln_ref, d_ref,
         dn_ref, do_ref, don_ref, dot_ref, dotn_ref, b_ref, dbi_ref) = refs[:20]
        rest = refs[20:]
        if has_sink:
            s_ref, dq_ref, dkt_ref, dvt_ref, db_ref, ds_ref = rest
        else:
            dq_ref, dkt_ref, dvt_ref, db_ref = rest
        j = pl.program_id(1)

        @pl.when(jnp.logical_and(pl.program_id(0) == 0, j == 0))
        def _():
            db_ref[...] = dbi_ref[...]
            if has_sink:
                ds_ref[...] = jnp.zeros_like(ds_ref)

        mprev_static, mcur = _band_masks(max_dist)
        mask = jnp.concatenate([jnp.logical_and(mprev_static, j > 0), mcur], axis=1)
        mnext = jnp.logical_and(mprev_static, j + 1 < nb)
        dkt_acc = [jnp.zeros((HEAD_DIM, BLOCK), F32) for _ in range(hk)]
        dvt_acc = [jnp.zeros((HEAD_DIM, BLOCK), F32) for _ in range(hk)]
        for h in range(hq):
            g = h // grp
            sq = slice(h * HEAD_DIM, (h + 1) * HEAD_DIM)
            sk = slice(g * HEAD_DIM, (g + 1) * HEAD_DIM)
            kt2 = jnp.concatenate([ktp_ref[sk, :], ktc_ref[sk, :]], axis=1)
            vt2 = jnp.concatenate([vtp_ref[sk, :], vtc_ref[sk, :]], axis=1)
            k2 = jnp.concatenate([kp_ref[:, sk], kc_ref[:, sk]], axis=0)
            lcol = l_ref[:, h:h + 1]
            dcol = d_ref[:, h:h + 1]
            sc = _dot(q_ref[:, sq], kt2) + b_ref[h]
            p = jnp.where(mask, jnp.exp(sc - lcol), 0.0)
            ds = p * (_dot(do_ref[:, sq], vt2) - dcol)
            dsb = ds.astype(BF16)
            dq_ref[:, sq] = (_dot(dsb, k2) * QK_SCALE).astype(BF16)
            db_ref[h] += ds
            if has_sink:
                psink = jnp.exp(s_ref[h] - lcol)
                tot = jnp.sum(psink * dcol, axis=0, keepdims=True)
                ds_ref[h:h + 1, :] -= jnp.broadcast_to(tot, (1, LANES))
            lncol = ln_ref[:, h:h + 1]
            dncol = dn_ref[:, h:h + 1]
            sn = _dot(qn_ref[:, sq], ktc_ref[sk, :]) + b_ref[h, :, 0:BLOCK]
            pn = jnp.where(mnext, jnp.exp(sn - lncol), 0.0)
            dsn = pn * (_dot(don_ref[:, sq], vtc_ref[sk, :]) - dncol)
            dkt_acc[g] = dkt_acc[g] + (_dot(qt_ref[sq, :], dsb[:, BLOCK:]) + _dot(qtn_ref[sq, :], dsn.astype(BF16)))
            dvt_acc[g] = dvt_acc[g] + (_dot(dot_ref[sq, :], p[:, BLOCK:].astype(BF16))
                                       + _dot(dotn_ref[sq, :], pn.astype(BF16)))
        for g in range(hk):
            sk = slice(g * HEAD_DIM, (g + 1) * HEAD_DIM)
            dkt_ref[sk, :] = dkt_acc[g].astype(BF16)
            dvt_ref[sk, :] = dvt_acc[g].astype(BF16)

    qcur = pl.BlockSpec((BLOCK, wq), lambda r, j: (j, r))
    qnext = pl.BlockSpec((BLOCK, wq), lambda r, j: (jnp.minimum(j + 1, nb - 1), r))
    qtcur = pl.BlockSpec((None, wq, BLOCK), lambda r, j: (j, r, 0))
    qtnext = pl.BlockSpec((None, wq, BLOCK), lambda r, j: (jnp.minimum(j + 1, nb - 1), r, 0))
    kprev = pl.BlockSpec((BLOCK, wk), lambda r, j: (jnp.maximum(j - 1, 0), r))
    kcur = pl.BlockSpec((BLOCK, wk), lambda r, j: (j, r))
    ktprev = pl.BlockSpec((None, wk, BLOCK), lambda r, j: (jnp.maximum(j - 1, 0), r, 0))
    ktcur = pl.BlockSpec((None, wk, BLOCK), lambda r, j: (j, r, 0))
    bspec = pl.BlockSpec((hq, BLOCK, 2 * BLOCK), lambda r, j: (0, 0, 0))
    hcur = pl.BlockSpec((None, BLOCK, hq), lambda r, j: (r, j, 0))
    hnext = pl.BlockSpec((None, BLOCK, hq), lambda r, j: (r, jnp.minimum(j + 1, nb - 1), 0))
    dob = do.astype(BF16)
    q2, k2, do2 = q.reshape(sd, dil * wq), k.reshape(sd, dil * wk), dob.reshape(sd, dil * wq)
    l3, d3 = _per_head_dilated(lse, dil), _per_head_dilated(dsum, dil)
    qt, vt, dot = _dilated_t(q, dil), _dilated_t(v, dil), _dilated_t(dob, dil)
    in_specs = [qcur, qnext, qtcur, qtnext, kprev, kcur, ktprev, ktcur, ktprev, ktcur, hcur, hnext, hcur, hnext,
                qcur, qnext, qtcur, qtnext, bspec, bspec]
    args = [q2, q2, qt, qt, k2, k2, kt, kt, vt, vt, l3, l3, d3, d3, do2, do2, dot, dot, bias, dbias_init]
    out_specs = [qcur, ktcur, ktcur, bspec]
    out_shape = [jax.ShapeDtypeStruct((sd, dil * wq), BF16), jax.ShapeDtypeStruct((nb, dil * wk, BLOCK), BF16),
                 jax.ShapeDtypeStruct((nb, dil * wk, BLOCK), BF16), jax.ShapeDtypeStruct((hq, BLOCK, 2 * BLOCK), F32)]
    if has_sink:
        in_specs.append(pl.BlockSpec(memory_space=pltpu.SMEM))
        args.append(sinks)
        out_specs.append(pl.BlockSpec((hq, LANES), lambda r, j: (0, 0)))
        out_shape.append(jax.ShapeDtypeStruct((hq, LANES), F32))
    res = pl.pallas_call(
        body, name=name, grid=(dil, nb), in_specs=in_specs, out_specs=out_specs, out_shape=out_shape,
        compiler_params=_params(("arbitrary", "arbitrary")),
    )(*args)
    dq, dk, dv, dbias = res[0].reshape(s, wq), _undilated(res[1], dil), _undilated(res[2], dil), res[3]
    return dq, dk, dv, dbias, (res[4][:, 0] if has_sink else None)


def _neg_softplus(z):
    return -(jnp.maximum(z, 0.0) + jnp.log(1.0 + jnp.exp(-jnp.abs(z))))


SB_CHUNK = 256
HEADS_PER_PAIR = LANES // HEAD_DIM


def _tri(kind):
    row = lax.broadcasted_iota(jnp.int32, (SB_CHUNK, SB_CHUNK), 0)
    col = lax.broadcasted_iota(jnp.int32, (SB_CHUNK, SB_CHUNK), 1)
    return {"ge": row >= col, "lt": row < col, "le": row <= col}[kind].astype(BF16)


def _keys_on_lanes(a, rows):
    s, w = a.shape
    return jnp.transpose(a.reshape(s // rows, rows, w), (0, 2, 1))


def _sb_mask(i, jj):
    row = lax.broadcasted_iota(jnp.int32, (BLOCK, SB_CHUNK), 0)
    col = lax.broadcasted_iota(jnp.int32, (BLOCK, SB_CHUNK), 1)
    return col < row + (i * BLOCK - jj * SB_CHUNK)


def _sb_trips(i):
    return (i * BLOCK) // (2 * SB_CHUNK) + 1


def _sb_rows(jj, n):
    return pl.ds(pl.multiple_of(jj * SB_CHUNK, SB_CHUNK), n * SB_CHUNK)


def _sb_fwd(q, kt, v, name):
    s, w = q.shape
    npair, nb, nc = w // LANES, s // BLOCK, s // SB_CHUNK

    def body(q_ref, kt_ref, v_ref, o_ref, t_ref):
        i = pl.program_id(1)
        lincl = _tri("ge")
        heads = [slice(hh * HEAD_DIM, (hh + 1) * HEAD_DIM) for hh in range(HEADS_PER_PAIR)]
        qs = [q_ref[:, sl] for sl in heads]

        def trip(t, carry, masked):
            lo, hi = 2 * t, 2 * t + 1
            mlo, mhi = (_sb_mask(i, lo), _sb_mask(i, hi)) if masked else (None, None)

            def keep(m, val):
                return val if m is None else jnp.where(m, val, 0.0)

            new = []
            for hh, sl in enumerate(heads):
                o_acc, rem = carry[hh]
                zhi = _dot(qs[hh], kt_ref[hi, sl, :])
                zlo = _dot(qs[hh], kt_ref[lo, sl, :])
                lrhi = keep(mhi, _neg_softplus(zhi))
                lrlo = keep(mlo, _neg_softplus(zlo))
                tothi = jnp.sum(lrhi, axis=-1, keepdims=True)
                ahi = keep(mhi, jnp.exp(zhi + (rem + _split_dot(lrhi, lincl))))
                alo = keep(mlo, jnp.exp(zlo + (rem + tothi + _split_dot(lrlo, lincl))))
                a = jnp.concatenate([alo, ahi], axis=1).astype(BF16)
                new.append((o_acc + _dot(a, v_ref[_sb_rows(lo, 2), sl]),
                            rem + tothi + jnp.sum(lrlo, axis=-1, keepdims=True)))
            return tuple(new)

        init = tuple((jnp.zeros((BLOCK, HEAD_DIM), F32), jnp.zeros((BLOCK, 1), F32)) for _ in heads)
        trips = _sb_trips(i)
        carry = trip(trips - 1, init, True)
        carry = lax.fori_loop(0, trips - 1, lambda t, cr: trip(trips - 2 - t, cr, False), carry)
        for hh, sl in enumerate(heads):
            o_ref[:, sl] = carry[hh][0]
            t_ref[:, sl] = jnp.broadcast_to(carry[hh][1], (BLOCK, HEAD_DIM))

    qspec = pl.BlockSpec((BLOCK, LANES), lambda p, i: (i, p))
    return pl.pallas_call(
        body, name=name, grid=(npair, nb),
        in_specs=[qspec, pl.BlockSpec((nc, LANES, SB_CHUNK), lambda p, i: (0, p, 0)),
                  pl.BlockSpec((s, LANES), lambda p, i: (0, p))],
        out_specs=[qspec, qspec], out_shape=[jax.ShapeDtypeStruct((s, w), F32)] * 2,
        compiler_params=_params(("parallel", "parallel")),
    )(q, kt, v)


def _sb_bwd(q, k, kt, v, tot, do, name):
    s, w = q.shape
    npair, nb, nc = w // LANES, s // BLOCK, s // SB_CHUNK
    dob = do.astype(BF16)

    def body(q_ref, qt_ref, k_ref, kt_ref, vt_ref, t_ref, do_ref, dot_ref, dq_ref, dkt_ref, dvt_ref):
        i = pl.program_id(1)

        @pl.when(i == 0)
        def _():
            dkt_ref[...] = jnp.zeros_like(dkt_ref)
            dvt_ref[...] = jnp.zeros_like(dvt_ref)

        lbefore = _tri("lt")
        lupto = _tri("le")
        heads = [slice(hh * HEAD_DIM, (hh + 1) * HEAD_DIM) for hh in range(HEADS_PER_PAIR)]
        qs = [q_ref[:, sl] for sl in heads]
        qts = [qt_ref[sl, :] for sl in heads]
        dos = [do_ref[:, sl] for sl in heads]
        dots = [dot_ref[sl, :] for sl in heads]
        totals = [t_ref[:, sl.start:sl.start + 1] for sl in heads]

        def trip(t, carry, masked):
            chunks = (2 * t, 2 * t + 1)
            masks = [_sb_mask(i, jj) if masked else None for jj in chunks]

            def keep(m, val):
                return val if m is None else jnp.where(m, val, 0.0)

            new = []
            for hh, sl in enumerate(heads):
                dq_acc, plr, pg = carry[hh]
                zs = [_dot(qs[hh], kt_ref[jj, sl, :]) for jj in chunks]
                lrs = [keep(m, _neg_softplus(z)) for m, z in zip(masks, zs)]
                lr_sums = [jnp.sum(lr, axis=-1, keepdims=True) for lr in lrs]
                before = [plr, plr + lr_sums[0]]
                avs = [keep(m, jnp.exp(z + (totals[hh] - (b + _split_dot(lr, lbefore)))))
                       for m, z, lr, b in zip(masks, zs, lrs, before)]
                gs = [_dot(dos[hh], vt_ref[jj, sl, :]) * a for jj, a in zip(chunks, avs)]
                g_sums = [jnp.sum(g, axis=-1, keepdims=True) for g in gs]
                upto = [pg, pg + g_sums[0]]
                dzs = [keep(m, g - jnp.exp(z + lr) * (u + _split_dot(g, lupto))).astype(BF16)
                       for m, z, lr, g, u in zip(masks, zs, lrs, gs, upto)]
                for jj, dzb, a in zip(chunks, dzs, avs):
                    dkt_ref[jj, sl, :] += _dot(qts[hh], dzb)
                    dvt_ref[jj, sl, :] += _dot(dots[hh], a.astype(BF16))
                dz2 = jnp.concatenate(dzs, axis=1)
                new.append((dq_acc + _dot(dz2, k_ref[_sb_rows(chunks[0], 2), sl]), plr + lr_sums[0] + lr_sums[1],
                            pg + g_sums[0] + g_sums[1]))
            return tuple(new)

        zero = jnp.zeros((BLOCK, 1), F32)
        init = tuple((jnp.zeros((BLOCK, HEAD_DIM), F32), zero, zero) for _ in heads)
        trips = _sb_trips(i)
        carry = lax.fori_loop(0, trips - 1, lambda t, cr: trip(t, cr, False), init)
        carry = trip(trips - 1, carry, True)
        for hh, sl in enumerate(heads):
            dq_ref[:, sl] = (carry[hh][0] * QK_SCALE).astype(BF16)

    qspec = pl.BlockSpec((BLOCK, LANES), lambda p, i: (i, p))
    qtspec = pl.BlockSpec((None, LANES, BLOCK), lambda p, i: (i, p, 0))
    kspec = pl.BlockSpec((s, LANES), lambda p, i: (0, p))
    ktspec = pl.BlockSpec((nc, LANES, SB_CHUNK), lambda p, i: (0, p, 0))
    dq, dkt, dvt = pl.pallas_call(
        body, name=name, grid=(npair, nb), in_specs=[qspec, qtspec, kspec, ktspec, ktspec, qspec, qspec, qtspec],
        out_specs=[qspec, ktspec, ktspec],
        out_shape=[jax.ShapeDtypeStruct((s, w), BF16)] + [jax.ShapeDtypeStruct((nc, w, SB_CHUNK), F32)] * 2,
        compiler_params=_params(("parallel", "arbitrary")),
    )(q, _keys_on_lanes(q, BLOCK), k, kt, _keys_on_lanes(v, SB_CHUNK), tot, dob, _keys_on_lanes(dob, BLOCK))

    def rows_first(t):
        return jnp.transpose(t, (0, 2, 1)).reshape(s, w)

    return dq, rows_first(dkt), rows_first(dvt)


def _group_norm(xv, g):
    r = lax.rsqrt(jnp.mean(xv * xv, axis=-1, keepdims=True) + EPS)
    return xv * r * g


def _head_spread(nheads):
    return jnp.asarray(np.repeat(np.eye(nheads, dtype=np.float32), HEAD_DIM, axis=1), dtype=BF16)


def _mix_fwd(oa, ob, ocs, lses, gain, cfg, name):
    s = oa.shape[0]
    ts = _div_tile(s, 512, 16)
    aq, bw, cw, nhc = cfg.a_q, cfg.b_w, cfg.c_w, cfg.nhc

    def body(oa_ref, ob_ref, c1, c2, c3, l1, l2, l3, sp_ref, g_ref, mix_ref, oc_ref, lse_ref):
        m = jnp.maximum(jnp.maximum(l1[...], l2[...]), l3[...])
        es = [jnp.exp(l[...] - m) for l in (l1, l2, l3)]
        den = es[0] + es[1] + es[2]
        oc = sum(_split_dot(e / den, sp_ref[...]) * c[...] for e, c in zip(es, (c1, c2, c3)))
        oc_ref[...] = oc
        lse_ref[...] = m + jnp.log(den)
        mix_ref[:, 0:aq] = _group_norm(oa_ref[...], g_ref[:, 0:aq]).astype(BF16)
        mix_ref[:, aq:aq + bw] = _group_norm(ob_ref[...], g_ref[:, aq:aq + bw]).astype(BF16)
        mix_ref[:, aq + bw:] = _group_norm(oc, g_ref[:, aq + bw:]).astype(BF16)

    def row(wd):
        return pl.BlockSpec((ts, wd), lambda i: (i, 0))

    return pl.pallas_call(
        body, name=name, grid=(s // ts,),
        in_specs=[row(aq), row(bw)] + [row(cw)] * 3 + [row(nhc)] * 3
        + [pl.BlockSpec((nhc, cw), lambda i: (0, 0)), pl.BlockSpec((1, cfg.d), lambda i: (0, 0))],
        out_specs=[row(cfg.d), row(cw), row(nhc)],
        out_shape=[jax.ShapeDtypeStruct((s, cfg.d), BF16), jax.ShapeDtypeStruct((s, cw), F32),
                   jax.ShapeDtypeStruct((s, nhc), F32)],
        compiler_params=_params(("parallel",)),
    )(oa, ob, *ocs, *lses, _head_spread(nhc), gain)


def _mix_bwd(dmix, oa, ob, oc, gain, cfg, name):
    s = oa.shape[0]
    ts = _div_tile(s, 256, 8)
    aq, bw, cw = cfg.a_q, cfg.b_w, cfg.c_w

    def body(dm_ref, oa_ref, ob_ref, oc_ref, g_ref, fa_ref, fc_ref, da_ref, db_ref, dc_ref, dg_ref, sa_ref, sc_ref):
        @pl.when(pl.program_id(0) == 0)
        def _():
            dg_ref[...] = jnp.zeros_like(dg_ref)

        for x_ref, dx_ref, lo, hi, fold in ((oa_ref, da_ref, 0, aq, (fa_ref, sa_ref)), (ob_ref, db_ref, aq, aq + bw, None),
                                            (oc_ref, dc_ref, aq + bw, aq + bw + cw, (fc_ref, sc_ref))):
            xv = x_ref[...]
            dy = dm_ref[:, lo:hi]
            r = lax.rsqrt(jnp.mean(xv * xv, axis=-1, keepdims=True) + EPS)
            xhat = xv * r
            dxhat = dy * g_ref[:, lo:hi]
            dx = r * (dxhat - xhat * jnp.mean(dxhat * xhat, axis=-1, keepdims=True))
            dx_ref[...] = dx
            dg_ref[:, lo:hi] += jnp.sum(dy * xhat, axis=0, keepdims=True)
            if fold is not None:
                fold[1][...] = _split_dot(dx * xv, fold[0][...])

    def row(wd):
        return pl.BlockSpec((ts, wd), lambda i: (i, 0))

    vec = pl.BlockSpec((1, cfg.d), lambda i: (0, 0))
    return pl.pallas_call(
        body, name=name, grid=(s // ts,),
        in_specs=[row(cfg.d), row(aq), row(bw), row(cw), vec, pl.BlockSpec((aq, cfg.nha), lambda i: (0, 0)),
                  pl.BlockSpec((cw, cfg.nhc), lambda i: (0, 0))],
        out_specs=[row(aq), row(bw), row(cw), vec, row(cfg.nha), row(cfg.nhc)],
        out_shape=[jax.ShapeDtypeStruct((s, aq), F32), jax.ShapeDtypeStruct((s, bw), F32),
                   jax.ShapeDtypeStruct((s, cw), F32), jax.ShapeDtypeStruct((1, cfg.d), F32),
                   jax.ShapeDtypeStruct((s, cfg.nha), F32), jax.ShapeDtypeStruct((s, cfg.nhc), F32)],
        compiler_params=_params(("arbitrary",)),
    )(dmix, oa, ob, oc, gain, _head_spread(cfg.nha).T, _head_spread(cfg.nhc).T)


def _bias_table_grad(dbiases, buckets, name):
    outs = []
    for idx, (db, bk) in enumerate(zip(dbiases, buckets)):
        h = db.shape[0]

        def body(db_ref, bk_ref, o_ref):
            xv = db_ref[0]
            ids = bk_ref[...]
            lane = lax.broadcasted_iota(jnp.int32, (1, LANES), 1)
            acc = jnp.zeros((1, LANES), F32)
            for b in range(N_BUCKETS):
                tot = jnp.sum(jnp.where(ids == b, xv, 0.0), axis=0, keepdims=True)
                tot = jnp.sum(tot, axis=1, keepdims=True)
                acc = jnp.where(lane == b, tot, acc)
            o_ref[0] = acc

        outs.append(pl.pallas_call(
            body, name=f"{name}_{idx}", grid=(h,),
            in_specs=[pl.BlockSpec((1, BLOCK, 2 * BLOCK), lambda i: (i, 0, 0)),
                      pl.BlockSpec((BLOCK, 2 * BLOCK), lambda i: (0, 0))],
            out_specs=pl.BlockSpec((1, 1, LANES), lambda i: (i, 0, 0)),
            out_shape=jax.ShapeDtypeStruct((h, 1, LANES), F32), compiler_params=_params(("parallel",)),
        )(db, bk)[:, 0, :])
    return outs


SUBLANES = 8


def _shift_down(u, n, rows):
    r = pltpu.roll(u, n, 0)
    return jnp.concatenate([jnp.where(rows[:SUBLANES] >= n, r[:SUBLANES], 0.0), r[SUBLANES:]], axis=0)


def _shift_up(u, n, rows, s):
    r = pltpu.roll(u, s - n, 0)
    return jnp.concatenate([r[:s - SUBLANES], jnp.where(rows[s - SUBLANES:] < s - n, r[s - SUBLANES:], 0.0)], axis=0)


def _conv(u, w_ref, b_ref, rows):
    return (b_ref[...] + w_ref[0:1, :] * _shift_down(u, 2, rows) + w_ref[1:2, :] * _shift_down(u, 1, rows)
            + w_ref[2:3, :] * u)


def _conv_act_fwd(u, conv_w, conv_b, f, name):
    s = u.shape[0]
    nf = f // LANES

    def body(ug_ref, uu_ref, wg_ref, wu_ref, bg_ref, bu_ref, act_ref):
        rows = lax.broadcasted_iota(jnp.int32, (s, LANES), 0)
        gate = _conv(ug_ref[...], wg_ref, bg_ref, rows)
        up = _conv(uu_ref[...], wu_ref, bu_ref, rows)
        act_ref[...] = (gate * jax.nn.sigmoid(gate) * up).astype(BF16)

    def col(rws, off):
        return pl.BlockSpec((rws, LANES), lambda j: (0, j + off))

    return pl.pallas_call(
        body, name=name, grid=(nf,),
        in_specs=[col(s, 0), col(s, nf), col(CONV_WIDTH, 0), col(CONV_WIDTH, nf), col(1, 0), col(1, nf)],
        out_specs=col(s, 0), out_shape=jax.ShapeDtypeStruct((s, f), BF16), compiler_params=_params(("parallel",)),
    )(u, u, conv_w, conv_w, conv_b, conv_b)


def _conv_act_bwd(u, dact, conv_w, conv_b, f, name):
    s = u.shape[0]
    nf = f // LANES

    def body(ug_ref, uu_ref, da_ref, wg_ref, wu_ref, bg_ref, bu_ref, dug_ref, duu_ref, dwg_ref, dwu_ref, dbg_ref,
             dbu_ref):
        rows = lax.broadcasted_iota(jnp.int32, (s, LANES), 0)
        ug, uu = ug_ref[...], uu_ref[...]
        gate = _conv(ug, wg_ref, bg_ref, rows)
        up = _conv(uu, wu_ref, bu_ref, rows)
        sg = jax.nn.sigmoid(gate)
        da = da_ref[...]
        dgate = da * up * (sg * (1.0 + gate * (1.0 - sg)))
        dup = da * (gate * sg)
        for du, uv, w_ref, du_ref, dw_ref, db_ref in ((dgate, ug, wg_ref, dug_ref, dwg_ref, dbg_ref),
                                                     (dup, uu, wu_ref, duu_ref, dwu_ref, dbu_ref)):
            du_ref[...] = (w_ref[2:3, :] * du + w_ref[1:2, :] * _shift_up(du, 1, rows, s)
                           + w_ref[0:1, :] * _shift_up(du, 2, rows, s)).astype(BF16)
            dw_ref[0:1, :] = jnp.sum(du * _shift_down(uv, 2, rows), axis=0, keepdims=True)
            dw_ref[1:2, :] = jnp.sum(du * _shift_down(uv, 1, rows), axis=0, keepdims=True)
            dw_ref[2:3, :] = jnp.sum(du * uv, axis=0, keepdims=True)
            db_ref[...] = jnp.sum(du, axis=0, keepdims=True)

    def col(rws, off):
        return pl.BlockSpec((rws, LANES), lambda j: (0, j + off))

    return pl.pallas_call(
        body, name=name, grid=(nf,),
        in_specs=[col(s, 0), col(s, nf), col(s, 0), col(CONV_WIDTH, 0), col(CONV_WIDTH, nf), col(1, 0), col(1, nf)],
        out_specs=[col(s, 0), col(s, 0), col(CONV_WIDTH, 0), col(CONV_WIDTH, 0), col(1, 0), col(1, 0)],
        out_shape=[jax.ShapeDtypeStruct((s, f), BF16)] * 2 + [jax.ShapeDtypeStruct((CONV_WIDTH, f), F32)] * 2
        + [jax.ShapeDtypeStruct((1, f), F32)] * 2,
        compiler_params=_params(("parallel",)),
    )(u, u, dact, conv_w, conv_w, conv_b, conv_b)


def _loss_head(y, target, name):
    s, d = y.shape
    ts = _div_tile(s, 256, 16)

    def body(y_ref, t_ref, dy_ref, dyb_ref, l_ref):
        @pl.when(pl.program_id(0) == 0)
        def _():
            l_ref[...] = jnp.zeros_like(l_ref)

        err = y_ref[...] - t_ref[...]
        dy = err * (1.0 / d)
        dy_ref[...] = dy
        dyb_ref[...] = dy.astype(BF16)
        tot = jnp.sum(jnp.sum(err * err, axis=0, keepdims=True), axis=1, keepdims=True) * (0.5 / d)
        l_ref[...] += jnp.broadcast_to(tot, l_ref.shape)

    row = pl.BlockSpec((ts, d), lambda i: (i, 0))
    return pl.pallas_call(
        body, name=name, grid=(s // ts,), in_specs=[row, row],
        out_specs=[row, row, pl.BlockSpec((8, LANES), lambda i: (0, 0))],
        out_shape=[jax.ShapeDtypeStruct((s, d), F32), jax.ShapeDtypeStruct((s, d), BF16),
                   jax.ShapeDtypeStruct((8, LANES), F32)],
        compiler_params=_params(("arbitrary",)),
    )(y, target)


def _adamw(w, g, m, v, name):
    r, c = w.shape
    tr = _div_tile(r, max(8, (1 << 18) // c // 8 * 8), 8)
    c1 = 1.0 - ADAM_B1 ** ADAM_STEP
    c2 = 1.0 - ADAM_B2 ** ADAM_STEP

    def body(w_ref, g_ref, m_ref, v_ref, d_ref, nm_ref, nv_ref):
        gv = g_ref[...]
        nm = ADAM_B1 * m_ref[...] + (1.0 - ADAM_B1) * gv
        nv = ADAM_B2 * v_ref[...] + (1.0 - ADAM_B2) * (gv * gv)
        d_ref[...] = -ADAM_LR * ((nm / c1) / (jnp.sqrt(nv / c2) + ADAM_EPS) + ADAM_WD * w_ref[...])
        nm_ref[...] = nm
        nv_ref[...] = nv

    spec = pl.BlockSpec((tr, c), lambda i: (i, 0))
    return pl.pallas_call(
        body, name=name, grid=(r // tr,), in_specs=[spec] * 4, out_specs=[spec] * 3,
        out_shape=[jax.ShapeDtypeStruct((r, c), F32)] * 3, compiler_params=_params(("parallel",)),
    )(w, g, m, v)


def _adamw_layer(layer, w, g, m, v, bufs, name):
    depth, r, c = w.shape
    tr = _div_tile(r, max(8, (1 << 19) // c // 8 * 8), 8)
    c1 = 1.0 - ADAM_B1 ** ADAM_STEP
    c2 = 1.0 - ADAM_B2 ** ADAM_STEP

    def body(*refs):
        w_ref, g_ref, m_ref, v_ref = refs[:4]
        go_ref, d_ref, nm_ref, nv_ref = refs[-4:]
        gv = g_ref[...]
        nm = ADAM_B1 * m_ref[...] + (1.0 - ADAM_B1) * gv
        nv = ADAM_B2 * v_ref[...] + (1.0 - ADAM_B2) * (gv * gv)
        d_ref[...] = -ADAM_LR * ((nm / c1) / (jnp.sqrt(nv / c2) + ADAM_EPS) + ADAM_WD * w_ref[...])
        nm_ref[...] = nm
        nv_ref[...] = nv
        go_ref[...] = gv

    lay = pl.BlockSpec((None, tr, c), lambda i: (layer, i, 0))
    in_specs = [lay, pl.BlockSpec((tr, c), lambda i: (i, 0)), lay, lay]
    args = [w, g, m, v]
    aliases = {}
    if bufs is not None:
        in_specs += [pl.BlockSpec(memory_space=pl.ANY)] * 4
        args += list(bufs)
        aliases = {4 + k: k for k in range(4)}
    return pl.pallas_call(
        body, name=name, grid=(r // tr,), in_specs=in_specs, out_specs=[lay] * 4,
        out_shape=[jax.ShapeDtypeStruct((depth, r, c), F32)] * 4, input_output_aliases=aliases,
        compiler_params=_params(("parallel",)),
    )(*args)


def _mesh_pos():
    return lax.axis_index("x"), lax.axis_index("y"), lax.axis_index("c")


def _flip(v, bit):
    return 1 - v if bit else v


def _sum_parts(parts, name):
    _, r, c = parts.shape
    tr = _div_tile(r, 256, 16)

    def body(p_ref, o_ref):
        acc = p_ref[0].astype(F32)
        for src in range(1, N_DEVICES):
            acc = acc + p_ref[src].astype(F32)
        o_ref[...] = acc

    return pl.pallas_call(
        body, name=name, grid=(r // tr,), in_specs=[pl.BlockSpec((N_DEVICES, tr, c), lambda i: (0, i, 0))],
        out_specs=pl.BlockSpec((tr, c), lambda i: (i, 0)), out_shape=jax.ShapeDtypeStruct((r, c), F32),
        compiler_params=_params(("parallel",)),
    )(parts)


def _split_start(srcs, lands, plan, ncopies, name):
    nbuf = len(srcs) + len(lands)

    def body(*refs):
        bufs = refs[:nbuf]
        send_sem, recv_sem, token = refs[nbuf], refs[nbuf + 1], refs[-1]
        for k, (src, dst, dev) in enumerate(plan(bufs[:len(srcs)], bufs[len(srcs):])):
            pltpu.make_async_remote_copy(src_ref=src, dst_ref=dst, send_sem=send_sem.at[k], recv_sem=recv_sem.at[k],
                                         device_id=dev, device_id_type=MESH).start()
        token[...] = jnp.zeros_like(token)

    hbm = pl.BlockSpec(memory_space=pltpu.HBM)
    sem = pl.BlockSpec(memory_space=pltpu.SEMAPHORE)
    operands = [pltpu.with_memory_space_constraint(a, pltpu.HBM) for a in (*srcs, *lands)]
    outs = pl.pallas_call(
        body, name=name, in_specs=[hbm] * nbuf,
        out_specs=(sem, sem, *[hbm] * nbuf, pl.BlockSpec(memory_space=pltpu.VMEM)),
        out_shape=(pltpu.SemaphoreType.DMA((ncopies,)), pltpu.SemaphoreType.DMA((ncopies,)),
                   *[pltpu.HBM(a.shape, a.dtype) for a in operands], jax.ShapeDtypeStruct((8, LANES), F32)),
        input_output_aliases={i: 2 + i for i in range(nbuf)},
        compiler_params=pltpu.CompilerParams(has_side_effects=pltpu.SideEffectType.DATAFLOW_SIDE_EFFECTING),
    )(*operands)
    handle = dict(send=outs[0], recv=outs[1], bufs=list(outs[2:2 + nbuf]), nsrc=len(srcs), plan=plan)
    return handle, outs[-1]


def _split_wait(handle, after, name):
    nbuf, nsrc, plan = len(handle["bufs"]), handle["nsrc"], handle["plan"]

    def body(*refs):
        bufs = refs[:nbuf]
        send_sem, recv_sem = refs[nbuf], refs[nbuf + 1]
        for k, (src, dst, dev) in enumerate(plan(bufs[:nsrc], bufs[nsrc:])):
            copy = pltpu.make_async_remote_copy(src_ref=src, dst_ref=dst, send_sem=send_sem.at[k],
                                                recv_sem=recv_sem.at[k], device_id=dev, device_id_type=MESH)
            copy.wait_send()
            copy.wait_recv()

    hbm = pl.BlockSpec(memory_space=pltpu.HBM)
    sem = pl.BlockSpec(memory_space=pltpu.SEMAPHORE)
    outs = pl.pallas_call(
        body, name=name, in_specs=[hbm] * nbuf + [sem, sem, pl.BlockSpec(memory_space=pl.ANY)],
        out_specs=[hbm] * nbuf, out_shape=[pltpu.HBM(a.shape, a.dtype) for a in handle["bufs"]],
        input_output_aliases={i: i for i in range(nbuf)},
        compiler_params=pltpu.CompilerParams(has_side_effects=pltpu.SideEffectType.DATAFLOW_SIDE_EFFECTING),
    )(*handle["bufs"], handle["send"], handle["recv"], after)
    return list(outs[nsrc:])


def _own_slot(shape, dtype, block, index):
    return lax.dynamic_update_slice(lax.empty(shape, dtype), block[None], (index,) + (0,) * block.ndim)


def _gather_plan(srcs, lands):
    x, y, c = _mesh_pos()
    return [(land.at[2 * x + y], land.at[2 * x + y], (*chip, c))
            for land in lands for chip in ((1 - x, y), (x, 1 - y), (1 - x, 1 - y))]


def _scatter_plan(srcs, lands):
    x, y, c = _mesh_pos()
    out = []
    for src, land in zip(srcs, lands):
        half = src.shape[1] // 2
        for d in range(1, N_DEVICES):
            p = (_flip(x, d & 4), _flip(y, d & 2), _flip(c, d & 1))
            out.append((src.at[2 * p[0] + p[1], pl.ds(p[2] * half, half), :], land.at[4 * x + 2 * y + c], p))
    return out


def _swap_plan(srcs, lands):
    x, y, c = _mesh_pos()
    return [(src, land.at[c], (x, y, 1 - c)) for src, land in zip(srcs, lands)]


class _Gathered:
    def __init__(self, groups):
        self.groups = groups
        self.ready = {}

    def get(self, name, after=None):
        if name not in self.ready:
            handle, names, wait_name = next(g for g in self.groups if name in g[1])
            for n, full in zip(names, _split_wait(handle, after, wait_name)):
                self.ready[n] = full.reshape(-1, full.shape[-1])
        return self.ready[name]


def _allgather_plan(srcs, lands):
    x, y, c = _mesh_pos()
    mine = lands[0].at[4 * x + 2 * y + c]
    return [(mine, mine, (_flip(x, d & 4), _flip(y, d & 2), _flip(c, d & 1))) for d in range(1, N_DEVICES)]


def _sum_slots(slots, name):
    r = slots.shape[1]

    def body(s_ref, o_ref):
        acc = s_ref[0]
        for src in range(1, N_DEVICES):
            acc = acc + s_ref[src]
        o_ref[...] = acc

    vm = pl.BlockSpec(memory_space=pltpu.VMEM)
    return pl.pallas_call(
        body, name=name, in_specs=[vm], out_specs=vm, out_shape=jax.ShapeDtypeStruct((r, LANES), F32),
        compiler_params=pltpu.CompilerParams(vmem_limit_bytes=VMEM_LIMIT_BYTES),
    )(slots)


def _bucket_ids(dil):
    rel = (np.arange(BLOCK)[:, None] + BLOCK - np.arange(2 * BLOCK)[None, :]) * dil
    max_exact = N_BUCKETS // 2
    d = np.maximum(rel, 0)
    large = max_exact + (np.log(np.maximum(d, 1).astype(np.float32) / max_exact)
                         / np.float32(np.log(T5_MAX_DIST / max_exact)) * (N_BUCKETS - max_exact)).astype(np.int32)
    large = np.minimum(large, N_BUCKETS - 1)
    return np.where(d < max_exact, d, large).astype(np.int32)


def _block_bias(table, dil):
    onehot = (jnp.asarray(_bucket_ids(dil))[:, :, None] == jnp.arange(N_BUCKETS)[None, None, :]).astype(F32)
    return jnp.einsum("ijb,bh->hij", onehot, table.astype(F32), precision=lax.Precision.HIGHEST)


def _tile_gain(g, n):
    return jnp.tile(g.reshape(1, HEAD_DIM), (1, n))


def _layer_fwd(x, p, cfg):
    w = p["weights"]
    h1 = _rmsnorm_fwd(x, p["attn_norm"], "attn_norm_fwd")
    proj = _matmul(h1, w.get("w_in_t", h1), "nt", F32, "in_proj", tm=1024, tn=896, tk=2048)
    aq, ak, av, bq, bk, bv, cq, ck, cv = _qk_prep(proj, p["gains"], cfg, "qk_prep")
    akt = _dilated_t(ak, 1)
    oa, lse_a = _banded_fwd(aq, akt, av, p["bias_a"], p["sinks"], cfg.nha, cfg.nkva, WINDOW_A - 1, 1, "swa_fwd")
    bkt = _keys_on_lanes(bk, SB_CHUNK)
    ob, tot_b = _sb_fwd(bq, bkt, bv, "stickbreak_fwd")
    ocs, lses, ckts = [], [], []
    for (window, dil), bias in zip(DILATED_PAIRS, p["bias_c"]):
        ckts.append(_dilated_t(ck, dil))
        o, l = _banded_fwd(cq, ckts[-1], cv, bias, None, cfg.nhc, cfg.nhc, window // dil, dil, f"dilated{dil}_fwd")
        ocs.append(o)
        lses.append(l)
    mix, oc, lse_c = _mix_fwd(oa, ob, ocs, lses, p["mix_gain"], cfg, "mix_fwd")
    xm = _matmul(mix, w.get("w_out", mix), "nn", F32, "out_proj", tm=1024, tn=1024, tk=2048, residual=x)
    h2 = _rmsnorm_fwd(xm, p["ffn_norm"], "ffn_norm_fwd")
    u = _matmul(h2, w.get("w_up_t", h2), "nt", F32, "up_proj", tm=1024, tn=1024, tk=2048)
    act = _conv_act_fwd(u, p["conv_w"], p["conv_b"], cfg.f, "conv_act_fwd")
    y = _matmul(act, w.get("w_down", act), "nn", F32, "down_proj", tm=1024, tn=1024, tk=1408, residual=xm)
    saved = dict(x=x, h1=h1, proj=proj, q=(aq, ak, av, bq, bk, bv, cq, ck, cv), oa=oa, lse_a=lse_a, ob=ob,
                 tot_b=tot_b, akt=akt, bkt=bkt, ckts=ckts, oc=oc, lse_c=lse_c, mix=mix, xm=xm, h2=h2, u=u, act=act)
    return y, saved


def _layer_bwd(dy, dyb, sv, p, dbias, cfg, on_grad):
    aq, ak, av, bq, bk, bv, cq, ck, cv = sv["q"]
    w = p["weights"]
    anchor = on_grad(_matmul(sv["act"], dyb, "tn", BF16, "down_proj_dw", tm=1408, tn=2048, tk=1024))
    dact = _matmul(dyb, w.get("w_down"), "nt", F32, "down_proj_dx", tm=1024, tn=1408, tk=2048)
    dug, duu, dwg, dwu, dbg, dbu = _conv_act_bwd(sv["u"], dact, p["conv_w"], p["conv_b"] + anchor, cfg.f,
                                                 "conv_act_bwd")
    du = jnp.concatenate([dug, duu], axis=1)
    anchor = on_grad(_matmul(du, sv["h2"], "tn", BF16, "up_proj_dw", tm=1408, tn=2048, tk=1024))
    dh2 = _matmul(du, w.get("w_up_t"), "nn", F32, "up_proj_dx", tm=1024, tn=2048, tk=1024)
    dxm, dxmb, g_ffn_norm = _rmsnorm_bwd(sv["xm"], p["ffn_norm"] + anchor, dh2, dy, "ffn_norm_bwd")
    anchor = on_grad(_matmul(sv["mix"], dxmb, "tn", BF16, "out_proj_dw", tm=1024, tn=2048, tk=1024))
    dmix = _matmul(dxmb, w.get("w_out"), "nt", F32, "out_proj_dx", tm=1024, tn=1024, tk=2048)
    doa, dob, doc, g_mix_gain, dsum_a, dsum_c = _mix_bwd(dmix, sv["oa"], sv["ob"], sv["oc"], p["mix_gain"] + anchor,
                                                         cfg, "mix_bwd")
    daq, dak, dav, dbias_a, g_sinks = _banded_bwd(aq, ak, sv["akt"], av, sv["lse_a"], dsum_a, doa, p["bias_a"],
                                                 p["sinks"], dbias[0], cfg.nha, cfg.nkva, WINDOW_A - 1, 1, "swa_bwd")
    dbq, dbk, dbv = _sb_bwd(bq, bk, sv["bkt"], bv, sv["tot_b"], dob, "stickbreak_bwd")
    dcq, dck, dcv, dbias_c = [], [], [], []
    for idx, ((window, dil), bias) in enumerate(zip(DILATED_PAIRS, p["bias_c"])):
        a, b, c, d, _ = _banded_bwd(cq, ck, sv["ckts"][idx], cv, sv["lse_c"], dsum_c, doc, bias, None, dbias[1][idx],
                                    cfg.nhc, cfg.nhc, window // dil, dil, f"dilated{dil}_bwd")
        dcq.append(a)
        dck.append(b)
        dcv.append(c)
        dbias_c.append(d)
    dproj, g_aq, g_ak, g_cq, g_ck = _qk_prep_bwd(
        sv["proj"], p["gains"], [[daq], [dak], [dav], [dbq], [dbk], [dbv], dcq, dck, dcv], cfg, "qk_prep_bwd")
    anchor = on_grad(_matmul(dproj, sv["h1"], "tn", BF16, "in_proj_dw", tm=768, tn=2048, tk=1024))
    dh1 = _matmul(dproj, w.get("w_in_t"), "nn", F32, "in_proj_dx", tm=1024, tn=2048, tk=768)
    dx, dxb, g_attn_norm = _rmsnorm_bwd(sv["x"], p["attn_norm"] + anchor, dh1, dxm, "attn_norm_bwd")

    def fold(g):
        return jnp.sum(g.reshape(-1, HEAD_DIM), axis=0)

    small = dict(attn_norm=g_attn_norm[0], a_q_gain=fold(g_aq), a_k_gain=fold(g_ak), a_sinks=g_sinks,
                 c_q_gain=fold(g_cq), c_k_gain=fold(g_ck), mix_out_gain=g_mix_gain[0], ffn_norm=g_ffn_norm[0],
                 conv_w=jnp.concatenate([dwg, dwu], axis=1), conv_b=jnp.concatenate([dbg, dbu], axis=1)[0])
    return dx, dxb, small, (dbias_a, dbias_c)


_SMALL = ("attn_norm", "a_q_gain", "a_k_gain", "a_sinks", "c_q_gain", "c_k_gain", "rel_bias_table", "mix_out_gain",
          "ffn_norm", "conv_w", "conv_b")


def _pack(arrays):
    flat = jnp.concatenate([a.reshape(-1).astype(F32) for a in arrays])
    pad = (-flat.shape[0]) % (8 * LANES)
    return jnp.pad(flat, (0, pad)).reshape(-1, LANES)


def _unpack(flat, shapes):
    flat = flat.reshape(-1)
    out, pos = [], 0
    for sh in shapes:
        n = int(np.prod(sh))
        out.append(flat[pos:pos + n].reshape(sh))
        pos += n
    return out


def kernel(x, attn_norm, w_in, a_q_gain, a_k_gain, a_sinks, c_q_gain, c_k_gain, rel_bias_table, mix_out_gain, w_out, ffn_norm, w_up, conv_w, conv_b, w_down, loss_target, m_attn_norm, m_w_in, m_a_q_gain, m_a_k_gain, m_a_sinks, m_c_q_gain, m_c_k_gain, m_rel_bias_table, m_mix_out_gain, m_w_out, m_ffn_norm, m_w_up, m_conv_w, m_conv_b, m_w_down, v_attn_norm, v_w_in, v_a_q_gain, v_a_k_gain, v_a_sinks, v_c_q_gain, v_c_k_gain, v_rel_bias_table, v_mix_out_gain, v_w_out, v_ffn_norm, v_w_up, v_conv_w, v_conv_b, v_w_down):
    depth, d = attn_norm.shape
    f = w_down.shape[1] * N_CHIPS
    cfg = _Cfg(d, f)
    chip = 2 * lax.axis_index("x") + lax.axis_index("y")

    cw_cols = conv_w.shape[2]
    cw_flat = conv_w.reshape(-1)
    cw_rows = -(-cw_flat.shape[0] // (16 * LANES)) * 16
    cw_pad = jnp.pad(cw_flat, (0, cw_rows * LANES - cw_flat.shape[0])).reshape(cw_rows, LANES)

    table_a, table_c = rel_bias_table[:, :cfg.nha], rel_bias_table[:, cfg.nha:]
    bias_a = _block_bias(table_a, 1)
    bias_c = [_block_bias(table_c, dil) for _, dil in DILATED_PAIRS]

    layers, anchor = [], 0.0
    for l in range(depth):
        shards = [w_in[l].T.astype(BF16), w_out[l].astype(BF16), w_up[l].T.astype(BF16), w_down[l].astype(BF16)]
        names = ["w_in_t", "w_out", "w_up_t", "w_down"]
        if l == 0:
            todo = [([cw_pad, shards[0]], ["conv_w", names[0]])] + [([s], [n]) for s, n in zip(shards[1:], names[1:])]
        else:
            todo = [(shards, names)]
        groups = []
        for k, (srcs, group_names) in enumerate(todo):
            lands = [_own_slot((N_CHIPS,) + s.shape, s.dtype, s, chip) for s in srcs]
            handle, token = _split_start([], lands, _gather_plan, 3 * len(lands), f"gather_start_{l}_{k}")
            anchor = anchor + token[0, 0]
            groups.append((handle, group_names, f"gather_wait_{l}_{k}"))
        layers.append(dict(
            attn_norm=attn_norm[l].reshape(1, d), ffn_norm=ffn_norm[l].reshape(1, d),
            mix_gain=mix_out_gain[l].reshape(1, d),
            gains=(_tile_gain(a_q_gain[l], cfg.nha), _tile_gain(a_k_gain[l], cfg.nkva),
                   _tile_gain(c_q_gain[l], cfg.nhc), _tile_gain(c_k_gain[l], cfg.nhc)),
            sinks=a_sinks[l], bias_a=bias_a, bias_c=bias_c, conv_b=conv_b[l].reshape(1, 2 * f),
            weights=_Gathered(groups)))
    cw_all = layers[0]["weights"].get("conv_w", layers[0]["attn_norm"] + anchor)
    cw_all = cw_all.reshape(N_CHIPS, -1)[:, :cw_flat.shape[0]].reshape(N_CHIPS, depth, CONV_WIDTH, cw_cols)
    conv_w_full = jnp.transpose(cw_all, (1, 2, 0, 3)).reshape(depth, CONV_WIDTH, N_CHIPS * cw_cols)
    for l in range(depth):
        layers[l]["conv_w"] = conv_w_full[l]

    act = x[0]
    saved = []
    for l in range(depth):
        act, sv = _layer_fwd(act, layers[l], cfg)
        saved.append(sv)
    dact, dactb, loss_blk = _loss_head(act, loss_target[0], "loss_head")
    loss = lax.psum(loss_blk[0, 0], ("x", "y", "c"))

    core = lax.axis_index("c")

    def start_scatter(grads, name):
        srcs = [g.reshape(N_CHIPS, -1, g.shape[-1]) for g in grads]
        lands = []
        for g in srcs:
            half = g.shape[1] // 2
            own = lax.dynamic_slice(g, (chip, core * half, 0), (1, half, g.shape[2]))[0]
            lands.append(_own_slot((N_DEVICES, half, g.shape[2]), g.dtype, own, 2 * chip + core))
        return _split_start(srcs, lands, _scatter_plan, (N_DEVICES - 1) * len(srcs), name)

    def finish_scatter(l, handles, after):
        parts = [pt for k, h in enumerate(handles) for pt in _split_wait(h, after, f"scatter_wait_{l}_{k}")][::-1]
        halves = [_sum_parts(pt, f"sum_grads_{t}") for t, pt in enumerate(parts)]
        lands = [_own_slot((2,) + h.shape, h.dtype, h, core) for h in halves]
        return _split_start(halves, lands, _swap_plan, len(halves), f"swap_start_{l}")[0]

    dbias = (jnp.zeros_like(bias_a), [jnp.zeros_like(b) for b in bias_c])
    small_grads = [None] * depth
    swaps = [None] * depth
    pending = None
    for l in reversed(range(depth)):
        made = []

        def on_grad(g, l=l, made=made):
            if l:
                made.append(g)
                return 0.0
            handle, token = start_scatter([g], f"scatter_start_0_{len(made)}")
            made.append(handle)
            return token[0, 0]

        dact, dactb, small_grads[l], dbias = _layer_bwd(dact, dactb, saved[l], layers[l], dbias, cfg, on_grad)
        if pending is not None:
            swaps[l + 1] = finish_scatter(l + 1, pending, dact)
        if l:
            handle, token = start_scatter(made, f"scatter_start_{l}")
            pending = [handle]
            layers[l - 1]["conv_b"] = layers[l - 1]["conv_b"] + token[0, 0]
        else:
            pending = made
    grad_x = dact[None]

    tabs = _bias_table_grad([dbias[0]] + dbias[1], [jnp.asarray(_bucket_ids(1))]
                            + [jnp.asarray(_bucket_ids(dil)) for _, dil in DILATED_PAIRS], "bias_table_grad")
    g_table_a = tabs[0][:, :N_BUCKETS].T
    g_table_c = (tabs[1] + tabs[2] + tabs[3])[:, :N_BUCKETS].T
    g_table = jnp.concatenate([g_table_a, g_table_c], axis=1)
    small_local = {k: jnp.stack([small_grads[l][k] for l in range(depth)]) for k in _SMALL if k != "rel_bias_table"}
    small_local["rel_bias_table"] = g_table
    shapes = [small_local[k].shape for k in _SMALL]
    packed = _pack([small_local[k] for k in _SMALL])
    small_handle, small_token = _split_start([], [_own_slot((N_DEVICES,) + packed.shape, F32, packed, 2 * chip + core)],
                                             _allgather_plan, N_DEVICES - 1, "small_start")

    given = dict(attn_norm=attn_norm, a_q_gain=a_q_gain, a_k_gain=a_k_gain, a_sinks=a_sinks, c_q_gain=c_q_gain,
                 c_k_gain=c_k_gain, rel_bias_table=rel_bias_table, mix_out_gain=mix_out_gain, ffn_norm=ffn_norm,
                 conv_w=conv_w, conv_b=conv_b)
    moms = dict(attn_norm=(m_attn_norm, v_attn_norm), a_q_gain=(m_a_q_gain, v_a_q_gain),
                a_k_gain=(m_a_k_gain, v_a_k_gain), a_sinks=(m_a_sinks, v_a_sinks), c_q_gain=(m_c_q_gain, v_c_q_gain),
                c_k_gain=(m_c_k_gain, v_c_k_gain), rel_bias_table=(m_rel_bias_table, v_rel_bias_table),
                mix_out_gain=(m_mix_out_gain, v_mix_out_gain), ffn_norm=(m_ffn_norm, v_ffn_norm),
                conv_w=(m_conv_w, v_conv_w), conv_b=(m_conv_b, v_conv_b))
    sshapes = [given[k].shape for k in _SMALL]
    grads, deltas, new_m, new_v = {}, {}, {}, {}

    def finish_small(after):
        (slots,) = _split_wait(small_handle, after, "small_wait")
        reduced = dict(zip(_SMALL, _unpack(_sum_slots(slots, "sum_small"), shapes)))
        reduced["conv_w"] = lax.dynamic_slice_in_dim(reduced["conv_w"], chip * cw_cols, cw_cols, axis=2)
        s_delta, s_m, s_v = _adamw(_pack([given[k] for k in _SMALL]), _pack([reduced[k] for k in _SMALL]),
                                   _pack([moms[k][0] for k in _SMALL]), _pack([moms[k][1] for k in _SMALL]),
                                   "adamw_small")
        grads.update(reduced)
        deltas.update(zip(_SMALL, _unpack(s_delta, sshapes)))
        new_m.update(zip(_SMALL, _unpack(s_m, sshapes)))
        new_v.update(zip(_SMALL, _unpack(s_v, sshapes)))
        return s_delta

    big_given = dict(w_in=(w_in, m_w_in, v_w_in, True), w_out=(w_out, m_w_out, v_w_out, False),
                     w_up=(w_up, m_w_up, v_w_up, True), w_down=(w_down, m_w_down, v_w_down, False))
    names = ("w_in", "w_out", "w_up", "w_down")
    bufs = {name: None for name in names}
    after = small_token
    for l in reversed(range(depth)):
        if l == 0:
            after = finish_small(after)
            swaps[0] = finish_scatter(0, pending, after)
        layer_grads = [g.reshape(-1, g.shape[-1]) for g in _split_wait(swaps[l], after, f"swap_wait_{l}")]
        for t, name in enumerate(names):
            wt, mt, vt, transposed = big_given[name]
            g = layer_grads[t].T if transposed else layer_grads[t]
            bufs[name] = _adamw_layer(l, wt, g, mt, vt, bufs[name], f"adamw_{name}_{l}")
            after = bufs[name][1]
    for name in names:
        grads[name], deltas[name], new_m[name], new_v[name] = bufs[name]

    order = ("attn_norm", "w_in", "a_q_gain", "a_k_gain", "a_sinks", "c_q_gain", "c_k_gain", "rel_bias_table",
             "mix_out_gain", "w_out", "ffn_norm", "w_up", "conv_w", "conv_b", "w_down")
    return (loss, grad_x, *[grads[k] for k in order], *[deltas[k] for k in order], *[new_m[k] for k in order],
            *[new_v[k] for k in order])
```

```python
import numpy as np
import jax
import jax.numpy as jnp
from jax import lax
from jax.experimental import pallas as pl
from jax.experimental.pallas import tpu as pltpu

F32 = jnp.float32
BF16 = jnp.bfloat16
MESH = pl.DeviceIdType.MESH

HEAD_DIM = 64
BLOCK = 128
LANES = 128
EPS = 1e-6
NEG_INF = -1e30
WINDOW_A = 128
DILATED_PAIRS = ((128, 1), (512, 4), (2048, 16))
N_BUCKETS = 32
T5_MAX_DIST = 2048
CONV_WIDTH = 3
ADAM_LR = 0.001
ADAM_B1 = 0.9
ADAM_B2 = 0.999
ADAM_EPS = 1e-08
ADAM_WD = 0.01
ADAM_STEP = 10
N_CHIPS = 4
N_DEVICES = 8
VMEM_LIMIT_BYTES = 48 * 1024 * 1024
QK_SCALE = HEAD_DIM ** -0.5


def _params(sem=None):
    return pltpu.CompilerParams(dimension_semantics=sem, vmem_limit_bytes=VMEM_LIMIT_BYTES)


def _div_tile(n, cap, mult):
    best = None
    for t in range(mult, min(n, cap) + 1, mult):
        if n % t == 0:
            best = t
    return n if best is None else best


def _dot(a, b):
    return lax.dot_general(a, b, (((1,), (0,)), ((), ())), preferred_element_type=F32)


def _dot_nt(a, b):
    return lax.dot_general(a, b, (((1,), (1,)), ((), ())), preferred_element_type=F32)


def _dot_tn(a, b):
    return lax.dot_general(a, b, (((0,), (0,)), ((), ())), preferred_element_type=F32)


def _split_dot(x, m):
    hi = x.astype(BF16)
    lo = (x - hi.astype(F32)).astype(BF16)
    return _dot(hi, m) + _dot(lo, m)


class _Cfg:
    def __init__(self, d_model, d_ff):
        nh = d_model // HEAD_DIM
        self.d = d_model
        self.f = d_ff
        self.nha = nh // 4
        self.nkva = self.nha // 4
        self.nhb = nh // 4
        self.nhc = nh // 2
        self.a_q = self.nha * HEAD_DIM
        self.a_kv = self.nkva * HEAD_DIM
        self.b_w = self.nhb * HEAD_DIM
        self.c_w = self.nhc * HEAD_DIM
        sizes = [self.a_q, self.a_kv, self.a_kv, self.b_w, self.b_w, self.b_w, self.c_w, self.c_w, self.c_w]
        starts = [0] + [int(s) for s in np.cumsum(sizes)[:-1]]
        self.sections = list(zip(starts, sizes))
        self.in_width = int(sum(sizes))
        assert all(s % LANES == 0 for s in sizes)


def _matmul(a, b, mode, out_dtype, name, tm=512, tn=512, tk=512, residual=None):
    if mode == "tn":
        kdim, m = a.shape
    else:
        m, kdim = a.shape
    n = b.shape[0] if mode == "nt" else b.shape[1]
    tm, tn, tk = _div_tile(m, tm, LANES), _div_tile(n, tn, LANES), _div_tile(kdim, tk, LANES)
    nk = kdim // tk
    if mode == "tn":
        a_spec = pl.BlockSpec((tk, tm), lambda i, j, k: (k, i))
    else:
        a_spec = pl.BlockSpec((tm, tk), lambda i, j, k: (i, k))
    if mode == "nt":
        b_spec = pl.BlockSpec((tn, tk), lambda i, j, k: (j, k))
    else:
        b_spec = pl.BlockSpec((tk, tn), lambda i, j, k: (k, j))
    dot = {"nn": _dot, "nt": _dot_nt, "tn": _dot_tn}[mode]
    o_spec = pl.BlockSpec((tm, tn), lambda i, j, k: (i, j))
    in_specs = [a_spec, b_spec]
    args = [a, b]
    if residual is not None:
        in_specs.append(o_spec)
        args.append(residual)

    def body(*refs):
        if residual is None:
            a_ref, b_ref, o_ref, acc = refs
        else:
            a_ref, b_ref, r_ref, o_ref, acc = refs
        k = pl.program_id(2)

        @pl.when(k == 0)
        def _():
            acc[...] = jnp.zeros_like(acc)

        acc[...] += dot(a_ref[...].astype(BF16), b_ref[...].astype(BF16))

        @pl.when(k == nk - 1)
        def _():
            r = acc[...]
            if residual is not None:
                r = r + r_ref[...]
            o_ref[...] = r.astype(out_dtype)

    return pl.pallas_call(
        body, name=name, grid=(m // tm, n // tn, nk), in_specs=in_specs, out_specs=o_spec,
        out_shape=jax.ShapeDtypeStruct((m, n), out_dtype), scratch_shapes=[pltpu.VMEM((tm, tn), F32)],
        compiler_params=_params(("parallel", "parallel", "arbitrary")),
    )(*args)


def _rmsnorm_fwd(x, g, name):
    s, d = x.shape
    ts = _div_tile(s, 512, 16)

    def body(x_ref, g_ref, o_ref):
        xv = x_ref[...]
        r = lax.rsqrt(jnp.mean(xv * xv, axis=-1, keepdims=True) + EPS)
        o_ref[...] = (xv * r * g_ref[...]).astype(BF16)

    return pl.pallas_call(
        body, name=name, grid=(s // ts,),
        in_specs=[pl.BlockSpec((ts, d), lambda i: (i, 0)), pl.BlockSpec((1, d), lambda i: (0, 0))],
        out_specs=pl.BlockSpec((ts, d), lambda i: (i, 0)), out_shape=jax.ShapeDtypeStruct((s, d), BF16),
        compiler_params=_params(("parallel",)),
    )(x, g)


def _rmsnorm_bwd(x, g, dh, dres, name):
    s, d = x.shape
    ts = _div_tile(s, 256, 16)

    def body(x_ref, g_ref, dh_ref, dres_ref, dx_ref, dxb_ref, dg_ref):
        @pl.when(pl.program_id(0) == 0)
        def _():
            dg_ref[...] = jnp.zeros_like(dg_ref)

        xv = x_ref[...]
        r = lax.rsqrt(jnp.mean(xv * xv, axis=-1, keepdims=True) + EPS)
        xhat = xv * r
        dhv = dh_ref[...]
        dxhat = dhv * g_ref[...]
        dx = dres_ref[...] + r * (dxhat - xhat * jnp.mean(dxhat * xhat, axis=-1, keepdims=True))
        dx_ref[...] = dx
        dxb_ref[...] = dx.astype(BF16)
        dg_ref[...] += jnp.sum(dhv * xhat, axis=0, keepdims=True)

    row = pl.BlockSpec((ts, d), lambda i: (i, 0))
    vec = pl.BlockSpec((1, d), lambda i: (0, 0))
    return pl.pallas_call(
        body, name=name, grid=(s // ts,), in_specs=[row, vec, row, row], out_specs=[row, row, vec],
        out_shape=[jax.ShapeDtypeStruct((s, d), F32), jax.ShapeDtypeStruct((s, d), BF16),
                   jax.ShapeDtypeStruct((1, d), F32)],
        compiler_params=_params(("arbitrary",)),
    )(x, g, dh, dres)


def _head_mean_matrix():
    idx = np.arange(LANES) // HEAD_DIM
    return jnp.asarray((idx[:, None] == idx[None, :]).astype(np.float32) / HEAD_DIM, dtype=BF16)


def _head_mean(y, m128):
    w = y.shape[1]
    parts = [_split_dot(y[:, c * LANES:(c + 1) * LANES], m128) for c in range(w // LANES)]
    return parts[0] if len(parts) == 1 else jnp.concatenate(parts, axis=1)


_NORMED_SECTIONS = (0, 1, 6, 7)
_QUERY_SECTIONS = (0, 3, 6)


def _qk_prep(proj, gains, cfg, name):
    s = proj.shape[0]
    ts = _div_tile(s, 256, 16)
    m128 = _head_mean_matrix()

    def body(p_ref, m_ref, g0, g1, g6, g7, *outs):
        gref = dict(zip(_NORMED_SECTIONS, (g0, g1, g6, g7)))
        for idx, (st, w) in enumerate(cfg.sections):
            xv = p_ref[:, st:st + w]
            if idx in gref:
                r = lax.rsqrt(_head_mean(xv * xv, m_ref[...]) + EPS)
                xv = xv * r * gref[idx][...]
            if idx in _QUERY_SECTIONS:
                xv = xv * QK_SCALE
            outs[idx][...] = xv.astype(BF16)

    in_specs = [pl.BlockSpec((ts, cfg.in_width), lambda i: (i, 0)), pl.BlockSpec((LANES, LANES), lambda i: (0, 0))]
    in_specs += [pl.BlockSpec((1, cfg.sections[k][1]), lambda i: (0, 0)) for k in _NORMED_SECTIONS]
    out_specs = [pl.BlockSpec((ts, w), lambda i: (i, 0)) for _, w in cfg.sections]
    out_shape = [jax.ShapeDtypeStruct((s, w), BF16) for _, w in cfg.sections]
    return pl.pallas_call(
        body, name=name, grid=(s // ts,), in_specs=in_specs, out_specs=out_specs, out_shape=out_shape,
        compiler_params=_params(("parallel",)),
    )(proj, m128, *gains)


def _qk_prep_bwd(proj, gains, grads, cfg, name):
    s = proj.shape[0]
    ts = _div_tile(s, 128, 16)
    m128 = _head_mean_matrix()
    counts = [len(gl) for gl in grads]
    flat = [g for gl in grads for g in gl]

    def body(*refs):
        p_ref, m_ref = refs[0], refs[1]
        gref = dict(zip(_NORMED_SECTIONS, refs[2:6]))
        g_in = refs[6:6 + len(flat)]
        dp_ref = refs[6 + len(flat)]
        dgain = dict(zip(_NORMED_SECTIONS, refs[7 + len(flat):]))

        @pl.when(pl.program_id(0) == 0)
        def _():
            for k in _NORMED_SECTIONS:
                dgain[k][...] = jnp.zeros_like(dgain[k])

        pos = 0
        for idx, (st, w) in enumerate(cfg.sections):
            dy = g_in[pos][...].astype(F32)
            for extra in g_in[pos + 1:pos + counts[idx]]:
                dy = dy + extra[...].astype(F32)
            pos += counts[idx]
            if idx in gref:
                xv = p_ref[:, st:st + w]
                r = lax.rsqrt(_head_mean(xv * xv, m_ref[...]) + EPS)
                xhat = xv * r
                dxhat = dy * gref[idx][...]
                dgain[idx][...] += jnp.sum(dy * xhat, axis=0, keepdims=True)
                dy = r * (dxhat - xhat * _head_mean(dxhat * xhat, m_ref[...]))
            dp_ref[:, st:st + w] = dy.astype(BF16)

    in_specs = [pl.BlockSpec((ts, cfg.in_width), lambda i: (i, 0)), pl.BlockSpec((LANES, LANES), lambda i: (0, 0))]
    in_specs += [pl.BlockSpec((1, cfg.sections[k][1]), lambda i: (0, 0)) for k in _NORMED_SECTIONS]
    for idx, (_, w) in enumerate(cfg.sections):
        in_specs += [pl.BlockSpec((ts, w), lambda i: (i, 0))] * counts[idx]
    out_specs = [pl.BlockSpec((ts, cfg.in_width), lambda i: (i, 0))]
    out_specs += [pl.BlockSpec((1, cfg.sections[k][1]), lambda i: (0, 0)) for k in _NORMED_SECTIONS]
    out_shape = [jax.ShapeDtypeStruct((s, cfg.in_width), BF16)]
    out_shape += [jax.ShapeDtypeStruct((1, cfg.sections[k][1]), F32) for k in _NORMED_SECTIONS]
    return pl.pallas_call(
        body, name=name, grid=(s // ts,), in_specs=in_specs, out_specs=out_specs, out_shape=out_shape,
        compiler_params=_params(("arbitrary",)),
    )(proj, m128, *gains, *flat)


def _band_masks(max_dist):
    row = lax.broadcasted_iota(jnp.int32, (BLOCK, BLOCK), 0)
    col = lax.broadcasted_iota(jnp.int32, (BLOCK, BLOCK), 1)
    return row + BLOCK - col <= max_dist, col <= row


def _dilated_t(a, dil):
    s, w = a.shape
    return _keys_on_lanes(a.reshape(s // dil, dil * w), BLOCK)


def _undilated(at, dil):
    nblk, dw, _ = at.shape
    return jnp.transpose(at, (0, 2, 1)).reshape(nblk * BLOCK * dil, dw // dil)


def _banded_fwd(q, kt, v, bias, sinks, hq, hk, max_dist, dil, name):
    s = q.shape[0]
    wq, wk, sd, grp = hq * HEAD_DIM, hk * HEAD_DIM, s // dil, hq // hk
    nb = sd // BLOCK
    has_sink = sinks is not None

    def body(*refs):
        if has_sink:
            q_ref, ktp_ref, ktc_ref, vp_ref, vc_ref, b_ref, s_ref, o_ref, l_ref = refs
        else:
            q_ref, ktp_ref, ktc_ref, vp_ref, vc_ref, b_ref, o_ref, l_ref = refs
        i = pl.program_id(1)
        mprev, mcur = _band_masks(max_dist)
        mask = jnp.concatenate([jnp.logical_and(mprev, i > 0), mcur], axis=1)
        for h in range(hq):
            sq = slice(h * HEAD_DIM, (h + 1) * HEAD_DIM)
            sk = slice((h // grp) * HEAD_DIM, (h // grp + 1) * HEAD_DIM)
            kt = jnp.concatenate([ktp_ref[sk, :], ktc_ref[sk, :]], axis=1)
            vv = jnp.concatenate([vp_ref[:, sk], vc_ref[:, sk]], axis=0)
            sc = jnp.where(mask, _dot(q_ref[:, sq], kt) + b_ref[h], NEG_INF)
            m = jnp.max(sc, axis=-1, keepdims=True)
            if has_sink:
                m = jnp.maximum(m, s_ref[h])
            p = jnp.exp(sc - m)
            den = jnp.sum(p, axis=-1, keepdims=True)
            if has_sink:
                den = den + jnp.exp(s_ref[h] - m)
            o_ref[:, sq] = _dot(p.astype(BF16), vv) / den
            l_ref[:, h:h + 1] = m + jnp.log(den)

    qspec = pl.BlockSpec((BLOCK, wq), lambda r, i: (i, r))
    kprev = pl.BlockSpec((BLOCK, wk), lambda r, i: (jnp.maximum(i - 1, 0), r))
    kcur = pl.BlockSpec((BLOCK, wk), lambda r, i: (i, r))
    ktprev = pl.BlockSpec((None, wk, BLOCK), lambda r, i: (jnp.maximum(i - 1, 0), r, 0))
    ktcur = pl.BlockSpec((None, wk, BLOCK), lambda r, i: (i, r, 0))
    in_specs = [qspec, ktprev, ktcur, kprev, kcur, pl.BlockSpec((hq, BLOCK, 2 * BLOCK), lambda r, i: (0, 0, 0))]
    v2 = v.reshape(sd, dil * wk)
    args = [q.reshape(sd, dil * wq), kt, kt, v2, v2, bias]
    if has_sink:
        in_specs.append(pl.BlockSpec(memory_space=pltpu.SMEM))
        args.append(sinks)
    out, lse = pl.pallas_call(
        body, name=name, grid=(dil, nb), in_specs=in_specs,
        out_specs=[qspec, pl.BlockSpec((None, BLOCK, hq), lambda r, i: (r, i, 0))],
        out_shape=[jax.ShapeDtypeStruct((sd, dil * wq), F32), jax.ShapeDtypeStruct((dil, sd, hq), F32)],
        compiler_params=_params(("parallel", "parallel")),
    )(*args)
    return out.reshape(s, wq), jnp.transpose(lse, (1, 0, 2)).reshape(s, hq)


def _per_head_dilated(a, dil):
    s, h = a.shape
    return jnp.transpose(a.reshape(s // dil, dil, h), (1, 0, 2))


def _banded_bwd(q, k, kt, v, lse, dsum, do, bias, sinks, dbias_init, hq, hk, max_dist, dil, name):
    s = q.shape[0]
    wq, wk, sd, grp = hq * HEAD_DIM, hk * HEAD_DIM, s // dil, hq // hk
    nb = sd // BLOCK
    has_sink = sinks is not None

    def body(*refs):
        (q_ref, qn_ref, qt_ref, qtn_ref, kp_ref, kc_ref, ktp_ref, ktc_ref, vtp_ref, vtc_ref, l_ref, ln_ref, d_ref,
         dn_ref, do_ref, don_ref, dot_ref, dotn_ref, b_ref, dbi_ref) = refs[:20]
        rest = refs[20:]
        if has_sink:
            s_ref, dq_ref, dkt_ref, dvt_ref, db_ref, ds_ref = rest
        else:
            dq_ref, dkt_ref, dvt_ref, db_ref = rest
        j = pl.program_id(1)

        @pl.when(jnp.logical_and(pl.program_id(0) == 0, j == 0))
        def _():
            db_ref[...] = dbi_ref[...]
            if has_sink:
                ds_ref[...] = jnp.zeros_like(ds_ref)

        mprev_static, mcur = _band_masks(max_dist)
        mask = jnp.concatenate([jnp.logical_and(mprev_static, j > 0), mcur], axis=1)
        mnext = jnp.logical_and(mprev_static, j + 1 < nb)
        dkt_acc = [jnp.zeros((HEAD_DIM, BLOCK), F32) for _ in range(hk)]
        dvt_acc = [jnp.zeros((HEAD_DIM, BLOCK), F32) for _ in range(hk)]
        for h in range(hq):
            g = h // grp
            sq = slice(h * HEAD_DIM, (h + 1) * HEAD_DIM)
            sk = slice(g * HEAD_DIM, (g + 1) * HEAD_DIM)
            kt2 = jnp.concatenate([ktp_ref[sk, :], ktc_ref[sk, :]], axis=1)
            vt2 = jnp.concatenate([vtp_ref[sk, :], vtc_ref[sk, :]], axis=1)
            k2 = jnp.concatenate([kp_ref[:, sk], kc_ref[:, sk]], axis=0)
            lcol = l_ref[:, h:h + 1]
            dcol = d_ref[:, h:h + 1]
            sc = _dot(q_ref[:, sq], kt2) + b_ref[h]
            p = jnp.where(mask, jnp.exp(sc - lcol), 0.0)
            ds = p * (_dot(do_ref[:, sq], vt2) - dcol)
            dsb = ds.astype(BF16)
            dq_ref[:, sq] = (_dot(dsb, k2) * QK_SCALE).astype(BF16)
            db_ref[h] += ds
            if has_sink:
                psink = jnp.exp(s_ref[h] - lcol)
                tot = jnp.sum(psink * dcol, axis=0, keepdims=True)
                ds_ref[h:h + 1, :] -= jnp.broadcast_to(tot, (1, LANES))
            lncol = ln_ref[:, h:h + 1]
            dncol = dn_ref[:, h:h + 1]
            sn = _dot(qn_ref[:, sq], ktc_ref[sk, :]) + b_ref[h, :, 0:BLOCK]
            pn = jnp.where(mnext, jnp.exp(sn - lncol), 0.0)
            dsn = pn * (_dot(don_ref[:, sq], vtc_ref[sk, :]) - dncol)
            dkt_acc[g] = dkt_acc[g] + (_dot(qt_ref[sq, :], dsb[:, BLOCK:]) + _dot(qtn_ref[sq, :], dsn.astype(BF16)))
            dvt_acc[g] = dvt_acc[g] + (_dot(dot_ref[sq, :], p[:, BLOCK:].astype(BF16))
                                       + _dot(dotn_ref[sq, :], pn.astype(BF16)))
        for g in range(hk):
            sk = slice(g * HEAD_DIM, (g + 1) * HEAD_DIM)
            dkt_ref[sk, :] = dkt_acc[g].astype(BF16)
            dvt_ref[sk, :] = dvt_acc[g].astype(BF16)

    qcur = pl.BlockSpec((BLOCK, wq), lambda r, j: (j, r))
    qnext = pl.BlockSpec((BLOCK, wq), lambda r, j: (jnp.minimum(j + 1, nb - 1), r))
    qtcur = pl.BlockSpec((None, wq, BLOCK), lambda r, j: (j, r, 0))
    qtnext = pl.BlockSpec((None, wq, BLOCK), lambda r, j: (jnp.minimum(j + 1, nb - 1), r, 0))
    kprev = pl.BlockSpec((BLOCK, wk), lambda r, j: (jnp.maximum(j - 1, 0), r))
    kcur = pl.BlockSpec((BLOCK, wk), lambda r, j: (j, r))
    ktprev = pl.BlockSpec((None, wk, BLOCK), lambda r, j: (jnp.maximum(j - 1, 0), r, 0))
    ktcur = pl.BlockSpec((None, wk, BLOCK), lambda r, j: (j, r, 0))
    bspec = pl.BlockSpec((hq, BLOCK, 2 * BLOCK), lambda r, j: (0, 0, 0))
    hcur = pl.BlockSpec((None, BLOCK, hq), lambda r, j: (r, j, 0))
    hnext = pl.BlockSpec((None, BLOCK, hq), lambda r, j: (r, jnp.minimum(j + 1, nb - 1), 0))
    dob = do.astype(BF16)
    q2, k2, do2 = q.reshape(sd, dil * wq), k.reshape(sd, dil * wk), dob.reshape(sd, dil * wq)
    l3, d3 = _per_head_dilated(lse, dil), _per_head_dilated(dsum, dil)
    qt, vt, dot = _dilated_t(q, dil), _dilated_t(v, dil), _dilated_t(dob, dil)
    in_specs = [qcur, qnext, qtcur, qtnext, kprev, kcur, ktprev, ktcur, ktprev, ktcur, hcur, hnext, hcur, hnext,
                qcur, qnext, qtcur, qtnext, bspec, bspec]
    args = [q2, q2, qt, qt, k2, k2, kt, kt, vt, vt, l3, l3, d3, d3, do2, do2, dot, dot, bias, dbias_init]
    out_specs = [qcur, ktcur, ktcur, bspec]
    out_shape = [jax.ShapeDtypeStruct((sd, dil * wq), BF16), jax.ShapeDtypeStruct((nb, dil * wk, BLOCK), BF16),
                 jax.ShapeDtypeStruct((nb, dil * wk, BLOCK), BF16), jax.ShapeDtypeStruct((hq, BLOCK, 2 * BLOCK), F32)]
    if has_sink:
        in_specs.append(pl.BlockSpec(memory_space=pltpu.SMEM))
        args.append(sinks)
        out_specs.append(pl.BlockSpec((hq, LANES), lambda r, j: (0, 0)))
        out_shape.append(jax.ShapeDtypeStruct((hq, LANES), F32))
    res = pl.pallas_call(
        body, name=name, grid=(dil, nb), in_specs=in_specs, out_specs=out_specs, out_shape=out_shape,
        compiler_params=_params(("arbitrary", "arbitrary")),
    )(*args)
    dq, dk, dv, dbias = res[0].reshape(s, wq), _undilated(res[1], dil), _undilated(res[2], dil), res[3]
    return dq, dk, dv, dbias, (res[4][:, 0] if has_sink else None)


def _neg_softplus(z):
    return -(jnp.maximum(z, 0.0) + jnp.log(1.0 + jnp.exp(-jnp.abs(z))))


SB_CHUNK = 256
SB_ROWS = 256
HEADS_PER_PAIR = LANES // HEAD_DIM


def _tri(kind):
    row = lax.broadcasted_iota(jnp.int32, (SB_CHUNK, SB_CHUNK), 0)
    col = lax.broadcasted_iota(jnp.int32, (SB_CHUNK, SB_CHUNK), 1)
    return {"ge": row >= col, "lt": row < col, "le": row <= col}[kind].astype(BF16)


def _keys_on_lanes(a, rows):
    s, w = a.shape
    return jnp.transpose(a.reshape(s // rows, rows, w), (0, 2, 1))


def _sb_mask(i, jj):
    row = lax.broadcasted_iota(jnp.int32, (SB_ROWS, SB_CHUNK), 0)
    col = lax.broadcasted_iota(jnp.int32, (SB_ROWS, SB_CHUNK), 1)
    return col < row + (i * SB_ROWS - jj * SB_CHUNK)


def _sb_trips(i):
    return (i * SB_ROWS) // (2 * SB_CHUNK) + 1


def _sb_rows(jj, n):
    return pl.ds(pl.multiple_of(jj * SB_CHUNK, SB_CHUNK), n * SB_CHUNK)


def _sb_fwd(q, kt, v, name):
    s, w = q.shape
    npair, nb, nc = w // LANES, s // SB_ROWS, s // SB_CHUNK

    def body(q_ref, kt_ref, v_ref, o_ref, t_ref):
        i = pl.program_id(1)
        lincl = _tri("ge")
        heads = [slice(hh * HEAD_DIM, (hh + 1) * HEAD_DIM) for hh in range(HEADS_PER_PAIR)]
        qs = [q_ref[:, sl] for sl in heads]

        def trip(t, carry, masked):
            lo, hi = 2 * t, 2 * t + 1
            mlo, mhi = (_sb_mask(i, lo), _sb_mask(i, hi)) if masked else (None, None)

            def keep(m, val):
                return val if m is None else jnp.where(m, val, 0.0)

            new = []
            for hh, sl in enumerate(heads):
                o_acc, rem = carry[hh]
                zhi = _dot(qs[hh], kt_ref[hi, sl, :])
                zlo = _dot(qs[hh], kt_ref[lo, sl, :])
                lrhi = keep(mhi, _neg_softplus(zhi))
                lrlo = keep(mlo, _neg_softplus(zlo))
                tothi = jnp.sum(lrhi, axis=-1, keepdims=True)
                ahi = keep(mhi, jnp.exp(zhi + (rem + _split_dot(lrhi, lincl))))
                alo = keep(mlo, jnp.exp(zlo + (rem + tothi + _split_dot(lrlo, lincl))))
                a = jnp.concatenate([alo, ahi], axis=1).astype(BF16)
                new.append((o_acc + _dot(a, v_ref[_sb_rows(lo, 2), sl]),
                            rem + tothi + jnp.sum(lrlo, axis=-1, keepdims=True)))
            return tuple(new)

        init = tuple((jnp.zeros((SB_ROWS, HEAD_DIM), F32), jnp.zeros((SB_ROWS, 1), F32)) for _ in heads)
        trips = _sb_trips(i)
        carry = trip(trips - 1, init, True)
        carry = lax.fori_loop(0, trips - 1, lambda t, cr: trip(trips - 2 - t, cr, False), carry)
        for hh, sl in enumerate(heads):
            o_ref[:, sl] = carry[hh][0]
            t_ref[:, sl] = jnp.broadcast_to(carry[hh][1], (SB_ROWS, HEAD_DIM))

    qspec = pl.BlockSpec((SB_ROWS, LANES), lambda p, i: (i, p))
    return pl.pallas_call(
        body, name=name, grid=(npair, nb),
        in_specs=[qspec, pl.BlockSpec((nc, LANES, SB_CHUNK), lambda p, i: (0, p, 0)),
                  pl.BlockSpec((s, LANES), lambda p, i: (0, p))],
        out_specs=[qspec, qspec], out_shape=[jax.ShapeDtypeStruct((s, w), F32)] * 2,
        compiler_params=_params(("parallel", "parallel")),
    )(q, kt, v)


def _sb_bwd(q, k, kt, v, tot, do, name):
    s, w = q.shape
    npair, nb, nc = w // LANES, s // SB_ROWS, s // SB_CHUNK
    dob = do.astype(BF16)

    def body(q_ref, qt_ref, k_ref, kt_ref, vt_ref, t_ref, do_ref, dot_ref, dq_ref, dkt_ref, dvt_ref):
        i = pl.program_id(1)

        @pl.when(i == 0)
        def _():
            dkt_ref[...] = jnp.zeros_like(dkt_ref)
            dvt_ref[...] = jnp.zeros_like(dvt_ref)

        lbefore = _tri("lt")
        lupto = _tri("le")
        heads = [slice(hh * HEAD_DIM, (hh + 1) * HEAD_DIM) for hh in range(HEADS_PER_PAIR)]
        qs = [q_ref[:, sl] for sl in heads]
        qts = [qt_ref[sl, :] for sl in heads]
        dos = [do_ref[:, sl] for sl in heads]
        dots = [dot_ref[sl, :] for sl in heads]
        totals = [t_ref[:, sl.start:sl.start + 1] for sl in heads]

        def trip(t, carry, masked):
            chunks = (2 * t, 2 * t + 1)
            masks = [_sb_mask(i, jj) if masked else None for jj in chunks]

            def keep(m, val):
                return val if m is None else jnp.where(m, val, 0.0)

            new = []
            for hh, sl in enumerate(heads):
                dq_acc, plr, pg = carry[hh]
                zs = [_dot(qs[hh], kt_ref[jj, sl, :]) for jj in chunks]
                lrs = [keep(m, _neg_softplus(z)) for m, z in zip(masks, zs)]
                lr_sums = [jnp.sum(lr, axis=-1, keepdims=True) for lr in lrs]
                before = [plr, plr + lr_sums[0]]
                avs = [keep(m, jnp.exp(z + (totals[hh] - (b + _split_dot(lr, lbefore)))))
                       for m, z, lr, b in zip(masks, zs, lrs, before)]
                gs = [_dot(dos[hh], vt_ref[jj, sl, :]) * a for jj, a in zip(chunks, avs)]
                g_sums = [jnp.sum(g, axis=-1, keepdims=True) for g in gs]
                upto = [pg, pg + g_sums[0]]
                dzs = [keep(m, g - jnp.exp(z + lr) * (u + _split_dot(g, lupto))).astype(BF16)
                       for m, z, lr, g, u in zip(masks, zs, lrs, gs, upto)]
                for jj, dzb, a in zip(chunks, dzs, avs):
                    dkt_ref[jj, sl, :] += _dot(qts[hh], dzb)
                    dvt_ref[jj, sl, :] += _dot(dots[hh], a.astype(BF16))
                dz2 = jnp.concatenate(dzs, axis=1)
                new.append((dq_acc + _dot(dz2, k_ref[_sb_rows(chunks[0], 2), sl]), plr + lr_sums[0] + lr_sums[1],
                            pg + g_sums[0] + g_sums[1]))
            return tuple(new)

        zero = jnp.zeros((SB_ROWS, 1), F32)
        init = tuple((jnp.zeros((SB_ROWS, HEAD_DIM), F32), zero, zero) for _ in heads)
        trips = _sb_trips(i)
        carry = lax.fori_loop(0, trips - 1, lambda t, cr: trip(t, cr, False), init)
        carry = trip(trips - 1, carry, True)
        for hh, sl in enumerate(heads):
            dq_ref[:, sl] = (carry[hh][0] * QK_SCALE).astype(BF16)

    qspec = pl.BlockSpec((SB_ROWS, LANES), lambda p, i: (i, p))
    qtspec = pl.BlockSpec((None, LANES, SB_ROWS), lambda p, i: (i, p, 0))
    kspec = pl.BlockSpec((s, LANES), lambda p, i: (0, p))
    ktspec = pl.BlockSpec((nc, LANES, SB_CHUNK), lambda p, i: (0, p, 0))
    dq, dkt, dvt = pl.pallas_call(
        body, name=name, grid=(npair, nb), in_specs=[qspec, qtspec, kspec, ktspec, ktspec, qspec, qspec, qtspec],
        out_specs=[qspec, ktspec, ktspec],
        out_shape=[jax.ShapeDtypeStruct((s, w), BF16)] + [jax.ShapeDtypeStruct((nc, w, SB_CHUNK), F32)] * 2,
        compiler_params=_params(("parallel", "arbitrary")),
    )(q, _keys_on_lanes(q, SB_ROWS), k, kt, _keys_on_lanes(v, SB_CHUNK), tot, dob, _keys_on_lanes(dob, SB_ROWS))

    def rows_first(t):
        return jnp.transpose(t, (0, 2, 1)).reshape(s, w)

    return dq, rows_first(dkt), rows_first(dvt)


def _group_norm(xv, g):
    r = lax.rsqrt(jnp.mean(xv * xv, axis=-1, keepdims=True) + EPS)
    return xv * r * g


def _head_spread(nheads):
    return jnp.asarray(np.repeat(np.eye(nheads, dtype=np.float32), HEAD_DIM, axis=1), dtype=BF16)


def _mix_fwd(oa, ob, ocs, lses, gain, cfg, name):
    s = oa.shape[0]
    ts = _div_tile(s, 512, 16)
    aq, bw, cw, nhc = cfg.a_q, cfg.b_w, cfg.c_w, cfg.nhc

    def body(oa_ref, ob_ref, c1, c2, c3, l1, l2, l3, sp_ref, g_ref, mix_ref, oc_ref, lse_ref):
        m = jnp.maximum(jnp.maximum(l1[...], l2[...]), l3[...])
        es = [jnp.exp(l[...] - m) for l in (l1, l2, l3)]
        den = es[0] + es[1] + es[2]
        oc = sum(_split_dot(e / den, sp_ref[...]) * c[...] for e, c in zip(es, (c1, c2, c3)))
        oc_ref[...] = oc
        lse_ref[...] = m + jnp.log(den)
        mix_ref[:, 0:aq] = _group_norm(oa_ref[...], g_ref[:, 0:aq]).astype(BF16)
        mix_ref[:, aq:aq + bw] = _group_norm(ob_ref[...], g_ref[:, aq:aq + bw]).astype(BF16)
        mix_ref[:, aq + bw:] = _group_norm(oc, g_ref[:, aq + bw:]).astype(BF16)

    def row(wd):
        return pl.BlockSpec((ts, wd), lambda i: (i, 0))

    return pl.pallas_call(
        body, name=name, grid=(s // ts,),
        in_specs=[row(aq), row(bw)] + [row(cw)] * 3 + [row(nhc)] * 3
        + [pl.BlockSpec((nhc, cw), lambda i: (0, 0)), pl.BlockSpec((1, cfg.d), lambda i: (0, 0))],
        out_specs=[row(cfg.d), row(cw), row(nhc)],
        out_shape=[jax.ShapeDtypeStruct((s, cfg.d), BF16), jax.ShapeDtypeStruct((s, cw), F32),
                   jax.ShapeDtypeStruct((s, nhc), F32)],
        compiler_params=_params(("parallel",)),
    )(oa, ob, *ocs, *lses, _head_spread(nhc), gain)


def _mix_bwd(dmix, oa, ob, oc, gain, cfg, name):
    s = oa.shape[0]
    ts = _div_tile(s, 256, 8)
    aq, bw, cw = cfg.a_q, cfg.b_w, cfg.c_w

    def body(dm_ref, oa_ref, ob_ref, oc_ref, g_ref, fa_ref, fc_ref, da_ref, db_ref, dc_ref, dg_ref, sa_ref, sc_ref):
        @pl.when(pl.program_id(0) == 0)
        def _():
            dg_ref[...] = jnp.zeros_like(dg_ref)

        for x_ref, dx_ref, lo, hi, fold in ((oa_ref, da_ref, 0, aq, (fa_ref, sa_ref)), (ob_ref, db_ref, aq, aq + bw, None),
                                            (oc_ref, dc_ref, aq + bw, aq + bw + cw, (fc_ref, sc_ref))):
            xv = x_ref[...]
            dy = dm_ref[:, lo:hi]
            r = lax.rsqrt(jnp.mean(xv * xv, axis=-1, keepdims=True) + EPS)
            xhat = xv * r
            dxhat = dy * g_ref[:, lo:hi]
            dx = r * (dxhat - xhat * jnp.mean(dxhat * xhat, axis=-1, keepdims=True))
            dx_ref[...] = dx
            dg_ref[:, lo:hi] += jnp.sum(dy * xhat, axis=0, keepdims=True)
            if fold is not None:
                fold[1][...] = _split_dot(dx * xv, fold[0][...])

    def row(wd):
        return pl.BlockSpec((ts, wd), lambda i: (i, 0))

    vec = pl.BlockSpec((1, cfg.d), lambda i: (0, 0))
    return pl.pallas_call(
        body, name=name, grid=(s // ts,),
        in_specs=[row(cfg.d), row(aq), row(bw), row(cw), vec, pl.BlockSpec((aq, cfg.nha), lambda i: (0, 0)),
                  pl.BlockSpec((cw, cfg.nhc), lambda i: (0, 0))],
        out_specs=[row(aq), row(bw), row(cw), vec, row(cfg.nha), row(cfg.nhc)],
        out_shape=[jax.ShapeDtypeStruct((s, aq), F32), jax.ShapeDtypeStruct((s, bw), F32),
                   jax.ShapeDtypeStruct((s, cw), F32), jax.ShapeDtypeStruct((1, cfg.d), F32),
                   jax.ShapeDtypeStruct((s, cfg.nha), F32), jax.ShapeDtypeStruct((s, cfg.nhc), F32)],
        compiler_params=_params(("arbitrary",)),
    )(dmix, oa, ob, oc, gain, _head_spread(cfg.nha).T, _head_spread(cfg.nhc).T)


def _bias_table_grad(dbiases, buckets, name):
    outs = []
    for idx, (db, bk) in enumerate(zip(dbiases, buckets)):
        h = db.shape[0]

        def body(db_ref, bk_ref, o_ref):
            xv = db_ref[0]
            ids = bk_ref[...]
            lane = lax.broadcasted_iota(jnp.int32, (1, LANES), 1)
            acc = jnp.zeros((1, LANES), F32)
            for b in range(N_BUCKETS):
                tot = jnp.sum(jnp.where(ids == b, xv, 0.0), axis=0, keepdims=True)
                tot = jnp.sum(tot, axis=1, keepdims=True)
                acc = jnp.where(lane == b, tot, acc)
            o_ref[0] = acc

        outs.append(pl.pallas_call(
            body, name=f"{name}_{idx}", grid=(h,),
            in_specs=[pl.BlockSpec((1, BLOCK, 2 * BLOCK), lambda i: (i, 0, 0)),
                      pl.BlockSpec((BLOCK, 2 * BLOCK), lambda i: (0, 0))],
            out_specs=pl.BlockSpec((1, 1, LANES), lambda i: (i, 0, 0)),
            out_shape=jax.ShapeDtypeStruct((h, 1, LANES), F32), compiler_params=_params(("parallel",)),
        )(db, bk)[:, 0, :])
    return outs


SUBLANES = 8


def _shift_down(u, n, rows):
    r = pltpu.roll(u, n, 0)
    return jnp.concatenate([jnp.where(rows[:SUBLANES] >= n, r[:SUBLANES], 0.0), r[SUBLANES:]], axis=0)


def _shift_up(u, n, rows, s):
    r = pltpu.roll(u, s - n, 0)
    return jnp.concatenate([r[:s - SUBLANES], jnp.where(rows[s - SUBLANES:] < s - n, r[s - SUBLANES:], 0.0)], axis=0)


def _conv(u, w_ref, b_ref, rows):
    return (b_ref[...] + w_ref[0:1, :] * _shift_down(u, 2, rows) + w_ref[1:2, :] * _shift_down(u, 1, rows)
            + w_ref[2:3, :] * u)


def _conv_act_fwd(u, conv_w, conv_b, f, name):
    s = u.shape[0]
    nf = f // LANES

    def body(ug_ref, uu_ref, wg_ref, wu_ref, bg_ref, bu_ref, act_ref):
        rows = lax.broadcasted_iota(jnp.int32, (s, LANES), 0)
        gate = _conv(ug_ref[...], wg_ref, bg_ref, rows)
        up = _conv(uu_ref[...], wu_ref, bu_ref, rows)
        act_ref[...] = (gate * jax.nn.sigmoid(gate) * up).astype(BF16)

    def col(rws, off):
        return pl.BlockSpec((rws, LANES), lambda j: (0, j + off))

    return pl.pallas_call(
        body, name=name, grid=(nf,),
        in_specs=[col(s, 0), col(s, nf), col(CONV_WIDTH, 0), col(CONV_WIDTH, nf), col(1, 0), col(1, nf)],
        out_specs=col(s, 0), out_shape=jax.ShapeDtypeStruct((s, f), BF16), compiler_params=_params(("parallel",)),
    )(u, u, conv_w, conv_w, conv_b, conv_b)


def _conv_act_bwd(u, dact, conv_w, conv_b, f, name):
    s = u.shape[0]
    nf = f // LANES

    def body(ug_ref, uu_ref, da_ref, wg_ref, wu_ref, bg_ref, bu_ref, dug_ref, duu_ref, dwg_ref, dwu_ref, dbg_ref,
             dbu_ref):
        rows = lax.broadcasted_iota(jnp.int32, (s, LANES), 0)
        ug, uu = ug_ref[...], uu_ref[...]
        gate = _conv(ug, wg_ref, bg_ref, rows)
        up = _conv(uu, wu_ref, bu_ref, rows)
        sg = jax.nn.sigmoid(gate)
        da = da_ref[...]
        dgate = da * up * (sg * (1.0 + gate * (1.0 - sg)))
        dup = da * (gate * sg)
        for du, uv, w_ref, du_ref, dw_ref, db_ref in ((dgate, ug, wg_ref, dug_ref, dwg_ref, dbg_ref),
                                                     (dup, uu, wu_ref, duu_ref, dwu_ref, dbu_ref)):
            du_ref[...] = (w_ref[2:3, :] * du + w_ref[1:2, :] * _shift_up(du, 1, rows, s)
                           + w_ref[0:1, :] * _shift_up(du, 2, rows, s)).astype(BF16)
            dw_ref[0:1, :] = jnp.sum(du * _shift_down(uv, 2, rows), axis=0, keepdims=True)
            dw_ref[1:2, :] = jnp.sum(du * _shift_down(uv, 1, rows), axis=0, keepdims=True)
            dw_ref[2:3, :] = jnp.sum(du * uv, axis=0, keepdims=True)
            db_ref[...] = jnp.sum(du, axis=0, keepdims=True)

    def col(rws, off):
        return pl.BlockSpec((rws, LANES), lambda j: (0, j + off))

    return pl.pallas_call(
        body, name=name, grid=(nf,),
        in_specs=[col(s, 0), col(s, nf), col(s, 0), col(CONV_WIDTH, 0), col(CONV_WIDTH, nf), col(1, 0), col(1, nf)],
        out_specs=[col(s, 0), col(s, 0), col(CONV_WIDTH, 0), col(CONV_WIDTH, 0), col(1, 0), col(1, 0)],
        out_shape=[jax.ShapeDtypeStruct((s, f), BF16)] * 2 + [jax.ShapeDtypeStruct((CONV_WIDTH, f), F32)] * 2
        + [jax.ShapeDtypeStruct((1, f), F32)] * 2,
        compiler_params=_params(("parallel",)),
    )(u, u, dact, conv_w, conv_w, conv_b, conv_b)


def _loss_head(y, target, name):
    s, d = y.shape
    ts = _div_tile(s, 256, 16)

    def body(y_ref, t_ref, dy_ref, dyb_ref, l_ref):
        @pl.when(pl.program_id(0) == 0)
        def _():
            l_ref[...] = jnp.zeros_like(l_ref)

        err = y_ref[...] - t_ref[...]
        dy = err * (1.0 / d)
        dy_ref[...] = dy
        dyb_ref[...] = dy.astype(BF16)
        tot = jnp.sum(jnp.sum(err * err, axis=0, keepdims=True), axis=1, keepdims=True) * (0.5 / d)
        l_ref[...] += jnp.broadcast_to(tot, l_ref.shape)

    row = pl.BlockSpec((ts, d), lambda i: (i, 0))
    return pl.pallas_call(
        body, name=name, grid=(s // ts,), in_specs=[row, row],
        out_specs=[row, row, pl.BlockSpec((8, LANES), lambda i: (0, 0))],
        out_shape=[jax.ShapeDtypeStruct((s, d), F32), jax.ShapeDtypeStruct((s, d), BF16),
                   jax.ShapeDtypeStruct((8, LANES), F32)],
        compiler_params=_params(("arbitrary",)),
    )(y, target)


def _adamw(w, g, m, v, name):
    r, c = w.shape
    tr = _div_tile(r, max(8, (1 << 18) // c // 8 * 8), 8)
    c1 = 1.0 - ADAM_B1 ** ADAM_STEP
    c2 = 1.0 - ADAM_B2 ** ADAM_STEP

    def body(w_ref, g_ref, m_ref, v_ref, d_ref, nm_ref, nv_ref):
        gv = g_ref[...]
        nm = ADAM_B1 * m_ref[...] + (1.0 - ADAM_B1) * gv
        nv = ADAM_B2 * v_ref[...] + (1.0 - ADAM_B2) * (gv * gv)
        d_ref[...] = -ADAM_LR * ((nm / c1) / (jnp.sqrt(nv / c2) + ADAM_EPS) + ADAM_WD * w_ref[...])
        nm_ref[...] = nm
        nv_ref[...] = nv

    spec = pl.BlockSpec((tr, c), lambda i: (i, 0))
    return pl.pallas_call(
        body, name=name, grid=(r // tr,), in_specs=[spec] * 4, out_specs=[spec] * 3,
        out_shape=[jax.ShapeDtypeStruct((r, c), F32)] * 3, compiler_params=_params(("parallel",)),
    )(w, g, m, v)


def _adamw_layer(layer, w, g, m, v, bufs, name):
    depth, r, c = w.shape
    tr = _div_tile(r, max(8, (1 << 19) // c // 8 * 8), 8)
    c1 = 1.0 - ADAM_B1 ** ADAM_STEP
    c2 = 1.0 - ADAM_B2 ** ADAM_STEP

    def body(*refs):
        w_ref, g_ref, m_ref, v_ref = refs[:4]
        go_ref, d_ref, nm_ref, nv_ref = refs[-4:]
        gv = g_ref[...]
        nm = ADAM_B1 * m_ref[...] + (1.0 - ADAM_B1) * gv
        nv = ADAM_B2 * v_ref[...] + (1.0 - ADAM_B2) * (gv * gv)
        d_ref[...] = -ADAM_LR * ((nm / c1) / (jnp.sqrt(nv / c2) + ADAM_EPS) + ADAM_WD * w_ref[...])
        nm_ref[...] = nm
        nv_ref[...] = nv
        go_ref[...] = gv

    lay = pl.BlockSpec((None, tr, c), lambda i: (layer, i, 0))
    in_specs = [lay, pl.BlockSpec((tr, c), lambda i: (i, 0)), lay, lay]
    args = [w, g, m, v]
    aliases = {}
    if bufs is not None:
        in_specs += [pl.BlockSpec(memory_space=pl.ANY)] * 4
        args += list(bufs)
        aliases = {4 + k: k for k in range(4)}
    return pl.pallas_call(
        body, name=name, grid=(r // tr,), in_specs=in_specs, out_specs=[lay] * 4,
        out_shape=[jax.ShapeDtypeStruct((depth, r, c), F32)] * 4, input_output_aliases=aliases,
        compiler_params=_params(("parallel",)),
    )(*args)


def _mesh_pos():
    return lax.axis_index("x"), lax.axis_index("y"), lax.axis_index("c")


def _flip(v, bit):
    return 1 - v if bit else v


def _sum_parts(parts, name):
    _, r, c = parts.shape
    tr = _div_tile(r, 256, 16)

    def body(p_ref, o_ref):
        acc = p_ref[0].astype(F32)
        for src in range(1, N_DEVICES):
            acc = acc + p_ref[src].astype(F32)
        o_ref[...] = acc

    return pl.pallas_call(
        body, name=name, grid=(r // tr,), in_specs=[pl.BlockSpec((N_DEVICES, tr, c), lambda i: (0, i, 0))],
        out_specs=pl.BlockSpec((tr, c), lambda i: (i, 0)), out_shape=jax.ShapeDtypeStruct((r, c), F32),
        compiler_params=_params(("parallel",)),
    )(parts)


def _split_start(srcs, lands, plan, ncopies, name):
    nbuf = len(srcs) + len(lands)

    def body(*refs):
        bufs = refs[:nbuf]
        send_sem, recv_sem, token = refs[nbuf], refs[nbuf + 1], refs[-1]
        for k, (src, dst, dev) in enumerate(plan(bufs[:len(srcs)], bufs[len(srcs):])):
            pltpu.make_async_remote_copy(src_ref=src, dst_ref=dst, send_sem=send_sem.at[k], recv_sem=recv_sem.at[k],
                                         device_id=dev, device_id_type=MESH).start()
        token[...] = jnp.zeros_like(token)

    hbm = pl.BlockSpec(memory_space=pltpu.HBM)
    sem = pl.BlockSpec(memory_space=pltpu.SEMAPHORE)
    operands = [pltpu.with_memory_space_constraint(a, pltpu.HBM) for a in (*srcs, *lands)]
    outs = pl.pallas_call(
        body, name=name, in_specs=[hbm] * nbuf,
        out_specs=(sem, sem, *[hbm] * nbuf, pl.BlockSpec(memory_space=pltpu.VMEM)),
        out_shape=(pltpu.SemaphoreType.DMA((ncopies,)), pltpu.SemaphoreType.DMA((ncopies,)),
                   *[pltpu.HBM(a.shape, a.dtype) for a in operands], jax.ShapeDtypeStruct((8, LANES), F32)),
        input_output_aliases={i: 2 + i for i in range(nbuf)},
        compiler_params=pltpu.CompilerParams(has_side_effects=pltpu.SideEffectType.DATAFLOW_SIDE_EFFECTING),
    )(*operands)
    handle = dict(send=outs[0], recv=outs[1], bufs=list(outs[2:2 + nbuf]), nsrc=len(srcs), plan=plan)
    return handle, outs[-1]


def _split_wait(handle, after, name):
    nbuf, nsrc, plan = len(handle["bufs"]), handle["nsrc"], handle["plan"]

    def body(*refs):
        bufs = refs[:nbuf]
        send_sem, recv_sem = refs[nbuf], refs[nbuf + 1]
        for k, (src, dst, dev) in enumerate(plan(bufs[:nsrc], bufs[nsrc:])):
            copy = pltpu.make_async_remote_copy(src_ref=src, dst_ref=dst, send_sem=send_sem.at[k],
                                                recv_sem=recv_sem.at[k], device_id=dev, device_id_type=MESH)
            copy.wait_send()
            copy.wait_recv()

    hbm = pl.BlockSpec(memory_space=pltpu.HBM)
    sem = pl.BlockSpec(memory_space=pltpu.SEMAPHORE)
    outs = pl.pallas_call(
        body, name=name, in_specs=[hbm] * nbuf + [sem, sem, pl.BlockSpec(memory_space=pl.ANY)],
        out_specs=[hbm] * nbuf, out_shape=[pltpu.HBM(a.shape, a.dtype) for a in handle["bufs"]],
        input_output_aliases={i: i for i in range(nbuf)},
        compiler_params=pltpu.CompilerParams(has_side_effects=pltpu.SideEffectType.DATAFLOW_SIDE_EFFECTING),
    )(*handle["bufs"], handle["send"], handle["recv"], after)
    return list(outs[nsrc:])


def _own_slot(shape, dtype, block, index):
    return lax.dynamic_update_slice(lax.empty(shape, dtype), block[None], (index,) + (0,) * block.ndim)


def _gather_plan(srcs, lands):
    x, y, c = _mesh_pos()
    return [(land.at[2 * x + y], land.at[2 * x + y], (*chip, c))
            for land in lands for chip in ((1 - x, y), (x, 1 - y), (1 - x, 1 - y))]


def _scatter_plan(srcs, lands):
    x, y, c = _mesh_pos()
    out = []
    for src, land in zip(srcs, lands):
        half = src.shape[1] // 2
        for d in range(1, N_DEVICES):
            p = (_flip(x, d & 4), _flip(y, d & 2), _flip(c, d & 1))
            out.append((src.at[2 * p[0] + p[1], pl.ds(p[2] * half, half), :], land.at[4 * x + 2 * y + c], p))
    return out


def _swap_plan(srcs, lands):
    x, y, c = _mesh_pos()
    return [(src, land.at[c], (x, y, 1 - c)) for src, land in zip(srcs, lands)]


class _Gathered:
    def __init__(self, groups):
        self.groups = groups
        self.ready = {}

    def get(self, name, after=None):
        if name not in self.ready:
            handle, names, wait_name = next(g for g in self.groups if name in g[1])
            for n, full in zip(names, _split_wait(handle, after, wait_name)):
                self.ready[n] = full.reshape(-1, full.shape[-1])
        return self.ready[name]


def _allgather_plan(srcs, lands):
    x, y, c = _mesh_pos()
    mine = lands[0].at[4 * x + 2 * y + c]
    return [(mine, mine, (_flip(x, d & 4), _flip(y, d & 2), _flip(c, d & 1))) for d in range(1, N_DEVICES)]


def _sum_slots(slots, name):
    r = slots.shape[1]

    def body(s_ref, o_ref):
        acc = s_ref[0]
        for src in range(1, N_DEVICES):
            acc = acc + s_ref[src]
        o_ref[...] = acc

    vm = pl.BlockSpec(memory_space=pltpu.VMEM)
    return pl.pallas_call(
        body, name=name, in_specs=[vm], out_specs=vm, out_shape=jax.ShapeDtypeStruct((r, LANES), F32),
        compiler_params=pltpu.CompilerParams(vmem_limit_bytes=VMEM_LIMIT_BYTES),
    )(slots)


def _bucket_ids(dil):
    rel = (np.arange(BLOCK)[:, None] + BLOCK - np.arange(2 * BLOCK)[None, :]) * dil
    max_exact = N_BUCKETS // 2
    d = np.maximum(rel, 0)
    large = max_exact + (np.log(np.maximum(d, 1).astype(np.float32) / max_exact)
                         / np.float32(np.log(T5_MAX_DIST / max_exact)) * (N_BUCKETS - max_exact)).astype(np.int32)
    large = np.minimum(large, N_BUCKETS - 1)
    return np.where(d < max_exact, d, large).astype(np.int32)


def _block_bias(table, dil):
    onehot = (jnp.asarray(_bucket_ids(dil))[:, :, None] == jnp.arange(N_BUCKETS)[None, None, :]).astype(F32)
    return jnp.einsum("ijb,bh->hij", onehot, table.astype(F32), precision=lax.Precision.HIGHEST)


def _tile_gain(g, n):
    return jnp.tile(g.reshape(1, HEAD_DIM), (1, n))


def _layer_fwd(x, p, cfg):
    w = p["weights"]
    h1 = _rmsnorm_fwd(x, p["attn_norm"], "attn_norm_fwd")
    proj = _matmul(h1, w.get("w_in_t", h1), "nt", F32, "in_proj", tm=1024, tn=896, tk=2048)
    aq, ak, av, bq, bk, bv, cq, ck, cv = _qk_prep(proj, p["gains"], cfg, "qk_prep")
    akt = _dilated_t(ak, 1)
    oa, lse_a = _banded_fwd(aq, akt, av, p["bias_a"], p["sinks"], cfg.nha, cfg.nkva, WINDOW_A - 1, 1, "swa_fwd")
    bkt = _keys_on_lanes(bk, SB_CHUNK)
    ob, tot_b = _sb_fwd(bq, bkt, bv, "stickbreak_fwd")
    ocs, lses, ckts = [], [], []
    for (window, dil), bias in zip(DILATED_PAIRS, p["bias_c"]):
        ckts.append(_dilated_t(ck, dil))
        o, l = _banded_fwd(cq, ckts[-1], cv, bias, None, cfg.nhc, cfg.nhc, window // dil, dil, f"dilated{dil}_fwd")
        ocs.append(o)
        lses.append(l)
    mix, oc, lse_c = _mix_fwd(oa, ob, ocs, lses, p["mix_gain"], cfg, "mix_fwd")
    xm = _matmul(mix, w.get("w_out", mix), "nn", F32, "out_proj", tm=1024, tn=1024, tk=2048, residual=x)
    h2 = _rmsnorm_fwd(xm, p["ffn_norm"], "ffn_norm_fwd")
    u = _matmul(h2, w.get("w_up_t", h2), "nt", F32, "up_proj", tm=1024, tn=1024, tk=2048)
    act = _conv_act_fwd(u, p["conv_w"], p["conv_b"], cfg.f, "conv_act_fwd")
    y = _matmul(act, w.get("w_down", act), "nn", F32, "down_proj", tm=1024, tn=1024, tk=1408, residual=xm)
    saved = dict(x=x, h1=h1, proj=proj, q=(aq, ak, av, bq, bk, bv, cq, ck, cv), oa=oa, lse_a=lse_a, ob=ob,
                 tot_b=tot_b, akt=akt, bkt=bkt, ckts=ckts, oc=oc, lse_c=lse_c, mix=mix, xm=xm, h2=h2, u=u, act=act)
    return y, saved


def _layer_bwd(dy, dyb, sv, p, dbias, cfg, on_grad):
    aq, ak, av, bq, bk, bv, cq, ck, cv = sv["q"]
    w = p["weights"]
    anchor = on_grad(_matmul(sv["act"], dyb, "tn", BF16, "down_proj_dw", tm=1408, tn=2048, tk=1024))
    dact = _matmul(dyb, w.get("w_down"), "nt", F32, "down_proj_dx", tm=1024, tn=1408, tk=2048)
    dug, duu, dwg, dwu, dbg, dbu = _conv_act_bwd(sv["u"], dact, p["conv_w"], p["conv_b"] + anchor, cfg.f,
                                                 "conv_act_bwd")
    du = jnp.concatenate([dug, duu], axis=1)
    anchor = on_grad(_matmul(du, sv["h2"], "tn", BF16, "up_proj_dw", tm=1408, tn=2048, tk=1024))
    dh2 = _matmul(du, w.get("w_up_t"), "nn", F32, "up_proj_dx", tm=1024, tn=2048, tk=1024)
    dxm, dxmb, g_ffn_norm = _rmsnorm_bwd(sv["xm"], p["ffn_norm"] + anchor, dh2, dy, "ffn_norm_bwd")
    anchor = on_grad(_matmul(sv["mix"], dxmb, "tn", BF16, "out_proj_dw", tm=1024, tn=2048, tk=1024))
    dmix = _matmul(dxmb, w.get("w_out"), "nt", F32, "out_proj_dx", tm=1024, tn=1024, tk=2048)
    doa, dob, doc, g_mix_gain, dsum_a, dsum_c = _mix_bwd(dmix, sv["oa"], sv["ob"], sv["oc"], p["mix_gain"] + anchor,
                                                         cfg, "mix_bwd")
    daq, dak, dav, dbias_a, g_sinks = _banded_bwd(aq, ak, sv["akt"], av, sv["lse_a"], dsum_a, doa, p["bias_a"],
                                                 p["sinks"], dbias[0], cfg.nha, cfg.nkva, WINDOW_A - 1, 1, "swa_bwd")
    dbq, dbk, dbv = _sb_bwd(bq, bk, sv["bkt"], bv, sv["tot_b"], dob, "stickbreak_bwd")
    dcq, dck, dcv, dbias_c = [], [], [], []
    for idx, ((window, dil), bias) in enumerate(zip(DILATED_PAIRS, p["bias_c"])):
        a, b, c, d, _ = _banded_bwd(cq, ck, sv["ckts"][idx], cv, sv["lse_c"], dsum_c, doc, bias, None, dbias[1][idx],
                                    cfg.nhc, cfg.nhc, window // dil, dil, f"dilated{dil}_bwd")
        dcq.append(a)
        dck.append(b)
        dcv.append(c)
        dbias_c.append(d)
    dproj, g_aq, g_ak, g_cq, g_ck = _qk_prep_bwd(
        sv["proj"], p["gains"], [[daq], [dak], [dav], [dbq], [dbk], [dbv], dcq, dck, dcv], cfg, "qk_prep_bwd")
    anchor = on_grad(_matmul(dproj, sv["h1"], "tn", BF16, "in_proj_dw", tm=768, tn=2048, tk=1024))
    dh1 = _matmul(dproj, w.get("w_in_t"), "nn", F32, "in_proj_dx", tm=1024, tn=2048, tk=768)
    dx, dxb, g_attn_norm = _rmsnorm_bwd(sv["x"], p["attn_norm"] + anchor, dh1, dxm, "attn_norm_bwd")

    def fold(g):
        return jnp.sum(g.reshape(-1, HEAD_DIM), axis=0)

    small = dict(attn_norm=g_attn_norm[0], a_q_gain=fold(g_aq), a_k_gain=fold(g_ak), a_sinks=g_sinks,
                 c_q_gain=fold(g_cq), c_k_gain=fold(g_ck), mix_out_gain=g_mix_gain[0], ffn_norm=g_ffn_norm[0],
                 conv_w=jnp.concatenate([dwg, dwu], axis=1), conv_b=jnp.concatenate([dbg, dbu], axis=1)[0])
    return dx, dxb, small, (dbias_a, dbias_c)


_SMALL = ("attn_norm", "a_q_gain", "a_k_gain", "a_sinks", "c_q_gain", "c_k_gain", "rel_bias_table", "mix_out_gain",
          "ffn_norm", "conv_w", "conv_b")


def _pack(arrays):
    flat = jnp.concatenate([a.reshape(-1).astype(F32) for a in arrays])
    pad = (-flat.shape[0]) % (8 * LANES)
    return jnp.pad(flat, (0, pad)).reshape(-1, LANES)


def _unpack(flat, shapes):
    flat = flat.reshape(-1)
    out, pos = [], 0
    for sh in shapes:
        n = int(np.prod(sh))
        out.append(flat[pos:pos + n].reshape(sh))
        pos += n
    return out


def kernel(x, attn_norm, w_in, a_q_gain, a_k_gain, a_sinks, c_q_gain, c_k_gain, rel_bias_table, mix_out_gain, w_out, ffn_norm, w_up, conv_w, conv_b, w_down, loss_target, m_attn_norm, m_w_in, m_a_q_gain, m_a_k_gain, m_a_sinks, m_c_q_gain, m_c_k_gain, m_rel_bias_table, m_mix_out_gain, m_w_out, m_ffn_norm, m_w_up, m_conv_w, m_conv_b, m_w_down, v_attn_norm, v_w_in, v_a_q_gain, v_a_k_gain, v_a_sinks, v_c_q_gain, v_c_k_gain, v_rel_bias_table, v_mix_out_gain, v_w_out, v_ffn_norm, v_w_up, v_conv_w, v_conv_b, v_w_down):
    depth, d = attn_norm.shape
    f = w_down.shape[1] * N_CHIPS
    cfg = _Cfg(d, f)
    chip = 2 * lax.axis_index("x") + lax.axis_index("y")

    cw_cols = conv_w.shape[2]
    cw_flat = conv_w.reshape(-1)
    cw_rows = -(-cw_flat.shape[0] // (16 * LANES)) * 16
    cw_pad = jnp.pad(cw_flat, (0, cw_rows * LANES - cw_flat.shape[0])).reshape(cw_rows, LANES)

    table_a, table_c = rel_bias_table[:, :cfg.nha], rel_bias_table[:, cfg.nha:]
    bias_a = _block_bias(table_a, 1)
    bias_c = [_block_bias(table_c, dil) for _, dil in DILATED_PAIRS]

    layers, anchor = [], 0.0
    for l in range(depth):
        shards = [w_in[l].T.astype(BF16), w_out[l].astype(BF16), w_up[l].T.astype(BF16), w_down[l].astype(BF16)]
        names = ["w_in_t", "w_out", "w_up_t", "w_down"]
        if l == 0:
            todo = [([cw_pad, shards[0]], ["conv_w", names[0]])] + [([s], [n]) for s, n in zip(shards[1:], names[1:])]
        else:
            todo = [(shards, names)]
        groups = []
        for k, (srcs, group_names) in enumerate(todo):
            lands = [_own_slot((N_CHIPS,) + s.shape, s.dtype, s, chip) for s in srcs]
            handle, token = _split_start([], lands, _gather_plan, 3 * len(lands), f"gather_start_{l}_{k}")
            anchor = anchor + token[0, 0]
            groups.append((handle, group_names, f"gather_wait_{l}_{k}"))
        layers.append(dict(
            attn_norm=attn_norm[l].reshape(1, d), ffn_norm=ffn_norm[l].reshape(1, d),
            mix_gain=mix_out_gain[l].reshape(1, d),
            gains=(_tile_gain(a_q_gain[l], cfg.nha), _tile_gain(a_k_gain[l], cfg.nkva),
                   _tile_gain(c_q_gain[l], cfg.nhc), _tile_gain(c_k_gain[l], cfg.nhc)),
            sinks=a_sinks[l], bias_a=bias_a, bias_c=bias_c, conv_b=conv_b[l].reshape(1, 2 * f),
            weights=_Gathered(groups)))
    cw_all = layers[0]["weights"].get("conv_w", layers[0]["attn_norm"] + anchor)
    cw_all = cw_all.reshape(N_CHIPS, -1)[:, :cw_flat.shape[0]].reshape(N_CHIPS, depth, CONV_WIDTH, cw_cols)
    conv_w_full = jnp.transpose(cw_all, (1, 2, 0, 3)).reshape(depth, CONV_WIDTH, N_CHIPS * cw_cols)
    for l in range(depth):
        layers[l]["conv_w"] = conv_w_full[l]

    act = x[0]
    saved = []
    for l in range(depth):
        act, sv = _layer_fwd(act, layers[l], cfg)
        saved.append(sv)
    dact, dactb, loss_blk = _loss_head(act, loss_target[0], "loss_head")
    loss = lax.psum(loss_blk[0, 0], ("x", "y", "c"))

    core = lax.axis_index("c")

    def start_scatter(grads, name):
        srcs = [g.reshape(N_CHIPS, -1, g.shape[-1]) for g in grads]
        lands = []
        for g in srcs:
            half = g.shape[1] // 2
            own = lax.dynamic_slice(g, (chip, core * half, 0), (1, half, g.shape[2]))[0]
            lands.append(_own_slot((N_DEVICES, half, g.shape[2]), g.dtype, own, 2 * chip + core))
        return _split_start(srcs, lands, _scatter_plan, (N_DEVICES - 1) * len(srcs), name)

    def finish_scatter(l, handles, after):
        parts = [pt for k, h in enumerate(handles) for pt in _split_wait(h, after, f"scatter_wait_{l}_{k}")][::-1]
        halves = [_sum_parts(pt, f"sum_grads_{t}") for t, pt in enumerate(parts)]
        lands = [_own_slot((2,) + h.shape, h.dtype, h, core) for h in halves]
        return _split_start(halves, lands, _swap_plan, len(halves), f"swap_start_{l}")[0]

    dbias = (jnp.zeros_like(bias_a), [jnp.zeros_like(b) for b in bias_c])
    small_grads = [None] * depth
    swaps = [None] * depth
    pending = None
    for l in reversed(range(depth)):
        made = []

        def on_grad(g, l=l, made=made):
            if l:
                made.append(g)
                return 0.0
            handle, token = start_scatter([g], f"scatter_start_0_{len(made)}")
            made.append(handle)
            return token[0, 0]

        dact, dactb, small_grads[l], dbias = _layer_bwd(dact, dactb, saved[l], layers[l], dbias, cfg, on_grad)
        if pending is not None:
            swaps[l + 1] = finish_scatter(l + 1, pending, dact)
        if l:
            handle, token = start_scatter(made, f"scatter_start_{l}")
            pending = [handle]
            layers[l - 1]["conv_b"] = layers[l - 1]["conv_b"] + token[0, 0]
        else:
            pending = made
    grad_x = dact[None]

    tabs = _bias_table_grad([dbias[0]] + dbias[1], [jnp.asarray(_bucket_ids(1))]
                            + [jnp.asarray(_bucket_ids(dil)) for _, dil in DILATED_PAIRS], "bias_table_grad")
    g_table_a = tabs[0][:, :N_BUCKETS].T
    g_table_c = (tabs[1] + tabs[2] + tabs[3])[:, :N_BUCKETS].T
    g_table = jnp.concatenate([g_table_a, g_table_c], axis=1)
    small_local = {k: jnp.stack([small_grads[l][k] for l in range(depth)]) for k in _SMALL if k != "rel_bias_table"}
    small_local["rel_bias_table"] = g_table
    shapes = [small_local[k].shape for k in _SMALL]
    packed = _pack([small_local[k] for k in _SMALL])
    small_handle, small_token = _split_start([], [_own_slot((N_DEVICES,) + packed.shape, F32, packed, 2 * chip + core)],
                                             _allgather_plan, N_DEVICES - 1, "small_start")

    given = dict(attn_norm=attn_norm, a_q_gain=a_q_gain, a_k_gain=a_k_gain, a_sinks=a_sinks, c_q_gain=c_q_gain,
                 c_k_gain=c_k_gain, rel_bias_table=rel_bias_table, mix_out_gain=mix_out_gain, ffn_norm=ffn_norm,
                 conv_w=conv_w, conv_b=conv_b)
    moms = dict(attn_norm=(m_attn_norm, v_attn_norm), a_q_gain=(m_a_q_gain, v_a_q_gain),
                a_k_gain=(m_a_k_gain, v_a_k_gain), a_sinks=(m_a_sinks, v_a_sinks), c_q_gain=(m_c_q_gain, v_c_q_gain),
                c_k_gain=(m_c_k_gain, v_c_k_gain), rel_bias_table=(m_rel_bias_table, v_rel_bias_table),
                mix_out_gain=(m_mix_out_gain, v_mix_out_gain), ffn_norm=(m_ffn_norm, v_ffn_norm),
                conv_w=(m_conv_w, v_conv_w), conv_b=(m_conv_b, v_conv_b))
    sshapes = [given[k].shape for k in _SMALL]
    grads, deltas, new_m, new_v = {}, {}, {}, {}

    def finish_small(after):
        (slots,) = _split_wait(small_handle, after, "small_wait")
        reduced = dict(zip(_SMALL, _unpack(_sum_slots(slots, "sum_small"), shapes)))
        reduced["conv_w"] = lax.dynamic_slice_in_dim(reduced["conv_w"], chip * cw_cols, cw_cols, axis=2)
        s_delta, s_m, s_v = _adamw(_pack([given[k] for k in _SMALL]), _pack([reduced[k] for k in _SMALL]),
                                   _pack([moms[k][0] for k in _SMALL]), _pack([moms[k][1] for k in _SMALL]),
                                   "adamw_small")
        grads.update(reduced)
        deltas.update(zip(_SMALL, _unpack(s_delta, sshapes)))
        new_m.update(zip(_SMALL, _unpack(s_m, sshapes)))
        new_v.update(zip(_SMALL, _unpack(s_v, sshapes)))
        return s_delta

    big_given = dict(w_in=(w_in, m_w_in, v_w_in, True), w_out=(w_out, m_w_out, v_w_out, False),
                     w_up=(w_up, m_w_up, v_w_up, True), w_down=(w_down, m_w_down, v_w_down, False))
    names = ("w_in", "w_out", "w_up", "w_down")
    bufs = {name: None for name in names}
    after = small_token
    for l in reversed(range(depth)):
        if l == 0:
            after = finish_small(after)
            swaps[0] = finish_scatter(0, pending, after)
        layer_grads = [g.reshape(-1, g.shape[-1]) for g in _split_wait(swaps[l], after, f"swap_wait_{l}")]
        for t, name in enumerate(names):
            wt, mt, vt, transposed = big_given[name]
            g = layer_grads[t].T if transposed else layer_grads[t]
            bufs[name] = _adamw_layer(l, wt, g, mt, vt, bufs[name], f"adamw_{name}_{l}")
            after = bufs[name][1]
    for name in names:
        grads[name], deltas[name], new_m[name], new_v[name] = bufs[name]

    order = ("attn_norm", "w_in", "a_q_gain", "a_k_gain", "a_sinks", "c_q_gain", "c_k_gain", "rel_bias_table",
             "mix_out_gain", "w_out", "ffn_norm", "w_up", "conv_w", "conv_b", "w_down")
    return (loss, grad_x, *[grads[k] for k in order], *[deltas[k] for k in order], *[new_m[k] for k in order],
            *[new_v[k] for k in order])
```
